```python
import math
import jax, jax.numpy as jnp
from jax import lax
import numpy as np

D_MODEL = 1024
BATCH = 8
SEQ = 4096
DEPTH = 2

MEM_LEN = 256
D_FF = 2816
LN_EPS = 1e-5
RMS_EPS = 1e-6
ROPE_THETA = 10000.0
DEEPNORM_ALPHA = (2 * DEPTH) ** 0.25
DEEPNORM_BETA = (8 * DEPTH) ** -0.25
Q_BLOCK = 128
NEG_INF = -1e30
N_BRANCHES = 4

GMLP_CHUNK = 128
GMLP_GROUPS = 4
GMLP_WIDTH = 512
GMLP_GROUP_DIM = GMLP_WIDTH // GMLP_GROUPS

CONV_WIDTH = 512
CONV_K = 3

MLA_HEADS = 8
MLA_Q_RANK = 256
MLA_KV_RANK = 128
MLA_NOPE = 64
MLA_ROPE = 32
MLA_V = 64

NSA_HEADS = 8
NSA_KV_GROUPS = 2
NSA_HPG = NSA_HEADS // NSA_KV_GROUPS
NSA_DIM = 64
NSA_KV_WIDTH = NSA_KV_GROUPS * NSA_DIM
CMP_BLOCK = 32
CMP_STRIDE = 16
SLC_BLOCK = 64
SLC_TOPK = 8
WINDOW = 512

XATTN_HEADS = 4
XATTN_DIM = 128

IN_SPLITS = (GMLP_WIDTH, GMLP_WIDTH,
             CONV_WIDTH, CONV_WIDTH, CONV_WIDTH,
             MLA_Q_RANK, MLA_KV_RANK, MLA_ROPE,
             NSA_HEADS * NSA_DIM) + (NSA_KV_WIDTH,) * 6 + (NSA_HEADS * 3,) + (D_MODEL,) * N_BRANCHES
IN_WIDTH = sum(IN_SPLITS)

kernel_name = 'hybrid_gated_parallel_mixer_deepnorm'


def _layer_norm(x, g, b):
    xf = x.astype(jnp.float32)
    mu = jnp.mean(xf, -1, keepdims=True)
    var = jnp.mean(jnp.square(xf - mu), -1, keepdims=True)
    return ((xf - mu) * lax.rsqrt(var + LN_EPS) * g + b).astype(x.dtype)


def _rms_norm(x, g):
    xf = x.astype(jnp.float32)
    return (xf * lax.rsqrt(jnp.mean(xf * xf, -1, keepdims=True) + RMS_EPS) * g).astype(x.dtype)


def _rope_tables(pos, dim):
    inv = ROPE_THETA ** (-(jnp.arange(0, dim, 2, dtype=jnp.float32) / dim))
    ang = pos[:, None] * inv[None, :]
    return jnp.cos(ang), jnp.sin(ang)


def _apply_rope(x, cos, sin):
    x1, x2 = jnp.split(x.astype(jnp.float32), 2, axis=-1)
    return jnp.concatenate([x1 * cos - x2 * sin, x1 * sin + x2 * cos], -1).astype(x.dtype)


def _swiglu(x, w1, w3, w2):
    return (jax.nn.silu(x @ w1) * (x @ w3)) @ w2


def _blocked_causal_attention(q, k, v, scale):
    Bsz, S, H, dk = q.shape
    nq = S // Q_BLOCK
    qb = q.reshape(Bsz, nq, Q_BLOCK, H, dk).transpose(1, 0, 2, 3, 4)
    kpos = jnp.arange(S)

    def one(args):
        qi, blk = args
        qpos = blk * Q_BLOCK + jnp.arange(Q_BLOCK)
        s = jnp.einsum('bthd,bshd->bhts', qi, k).astype(jnp.float32) * scale
        s = jnp.where(kpos[None, :] <= qpos[:, None], s, NEG_INF)
        p = jax.nn.softmax(s, axis=-1).astype(v.dtype)
        return jnp.einsum('bhts,bshd->bthd', p, v)

    o = lax.map(one, (qb, jnp.arange(nq)))
    return o.transpose(1, 0, 2, 3, 4).reshape(Bsz, S, H, v.shape[-1])


def _gmlp_branch(u, v, ln_g, ln_b, w_s, b_s, w_out):
    Bsz, S, _ = u.shape
    n_chunk = S // GMLP_CHUNK
    v = _layer_norm(v, ln_g, ln_b).reshape(Bsz, n_chunk, GMLP_CHUNK, GMLP_GROUPS, GMLP_GROUP_DIM)
    causal = jnp.tril(jnp.ones((GMLP_CHUNK, GMLP_CHUNK), dtype=bool))
    w = jnp.where(causal, w_s, 0)
    s = jnp.einsum('gts,bcsgd->bctgd', w, v) + b_s.T[:, :, None]
    return (u * s.reshape(Bsz, S, GMLP_WIDTH)) @ w_out


def _short_conv_branch(b_gate, c_gate, h, conv_w, w_out):
    S = h.shape[1]
    zp = jnp.pad(c_gate * h, ((0, 0), (CONV_K - 1, 0), (0, 0)))
    y = zp[:, 0:S] * conv_w[0]
    for k in range(1, CONV_K):
        y = y + zp[:, k:k + S] * conv_w[k]
    return (b_gate * y) @ w_out


def _mla_branch(q_lat, kv_lat, k_rope, qn_g, kvn_g, w_uq, w_ukv, w_out, pos):
    Bsz, S, _ = q_lat.shape
    cos, sin = _rope_tables(pos, MLA_ROPE)
    q = (_rms_norm(q_lat, qn_g) @ w_uq).reshape(Bsz, S, MLA_HEADS, MLA_NOPE + MLA_ROPE)
    q = jnp.concatenate([q[..., :MLA_NOPE], _apply_rope(q[..., MLA_NOPE:], cos[:, None], sin[:, None])], -1)
    kv = (_rms_norm(kv_lat, kvn_g) @ w_ukv).reshape(Bsz, S, MLA_HEADS, MLA_NOPE + MLA_V)
    k_pe = _apply_rope(k_rope, cos, sin)[:, :, None, :]
    k = jnp.concatenate([kv[..., :MLA_NOPE], jnp.broadcast_to(k_pe, (Bsz, S, MLA_HEADS, MLA_ROPE))], -1)
    v = kv[..., MLA_NOPE:]
    o = _blocked_causal_attention(q, k, v, (MLA_NOPE + MLA_ROPE) ** -0.5)
    return o.reshape(Bsz, S, MLA_HEADS * MLA_V) @ w_out


def _nsa_branch(q, k_c, v_c, k_s, v_s, k_w, v_w, gate, pe_k, pe_v, wcmp_k, wcmp_v, w_out, pos):
    Bsz, S, _ = q.shape
    G, Hg, d = NSA_KV_GROUPS, NSA_HPG, NSA_DIM
    cos, sin = _rope_tables(pos, d)
    q = _apply_rope(q.reshape(Bsz, S, NSA_HEADS, d), cos[:, None], sin[:, None])
    k_c, v_c, k_s, v_s, k_w, v_w = [t.reshape(Bsz, S, G, d) for t in (k_c, v_c, k_s, v_s, k_w, v_w)]
    k_s = _apply_rope(k_s, cos[:, None], sin[:, None])
    k_w = _apply_rope(k_w, cos[:, None], sin[:, None])
    gate = jax.nn.sigmoid(gate).reshape(Bsz, S, NSA_HEADS, 3)

    n_cmp = (S - CMP_BLOCK) // CMP_STRIDE + 1
    cmp_start = jnp.arange(n_cmp) * CMP_STRIDE
    cmp_end = cmp_start + CMP_BLOCK - 1
    cmp_idx = cmp_start[:, None] + jnp.arange(CMP_BLOCK)[None, :]
    k_cmp = jnp.einsum('bnlgd,lde->bnge', k_c[:, cmp_idx] + pe_k[:, None, :], wcmp_k)
    v_cmp = jnp.einsum('bnlgd,lde->bnge', v_c[:, cmp_idx] + pe_v[:, None, :], wcmp_v)
    ccos, csin = _rope_tables(cmp_end.astype(jnp.float32), d)
    k_cmp = _apply_rope(k_cmp, ccos[:, None], csin[:, None])

    n_slc = S // SLC_BLOCK
    slc_start = jnp.arange(n_slc) * SLC_BLOCK
    ov = (jnp.minimum(cmp_start[:, None] + CMP_BLOCK, slc_start[None, :] + SLC_BLOCK)
          - jnp.maximum(cmp_start[:, None], slc_start[None, :]))
    overlap = jnp.clip(ov, 0).astype(jnp.float32) / CMP_BLOCK
    top_k = min(SLC_TOPK, n_slc)

    ks_blk = k_s.transpose(0, 2, 1, 3).reshape(Bsz, G, n_slc, SLC_BLOCK, d)
    vs_blk = v_s.transpose(0, 2, 1, 3).reshape(Bsz, G, n_slc, SLC_BLOCK, d)
    kw_pad = jnp.pad(k_w, ((0, 0), (WINDOW, 0), (0, 0), (0, 0)))
    vw_pad = jnp.pad(v_w, ((0, 0), (WINDOW, 0), (0, 0), (0, 0)))
    b_idx = jnp.arange(Bsz)[:, None, None, None]
    g_idx = jnp.arange(G)[None, :, None, None]
    scale = d ** -0.5
    jr = jnp.arange(n_slc)

    nq = S // Q_BLOCK
    qb = q.reshape(Bsz, nq, Q_BLOCK, G, Hg, d).transpose(1, 0, 2, 3, 4, 5)
    gb = gate.reshape(Bsz, nq, Q_BLOCK, G, Hg, 3).transpose(1, 0, 2, 3, 4, 5)

    def one(args):
        qi, g_blk, blk = args
        qpos = blk * Q_BLOCK + jnp.arange(Q_BLOCK)
        s = jnp.einsum('btghd,bngd->bghtn', qi, k_cmp).astype(jnp.float32) * scale
        valid = cmp_end[None, :] <= qpos[:, None]
        p_cmp = jnp.where(valid, jax.nn.softmax(jnp.where(valid, s, NEG_INF), axis=-1), 0.0)
        o_cmp = jnp.einsum('bghtn,bngd->btghd', p_cmp.astype(v_cmp.dtype), v_cmp)
        imp = jnp.einsum('bghtn,nj->bgtj', p_cmp, overlap)
        j_q = qpos // SLC_BLOCK
        forced = (jr[None, :] == 0) | (jr[None, :] == j_q[:, None]) | (jr[None, :] == j_q[:, None] - 1)
        imp = jnp.where(forced, 1e9, imp)
        imp = jnp.where(jr[None, :] <= j_q[:, None], imp, -1.0)
        top_s, top_i = lax.top_k(imp, top_k)
        k_sel = ks_blk[b_idx, g_idx, top_i].reshape(Bsz, G, Q_BLOCK, top_k * SLC_BLOCK, d)
        v_sel = vs_blk[b_idx, g_idx, top_i].reshape(Bsz, G, Q_BLOCK, top_k * SLC_BLOCK, d)
        tok = (top_i[..., None] * SLC_BLOCK + jnp.arange(SLC_BLOCK)).reshape(Bsz, G, Q_BLOCK, top_k * SLC_BLOCK)
        ok = (tok <= qpos[:, None]) & jnp.repeat(top_s >= 0, SLC_BLOCK, axis=-1)
        s = jnp.einsum('btghd,bgtnd->bghtn', qi, k_sel).astype(jnp.float32) * scale
        s = jnp.where(ok[:, :, None], s, NEG_INF)
        o_slc = jnp.einsum('bghtn,bgtnd->btghd', jax.nn.softmax(s, axis=-1).astype(v_sel.dtype), v_sel)
        k_win = lax.dynamic_slice_in_dim(kw_pad, blk * Q_BLOCK, WINDOW + Q_BLOCK, axis=1)
        v_win = lax.dynamic_slice_in_dim(vw_pad, blk * Q_BLOCK, WINDOW + Q_BLOCK, axis=1)
        kpos = blk * Q_BLOCK - WINDOW + jnp.arange(WINDOW + Q_BLOCK)
        dist = qpos[:, None] - kpos[None, :]
        okw = (dist >= 0) & (dist < WINDOW) & (kpos[None, :] >= 0)
        s = jnp.einsum('btghd,bsgd->bghts', qi, k_win).astype(jnp.float32) * scale
        s = jnp.where(okw, s, NEG_INF)
        o_win = jnp.einsum('bghts,bsgd->btghd', jax.nn.softmax(s, axis=-1).astype(v_win.dtype), v_win)
        return g_blk[..., 0:1] * o_cmp + g_blk[..., 1:2] * o_slc + g_blk[..., 2:3] * o_win

    o = lax.map(one, (qb, gb, jnp.arange(nq)))
    o = o.transpose(1, 0, 2, 3, 4, 5).reshape(Bsz, S, NSA_HEADS * d)
    return o @ w_out


def _token_mixing(h, w_in, b_in, gmlp_ln_g, gmlp_ln_b, gmlp_ws, gmlp_bs, gmlp_wout,
                  conv_w, conv_wout, mla_qnorm_g, mla_kvnorm_g, mla_wuq, mla_wukv, mla_wout,
                  nsa_pe_k, nsa_pe_v, nsa_wcmp_k, nsa_wcmp_v, nsa_wout, w_o):
    S = h.shape[1]
    z = h @ w_in + b_in
    offs = np.cumsum(IN_SPLITS)[:-1].tolist()
    (u, v, cb, cc, ch, q_lat, kv_lat, k_rope, nq, nkc, nvc, nks, nvs, nkw, nvw, ngate,
     ga, gb, gc, gd) = jnp.split(z, offs, axis=-1)
    pos = jnp.arange(S, dtype=jnp.float32)
    y_a = _gmlp_branch(u, v, gmlp_ln_g, gmlp_ln_b, gmlp_ws, gmlp_bs, gmlp_wout)
    y_b = _short_conv_branch(cb, cc, ch, conv_w, conv_wout)
    y_c = _mla_branch(q_lat, kv_lat, k_rope, mla_qnorm_g, mla_kvnorm_g, mla_wuq, mla_wukv, mla_wout, pos)
    y_d = _nsa_branch(nq, nkc, nvc, nks, nvs, nkw, nvw, ngate, nsa_pe_k, nsa_pe_v,
                      nsa_wcmp_k, nsa_wcmp_v, nsa_wout, pos)
    merged = (jax.nn.sigmoid(ga) * y_a + jax.nn.sigmoid(gb) * y_b
              + jax.nn.sigmoid(gc) * y_c + jax.nn.sigmoid(gd) * y_d)
    return merged @ w_o


def _cross_attention(x, mem, wq, wk, wv, wo):
    Bsz, S, _ = x.shape
    M = mem.shape[1]
    q = (x @ wq).reshape(Bsz, S, XATTN_HEADS, XATTN_DIM)
    k = (mem @ wk).reshape(Bsz, M, XATTN_HEADS, XATTN_DIM)
    v = (mem @ wv).reshape(Bsz, M, XATTN_HEADS, XATTN_DIM)
    s = jnp.einsum('bthd,bmhd->bhtm', q, k).astype(jnp.float32) * XATTN_DIM ** -0.5
    p = jax.nn.softmax(s, axis=-1).astype(v.dtype)
    o = jnp.einsum('bhtm,bmhd->bthd', p, v).reshape(Bsz, S, XATTN_HEADS * XATTN_DIM)
    return o @ wo


def setup_inputs(seed: int = 0) -> dict:
    key = jax.random.key(seed)
    keys = iter(jax.random.split(key, 64))

    def dense(shape, fan_in, scale=1.0):
        return jax.random.normal(next(keys), (DEPTH,) + shape, jnp.float32) * (scale * fan_in ** -0.5)

    def gain(shape):
        return 1.0 + 0.02 * jax.random.normal(next(keys), (DEPTH,) + shape, jnp.float32)

    def small(shape, s=0.01):
        return s * jax.random.normal(next(keys), (DEPTH,) + shape, jnp.float32)

    D, F = D_MODEL, D_FF
    return {
        'x': jax.random.normal(next(keys), (BATCH, SEQ, D), jnp.float32),
        'mem': jax.random.normal(next(keys), (BATCH, MEM_LEN, D), jnp.float32),
        'ffn1_w1': dense((D, F), D),
        'ffn1_w3': dense((D, F), D),
        'ffn1_w2': dense((F, D), F, DEEPNORM_BETA),
        'ln1_g': gain((D,)),
        'ln1_b': small((D,)),
        'w_in': dense((D, IN_WIDTH), D),
        'b_in': small((IN_WIDTH,)),
        'gmlp_ln_g': gain((GMLP_WIDTH,)),
        'gmlp_ln_b': small((GMLP_WIDTH,)),
        'gmlp_ws': dense((GMLP_GROUPS, GMLP_CHUNK, GMLP_CHUNK), GMLP_CHUNK),
        'gmlp_bs': gain((GMLP_GROUPS, GMLP_CHUNK)),
        'gmlp_wout': dense((GMLP_WIDTH, D), GMLP_WIDTH),
        'conv_w': dense((CONV_K, CONV_WIDTH), CONV_K),
        'conv_wout': dense((CONV_WIDTH, D), CONV_WIDTH),
        'mla_qnorm_g': gain((MLA_Q_RANK,)),
        'mla_kvnorm_g': gain((MLA_KV_RANK,)),
        'mla_wuq': dense((MLA_Q_RANK, MLA_HEADS * (MLA_NOPE + MLA_ROPE)), MLA_Q_RANK),
        'mla_wukv': dense((MLA_KV_RANK, MLA_HEADS * (MLA_NOPE + MLA_V)), MLA_KV_RANK),
        'mla_wout': dense((MLA_HEADS * MLA_V, D), MLA_HEADS * MLA_V),
        'nsa_pe_k': small((CMP_BLOCK, NSA_DIM), 0.02),
        'nsa_pe_v': small((CMP_BLOCK, NSA_DIM), 0.02),
        'nsa_wcmp_k': dense((CMP_BLOCK, NSA_DIM, NSA_DIM), CMP_BLOCK * NSA_DIM),
        'nsa_wcmp_v': dense((CMP_BLOCK, NSA_DIM, NSA_DIM), CMP_BLOCK * NSA_DIM),
        'nsa_wout': dense((NSA_HEADS * NSA_DIM, D), NSA_HEADS * NSA_DIM),
        'w_o': dense((D, D), D, DEEPNORM_BETA),
        'ln2_g': gain((D,)),
        'ln2_b': small((D,)),
        'xattn_wq': dense((D, XATTN_HEADS * XATTN_DIM), D),
        'xattn_wk': dense((D, XATTN_HEADS * XATTN_DIM), D),
        'xattn_wv': dense((D, XATTN_HEADS * XATTN_DIM), D),
        'xattn_wo': dense((XATTN_HEADS * XATTN_DIM, D), XATTN_HEADS * XATTN_DIM, DEEPNORM_BETA),
        'ln3_g': gain((D,)),
        'ln3_b': small((D,)),
        'ffn2_w1': dense((D, F), D),
        'ffn2_w3': dense((D, F), D),
        'ffn2_w2': dense((F, D), F, DEEPNORM_BETA),
        'ln4_g': gain((D,)),
        'ln4_b': small((D,)),
    }


def reference(x, mem, ffn1_w1, ffn1_w3, ffn1_w2, ln1_g, ln1_b, w_in, b_in,
              gmlp_ln_g, gmlp_ln_b, gmlp_ws, gmlp_bs, gmlp_wout, conv_w, conv_wout,
              mla_qnorm_g, mla_kvnorm_g, mla_wuq, mla_wukv, mla_wout,
              nsa_pe_k, nsa_pe_v, nsa_wcmp_k, nsa_wcmp_v, nsa_wout,
              w_o, ln2_g, ln2_b, xattn_wq, xattn_wk, xattn_wv, xattn_wo, ln3_g, ln3_b,
              ffn2_w1, ffn2_w3, ffn2_w2, ln4_g, ln4_b):
    a = DEEPNORM_ALPHA
    for l in range(DEPTH):
        x = _layer_norm(a * x + 0.5 * _swiglu(x, ffn1_w1[l], ffn1_w3[l], ffn1_w2[l]), ln1_g[l], ln1_b[l])
        mix = _token_mixing(x, w_in[l], b_in[l], gmlp_ln_g[l], gmlp_ln_b[l], gmlp_ws[l], gmlp_bs[l],
                            gmlp_wout[l], conv_w[l], conv_wout[l], mla_qnorm_g[l], mla_kvnorm_g[l],
                            mla_wuq[l], mla_wukv[l], mla_wout[l], nsa_pe_k[l], nsa_pe_v[l],
                            nsa_wcmp_k[l], nsa_wcmp_v[l], nsa_wout[l], w_o[l])
        x = _layer_norm(a * x + mix, ln2_g[l], ln2_b[l])
        x = _layer_norm(a * x + _cross_attention(x, mem, xattn_wq[l], xattn_wk[l], xattn_wv[l], xattn_wo[l]),
                        ln3_g[l], ln3_b[l])
        x = _layer_norm(a * x + 0.5 * _swiglu(x, ffn2_w1[l], ffn2_w3[l], ffn2_w2[l]), ln4_g[l], ln4_b[l])
    return x
```

```python
import functools

import jax
import jax.numpy as jnp
from jax import lax
from jax.experimental import pallas as pl
from jax.experimental.pallas import tpu as pltpu

BF = jnp.bfloat16
F32 = jnp.float32

D_MODEL = 1024
D_FF = 2816
LN_EPS = 1e-5
RMS_EPS = 1e-6
ROPE_THETA = 10000.0
DEPTH = 2
ALPHA = (2 * DEPTH) ** 0.25
NEG = -1e30

GMLP_CHUNK = 128
GMLP_GROUPS = 4
GMLP_WIDTH = 512
CONV_WIDTH = 512
CONV_K = 3
MLA_HEADS = 8
MLA_Q_RANK = 256
MLA_KV_RANK = 128
MLA_NOPE = 64
MLA_ROPE = 32
MLA_V = 64
NSA_HEADS = 8
NSA_GROUPS = 2
NSA_HPG = 4
NSA_DIM = 64
CMP_BLOCK = 32
CMP_STRIDE = 16
SLC_BLOCK = 64
SLC_TOPK = 8
WINDOW = 512
XATTN_HEADS = 4
XATTN_DIM = 128

LANE = 128
CONV_HALO = 8
VMEM_LIMIT = 56 * 1024 * 1024

_O_U, _O_V, _O_CB, _O_CC, _O_CH = 0, 512, 1024, 1536, 2048
_O_QLAT, _O_KVLAT, _O_KROPE = 2560, 2816, 2944
_O_NQ, _O_NKC, _O_NVC, _O_NKS, _O_NVS, _O_NKW, _O_NVW, _O_NGATE = 2976, 3488, 3616, 3744, 3872, 4000, 4128, 4256
_O_GA, _O_GB, _O_GC, _O_GD = 4280, 5304, 6328, 7352


def _dot(a, b):
    return jnp.dot(a, b, preferred_element_type=F32)


def _dot_t(a, b):
    return lax.dot_general(a, b, (((1,), (1,)), ((), ())), preferred_element_type=F32)


def _ln(y, g, b):
    mu = jnp.mean(y, -1, keepdims=True)
    d = y - mu
    var = jnp.mean(d * d, -1, keepdims=True)
    return d * lax.rsqrt(var + LN_EPS) * g + b


def _rms(x, g):
    return x * lax.rsqrt(jnp.mean(x * x, -1, keepdims=True) + RMS_EPS) * g


def _resident(shape):
    n = len(shape)
    return pl.BlockSpec(shape, lambda *_: (0,) * n)


def _params(sem):
    return pltpu.CompilerParams(dimension_semantics=sem, vmem_limit_bytes=VMEM_LIMIT)


def _ffn_ln_kernel(x_ref, w1_ref, w3_ref, w2_ref, g_ref, b_ref, o_ref, acc_ref):
    j = pl.program_id(1)

    @pl.when(j == 0)
    def _():
        acc_ref[...] = jnp.zeros_like(acc_ref)

    xb = x_ref[...].astype(BF)
    h1 = _dot(xb, w1_ref[...])
    h3 = _dot(xb, w3_ref[...])
    hh = (h1 * jax.nn.sigmoid(h1)) * h3
    acc_ref[...] += _dot(hh.astype(BF), w2_ref[...])

    @pl.when(j == pl.num_programs(1) - 1)
    def _():
        y = ALPHA * x_ref[...] + 0.5 * acc_ref[...]
        o_ref[...] = _ln(y, g_ref[...], b_ref[...])


def _ffn_ln(x, w1, w3, w2, g, b, *, tm, tf):
    n, d = x.shape
    f = w1.shape[1]
    return pl.pallas_call(
        _ffn_ln_kernel,
        grid=(n // tm, f // tf),
        in_specs=[
            pl.BlockSpec((tm, d), lambda i, j: (i, 0)),
            pl.BlockSpec((d, tf), lambda i, j: (0, j)),
            pl.BlockSpec((d, tf), lambda i, j: (0, j)),
            pl.BlockSpec((tf, d), lambda i, j: (j, 0)),
            pl.BlockSpec((1, d), lambda i, j: (0, 0)),
            pl.BlockSpec((1, d), lambda i, j: (0, 0)),
        ],
        out_specs=pl.BlockSpec((tm, d), lambda i, j: (i, 0)),
        out_shape=jax.ShapeDtypeStruct((n, d), F32),
        scratch_shapes=[pltpu.VMEM((tm, d), F32)],
        compiler_params=_params(("parallel", "arbitrary")),
        name="ffn_ln",
    )(x, w1, w3, w2, g, b)


def _ab_kernel(h_ref, w_ref, b_ref, lng_ref, lnb_ref, ws_ref, bst_ref, wga_ref, cw_ref, wcb_ref,
               o_ref, prev_ref, *, tiles_per_seq):
    i = pl.program_id(0)
    tm = h_ref.shape[0]
    hb = h_ref[...].astype(BF)

    def proj(c0, width):
        return _dot(hb, w_ref[:, c0:c0 + width]) + b_ref[:, c0:c0 + width]

    u = proj(0, GMLP_WIDTH)
    v = _ln(proj(512, GMLP_WIDTH), lng_ref[...], lnb_ref[...]).astype(BF)
    row = lax.broadcasted_iota(jnp.int32, (GMLP_CHUNK, GMLP_CHUNK), 0)
    col = lax.broadcasted_iota(jnp.int32, (GMLP_CHUNK, GMLP_CHUNK), 1)
    gd = GMLP_WIDTH // GMLP_GROUPS
    wgs = [jnp.where(row >= col, ws_ref[g], 0.0).astype(BF) for g in range(GMLP_GROUPS)]
    chunks = []
    for c in range(tm // GMLP_CHUNK):
        r0 = c * GMLP_CHUNK
        chunks.append(jnp.concatenate(
            [_dot(wgs[g], v[r0:r0 + GMLP_CHUNK, g * gd:(g + 1) * gd]) + bst_ref[:, g:g + 1]
             for g in range(GMLP_GROUPS)], axis=1))
    s = jnp.concatenate(chunks, axis=0)
    ya = _dot((u * s).astype(BF), wga_ref[...])

    cb = proj(1024, CONV_WIDTH)
    z = proj(1536, CONV_WIDTH) * proj(2048, CONV_WIDTH)

    @pl.when(i % tiles_per_seq == 0)
    def _():
        prev_ref[...] = jnp.zeros_like(prev_ref)

    zext = jnp.concatenate([prev_ref[...], z], axis=0)
    z1 = pltpu.roll(zext, 1, 0)[CONV_HALO:]
    z2 = pltpu.roll(zext, 2, 0)[CONV_HALO:]
    y = cw_ref[0:1, :] * z2 + cw_ref[1:2, :] * z1 + cw_ref[2:3, :] * z
    prev_ref[...] = z[tm - CONV_HALO:, :]
    yb = _dot((cb * y).astype(BF), wcb_ref[...])

    ga = proj(2560, D_MODEL)
    gb = proj(3584, D_MODEL)
    o_ref[...] = jax.nn.sigmoid(ga) * ya + jax.nn.sigmoid(gb) * yb


def _mix_ab(h, w, b, lng, lnb, ws, bst, wga, cw, wcb, *, tm, seq):
    n, d = h.shape
    kern = functools.partial(_ab_kernel, tiles_per_seq=seq // tm)
    return pl.pallas_call(
        kern,
        grid=(n // tm,),
        in_specs=[pl.BlockSpec((tm, d), lambda i: (i, 0))] + [_resident(a.shape) for a in (w, b, lng, lnb, ws, bst, wga, cw, wcb)],
        out_specs=pl.BlockSpec((tm, d), lambda i: (i, 0)),
        out_shape=jax.ShapeDtypeStruct((n, d), F32),
        scratch_shapes=[pltpu.VMEM((CONV_HALO, CONV_WIDTH), F32)],
        compiler_params=_params(("arbitrary",)),
        name="mix_ab",
    )(h, w, b, lng, lnb, ws, bst, wga, cw, wcb)


def _mla_proj_kernel(h_ref, w_ref, b_ref, qg_ref, kvg_ref, wqa_ref, wqb_ref, wk_ref, wv_ref, pk_ref,
                     cq_ref, sq_ref, ck_ref, sk_ref, q_ref, k_ref, v_ref):
    hb = h_ref[...].astype(BF)
    z = _dot(hb, w_ref[...]) + b_ref[...]
    qn = _rms(z[:, 0:256], qg_ref[...]).astype(BF)
    kvn = _rms(z[:, 256:384], kvg_ref[...]).astype(BF)
    cq = jnp.concatenate([cq_ref[...]] * MLA_HEADS, axis=1)
    sq = jnp.concatenate([sq_ref[...]] * MLA_HEADS, axis=1)
    scale = (MLA_NOPE + MLA_ROPE) ** -0.5
    q = (_dot(qn, wqa_ref[...]) * cq + _dot(qn, wqb_ref[...]) * sq) * scale
    q_ref[...] = q.astype(BF)
    kpe = (z[:, 384:512] * ck_ref[...] + z[:, 512:640] * sk_ref[...]).astype(BF)
    k_ref[...] = (_dot(kvn, wk_ref[...]) + _dot(kpe, pk_ref[...])).astype(BF)
    v_ref[...] = _dot(kvn, wv_ref[...]).astype(BF)


def _mla_proj(h, w, b, qg, kvg, wqa, wqb, wk, wv, pk, cq, sq, ck, sk, *, tm, seq):
    n, d = h.shape
    tps = seq // tm
    tab = pl.BlockSpec((tm, LANE), lambda i: (i % tps, 0))
    wide = MLA_HEADS * LANE
    out = jax.ShapeDtypeStruct((n, wide), BF)
    return pl.pallas_call(
        _mla_proj_kernel,
        grid=(n // tm,),
        in_specs=[pl.BlockSpec((tm, d), lambda i: (i, 0))]
        + [_resident(a.shape) for a in (w, b, qg, kvg, wqa, wqb, wk, wv, pk)] + [tab] * 4,
        out_specs=[pl.BlockSpec((tm, wide), lambda i: (i, 0))] * 3,
        out_shape=[out, out, out],
        compiler_params=_params(("parallel",)),
        name="mla_proj",
    )(h, w, b, qg, kvg, wqa, wqb, wk, wv, pk, cq, sq, ck, sk)


def _online_softmax_step(s, v, carry):
    m, l, acc = carry
    m_new = jnp.maximum(m, jnp.max(s, -1, keepdims=True))
    alpha = jnp.exp(m - m_new)
    p = jnp.exp(s - m_new)
    l = alpha * l + jnp.sum(p, -1, keepdims=True)
    acc = alpha * acc + _dot(p.astype(BF), v)
    return m_new, l, acc


def _softmax_init(rows, width):
    return (jnp.full((rows, 1), NEG, F32), jnp.zeros((rows, 1), F32), jnp.zeros((rows, width), F32))


def _flash_kernel(q_ref, k_ref, v_ref, o_ref, *, tq):
    qi = pl.program_id(2)
    q = q_ref[0]

    def tile(j, carry, diagonal):
        k0 = pl.multiple_of(j * tq, tq)
        s = _dot_t(q, k_ref[0, pl.ds(k0, tq), :])
        if diagonal:
            r = lax.broadcasted_iota(jnp.int32, (tq, tq), 0)
            c = lax.broadcasted_iota(jnp.int32, (tq, tq), 1)
            s = jnp.where(c <= r, s, NEG)
        return _online_softmax_step(s, v_ref[0, pl.ds(k0, tq), :], carry)

    carry = lax.fori_loop(0, qi, lambda j, c: tile(j, c, False), _softmax_init(tq, LANE))
    _, l, acc = tile(qi, carry, True)
    o_ref[0] = (acc / l).astype(o_ref.dtype)


def _flash_causal(q, k, v, *, tq):
    bsz, seq, wide = q.shape
    heads = wide // LANE
    kern = functools.partial(_flash_kernel, tq=tq)
    return pl.pallas_call(
        kern,
        grid=(bsz, heads, seq // tq),
        in_specs=[
            pl.BlockSpec((1, tq, LANE), lambda b, h, i: (b, i, h)),
            pl.BlockSpec((1, seq, LANE), lambda b, h, i: (b, 0, h)),
            pl.BlockSpec((1, seq, LANE), lambda b, h, i: (b, 0, h)),
        ],
        out_specs=pl.BlockSpec((1, tq, LANE), lambda b, h, i: (b, i, h)),
        out_shape=jax.ShapeDtypeStruct((bsz, seq, wide), BF),
        compiler_params=_params(("parallel", "parallel", "arbitrary")),
        name="mla_flash",
    )(q, k, v)


def _nsa_proj_kernel(h_ref, w_ref, b_ref, c_ref, s_ref, q_ref, kc_ref, vc_ref, ks_ref, vs_ref, kw_ref, vw_ref, g_ref):
    hb = h_ref[...].astype(BF)

    def proj(c0, width):
        return _dot(hb, w_ref[:, c0:c0 + width]) + b_ref[:, c0:c0 + width]

    c = c_ref[...]
    s = s_ref[...]
    c8 = jnp.concatenate([c] * NSA_HEADS, axis=1)
    s8 = jnp.concatenate([s] * NSA_HEADS, axis=1)
    c2 = jnp.concatenate([c] * NSA_GROUPS, axis=1)
    s2 = jnp.concatenate([s] * NSA_GROUPS, axis=1)
    q = (proj(0, 1024) * c8 + proj(1024, 1024) * s8) * (NSA_DIM ** -0.5)
    q_ref[...] = q.astype(BF)
    kc_ref[...] = proj(2048, 128)
    vc_ref[...] = proj(2176, 128)
    ks_ref[...] = (proj(2304, 256) * c2 + proj(2560, 256) * s2).astype(BF)
    vs_ref[...] = proj(2816, 256).astype(BF)
    kw_ref[...] = (proj(3072, 256) * c2 + proj(3328, 256) * s2).astype(BF)
    vw_ref[...] = proj(3584, 256).astype(BF)
    g_ref[...] = jax.nn.sigmoid(proj(3840, 256))


def _nsa_proj(h, w, b, cn, sn, *, tm, seq):
    n, d = h.shape
    tps = seq // tm
    tab = pl.BlockSpec((tm, LANE), lambda i: (i % tps, 0))

    def out(width, dt):
        return pl.BlockSpec((tm, width), lambda i: (i, 0)), jax.ShapeDtypeStruct((n, width), dt)

    outs = [out(1024, BF), out(128, F32), out(128, F32), out(256, BF), out(256, BF), out(256, BF), out(256, BF), out(256, F32)]
    return pl.pallas_call(
        _nsa_proj_kernel,
        grid=(n // tm,),
        in_specs=[pl.BlockSpec((tm, d), lambda i: (i, 0)), _resident(w.shape), _resident(b.shape), tab, tab],
        out_specs=[o[0] for o in outs],
        out_shape=[o[1] for o in outs],
        compiler_params=_params(("parallel",)),
        name="nsa_proj",
    )(h, w, b, cn, sn)


def _nsa_cmp_kernel(kc_ref, vc_ref, pe_ref, wk_ref, wkr_ref, wv_ref, c_ref, s_ref, kcmp_ref, vcmp_ref):
    n16 = kc_ref.shape[1]

    def halves(x_ref, pe_lo, pe_hi):
        a = x_ref[0]
        nxt = pltpu.roll(a, n16 - 1, 0)
        return (a + pe_lo).astype(BF), (nxt + pe_hi).astype(BF)

    klo, khi = halves(kc_ref, pe_ref[0:1, :], pe_ref[1:2, :])
    kc = _dot(klo, wk_ref[0]) + _dot(khi, wk_ref[1])
    kcr = _dot(klo, wkr_ref[0]) + _dot(khi, wkr_ref[1])
    kcmp_ref[0] = (kc * c_ref[...] + kcr * s_ref[...]).astype(BF)
    vlo, vhi = halves(vc_ref, pe_ref[2:3, :], pe_ref[3:4, :])
    vcmp_ref[0] = (_dot(vlo, wv_ref[0]) + _dot(vhi, wv_ref[1])).astype(BF)


def _nsa_compress(kc, vc, pe, wk, wkr, wv, cc, sc):
    bsz, n16, wide = kc.shape
    blk = pl.BlockSpec((1, n16, wide), lambda b: (b, 0, 0))
    oblk = pl.BlockSpec((1, n16, NSA_GROUPS * LANE), lambda b: (b, 0, 0))
    osh = jax.ShapeDtypeStruct((bsz, n16, NSA_GROUPS * LANE), BF)
    return pl.pallas_call(
        _nsa_cmp_kernel,
        grid=(bsz,),
        in_specs=[blk, blk] + [_resident(a.shape) for a in (pe, wk, wkr, wv, cc, sc)],
        out_specs=[oblk, oblk],
        out_shape=[osh, osh],
        compiler_params=_params(("parallel",)),
        name="nsa_compress",
    )(kc, vc, pe, wk, wkr, wv, cc, sc)


def _nsa_attn_kernel(q_ref, kcmp_ref, vcmp_ref, ks_ref, vs_ref, kw_ref, vw_ref, g_ref, ov_ref, o_ref, *, tk, top_k):
    qi = pl.program_id(2)
    T = q_ref.shape[1]
    R = NSA_HPG * T
    q0 = qi * T
    q4 = jnp.concatenate([q_ref[0, :, h * LANE:(h + 1) * LANE] for h in range(NSA_HPG)], axis=0)
    qpos = lax.broadcasted_iota(jnp.int32, (T, 1), 0) + q0

    def rep(x):
        return jnp.concatenate([x] * NSA_HPG, axis=0)

    ncp = kcmp_ref.shape[1]
    s = _dot_t(q4, kcmp_ref[0])
    cmp_end = lax.broadcasted_iota(jnp.int32, (1, ncp), 1) * CMP_STRIDE + (CMP_BLOCK - 1)
    validf = rep(jnp.where(cmp_end <= qpos, 1.0, 0.0))
    valid = validf > 0.5
    sm = jnp.where(valid, s, NEG)
    e = jnp.exp(sm - jnp.max(sm, -1, keepdims=True))
    p = jnp.where(valid, e / jnp.sum(e, -1, keepdims=True), 0.0)
    o_cmp = _dot(p.astype(BF), vcmp_ref[0])

    psum = p[0:T] + p[T:2 * T] + p[2 * T:3 * T] + p[3 * T:4 * T]
    hi = psum.astype(BF)
    r1 = psum - hi.astype(F32)
    mid = r1.astype(BF)
    lo = (r1 - mid.astype(F32)).astype(BF)
    ov = ov_ref[...]
    imp = _dot(hi, ov) + _dot(mid, ov) + _dot(lo, ov)
    nb = ov_ref.shape[1]
    jr = lax.broadcasted_iota(jnp.int32, (1, nb), 1)
    jq = lax.shift_right_logical(qpos, 6)
    forced = (jr == 0) | (jr == jq) | (jr == jq - 1)
    imp = jnp.where(forced, 1e9, imp)
    imp = jnp.where(jr <= jq, imp, -1.0)

    jrf = jr.astype(F32)
    work = imp
    sel = jnp.zeros_like(imp)
    for _ in range(top_k):
        mx = jnp.max(work, -1, keepdims=True)
        idx = jnp.min(jnp.where(work == mx, jrf, float(nb)), -1, keepdims=True)
        pick = jrf == idx
        sel = jnp.where(pick, 1.0, sel)
        work = jnp.where(pick, -2.0, work)
    sel_b = jnp.where(imp >= 0.0, sel, 0.0).astype(BF)

    def slc_tile(j, carry):
        k0 = pl.multiple_of(j * tk, tk)
        sc = _dot_t(q4, ks_ref[0, pl.ds(k0, tk), :])
        blk_of_key = lax.shift_right_logical(lax.broadcasted_iota(jnp.int32, (nb, tk), 1) + k0, 6)
        expand = jnp.where(blk_of_key == lax.broadcasted_iota(jnp.int32, (nb, tk), 0), 1.0, 0.0).astype(BF)
        chosen = _dot(sel_b, expand)
        kpos = lax.broadcasted_iota(jnp.int32, (1, tk), 1) + k0
        bias = jnp.where((chosen > 0.5) & (kpos <= qpos), 0.0, NEG)
        return _online_softmax_step(sc + rep(bias), vs_ref[0, pl.ds(k0, tk), :], carry)

    n_tiles = (q0 + T + tk - 1) // tk
    _, l, acc = lax.fori_loop(0, n_tiles, slc_tile, _softmax_init(R, LANE))
    o_slc = acc / l

    def win_tile(j, carry):
        k0 = pl.multiple_of(j * T, T)
        sc = _dot_t(q4, kw_ref[0, pl.ds(k0, T), :])
        dist = qpos - (lax.broadcasted_iota(jnp.int32, (1, T), 1) + k0)
        bias = jnp.where((dist >= 0) & (dist < WINDOW), 0.0, NEG)
        return _online_softmax_step(sc + rep(bias), vw_ref[0, pl.ds(k0, T), :], carry)

    _, l, acc = lax.fori_loop(jnp.maximum(qi - WINDOW // T, 0), qi + 1, win_tile, _softmax_init(R, LANE))
    o_win = acc / l

    g = g_ref[0]
    outs = []
    for h in range(NSA_HPG):
        rows = slice(h * T, (h + 1) * T)
        outs.append(g[:, 3 * h:3 * h + 1] * o_cmp[rows] + g[:, 3 * h + 1:3 * h + 2] * o_slc[rows]
                    + g[:, 3 * h + 2:3 * h + 3] * o_win[rows])
    o_ref[0] = jnp.concatenate(outs, axis=1).astype(o_ref.dtype)


def _nsa_attention(q, kcmp, vcmp, ks, vs, kw, vw, gates, ov, *, tq, tk, top_k):
    bsz, seq, _ = q.shape
    n16 = kcmp.shape[1]
    gw = NSA_HPG * LANE
    kern = functools.partial(_nsa_attn_kernel, tk=tk, top_k=top_k)
    cblk = pl.BlockSpec((1, n16, LANE), lambda b, g, i: (b, 0, g))
    sblk = pl.BlockSpec((1, seq, LANE), lambda b, g, i: (b, 0, g))
    return pl.pallas_call(
        kern,
        grid=(bsz, NSA_GROUPS, seq // tq),
        in_specs=[pl.BlockSpec((1, tq, gw), lambda b, g, i: (b, i, g)), cblk, cblk, sblk, sblk, sblk, sblk,
                  pl.BlockSpec((1, tq, LANE), lambda b, g, i: (b, i, g)), _resident(ov.shape)],
        out_specs=pl.BlockSpec((1, tq, gw), lambda b, g, i: (b, i, g)),
        out_shape=jax.ShapeDtypeStruct((bsz, seq, NSA_GROUPS * gw), BF),
        compiler_params=_params(("parallel", "parallel", "arbitrary")),
        name="nsa_attn",
    )(q, kcmp, vcmp, ks, vs, kw, vw, gates, ov)


def _merge_kernel(x_ref, ab_ref, oc_ref, od_ref, wg_ref, bg_ref, wc_ref, wd_ref, wo_ref, g_ref, b_ref, o_ref):
    x = x_ref[...]
    gates = _dot(x.astype(BF), wg_ref[...]) + bg_ref[...]
    yc = _dot(oc_ref[...], wc_ref[...])
    yd = _dot(od_ref[...], wd_ref[...])
    merged = ab_ref[...] + jax.nn.sigmoid(gates[:, :D_MODEL]) * yc + jax.nn.sigmoid(gates[:, D_MODEL:]) * yd
    mix = _dot(merged.astype(BF), wo_ref[...])
    o_ref[...] = _ln(ALPHA * x + mix, g_ref[...], b_ref[...])


def _merge_ln(x, ab, oc, od, wg, bg, wc, wd, wo, g, b, *, tm):
    n, d = x.shape
    row = pl.BlockSpec((tm, d), lambda i: (i, 0))
    return pl.pallas_call(
        _merge_kernel,
        grid=(n // tm,),
        in_specs=[row, row, row, row] + [_resident(a.shape) for a in (wg, bg, wc, wd, wo, g, b)],
        out_specs=row,
        out_shape=jax.ShapeDtypeStruct((n, d), F32),
        compiler_params=_params(("parallel",)),
        name="merge_ln",
    )(x, ab, oc, od, wg, bg, wc, wd, wo, g, b)


def _linear_kernel(x_ref, w_ref, o_ref):
    o_ref[...] = _dot(x_ref[...].astype(BF), w_ref[...]).astype(o_ref.dtype)


def _linear(x, w, *, tm, dtype):
    n, d = x.shape
    return pl.pallas_call(
        _linear_kernel,
        grid=(n // tm,),
        in_specs=[pl.BlockSpec((tm, d), lambda i: (i, 0)), _resident(w.shape)],
        out_specs=pl.BlockSpec((tm, w.shape[1]), lambda i: (i, 0)),
        out_shape=jax.ShapeDtypeStruct((n, w.shape[1]), dtype),
        compiler_params=_params(("parallel",)),
        name="mem_kv",
    )(x, w)


def _xattn_kernel(x_ref, k_ref, v_ref, wq_ref, wo_ref, g_ref, b_ref, o_ref):
    x = x_ref[...]
    q = _dot(x.astype(BF), wq_ref[...]).astype(BF)
    k = k_ref[0]
    v = v_ref[0]
    heads = []
    for h in range(XATTN_HEADS):
        sl = slice(h * XATTN_DIM, (h + 1) * XATTN_DIM)
        s = _dot_t(q[:, sl], k[:, sl]) * (XATTN_DIM ** -0.5)
        e = jnp.exp(s - jnp.max(s, -1, keepdims=True))
        p = e / jnp.sum(e, -1, keepdims=True)
        heads.append(_dot(p.astype(BF), v[:, sl]))
    o = jnp.concatenate(heads, axis=1).astype(BF)
    o_ref[...] = _ln(ALPHA * x + _dot(o, wo_ref[...]), g_ref[...], b_ref[...])


def _xattn_ln(x, kv, wq, wo, g, b, *, tm, seq):
    n, d = x.shape
    tps = seq // tm
    mlen = kv.shape[1]
    hd = XATTN_HEADS * XATTN_DIM
    return pl.pallas_call(
        _xattn_kernel,
        grid=(n // tm,),
        in_specs=[pl.BlockSpec((tm, d), lambda i: (i, 0)),
                  pl.BlockSpec((1, mlen, hd), lambda i: (i // tps, 0, 0)),
                  pl.BlockSpec((1, mlen, hd), lambda i: (i // tps, 0, 1))]
        + [_resident(a.shape) for a in (wq, wo, g, b)],
        out_specs=pl.BlockSpec((tm, d), lambda i: (i, 0)),
        out_shape=jax.ShapeDtypeStruct((n, d), F32),
        compiler_params=_params(("parallel",)),
        name="xattn_ln",
    )(x, kv, kv, wq, wo, g, b)


def _rope_tab(pos, dim):
    inv = ROPE_THETA ** (-(jnp.arange(0, dim, 2, dtype=F32) / dim))
    ang = pos[:, None] * inv[None, :]
    return jnp.cos(ang), jnp.sin(ang)


def _rot_cols(w, half):
    return jnp.concatenate([-w[..., half:2 * half], w[..., :half]], axis=-1)


def _pad_slots(w, n_slots, width):
    lead = w.shape[:-1]
    w = w.reshape(lead + (n_slots, width))
    w = jnp.pad(w, [(0, 0)] * len(lead) + [(0, 0), (0, LANE - width)])
    return w.reshape(lead + (n_slots * LANE,))


def _rot_slots(w, n_slots, width, half):
    lead = w.shape[:-1]
    w = w.reshape(lead + (n_slots, width))
    return _rot_cols(w, half).reshape(lead + (n_slots * width,))


def _pad_rows(w, n_slots, width):
    d = w.shape[-1]
    w = w.reshape(n_slots, width, d)
    w = jnp.pad(w, [(0, 0), (0, LANE - width), (0, 0)])
    return w.reshape(n_slots * LANE, d)


def _layer_params(l, p):
    w_in, b_in = p["w_in"][l], p["b_in"][l]

    def cols(o, wd):
        return w_in[:, o:o + wd], b_in[o:o + wd]

    out = {}
    out["w_ab"] = jnp.concatenate([w_in[:, 0:2560], w_in[:, _O_GA:_O_GC]], axis=1).astype(BF)
    out["b_ab"] = jnp.concatenate([b_in[0:2560], b_in[_O_GA:_O_GC]])[None, :]
    wkr, bkr = cols(_O_KROPE, MLA_ROPE)
    half = MLA_ROPE // 2
    padk = lambda a: jnp.pad(a, [(0, 0)] * (a.ndim - 1) + [(0, LANE - MLA_ROPE)])
    out["w_c"] = jnp.concatenate([w_in[:, _O_QLAT:_O_KROPE], padk(wkr), padk(_rot_cols(wkr, half))], axis=1).astype(BF)
    out["b_c"] = jnp.concatenate([b_in[_O_QLAT:_O_KROPE], padk(bkr), padk(_rot_cols(bkr, half))])[None, :]
    wuq = p["mla_wuq"][l].reshape(MLA_Q_RANK, MLA_HEADS, MLA_NOPE + MLA_ROPE)
    rope_rot = _rot_cols(wuq[..., MLA_NOPE:], half)
    wqa = jnp.pad(wuq, [(0, 0), (0, 0), (0, LANE - MLA_NOPE - MLA_ROPE)])
    wqb = jnp.pad(rope_rot, [(0, 0), (0, 0), (MLA_NOPE, LANE - MLA_NOPE - MLA_ROPE)])
    out["wqa"] = wqa.reshape(MLA_Q_RANK, MLA_HEADS * LANE).astype(BF)
    out["wqb"] = wqb.reshape(MLA_Q_RANK, MLA_HEADS * LANE).astype(BF)
    wukv = p["mla_wukv"][l].reshape(MLA_KV_RANK, MLA_HEADS, MLA_NOPE + MLA_V)
    out["wk_c"] = jnp.pad(wukv[..., :MLA_NOPE], [(0, 0), (0, 0), (0, LANE - MLA_NOPE)]).reshape(MLA_KV_RANK, -1).astype(BF)
    out["wv_c"] = jnp.pad(wukv[..., MLA_NOPE:], [(0, 0), (0, 0), (0, LANE - MLA_V)]).reshape(MLA_KV_RANK, -1).astype(BF)
    wq, bq = cols(_O_NQ, NSA_HEADS * NSA_DIM)
    hd = NSA_DIM // 2
    pieces_w, pieces_b = [], []

    def add(w, b, slots, roped):
        pieces_w.append(_pad_slots(w, slots, NSA_DIM))
        pieces_b.append(_pad_slots(b, slots, NSA_DIM))
        if roped:
            pieces_w.append(_pad_slots(_rot_slots(w, slots, NSA_DIM, hd), slots, NSA_DIM))
            pieces_b.append(_pad_slots(_rot_slots(b, slots, NSA_DIM, hd), slots, NSA_DIM))

    add(wq, bq, NSA_HEADS, True)
    wkc, bkc = cols(_O_NKC, 128)
    wvc, bvc = cols(_O_NVC, 128)
    pieces_w += [wkc, wvc]
    pieces_b += [bkc, bvc]
    add(*cols(_O_NKS, 128), NSA_GROUPS, True)
    add(*cols(_O_NVS, 128), NSA_GROUPS, False)
    add(*cols(_O_NKW, 128), NSA_GROUPS, True)
    add(*cols(_O_NVW, 128), NSA_GROUPS, False)
    wg, bg = cols(_O_NGATE, NSA_HEADS * 3)
    pieces_w.append(_pad_slots(wg, NSA_GROUPS, NSA_HPG * 3))
    pieces_b.append(_pad_slots(bg, NSA_GROUPS, NSA_HPG * 3))
    out["w_d"] = jnp.concatenate(pieces_w, axis=1).astype(BF)
    out["b_d"] = jnp.concatenate(pieces_b)[None, :]

    def cmp_weights(w):
        eye = jnp.eye(NSA_GROUPS, dtype=F32)
        wp = jnp.pad(w, [(0, 0), (0, 0), (0, LANE - NSA_DIM)])
        full = jnp.einsum("lde,gh->lgdhe", wp, eye).reshape(CMP_BLOCK, NSA_GROUPS * NSA_DIM, NSA_GROUPS * LANE)
        return full.reshape(2, CMP_STRIDE * NSA_GROUPS * NSA_DIM, NSA_GROUPS * LANE).astype(BF)

    wck = p["nsa_wcmp_k"][l]
    out["wcmp_k"] = cmp_weights(wck)
    out["wcmp_kr"] = cmp_weights(_rot_cols(wck, hd))
    out["wcmp_v"] = cmp_weights(p["nsa_wcmp_v"][l])

    def pe_rows(pe):
        t = jnp.broadcast_to(pe[:, None, :], (CMP_BLOCK, NSA_GROUPS, NSA_DIM))
        return t.reshape(2, CMP_STRIDE * NSA_GROUPS * NSA_DIM)

    out["pe"] = jnp.concatenate([pe_rows(p["nsa_pe_k"][l]), pe_rows(p["nsa_pe_v"][l])], axis=0)
    out["w_g"] = w_in[:, _O_GC:].astype(BF)
    out["b_g"] = b_in[_O_GC:][None, :]
    out["wout_c"] = _pad_rows(p["mla_wout"][l], MLA_HEADS, MLA_V).astype(BF)
    out["wout_d"] = _pad_rows(p["nsa_wout"][l], NSA_HEADS, NSA_DIM).astype(BF)
    return out


def _tables(seq):
    pos = jnp.arange(seq, dtype=F32)
    c16, s16 = _rope_tab(pos, MLA_ROPE)
    one = jnp.ones((seq, MLA_NOPE), F32)
    zero = jnp.zeros((seq, MLA_NOPE), F32)
    tail = LANE - MLA_NOPE - MLA_ROPE
    cq = jnp.concatenate([one, c16, c16, jnp.ones((seq, tail), F32)], axis=1)
    sq = jnp.concatenate([zero, s16, s16, jnp.zeros((seq, tail), F32)], axis=1)
    ck = jnp.pad(jnp.concatenate([c16, c16], axis=1), [(0, 0), (0, LANE - MLA_ROPE)])
    sk = jnp.pad(jnp.concatenate([s16, s16], axis=1), [(0, 0), (0, LANE - MLA_ROPE)])
    c32, s32 = _rope_tab(pos, NSA_DIM)
    cn = jnp.pad(jnp.concatenate([c32, c32], axis=1), [(0, 0), (0, LANE - NSA_DIM)])
    sn = jnp.pad(jnp.concatenate([s32, s32], axis=1), [(0, 0), (0, LANE - NSA_DIM)])
    n16 = seq // CMP_STRIDE
    cend = (jnp.arange(n16) * CMP_STRIDE + CMP_BLOCK - 1).astype(F32)
    cc32, cs32 = _rope_tab(cend, NSA_DIM)
    ccg = jnp.pad(jnp.concatenate([cc32, cc32], axis=1), [(0, 0), (0, LANE - NSA_DIM)])
    csg = jnp.pad(jnp.concatenate([cs32, cs32], axis=1), [(0, 0), (0, LANE - NSA_DIM)])
    cc = jnp.concatenate([ccg] * NSA_GROUPS, axis=1)
    cs = jnp.concatenate([csg] * NSA_GROUPS, axis=1)
    n_cmp = (seq - CMP_BLOCK) // CMP_STRIDE + 1
    n_slc = seq // SLC_BLOCK
    cstart = jnp.arange(n16) * CMP_STRIDE
    sstart = jnp.arange(LANE) * SLC_BLOCK
    ovl = (jnp.minimum(cstart[:, None] + CMP_BLOCK, sstart[None, :] + SLC_BLOCK)
           - jnp.maximum(cstart[:, None], sstart[None, :]))
    ovl = jnp.clip(ovl, 0).astype(F32) / CMP_BLOCK
    ovl = jnp.where((jnp.arange(n16)[:, None] < n_cmp) & (jnp.arange(LANE)[None, :] < n_slc), ovl, 0.0).astype(BF)
    pk = jnp.zeros((LANE, MLA_HEADS, LANE), F32)
    pk = pk.at[jnp.arange(MLA_ROPE), :, MLA_NOPE + jnp.arange(MLA_ROPE)].set(1.0)
    pk = pk.reshape(LANE, MLA_HEADS * LANE).astype(BF)
    return dict(cq=cq, sq=sq, ck=ck, sk=sk, cn=cn, sn=sn, cc=cc, cs=cs, ovl=ovl, pk=pk)


def kernel(x, mem, ffn1_w1, ffn1_w3, ffn1_w2, ln1_g, ln1_b, w_in, b_in, gmlp_ln_g, gmlp_ln_b, gmlp_ws, gmlp_bs, gmlp_wout, conv_w, conv_wout, mla_qnorm_g, mla_kvnorm_g, mla_wuq, mla_wukv, mla_wout, nsa_pe_k, nsa_pe_v, nsa_wcmp_k, nsa_wcmp_v, nsa_wout, w_o, ln2_g, ln2_b, xattn_wq, xattn_wk, xattn_wv, xattn_wo, ln3_g, ln3_b, ffn2_w1, ffn2_w3, ffn2_w2, ln4_g, ln4_b):
    bsz, seq, d = x.shape
    mlen = mem.shape[1]
    n = bsz * seq
    assert d == D_MODEL and seq % 256 == 0 and seq >= WINDOW + 128
    p = dict(w_in=w_in, b_in=b_in, mla_wuq=mla_wuq, mla_wukv=mla_wukv, mla_wout=mla_wout,
             nsa_pe_k=nsa_pe_k, nsa_pe_v=nsa_pe_v, nsa_wcmp_k=nsa_wcmp_k, nsa_wcmp_v=nsa_wcmp_v, nsa_wout=nsa_wout)
    tb = _tables(seq)
    tm = 256
    tm_ffn = 512
    tf = D_FF // 2
    n16 = seq // CMP_STRIDE
    top_k = min(SLC_TOPK, seq // SLC_BLOCK)
    row = lambda a: a[None, :]

    h = x.reshape(n, d)
    mem2 = mem.reshape(bsz * mlen, d)
    for l in range(DEPTH):
        lp = _layer_params(l, p)
        h = _ffn_ln(h, ffn1_w1[l].astype(BF), ffn1_w3[l].astype(BF), ffn1_w2[l].astype(BF),
                    row(ln1_g[l]), row(ln1_b[l]), tm=tm_ffn, tf=tf)
        ab = _mix_ab(h, lp["w_ab"], lp["b_ab"], row(gmlp_ln_g[l]), row(gmlp_ln_b[l]), gmlp_ws[l], gmlp_bs[l].T,
                     gmlp_wout[l].astype(BF), conv_w[l], conv_wout[l].astype(BF), tm=tm, seq=seq)
        qc, kc_, vc_ = _mla_proj(h, lp["w_c"], lp["b_c"], row(mla_qnorm_g[l]), row(mla_kvnorm_g[l]),
                                 lp["wqa"], lp["wqb"], lp["wk_c"], lp["wv_c"], tb["pk"],
                                 tb["cq"], tb["sq"], tb["ck"], tb["sk"], tm=tm, seq=seq)
        wide = MLA_HEADS * LANE
        oc = _flash_causal(qc.reshape(bsz, seq, wide), kc_.reshape(bsz, seq, wide), vc_.reshape(bsz, seq, wide), tq=256)
        qn, nkc, nvc, nks, nvs, nkw, nvw, gates = _nsa_proj(h, lp["w_d"], lp["b_d"], tb["cn"], tb["sn"], tm=tm, seq=seq)
        kcmp, vcmp = _nsa_compress(nkc.reshape(bsz, n16, CMP_STRIDE * 128), nvc.reshape(bsz, n16, CMP_STRIDE * 128),
                                   lp["pe"], lp["wcmp_k"], lp["wcmp_kr"], lp["wcmp_v"], tb["cc"], tb["cs"])
        gw = NSA_GROUPS * LANE
        od = _nsa_attention(qn.reshape(bsz, seq, NSA_HEADS * LANE), kcmp, vcmp,
                            nks.reshape(bsz, seq, gw), nvs.reshape(bsz, seq, gw),
                            nkw.reshape(bsz, seq, gw), nvw.reshape(bsz, seq, gw),
                            gates.reshape(bsz, seq, gw), tb["ovl"], tq=128, tk=256, top_k=top_k)
        h = _merge_ln(h, ab, oc.reshape(n, wide), od.reshape(n, NSA_HEADS * LANE), lp["w_g"], lp["b_g"],
                      lp["wout_c"], lp["wout_d"], w_o[l].astype(BF), row(ln2_g[l]), row(ln2_b[l]), tm=tm)
        kv = _linear(mem2, jnp.concatenate([xattn_wk[l], xattn_wv[l]], axis=1).astype(BF), tm=min(256, bsz * mlen), dtype=BF)
        h = _xattn_ln(h, kv.reshape(bsz, mlen, 2 * XATTN_HEADS * XATTN_DIM), xattn_wq[l].astype(BF),
                      xattn_wo[l].astype(BF), row(ln3_g[l]), row(ln3_b[l]), tm=tm, seq=seq)
        h = _ffn_ln(h, ffn2_w1[l].astype(BF), ffn2_w3[l].astype(BF), ffn2_w2[l].astype(BF),
                    row(ln4_g[l]), row(ln4_b[l]), tm=tm_ffn, tf=tf)
    return h.reshape(bsz, seq, d)
```

```python
import functools

import jax
import jax.numpy as jnp
from jax import lax
from jax.experimental import pallas as pl
from jax.experimental.pallas import tpu as pltpu

BF = jnp.bfloat16
F32 = jnp.float32

D_MODEL = 1024
D_FF = 2816
LN_EPS = 1e-5
RMS_EPS = 1e-6
ROPE_THETA = 10000.0
DEPTH = 2
ALPHA = (2 * DEPTH) ** 0.25
NEG = -1e30

GMLP_CHUNK = 128
GMLP_GROUPS = 4
GMLP_WIDTH = 512
CONV_WIDTH = 512
CONV_K = 3
MLA_HEADS = 8
MLA_Q_RANK = 256
MLA_KV_RANK = 128
MLA_NOPE = 64
MLA_ROPE = 32
MLA_V = 64
NSA_HEADS = 8
NSA_GROUPS = 2
NSA_HPG = 4
NSA_DIM = 64
CMP_BLOCK = 32
CMP_STRIDE = 16
SLC_BLOCK = 64
SLC_TOPK = 8
WINDOW = 512
XATTN_HEADS = 4
XATTN_DIM = 128

LANE = 128
CONV_HALO = 8
VMEM_LIMIT = 56 * 1024 * 1024

_O_U, _O_V, _O_CB, _O_CC, _O_CH = 0, 512, 1024, 1536, 2048
_O_QLAT, _O_KVLAT, _O_KROPE = 2560, 2816, 2944
_O_NQ, _O_NKC, _O_NVC, _O_NKS, _O_NVS, _O_NKW, _O_NVW, _O_NGATE = 2976, 3488, 3616, 3744, 3872, 4000, 4128, 4256
_O_GA, _O_GB, _O_GC, _O_GD = 4280, 5304, 6328, 7352


def _dot(a, b):
    return jnp.dot(a, b, preferred_element_type=F32)


def _dot_t(a, b):
    return lax.dot_general(a, b, (((1,), (1,)), ((), ())), preferred_element_type=F32)


def _ln(y, g, b):
    mu = jnp.mean(y, -1, keepdims=True)
    d = y - mu
    var = jnp.mean(d * d, -1, keepdims=True)
    return d * lax.rsqrt(var + LN_EPS) * g + b


def _rms(x, g):
    return x * lax.rsqrt(jnp.mean(x * x, -1, keepdims=True) + RMS_EPS) * g


def _resident(shape):
    n = len(shape)
    return pl.BlockSpec(shape, lambda *_: (0,) * n)


def _params(sem):
    return pltpu.CompilerParams(dimension_semantics=sem, vmem_limit_bytes=VMEM_LIMIT)


def _ffn_ln_kernel(x_ref, w1_ref, w3_ref, w2_ref, g_ref, b_ref, o_ref, acc_ref):
    j = pl.program_id(1)

    @pl.when(j == 0)
    def _():
        acc_ref[...] = jnp.zeros_like(acc_ref)

    xb = x_ref[...].astype(BF)
    h1 = _dot(xb, w1_ref[...])
    h3 = _dot(xb, w3_ref[...])
    hh = (h1 * jax.nn.sigmoid(h1)) * h3
    acc_ref[...] += _dot(hh.astype(BF), w2_ref[...])

    @pl.when(j == pl.num_programs(1) - 1)
    def _():
        y = ALPHA * x_ref[...] + 0.5 * acc_ref[...]
        o_ref[...] = _ln(y, g_ref[...], b_ref[...])


def _ffn_ln(x, w1, w3, w2, g, b, *, tm, tf):
    n, d = x.shape
    f = w1.shape[1]
    return pl.pallas_call(
        _ffn_ln_kernel,
        grid=(n // tm, f // tf),
        in_specs=[
            pl.BlockSpec((tm, d), lambda i, j: (i, 0)),
            pl.BlockSpec((d, tf), lambda i, j: (0, j)),
            pl.BlockSpec((d, tf), lambda i, j: (0, j)),
            pl.BlockSpec((tf, d), lambda i, j: (j, 0)),
            pl.BlockSpec((1, d), lambda i, j: (0, 0)),
            pl.BlockSpec((1, d), lambda i, j: (0, 0)),
        ],
        out_specs=pl.BlockSpec((tm, d), lambda i, j: (i, 0)),
        out_shape=jax.ShapeDtypeStruct((n, d), F32),
        scratch_shapes=[pltpu.VMEM((tm, d), F32)],
        compiler_params=_params(("parallel", "arbitrary")),
        name="ffn_ln",
    )(x, w1, w3, w2, g, b)


def _ab_kernel(h_ref, w_ref, b_ref, lng_ref, lnb_ref, ws_ref, bst_ref, wga_ref, cw_ref, wcb_ref,
               o_ref, prev_ref, *, tiles_per_seq):
    i = pl.program_id(0)
    tm = h_ref.shape[0]
    hb = h_ref[...].astype(BF)

    def proj(c0, width):
        return _dot(hb, w_ref[:, c0:c0 + width]) + b_ref[:, c0:c0 + width]

    u = proj(0, GMLP_WIDTH)
    v = _ln(proj(512, GMLP_WIDTH), lng_ref[...], lnb_ref[...]).astype(BF)
    row = lax.broadcasted_iota(jnp.int32, (GMLP_CHUNK, GMLP_CHUNK), 0)
    col = lax.broadcasted_iota(jnp.int32, (GMLP_CHUNK, GMLP_CHUNK), 1)
    gd = GMLP_WIDTH // GMLP_GROUPS
    wgs = [jnp.where(row >= col, ws_ref[g], 0.0).astype(BF) for g in range(GMLP_GROUPS)]
    chunks = []
    for c in range(tm // GMLP_CHUNK):
        r0 = c * GMLP_CHUNK
        chunks.append(jnp.concatenate(
            [_dot(wgs[g], v[r0:r0 + GMLP_CHUNK, g * gd:(g + 1) * gd]) + bst_ref[:, g:g + 1]
             for g in range(GMLP_GROUPS)], axis=1))
    s = jnp.concatenate(chunks, axis=0)
    ya = _dot((u * s).astype(BF), wga_ref[...])

    cb = proj(1024, CONV_WIDTH)
    z = proj(1536, CONV_WIDTH) * proj(2048, CONV_WIDTH)

    @pl.when(i % tiles_per_seq == 0)
    def _():
        prev_ref[...] = jnp.zeros_like(prev_ref)

    zext = jnp.concatenate([prev_ref[...], z], axis=0)
    z1 = pltpu.roll(zext, 1, 0)[CONV_HALO:]
    z2 = pltpu.roll(zext, 2, 0)[CONV_HALO:]
    y = cw_ref[0:1, :] * z2 + cw_ref[1:2, :] * z1 + cw_ref[2:3, :] * z
    prev_ref[...] = z[tm - CONV_HALO:, :]
    yb = _dot((cb * y).astype(BF), wcb_ref[...])

    ga = proj(2560, D_MODEL)
    gb = proj(3584, D_MODEL)
    o_ref[...] = jax.nn.sigmoid(ga) * ya + jax.nn.sigmoid(gb) * yb


def _mix_ab(h, w, b, lng, lnb, ws, bst, wga, cw, wcb, *, tm, seq):
    n, d = h.shape
    kern = functools.partial(_ab_kernel, tiles_per_seq=seq // tm)
    return pl.pallas_call(
        kern,
        grid=(n // tm,),
        in_specs=[pl.BlockSpec((tm, d), lambda i: (i, 0))] + [_resident(a.shape) for a in (w, b, lng, lnb, ws, bst, wga, cw, wcb)],
        out_specs=pl.BlockSpec((tm, d), lambda i: (i, 0)),
        out_shape=jax.ShapeDtypeStruct((n, d), F32),
        scratch_shapes=[pltpu.VMEM((CONV_HALO, CONV_WIDTH), F32)],
        compiler_params=_params(("arbitrary",)),
        name="mix_ab",
    )(h, w, b, lng, lnb, ws, bst, wga, cw, wcb)


def _mla_proj_kernel(h_ref, w_ref, b_ref, qg_ref, kvg_ref, wqa_ref, wqb_ref, wk_ref, wv_ref, pk_ref,
                     cq_ref, sq_ref, ck_ref, sk_ref, q_ref, k_ref, v_ref):
    hb = h_ref[...].astype(BF)
    z = _dot(hb, w_ref[...]) + b_ref[...]
    qn = _rms(z[:, 0:256], qg_ref[...]).astype(BF)
    kvn = _rms(z[:, 256:384], kvg_ref[...]).astype(BF)
    cq = jnp.concatenate([cq_ref[...]] * MLA_HEADS, axis=1)
    sq = jnp.concatenate([sq_ref[...]] * MLA_HEADS, axis=1)
    scale = (MLA_NOPE + MLA_ROPE) ** -0.5
    q = (_dot(qn, wqa_ref[...]) * cq + _dot(qn, wqb_ref[...]) * sq) * scale
    q_ref[...] = q.astype(BF)
    kpe = (z[:, 384:512] * ck_ref[...] + z[:, 512:640] * sk_ref[...]).astype(BF)
    k_ref[...] = (_dot(kvn, wk_ref[...]) + _dot(kpe, pk_ref[...])).astype(BF)
    v_ref[...] = _dot(kvn, wv_ref[...]).astype(BF)


def _mla_proj(h, w, b, qg, kvg, wqa, wqb, wk, wv, pk, cq, sq, ck, sk, *, tm, seq):
    n, d = h.shape
    tps = seq // tm
    tab = pl.BlockSpec((tm, LANE), lambda i: (i % tps, 0))
    wide = MLA_HEADS * LANE
    out = jax.ShapeDtypeStruct((n, wide), BF)
    return pl.pallas_call(
        _mla_proj_kernel,
        grid=(n // tm,),
        in_specs=[pl.BlockSpec((tm, d), lambda i: (i, 0))]
        + [_resident(a.shape) for a in (w, b, qg, kvg, wqa, wqb, wk, wv, pk)] + [tab] * 4,
        out_specs=[pl.BlockSpec((tm, wide), lambda i: (i, 0))] * 3,
        out_shape=[out, out, out],
        compiler_params=_params(("parallel",)),
        name="mla_proj",
    )(h, w, b, qg, kvg, wqa, wqb, wk, wv, pk, cq, sq, ck, sk)


def _online_softmax_step(s, v, carry):
    m, l, acc = carry
    m_new = jnp.maximum(m, jnp.max(s, -1, keepdims=True))
    alpha = jnp.exp(m - m_new)
    p = jnp.exp(s - m_new)
    l = alpha * l + jnp.sum(p, -1, keepdims=True)
    acc = alpha * acc + _dot(p.astype(BF), v)
    return m_new, l, acc


def _softmax_init(rows, width):
    return (jnp.full((rows, 1), NEG, F32), jnp.zeros((rows, 1), F32), jnp.zeros((rows, width), F32))


def _flash_kernel(q_ref, k_ref, v_ref, o_ref, *, tq, tk, hp):
    qi = pl.program_id(2)
    q0 = qi * tq
    qs = [q_ref[0, :, h * LANE:(h + 1) * LANE] for h in range(hp)]

    def tile(j, carries, width, masked):
        k0 = pl.multiple_of(j * width, width)
        out = []
        for h in range(hp):
            s = _dot_t(qs[h], k_ref[0, pl.ds(k0, width), h * LANE:(h + 1) * LANE])
            if masked:
                r = lax.broadcasted_iota(jnp.int32, (tq, width), 0) + q0
                c = lax.broadcasted_iota(jnp.int32, (tq, width), 1) + k0
                s = jnp.where(c <= r, s, NEG)
            out.append(_online_softmax_step(s, v_ref[0, pl.ds(k0, width), h * LANE:(h + 1) * LANE], carries[h]))
        return tuple(out)

    n_wide = q0 // tk
    init = tuple(_softmax_init(tq, LANE) for _ in range(hp))
    carries = lax.fori_loop(0, n_wide, lambda j, c: tile(j, c, tk, False), init)
    carries = lax.fori_loop(n_wide * (tk // tq), qi + 1, lambda j, c: tile(j, c, tq, True), carries)
    o_ref[0] = jnp.concatenate([acc / l for (_, l, acc) in carries], axis=1).astype(o_ref.dtype)


def _flash_causal(q, k, v, *, tq, tk, hp):
    bsz, seq, wide = q.shape
    heads = wide // LANE
    assert tk % tq == 0 and seq % tq == 0
    kern = functools.partial(_flash_kernel, tq=tq, tk=tk, hp=hp)
    return pl.pallas_call(
        kern,
        grid=(bsz, heads // hp, seq // tq),
        in_specs=[
            pl.BlockSpec((1, tq, hp * LANE), lambda b, h, i: (b, i, h)),
            pl.BlockSpec((1, seq, hp * LANE), lambda b, h, i: (b, 0, h)),
            pl.BlockSpec((1, seq, hp * LANE), lambda b, h, i: (b, 0, h)),
        ],
        out_specs=pl.BlockSpec((1, tq, hp * LANE), lambda b, h, i: (b, i, h)),
        out_shape=jax.ShapeDtypeStruct((bsz, seq, wide), BF),
        compiler_params=_params(("parallel", "parallel", "arbitrary")),
        name="mla_flash",
    )(q, k, v)


def _nsa_proj_kernel(h_ref, w_ref, b_ref, c_ref, s_ref, q_ref, kc_ref, vc_ref, ks_ref, vs_ref, kw_ref, vw_ref, g_ref):
    hb = h_ref[...].astype(BF)

    def proj(c0, width):
        return _dot(hb, w_ref[:, c0:c0 + width]) + b_ref[:, c0:c0 + width]

    c = c_ref[...]
    s = s_ref[...]
    c8 = jnp.concatenate([c] * NSA_HEADS, axis=1)
    s8 = jnp.concatenate([s] * NSA_HEADS, axis=1)
    c2 = jnp.concatenate([c] * NSA_GROUPS, axis=1)
    s2 = jnp.concatenate([s] * NSA_GROUPS, axis=1)
    q = (proj(0, 1024) * c8 + proj(1024, 1024) * s8) * (NSA_DIM ** -0.5)
    q_ref[...] = q.astype(BF)
    kc_ref[...] = proj(2048, 128)
    vc_ref[...] = proj(2176, 128)
    ks_ref[...] = (proj(2304, 256) * c2 + proj(2560, 256) * s2).astype(BF)
    vs_ref[...] = proj(2816, 256).astype(BF)
    kw_ref[...] = (proj(3072, 256) * c2 + proj(3328, 256) * s2).astype(BF)
    vw_ref[...] = proj(3584, 256).astype(BF)
    g_ref[...] = jax.nn.sigmoid(proj(3840, 256))


def _nsa_proj(h, w, b, cn, sn, *, tm, seq):
    n, d = h.shape
    tps = seq // tm
    tab = pl.BlockSpec((tm, LANE), lambda i: (i % tps, 0))

    def out(width, dt):
        return pl.BlockSpec((tm, width), lambda i: (i, 0)), jax.ShapeDtypeStruct((n, width), dt)

    outs = [out(1024, BF), out(128, F32), out(128, F32), out(256, BF), out(256, BF), out(256, BF), out(256, BF), out(256, F32)]
    return pl.pallas_call(
        _nsa_proj_kernel,
        grid=(n // tm,),
        in_specs=[pl.BlockSpec((tm, d), lambda i: (i, 0)), _resident(w.shape), _resident(b.shape), tab, tab],
        out_specs=[o[0] for o in outs],
        out_shape=[o[1] for o in outs],
        compiler_params=_params(("parallel",)),
        name="nsa_proj",
    )(h, w, b, cn, sn)


def _nsa_cmp_kernel(kc_ref, vc_ref, pe_ref, wk_ref, wkr_ref, wv_ref, c_ref, s_ref, kcmp_ref, vcmp_ref):
    n16 = kc_ref.shape[1]

    def halves(x_ref, pe_lo, pe_hi):
        a = x_ref[0]
        nxt = pltpu.roll(a, n16 - 1, 0)
        return (a + pe_lo).astype(BF), (nxt + pe_hi).astype(BF)

    klo, khi = halves(kc_ref, pe_ref[0:1, :], pe_ref[1:2, :])
    kc = _dot(klo, wk_ref[0]) + _dot(khi, wk_ref[1])
    kcr = _dot(klo, wkr_ref[0]) + _dot(khi, wkr_ref[1])
    kcmp_ref[0] = (kc * c_ref[...] + kcr * s_ref[...]).astype(BF)
    vlo, vhi = halves(vc_ref, pe_ref[2:3, :], pe_ref[3:4, :])
    vcmp_ref[0] = (_dot(vlo, wv_ref[0]) + _dot(vhi, wv_ref[1])).astype(BF)


def _nsa_compress(kc, vc, pe, wk, wkr, wv, cc, sc):
    bsz, n16, wide = kc.shape
    blk = pl.BlockSpec((1, n16, wide), lambda b: (b, 0, 0))
    oblk = pl.BlockSpec((1, n16, NSA_GROUPS * LANE), lambda b: (b, 0, 0))
    osh = jax.ShapeDtypeStruct((bsz, n16, NSA_GROUPS * LANE), BF)
    return pl.pallas_call(
        _nsa_cmp_kernel,
        grid=(bsz,),
        in_specs=[blk, blk] + [_resident(a.shape) for a in (pe, wk, wkr, wv, cc, sc)],
        out_specs=[oblk, oblk],
        out_shape=[osh, osh],
        compiler_params=_params(("parallel",)),
        name="nsa_compress",
    )(kc, vc, pe, wk, wkr, wv, cc, sc)


def _nsa_attn_kernel(q_ref, kcmp_ref, vcmp_ref, ks_ref, vs_ref, kw_ref, vw_ref, g_ref, ov_ref, o_ref, *, tk, top_k):
    qi = pl.program_id(1)
    T = q_ref.shape[1]
    R = NSA_HPG * T
    G = NSA_GROUPS
    q0 = qi * T
    qpos = lax.broadcasted_iota(jnp.int32, (T, 1), 0) + q0
    ncp = kcmp_ref.shape[1]
    nb = ov_ref.shape[0]
    ov_t = ov_ref[...]

    def rep(x):
        return jnp.concatenate([x] * NSA_HPG, axis=0)

    def lanes(g):
        return slice(g * LANE, (g + 1) * LANE)

    q4 = [jnp.concatenate([q_ref[0, :, (g * NSA_HPG + h) * LANE:(g * NSA_HPG + h + 1) * LANE]
                           for h in range(NSA_HPG)], axis=0) for g in range(G)]

    cmp_end = lax.broadcasted_iota(jnp.int32, (1, ncp), 1) * CMP_STRIDE + (CMP_BLOCK - 1)
    valid = rep(jnp.where(cmp_end <= qpos, 1.0, 0.0)) > 0.5
    jr = lax.broadcasted_iota(jnp.int32, (nb, 1), 0)
    jrf = jr.astype(F32)
    jq = lax.shift_right_logical(lax.broadcasted_iota(jnp.int32, (1, T), 1) + q0, 6)
    forced = (jr == 0) | (jr == jq) | (jr == jq - 1)
    eye_t = jnp.where(lax.broadcasted_iota(jnp.int32, (T, T), 0) == lax.broadcasted_iota(jnp.int32, (T, T), 1),
                      1.0, 0.0).astype(BF)
    o_cmp, sel_b = [], []
    for g in range(G):
        sm = jnp.where(valid, _dot_t(q4[g], kcmp_ref[0, :, lanes(g)]), NEG)
        e = jnp.exp(sm - jnp.max(sm, -1, keepdims=True))
        p = jnp.where(valid, e / jnp.sum(e, -1, keepdims=True), 0.0)
        o_cmp.append(_dot(p.astype(BF), vcmp_ref[0, :, lanes(g)]))
        psum = p[0:T] + p[T:2 * T] + p[2 * T:3 * T] + p[3 * T:4 * T]
        hi = psum.astype(BF)
        r1 = psum - hi.astype(F32)
        mid = r1.astype(BF)
        lo = (r1 - mid.astype(F32)).astype(BF)
        imp = _dot_t(ov_t, hi) + _dot_t(ov_t, mid) + _dot_t(ov_t, lo)
        imp = jnp.where(forced, 1e9, imp)
        imp = jnp.where(jr <= jq, imp, -1.0)
        work = imp
        sel = jnp.zeros_like(imp)
        for _ in range(top_k):
            mx = jnp.max(work, 0, keepdims=True)
            idx = jnp.min(jnp.where(work == mx, jrf, float(nb)), 0, keepdims=True)
            pick = jrf == idx
            sel = jnp.where(pick, 1.0, sel)
            work = jnp.where(pick, -2.0, work)
        sel_t = jnp.where(imp >= 0.0, sel, 0.0).astype(BF)
        sel_b.append(_dot_t(eye_t, sel_t).astype(BF))

    def slc_tile(j, carries):
        k0 = pl.multiple_of(j * tk, tk)
        blk_of_key = lax.shift_right_logical(lax.broadcasted_iota(jnp.int32, (nb, tk), 1) + k0, 6)
        expand = jnp.where(blk_of_key == lax.broadcasted_iota(jnp.int32, (nb, tk), 0), 1.0, 0.0).astype(BF)
        causal = (lax.broadcasted_iota(jnp.int32, (1, tk), 1) + k0) <= qpos
        out = []
        for g in range(G):
            sc = _dot_t(q4[g], ks_ref[0, pl.ds(k0, tk), lanes(g)])
            bias = jnp.where((_dot(sel_b[g], expand) > 0.5) & causal, 0.0, NEG)
            out.append(_online_softmax_step(sc + rep(bias), vs_ref[0, pl.ds(k0, tk), lanes(g)], carries[g]))
        return tuple(out)

    n_tiles = (q0 + T + tk - 1) // tk
    carries = lax.fori_loop(0, n_tiles, slc_tile, tuple(_softmax_init(R, LANE) for _ in range(G)))
    o_slc = [acc / l for (_, l, acc) in carries]

    wk = WINDOW + T
    w0 = pl.multiple_of(jnp.maximum(q0 - WINDOW, 0), T)
    dist = qpos - (lax.broadcasted_iota(jnp.int32, (1, wk), 1) + w0)
    wbias = rep(jnp.where((dist >= 0) & (dist < WINDOW), 0.0, NEG))
    gates = g_ref[0]
    outs = []
    for g in range(G):
        sc = _dot_t(q4[g], kw_ref[0, pl.ds(w0, wk), lanes(g)]) + wbias
        e = jnp.exp(sc - jnp.max(sc, -1, keepdims=True))
        o_win = _dot(e.astype(BF), vw_ref[0, pl.ds(w0, wk), lanes(g)]) / jnp.sum(e, -1, keepdims=True)
        for h in range(NSA_HPG):
            rows = slice(h * T, (h + 1) * T)
            c = g * LANE + 3 * h
            outs.append(gates[:, c:c + 1] * o_cmp[g][rows] + gates[:, c + 1:c + 2] * o_slc[g][rows]
                        + gates[:, c + 2:c + 3] * o_win[rows])
    o_ref[0] = jnp.concatenate(outs, axis=1).astype(o_ref.dtype)


def _nsa_attention(q, kcmp, vcmp, ks, vs, kw, vw, gates, ov, *, tq, tk, top_k):
    bsz, seq, wide = q.shape
    n16 = kcmp.shape[1]
    gw = NSA_GROUPS * LANE
    kern = functools.partial(_nsa_attn_kernel, tk=tk, top_k=top_k)
    cblk = pl.BlockSpec((1, n16, gw), lambda b, i: (b, 0, 0))
    sblk = pl.BlockSpec((1, seq, gw), lambda b, i: (b, 0, 0))
    return pl.pallas_call(
        kern,
        grid=(bsz, seq // tq),
        in_specs=[pl.BlockSpec((1, tq, wide), lambda b, i: (b, i, 0)), cblk, cblk, sblk, sblk, sblk, sblk,
                  pl.BlockSpec((1, tq, gw), lambda b, i: (b, i, 0)), _resident(ov.shape)],
        out_specs=pl.BlockSpec((1, tq, wide), lambda b, i: (b, i, 0)),
        out_shape=jax.ShapeDtypeStruct((bsz, seq, wide), BF),
        compiler_params=_params(("parallel", "arbitrary")),
        name="nsa_attn",
    )(q, kcmp, vcmp, ks, vs, kw, vw, gates, ov)


def _merge_kernel(x_ref, ab_ref, oc_ref, od_ref, wg_ref, bg_ref, wc_ref, wd_ref, wo_ref, g_ref, b_ref, o_ref):
    x = x_ref[...]
    gates = _dot(x.astype(BF), wg_ref[...]) + bg_ref[...]
    yc = _dot(oc_ref[...], wc_ref[...])
    yd = _dot(od_ref[...], wd_ref[...])
    merged = ab_ref[...] + jax.nn.sigmoid(gates[:, :D_MODEL]) * yc + jax.nn.sigmoid(gates[:, D_MODEL:]) * yd
    mix = _dot(merged.astype(BF), wo_ref[...])
    o_ref[...] = _ln(ALPHA * x + mix, g_ref[...], b_ref[...])


def _merge_ln(x, ab, oc, od, wg, bg, wc, wd, wo, g, b, *, tm):
    n, d = x.shape
    row = pl.BlockSpec((tm, d), lambda i: (i, 0))
    return pl.pallas_call(
        _merge_kernel,
        grid=(n // tm,),
        in_specs=[row, row, row, row] + [_resident(a.shape) for a in (wg, bg, wc, wd, wo, g, b)],
        out_specs=row,
        out_shape=jax.ShapeDtypeStruct((n, d), F32),
        compiler_params=_params(("parallel",)),
        name="merge_ln",
    )(x, ab, oc, od, wg, bg, wc, wd, wo, g, b)


def _linear_kernel(x_ref, w_ref, o_ref):
    o_ref[...] = _dot(x_ref[...].astype(BF), w_ref[...]).astype(o_ref.dtype)


def _linear(x, w, *, tm, dtype):
    n, d = x.shape
    return pl.pallas_call(
        _linear_kernel,
        grid=(n // tm,),
        in_specs=[pl.BlockSpec((tm, d), lambda i: (i, 0)), _resident(w.shape)],
        out_specs=pl.BlockSpec((tm, w.shape[1]), lambda i: (i, 0)),
        out_shape=jax.ShapeDtypeStruct((n, w.shape[1]), dtype),
        compiler_params=_params(("parallel",)),
        name="mem_kv",
    )(x, w)


def _xattn_kernel(x_ref, k_ref, v_ref, wq_ref, wo_ref, g_ref, b_ref, o_ref):
    x = x_ref[...]
    q = _dot(x.astype(BF), wq_ref[...]).astype(BF)
    k = k_ref[0]
    v = v_ref[0]
    heads = []
    for h in range(XATTN_HEADS):
        sl = slice(h * XATTN_DIM, (h + 1) * XATTN_DIM)
        s = _dot_t(q[:, sl], k[:, sl]) * (XATTN_DIM ** -0.5)
        e = jnp.exp(s - jnp.max(s, -1, keepdims=True))
        p = e / jnp.sum(e, -1, keepdims=True)
        heads.append(_dot(p.astype(BF), v[:, sl]))
    o = jnp.concatenate(heads, axis=1).astype(BF)
    o_ref[...] = _ln(ALPHA * x + _dot(o, wo_ref[...]), g_ref[...], b_ref[...])


def _xattn_ln(x, kv, wq, wo, g, b, *, tm, seq):
    n, d = x.shape
    tps = seq // tm
    mlen = kv.shape[1]
    hd = XATTN_HEADS * XATTN_DIM
    return pl.pallas_call(
        _xattn_kernel,
        grid=(n // tm,),
        in_specs=[pl.BlockSpec((tm, d), lambda i: (i, 0)),
                  pl.BlockSpec((1, mlen, hd), lambda i: (i // tps, 0, 0)),
                  pl.BlockSpec((1, mlen, hd), lambda i: (i // tps, 0, 1))]
        + [_resident(a.shape) for a in (wq, wo, g, b)],
        out_specs=pl.BlockSpec((tm, d), lambda i: (i, 0)),
        out_shape=jax.ShapeDtypeStruct((n, d), F32),
        compiler_params=_params(("parallel",)),
        name="xattn_ln",
    )(x, kv, kv, wq, wo, g, b)


def _rope_tab(pos, dim):
    inv = ROPE_THETA ** (-(jnp.arange(0, dim, 2, dtype=F32) / dim))
    ang = pos[:, None] * inv[None, :]
    return jnp.cos(ang), jnp.sin(ang)


def _rot_cols(w, half):
    return jnp.concatenate([-w[..., half:2 * half], w[..., :half]], axis=-1)


def _pad_slots(w, n_slots, width):
    lead = w.shape[:-1]
    w = w.reshape(lead + (n_slots, width))
    w = jnp.pad(w, [(0, 0)] * len(lead) + [(0, 0), (0, LANE - width)])
    return w.reshape(lead + (n_slots * LANE,))


def _rot_slots(w, n_slots, width, half):
    lead = w.shape[:-1]
    w = w.reshape(lead + (n_slots, width))
    return _rot_cols(w, half).reshape(lead + (n_slots * width,))


def _pad_rows(w, n_slots, width):
    d = w.shape[-1]
    w = w.reshape(n_slots, width, d)
    w = jnp.pad(w, [(0, 0), (0, LANE - width), (0, 0)])
    return w.reshape(n_slots * LANE, d)


def _layer_params(l, p):
    w_in, b_in = p["w_in"][l], p["b_in"][l]

    def cols(o, wd):
        return w_in[:, o:o + wd], b_in[o:o + wd]

    out = {}
    out["w_ab"] = jnp.concatenate([w_in[:, 0:2560], w_in[:, _O_GA:_O_GC]], axis=1).astype(BF)
    out["b_ab"] = jnp.concatenate([b_in[0:2560], b_in[_O_GA:_O_GC]])[None, :]
    wkr, bkr = cols(_O_KROPE, MLA_ROPE)
    half = MLA_ROPE // 2
    padk = lambda a: jnp.pad(a, [(0, 0)] * (a.ndim - 1) + [(0, LANE - MLA_ROPE)])
    out["w_c"] = jnp.concatenate([w_in[:, _O_QLAT:_O_KROPE], padk(wkr), padk(_rot_cols(wkr, half))], axis=1).astype(BF)
    out["b_c"] = jnp.concatenate([b_in[_O_QLAT:_O_KROPE], padk(bkr), padk(_rot_cols(bkr, half))])[None, :]
    wuq = p["mla_wuq"][l].reshape(MLA_Q_RANK, MLA_HEADS, MLA_NOPE + MLA_ROPE)
    rope_rot = _rot_cols(wuq[..., MLA_NOPE:], half)
    wqa = jnp.pad(wuq, [(0, 0), (0, 0), (0, LANE - MLA_NOPE - MLA_ROPE)])
    wqb = jnp.pad(rope_rot, [(0, 0), (0, 0), (MLA_NOPE, LANE - MLA_NOPE - MLA_ROPE)])
    out["wqa"] = wqa.reshape(MLA_Q_RANK, MLA_HEADS * LANE).astype(BF)
    out["wqb"] = wqb.reshape(MLA_Q_RANK, MLA_HEADS * LANE).astype(BF)
    wukv = p["mla_wukv"][l].reshape(MLA_KV_RANK, MLA_HEADS, MLA_NOPE + MLA_V)
    out["wk_c"] = jnp.pad(wukv[..., :MLA_NOPE], [(0, 0), (0, 0), (0, LANE - MLA_NOPE)]).reshape(MLA_KV_RANK, -1).astype(BF)
    out["wv_c"] = jnp.pad(wukv[..., MLA_NOPE:], [(0, 0), (0, 0), (0, LANE - MLA_V)]).reshape(MLA_KV_RANK, -1).astype(BF)
    wq, bq = cols(_O_NQ, NSA_HEADS * NSA_DIM)
    hd = NSA_DIM // 2
    pieces_w, pieces_b = [], []

    def add(w, b, slots, roped):
        pieces_w.append(_pad_slots(w, slots, NSA_DIM))
        pieces_b.append(_pad_slots(b, slots, NSA_DIM))
        if roped:
            pieces_w.append(_pad_slots(_rot_slots(w, slots, NSA_DIM, hd), slots, NSA_DIM))
            pieces_b.append(_pad_slots(_rot_slots(b, slots, NSA_DIM, hd), slots, NSA_DIM))

    add(wq, bq, NSA_HEADS, True)
    wkc, bkc = cols(_O_NKC, 128)
    wvc, bvc = cols(_O_NVC, 128)
    pieces_w += [wkc, wvc]
    pieces_b += [bkc, bvc]
    add(*cols(_O_NKS, 128), NSA_GROUPS, True)
    add(*cols(_O_NVS, 128), NSA_GROUPS, False)
    add(*cols(_O_NKW, 128), NSA_GROUPS, True)
    add(*cols(_O_NVW, 128), NSA_GROUPS, False)
    wg, bg = cols(_O_NGATE, NSA_HEADS * 3)
    pieces_w.append(_pad_slots(wg, NSA_GROUPS, NSA_HPG * 3))
    pieces_b.append(_pad_slots(bg, NSA_GROUPS, NSA_HPG * 3))
    out["w_d"] = jnp.concatenate(pieces_w, axis=1).astype(BF)
    out["b_d"] = jnp.concatenate(pieces_b)[None, :]

    def cmp_weights(w):
        eye = jnp.eye(NSA_GROUPS, dtype=F32)
        wp = jnp.pad(w, [(0, 0), (0, 0), (0, LANE - NSA_DIM)])
        full = jnp.einsum("lde,gh->lgdhe", wp, eye).reshape(CMP_BLOCK, NSA_GROUPS * NSA_DIM, NSA_GROUPS * LANE)
        return full.reshape(2, CMP_STRIDE * NSA_GROUPS * NSA_DIM, NSA_GROUPS * LANE).astype(BF)

    wck = p["nsa_wcmp_k"][l]
    out["wcmp_k"] = cmp_weights(wck)
    out["wcmp_kr"] = cmp_weights(_rot_cols(wck, hd))
    out["wcmp_v"] = cmp_weights(p["nsa_wcmp_v"][l])

    def pe_rows(pe):
        t = jnp.broadcast_to(pe[:, None, :], (CMP_BLOCK, NSA_GROUPS, NSA_DIM))
        return t.reshape(2, CMP_STRIDE * NSA_GROUPS * NSA_DIM)

    out["pe"] = jnp.concatenate([pe_rows(p["nsa_pe_k"][l]), pe_rows(p["nsa_pe_v"][l])], axis=0)
    out["w_g"] = w_in[:, _O_GC:].astype(BF)
    out["b_g"] = b_in[_O_GC:][None, :]
    out["wout_c"] = _pad_rows(p["mla_wout"][l], MLA_HEADS, MLA_V).astype(BF)
    out["wout_d"] = _pad_rows(p["nsa_wout"][l], NSA_HEADS, NSA_DIM).astype(BF)
    return out


def _tables(seq):
    pos = jnp.arange(seq, dtype=F32)
    c16, s16 = _rope_tab(pos, MLA_ROPE)
    one = jnp.ones((seq, MLA_NOPE), F32)
    zero = jnp.zeros((seq, MLA_NOPE), F32)
    tail = LANE - MLA_NOPE - MLA_ROPE
    cq = jnp.concatenate([one, c16, c16, jnp.ones((seq, tail), F32)], axis=1)
    sq = jnp.concatenate([zero, s16, s16, jnp.zeros((seq, tail), F32)], axis=1)
    ck = jnp.pad(jnp.concatenate([c16, c16], axis=1), [(0, 0), (0, LANE - MLA_ROPE)])
    sk = jnp.pad(jnp.concatenate([s16, s16], axis=1), [(0, 0), (0, LANE - MLA_ROPE)])
    c32, s32 = _rope_tab(pos, NSA_DIM)
    cn = jnp.pad(jnp.concatenate([c32, c32], axis=1), [(0, 0), (0, LANE - NSA_DIM)])
    sn = jnp.pad(jnp.concatenate([s32, s32], axis=1), [(0, 0), (0, LANE - NSA_DIM)])
    n16 = seq // CMP_STRIDE
    cend = (jnp.arange(n16) * CMP_STRIDE + CMP_BLOCK - 1).astype(F32)
    cc32, cs32 = _rope_tab(cend, NSA_DIM)
    ccg = jnp.pad(jnp.concatenate([cc32, cc32], axis=1), [(0, 0), (0, LANE - NSA_DIM)])
    csg = jnp.pad(jnp.concatenate([cs32, cs32], axis=1), [(0, 0), (0, LANE - NSA_DIM)])
    cc = jnp.concatenate([ccg] * NSA_GROUPS, axis=1)
    cs = jnp.concatenate([csg] * NSA_GROUPS, axis=1)
    n_cmp = (seq - CMP_BLOCK) // CMP_STRIDE + 1
    n_slc = seq // SLC_BLOCK
    cstart = jnp.arange(n16) * CMP_STRIDE
    sstart = jnp.arange(n_slc) * SLC_BLOCK
    ovl = (jnp.minimum(cstart[None, :] + CMP_BLOCK, sstart[:, None] + SLC_BLOCK)
           - jnp.maximum(cstart[None, :], sstart[:, None]))
    ovl = jnp.clip(ovl, 0).astype(F32) / CMP_BLOCK
    ovl = jnp.where(jnp.arange(n16)[None, :] < n_cmp, ovl, 0.0).astype(BF)
    pk = jnp.zeros((LANE, MLA_HEADS, LANE), F32)
    pk = pk.at[jnp.arange(MLA_ROPE), :, MLA_NOPE + jnp.arange(MLA_ROPE)].set(1.0)
    pk = pk.reshape(LANE, MLA_HEADS * LANE).astype(BF)
    return dict(cq=cq, sq=sq, ck=ck, sk=sk, cn=cn, sn=sn, cc=cc, cs=cs, ovl=ovl, pk=pk)


def kernel(x, mem, ffn1_w1, ffn1_w3, ffn1_w2, ln1_g, ln1_b, w_in, b_in, gmlp_ln_g, gmlp_ln_b, gmlp_ws, gmlp_bs, gmlp_wout, conv_w, conv_wout, mla_qnorm_g, mla_kvnorm_g, mla_wuq, mla_wukv, mla_wout, nsa_pe_k, nsa_pe_v, nsa_wcmp_k, nsa_wcmp_v, nsa_wout, w_o, ln2_g, ln2_b, xattn_wq, xattn_wk, xattn_wv, xattn_wo, ln3_g, ln3_b, ffn2_w1, ffn2_w3, ffn2_w2, ln4_g, ln4_b):
    bsz, seq, d = x.shape
    mlen = mem.shape[1]
    n = bsz * seq
    assert d == D_MODEL and seq % 256 == 0 and seq >= WINDOW + 128
    p = dict(w_in=w_in, b_in=b_in, mla_wuq=mla_wuq, mla_wukv=mla_wukv, mla_wout=mla_wout,
             nsa_pe_k=nsa_pe_k, nsa_pe_v=nsa_pe_v, nsa_wcmp_k=nsa_wcmp_k, nsa_wcmp_v=nsa_wcmp_v, nsa_wout=nsa_wout)
    tb = _tables(seq)
    tm = 256
    tm_ffn = 512
    tf = D_FF // 2
    n16 = seq // CMP_STRIDE
    top_k = min(SLC_TOPK, seq // SLC_BLOCK)
    row = lambda a: a[None, :]

    h = x.reshape(n, d)
    mem2 = mem.reshape(bsz * mlen, d)
    for l in range(DEPTH):
        lp = _layer_params(l, p)
        h = _ffn_ln(h, ffn1_w1[l].astype(BF), ffn1_w3[l].astype(BF), ffn1_w2[l].astype(BF),
                    row(ln1_g[l]), row(ln1_b[l]), tm=tm_ffn, tf=tf)
        ab = _mix_ab(h, lp["w_ab"], lp["b_ab"], row(gmlp_ln_g[l]), row(gmlp_ln_b[l]), gmlp_ws[l], gmlp_bs[l].T,
                     gmlp_wout[l].astype(BF), conv_w[l], conv_wout[l].astype(BF), tm=tm, seq=seq)
        qc, kc_, vc_ = _mla_proj(h, lp["w_c"], lp["b_c"], row(mla_qnorm_g[l]), row(mla_kvnorm_g[l]),
                                 lp["wqa"], lp["wqb"], lp["wk_c"], lp["wv_c"], tb["pk"],
                                 tb["cq"], tb["sq"], tb["ck"], tb["sk"], tm=tm, seq=seq)
        wide = MLA_HEADS * LANE
        oc = _flash_causal(qc.reshape(bsz, seq, wide), kc_.reshape(bsz, seq, wide), vc_.reshape(bsz, seq, wide), tq=512, tk=1024, hp=2)
        qn, nkc, nvc, nks, nvs, nkw, nvw, gates = _nsa_proj(h, lp["w_d"], lp["b_d"], tb["cn"], tb["sn"], tm=tm, seq=seq)
        kcmp, vcmp = _nsa_compress(nkc.reshape(bsz, n16, CMP_STRIDE * 128), nvc.reshape(bsz, n16, CMP_STRIDE * 128),
                                   lp["pe"], lp["wcmp_k"], lp["wcmp_kr"], lp["wcmp_v"], tb["cc"], tb["cs"])
        gw = NSA_GROUPS * LANE
        od = _nsa_attention(qn.reshape(bsz, seq, NSA_HEADS * LANE), kcmp, vcmp,
                            nks.reshape(bsz, seq, gw), nvs.reshape(bsz, seq, gw),
                            nkw.reshape(bsz, seq, gw), nvw.reshape(bsz, seq, gw),
                            gates.reshape(bsz, seq, gw), tb["ovl"], tq=128, tk=512, top_k=top_k)
        h = _merge_ln(h, ab, oc.reshape(n, wide), od.reshape(n, NSA_HEADS * LANE), lp["w_g"], lp["b_g"],
                      lp["wout_c"], lp["wout_d"], w_o[l].astype(BF), row(ln2_g[l]), row(ln2_b[l]), tm=tm)
        kv = _linear(mem2, jnp.concatenate([xattn_wk[l], xattn_wv[l]], axis=1).astype(BF), tm=min(256, bsz * mlen), dtype=BF)
        h = _xattn_ln(h, kv.reshape(bsz, mlen, 2 * XATTN_HEADS * XATTN_DIM), xattn_wq[l].astype(BF),
                      xattn_wo[l].astype(BF), row(ln3_g[l]), row(ln3_b[l]), tm=tm, seq=seq)
        h = _ffn_ln(h, ffn2_w1[l].astype(BF), ffn2_w3[l].astype(BF), ffn2_w2[l].astype(BF),
                    row(ln4_g[l]), row(ln4_b[l]), tm=tm_ffn, tf=tf)
    return h.reshape(bsz, seq, d)
```

```python
import functools

import jax
import jax.numpy as jnp
from jax import lax
from jax.experimental import pallas as pl
from jax.experimental.pallas import tpu as pltpu

BF = jnp.bfloat16
F32 = jnp.float32

D_MODEL = 1024
D_FF = 2816
LN_EPS = 1e-5
RMS_EPS = 1e-6
ROPE_THETA = 10000.0
DEPTH = 2
ALPHA = (2 * DEPTH) ** 0.25
NEG = -1e30
MASK_BIG = 2.0 ** 100

GMLP_CHUNK = 128
GMLP_GROUPS = 4
GMLP_WIDTH = 512
CONV_WIDTH = 512
CONV_K = 3
MLA_HEADS = 8
MLA_Q_RANK = 256
MLA_KV_RANK = 128
MLA_NOPE = 64
MLA_ROPE = 32
MLA_V = 64
NSA_HEADS = 8
NSA_GROUPS = 2
NSA_HPG = 4
NSA_DIM = 64
CMP_BLOCK = 32
CMP_STRIDE = 16
SLC_BLOCK = 64
SLC_TOPK = 8
WINDOW = 512
XATTN_HEADS = 4
XATTN_DIM = 128

LANE = 128
CONV_HALO = 8
VMEM_LIMIT = 56 * 1024 * 1024

_O_U, _O_V, _O_CB, _O_CC, _O_CH = 0, 512, 1024, 1536, 2048
_O_QLAT, _O_KVLAT, _O_KROPE = 2560, 2816, 2944
_O_NQ, _O_NKC, _O_NVC, _O_NKS, _O_NVS, _O_NKW, _O_NVW, _O_NGATE = 2976, 3488, 3616, 3744, 3872, 4000, 4128, 4256
_O_GA, _O_GB, _O_GC, _O_GD = 4280, 5304, 6328, 7352


def _dot(a, b):
    return jnp.dot(a, b, preferred_element_type=F32)


def _dot_t(a, b):
    return lax.dot_general(a, b, (((1,), (1,)), ((), ())), preferred_element_type=F32)


def _ln(y, g, b):
    mu = jnp.mean(y, -1, keepdims=True)
    d = y - mu
    var = jnp.mean(d * d, -1, keepdims=True)
    return d * lax.rsqrt(var + LN_EPS) * g + b


def _rms(x, g):
    return x * lax.rsqrt(jnp.mean(x * x, -1, keepdims=True) + RMS_EPS) * g


def _resident(shape):
    n = len(shape)
    return pl.BlockSpec(shape, lambda *_: (0,) * n, pipeline_mode=pl.Buffered(1))


def _params(sem):
    return pltpu.CompilerParams(dimension_semantics=sem, vmem_limit_bytes=VMEM_LIMIT)


def _ffn_ln_kernel(x_ref, w1_ref, w3_ref, w2_ref, g_ref, b_ref, o_ref, *, tf):
    x = x_ref[...]
    xb = x.astype(BF)
    acc = None
    for c in range(w1_ref.shape[1] // tf):
        cols = slice(c * tf, (c + 1) * tf)
        h1 = _dot(xb, w1_ref[:, cols])
        h3 = _dot(xb, w3_ref[:, cols])
        hh = (h1 * jax.nn.sigmoid(h1)) * h3
        part = _dot(hh.astype(BF), w2_ref[cols, :])
        acc = part if acc is None else acc + part
    o_ref[...] = _ln(ALPHA * x + 0.5 * acc, g_ref[...], b_ref[...])


def _ffn_ln(x, w1, w3, w2, g, b, *, tm, tf):
    n, d = x.shape
    return pl.pallas_call(
        functools.partial(_ffn_ln_kernel, tf=tf),
        grid=(n // tm,),
        in_specs=[pl.BlockSpec((tm, d), lambda i: (i, 0))] + [_resident(a.shape) for a in (w1, w3, w2, g, b)],
        out_specs=pl.BlockSpec((tm, d), lambda i: (i, 0)),
        out_shape=jax.ShapeDtypeStruct((n, d), F32),
        compiler_params=_params(("parallel",)),
        name="ffn_ln",
    )(x, w1, w3, w2, g, b)


def _ab_kernel(h_ref, w_ref, b_ref, lng_ref, lnb_ref, ws_ref, bst_ref, wga_ref, cw_ref, wcb_ref,
               o_ref, prev_ref, *, tiles_per_seq):
    i = pl.program_id(0)
    tm = h_ref.shape[0]
    hb = h_ref[...].astype(BF)

    def proj(c0, width):
        return _dot(hb, w_ref[:, c0:c0 + width]) + b_ref[:, c0:c0 + width]

    u = proj(0, GMLP_WIDTH)
    v = _ln(proj(512, GMLP_WIDTH), lng_ref[...], lnb_ref[...]).astype(BF)
    row = lax.broadcasted_iota(jnp.int32, (GMLP_CHUNK, GMLP_CHUNK), 0)
    col = lax.broadcasted_iota(jnp.int32, (GMLP_CHUNK, GMLP_CHUNK), 1)
    gd = GMLP_WIDTH // GMLP_GROUPS
    wgs = [jnp.where(row >= col, ws_ref[g], 0.0).astype(BF) for g in range(GMLP_GROUPS)]
    chunks = []
    for c in range(tm // GMLP_CHUNK):
        r0 = c * GMLP_CHUNK
        chunks.append(jnp.concatenate(
            [_dot(wgs[g], v[r0:r0 + GMLP_CHUNK, g * gd:(g + 1) * gd]) + bst_ref[:, g:g + 1]
             for g in range(GMLP_GROUPS)], axis=1))
    s = jnp.concatenate(chunks, axis=0)
    ya = _dot((u * s).astype(BF), wga_ref[...])

    cb = proj(1024, CONV_WIDTH)
    z = proj(1536, CONV_WIDTH) * proj(2048, CONV_WIDTH)

    @pl.when(i % tiles_per_seq == 0)
    def _():
        prev_ref[...] = jnp.zeros_like(prev_ref)

    zext = jnp.concatenate([prev_ref[...], z], axis=0)
    z1 = pltpu.roll(zext, 1, 0)[CONV_HALO:]
    z2 = pltpu.roll(zext, 2, 0)[CONV_HALO:]
    y = cw_ref[0:1, :] * z2 + cw_ref[1:2, :] * z1 + cw_ref[2:3, :] * z
    prev_ref[...] = z[tm - CONV_HALO:, :]
    yb = _dot((cb * y).astype(BF), wcb_ref[...])

    ga = proj(2560, D_MODEL)
    gb = proj(3584, D_MODEL)
    o_ref[...] = jax.nn.sigmoid(ga) * ya + jax.nn.sigmoid(gb) * yb


def _mix_ab(h, w, b, lng, lnb, ws, bst, wga, cw, wcb, *, tm, seq):
    n, d = h.shape
    kern = functools.partial(_ab_kernel, tiles_per_seq=seq // tm)
    return pl.pallas_call(
        kern,
        grid=(n // tm,),
        in_specs=[pl.BlockSpec((tm, d), lambda i: (i, 0))] + [_resident(a.shape) for a in (w, b, lng, lnb, ws, bst, wga, cw, wcb)],
        out_specs=pl.BlockSpec((tm, d), lambda i: (i, 0)),
        out_shape=jax.ShapeDtypeStruct((n, d), F32),
        scratch_shapes=[pltpu.VMEM((CONV_HALO, CONV_WIDTH), F32)],
        compiler_params=_params(("arbitrary",)),
        name="mix_ab",
    )(h, w, b, lng, lnb, ws, bst, wga, cw, wcb)


def _mla_proj_kernel(h_ref, w_ref, b_ref, qg_ref, kvg_ref, wqa_ref, wqb_ref, wk_ref, wv_ref, pk_ref,
                     cq_ref, sq_ref, ck_ref, sk_ref, q_ref, k_ref, v_ref):
    hb = h_ref[...].astype(BF)
    z = _dot(hb, w_ref[...]) + b_ref[...]
    qn = _rms(z[:, 0:256], qg_ref[...]).astype(BF)
    kvn = _rms(z[:, 256:384], kvg_ref[...]).astype(BF)
    cq = jnp.concatenate([cq_ref[...]] * MLA_HEADS, axis=1)
    sq = jnp.concatenate([sq_ref[...]] * MLA_HEADS, axis=1)
    scale = (MLA_NOPE + MLA_ROPE) ** -0.5
    q = (_dot(qn, wqa_ref[...]) * cq + _dot(qn, wqb_ref[...]) * sq) * scale
    q_ref[...] = q.astype(BF)
    kpe = (z[:, 384:512] * ck_ref[...] + z[:, 512:640] * sk_ref[...]).astype(BF)
    k_ref[...] = (_dot(kvn, wk_ref[...]) + _dot(kpe, pk_ref[...])).astype(BF)
    v_ref[...] = _dot(kvn, wv_ref[...]).astype(BF)


def _mla_proj(h, w, b, qg, kvg, wqa, wqb, wk, wv, pk, cq, sq, ck, sk, *, tm, seq):
    n, d = h.shape
    tps = seq // tm
    tab = pl.BlockSpec((tm, LANE), lambda i: (i % tps, 0))
    wide = MLA_HEADS * LANE
    out = jax.ShapeDtypeStruct((n, wide), BF)
    return pl.pallas_call(
        _mla_proj_kernel,
        grid=(n // tm,),
        in_specs=[pl.BlockSpec((tm, d), lambda i: (i, 0))]
        + [_resident(a.shape) for a in (w, b, qg, kvg, wqa, wqb, wk, wv, pk)] + [tab] * 4,
        out_specs=[pl.BlockSpec((tm, wide), lambda i: (i, 0))] * 3,
        out_shape=[out, out, out],
        compiler_params=_params(("parallel",)),
        name="mla_proj",
    )(h, w, b, qg, kvg, wqa, wqb, wk, wv, pk, cq, sq, ck, sk)


def _online_softmax_step(s, v, carry):
    m, l, acc = carry
    m_new = jnp.maximum(m, jnp.max(s, -1, keepdims=True))
    alpha = jnp.exp(m - m_new)
    p = jnp.exp(s - m_new)
    l = alpha * l + jnp.sum(p, -1, keepdims=True)
    acc = alpha * acc + _dot(p.astype(BF), v)
    return m_new, l, acc


def _softmax_init(rows, width):
    return (jnp.full((rows, 1), NEG, F32), jnp.zeros((rows, 1), F32), jnp.zeros((rows, width), F32))


def _flash_kernel(q_ref, k_ref, v_ref, o_ref, *, tq, tk, hp):
    qi = pl.program_id(2)
    q0 = qi * tq
    qs = [q_ref[0, :, h * LANE:(h + 1) * LANE] for h in range(hp)]

    def tile(j, carries, width, masked):
        k0 = pl.multiple_of(j * width, width)
        out = []
        for h in range(hp):
            s = _dot_t(qs[h], k_ref[0, pl.ds(k0, width), h * LANE:(h + 1) * LANE])
            if masked:
                r = lax.broadcasted_iota(jnp.int32, (tq, width), 0) + q0
                c = lax.broadcasted_iota(jnp.int32, (tq, width), 1) + k0
                s = jnp.where(c <= r, s, NEG)
            out.append(_online_softmax_step(s, v_ref[0, pl.ds(k0, width), h * LANE:(h + 1) * LANE], carries[h]))
        return tuple(out)

    n_wide = q0 // tk
    init = tuple(_softmax_init(tq, LANE) for _ in range(hp))
    carries = lax.fori_loop(0, n_wide, lambda j, c: tile(j, c, tk, False), init)
    carries = lax.fori_loop(n_wide * (tk // tq), qi + 1, lambda j, c: tile(j, c, tq, True), carries)
    o_ref[0] = jnp.concatenate([acc / l for (_, l, acc) in carries], axis=1).astype(o_ref.dtype)


def _flash_causal(q, k, v, *, tq, tk, hp):
    bsz, seq, wide = q.shape
    heads = wide // LANE
    assert tk % tq == 0 and seq % tq == 0
    kern = functools.partial(_flash_kernel, tq=tq, tk=tk, hp=hp)
    return pl.pallas_call(
        kern,
        grid=(bsz, heads // hp, seq // tq),
        in_specs=[
            pl.BlockSpec((1, tq, hp * LANE), lambda b, h, i: (b, i, h)),
            pl.BlockSpec((1, seq, hp * LANE), lambda b, h, i: (b, 0, h)),
            pl.BlockSpec((1, seq, hp * LANE), lambda b, h, i: (b, 0, h)),
        ],
        out_specs=pl.BlockSpec((1, tq, hp * LANE), lambda b, h, i: (b, i, h)),
        out_shape=jax.ShapeDtypeStruct((bsz, seq, wide), BF),
        compiler_params=_params(("parallel", "parallel", "arbitrary")),
        name="mla_flash",
    )(q, k, v)


def _nsa_proj_kernel(h_ref, w_ref, b_ref, c_ref, s_ref, q_ref, kc_ref, vc_ref, ks_ref, vs_ref, kw_ref, vw_ref, g_ref,
                     *, tiles_per_seq):
    tm = h_ref.shape[0]
    hb = h_ref[...].astype(BF)
    pos = (pl.program_id(0) % tiles_per_seq) * tm + lax.broadcasted_iota(jnp.int32, (tm, LANE), 0)
    lane = lax.broadcasted_iota(jnp.int32, (tm, LANE), 1)
    tag = jnp.where(lane == NSA_DIM + lax.shift_right_logical(pos, 6), MASK_BIG, 0.0)
    tag2 = jnp.concatenate([tag] * NSA_GROUPS, axis=1)

    def proj(c0, width):
        return _dot(hb, w_ref[:, c0:c0 + width]) + b_ref[:, c0:c0 + width]

    c = c_ref[...]
    s = s_ref[...]
    c8 = jnp.concatenate([c] * NSA_HEADS, axis=1)
    s8 = jnp.concatenate([s] * NSA_HEADS, axis=1)
    c2 = jnp.concatenate([c] * NSA_GROUPS, axis=1)
    s2 = jnp.concatenate([s] * NSA_GROUPS, axis=1)
    q = (proj(0, 1024) * c8 + proj(1024, 1024) * s8) * (NSA_DIM ** -0.5)
    q_ref[...] = q.astype(BF)
    kc_ref[...] = proj(2048, 128)
    vc_ref[...] = proj(2176, 128)
    ks_ref[...] = (proj(2304, 256) * c2 + proj(2560, 256) * s2 + tag2).astype(BF)
    vs_ref[...] = proj(2816, 256).astype(BF)
    kw_ref[...] = (proj(3072, 256) * c2 + proj(3328, 256) * s2).astype(BF)
    vw_ref[...] = proj(3584, 256).astype(BF)
    g_ref[...] = jax.nn.sigmoid(proj(3840, 256))


def _nsa_proj(h, w, b, cn, sn, *, tm, seq):
    n, d = h.shape
    tps = seq // tm
    tab = pl.BlockSpec((tm, LANE), lambda i: (i % tps, 0))

    def out(width, dt):
        return pl.BlockSpec((tm, width), lambda i: (i, 0)), jax.ShapeDtypeStruct((n, width), dt)

    outs = [out(1024, BF), out(128, F32), out(128, F32), out(256, BF), out(256, BF), out(256, BF), out(256, BF), out(256, F32)]
    assert seq // SLC_BLOCK <= LANE - NSA_DIM
    return pl.pallas_call(
        functools.partial(_nsa_proj_kernel, tiles_per_seq=tps),
        grid=(n // tm,),
        in_specs=[pl.BlockSpec((tm, d), lambda i: (i, 0)), _resident(w.shape), _resident(b.shape), tab, tab],
        out_specs=[o[0] for o in outs],
        out_shape=[o[1] for o in outs],
        compiler_params=_params(("parallel",)),
        name="nsa_proj",
    )(h, w, b, cn, sn)


def _nsa_cmp_kernel(kc_ref, vc_ref, pe_ref, wk_ref, wkr_ref, wv_ref, c_ref, s_ref, kcmp_ref, vcmp_ref):
    n16 = kc_ref.shape[1]

    def halves(x_ref, pe_lo, pe_hi):
        a = x_ref[0]
        nxt = pltpu.roll(a, n16 - 1, 0)
        return (a + pe_lo).astype(BF), (nxt + pe_hi).astype(BF)

    klo, khi = halves(kc_ref, pe_ref[0:1, :], pe_ref[1:2, :])
    kc = _dot(klo, wk_ref[0]) + _dot(khi, wk_ref[1])
    kcr = _dot(klo, wkr_ref[0]) + _dot(khi, wkr_ref[1])
    kcmp_ref[0] = (kc * c_ref[...] + kcr * s_ref[...]).astype(BF)
    vlo, vhi = halves(vc_ref, pe_ref[2:3, :], pe_ref[3:4, :])
    vcmp_ref[0] = (_dot(vlo, wv_ref[0]) + _dot(vhi, wv_ref[1])).astype(BF)


def _nsa_compress(kc, vc, pe, wk, wkr, wv, cc, sc):
    bsz, n16, wide = kc.shape
    blk = pl.BlockSpec((1, n16, wide), lambda b: (b, 0, 0))
    oblk = pl.BlockSpec((1, n16, NSA_GROUPS * LANE), lambda b: (b, 0, 0))
    osh = jax.ShapeDtypeStruct((bsz, n16, NSA_GROUPS * LANE), BF)
    return pl.pallas_call(
        _nsa_cmp_kernel,
        grid=(bsz,),
        in_specs=[blk, blk] + [_resident(a.shape) for a in (pe, wk, wkr, wv, cc, sc)],
        out_specs=[oblk, oblk],
        out_shape=[osh, osh],
        compiler_params=_params(("parallel",)),
        name="nsa_compress",
    )(kc, vc, pe, wk, wkr, wv, cc, sc)


def _nsa_attn_kernel(q_ref, kcmp_ref, vcmp_ref, ks_ref, vs_ref, kw_ref, vw_ref, g_ref, ov_ref, o_ref, *, tk, top_k):
    qi = pl.program_id(1)
    T = q_ref.shape[1]
    R = NSA_HPG * T
    G = NSA_GROUPS
    q0 = qi * T
    qpos = lax.broadcasted_iota(jnp.int32, (T, 1), 0) + q0
    ncp = kcmp_ref.shape[1]
    nb = ov_ref.shape[0]
    ov_t = ov_ref[...]

    def rep(x):
        return jnp.concatenate([x] * NSA_HPG, axis=0)

    def lanes(g):
        return slice(g * LANE, (g + 1) * LANE)

    q4 = [jnp.concatenate([q_ref[0, :, (g * NSA_HPG + h) * LANE:(g * NSA_HPG + h + 1) * LANE]
                           for h in range(NSA_HPG)], axis=0) for g in range(G)]

    cmp_end = lax.broadcasted_iota(jnp.int32, (1, ncp), 1) * CMP_STRIDE + (CMP_BLOCK - 1)
    cbias = rep(jnp.where(cmp_end <= qpos, 0.0, NEG))
    any_valid = rep(jnp.where(qpos >= CMP_BLOCK - 1, 1.0, 0.0))
    jr = lax.broadcasted_iota(jnp.int32, (nb, 1), 0)
    jrf = jr.astype(F32)
    jq = lax.shift_right_logical(lax.broadcasted_iota(jnp.int32, (1, T), 1) + q0, 6)
    forced = (jr == 0) | (jr == jq) | (jr == jq - 1)
    eye_t = jnp.where(lax.broadcasted_iota(jnp.int32, (T, T), 0) == lax.broadcasted_iota(jnp.int32, (T, T), 1),
                      1.0, 0.0).astype(BF)
    o_cmp, q4s = [], []
    for g in range(G):
        sm = _dot_t(q4[g], kcmp_ref[0, :, lanes(g)]) + cbias
        e = jnp.exp(sm - jnp.max(sm, -1, keepdims=True))
        p = e * (any_valid / jnp.sum(e, -1, keepdims=True))
        o_cmp.append(_dot(p.astype(BF), vcmp_ref[0, :, lanes(g)]))
        psum = p[0:T] + p[T:2 * T] + p[2 * T:3 * T] + p[3 * T:4 * T]
        hi = psum.astype(BF)
        r1 = psum - hi.astype(F32)
        mid = r1.astype(BF)
        lo = (r1 - mid.astype(F32)).astype(BF)
        imp = _dot_t(ov_t, hi) + _dot_t(ov_t, mid) + _dot_t(ov_t, lo)
        imp = jnp.where(forced, 1e9, imp)
        imp = jnp.where(jr <= jq, imp, -1.0)
        work = imp
        sel = jnp.zeros_like(imp)
        for _ in range(top_k):
            mx = jnp.max(work, 0, keepdims=True)
            idx = jnp.min(jnp.where(work == mx, jrf, float(nb)), 0, keepdims=True)
            pick = jrf == idx
            sel = jnp.where(pick, 1.0, sel)
            work = jnp.where(pick, -2.0, work)
        unsel_t = jnp.where(imp >= 0.0, sel, 0.0) - 1.0
        pad_t = [jnp.zeros((NSA_DIM, T), F32), unsel_t]
        if LANE - NSA_DIM - nb:
            pad_t.append(jnp.zeros((LANE - NSA_DIM - nb, T), F32))
        unsel = _dot_t(eye_t, jnp.concatenate(pad_t, axis=0).astype(BF)).astype(BF)
        q4s.append(q4[g] + rep(unsel))

    def slc_tile(j, carries, diagonal):
        k0 = pl.multiple_of(j * tk, tk)
        out = []
        for g in range(G):
            sc = _dot_t(q4s[g], ks_ref[0, pl.ds(k0, tk), lanes(g)])
            if diagonal:
                sc = sc + rep(jnp.where((lax.broadcasted_iota(jnp.int32, (1, tk), 1) + k0) <= qpos, 0.0, NEG))
            out.append(_online_softmax_step(sc, vs_ref[0, pl.ds(k0, tk), lanes(g)], carries[g]))
        return tuple(out)

    j_last = q0 // tk
    carries = lax.fori_loop(0, j_last, lambda j, c: slc_tile(j, c, False),
                            tuple(_softmax_init(R, LANE) for _ in range(G)))
    carries = slc_tile(j_last, carries, True)

    wk = WINDOW + T
    w0 = pl.multiple_of(jnp.maximum(q0 - WINDOW, 0), T)
    dist = qpos - (lax.broadcasted_iota(jnp.int32, (1, wk), 1) + w0)
    wbias = rep(jnp.where((dist >= 0) & (dist < WINDOW), 0.0, NEG))
    o_win = []
    for g in range(G):
        sc = _dot_t(q4[g], kw_ref[0, pl.ds(w0, wk), lanes(g)]) + wbias
        e = jnp.exp(sc - jnp.max(sc, -1, keepdims=True))
        o_win.append(_dot(e.astype(BF), vw_ref[0, pl.ds(w0, wk), lanes(g)]) * (1.0 / jnp.sum(e, -1, keepdims=True)))

    gates = g_ref[0]
    outs = []
    for g in range(G):
        _, l, acc = carries[g]
        o_slc = acc * (1.0 / l)
        for h in range(NSA_HPG):
            rows = slice(h * T, (h + 1) * T)
            c = g * LANE + 3 * h
            outs.append(gates[:, c:c + 1] * o_cmp[g][rows] + gates[:, c + 1:c + 2] * o_slc[rows]
                        + gates[:, c + 2:c + 3] * o_win[g][rows])
    o_ref[0] = jnp.concatenate(outs, axis=1).astype(o_ref.dtype)


def _nsa_attention(q, kcmp, vcmp, ks, vs, kw, vw, gates, ov, *, tq, tk, top_k):
    bsz, seq, wide = q.shape
    n16 = kcmp.shape[1]
    gw = NSA_GROUPS * LANE
    kern = functools.partial(_nsa_attn_kernel, tk=tk, top_k=top_k)
    cblk = pl.BlockSpec((1, n16, gw), lambda b, i: (b, 0, 0))
    sblk = pl.BlockSpec((1, seq, gw), lambda b, i: (b, 0, 0))
    return pl.pallas_call(
        kern,
        grid=(bsz, seq // tq),
        in_specs=[pl.BlockSpec((1, tq, wide), lambda b, i: (b, i, 0)), cblk, cblk, sblk, sblk, sblk, sblk,
                  pl.BlockSpec((1, tq, gw), lambda b, i: (b, i, 0)), _resident(ov.shape)],
        out_specs=pl.BlockSpec((1, tq, wide), lambda b, i: (b, i, 0)),
        out_shape=jax.ShapeDtypeStruct((bsz, seq, wide), BF),
        compiler_params=_params(("parallel", "arbitrary")),
        name="nsa_attn",
    )(q, kcmp, vcmp, ks, vs, kw, vw, gates, ov)


def _merge_kernel(x_ref, ab_ref, oc_ref, od_ref, wg_ref, bg_ref, wc_ref, wd_ref, wo_ref, g_ref, b_ref, o_ref):
    x = x_ref[...]
    gates = _dot(x.astype(BF), wg_ref[...]) + bg_ref[...]
    yc = _dot(oc_ref[...], wc_ref[...])
    yd = _dot(od_ref[...], wd_ref[...])
    merged = ab_ref[...] + jax.nn.sigmoid(gates[:, :D_MODEL]) * yc + jax.nn.sigmoid(gates[:, D_MODEL:]) * yd
    mix = _dot(merged.astype(BF), wo_ref[...])
    o_ref[...] = _ln(ALPHA * x + mix, g_ref[...], b_ref[...])


def _merge_ln(x, ab, oc, od, wg, bg, wc, wd, wo, g, b, *, tm):
    n, d = x.shape
    row = pl.BlockSpec((tm, d), lambda i: (i, 0))
    return pl.pallas_call(
        _merge_kernel,
        grid=(n // tm,),
        in_specs=[row, row, row, row] + [_resident(a.shape) for a in (wg, bg, wc, wd, wo, g, b)],
        out_specs=row,
        out_shape=jax.ShapeDtypeStruct((n, d), F32),
        compiler_params=_params(("parallel",)),
        name="merge_ln",
    )(x, ab, oc, od, wg, bg, wc, wd, wo, g, b)


def _linear_kernel(x_ref, w_ref, o_ref):
    o_ref[...] = _dot(x_ref[...].astype(BF), w_ref[...]).astype(o_ref.dtype)


def _linear(x, w, *, tm, dtype):
    n, d = x.shape
    return pl.pallas_call(
        _linear_kernel,
        grid=(n // tm,),
        in_specs=[pl.BlockSpec((tm, d), lambda i: (i, 0)), _resident(w.shape)],
        out_specs=pl.BlockSpec((tm, w.shape[1]), lambda i: (i, 0)),
        out_shape=jax.ShapeDtypeStruct((n, w.shape[1]), dtype),
        compiler_params=_params(("parallel",)),
        name="mem_kv",
    )(x, w)


def _xattn_kernel(x_ref, k_ref, v_ref, wq_ref, wo_ref, g_ref, b_ref, o_ref):
    x = x_ref[...]
    q = _dot(x.astype(BF), wq_ref[...]).astype(BF)
    k = k_ref[0]
    v = v_ref[0]
    heads = []
    for h in range(XATTN_HEADS):
        sl = slice(h * XATTN_DIM, (h + 1) * XATTN_DIM)
        s = _dot_t(q[:, sl], k[:, sl]) * (XATTN_DIM ** -0.5)
        e = jnp.exp(s - jnp.max(s, -1, keepdims=True))
        p = e / jnp.sum(e, -1, keepdims=True)
        heads.append(_dot(p.astype(BF), v[:, sl]))
    o = jnp.concatenate(heads, axis=1).astype(BF)
    o_ref[...] = _ln(ALPHA * x + _dot(o, wo_ref[...]), g_ref[...], b_ref[...])


def _xattn_ln(x, kv, wq, wo, g, b, *, tm, seq):
    n, d = x.shape
    tps = seq // tm
    mlen = kv.shape[1]
    hd = XATTN_HEADS * XATTN_DIM
    return pl.pallas_call(
        _xattn_kernel,
        grid=(n // tm,),
        in_specs=[pl.BlockSpec((tm, d), lambda i: (i, 0)),
                  pl.BlockSpec((1, mlen, hd), lambda i: (i // tps, 0, 0)),
                  pl.BlockSpec((1, mlen, hd), lambda i: (i // tps, 0, 1))]
        + [_resident(a.shape) for a in (wq, wo, g, b)],
        out_specs=pl.BlockSpec((tm, d), lambda i: (i, 0)),
        out_shape=jax.ShapeDtypeStruct((n, d), F32),
        compiler_params=_params(("parallel",)),
        name="xattn_ln",
    )(x, kv, kv, wq, wo, g, b)


def _rope_tab(pos, dim):
    inv = ROPE_THETA ** (-(jnp.arange(0, dim, 2, dtype=F32) / dim))
    ang = pos[:, None] * inv[None, :]
    return jnp.cos(ang), jnp.sin(ang)


def _rot_cols(w, half):
    return jnp.concatenate([-w[..., half:2 * half], w[..., :half]], axis=-1)


def _pad_slots(w, n_slots, width):
    lead = w.shape[:-1]
    w = w.reshape(lead + (n_slots, width))
    w = jnp.pad(w, [(0, 0)] * len(lead) + [(0, 0), (0, LANE - width)])
    return w.reshape(lead + (n_slots * LANE,))


def _rot_slots(w, n_slots, width, half):
    lead = w.shape[:-1]
    w = w.reshape(lead + (n_slots, width))
    return _rot_cols(w, half).reshape(lead + (n_slots * width,))


def _pad_rows(w, n_slots, width):
    d = w.shape[-1]
    w = w.reshape(n_slots, width, d)
    w = jnp.pad(w, [(0, 0), (0, LANE - width), (0, 0)])
    return w.reshape(n_slots * LANE, d)


def _layer_params(l, p):
    w_in, b_in = p["w_in"][l], p["b_in"][l]

    def cols(o, wd):
        return w_in[:, o:o + wd], b_in[o:o + wd]

    out = {}
    out["w_ab"] = jnp.concatenate([w_in[:, 0:2560], w_in[:, _O_GA:_O_GC]], axis=1).astype(BF)
    out["b_ab"] = jnp.concatenate([b_in[0:2560], b_in[_O_GA:_O_GC]])[None, :]
    wkr, bkr = cols(_O_KROPE, MLA_ROPE)
    half = MLA_ROPE // 2
    padk = lambda a: jnp.pad(a, [(0, 0)] * (a.ndim - 1) + [(0, LANE - MLA_ROPE)])
    out["w_c"] = jnp.concatenate([w_in[:, _O_QLAT:_O_KROPE], padk(wkr), padk(_rot_cols(wkr, half))], axis=1).astype(BF)
    out["b_c"] = jnp.concatenate([b_in[_O_QLAT:_O_KROPE], padk(bkr), padk(_rot_cols(bkr, half))])[None, :]
    wuq = p["mla_wuq"][l].reshape(MLA_Q_RANK, MLA_HEADS, MLA_NOPE + MLA_ROPE)
    rope_rot = _rot_cols(wuq[..., MLA_NOPE:], half)
    wqa = jnp.pad(wuq, [(0, 0), (0, 0), (0, LANE - MLA_NOPE - MLA_ROPE)])
    wqb = jnp.pad(rope_rot, [(0, 0), (0, 0), (MLA_NOPE, LANE - MLA_NOPE - MLA_ROPE)])
    out["wqa"] = wqa.reshape(MLA_Q_RANK, MLA_HEADS * LANE).astype(BF)
    out["wqb"] = wqb.reshape(MLA_Q_RANK, MLA_HEADS * LANE).astype(BF)
    wukv = p["mla_wukv"][l].reshape(MLA_KV_RANK, MLA_HEADS, MLA_NOPE + MLA_V)
    out["wk_c"] = jnp.pad(wukv[..., :MLA_NOPE], [(0, 0), (0, 0), (0, LANE - MLA_NOPE)]).reshape(MLA_KV_RANK, -1).astype(BF)
    out["wv_c"] = jnp.pad(wukv[..., MLA_NOPE:], [(0, 0), (0, 0), (0, LANE - MLA_V)]).reshape(MLA_KV_RANK, -1).astype(BF)
    wq, bq = cols(_O_NQ, NSA_HEADS * NSA_DIM)
    hd = NSA_DIM // 2
    pieces_w, pieces_b = [], []

    def add(w, b, slots, roped):
        pieces_w.append(_pad_slots(w, slots, NSA_DIM))
        pieces_b.append(_pad_slots(b, slots, NSA_DIM))
        if roped:
            pieces_w.append(_pad_slots(_rot_slots(w, slots, NSA_DIM, hd), slots, NSA_DIM))
            pieces_b.append(_pad_slots(_rot_slots(b, slots, NSA_DIM, hd), slots, NSA_DIM))

    add(wq, bq, NSA_HEADS, True)
    wkc, bkc = cols(_O_NKC, 128)
    wvc, bvc = cols(_O_NVC, 128)
    pieces_w += [wkc, wvc]
    pieces_b += [bkc, bvc]
    add(*cols(_O_NKS, 128), NSA_GROUPS, True)
    add(*cols(_O_NVS, 128), NSA_GROUPS, False)
    add(*cols(_O_NKW, 128), NSA_GROUPS, True)
    add(*cols(_O_NVW, 128), NSA_GROUPS, False)
    wg, bg = cols(_O_NGATE, NSA_HEADS * 3)
    pieces_w.append(_pad_slots(wg, NSA_GROUPS, NSA_HPG * 3))
    pieces_b.append(_pad_slots(bg, NSA_GROUPS, NSA_HPG * 3))
    out["w_d"] = jnp.concatenate(pieces_w, axis=1).astype(BF)
    out["b_d"] = jnp.concatenate(pieces_b)[None, :]

    def cmp_weights(w):
        eye = jnp.eye(NSA_GROUPS, dtype=F32)
        wp = jnp.pad(w, [(0, 0), (0, 0), (0, LANE - NSA_DIM)])
        full = jnp.einsum("lde,gh->lgdhe", wp, eye).reshape(CMP_BLOCK, NSA_GROUPS * NSA_DIM, NSA_GROUPS * LANE)
        return full.reshape(2, CMP_STRIDE * NSA_GROUPS * NSA_DIM, NSA_GROUPS * LANE).astype(BF)

    wck = p["nsa_wcmp_k"][l]
    out["wcmp_k"] = cmp_weights(wck)
    out["wcmp_kr"] = cmp_weights(_rot_cols(wck, hd))
    out["wcmp_v"] = cmp_weights(p["nsa_wcmp_v"][l])

    def pe_rows(pe):
        t = jnp.broadcast_to(pe[:, None, :], (CMP_BLOCK, NSA_GROUPS, NSA_DIM))
        return t.reshape(2, CMP_STRIDE * NSA_GROUPS * NSA_DIM)

    out["pe"] = jnp.concatenate([pe_rows(p["nsa_pe_k"][l]), pe_rows(p["nsa_pe_v"][l])], axis=0)
    out["w_g"] = w_in[:, _O_GC:].astype(BF)
    out["b_g"] = b_in[_O_GC:][None, :]
    out["wout_c"] = _pad_rows(p["mla_wout"][l], MLA_HEADS, MLA_V).astype(BF)
    out["wout_d"] = _pad_rows(p["nsa_wout"][l], NSA_HEADS, NSA_DIM).astype(BF)
    return out


def _tables(seq):
    pos = jnp.arange(seq, dtype=F32)
    c16, s16 = _rope_tab(pos, MLA_ROPE)
    one = jnp.ones((seq, MLA_NOPE), F32)
    zero = jnp.zeros((seq, MLA_NOPE), F32)
    tail = LANE - MLA_NOPE - MLA_ROPE
    cq = jnp.concatenate([one, c16, c16, jnp.ones((seq, tail), F32)], axis=1)
    sq = jnp.concatenate([zero, s16, s16, jnp.zeros((seq, tail), F32)], axis=1)
    ck = jnp.pad(jnp.concatenate([c16, c16], axis=1), [(0, 0), (0, LANE - MLA_ROPE)])
    sk = jnp.pad(jnp.concatenate([s16, s16], axis=1), [(0, 0), (0, LANE - MLA_ROPE)])
    c32, s32 = _rope_tab(pos, NSA_DIM)
    cn = jnp.pad(jnp.concatenate([c32, c32], axis=1), [(0, 0), (0, LANE - NSA_DIM)])
    sn = jnp.pad(jnp.concatenate([s32, s32], axis=1), [(0, 0), (0, LANE - NSA_DIM)])
    n16 = seq // CMP_STRIDE
    cend = (jnp.arange(n16) * CMP_STRIDE + CMP_BLOCK - 1).astype(F32)
    cc32, cs32 = _rope_tab(cend, NSA_DIM)
    ccg = jnp.pad(jnp.concatenate([cc32, cc32], axis=1), [(0, 0), (0, LANE - NSA_DIM)])
    csg = jnp.pad(jnp.concatenate([cs32, cs32], axis=1), [(0, 0), (0, LANE - NSA_DIM)])
    cc = jnp.concatenate([ccg] * NSA_GROUPS, axis=1)
    cs = jnp.concatenate([csg] * NSA_GROUPS, axis=1)
    n_cmp = (seq - CMP_BLOCK) // CMP_STRIDE + 1
    n_slc = seq // SLC_BLOCK
    cstart = jnp.arange(n16) * CMP_STRIDE
    sstart = jnp.arange(n_slc) * SLC_BLOCK
    ovl = (jnp.minimum(cstart[None, :] + CMP_BLOCK, sstart[:, None] + SLC_BLOCK)
           - jnp.maximum(cstart[None, :], sstart[:, None]))
    ovl = jnp.clip(ovl, 0).astype(F32) / CMP_BLOCK
    ovl = jnp.where(jnp.arange(n16)[None, :] < n_cmp, ovl, 0.0).astype(BF)
    pk = jnp.zeros((LANE, MLA_HEADS, LANE), F32)
    pk = pk.at[jnp.arange(MLA_ROPE), :, MLA_NOPE + jnp.arange(MLA_ROPE)].set(1.0)
    pk = pk.reshape(LANE, MLA_HEADS * LANE).astype(BF)
    return dict(cq=cq, sq=sq, ck=ck, sk=sk, cn=cn, sn=sn, cc=cc, cs=cs, ovl=ovl, pk=pk)


def kernel(x, mem, ffn1_w1, ffn1_w3, ffn1_w2, ln1_g, ln1_b, w_in, b_in, gmlp_ln_g, gmlp_ln_b, gmlp_ws, gmlp_bs, gmlp_wout, conv_w, conv_wout, mla_qnorm_g, mla_kvnorm_g, mla_wuq, mla_wukv, mla_wout, nsa_pe_k, nsa_pe_v, nsa_wcmp_k, nsa_wcmp_v, nsa_wout, w_o, ln2_g, ln2_b, xattn_wq, xattn_wk, xattn_wv, xattn_wo, ln3_g, ln3_b, ffn2_w1, ffn2_w3, ffn2_w2, ln4_g, ln4_b):
    bsz, seq, d = x.shape
    mlen = mem.shape[1]
    n = bsz * seq
    assert d == D_MODEL and seq % 512 == 0 and seq >= WINDOW + 256
    p = dict(w_in=w_in, b_in=b_in, mla_wuq=mla_wuq, mla_wukv=mla_wukv, mla_wout=mla_wout,
             nsa_pe_k=nsa_pe_k, nsa_pe_v=nsa_pe_v, nsa_wcmp_k=nsa_wcmp_k, nsa_wcmp_v=nsa_wcmp_v, nsa_wout=nsa_wout)
    tb = _tables(seq)
    tm = 512
    tm_ffn = 512
    tf = D_FF // 2
    n16 = seq // CMP_STRIDE
    top_k = min(SLC_TOPK, seq // SLC_BLOCK)
    row = lambda a: a[None, :]

    h = x.reshape(n, d)
    mem2 = mem.reshape(bsz * mlen, d)
    for l in range(DEPTH):
        lp = _layer_params(l, p)
        h = _ffn_ln(h, ffn1_w1[l].astype(BF), ffn1_w3[l].astype(BF), ffn1_w2[l].astype(BF),
                    row(ln1_g[l]), row(ln1_b[l]), tm=tm_ffn, tf=tf)
        ab = _mix_ab(h, lp["w_ab"], lp["b_ab"], row(gmlp_ln_g[l]), row(gmlp_ln_b[l]), gmlp_ws[l], gmlp_bs[l].T,
                     gmlp_wout[l].astype(BF), conv_w[l], conv_wout[l].astype(BF), tm=tm, seq=seq)
        qc, kc_, vc_ = _mla_proj(h, lp["w_c"], lp["b_c"], row(mla_qnorm_g[l]), row(mla_kvnorm_g[l]),
                                 lp["wqa"], lp["wqb"], lp["wk_c"], lp["wv_c"], tb["pk"],
                                 tb["cq"], tb["sq"], tb["ck"], tb["sk"], tm=tm, seq=seq)
        wide = MLA_HEADS * LANE
        oc = _flash_causal(qc.reshape(bsz, seq, wide), kc_.reshape(bsz, seq, wide), vc_.reshape(bsz, seq, wide), tq=512, tk=1024, hp=4)
        qn, nkc, nvc, nks, nvs, nkw, nvw, gates = _nsa_proj(h, lp["w_d"], lp["b_d"], tb["cn"], tb["sn"], tm=tm, seq=seq)
        kcmp, vcmp = _nsa_compress(nkc.reshape(bsz, n16, CMP_STRIDE * 128), nvc.reshape(bsz, n16, CMP_STRIDE * 128),
                                   lp["pe"], lp["wcmp_k"], lp["wcmp_kr"], lp["wcmp_v"], tb["cc"], tb["cs"])
        gw = NSA_GROUPS * LANE
        od = _nsa_attention(qn.reshape(bsz, seq, NSA_HEADS * LANE), kcmp, vcmp,
                            nks.reshape(bsz, seq, gw), nvs.reshape(bsz, seq, gw),
                            nkw.reshape(bsz, seq, gw), nvw.reshape(bsz, seq, gw),
                            gates.reshape(bsz, seq, gw), tb["ovl"], tq=256, tk=512, top_k=top_k)
        h = _merge_ln(h, ab, oc.reshape(n, wide), od.reshape(n, NSA_HEADS * LANE), lp["w_g"], lp["b_g"],
                      lp["wout_c"], lp["wout_d"], w_o[l].astype(BF), row(ln2_g[l]), row(ln2_b[l]), tm=tm)
        kv = _linear(mem2, jnp.concatenate([xattn_wk[l], xattn_wv[l]], axis=1).astype(BF), tm=min(256, bsz * mlen), dtype=BF)
        h = _xattn_ln(h, kv.reshape(bsz, mlen, 2 * XATTN_HEADS * XATTN_DIM), xattn_wq[l].astype(BF),
                      xattn_wo[l].astype(BF), row(ln3_g[l]), row(ln3_b[l]), tm=tm, seq=seq)
        h = _ffn_ln(h, ffn2_w1[l].astype(BF), ffn2_w3[l].astype(BF), ffn2_w2[l].astype(BF),
                    row(ln4_g[l]), row(ln4_b[l]), tm=tm_ffn, tf=tf)
    return h.reshape(bsz, seq, d)
```

```python
import functools

import jax
import jax.numpy as jnp
from jax import lax
from jax.experimental import pallas as pl
from jax.experimental.pallas import tpu as pltpu

BF = jnp.bfloat16
F32 = jnp.float32

D_MODEL = 1024
D_FF = 2816
LN_EPS = 1e-5
RMS_EPS = 1e-6
ROPE_THETA = 10000.0
DEPTH = 2
ALPHA = (2 * DEPTH) ** 0.25
NEG = -1e30
LOG2_E = 1.4426950408889634
DENOM_LANE = 64
MASK_BIG = 2.0 ** 100

GMLP_CHUNK = 128
GMLP_GROUPS = 4
GMLP_WIDTH = 512
CONV_WIDTH = 512
CONV_K = 3
MLA_HEADS = 8
MLA_Q_RANK = 256
MLA_KV_RANK = 128
MLA_NOPE = 64
MLA_ROPE = 32
MLA_V = 64
NSA_HEADS = 8
NSA_GROUPS = 2
NSA_HPG = 4
NSA_DIM = 64
CMP_BLOCK = 32
CMP_STRIDE = 16
SLC_BLOCK = 64
SLC_TOPK = 8
WINDOW = 512
XATTN_HEADS = 4
XATTN_DIM = 128

LANE = 128
CONV_HALO = 8
VMEM_LIMIT = 56 * 1024 * 1024

_O_U, _O_V, _O_CB, _O_CC, _O_CH = 0, 512, 1024, 1536, 2048
_O_QLAT, _O_KVLAT, _O_KROPE = 2560, 2816, 2944
_O_NQ, _O_NKC, _O_NVC, _O_NKS, _O_NVS, _O_NKW, _O_NVW, _O_NGATE = 2976, 3488, 3616, 3744, 3872, 4000, 4128, 4256
_O_GA, _O_GB, _O_GC, _O_GD = 4280, 5304, 6328, 7352


def _dot(a, b):
    return jnp.dot(a, b, preferred_element_type=F32)


def _dot_t(a, b):
    return lax.dot_general(a, b, (((1,), (1,)), ((), ())), preferred_element_type=F32)


def _ln(y, g, b):
    mu = jnp.mean(y, -1, keepdims=True)
    d = y - mu
    var = jnp.mean(d * d, -1, keepdims=True)
    return d * lax.rsqrt(var + LN_EPS) * g + b


def _rms(x, g):
    return x * lax.rsqrt(jnp.mean(x * x, -1, keepdims=True) + RMS_EPS) * g


def _resident(shape):
    n = len(shape)
    return pl.BlockSpec(shape, lambda *_: (0,) * n, pipeline_mode=pl.Buffered(1))


def _params(sem):
    return pltpu.CompilerParams(dimension_semantics=sem, vmem_limit_bytes=VMEM_LIMIT)


def _ffn_ln_kernel(x_ref, w1_ref, w3_ref, w2_ref, g_ref, b_ref, o_ref, *, tf):
    x = x_ref[...]
    xb = x.astype(BF)
    acc = None
    for c in range(w1_ref.shape[1] // tf):
        cols = slice(c * tf, (c + 1) * tf)
        h1 = _dot(xb, w1_ref[:, cols])
        h3 = _dot(xb, w3_ref[:, cols])
        hh = (h1 * jax.nn.sigmoid(h1)) * h3
        part = _dot(hh.astype(BF), w2_ref[cols, :])
        acc = part if acc is None else acc + part
    o_ref[...] = _ln(ALPHA * x + 0.5 * acc, g_ref[...], b_ref[...])


def _ffn_ln(x, w1, w3, w2, g, b, *, tm, tf):
    n, d = x.shape
    return pl.pallas_call(
        functools.partial(_ffn_ln_kernel, tf=tf),
        grid=(n // tm,),
        in_specs=[pl.BlockSpec((tm, d), lambda i: (i, 0))] + [_resident(a.shape) for a in (w1, w3, w2, g, b)],
        out_specs=pl.BlockSpec((tm, d), lambda i: (i, 0)),
        out_shape=jax.ShapeDtypeStruct((n, d), F32),
        compiler_params=_params(("parallel",)),
        name="ffn_ln",
    )(x, w1, w3, w2, g, b)


def _ab_kernel(h_ref, w_ref, b_ref, lng_ref, lnb_ref, ws_ref, bst_ref, wga_ref, cw_ref, wcb_ref,
               o_ref, prev_ref, *, tiles_per_seq):
    i = pl.program_id(0)
    tm = h_ref.shape[0]
    hb = h_ref[...].astype(BF)

    def proj(c0, width):
        return _dot(hb, w_ref[:, c0:c0 + width]) + b_ref[:, c0:c0 + width]

    u = proj(0, GMLP_WIDTH)
    v = _ln(proj(512, GMLP_WIDTH), lng_ref[...], lnb_ref[...]).astype(BF)
    row = lax.broadcasted_iota(jnp.int32, (GMLP_CHUNK, GMLP_CHUNK), 0)
    col = lax.broadcasted_iota(jnp.int32, (GMLP_CHUNK, GMLP_CHUNK), 1)
    gd = GMLP_WIDTH // GMLP_GROUPS
    wgs = [jnp.where(row >= col, ws_ref[g], 0.0).astype(BF) for g in range(GMLP_GROUPS)]
    chunks = []
    for c in range(tm // GMLP_CHUNK):
        r0 = c * GMLP_CHUNK
        chunks.append(jnp.concatenate(
            [_dot(wgs[g], v[r0:r0 + GMLP_CHUNK, g * gd:(g + 1) * gd]) + bst_ref[:, g:g + 1]
             for g in range(GMLP_GROUPS)], axis=1))
    s = jnp.concatenate(chunks, axis=0)
    ya = _dot((u * s).astype(BF), wga_ref[...])

    cb = proj(1024, CONV_WIDTH)
    z = proj(1536, CONV_WIDTH) * proj(2048, CONV_WIDTH)

    @pl.when(i % tiles_per_seq == 0)
    def _():
        prev_ref[...] = jnp.zeros_like(prev_ref)

    zext = jnp.concatenate([prev_ref[...], z], axis=0)
    z1 = pltpu.roll(zext, 1, 0)[CONV_HALO:]
    z2 = pltpu.roll(zext, 2, 0)[CONV_HALO:]
    y = cw_ref[0:1, :] * z2 + cw_ref[1:2, :] * z1 + cw_ref[2:3, :] * z
    prev_ref[...] = z[tm - CONV_HALO:, :]
    yb = _dot((cb * y).astype(BF), wcb_ref[...])

    ga = proj(2560, D_MODEL)
    gb = proj(3584, D_MODEL)
    o_ref[...] = jax.nn.sigmoid(ga) * ya + jax.nn.sigmoid(gb) * yb


def _mix_ab(h, w, b, lng, lnb, ws, bst, wga, cw, wcb, *, tm, seq):
    n, d = h.shape
    kern = functools.partial(_ab_kernel, tiles_per_seq=seq // tm)
    return pl.pallas_call(
        kern,
        grid=(n // tm,),
        in_specs=[pl.BlockSpec((tm, d), lambda i: (i, 0))] + [_resident(a.shape) for a in (w, b, lng, lnb, ws, bst, wga, cw, wcb)],
        out_specs=pl.BlockSpec((tm, d), lambda i: (i, 0)),
        out_shape=jax.ShapeDtypeStruct((n, d), F32),
        scratch_shapes=[pltpu.VMEM((CONV_HALO, CONV_WIDTH), F32)],
        compiler_params=_params(("arbitrary",)),
        name="mix_ab",
    )(h, w, b, lng, lnb, ws, bst, wga, cw, wcb)


def _mla_proj_kernel(h_ref, w_ref, b_ref, qg_ref, kvg_ref, wqa_ref, wqb_ref, wk_ref, wv_ref, pk_ref,
                     cq_ref, sq_ref, ck_ref, sk_ref, q_ref, k_ref, v_ref):
    hb = h_ref[...].astype(BF)
    z = _dot(hb, w_ref[...]) + b_ref[...]
    qn = _rms(z[:, 0:256], qg_ref[...]).astype(BF)
    kvn = _rms(z[:, 256:384], kvg_ref[...]).astype(BF)
    cq = jnp.concatenate([cq_ref[...]] * MLA_HEADS, axis=1)
    sq = jnp.concatenate([sq_ref[...]] * MLA_HEADS, axis=1)
    scale = (MLA_NOPE + MLA_ROPE) ** -0.5 * LOG2_E
    q = (_dot(qn, wqa_ref[...]) * cq + _dot(qn, wqb_ref[...]) * sq) * scale
    q_ref[...] = q.astype(BF)
    kpe = (z[:, 384:512] * ck_ref[...] + z[:, 512:640] * sk_ref[...]).astype(BF)
    k_ref[...] = (_dot(kvn, wk_ref[...]) + _dot(kpe, pk_ref[...])).astype(BF)
    v_ref[...] = (_dot(kvn, wv_ref[...]) + _denom_ones(v_ref.shape[1])).astype(BF)


def _mla_proj(h, w, b, qg, kvg, wqa, wqb, wk, wv, pk, cq, sq, ck, sk, *, tm, seq):
    n, d = h.shape
    tps = seq // tm
    tab = pl.BlockSpec((tm, LANE), lambda i: (i % tps, 0))
    wide = MLA_HEADS * LANE
    out = jax.ShapeDtypeStruct((n, wide), BF)
    return pl.pallas_call(
        _mla_proj_kernel,
        grid=(n // tm,),
        in_specs=[pl.BlockSpec((tm, d), lambda i: (i, 0))]
        + [_resident(a.shape) for a in (w, b, qg, kvg, wqa, wqb, wk, wv, pk)] + [tab] * 4,
        out_specs=[pl.BlockSpec((tm, wide), lambda i: (i, 0))] * 3,
        out_shape=[out, out, out],
        compiler_params=_params(("parallel",)),
        name="mla_proj",
    )(h, w, b, qg, kvg, wqa, wqb, wk, wv, pk, cq, sq, ck, sk)


def _online_softmax_step(s, v, carry):
    m, acc = carry
    m_new = jnp.maximum(m, jnp.max(s, -1, keepdims=True))
    p = jnp.exp2(s - m_new)
    acc = jnp.exp2(m - m_new) * acc + _dot(p.astype(BF), v)
    return m_new, acc


def _softmax_init(rows, width):
    return (jnp.full((rows, 1), NEG, F32), jnp.zeros((rows, width), F32))


def _normalize(acc):
    return acc * (1.0 / acc[:, DENOM_LANE:DENOM_LANE + 1])


def _denom_ones(width):
    lane = lax.broadcasted_iota(jnp.int32, (1, width), 1)
    return jnp.where(lane % LANE == DENOM_LANE, 1.0, 0.0)


def _flash_kernel(q_ref, k_ref, v_ref, o_ref, *, tq, tk, hp):
    qi = pl.program_id(2)
    q0 = qi * tq
    qs = [q_ref[0, :, h * LANE:(h + 1) * LANE] for h in range(hp)]

    def tile(j, carries, width, masked):
        k0 = pl.multiple_of(j * width, width)
        out = []
        for h in range(hp):
            s = _dot_t(qs[h], k_ref[0, pl.ds(k0, width), h * LANE:(h + 1) * LANE])
            if masked:
                r = lax.broadcasted_iota(jnp.int32, (tq, width), 0) + q0
                c = lax.broadcasted_iota(jnp.int32, (tq, width), 1) + k0
                s = jnp.where(c <= r, s, NEG)
            out.append(_online_softmax_step(s, v_ref[0, pl.ds(k0, width), h * LANE:(h + 1) * LANE], carries[h]))
        return tuple(out)

    n_wide = q0 // tk
    init = tuple(_softmax_init(tq, LANE) for _ in range(hp))
    carries = lax.fori_loop(0, n_wide, lambda j, c: tile(j, c, tk, False), init)
    carries = lax.fori_loop(n_wide * (tk // tq), qi + 1, lambda j, c: tile(j, c, tq, True), carries)
    o_ref[0] = jnp.concatenate([_normalize(acc) for (_, acc) in carries], axis=1).astype(o_ref.dtype)


def _flash_causal(q, k, v, *, tq, tk, hp):
    bsz, seq, wide = q.shape
    heads = wide // LANE
    assert tk % tq == 0 and seq % tq == 0
    kern = functools.partial(_flash_kernel, tq=tq, tk=tk, hp=hp)
    return pl.pallas_call(
        kern,
        grid=(bsz, heads // hp, seq // tq),
        in_specs=[
            pl.BlockSpec((1, tq, hp * LANE), lambda b, h, i: (b, i, h)),
            pl.BlockSpec((1, seq, hp * LANE), lambda b, h, i: (b, 0, h)),
            pl.BlockSpec((1, seq, hp * LANE), lambda b, h, i: (b, 0, h)),
        ],
        out_specs=pl.BlockSpec((1, tq, hp * LANE), lambda b, h, i: (b, i, h)),
        out_shape=jax.ShapeDtypeStruct((bsz, seq, wide), BF),
        compiler_params=_params(("parallel", "parallel", "arbitrary")),
        name="mla_flash",
    )(q, k, v)


def _nsa_proj_kernel(h_ref, w_ref, b_ref, c_ref, s_ref, q_ref, kc_ref, vc_ref, ks_ref, vs_ref, kw_ref, vw_ref, g_ref,
                     *, tiles_per_seq):
    tm = h_ref.shape[0]
    hb = h_ref[...].astype(BF)
    pos = (pl.program_id(0) % tiles_per_seq) * tm + lax.broadcasted_iota(jnp.int32, (tm, LANE), 0)
    lane = lax.broadcasted_iota(jnp.int32, (tm, LANE), 1)
    tag = jnp.where(lane == NSA_DIM + lax.shift_right_logical(pos, 6), MASK_BIG, 0.0)
    tag2 = jnp.concatenate([tag] * NSA_GROUPS, axis=1)

    def proj(c0, width):
        return _dot(hb, w_ref[:, c0:c0 + width]) + b_ref[:, c0:c0 + width]

    c = c_ref[...]
    s = s_ref[...]
    c8 = jnp.concatenate([c] * NSA_HEADS, axis=1)
    s8 = jnp.concatenate([s] * NSA_HEADS, axis=1)
    c2 = jnp.concatenate([c] * NSA_GROUPS, axis=1)
    s2 = jnp.concatenate([s] * NSA_GROUPS, axis=1)
    q = (proj(0, 1024) * c8 + proj(1024, 1024) * s8) * (NSA_DIM ** -0.5 * LOG2_E)
    q_ref[...] = q.astype(BF)
    kc_ref[...] = proj(2048, 128)
    vc_ref[...] = proj(2176, 128)
    ks_ref[...] = (proj(2304, 256) * c2 + proj(2560, 256) * s2 + tag2).astype(BF)
    ones = _denom_ones(NSA_GROUPS * LANE)
    vs_ref[...] = (proj(2816, 256) + ones).astype(BF)
    kw_ref[...] = (proj(3072, 256) * c2 + proj(3328, 256) * s2).astype(BF)
    vw_ref[...] = (proj(3584, 256) + ones).astype(BF)
    g_ref[...] = jax.nn.sigmoid(proj(3840, 256))


def _nsa_proj(h, w, b, cn, sn, *, tm, seq):
    n, d = h.shape
    tps = seq // tm
    tab = pl.BlockSpec((tm, LANE), lambda i: (i % tps, 0))

    def out(width, dt):
        return pl.BlockSpec((tm, width), lambda i: (i, 0)), jax.ShapeDtypeStruct((n, width), dt)

    outs = [out(1024, BF), out(128, F32), out(128, F32), out(256, BF), out(256, BF), out(256, BF), out(256, BF), out(256, F32)]
    assert seq // SLC_BLOCK <= LANE - NSA_DIM
    return pl.pallas_call(
        functools.partial(_nsa_proj_kernel, tiles_per_seq=tps),
        grid=(n // tm,),
        in_specs=[pl.BlockSpec((tm, d), lambda i: (i, 0)), _resident(w.shape), _resident(b.shape), tab, tab],
        out_specs=[o[0] for o in outs],
        out_shape=[o[1] for o in outs],
        compiler_params=_params(("parallel",)),
        name="nsa_proj",
    )(h, w, b, cn, sn)


def _nsa_cmp_kernel(kc_ref, vc_ref, pe_ref, wk_ref, wkr_ref, wv_ref, c_ref, s_ref, kcmp_ref, vcmp_ref):
    n16 = kc_ref.shape[1]

    def halves(x_ref, pe_lo, pe_hi):
        a = x_ref[0]
        nxt = pltpu.roll(a, n16 - 1, 0)
        return (a + pe_lo).astype(BF), (nxt + pe_hi).astype(BF)

    klo, khi = halves(kc_ref, pe_ref[0:1, :], pe_ref[1:2, :])
    kc = _dot(klo, wk_ref[0]) + _dot(khi, wk_ref[1])
    kcr = _dot(klo, wkr_ref[0]) + _dot(khi, wkr_ref[1])
    kcmp_ref[0] = (kc * c_ref[...] + kcr * s_ref[...]).astype(BF)
    vlo, vhi = halves(vc_ref, pe_ref[2:3, :], pe_ref[3:4, :])
    vcmp_ref[0] = (_dot(vlo, wv_ref[0]) + _dot(vhi, wv_ref[1])).astype(BF)


def _nsa_compress(kc, vc, pe, wk, wkr, wv, cc, sc):
    bsz, n16, wide = kc.shape
    blk = pl.BlockSpec((1, n16, wide), lambda b: (b, 0, 0))
    oblk = pl.BlockSpec((1, n16, NSA_GROUPS * LANE), lambda b: (b, 0, 0))
    osh = jax.ShapeDtypeStruct((bsz, n16, NSA_GROUPS * LANE), BF)
    return pl.pallas_call(
        _nsa_cmp_kernel,
        grid=(bsz,),
        in_specs=[blk, blk] + [_resident(a.shape) for a in (pe, wk, wkr, wv, cc, sc)],
        out_specs=[oblk, oblk],
        out_shape=[osh, osh],
        compiler_params=_params(("parallel",)),
        name="nsa_compress",
    )(kc, vc, pe, wk, wkr, wv, cc, sc)


def _nsa_attn_kernel(q_ref, kcmp_ref, vcmp_ref, ks_ref, vs_ref, kw_ref, vw_ref, g_ref, ov_ref, o_ref, *, tk, top_k):
    qi = pl.program_id(1)
    T = q_ref.shape[1]
    R = NSA_HPG * T
    G = NSA_GROUPS
    q0 = qi * T
    qpos = lax.broadcasted_iota(jnp.int32, (T, 1), 0) + q0
    ncp = kcmp_ref.shape[1]
    nb = ov_ref.shape[0]
    ov_t = ov_ref[...]

    def rep(x):
        return jnp.concatenate([x] * NSA_HPG, axis=0)

    def lanes(g):
        return slice(g * LANE, (g + 1) * LANE)

    q4 = [jnp.concatenate([q_ref[0, :, (g * NSA_HPG + h) * LANE:(g * NSA_HPG + h + 1) * LANE]
                           for h in range(NSA_HPG)], axis=0) for g in range(G)]

    cmp_end = lax.broadcasted_iota(jnp.int32, (1, ncp), 1) * CMP_STRIDE + (CMP_BLOCK - 1)
    cbias = rep(jnp.where(cmp_end <= qpos, 0.0, NEG))
    any_valid = rep(jnp.where(qpos >= CMP_BLOCK - 1, 1.0, 0.0))
    jr = lax.broadcasted_iota(jnp.int32, (nb, 1), 0)
    jrf = jr.astype(F32)
    jq = lax.shift_right_logical(lax.broadcasted_iota(jnp.int32, (1, T), 1) + q0, 6)
    forced = (jr == 0) | (jr == jq) | (jr == jq - 1)
    eye_t = jnp.where(lax.broadcasted_iota(jnp.int32, (T, T), 0) == lax.broadcasted_iota(jnp.int32, (T, T), 1),
                      1.0, 0.0).astype(BF)
    o_cmp, q4s = [], []
    for g in range(G):
        sm = _dot_t(q4[g], kcmp_ref[0, :, lanes(g)]) + cbias
        e = jnp.exp2(sm - jnp.max(sm, -1, keepdims=True))
        p = e * (any_valid / jnp.sum(e, -1, keepdims=True))
        o_cmp.append(_dot(p.astype(BF), vcmp_ref[0, :, lanes(g)]))
        psum = p[0:T] + p[T:2 * T] + p[2 * T:3 * T] + p[3 * T:4 * T]
        hi = psum.astype(BF)
        r1 = psum - hi.astype(F32)
        mid = r1.astype(BF)
        lo = (r1 - mid.astype(F32)).astype(BF)
        imp = _dot_t(ov_t, hi) + _dot_t(ov_t, mid) + _dot_t(ov_t, lo)
        imp = jnp.where(forced, 1e9, imp)
        imp = jnp.where(jr <= jq, imp, -1.0)
        work = imp
        sel = jnp.zeros_like(imp)
        for _ in range(top_k):
            mx = jnp.max(work, 0, keepdims=True)
            idx = jnp.min(jnp.where(work == mx, jrf, float(nb)), 0, keepdims=True)
            pick = jrf == idx
            sel = jnp.where(pick, 1.0, sel)
            work = jnp.where(pick, -2.0, work)
        unsel_t = jnp.where(imp >= 0.0, sel, 0.0) - 1.0
        pad_t = [jnp.zeros((NSA_DIM, T), F32), unsel_t]
        if LANE - NSA_DIM - nb:
            pad_t.append(jnp.zeros((LANE - NSA_DIM - nb, T), F32))
        unsel = _dot_t(eye_t, jnp.concatenate(pad_t, axis=0).astype(BF)).astype(BF)
        q4s.append(q4[g] + rep(unsel))

    def slc_tile(j, carries, diagonal):
        k0 = pl.multiple_of(j * tk, tk)
        out = []
        for g in range(G):
            sc = _dot_t(q4s[g], ks_ref[0, pl.ds(k0, tk), lanes(g)])
            if diagonal:
                sc = sc + rep(jnp.where((lax.broadcasted_iota(jnp.int32, (1, tk), 1) + k0) <= qpos, 0.0, NEG))
            out.append(_online_softmax_step(sc, vs_ref[0, pl.ds(k0, tk), lanes(g)], carries[g]))
        return tuple(out)

    j_last = q0 // tk
    carries = lax.fori_loop(0, j_last, lambda j, c: slc_tile(j, c, False),
                            tuple(_softmax_init(R, LANE) for _ in range(G)))
    carries = slc_tile(j_last, carries, True)

    wk = WINDOW + T
    w0 = pl.multiple_of(jnp.maximum(q0 - WINDOW, 0), T)
    dist = qpos - (lax.broadcasted_iota(jnp.int32, (1, wk), 1) + w0)
    wbias = rep(jnp.where((dist >= 0) & (dist < WINDOW), 0.0, NEG))
    o_win = []
    for g in range(G):
        sc = _dot_t(q4[g], kw_ref[0, pl.ds(w0, wk), lanes(g)]) + wbias
        e = jnp.exp2(sc - jnp.max(sc, -1, keepdims=True))
        o_win.append(_normalize(_dot(e.astype(BF), vw_ref[0, pl.ds(w0, wk), lanes(g)])))

    gates = g_ref[0]
    outs = []
    for g in range(G):
        o_slc = _normalize(carries[g][1])
        for h in range(NSA_HPG):
            rows = slice(h * T, (h + 1) * T)
            c = g * LANE + 3 * h
            outs.append(gates[:, c:c + 1] * o_cmp[g][rows] + gates[:, c + 1:c + 2] * o_slc[rows]
                        + gates[:, c + 2:c + 3] * o_win[g][rows])
    o_ref[0] = jnp.concatenate(outs, axis=1).astype(o_ref.dtype)


def _nsa_attention(q, kcmp, vcmp, ks, vs, kw, vw, gates, ov, *, tq, tk, top_k):
    bsz, seq, wide = q.shape
    n16 = kcmp.shape[1]
    gw = NSA_GROUPS * LANE
    kern = functools.partial(_nsa_attn_kernel, tk=tk, top_k=top_k)
    cblk = pl.BlockSpec((1, n16, gw), lambda b, i: (b, 0, 0))
    sblk = pl.BlockSpec((1, seq, gw), lambda b, i: (b, 0, 0))
    return pl.pallas_call(
        kern,
        grid=(bsz, seq // tq),
        in_specs=[pl.BlockSpec((1, tq, wide), lambda b, i: (b, i, 0)), cblk, cblk, sblk, sblk, sblk, sblk,
                  pl.BlockSpec((1, tq, gw), lambda b, i: (b, i, 0)), _resident(ov.shape)],
        out_specs=pl.BlockSpec((1, tq, wide), lambda b, i: (b, i, 0)),
        out_shape=jax.ShapeDtypeStruct((bsz, seq, wide), BF),
        compiler_params=_params(("parallel", "arbitrary")),
        name="nsa_attn",
    )(q, kcmp, vcmp, ks, vs, kw, vw, gates, ov)


def _merge_kernel(x_ref, ab_ref, oc_ref, od_ref, wg_ref, bg_ref, wc_ref, wd_ref, wo_ref, g_ref, b_ref, o_ref):
    x = x_ref[...]
    gates = _dot(x.astype(BF), wg_ref[...]) + bg_ref[...]
    yc = _dot(oc_ref[...], wc_ref[...])
    yd = _dot(od_ref[...], wd_ref[...])
    merged = ab_ref[...] + jax.nn.sigmoid(gates[:, :D_MODEL]) * yc + jax.nn.sigmoid(gates[:, D_MODEL:]) * yd
    mix = _dot(merged.astype(BF), wo_ref[...])
    o_ref[...] = _ln(ALPHA * x + mix, g_ref[...], b_ref[...])


def _merge_ln(x, ab, oc, od, wg, bg, wc, wd, wo, g, b, *, tm):
    n, d = x.shape
    row = pl.BlockSpec((tm, d), lambda i: (i, 0))
    return pl.pallas_call(
        _merge_kernel,
        grid=(n // tm,),
        in_specs=[row, row, row, row] + [_resident(a.shape) for a in (wg, bg, wc, wd, wo, g, b)],
        out_specs=row,
        out_shape=jax.ShapeDtypeStruct((n, d), F32),
        compiler_params=_params(("parallel",)),
        name="merge_ln",
    )(x, ab, oc, od, wg, bg, wc, wd, wo, g, b)


def _linear_kernel(x_ref, w_ref, o_ref):
    o_ref[...] = _dot(x_ref[...].astype(BF), w_ref[...]).astype(o_ref.dtype)


def _linear(x, w, *, tm, dtype):
    n, d = x.shape
    return pl.pallas_call(
        _linear_kernel,
        grid=(n // tm,),
        in_specs=[pl.BlockSpec((tm, d), lambda i: (i, 0)), _resident(w.shape)],
        out_specs=pl.BlockSpec((tm, w.shape[1]), lambda i: (i, 0)),
        out_shape=jax.ShapeDtypeStruct((n, w.shape[1]), dtype),
        compiler_params=_params(("parallel",)),
        name="mem_kv",
    )(x, w)


def _xattn_kernel(x_ref, k_ref, v_ref, wq_ref, wo_ref, g_ref, b_ref, o_ref):
    x = x_ref[...]
    q = _dot(x.astype(BF), wq_ref[...]).astype(BF)
    k = k_ref[0]
    v = v_ref[0]
    heads = []
    for h in range(XATTN_HEADS):
        sl = slice(h * XATTN_DIM, (h + 1) * XATTN_DIM)
        s = _dot_t(q[:, sl], k[:, sl]) * (XATTN_DIM ** -0.5)
        e = jnp.exp(s - jnp.max(s, -1, keepdims=True))
        p = e / jnp.sum(e, -1, keepdims=True)
        heads.append(_dot(p.astype(BF), v[:, sl]))
    o = jnp.concatenate(heads, axis=1).astype(BF)
    o_ref[...] = _ln(ALPHA * x + _dot(o, wo_ref[...]), g_ref[...], b_ref[...])


def _xattn_ln(x, kv, wq, wo, g, b, *, tm, seq):
    n, d = x.shape
    tps = seq // tm
    mlen = kv.shape[1]
    hd = XATTN_HEADS * XATTN_DIM
    return pl.pallas_call(
        _xattn_kernel,
        grid=(n // tm,),
        in_specs=[pl.BlockSpec((tm, d), lambda i: (i, 0)),
                  pl.BlockSpec((1, mlen, hd), lambda i: (i // tps, 0, 0)),
                  pl.BlockSpec((1, mlen, hd), lambda i: (i // tps, 0, 1))]
        + [_resident(a.shape) for a in (wq, wo, g, b)],
        out_specs=pl.BlockSpec((tm, d), lambda i: (i, 0)),
        out_shape=jax.ShapeDtypeStruct((n, d), F32),
        compiler_params=_params(("parallel",)),
        name="xattn_ln",
    )(x, kv, kv, wq, wo, g, b)


def _rope_tab(pos, dim):
    inv = ROPE_THETA ** (-(jnp.arange(0, dim, 2, dtype=F32) / dim))
    ang = pos[:, None] * inv[None, :]
    return jnp.cos(ang), jnp.sin(ang)


def _rot_cols(w, half):
    return jnp.concatenate([-w[..., half:2 * half], w[..., :half]], axis=-1)


def _pad_slots(w, n_slots, width):
    lead = w.shape[:-1]
    w = w.reshape(lead + (n_slots, width))
    w = jnp.pad(w, [(0, 0)] * len(lead) + [(0, 0), (0, LANE - width)])
    return w.reshape(lead + (n_slots * LANE,))


def _rot_slots(w, n_slots, width, half):
    lead = w.shape[:-1]
    w = w.reshape(lead + (n_slots, width))
    return _rot_cols(w, half).reshape(lead + (n_slots * width,))


def _pad_rows(w, n_slots, width):
    d = w.shape[-1]
    w = w.reshape(n_slots, width, d)
    w = jnp.pad(w, [(0, 0), (0, LANE - width), (0, 0)])
    return w.reshape(n_slots * LANE, d)


def _layer_params(l, p):
    w_in, b_in = p["w_in"][l], p["b_in"][l]

    def cols(o, wd):
        return w_in[:, o:o + wd], b_in[o:o + wd]

    out = {}
    out["w_ab"] = jnp.concatenate([w_in[:, 0:2560], w_in[:, _O_GA:_O_GC]], axis=1).astype(BF)
    out["b_ab"] = jnp.concatenate([b_in[0:2560], b_in[_O_GA:_O_GC]])[None, :]
    wkr, bkr = cols(_O_KROPE, MLA_ROPE)
    half = MLA_ROPE // 2
    padk = lambda a: jnp.pad(a, [(0, 0)] * (a.ndim - 1) + [(0, LANE - MLA_ROPE)])
    out["w_c"] = jnp.concatenate([w_in[:, _O_QLAT:_O_KROPE], padk(wkr), padk(_rot_cols(wkr, half))], axis=1).astype(BF)
    out["b_c"] = jnp.concatenate([b_in[_O_QLAT:_O_KROPE], padk(bkr), padk(_rot_cols(bkr, half))])[None, :]
    wuq = p["mla_wuq"][l].reshape(MLA_Q_RANK, MLA_HEADS, MLA_NOPE + MLA_ROPE)
    rope_rot = _rot_cols(wuq[..., MLA_NOPE:], half)
    wqa = jnp.pad(wuq, [(0, 0), (0, 0), (0, LANE - MLA_NOPE - MLA_ROPE)])
    wqb = jnp.pad(rope_rot, [(0, 0), (0, 0), (MLA_NOPE, LANE - MLA_NOPE - MLA_ROPE)])
    out["wqa"] = wqa.reshape(MLA_Q_RANK, MLA_HEADS * LANE).astype(BF)
    out["wqb"] = wqb.reshape(MLA_Q_RANK, MLA_HEADS * LANE).astype(BF)
    wukv = p["mla_wukv"][l].reshape(MLA_KV_RANK, MLA_HEADS, MLA_NOPE + MLA_V)
    out["wk_c"] = jnp.pad(wukv[..., :MLA_NOPE], [(0, 0), (0, 0), (0, LANE - MLA_NOPE)]).reshape(MLA_KV_RANK, -1).astype(BF)
    out["wv_c"] = jnp.pad(wukv[..., MLA_NOPE:], [(0, 0), (0, 0), (0, LANE - MLA_V)]).reshape(MLA_KV_RANK, -1).astype(BF)
    wq, bq = cols(_O_NQ, NSA_HEADS * NSA_DIM)
    hd = NSA_DIM // 2
    pieces_w, pieces_b = [], []

    def add(w, b, slots, roped):
        pieces_w.append(_pad_slots(w, slots, NSA_DIM))
        pieces_b.append(_pad_slots(b, slots, NSA_DIM))
        if roped:
            pieces_w.append(_pad_slots(_rot_slots(w, slots, NSA_DIM, hd), slots, NSA_DIM))
            pieces_b.append(_pad_slots(_rot_slots(b, slots, NSA_DIM, hd), slots, NSA_DIM))

    add(wq, bq, NSA_HEADS, True)
    wkc, bkc = cols(_O_NKC, 128)
    wvc, bvc = cols(_O_NVC, 128)
    pieces_w += [wkc, wvc]
    pieces_b += [bkc, bvc]
    add(*cols(_O_NKS, 128), NSA_GROUPS, True)
    add(*cols(_O_NVS, 128), NSA_GROUPS, False)
    add(*cols(_O_NKW, 128), NSA_GROUPS, True)
    add(*cols(_O_NVW, 128), NSA_GROUPS, False)
    wg, bg = cols(_O_NGATE, NSA_HEADS * 3)
    pieces_w.append(_pad_slots(wg, NSA_GROUPS, NSA_HPG * 3))
    pieces_b.append(_pad_slots(bg, NSA_GROUPS, NSA_HPG * 3))
    out["w_d"] = jnp.concatenate(pieces_w, axis=1).astype(BF)
    out["b_d"] = jnp.concatenate(pieces_b)[None, :]

    def cmp_weights(w):
        eye = jnp.eye(NSA_GROUPS, dtype=F32)
        wp = jnp.pad(w, [(0, 0), (0, 0), (0, LANE - NSA_DIM)])
        full = jnp.einsum("lde,gh->lgdhe", wp, eye).reshape(CMP_BLOCK, NSA_GROUPS * NSA_DIM, NSA_GROUPS * LANE)
        return full.reshape(2, CMP_STRIDE * NSA_GROUPS * NSA_DIM, NSA_GROUPS * LANE).astype(BF)

    wck = p["nsa_wcmp_k"][l]
    out["wcmp_k"] = cmp_weights(wck)
    out["wcmp_kr"] = cmp_weights(_rot_cols(wck, hd))
    out["wcmp_v"] = cmp_weights(p["nsa_wcmp_v"][l])

    def pe_rows(pe):
        t = jnp.broadcast_to(pe[:, None, :], (CMP_BLOCK, NSA_GROUPS, NSA_DIM))
        return t.reshape(2, CMP_STRIDE * NSA_GROUPS * NSA_DIM)

    out["pe"] = jnp.concatenate([pe_rows(p["nsa_pe_k"][l]), pe_rows(p["nsa_pe_v"][l])], axis=0)
    out["w_g"] = w_in[:, _O_GC:].astype(BF)
    out["b_g"] = b_in[_O_GC:][None, :]
    out["wout_c"] = _pad_rows(p["mla_wout"][l], MLA_HEADS, MLA_V).astype(BF)
    out["wout_d"] = _pad_rows(p["nsa_wout"][l], NSA_HEADS, NSA_DIM).astype(BF)
    return out


def _tables(seq):
    pos = jnp.arange(seq, dtype=F32)
    c16, s16 = _rope_tab(pos, MLA_ROPE)
    one = jnp.ones((seq, MLA_NOPE), F32)
    zero = jnp.zeros((seq, MLA_NOPE), F32)
    tail = LANE - MLA_NOPE - MLA_ROPE
    cq = jnp.concatenate([one, c16, c16, jnp.ones((seq, tail), F32)], axis=1)
    sq = jnp.concatenate([zero, s16, s16, jnp.zeros((seq, tail), F32)], axis=1)
    ck = jnp.pad(jnp.concatenate([c16, c16], axis=1), [(0, 0), (0, LANE - MLA_ROPE)])
    sk = jnp.pad(jnp.concatenate([s16, s16], axis=1), [(0, 0), (0, LANE - MLA_ROPE)])
    c32, s32 = _rope_tab(pos, NSA_DIM)
    cn = jnp.pad(jnp.concatenate([c32, c32], axis=1), [(0, 0), (0, LANE - NSA_DIM)])
    sn = jnp.pad(jnp.concatenate([s32, s32], axis=1), [(0, 0), (0, LANE - NSA_DIM)])
    n16 = seq // CMP_STRIDE
    cend = (jnp.arange(n16) * CMP_STRIDE + CMP_BLOCK - 1).astype(F32)
    cc32, cs32 = _rope_tab(cend, NSA_DIM)
    ccg = jnp.pad(jnp.concatenate([cc32, cc32], axis=1), [(0, 0), (0, LANE - NSA_DIM)])
    csg = jnp.pad(jnp.concatenate([cs32, cs32], axis=1), [(0, 0), (0, LANE - NSA_DIM)])
    cc = jnp.concatenate([ccg] * NSA_GROUPS, axis=1)
    cs = jnp.concatenate([csg] * NSA_GROUPS, axis=1)
    n_cmp = (seq - CMP_BLOCK) // CMP_STRIDE + 1
    n_slc = seq // SLC_BLOCK
    cstart = jnp.arange(n16) * CMP_STRIDE
    sstart = jnp.arange(n_slc) * SLC_BLOCK
    ovl = (jnp.minimum(cstart[None, :] + CMP_BLOCK, sstart[:, None] + SLC_BLOCK)
           - jnp.maximum(cstart[None, :], sstart[:, None]))
    ovl = jnp.clip(ovl, 0).astype(F32) / CMP_BLOCK
    ovl = jnp.where(jnp.arange(n16)[None, :] < n_cmp, ovl, 0.0).astype(BF)
    pk = jnp.zeros((LANE, MLA_HEADS, LANE), F32)
    pk = pk.at[jnp.arange(MLA_ROPE), :, MLA_NOPE + jnp.arange(MLA_ROPE)].set(1.0)
    pk = pk.reshape(LANE, MLA_HEADS * LANE).astype(BF)
    return dict(cq=cq, sq=sq, ck=ck, sk=sk, cn=cn, sn=sn, cc=cc, cs=cs, ovl=ovl, pk=pk)


def kernel(x, mem, ffn1_w1, ffn1_w3, ffn1_w2, ln1_g, ln1_b, w_in, b_in, gmlp_ln_g, gmlp_ln_b, gmlp_ws, gmlp_bs, gmlp_wout, conv_w, conv_wout, mla_qnorm_g, mla_kvnorm_g, mla_wuq, mla_wukv, mla_wout, nsa_pe_k, nsa_pe_v, nsa_wcmp_k, nsa_wcmp_v, nsa_wout, w_o, ln2_g, ln2_b, xattn_wq, xattn_wk, xattn_wv, xattn_wo, ln3_g, ln3_b, ffn2_w1, ffn2_w3, ffn2_w2, ln4_g, ln4_b):
    bsz, seq, d = x.shape
    mlen = mem.shape[1]
    n = bsz * seq
    assert d == D_MODEL and seq % 512 == 0 and seq >= WINDOW + 256
    p = dict(w_in=w_in, b_in=b_in, mla_wuq=mla_wuq, mla_wukv=mla_wukv, mla_wout=mla_wout,
             nsa_pe_k=nsa_pe_k, nsa_pe_v=nsa_pe_v, nsa_wcmp_k=nsa_wcmp_k, nsa_wcmp_v=nsa_wcmp_v, nsa_wout=nsa_wout)
    tb = _tables(seq)
    tm = 512
    tm_ffn = 1024
    tf = D_FF // 11
    n16 = seq // CMP_STRIDE
    top_k = min(SLC_TOPK, seq // SLC_BLOCK)
    row = lambda a: a[None, :]

    h = x.reshape(n, d)
    mem2 = mem.reshape(bsz * mlen, d)
    for l in range(DEPTH):
        lp = _layer_params(l, p)
        h = _ffn_ln(h, ffn1_w1[l].astype(BF), ffn1_w3[l].astype(BF), ffn1_w2[l].astype(BF),
                    row(ln1_g[l]), row(ln1_b[l]), tm=tm_ffn, tf=tf)
        ab = _mix_ab(h, lp["w_ab"], lp["b_ab"], row(gmlp_ln_g[l]), row(gmlp_ln_b[l]), gmlp_ws[l], gmlp_bs[l].T,
                     gmlp_wout[l].astype(BF), conv_w[l], conv_wout[l].astype(BF), tm=tm, seq=seq)
        qc, kc_, vc_ = _mla_proj(h, lp["w_c"], lp["b_c"], row(mla_qnorm_g[l]), row(mla_kvnorm_g[l]),
                                 lp["wqa"], lp["wqb"], lp["wk_c"], lp["wv_c"], tb["pk"],
                                 tb["cq"], tb["sq"], tb["ck"], tb["sk"], tm=tm, seq=seq)
        wide = MLA_HEADS * LANE
        oc = _flash_causal(qc.reshape(bsz, seq, wide), kc_.reshape(bsz, seq, wide), vc_.reshape(bsz, seq, wide), tq=512, tk=1024, hp=4)
        qn, nkc, nvc, nks, nvs, nkw, nvw, gates = _nsa_proj(h, lp["w_d"], lp["b_d"], tb["cn"], tb["sn"], tm=tm, seq=seq)
        kcmp, vcmp = _nsa_compress(nkc.reshape(bsz, n16, CMP_STRIDE * 128), nvc.reshape(bsz, n16, CMP_STRIDE * 128),
                                   lp["pe"], lp["wcmp_k"], lp["wcmp_kr"], lp["wcmp_v"], tb["cc"], tb["cs"])
        gw = NSA_GROUPS * LANE
        od = _nsa_attention(qn.reshape(bsz, seq, NSA_HEADS * LANE), kcmp, vcmp,
                            nks.reshape(bsz, seq, gw), nvs.reshape(bsz, seq, gw),
                            nkw.reshape(bsz, seq, gw), nvw.reshape(bsz, seq, gw),
                            gates.reshape(bsz, seq, gw), tb["ovl"], tq=256, tk=512, top_k=top_k)
        h = _merge_ln(h, ab, oc.reshape(n, wide), od.reshape(n, NSA_HEADS * LANE), lp["w_g"], lp["b_g"],
                      lp["wout_c"], lp["wout_d"], w_o[l].astype(BF), row(ln2_g[l]), row(ln2_b[l]), tm=tm)
        kv = _linear(mem2, jnp.concatenate([xattn_wk[l], xattn_wv[l]], axis=1).astype(BF), tm=min(256, bsz * mlen), dtype=BF)
        h = _xattn_ln(h, kv.reshape(bsz, mlen, 2 * XATTN_HEADS * XATTN_DIM), xattn_wq[l].astype(BF),
                      xattn_wo[l].astype(BF), row(ln3_g[l]), row(ln3_b[l]), tm=tm, seq=seq)
        h = _ffn_ln(h, ffn2_w1[l].astype(BF), ffn2_w3[l].astype(BF), ffn2_w2[l].astype(BF),
                    row(ln4_g[l]), row(ln4_b[l]), tm=tm_ffn, tf=tf)
    return h.reshape(bsz, seq, d)
```

```python
import functools

import jax
import jax.numpy as jnp
from jax import lax
from jax.experimental import pallas as pl
from jax.experimental.pallas import tpu as pltpu

BF = jnp.bfloat16
F32 = jnp.float32

D_MODEL = 1024
D_FF = 2816
LN_EPS = 1e-5
RMS_EPS = 1e-6
ROPE_THETA = 10000.0
DEPTH = 2
ALPHA = (2 * DEPTH) ** 0.25
NEG = -1e30
LOG2_E = 1.4426950408889634
DENOM_LANE = 64
MASK_BIG = 2.0 ** 100

GMLP_CHUNK = 128
GMLP_GROUPS = 4
GMLP_WIDTH = 512
CONV_WIDTH = 512
CONV_K = 3
MLA_HEADS = 8
MLA_Q_RANK = 256
MLA_KV_RANK = 128
MLA_NOPE = 64
MLA_ROPE = 32
MLA_V = 64
NSA_HEADS = 8
NSA_GROUPS = 2
NSA_HPG = 4
NSA_DIM = 64
CMP_BLOCK = 32
CMP_STRIDE = 16
SLC_BLOCK = 64
SLC_TOPK = 8
WINDOW = 512
XATTN_HEADS = 4
XATTN_DIM = 128

LANE = 128
CONV_HALO = 8
VMEM_LIMIT = 56 * 1024 * 1024

_O_U, _O_V, _O_CB, _O_CC, _O_CH = 0, 512, 1024, 1536, 2048
_O_QLAT, _O_KVLAT, _O_KROPE = 2560, 2816, 2944
_O_NQ, _O_NKC, _O_NVC, _O_NKS, _O_NVS, _O_NKW, _O_NVW, _O_NGATE = 2976, 3488, 3616, 3744, 3872, 4000, 4128, 4256
_O_GA, _O_GB, _O_GC, _O_GD = 4280, 5304, 6328, 7352


def _dot(a, b):
    return jnp.dot(a, b, preferred_element_type=F32)


def _dot_t(a, b):
    return lax.dot_general(a, b, (((1,), (1,)), ((), ())), preferred_element_type=F32)


def _ln(y, g, b):
    mu = jnp.mean(y, -1, keepdims=True)
    d = y - mu
    var = jnp.mean(d * d, -1, keepdims=True)
    return d * lax.rsqrt(var + LN_EPS) * g + b


def _rms(x, g):
    return x * lax.rsqrt(jnp.mean(x * x, -1, keepdims=True) + RMS_EPS) * g


def _resident(shape):
    n = len(shape)
    return pl.BlockSpec(shape, lambda *_: (0,) * n, pipeline_mode=pl.Buffered(1))


def _params(sem):
    return pltpu.CompilerParams(dimension_semantics=sem, vmem_limit_bytes=VMEM_LIMIT)


def _ffn_ln_kernel(x_ref, w1_ref, w3_ref, w2_ref, g_ref, b_ref, o_ref, *, tf):
    x = x_ref[...]
    xb = x.astype(BF)
    acc = None
    for c in range(w1_ref.shape[1] // tf):
        cols = slice(c * tf, (c + 1) * tf)
        h1 = _dot(xb, w1_ref[:, cols])
        h3 = _dot(xb, w3_ref[:, cols])
        hh = (h1 * jax.nn.sigmoid(h1)) * h3
        part = _dot(hh.astype(BF), w2_ref[cols, :])
        acc = part if acc is None else acc + part
    o_ref[...] = _ln(ALPHA * x + 0.5 * acc, g_ref[...], b_ref[...])


def _ffn_ln(x, w1, w3, w2, g, b, *, tm, tf):
    n, d = x.shape
    return pl.pallas_call(
        functools.partial(_ffn_ln_kernel, tf=tf),
        grid=(n // tm,),
        in_specs=[pl.BlockSpec((tm, d), lambda i: (i, 0))] + [_resident(a.shape) for a in (w1, w3, w2, g, b)],
        out_specs=pl.BlockSpec((tm, d), lambda i: (i, 0)),
        out_shape=jax.ShapeDtypeStruct((n, d), F32),
        compiler_params=_params(("parallel",)),
        name="ffn_ln",
    )(x, w1, w3, w2, g, b)


def _ab_kernel(h_ref, w_ref, b_ref, lng_ref, lnb_ref, ws_ref, bst_ref, wga_ref, cw_ref, wcb_ref,
               o_ref, prev_ref, *, tiles_per_seq):
    i = pl.program_id(0)
    tm = h_ref.shape[0]
    hb = h_ref[...].astype(BF)

    def proj(c0, width):
        return _dot(hb, w_ref[:, c0:c0 + width]) + b_ref[:, c0:c0 + width]

    u = proj(0, GMLP_WIDTH)
    v = _ln(proj(512, GMLP_WIDTH), lng_ref[...], lnb_ref[...]).astype(BF)
    row = lax.broadcasted_iota(jnp.int32, (GMLP_CHUNK, GMLP_CHUNK), 0)
    col = lax.broadcasted_iota(jnp.int32, (GMLP_CHUNK, GMLP_CHUNK), 1)
    gd = GMLP_WIDTH // GMLP_GROUPS
    wgs = [jnp.where(row >= col, ws_ref[g], 0.0).astype(BF) for g in range(GMLP_GROUPS)]
    chunks = []
    for c in range(tm // GMLP_CHUNK):
        r0 = c * GMLP_CHUNK
        chunks.append(jnp.concatenate(
            [_dot(wgs[g], v[r0:r0 + GMLP_CHUNK, g * gd:(g + 1) * gd]) + bst_ref[:, g:g + 1]
             for g in range(GMLP_GROUPS)], axis=1))
    s = jnp.concatenate(chunks, axis=0)
    ya = _dot((u * s).astype(BF), wga_ref[...])

    cb = proj(1024, CONV_WIDTH)
    z = proj(1536, CONV_WIDTH) * proj(2048, CONV_WIDTH)

    @pl.when(i % tiles_per_seq == 0)
    def _():
        prev_ref[...] = jnp.zeros_like(prev_ref)

    zext = jnp.concatenate([prev_ref[...], z], axis=0)
    z1 = pltpu.roll(zext, 1, 0)[CONV_HALO:]
    z2 = pltpu.roll(zext, 2, 0)[CONV_HALO:]
    y = cw_ref[0:1, :] * z2 + cw_ref[1:2, :] * z1 + cw_ref[2:3, :] * z
    prev_ref[...] = z[tm - CONV_HALO:, :]
    yb = _dot((cb * y).astype(BF), wcb_ref[...])

    ga = proj(2560, D_MODEL)
    gb = proj(3584, D_MODEL)
    o_ref[...] = jax.nn.sigmoid(ga) * ya + jax.nn.sigmoid(gb) * yb


def _mix_ab(h, w, b, lng, lnb, ws, bst, wga, cw, wcb, *, tm, seq):
    n, d = h.shape
    kern = functools.partial(_ab_kernel, tiles_per_seq=seq // tm)
    return pl.pallas_call(
        kern,
        grid=(n // tm,),
        in_specs=[pl.BlockSpec((tm, d), lambda i: (i, 0))] + [_resident(a.shape) for a in (w, b, lng, lnb, ws, bst, wga, cw, wcb)],
        out_specs=pl.BlockSpec((tm, d), lambda i: (i, 0)),
        out_shape=jax.ShapeDtypeStruct((n, d), F32),
        scratch_shapes=[pltpu.VMEM((CONV_HALO, CONV_WIDTH), F32)],
        compiler_params=_params(("arbitrary",)),
        name="mix_ab",
    )(h, w, b, lng, lnb, ws, bst, wga, cw, wcb)


def _mla_proj_kernel(h_ref, w_ref, b_ref, qg_ref, kvg_ref, wqa_ref, wqb_ref, wk_ref, wv_ref, pk_ref,
                     cq_ref, sq_ref, ck_ref, sk_ref, q_ref, k_ref, v_ref):
    hb = h_ref[...].astype(BF)
    z = _dot(hb, w_ref[...]) + b_ref[...]
    qn = _rms(z[:, 0:256], qg_ref[...]).astype(BF)
    kvn = _rms(z[:, 256:384], kvg_ref[...]).astype(BF)
    cq = jnp.concatenate([cq_ref[...]] * MLA_HEADS, axis=1)
    sq = jnp.concatenate([sq_ref[...]] * MLA_HEADS, axis=1)
    scale = (MLA_NOPE + MLA_ROPE) ** -0.5 * LOG2_E
    q = (_dot(qn, wqa_ref[...]) * cq + _dot(qn, wqb_ref[...]) * sq) * scale
    q_ref[...] = q.astype(BF)
    kpe = (z[:, 384:512] * ck_ref[...] + z[:, 512:640] * sk_ref[...]).astype(BF)
    k_ref[...] = (_dot(kvn, wk_ref[...]) + _dot(kpe, pk_ref[...])).astype(BF)
    v_ref[...] = (_dot(kvn, wv_ref[...]) + _denom_ones(v_ref.shape[1])).astype(BF)


def _mla_proj(h, w, b, qg, kvg, wqa, wqb, wk, wv, pk, cq, sq, ck, sk, *, tm, seq):
    n, d = h.shape
    tps = seq // tm
    tab = pl.BlockSpec((tm, LANE), lambda i: (i % tps, 0))
    wide = MLA_HEADS * LANE
    out = jax.ShapeDtypeStruct((n, wide), BF)
    return pl.pallas_call(
        _mla_proj_kernel,
        grid=(n // tm,),
        in_specs=[pl.BlockSpec((tm, d), lambda i: (i, 0))]
        + [_resident(a.shape) for a in (w, b, qg, kvg, wqa, wqb, wk, wv, pk)] + [tab] * 4,
        out_specs=[pl.BlockSpec((tm, wide), lambda i: (i, 0))] * 3,
        out_shape=[out, out, out],
        compiler_params=_params(("parallel",)),
        name="mla_proj",
    )(h, w, b, qg, kvg, wqa, wqb, wk, wv, pk, cq, sq, ck, sk)


def _online_softmax_step(s, v, carry):
    m, acc = carry
    m_new = jnp.maximum(m, jnp.max(s, -1, keepdims=True))
    p = jnp.exp2(s - m_new)
    acc = jnp.exp2(m - m_new) * acc + _dot(p.astype(BF), v)
    return m_new, acc


def _softmax_init(rows, width):
    return (jnp.full((rows, 1), NEG, F32), jnp.zeros((rows, width), F32))


def _normalize(acc):
    return acc * (1.0 / acc[:, DENOM_LANE:DENOM_LANE + 1])


def _denom_ones(width):
    lane = lax.broadcasted_iota(jnp.int32, (1, width), 1)
    return jnp.where(lane % LANE == DENOM_LANE, 1.0, 0.0)


def _flash_kernel(q_ref, k_ref, v_ref, o_ref, *, tq, tk, hp):
    qi = pl.program_id(2)
    q0 = qi * tq
    qs = [q_ref[0, :, h * LANE:(h + 1) * LANE] for h in range(hp)]

    def tile(j, carries, width, masked):
        k0 = pl.multiple_of(j * width, width)
        out = []
        for h in range(hp):
            s = _dot_t(qs[h], k_ref[0, pl.ds(k0, width), h * LANE:(h + 1) * LANE])
            if masked:
                r = lax.broadcasted_iota(jnp.int32, (tq, width), 0) + q0
                c = lax.broadcasted_iota(jnp.int32, (tq, width), 1) + k0
                s = jnp.where(c <= r, s, NEG)
            out.append(_online_softmax_step(s, v_ref[0, pl.ds(k0, width), h * LANE:(h + 1) * LANE], carries[h]))
        return tuple(out)

    n_wide = q0 // tk
    init = tuple(_softmax_init(tq, LANE) for _ in range(hp))
    carries = lax.fori_loop(0, n_wide, lambda j, c: tile(j, c, tk, False), init)
    carries = lax.fori_loop(n_wide * (tk // tq), qi + 1, lambda j, c: tile(j, c, tq, True), carries)
    o_ref[0] = jnp.concatenate([_normalize(acc) for (_, acc) in carries], axis=1).astype(o_ref.dtype)


def _flash_causal(q, k, v, *, tq, tk, hp):
    bsz, seq, wide = q.shape
    heads = wide // LANE
    assert tk % tq == 0 and seq % tq == 0
    kern = functools.partial(_flash_kernel, tq=tq, tk=tk, hp=hp)
    return pl.pallas_call(
        kern,
        grid=(bsz, heads // hp, seq // tq),
        in_specs=[
            pl.BlockSpec((1, tq, hp * LANE), lambda b, h, i: (b, i, h)),
            pl.BlockSpec((1, seq, hp * LANE), lambda b, h, i: (b, 0, h)),
            pl.BlockSpec((1, seq, hp * LANE), lambda b, h, i: (b, 0, h)),
        ],
        out_specs=pl.BlockSpec((1, tq, hp * LANE), lambda b, h, i: (b, i, h)),
        out_shape=jax.ShapeDtypeStruct((bsz, seq, wide), BF),
        compiler_params=_params(("parallel", "parallel", "arbitrary")),
        name="mla_flash",
    )(q, k, v)


def _nsa_proj_kernel(h_ref, w_ref, b_ref, c_ref, s_ref, q_ref, kc_ref, vc_ref, ks_ref, vs_ref, kw_ref, vw_ref, g_ref,
                     *, tiles_per_seq):
    tm = h_ref.shape[0]
    hb = h_ref[...].astype(BF)
    pos = (pl.program_id(0) % tiles_per_seq) * tm + lax.broadcasted_iota(jnp.int32, (tm, LANE), 0)
    lane = lax.broadcasted_iota(jnp.int32, (tm, LANE), 1)
    tag = jnp.where(lane == NSA_DIM + lax.shift_right_logical(pos, 6), MASK_BIG, 0.0)
    tag2 = jnp.concatenate([tag] * NSA_GROUPS, axis=1)

    def proj(c0, width):
        return _dot(hb, w_ref[:, c0:c0 + width]) + b_ref[:, c0:c0 + width]

    c = c_ref[...]
    s = s_ref[...]
    c8 = jnp.concatenate([c] * NSA_HEADS, axis=1)
    s8 = jnp.concatenate([s] * NSA_HEADS, axis=1)
    c2 = jnp.concatenate([c] * NSA_GROUPS, axis=1)
    s2 = jnp.concatenate([s] * NSA_GROUPS, axis=1)
    q = (proj(0, 1024) * c8 + proj(1024, 1024) * s8) * (NSA_DIM ** -0.5 * LOG2_E)
    q_ref[...] = q.astype(BF)
    kc_ref[...] = proj(2048, 128)
    vc_ref[...] = proj(2176, 128)
    ks_ref[...] = (proj(2304, 256) * c2 + proj(2560, 256) * s2 + tag2).astype(BF)
    ones = _denom_ones(NSA_GROUPS * LANE)
    vs_ref[...] = (proj(2816, 256) + ones).astype(BF)
    kw_ref[...] = (proj(3072, 256) * c2 + proj(3328, 256) * s2).astype(BF)
    vw_ref[...] = (proj(3584, 256) + ones).astype(BF)
    g_ref[...] = jax.nn.sigmoid(proj(3840, 256))


def _nsa_proj(h, w, b, cn, sn, *, tm, seq):
    n, d = h.shape
    tps = seq // tm
    tab = pl.BlockSpec((tm, LANE), lambda i: (i % tps, 0))

    def out(width, dt):
        return pl.BlockSpec((tm, width), lambda i: (i, 0)), jax.ShapeDtypeStruct((n, width), dt)

    outs = [out(1024, BF), out(128, F32), out(128, F32), out(256, BF), out(256, BF), out(256, BF), out(256, BF), out(256, F32)]
    assert seq // SLC_BLOCK <= LANE - NSA_DIM
    return pl.pallas_call(
        functools.partial(_nsa_proj_kernel, tiles_per_seq=tps),
        grid=(n // tm,),
        in_specs=[pl.BlockSpec((tm, d), lambda i: (i, 0)), _resident(w.shape), _resident(b.shape), tab, tab],
        out_specs=[o[0] for o in outs],
        out_shape=[o[1] for o in outs],
        compiler_params=_params(("parallel",)),
        name="nsa_proj",
    )(h, w, b, cn, sn)


def _nsa_cmp_kernel(kc_ref, vc_ref, pe_ref, wk_ref, wkr_ref, wv_ref, c_ref, s_ref, kcmp_ref, vcmp_ref):
    n16 = kc_ref.shape[1]

    def halves(x_ref, pe_lo, pe_hi):
        a = x_ref[0]
        nxt = pltpu.roll(a, n16 - 1, 0)
        return (a + pe_lo).astype(BF), (nxt + pe_hi).astype(BF)

    klo, khi = halves(kc_ref, pe_ref[0:1, :], pe_ref[1:2, :])
    kc = _dot(klo, wk_ref[0]) + _dot(khi, wk_ref[1])
    kcr = _dot(klo, wkr_ref[0]) + _dot(khi, wkr_ref[1])
    kcmp_ref[0] = (kc * c_ref[...] + kcr * s_ref[...]).astype(BF)
    vlo, vhi = halves(vc_ref, pe_ref[2:3, :], pe_ref[3:4, :])
    vcmp_ref[0] = (_dot(vlo, wv_ref[0]) + _dot(vhi, wv_ref[1])).astype(BF)


def _nsa_compress(kc, vc, pe, wk, wkr, wv, cc, sc):
    bsz, n16, wide = kc.shape
    blk = pl.BlockSpec((1, n16, wide), lambda b: (b, 0, 0))
    oblk = pl.BlockSpec((1, n16, NSA_GROUPS * LANE), lambda b: (b, 0, 0))
    osh = jax.ShapeDtypeStruct((bsz, n16, NSA_GROUPS * LANE), BF)
    return pl.pallas_call(
        _nsa_cmp_kernel,
        grid=(bsz,),
        in_specs=[blk, blk] + [_resident(a.shape) for a in (pe, wk, wkr, wv, cc, sc)],
        out_specs=[oblk, oblk],
        out_shape=[osh, osh],
        compiler_params=_params(("parallel",)),
        name="nsa_compress",
    )(kc, vc, pe, wk, wkr, wv, cc, sc)


def _nsa_attn_kernel(q_ref, kcmp_ref, vcmp_ref, ks_ref, vs_ref, kw_ref, vw_ref, g_ref, ov_ref, o_ref, *, tk, top_k):
    qi = pl.program_id(1)
    T = q_ref.shape[1]
    R = NSA_HPG * T
    G = NSA_GROUPS
    q0 = qi * T
    qpos = lax.broadcasted_iota(jnp.int32, (T, 1), 0) + q0
    ncp = kcmp_ref.shape[1]
    nb = ov_ref.shape[0]
    ov_t = ov_ref[...]

    def add_per_query(x, b):
        w = x.shape[1]
        return (x.reshape(NSA_HPG, T, w) + b[None]).reshape(R, w)

    def lanes(g):
        return slice(g * LANE, (g + 1) * LANE)

    q4 = [jnp.concatenate([q_ref[0, :, (g * NSA_HPG + h) * LANE:(g * NSA_HPG + h + 1) * LANE]
                           for h in range(NSA_HPG)], axis=0) for g in range(G)]

    cmp_end = lax.broadcasted_iota(jnp.int32, (1, ncp), 1) * CMP_STRIDE + (CMP_BLOCK - 1)
    cbias = jnp.where(cmp_end <= qpos, 0.0, NEG)
    any_valid = jnp.where(qpos >= CMP_BLOCK - 1, 1.0, 0.0)
    jr = lax.broadcasted_iota(jnp.int32, (nb, 1), 0)
    jrf = jr.astype(F32)
    jq = lax.shift_right_logical(lax.broadcasted_iota(jnp.int32, (1, T), 1) + q0, 6)
    forced = (jr == 0) | (jr == jq) | (jr == jq - 1)
    eye_t = jnp.where(lax.broadcasted_iota(jnp.int32, (T, T), 0) == lax.broadcasted_iota(jnp.int32, (T, T), 1),
                      1.0, 0.0).astype(BF)
    o_cmp, q4s = [], []
    for g in range(G):
        sm = _dot_t(q4[g], kcmp_ref[0, :, lanes(g)]).reshape(NSA_HPG, T, ncp) + cbias[None]
        e = jnp.exp2(sm - jnp.max(sm, -1, keepdims=True))
        p = e * (any_valid[None] / jnp.sum(e, -1, keepdims=True))
        o_cmp.append(_dot(p.reshape(R, ncp).astype(BF), vcmp_ref[0, :, lanes(g)]))
        psum = p[0] + p[1] + p[2] + p[3]
        hi = psum.astype(BF)
        r1 = psum - hi.astype(F32)
        mid = r1.astype(BF)
        lo = (r1 - mid.astype(F32)).astype(BF)
        imp = _dot_t(ov_t, hi) + _dot_t(ov_t, mid) + _dot_t(ov_t, lo)
        imp = jnp.where(forced, 1e9, imp)
        imp = jnp.where(jr <= jq, imp, -1.0)
        work = imp
        sel = jnp.zeros_like(imp)
        for _ in range(top_k):
            mx = jnp.max(work, 0, keepdims=True)
            idx = jnp.min(jnp.where(work == mx, jrf, float(nb)), 0, keepdims=True)
            pick = jrf == idx
            sel = jnp.where(pick, 1.0, sel)
            work = jnp.where(pick, -2.0, work)
        unsel_t = jnp.where(imp >= 0.0, sel, 0.0) - 1.0
        pad_t = [jnp.zeros((NSA_DIM, T), F32), unsel_t]
        if LANE - NSA_DIM - nb:
            pad_t.append(jnp.zeros((LANE - NSA_DIM - nb, T), F32))
        unsel = _dot_t(eye_t, jnp.concatenate(pad_t, axis=0).astype(BF)).astype(BF)
        q4s.append(add_per_query(q4[g], unsel))

    def slc_tile(j, carries, diagonal):
        k0 = pl.multiple_of(j * tk, tk)
        out = []
        for g in range(G):
            sc = _dot_t(q4s[g], ks_ref[0, pl.ds(k0, tk), lanes(g)])
            if diagonal:
                sc = add_per_query(sc, jnp.where((lax.broadcasted_iota(jnp.int32, (1, tk), 1) + k0) <= qpos, 0.0, NEG))
            out.append(_online_softmax_step(sc, vs_ref[0, pl.ds(k0, tk), lanes(g)], carries[g]))
        return tuple(out)

    j_last = q0 // tk
    carries = lax.fori_loop(0, j_last, lambda j, c: slc_tile(j, c, False),
                            tuple(_softmax_init(R, LANE) for _ in range(G)))
    carries = slc_tile(j_last, carries, True)

    wk = WINDOW + T
    w0 = pl.multiple_of(jnp.maximum(q0 - WINDOW, 0), T)
    dist = qpos - (lax.broadcasted_iota(jnp.int32, (1, wk), 1) + w0)
    wbias = jnp.where((dist >= 0) & (dist < WINDOW), 0.0, NEG)
    o_win = []
    for g in range(G):
        sc = add_per_query(_dot_t(q4[g], kw_ref[0, pl.ds(w0, wk), lanes(g)]), wbias)
        e = jnp.exp2(sc - jnp.max(sc, -1, keepdims=True))
        o_win.append(_normalize(_dot(e.astype(BF), vw_ref[0, pl.ds(w0, wk), lanes(g)])))

    gw = NSA_HPG * LANE
    e_row = lax.broadcasted_iota(jnp.int32, (LANE, 3 * gw), 0)
    e_col = lax.broadcasted_iota(jnp.int32, (LANE, 3 * gw), 1)
    branch = jnp.where(e_col >= 2 * gw, 2, jnp.where(e_col >= gw, 1, 0))
    head = lax.shift_right_logical(e_col - branch * gw, 7)
    expand = jnp.where(e_row == 3 * head + branch, 1.0, 0.0).astype(BF)

    def heads_on_lanes(x):
        return jnp.concatenate([x[h * T:(h + 1) * T] for h in range(NSA_HPG)], axis=1)

    outs = []
    for g in range(G):
        gs = g_ref[0, :, lanes(g)]
        hi = gs.astype(BF)
        r1 = gs - hi.astype(F32)
        mid = r1.astype(BF)
        lo = (r1 - mid.astype(F32)).astype(BF)
        gx = _dot(hi, expand) + _dot(mid, expand) + _dot(lo, expand)
        outs.append(gx[:, 0:gw] * heads_on_lanes(o_cmp[g])
                    + gx[:, gw:2 * gw] * heads_on_lanes(_normalize(carries[g][1]))
                    + gx[:, 2 * gw:3 * gw] * heads_on_lanes(o_win[g]))
    o_ref[0] = jnp.concatenate(outs, axis=1).astype(o_ref.dtype)


def _nsa_attention(q, kcmp, vcmp, ks, vs, kw, vw, gates, ov, *, tq, tk, top_k):
    bsz, seq, wide = q.shape
    n16 = kcmp.shape[1]
    gw = NSA_GROUPS * LANE
    kern = functools.partial(_nsa_attn_kernel, tk=tk, top_k=top_k)
    cblk = pl.BlockSpec((1, n16, gw), lambda b, i: (b, 0, 0))
    sblk = pl.BlockSpec((1, seq, gw), lambda b, i: (b, 0, 0))
    return pl.pallas_call(
        kern,
        grid=(bsz, seq // tq),
        in_specs=[pl.BlockSpec((1, tq, wide), lambda b, i: (b, i, 0)), cblk, cblk, sblk, sblk, sblk, sblk,
                  pl.BlockSpec((1, tq, gw), lambda b, i: (b, i, 0)), _resident(ov.shape)],
        out_specs=pl.BlockSpec((1, tq, wide), lambda b, i: (b, i, 0)),
        out_shape=jax.ShapeDtypeStruct((bsz, seq, wide), BF),
        compiler_params=_params(("parallel", "arbitrary")),
        name="nsa_attn",
    )(q, kcmp, vcmp, ks, vs, kw, vw, gates, ov)


def _merge_kernel(x_ref, ab_ref, oc_ref, od_ref, wg_ref, bg_ref, wc_ref, wd_ref, wo_ref, g_ref, b_ref, o_ref):
    x = x_ref[...]
    gates = _dot(x.astype(BF), wg_ref[...]) + bg_ref[...]
    yc = _dot(oc_ref[...], wc_ref[...])
    yd = _dot(od_ref[...], wd_ref[...])
    merged = ab_ref[...] + jax.nn.sigmoid(gates[:, :D_MODEL]) * yc + jax.nn.sigmoid(gates[:, D_MODEL:]) * yd
    mix = _dot(merged.astype(BF), wo_ref[...])
    o_ref[...] = _ln(ALPHA * x + mix, g_ref[...], b_ref[...])


def _merge_ln(x, ab, oc, od, wg, bg, wc, wd, wo, g, b, *, tm):
    n, d = x.shape
    row = pl.BlockSpec((tm, d), lambda i: (i, 0))
    return pl.pallas_call(
        _merge_kernel,
        grid=(n // tm,),
        in_specs=[row, row, row, row] + [_resident(a.shape) for a in (wg, bg, wc, wd, wo, g, b)],
        out_specs=row,
        out_shape=jax.ShapeDtypeStruct((n, d), F32),
        compiler_params=_params(("parallel",)),
        name="merge_ln",
    )(x, ab, oc, od, wg, bg, wc, wd, wo, g, b)


def _linear_kernel(x_ref, w_ref, o_ref):
    o_ref[...] = _dot(x_ref[...].astype(BF), w_ref[...]).astype(o_ref.dtype)


def _linear(x, w, *, tm, dtype):
    n, d = x.shape
    return pl.pallas_call(
        _linear_kernel,
        grid=(n // tm,),
        in_specs=[pl.BlockSpec((tm, d), lambda i: (i, 0)), _resident(w.shape)],
        out_specs=pl.BlockSpec((tm, w.shape[1]), lambda i: (i, 0)),
        out_shape=jax.ShapeDtypeStruct((n, w.shape[1]), dtype),
        compiler_params=_params(("parallel",)),
        name="mem_kv",
    )(x, w)


def _xattn_kernel(x_ref, k_ref, v_ref, wq_ref, wo_ref, g_ref, b_ref, o_ref):
    x = x_ref[...]
    q = _dot(x.astype(BF), wq_ref[...]).astype(BF)
    k = k_ref[0]
    v = v_ref[0]
    heads = []
    for h in range(XATTN_HEADS):
        sl = slice(h * XATTN_DIM, (h + 1) * XATTN_DIM)
        s = _dot_t(q[:, sl], k[:, sl]) * (XATTN_DIM ** -0.5)
        e = jnp.exp(s - jnp.max(s, -1, keepdims=True))
        p = e / jnp.sum(e, -1, keepdims=True)
        heads.append(_dot(p.astype(BF), v[:, sl]))
    o = jnp.concatenate(heads, axis=1).astype(BF)
    o_ref[...] = _ln(ALPHA * x + _dot(o, wo_ref[...]), g_ref[...], b_ref[...])


def _xattn_ln(x, kv, wq, wo, g, b, *, tm, seq):
    n, d = x.shape
    tps = seq // tm
    mlen = kv.shape[1]
    hd = XATTN_HEADS * XATTN_DIM
    return pl.pallas_call(
        _xattn_kernel,
        grid=(n // tm,),
        in_specs=[pl.BlockSpec((tm, d), lambda i: (i, 0)),
                  pl.BlockSpec((1, mlen, hd), lambda i: (i // tps, 0, 0)),
                  pl.BlockSpec((1, mlen, hd), lambda i: (i // tps, 0, 1))]
        + [_resident(a.shape) for a in (wq, wo, g, b)],
        out_specs=pl.BlockSpec((tm, d), lambda i: (i, 0)),
        out_shape=jax.ShapeDtypeStruct((n, d), F32),
        compiler_params=_params(("parallel",)),
        name="xattn_ln",
    )(x, kv, kv, wq, wo, g, b)


def _rope_tab(pos, dim):
    inv = ROPE_THETA ** (-(jnp.arange(0, dim, 2, dtype=F32) / dim))
    ang = pos[:, None] * inv[None, :]
    return jnp.cos(ang), jnp.sin(ang)


def _rot_cols(w, half):
    return jnp.concatenate([-w[..., half:2 * half], w[..., :half]], axis=-1)


def _pad_slots(w, n_slots, width):
    lead = w.shape[:-1]
    w = w.reshape(lead + (n_slots, width))
    w = jnp.pad(w, [(0, 0)] * len(lead) + [(0, 0), (0, LANE - width)])
    return w.reshape(lead + (n_slots * LANE,))


def _rot_slots(w, n_slots, width, half):
    lead = w.shape[:-1]
    w = w.reshape(lead + (n_slots, width))
    return _rot_cols(w, half).reshape(lead + (n_slots * width,))


def _pad_rows(w, n_slots, width):
    d = w.shape[-1]
    w = w.reshape(n_slots, width, d)
    w = jnp.pad(w, [(0, 0), (0, LANE - width), (0, 0)])
    return w.reshape(n_slots * LANE, d)


def _layer_params(l, p):
    w_in, b_in = p["w_in"][l], p["b_in"][l]

    def cols(o, wd):
        return w_in[:, o:o + wd], b_in[o:o + wd]

    out = {}
    out["w_ab"] = jnp.concatenate([w_in[:, 0:2560], w_in[:, _O_GA:_O_GC]], axis=1).astype(BF)
    out["b_ab"] = jnp.concatenate([b_in[0:2560], b_in[_O_GA:_O_GC]])[None, :]
    wkr, bkr = cols(_O_KROPE, MLA_ROPE)
    half = MLA_ROPE // 2
    padk = lambda a: jnp.pad(a, [(0, 0)] * (a.ndim - 1) + [(0, LANE - MLA_ROPE)])
    out["w_c"] = jnp.concatenate([w_in[:, _O_QLAT:_O_KROPE], padk(wkr), padk(_rot_cols(wkr, half))], axis=1).astype(BF)
    out["b_c"] = jnp.concatenate([b_in[_O_QLAT:_O_KROPE], padk(bkr), padk(_rot_cols(bkr, half))])[None, :]
    wuq = p["mla_wuq"][l].reshape(MLA_Q_RANK, MLA_HEADS, MLA_NOPE + MLA_ROPE)
    rope_rot = _rot_cols(wuq[..., MLA_NOPE:], half)
    wqa = jnp.pad(wuq, [(0, 0), (0, 0), (0, LANE - MLA_NOPE - MLA_ROPE)])
    wqb = jnp.pad(rope_rot, [(0, 0), (0, 0), (MLA_NOPE, LANE - MLA_NOPE - MLA_ROPE)])
    out["wqa"] = wqa.reshape(MLA_Q_RANK, MLA_HEADS * LANE).astype(BF)
    out["wqb"] = wqb.reshape(MLA_Q_RANK, MLA_HEADS * LANE).astype(BF)
    wukv = p["mla_wukv"][l].reshape(MLA_KV_RANK, MLA_HEADS, MLA_NOPE + MLA_V)
    out["wk_c"] = jnp.pad(wukv[..., :MLA_NOPE], [(0, 0), (0, 0), (0, LANE - MLA_NOPE)]).reshape(MLA_KV_RANK, -1).astype(BF)
    out["wv_c"] = jnp.pad(wukv[..., MLA_NOPE:], [(0, 0), (0, 0), (0, LANE - MLA_V)]).reshape(MLA_KV_RANK, -1).astype(BF)
    wq, bq = cols(_O_NQ, NSA_HEADS * NSA_DIM)
    hd = NSA_DIM // 2
    pieces_w, pieces_b = [], []

    def add(w, b, slots, roped):
        pieces_w.append(_pad_slots(w, slots, NSA_DIM))
        pieces_b.append(_pad_slots(b, slots, NSA_DIM))
        if roped:
            pieces_w.append(_pad_slots(_rot_slots(w, slots, NSA_DIM, hd), slots, NSA_DIM))
            pieces_b.append(_pad_slots(_rot_slots(b, slots, NSA_DIM, hd), slots, NSA_DIM))

    add(wq, bq, NSA_HEADS, True)
    wkc, bkc = cols(_O_NKC, 128)
    wvc, bvc = cols(_O_NVC, 128)
    pieces_w += [wkc, wvc]
    pieces_b += [bkc, bvc]
    add(*cols(_O_NKS, 128), NSA_GROUPS, True)
    add(*cols(_O_NVS, 128), NSA_GROUPS, False)
    add(*cols(_O_NKW, 128), NSA_GROUPS, True)
    add(*cols(_O_NVW, 128), NSA_GROUPS, False)
    wg, bg = cols(_O_NGATE, NSA_HEADS * 3)
    pieces_w.append(_pad_slots(wg, NSA_GROUPS, NSA_HPG * 3))
    pieces_b.append(_pad_slots(bg, NSA_GROUPS, NSA_HPG * 3))
    out["w_d"] = jnp.concatenate(pieces_w, axis=1).astype(BF)
    out["b_d"] = jnp.concatenate(pieces_b)[None, :]

    def cmp_weights(w):
        eye = jnp.eye(NSA_GROUPS, dtype=F32)
        wp = jnp.pad(w, [(0, 0), (0, 0), (0, LANE - NSA_DIM)])
        full = jnp.einsum("lde,gh->lgdhe", wp, eye).reshape(CMP_BLOCK, NSA_GROUPS * NSA_DIM, NSA_GROUPS * LANE)
        return full.reshape(2, CMP_STRIDE * NSA_GROUPS * NSA_DIM, NSA_GROUPS * LANE).astype(BF)

    wck = p["nsa_wcmp_k"][l]
    out["wcmp_k"] = cmp_weights(wck)
    out["wcmp_kr"] = cmp_weights(_rot_cols(wck, hd))
    out["wcmp_v"] = cmp_weights(p["nsa_wcmp_v"][l])

    def pe_rows(pe):
        t = jnp.broadcast_to(pe[:, None, :], (CMP_BLOCK, NSA_GROUPS, NSA_DIM))
        return t.reshape(2, CMP_STRIDE * NSA_GROUPS * NSA_DIM)

    out["pe"] = jnp.concatenate([pe_rows(p["nsa_pe_k"][l]), pe_rows(p["nsa_pe_v"][l])], axis=0)
    out["w_g"] = w_in[:, _O_GC:].astype(BF)
    out["b_g"] = b_in[_O_GC:][None, :]
    out["wout_c"] = _pad_rows(p["mla_wout"][l], MLA_HEADS, MLA_V).astype(BF)
    out["wout_d"] = _pad_rows(p["nsa_wout"][l], NSA_HEADS, NSA_DIM).astype(BF)
    return out


def _tables(seq):
    pos = jnp.arange(seq, dtype=F32)
    c16, s16 = _rope_tab(pos, MLA_ROPE)
    one = jnp.ones((seq, MLA_NOPE), F32)
    zero = jnp.zeros((seq, MLA_NOPE), F32)
    tail = LANE - MLA_NOPE - MLA_ROPE
    cq = jnp.concatenate([one, c16, c16, jnp.ones((seq, tail), F32)], axis=1)
    sq = jnp.concatenate([zero, s16, s16, jnp.zeros((seq, tail), F32)], axis=1)
    ck = jnp.pad(jnp.concatenate([c16, c16], axis=1), [(0, 0), (0, LANE - MLA_ROPE)])
    sk = jnp.pad(jnp.concatenate([s16, s16], axis=1), [(0, 0), (0, LANE - MLA_ROPE)])
    c32, s32 = _rope_tab(pos, NSA_DIM)
    cn = jnp.pad(jnp.concatenate([c32, c32], axis=1), [(0, 0), (0, LANE - NSA_DIM)])
    sn = jnp.pad(jnp.concatenate([s32, s32], axis=1), [(0, 0), (0, LANE - NSA_DIM)])
    n16 = seq // CMP_STRIDE
    cend = (jnp.arange(n16) * CMP_STRIDE + CMP_BLOCK - 1).astype(F32)
    cc32, cs32 = _rope_tab(cend, NSA_DIM)
    ccg = jnp.pad(jnp.concatenate([cc32, cc32], axis=1), [(0, 0), (0, LANE - NSA_DIM)])
    csg = jnp.pad(jnp.concatenate([cs32, cs32], axis=1), [(0, 0), (0, LANE - NSA_DIM)])
    cc = jnp.concatenate([ccg] * NSA_GROUPS, axis=1)
    cs = jnp.concatenate([csg] * NSA_GROUPS, axis=1)
    n_cmp = (seq - CMP_BLOCK) // CMP_STRIDE + 1
    n_slc = seq // SLC_BLOCK
    cstart = jnp.arange(n16) * CMP_STRIDE
    sstart = jnp.arange(n_slc) * SLC_BLOCK
    ovl = (jnp.minimum(cstart[None, :] + CMP_BLOCK, sstart[:, None] + SLC_BLOCK)
           - jnp.maximum(cstart[None, :], sstart[:, None]))
    ovl = jnp.clip(ovl, 0).astype(F32) / CMP_BLOCK
    ovl = jnp.where(jnp.arange(n16)[None, :] < n_cmp, ovl, 0.0).astype(BF)
    pk = jnp.zeros((LANE, MLA_HEADS, LANE), F32)
    pk = pk.at[jnp.arange(MLA_ROPE), :, MLA_NOPE + jnp.arange(MLA_ROPE)].set(1.0)
    pk = pk.reshape(LANE, MLA_HEADS * LANE).astype(BF)
    return dict(cq=cq, sq=sq, ck=ck, sk=sk, cn=cn, sn=sn, cc=cc, cs=cs, ovl=ovl, pk=pk)


def kernel(x, mem, ffn1_w1, ffn1_w3, ffn1_w2, ln1_g, ln1_b, w_in, b_in, gmlp_ln_g, gmlp_ln_b, gmlp_ws, gmlp_bs, gmlp_wout, conv_w, conv_wout, mla_qnorm_g, mla_kvnorm_g, mla_wuq, mla_wukv, mla_wout, nsa_pe_k, nsa_pe_v, nsa_wcmp_k, nsa_wcmp_v, nsa_wout, w_o, ln2_g, ln2_b, xattn_wq, xattn_wk, xattn_wv, xattn_wo, ln3_g, ln3_b, ffn2_w1, ffn2_w3, ffn2_w2, ln4_g, ln4_b):
    bsz, seq, d = x.shape
    mlen = mem.shape[1]
    n = bsz * seq
    assert d == D_MODEL and seq % 512 == 0 and seq >= WINDOW + 256
    p = dict(w_in=w_in, b_in=b_in, mla_wuq=mla_wuq, mla_wukv=mla_wukv, mla_wout=mla_wout,
             nsa_pe_k=nsa_pe_k, nsa_pe_v=nsa_pe_v, nsa_wcmp_k=nsa_wcmp_k, nsa_wcmp_v=nsa_wcmp_v, nsa_wout=nsa_wout)
    tb = _tables(seq)
    tm = 512
    tm_ffn = 1024
    tf = D_FF // 11
    n16 = seq // CMP_STRIDE
    top_k = min(SLC_TOPK, seq // SLC_BLOCK)
    row = lambda a: a[None, :]

    h = x.reshape(n, d)
    mem2 = mem.reshape(bsz * mlen, d)
    for l in range(DEPTH):
        lp = _layer_params(l, p)
        h = _ffn_ln(h, ffn1_w1[l].astype(BF), ffn1_w3[l].astype(BF), ffn1_w2[l].astype(BF),
                    row(ln1_g[l]), row(ln1_b[l]), tm=tm_ffn, tf=tf)
        ab = _mix_ab(h, lp["w_ab"], lp["b_ab"], row(gmlp_ln_g[l]), row(gmlp_ln_b[l]), gmlp_ws[l], gmlp_bs[l].T,
                     gmlp_wout[l].astype(BF), conv_w[l], conv_wout[l].astype(BF), tm=tm, seq=seq)
        qc, kc_, vc_ = _mla_proj(h, lp["w_c"], lp["b_c"], row(mla_qnorm_g[l]), row(mla_kvnorm_g[l]),
                                 lp["wqa"], lp["wqb"], lp["wk_c"], lp["wv_c"], tb["pk"],
                                 tb["cq"], tb["sq"], tb["ck"], tb["sk"], tm=tm, seq=seq)
        wide = MLA_HEADS * LANE
        oc = _flash_causal(qc.reshape(bsz, seq, wide), kc_.reshape(bsz, seq, wide), vc_.reshape(bsz, seq, wide), tq=512, tk=1024, hp=4)
        qn, nkc, nvc, nks, nvs, nkw, nvw, gates = _nsa_proj(h, lp["w_d"], lp["b_d"], tb["cn"], tb["sn"], tm=tm, seq=seq)
        kcmp, vcmp = _nsa_compress(nkc.reshape(bsz, n16, CMP_STRIDE * 128), nvc.reshape(bsz, n16, CMP_STRIDE * 128),
                                   lp["pe"], lp["wcmp_k"], lp["wcmp_kr"], lp["wcmp_v"], tb["cc"], tb["cs"])
        gw = NSA_GROUPS * LANE
        od = _nsa_attention(qn.reshape(bsz, seq, NSA_HEADS * LANE), kcmp, vcmp,
                            nks.reshape(bsz, seq, gw), nvs.reshape(bsz, seq, gw),
                            nkw.reshape(bsz, seq, gw), nvw.reshape(bsz, seq, gw),
                            gates.reshape(bsz, seq, gw), tb["ovl"], tq=256, tk=512, top_k=top_k)
        h = _merge_ln(h, ab, oc.reshape(n, wide), od.reshape(n, NSA_HEADS * LANE), lp["w_g"], lp["b_g"],
                      lp["wout_c"], lp["wout_d"], w_o[l].astype(BF), row(ln2_g[l]), row(ln2_b[l]), tm=tm)
        kv = _linear(mem2, jnp.concatenate([xattn_wk[l], xattn_wv[l]], axis=1).astype(BF), tm=min(256, bsz * mlen), dtype=BF)
        h = _xattn_ln(h, kv.reshape(bsz, mlen, 2 * XATTN_HEADS * XATTN_DIM), xattn_wq[l].astype(BF),
                      xattn_wo[l].astype(BF), row(ln3_g[l]), row(ln3_b[l]), tm=tm, seq=seq)
        h = _ffn_ln(h, ffn2_w1[l].astype(BF), ffn2_w3[l].astype(BF), ffn2_w2[l].astype(BF),
                    row(ln4_g[l]), row(ln4_b[l]), tm=tm_ffn, tf=tf)
    return h.reshape(bsz, seq, d)
```

```python
import functools

import jax
import jax.numpy as jnp
from jax import lax
from jax.experimental import pallas as pl
from jax.experimental.pallas import tpu as pltpu

BF = jnp.bfloat16
F32 = jnp.float32

D_MODEL = 1024
D_FF = 2816
LN_EPS = 1e-5
RMS_EPS = 1e-6
ROPE_THETA = 10000.0
DEPTH = 2
ALPHA = (2 * DEPTH) ** 0.25
NEG = -1e30
LOG2_E = 1.4426950408889634
DENOM_LANE = 64
MASK_BIG = 2.0 ** 100

GMLP_CHUNK = 128
GMLP_GROUPS = 4
GMLP_WIDTH = 512
CONV_WIDTH = 512
CONV_K = 3
MLA_HEADS = 8
MLA_Q_RANK = 256
MLA_KV_RANK = 128
MLA_NOPE = 64
MLA_ROPE = 32
MLA_V = 64
NSA_HEADS = 8
NSA_GROUPS = 2
NSA_HPG = 4
NSA_DIM = 64
CMP_BLOCK = 32
CMP_STRIDE = 16
SLC_BLOCK = 64
SLC_TOPK = 8
WINDOW = 512
XATTN_HEADS = 4
XATTN_DIM = 128

LANE = 128
CONV_HALO = 8
VMEM_LIMIT = 56 * 1024 * 1024

_O_U, _O_V, _O_CB, _O_CC, _O_CH = 0, 512, 1024, 1536, 2048
_O_QLAT, _O_KVLAT, _O_KROPE = 2560, 2816, 2944
_O_NQ, _O_NKC, _O_NVC, _O_NKS, _O_NVS, _O_NKW, _O_NVW, _O_NGATE = 2976, 3488, 3616, 3744, 3872, 4000, 4128, 4256
_O_GA, _O_GB, _O_GC, _O_GD = 4280, 5304, 6328, 7352


def _dot(a, b):
    return jnp.dot(a, b, preferred_element_type=F32)


def _dot_t(a, b):
    return lax.dot_general(a, b, (((1,), (1,)), ((), ())), preferred_element_type=F32)


def _ln(y, g, b):
    mu = jnp.mean(y, -1, keepdims=True)
    d = y - mu
    var = jnp.mean(d * d, -1, keepdims=True)
    return d * lax.rsqrt(var + LN_EPS) * g + b


def _rms(x, g):
    return x * lax.rsqrt(jnp.mean(x * x, -1, keepdims=True) + RMS_EPS) * g


def _resident(shape):
    n = len(shape)
    return pl.BlockSpec(shape, lambda *_: (0,) * n, pipeline_mode=pl.Buffered(1))


def _params(sem):
    return pltpu.CompilerParams(dimension_semantics=sem, vmem_limit_bytes=VMEM_LIMIT)


def _ffn_ln_kernel(x_ref, w1_ref, w3_ref, w2_ref, g_ref, b_ref, o_ref, *, tf):
    x = x_ref[...]
    xb = x.astype(BF)
    acc = None
    for c in range(w1_ref.shape[1] // tf):
        cols = slice(c * tf, (c + 1) * tf)
        h1 = _dot(xb, w1_ref[:, cols])
        h3 = _dot(xb, w3_ref[:, cols])
        hh = (h1 * jax.nn.sigmoid(h1)) * h3
        part = _dot(hh.astype(BF), w2_ref[cols, :])
        acc = part if acc is None else acc + part
    o_ref[...] = _ln(ALPHA * x + 0.5 * acc, g_ref[...], b_ref[...])


def _ffn_ln(x, w1, w3, w2, g, b, *, tm, tf):
    n, d = x.shape
    return pl.pallas_call(
        functools.partial(_ffn_ln_kernel, tf=tf),
        grid=(n // tm,),
        in_specs=[pl.BlockSpec((tm, d), lambda i: (i, 0))] + [_resident(a.shape) for a in (w1, w3, w2, g, b)],
        out_specs=pl.BlockSpec((tm, d), lambda i: (i, 0)),
        out_shape=jax.ShapeDtypeStruct((n, d), F32),
        compiler_params=_params(("parallel",)),
        name="ffn_ln",
    )(x, w1, w3, w2, g, b)


def _ab_kernel(h_ref, w_ref, b_ref, lng_ref, lnb_ref, ws_ref, bst_ref, wga_ref, cw_ref, wcb_ref,
               o_ref, prev_ref, *, tiles_per_seq):
    i = pl.program_id(0)
    tm = h_ref.shape[0]
    hb = h_ref[...].astype(BF)

    def proj(c0, width):
        return _dot(hb, w_ref[:, c0:c0 + width]) + b_ref[:, c0:c0 + width]

    u = proj(0, GMLP_WIDTH)
    v = _ln(proj(512, GMLP_WIDTH), lng_ref[...], lnb_ref[...]).astype(BF)
    row = lax.broadcasted_iota(jnp.int32, (GMLP_CHUNK, GMLP_CHUNK), 0)
    col = lax.broadcasted_iota(jnp.int32, (GMLP_CHUNK, GMLP_CHUNK), 1)
    gd = GMLP_WIDTH // GMLP_GROUPS
    wgs = [jnp.where(row >= col, ws_ref[g], 0.0).astype(BF) for g in range(GMLP_GROUPS)]
    chunks = []
    for c in range(tm // GMLP_CHUNK):
        r0 = c * GMLP_CHUNK
        chunks.append(jnp.concatenate(
            [_dot(wgs[g], v[r0:r0 + GMLP_CHUNK, g * gd:(g + 1) * gd]) + bst_ref[:, g:g + 1]
             for g in range(GMLP_GROUPS)], axis=1))
    s = jnp.concatenate(chunks, axis=0)
    ya = _dot((u * s).astype(BF), wga_ref[...])

    cb = proj(1024, CONV_WIDTH)
    z = proj(1536, CONV_WIDTH) * proj(2048, CONV_WIDTH)

    @pl.when(i % tiles_per_seq == 0)
    def _():
        prev_ref[...] = jnp.zeros_like(prev_ref)

    zext = jnp.concatenate([prev_ref[...], z], axis=0)
    z1 = pltpu.roll(zext, 1, 0)[CONV_HALO:]
    z2 = pltpu.roll(zext, 2, 0)[CONV_HALO:]
    y = cw_ref[0:1, :] * z2 + cw_ref[1:2, :] * z1 + cw_ref[2:3, :] * z
    prev_ref[...] = z[tm - CONV_HALO:, :]
    yb = _dot((cb * y).astype(BF), wcb_ref[...])

    ga = proj(2560, D_MODEL)
    gb = proj(3584, D_MODEL)
    o_ref[...] = jax.nn.sigmoid(ga) * ya + jax.nn.sigmoid(gb) * yb


def _mix_ab(h, w, b, lng, lnb, ws, bst, wga, cw, wcb, *, tm, seq):
    n, d = h.shape
    kern = functools.partial(_ab_kernel, tiles_per_seq=seq // tm)
    return pl.pallas_call(
        kern,
        grid=(n // tm,),
        in_specs=[pl.BlockSpec((tm, d), lambda i: (i, 0))] + [_resident(a.shape) for a in (w, b, lng, lnb, ws, bst, wga, cw, wcb)],
        out_specs=pl.BlockSpec((tm, d), lambda i: (i, 0)),
        out_shape=jax.ShapeDtypeStruct((n, d), F32),
        scratch_shapes=[pltpu.VMEM((CONV_HALO, CONV_WIDTH), F32)],
        compiler_params=_params(("arbitrary",)),
        name="mix_ab",
    )(h, w, b, lng, lnb, ws, bst, wga, cw, wcb)


def _mla_proj_kernel(h_ref, w_ref, b_ref, qg_ref, kvg_ref, wqa_ref, wqb_ref, wk_ref, wv_ref, pk_ref,
                     cq_ref, sq_ref, ck_ref, sk_ref, q_ref, k_ref, v_ref):
    hb = h_ref[...].astype(BF)
    z = _dot(hb, w_ref[...]) + b_ref[...]
    qn = _rms(z[:, 0:256], qg_ref[...]).astype(BF)
    kvn = _rms(z[:, 256:384], kvg_ref[...]).astype(BF)
    cq = jnp.concatenate([cq_ref[...]] * MLA_HEADS, axis=1)
    sq = jnp.concatenate([sq_ref[...]] * MLA_HEADS, axis=1)
    scale = (MLA_NOPE + MLA_ROPE) ** -0.5 * LOG2_E
    q = (_dot(qn, wqa_ref[...]) * cq + _dot(qn, wqb_ref[...]) * sq) * scale
    q_ref[...] = q.astype(BF)
    kpe = (z[:, 384:512] * ck_ref[...] + z[:, 512:640] * sk_ref[...]).astype(BF)
    k_ref[...] = (_dot(kvn, wk_ref[...]) + _dot(kpe, pk_ref[...])).astype(BF)
    v_ref[...] = (_dot(kvn, wv_ref[...]) + _denom_ones(v_ref.shape[1])).astype(BF)


def _mla_proj(h, w, b, qg, kvg, wqa, wqb, wk, wv, pk, cq, sq, ck, sk, *, tm, seq):
    n, d = h.shape
    tps = seq // tm
    tab = pl.BlockSpec((tm, LANE), lambda i: (i % tps, 0))
    wide = MLA_HEADS * LANE
    out = jax.ShapeDtypeStruct((n, wide), BF)
    return pl.pallas_call(
        _mla_proj_kernel,
        grid=(n // tm,),
        in_specs=[pl.BlockSpec((tm, d), lambda i: (i, 0))]
        + [_resident(a.shape) for a in (w, b, qg, kvg, wqa, wqb, wk, wv, pk)] + [tab] * 4,
        out_specs=[pl.BlockSpec((tm, wide), lambda i: (i, 0))] * 3,
        out_shape=[out, out, out],
        compiler_params=_params(("parallel",)),
        name="mla_proj",
    )(h, w, b, qg, kvg, wqa, wqb, wk, wv, pk, cq, sq, ck, sk)


def _online_softmax_step(s, v, carry):
    m, acc = carry
    m_new = jnp.maximum(m, jnp.max(s, -1, keepdims=True))
    p = jnp.exp2(s - m_new)
    acc = jnp.exp2(m - m_new) * acc + _dot(p.astype(BF), v)
    return m_new, acc


def _softmax_init(rows, width):
    return (jnp.full((rows, 1), NEG, F32), jnp.zeros((rows, width), F32))


def _normalize(acc):
    return acc * (1.0 / acc[:, DENOM_LANE:DENOM_LANE + 1])


def _denom_ones(width):
    lane = lax.broadcasted_iota(jnp.int32, (1, width), 1)
    return jnp.where(lane % LANE == DENOM_LANE, 1.0, 0.0)


def _flash_kernel(q_ref, k_ref, v_ref, o_ref, *, tq, tk, hp):
    qi = pl.program_id(2)
    q0 = qi * tq
    qs = [q_ref[0, :, h * LANE:(h + 1) * LANE] for h in range(hp)]

    def tile(j, carries, width, diagonal):
        k0 = pl.multiple_of(j * width, width)
        out = []
        for h in range(hp):
            s = _dot_t(qs[h], k_ref[0, pl.ds(k0, width), h * LANE:(h + 1) * LANE])
            if diagonal:
                r = lax.broadcasted_iota(jnp.int32, (tq, width), 0)
                c = lax.broadcasted_iota(jnp.int32, (tq, width), 1)
                s = jnp.where(c <= r, s, NEG)
            out.append(_online_softmax_step(s, v_ref[0, pl.ds(k0, width), h * LANE:(h + 1) * LANE], carries[h]))
        return tuple(out)

    n_wide = q0 // tk
    init = tuple(_softmax_init(tq, LANE) for _ in range(hp))
    carries = lax.fori_loop(0, n_wide, lambda j, c: tile(j, c, tk, False), init)
    carries = lax.fori_loop(n_wide * (tk // tq), qi, lambda j, c: tile(j, c, tq, False), carries)

    half = tq // 2
    carries = tile(2 * qi, carries, half, True)
    r = lax.broadcasted_iota(jnp.int32, (half, half), 0)
    c = lax.broadcasted_iota(jnp.int32, (half, half), 1)
    k1 = pl.multiple_of(q0 + half, half)
    out = []
    for h in range(hp):
        m, acc = carries[h]
        s = _dot_t(qs[h][half:], k_ref[0, pl.ds(k1, half), h * LANE:(h + 1) * LANE])
        m2, acc2 = _online_softmax_step(jnp.where(c <= r, s, NEG), v_ref[0, pl.ds(k1, half), h * LANE:(h + 1) * LANE],
                                        (m[half:], acc[half:]))
        out.append(jnp.concatenate([acc[:half], acc2], axis=0))
    o_ref[0] = jnp.concatenate([_normalize(acc) for acc in out], axis=1).astype(o_ref.dtype)


def _flash_causal(q, k, v, *, tq, tk, hp):
    bsz, seq, wide = q.shape
    heads = wide // LANE
    assert tk % tq == 0 and seq % tq == 0
    kern = functools.partial(_flash_kernel, tq=tq, tk=tk, hp=hp)
    return pl.pallas_call(
        kern,
        grid=(bsz, heads // hp, seq // tq),
        in_specs=[
            pl.BlockSpec((1, tq, hp * LANE), lambda b, h, i: (b, i, h)),
            pl.BlockSpec((1, seq, hp * LANE), lambda b, h, i: (b, 0, h)),
            pl.BlockSpec((1, seq, hp * LANE), lambda b, h, i: (b, 0, h)),
        ],
        out_specs=pl.BlockSpec((1, tq, hp * LANE), lambda b, h, i: (b, i, h)),
        out_shape=jax.ShapeDtypeStruct((bsz, seq, wide), BF),
        compiler_params=_params(("parallel", "parallel", "arbitrary")),
        name="mla_flash",
    )(q, k, v)


def _nsa_proj_kernel(h_ref, w_ref, b_ref, c_ref, s_ref, q_ref, kc_ref, vc_ref, ks_ref, vs_ref, kw_ref, vw_ref, g_ref,
                     *, tiles_per_seq):
    tm = h_ref.shape[0]
    hb = h_ref[...].astype(BF)
    pos = (pl.program_id(0) % tiles_per_seq) * tm + lax.broadcasted_iota(jnp.int32, (tm, LANE), 0)
    lane = lax.broadcasted_iota(jnp.int32, (tm, LANE), 1)
    tag = jnp.where(lane == NSA_DIM + lax.shift_right_logical(pos, 6), MASK_BIG, 0.0)
    tag2 = jnp.concatenate([tag] * NSA_GROUPS, axis=1)

    def proj(c0, width):
        return _dot(hb, w_ref[:, c0:c0 + width]) + b_ref[:, c0:c0 + width]

    c = c_ref[...]
    s = s_ref[...]
    c8 = jnp.concatenate([c] * NSA_HEADS, axis=1)
    s8 = jnp.concatenate([s] * NSA_HEADS, axis=1)
    c2 = jnp.concatenate([c] * NSA_GROUPS, axis=1)
    s2 = jnp.concatenate([s] * NSA_GROUPS, axis=1)
    q = (proj(0, 1024) * c8 + proj(1024, 1024) * s8) * (NSA_DIM ** -0.5 * LOG2_E)
    q_ref[...] = q.astype(BF)
    kc_ref[...] = proj(2048, 128)
    vc_ref[...] = proj(2176, 128)
    ks_ref[...] = (proj(2304, 256) * c2 + proj(2560, 256) * s2 + tag2).astype(BF)
    ones = _denom_ones(NSA_GROUPS * LANE)
    vs_ref[...] = (proj(2816, 256) + ones).astype(BF)
    kw_ref[...] = (proj(3072, 256) * c2 + proj(3328, 256) * s2).astype(BF)
    vw_ref[...] = (proj(3584, 256) + ones).astype(BF)
    g_ref[...] = jax.nn.sigmoid(proj(3840, 256))


def _nsa_proj(h, w, b, cn, sn, *, tm, seq):
    n, d = h.shape
    tps = seq // tm
    tab = pl.BlockSpec((tm, LANE), lambda i: (i % tps, 0))

    def out(width, dt):
        return pl.BlockSpec((tm, width), lambda i: (i, 0)), jax.ShapeDtypeStruct((n, width), dt)

    outs = [out(1024, BF), out(128, F32), out(128, F32), out(256, BF), out(256, BF), out(256, BF), out(256, BF), out(256, F32)]
    assert seq // SLC_BLOCK <= LANE - NSA_DIM
    return pl.pallas_call(
        functools.partial(_nsa_proj_kernel, tiles_per_seq=tps),
        grid=(n // tm,),
        in_specs=[pl.BlockSpec((tm, d), lambda i: (i, 0)), _resident(w.shape), _resident(b.shape), tab, tab],
        out_specs=[o[0] for o in outs],
        out_shape=[o[1] for o in outs],
        compiler_params=_params(("parallel",)),
        name="nsa_proj",
    )(h, w, b, cn, sn)


def _nsa_cmp_kernel(kc_ref, vc_ref, pe_ref, wk_ref, wkr_ref, wv_ref, c_ref, s_ref, kcmp_ref, vcmp_ref):
    n16 = kc_ref.shape[1]

    def halves(x_ref, pe_lo, pe_hi):
        a = x_ref[0]
        nxt = pltpu.roll(a, n16 - 1, 0)
        return (a + pe_lo).astype(BF), (nxt + pe_hi).astype(BF)

    klo, khi = halves(kc_ref, pe_ref[0:1, :], pe_ref[1:2, :])
    kc = _dot(klo, wk_ref[0]) + _dot(khi, wk_ref[1])
    kcr = _dot(klo, wkr_ref[0]) + _dot(khi, wkr_ref[1])
    kcmp_ref[0] = (kc * c_ref[...] + kcr * s_ref[...]).astype(BF)
    vlo, vhi = halves(vc_ref, pe_ref[2:3, :], pe_ref[3:4, :])
    vcmp_ref[0] = (_dot(vlo, wv_ref[0]) + _dot(vhi, wv_ref[1])).astype(BF)


def _nsa_compress(kc, vc, pe, wk, wkr, wv, cc, sc):
    bsz, n16, wide = kc.shape
    blk = pl.BlockSpec((1, n16, wide), lambda b: (b, 0, 0))
    oblk = pl.BlockSpec((1, n16, NSA_GROUPS * LANE), lambda b: (b, 0, 0))
    osh = jax.ShapeDtypeStruct((bsz, n16, NSA_GROUPS * LANE), BF)
    return pl.pallas_call(
        _nsa_cmp_kernel,
        grid=(bsz,),
        in_specs=[blk, blk] + [_resident(a.shape) for a in (pe, wk, wkr, wv, cc, sc)],
        out_specs=[oblk, oblk],
        out_shape=[osh, osh],
        compiler_params=_params(("parallel",)),
        name="nsa_compress",
    )(kc, vc, pe, wk, wkr, wv, cc, sc)


def _nsa_attn_kernel(q_ref, kcmp_ref, vcmp_ref, ks_ref, vs_ref, kw_ref, vw_ref, g_ref, ov_ref, o_ref, *, tk, top_k):
    qi = pl.program_id(1)
    T = q_ref.shape[1]
    R = NSA_HPG * T
    G = NSA_GROUPS
    q0 = qi * T
    qpos = lax.broadcasted_iota(jnp.int32, (T, 1), 0) + q0
    ncp = kcmp_ref.shape[1]
    nb = ov_ref.shape[0]
    ov_t = ov_ref[...]

    def add_per_query(x, b):
        w = x.shape[1]
        return (x.reshape(NSA_HPG, T, w) + b[None]).reshape(R, w)

    def lanes(g):
        return slice(g * LANE, (g + 1) * LANE)

    q4 = [jnp.concatenate([q_ref[0, :, (g * NSA_HPG + h) * LANE:(g * NSA_HPG + h + 1) * LANE]
                           for h in range(NSA_HPG)], axis=0) for g in range(G)]

    cmp_end = lax.broadcasted_iota(jnp.int32, (1, ncp), 1) * CMP_STRIDE + (CMP_BLOCK - 1)
    cbias = jnp.where(cmp_end <= qpos, 0.0, NEG)
    any_valid = jnp.where(qpos >= CMP_BLOCK - 1, 1.0, 0.0)
    jr = lax.broadcasted_iota(jnp.int32, (nb, 1), 0)
    jrf = jr.astype(F32)
    jq = lax.shift_right_logical(lax.broadcasted_iota(jnp.int32, (1, T), 1) + q0, 6)
    forced = (jr == 0) | (jr == jq) | (jr == jq - 1)
    eye_t = jnp.where(lax.broadcasted_iota(jnp.int32, (T, T), 0) == lax.broadcasted_iota(jnp.int32, (T, T), 1),
                      1.0, 0.0).astype(BF)
    o_cmp, q4s = [], []
    for g in range(G):
        sm = _dot_t(q4[g], kcmp_ref[0, :, lanes(g)]).reshape(NSA_HPG, T, ncp) + cbias[None]
        e = jnp.exp2(sm - jnp.max(sm, -1, keepdims=True))
        p = e * (any_valid[None] / jnp.sum(e, -1, keepdims=True))
        o_cmp.append(_dot(p.reshape(R, ncp).astype(BF), vcmp_ref[0, :, lanes(g)]))
        psum = p[0] + p[1] + p[2] + p[3]
        hi = psum.astype(BF)
        r1 = psum - hi.astype(F32)
        mid = r1.astype(BF)
        lo = (r1 - mid.astype(F32)).astype(BF)
        imp = _dot_t(ov_t, hi) + _dot_t(ov_t, mid) + _dot_t(ov_t, lo)
        imp = jnp.where(forced, 1e9, imp)
        imp = jnp.where(jr <= jq, imp, -1.0)
        work = imp
        sel = jnp.zeros_like(imp)
        for _ in range(top_k):
            mx = jnp.max(work, 0, keepdims=True)
            idx = jnp.min(jnp.where(work == mx, jrf, float(nb)), 0, keepdims=True)
            pick = jrf == idx
            sel = jnp.where(pick, 1.0, sel)
            work = jnp.where(pick, -2.0, work)
        unsel_t = jnp.where(imp >= 0.0, sel, 0.0) - 1.0
        pad_t = [jnp.zeros((NSA_DIM, T), F32), unsel_t]
        if LANE - NSA_DIM - nb:
            pad_t.append(jnp.zeros((LANE - NSA_DIM - nb, T), F32))
        unsel = _dot_t(eye_t, jnp.concatenate(pad_t, axis=0).astype(BF)).astype(BF)
        q4s.append(add_per_query(q4[g], unsel))

    def slc_tile(j, carries, diagonal):
        k0 = pl.multiple_of(j * tk, tk)
        out = []
        for g in range(G):
            sc = _dot_t(q4s[g], ks_ref[0, pl.ds(k0, tk), lanes(g)])
            if diagonal:
                sc = add_per_query(sc, jnp.where((lax.broadcasted_iota(jnp.int32, (1, tk), 1) + k0) <= qpos, 0.0, NEG))
            out.append(_online_softmax_step(sc, vs_ref[0, pl.ds(k0, tk), lanes(g)], carries[g]))
        return tuple(out)

    j_last = q0 // tk
    carries = lax.fori_loop(0, j_last, lambda j, c: slc_tile(j, c, False),
                            tuple(_softmax_init(R, LANE) for _ in range(G)))
    carries = slc_tile(j_last, carries, True)

    wk = WINDOW + T
    w0 = pl.multiple_of(jnp.maximum(q0 - WINDOW, 0), T)
    dist = qpos - (lax.broadcasted_iota(jnp.int32, (1, wk), 1) + w0)
    wbias = jnp.where((dist >= 0) & (dist < WINDOW), 0.0, NEG)
    o_win = []
    for g in range(G):
        sc = add_per_query(_dot_t(q4[g], kw_ref[0, pl.ds(w0, wk), lanes(g)]), wbias)
        e = jnp.exp2(sc - jnp.max(sc, -1, keepdims=True))
        o_win.append(_normalize(_dot(e.astype(BF), vw_ref[0, pl.ds(w0, wk), lanes(g)])))

    gw = NSA_HPG * LANE
    e_row = lax.broadcasted_iota(jnp.int32, (LANE, 3 * gw), 0)
    e_col = lax.broadcasted_iota(jnp.int32, (LANE, 3 * gw), 1)
    branch = jnp.where(e_col >= 2 * gw, 2, jnp.where(e_col >= gw, 1, 0))
    head = lax.shift_right_logical(e_col - branch * gw, 7)
    expand = jnp.where(e_row == 3 * head + branch, 1.0, 0.0).astype(BF)

    def heads_on_lanes(x):
        return jnp.concatenate([x[h * T:(h + 1) * T] for h in range(NSA_HPG)], axis=1)

    outs = []
    for g in range(G):
        gs = g_ref[0, :, lanes(g)]
        hi = gs.astype(BF)
        r1 = gs - hi.astype(F32)
        mid = r1.astype(BF)
        lo = (r1 - mid.astype(F32)).astype(BF)
        gx = _dot(hi, expand) + _dot(mid, expand) + _dot(lo, expand)
        outs.append(gx[:, 0:gw] * heads_on_lanes(o_cmp[g])
                    + gx[:, gw:2 * gw] * heads_on_lanes(_normalize(carries[g][1]))
                    + gx[:, 2 * gw:3 * gw] * heads_on_lanes(o_win[g]))
    o_ref[0] = jnp.concatenate(outs, axis=1).astype(o_ref.dtype)


def _nsa_attention(q, kcmp, vcmp, ks, vs, kw, vw, gates, ov, *, tq, tk, top_k):
    bsz, seq, wide = q.shape
    n16 = kcmp.shape[1]
    gw = NSA_GROUPS * LANE
    kern = functools.partial(_nsa_attn_kernel, tk=tk, top_k=top_k)
    cblk = pl.BlockSpec((1, n16, gw), lambda b, i: (b, 0, 0))
    sblk = pl.BlockSpec((1, seq, gw), lambda b, i: (b, 0, 0))
    return pl.pallas_call(
        kern,
        grid=(bsz, seq // tq),
        in_specs=[pl.BlockSpec((1, tq, wide), lambda b, i: (b, i, 0)), cblk, cblk, sblk, sblk, sblk, sblk,
                  pl.BlockSpec((1, tq, gw), lambda b, i: (b, i, 0)), _resident(ov.shape)],
        out_specs=pl.BlockSpec((1, tq, wide), lambda b, i: (b, i, 0)),
        out_shape=jax.ShapeDtypeStruct((bsz, seq, wide), BF),
        compiler_params=_params(("parallel", "arbitrary")),
        name="nsa_attn",
    )(q, kcmp, vcmp, ks, vs, kw, vw, gates, ov)


def _merge_kernel(x_ref, ab_ref, oc_ref, od_ref, wg_ref, bg_ref, wc_ref, wd_ref, wo_ref, g_ref, b_ref, o_ref):
    x = x_ref[...]
    gates = _dot(x.astype(BF), wg_ref[...]) + bg_ref[...]
    yc = _dot(oc_ref[...], wc_ref[...])
    yd = _dot(od_ref[...], wd_ref[...])
    merged = ab_ref[...] + jax.nn.sigmoid(gates[:, :D_MODEL]) * yc + jax.nn.sigmoid(gates[:, D_MODEL:]) * yd
    mix = _dot(merged.astype(BF), wo_ref[...])
    o_ref[...] = _ln(ALPHA * x + mix, g_ref[...], b_ref[...])


def _merge_ln(x, ab, oc, od, wg, bg, wc, wd, wo, g, b, *, tm):
    n, d = x.shape
    row = pl.BlockSpec((tm, d), lambda i: (i, 0))
    return pl.pallas_call(
        _merge_kernel,
        grid=(n // tm,),
        in_specs=[row, row, row, row] + [_resident(a.shape) for a in (wg, bg, wc, wd, wo, g, b)],
        out_specs=row,
        out_shape=jax.ShapeDtypeStruct((n, d), F32),
        compiler_params=_params(("parallel",)),
        name="merge_ln",
    )(x, ab, oc, od, wg, bg, wc, wd, wo, g, b)


def _linear_kernel(x_ref, w_ref, o_ref):
    o_ref[...] = _dot(x_ref[...].astype(BF), w_ref[...]).astype(o_ref.dtype)


def _linear(x, w, *, tm, dtype):
    n, d = x.shape
    return pl.pallas_call(
        _linear_kernel,
        grid=(n // tm,),
        in_specs=[pl.BlockSpec((tm, d), lambda i: (i, 0)), _resident(w.shape)],
        out_specs=pl.BlockSpec((tm, w.shape[1]), lambda i: (i, 0)),
        out_shape=jax.ShapeDtypeStruct((n, w.shape[1]), dtype),
        compiler_params=_params(("parallel",)),
        name="mem_kv",
    )(x, w)


def _xattn_kernel(x_ref, k_ref, v_ref, wq_ref, wo_ref, g_ref, b_ref, o_ref):
    x = x_ref[...]
    q = _dot(x.astype(BF), wq_ref[...]).astype(BF)
    k = k_ref[0]
    v = v_ref[0]
    heads = []
    for h in range(XATTN_HEADS):
        sl = slice(h * XATTN_DIM, (h + 1) * XATTN_DIM)
        s = _dot_t(q[:, sl], k[:, sl]) * (XATTN_DIM ** -0.5)
        e = jnp.exp(s - jnp.max(s, -1, keepdims=True))
        p = e / jnp.sum(e, -1, keepdims=True)
        heads.append(_dot(p.astype(BF), v[:, sl]))
    o = jnp.concatenate(heads, axis=1).astype(BF)
    o_ref[...] = _ln(ALPHA * x + _dot(o, wo_ref[...]), g_ref[...], b_ref[...])


def _xattn_ln(x, kv, wq, wo, g, b, *, tm, seq):
    n, d = x.shape
    tps = seq // tm
    mlen = kv.shape[1]
    hd = XATTN_HEADS * XATTN_DIM
    return pl.pallas_call(
        _xattn_kernel,
        grid=(n // tm,),
        in_specs=[pl.BlockSpec((tm, d), lambda i: (i, 0)),
                  pl.BlockSpec((1, mlen, hd), lambda i: (i // tps, 0, 0)),
                  pl.BlockSpec((1, mlen, hd), lambda i: (i // tps, 0, 1))]
        + [_resident(a.shape) for a in (wq, wo, g, b)],
        out_specs=pl.BlockSpec((tm, d), lambda i: (i, 0)),
        out_shape=jax.ShapeDtypeStruct((n, d), F32),
        compiler_params=_params(("parallel",)),
        name="xattn_ln",
    )(x, kv, kv, wq, wo, g, b)


def _rope_tab(pos, dim):
    inv = ROPE_THETA ** (-(jnp.arange(0, dim, 2, dtype=F32) / dim))
    ang = pos[:, None] * inv[None, :]
    return jnp.cos(ang), jnp.sin(ang)


def _rot_cols(w, half):
    return jnp.concatenate([-w[..., half:2 * half], w[..., :half]], axis=-1)


def _pad_slots(w, n_slots, width):
    lead = w.shape[:-1]
    w = w.reshape(lead + (n_slots, width))
    w = jnp.pad(w, [(0, 0)] * len(lead) + [(0, 0), (0, LANE - width)])
    return w.reshape(lead + (n_slots * LANE,))


def _rot_slots(w, n_slots, width, half):
    lead = w.shape[:-1]
    w = w.reshape(lead + (n_slots, width))
    return _rot_cols(w, half).reshape(lead + (n_slots * width,))


def _pad_rows(w, n_slots, width):
    d = w.shape[-1]
    w = w.reshape(n_slots, width, d)
    w = jnp.pad(w, [(0, 0), (0, LANE - width), (0, 0)])
    return w.reshape(n_slots * LANE, d)


def _layer_params(l, p):
    w_in, b_in = p["w_in"][l], p["b_in"][l]

    def cols(o, wd):
        return w_in[:, o:o + wd], b_in[o:o + wd]

    out = {}
    out["w_ab"] = jnp.concatenate([w_in[:, 0:2560], w_in[:, _O_GA:_O_GC]], axis=1).astype(BF)
    out["b_ab"] = jnp.concatenate([b_in[0:2560], b_in[_O_GA:_O_GC]])[None, :]
    wkr, bkr = cols(_O_KROPE, MLA_ROPE)
    half = MLA_ROPE // 2
    padk = lambda a: jnp.pad(a, [(0, 0)] * (a.ndim - 1) + [(0, LANE - MLA_ROPE)])
    out["w_c"] = jnp.concatenate([w_in[:, _O_QLAT:_O_KROPE], padk(wkr), padk(_rot_cols(wkr, half))], axis=1).astype(BF)
    out["b_c"] = jnp.concatenate([b_in[_O_QLAT:_O_KROPE], padk(bkr), padk(_rot_cols(bkr, half))])[None, :]
    wuq = p["mla_wuq"][l].reshape(MLA_Q_RANK, MLA_HEADS, MLA_NOPE + MLA_ROPE)
    rope_rot = _rot_cols(wuq[..., MLA_NOPE:], half)
    wqa = jnp.pad(wuq, [(0, 0), (0, 0), (0, LANE - MLA_NOPE - MLA_ROPE)])
    wqb = jnp.pad(rope_rot, [(0, 0), (0, 0), (MLA_NOPE, LANE - MLA_NOPE - MLA_ROPE)])
    out["wqa"] = wqa.reshape(MLA_Q_RANK, MLA_HEADS * LANE).astype(BF)
    out["wqb"] = wqb.reshape(MLA_Q_RANK, MLA_HEADS * LANE).astype(BF)
    wukv = p["mla_wukv"][l].reshape(MLA_KV_RANK, MLA_HEADS, MLA_NOPE + MLA_V)
    out["wk_c"] = jnp.pad(wukv[..., :MLA_NOPE], [(0, 0), (0, 0), (0, LANE - MLA_NOPE)]).reshape(MLA_KV_RANK, -1).astype(BF)
    out["wv_c"] = jnp.pad(wukv[..., MLA_NOPE:], [(0, 0), (0, 0), (0, LANE - MLA_V)]).reshape(MLA_KV_RANK, -1).astype(BF)
    wq, bq = cols(_O_NQ, NSA_HEADS * NSA_DIM)
    hd = NSA_DIM // 2
    pieces_w, pieces_b = [], []

    def add(w, b, slots, roped):
        pieces_w.append(_pad_slots(w, slots, NSA_DIM))
        pieces_b.append(_pad_slots(b, slots, NSA_DIM))
        if roped:
            pieces_w.append(_pad_slots(_rot_slots(w, slots, NSA_DIM, hd), slots, NSA_DIM))
            pieces_b.append(_pad_slots(_rot_slots(b, slots, NSA_DIM, hd), slots, NSA_DIM))

    add(wq, bq, NSA_HEADS, True)
    wkc, bkc = cols(_O_NKC, 128)
    wvc, bvc = cols(_O_NVC, 128)
    pieces_w += [wkc, wvc]
    pieces_b += [bkc, bvc]
    add(*cols(_O_NKS, 128), NSA_GROUPS, True)
    add(*cols(_O_NVS, 128), NSA_GROUPS, False)
    add(*cols(_O_NKW, 128), NSA_GROUPS, True)
    add(*cols(_O_NVW, 128), NSA_GROUPS, False)
    wg, bg = cols(_O_NGATE, NSA_HEADS * 3)
    pieces_w.append(_pad_slots(wg, NSA_GROUPS, NSA_HPG * 3))
    pieces_b.append(_pad_slots(bg, NSA_GROUPS, NSA_HPG * 3))
    out["w_d"] = jnp.concatenate(pieces_w, axis=1).astype(BF)
    out["b_d"] = jnp.concatenate(pieces_b)[None, :]

    def cmp_weights(w):
        eye = jnp.eye(NSA_GROUPS, dtype=F32)
        wp = jnp.pad(w, [(0, 0), (0, 0), (0, LANE - NSA_DIM)])
        full = jnp.einsum("lde,gh->lgdhe", wp, eye).reshape(CMP_BLOCK, NSA_GROUPS * NSA_DIM, NSA_GROUPS * LANE)
        return full.reshape(2, CMP_STRIDE * NSA_GROUPS * NSA_DIM, NSA_GROUPS * LANE).astype(BF)

    wck = p["nsa_wcmp_k"][l]
    out["wcmp_k"] = cmp_weights(wck)
    out["wcmp_kr"] = cmp_weights(_rot_cols(wck, hd))
    out["wcmp_v"] = cmp_weights(p["nsa_wcmp_v"][l])

    def pe_rows(pe):
        t = jnp.broadcast_to(pe[:, None, :], (CMP_BLOCK, NSA_GROUPS, NSA_DIM))
        return t.reshape(2, CMP_STRIDE * NSA_GROUPS * NSA_DIM)

    out["pe"] = jnp.concatenate([pe_rows(p["nsa_pe_k"][l]), pe_rows(p["nsa_pe_v"][l])], axis=0)
    out["w_g"] = w_in[:, _O_GC:].astype(BF)
    out["b_g"] = b_in[_O_GC:][None, :]
    out["wout_c"] = _pad_rows(p["mla_wout"][l], MLA_HEADS, MLA_V).astype(BF)
    out["wout_d"] = _pad_rows(p["nsa_wout"][l], NSA_HEADS, NSA_DIM).astype(BF)
    return out


def _tables(seq):
    pos = jnp.arange(seq, dtype=F32)
    c16, s16 = _rope_tab(pos, MLA_ROPE)
    one = jnp.ones((seq, MLA_NOPE), F32)
    zero = jnp.zeros((seq, MLA_NOPE), F32)
    tail = LANE - MLA_NOPE - MLA_ROPE
    cq = jnp.concatenate([one, c16, c16, jnp.ones((seq, tail), F32)], axis=1)
    sq = jnp.concatenate([zero, s16, s16, jnp.zeros((seq, tail), F32)], axis=1)
    ck = jnp.pad(jnp.concatenate([c16, c16], axis=1), [(0, 0), (0, LANE - MLA_ROPE)])
    sk = jnp.pad(jnp.concatenate([s16, s16], axis=1), [(0, 0), (0, LANE - MLA_ROPE)])
    c32, s32 = _rope_tab(pos, NSA_DIM)
    cn = jnp.pad(jnp.concatenate([c32, c32], axis=1), [(0, 0), (0, LANE - NSA_DIM)])
    sn = jnp.pad(jnp.concatenate([s32, s32], axis=1), [(0, 0), (0, LANE - NSA_DIM)])
    n16 = seq // CMP_STRIDE
    cend = (jnp.arange(n16) * CMP_STRIDE + CMP_BLOCK - 1).astype(F32)
    cc32, cs32 = _rope_tab(cend, NSA_DIM)
    ccg = jnp.pad(jnp.concatenate([cc32, cc32], axis=1), [(0, 0), (0, LANE - NSA_DIM)])
    csg = jnp.pad(jnp.concatenate([cs32, cs32], axis=1), [(0, 0), (0, LANE - NSA_DIM)])
    cc = jnp.concatenate([ccg] * NSA_GROUPS, axis=1)
    cs = jnp.concatenate([csg] * NSA_GROUPS, axis=1)
    n_cmp = (seq - CMP_BLOCK) // CMP_STRIDE + 1
    n_slc = seq // SLC_BLOCK
    cstart = jnp.arange(n16) * CMP_STRIDE
    sstart = jnp.arange(n_slc) * SLC_BLOCK
    ovl = (jnp.minimum(cstart[None, :] + CMP_BLOCK, sstart[:, None] + SLC_BLOCK)
           - jnp.maximum(cstart[None, :], sstart[:, None]))
    ovl = jnp.clip(ovl, 0).astype(F32) / CMP_BLOCK
    ovl = jnp.where(jnp.arange(n16)[None, :] < n_cmp, ovl, 0.0).astype(BF)
    pk = jnp.zeros((LANE, MLA_HEADS, LANE), F32)
    pk = pk.at[jnp.arange(MLA_ROPE), :, MLA_NOPE + jnp.arange(MLA_ROPE)].set(1.0)
    pk = pk.reshape(LANE, MLA_HEADS * LANE).astype(BF)
    return dict(cq=cq, sq=sq, ck=ck, sk=sk, cn=cn, sn=sn, cc=cc, cs=cs, ovl=ovl, pk=pk)


def kernel(x, mem, ffn1_w1, ffn1_w3, ffn1_w2, ln1_g, ln1_b, w_in, b_in, gmlp_ln_g, gmlp_ln_b, gmlp_ws, gmlp_bs, gmlp_wout, conv_w, conv_wout, mla_qnorm_g, mla_kvnorm_g, mla_wuq, mla_wukv, mla_wout, nsa_pe_k, nsa_pe_v, nsa_wcmp_k, nsa_wcmp_v, nsa_wout, w_o, ln2_g, ln2_b, xattn_wq, xattn_wk, xattn_wv, xattn_wo, ln3_g, ln3_b, ffn2_w1, ffn2_w3, ffn2_w2, ln4_g, ln4_b):
    bsz, seq, d = x.shape
    mlen = mem.shape[1]
    n = bsz * seq
    assert d == D_MODEL and seq % 512 == 0 and seq >= WINDOW + 256
    p = dict(w_in=w_in, b_in=b_in, mla_wuq=mla_wuq, mla_wukv=mla_wukv, mla_wout=mla_wout,
             nsa_pe_k=nsa_pe_k, nsa_pe_v=nsa_pe_v, nsa_wcmp_k=nsa_wcmp_k, nsa_wcmp_v=nsa_wcmp_v, nsa_wout=nsa_wout)
    tb = _tables(seq)
    tm = 512
    tm_ffn = 1024
    tf = D_FF // 11
    n16 = seq // CMP_STRIDE
    top_k = min(SLC_TOPK, seq // SLC_BLOCK)
    row = lambda a: a[None, :]

    h = x.reshape(n, d)
    mem2 = mem.reshape(bsz * mlen, d)
    for l in range(DEPTH):
        lp = _layer_params(l, p)
        h = _ffn_ln(h, ffn1_w1[l].astype(BF), ffn1_w3[l].astype(BF), ffn1_w2[l].astype(BF),
                    row(ln1_g[l]), row(ln1_b[l]), tm=tm_ffn, tf=tf)
        ab = _mix_ab(h, lp["w_ab"], lp["b_ab"], row(gmlp_ln_g[l]), row(gmlp_ln_b[l]), gmlp_ws[l], gmlp_bs[l].T,
                     gmlp_wout[l].astype(BF), conv_w[l], conv_wout[l].astype(BF), tm=tm, seq=seq)
        qc, kc_, vc_ = _mla_proj(h, lp["w_c"], lp["b_c"], row(mla_qnorm_g[l]), row(mla_kvnorm_g[l]),
                                 lp["wqa"], lp["wqb"], lp["wk_c"], lp["wv_c"], tb["pk"],
                                 tb["cq"], tb["sq"], tb["ck"], tb["sk"], tm=tm, seq=seq)
        wide = MLA_HEADS * LANE
        oc = _flash_causal(qc.reshape(bsz, seq, wide), kc_.reshape(bsz, seq, wide), vc_.reshape(bsz, seq, wide), tq=1024, tk=1024, hp=2)
        qn, nkc, nvc, nks, nvs, nkw, nvw, gates = _nsa_proj(h, lp["w_d"], lp["b_d"], tb["cn"], tb["sn"], tm=tm, seq=seq)
        kcmp, vcmp = _nsa_compress(nkc.reshape(bsz, n16, CMP_STRIDE * 128), nvc.reshape(bsz, n16, CMP_STRIDE * 128),
                                   lp["pe"], lp["wcmp_k"], lp["wcmp_kr"], lp["wcmp_v"], tb["cc"], tb["cs"])
        gw = NSA_GROUPS * LANE
        od = _nsa_attention(qn.reshape(bsz, seq, NSA_HEADS * LANE), kcmp, vcmp,
                            nks.reshape(bsz, seq, gw), nvs.reshape(bsz, seq, gw),
                            nkw.reshape(bsz, seq, gw), nvw.reshape(bsz, seq, gw),
                            gates.reshape(bsz, seq, gw), tb["ovl"], tq=256, tk=512, top_k=top_k)
        h = _merge_ln(h, ab, oc.reshape(n, wide), od.reshape(n, NSA_HEADS * LANE), lp["w_g"], lp["b_g"],
                      lp["wout_c"], lp["wout_d"], w_o[l].astype(BF), row(ln2_g[l]), row(ln2_b[l]), tm=tm)
        kv = _linear(mem2, jnp.concatenate([xattn_wk[l], xattn_wv[l]], axis=1).astype(BF), tm=min(256, bsz * mlen), dtype=BF)
        h = _xattn_ln(h, kv.reshape(bsz, mlen, 2 * XATTN_HEADS * XATTN_DIM), xattn_wq[l].astype(BF),
                      xattn_wo[l].astype(BF), row(ln3_g[l]), row(ln3_b[l]), tm=tm, seq=seq)
        h = _ffn_ln(h, ffn2_w1[l].astype(BF), ffn2_w3[l].astype(BF), ffn2_w2[l].astype(BF),
                    row(ln4_g[l]), row(ln4_b[l]), tm=tm_ffn, tf=tf)
    return h.reshape(bsz, seq, d)
```

```python
import functools

import jax
import jax.numpy as jnp
from jax import lax
from jax.experimental import pallas as pl
from jax.experimental.pallas import tpu as pltpu

BF = jnp.bfloat16
F32 = jnp.float32

D_MODEL = 1024
D_FF = 2816
LN_EPS = 1e-5
RMS_EPS = 1e-6
ROPE_THETA = 10000.0
DEPTH = 2
ALPHA = (2 * DEPTH) ** 0.25
NEG = -1e30
LOG2_E = 1.4426950408889634
DENOM_LANE = 64
MASK_BIG = 2.0 ** 100

GMLP_CHUNK = 128
GMLP_GROUPS = 4
GMLP_WIDTH = 512
CONV_WIDTH = 512
CONV_K = 3
MLA_HEADS = 8
MLA_Q_RANK = 256
MLA_KV_RANK = 128
MLA_NOPE = 64
MLA_ROPE = 32
MLA_V = 64
NSA_HEADS = 8
NSA_GROUPS = 2
NSA_HPG = 4
NSA_DIM = 64
CMP_BLOCK = 32
CMP_STRIDE = 16
SLC_BLOCK = 64
SLC_TOPK = 8
WINDOW = 512
XATTN_HEADS = 4
XATTN_DIM = 128

LANE = 128
CONV_HALO = 8
VMEM_LIMIT = 56 * 1024 * 1024

_O_U, _O_V, _O_CB, _O_CC, _O_CH = 0, 512, 1024, 1536, 2048
_O_QLAT, _O_KVLAT, _O_KROPE = 2560, 2816, 2944
_O_NQ, _O_NKC, _O_NVC, _O_NKS, _O_NVS, _O_NKW, _O_NVW, _O_NGATE = 2976, 3488, 3616, 3744, 3872, 4000, 4128, 4256
_O_GA, _O_GB, _O_GC, _O_GD = 4280, 5304, 6328, 7352


def _dot(a, b):
    return jnp.dot(a, b, preferred_element_type=F32)


def _dot_t(a, b):
    return lax.dot_general(a, b, (((1,), (1,)), ((), ())), preferred_element_type=F32)


def _ln(y, g, b):
    mu = jnp.mean(y, -1, keepdims=True)
    d = y - mu
    var = jnp.mean(d * d, -1, keepdims=True)
    return d * lax.rsqrt(var + LN_EPS) * g + b


def _rms(x, g):
    return x * lax.rsqrt(jnp.mean(x * x, -1, keepdims=True) + RMS_EPS) * g


def _resident(shape):
    n = len(shape)
    return pl.BlockSpec(shape, lambda *_: (0,) * n, pipeline_mode=pl.Buffered(1))


def _params(sem):
    return pltpu.CompilerParams(dimension_semantics=sem, vmem_limit_bytes=VMEM_LIMIT)


def _cast_kernel(x_ref, o_ref):
    o_ref[...] = x_ref[...].astype(o_ref.dtype)


def _to_bf16(w, *, rows=256):
    r, c = w.shape
    rows = min(rows, r)
    assert r % rows == 0
    return pl.pallas_call(
        _cast_kernel,
        grid=(r // rows,),
        in_specs=[pl.BlockSpec((rows, c), lambda i: (i, 0))],
        out_specs=pl.BlockSpec((rows, c), lambda i: (i, 0)),
        out_shape=jax.ShapeDtypeStruct((r, c), BF),
        compiler_params=_params(("parallel",)),
        name="to_bf16",
    )(w)


def _ffn_ln_kernel(x_ref, w1_ref, w3_ref, w2_ref, g_ref, b_ref, o_ref, *, tf):
    x = x_ref[...]
    xb = x.astype(BF)
    acc = None
    for c in range(w1_ref.shape[1] // tf):
        cols = slice(c * tf, (c + 1) * tf)
        h1 = _dot(xb, w1_ref[:, cols])
        h3 = _dot(xb, w3_ref[:, cols])
        hh = (h1 * jax.nn.sigmoid(h1)) * h3
        part = _dot(hh.astype(BF), w2_ref[cols, :])
        acc = part if acc is None else acc + part
    o_ref[...] = _ln(ALPHA * x + 0.5 * acc, g_ref[...], b_ref[...])


def _layer_block(shape, layer):
    return pl.BlockSpec(shape, lambda *_: (layer, 0), pipeline_mode=pl.Buffered(1))


def _ffn_ln(x, w1, w3, w2, g, b, *, layer, tm, tf):
    n, d = x.shape
    f = w1.shape[1]
    return pl.pallas_call(
        functools.partial(_ffn_ln_kernel, tf=tf),
        grid=(n // tm,),
        in_specs=[pl.BlockSpec((tm, d), lambda i: (i, 0)), _layer_block((d, f), layer), _layer_block((d, f), layer),
                  _layer_block((f, d), layer), _resident(g.shape), _resident(b.shape)],
        out_specs=pl.BlockSpec((tm, d), lambda i: (i, 0)),
        out_shape=jax.ShapeDtypeStruct((n, d), F32),
        compiler_params=_params(("parallel",)),
        name="ffn_ln",
    )(x, w1, w3, w2, g, b)


def _ab_kernel(h_ref, w_ref, b_ref, lng_ref, lnb_ref, ws_ref, bst_ref, wga_ref, cw_ref, wcb_ref,
               o_ref, prev_ref, *, tiles_per_seq):
    i = pl.program_id(0)
    tm = h_ref.shape[0]
    hb = h_ref[...].astype(BF)

    def proj(c0, width):
        return _dot(hb, w_ref[:, c0:c0 + width]) + b_ref[:, c0:c0 + width]

    u = proj(0, GMLP_WIDTH)
    v = _ln(proj(512, GMLP_WIDTH), lng_ref[...], lnb_ref[...]).astype(BF)
    row = lax.broadcasted_iota(jnp.int32, (GMLP_CHUNK, GMLP_CHUNK), 0)
    col = lax.broadcasted_iota(jnp.int32, (GMLP_CHUNK, GMLP_CHUNK), 1)
    gd = GMLP_WIDTH // GMLP_GROUPS
    wgs = [jnp.where(row >= col, ws_ref[g], 0.0).astype(BF) for g in range(GMLP_GROUPS)]
    chunks = []
    for c in range(tm // GMLP_CHUNK):
        r0 = c * GMLP_CHUNK
        chunks.append(jnp.concatenate(
            [_dot(wgs[g], v[r0:r0 + GMLP_CHUNK, g * gd:(g + 1) * gd]) + bst_ref[:, g:g + 1]
             for g in range(GMLP_GROUPS)], axis=1))
    s = jnp.concatenate(chunks, axis=0)
    ya = _dot((u * s).astype(BF), wga_ref[...])

    cb = proj(1024, CONV_WIDTH)
    z = proj(1536, CONV_WIDTH) * proj(2048, CONV_WIDTH)

    @pl.when(i % tiles_per_seq == 0)
    def _():
        prev_ref[...] = jnp.zeros_like(prev_ref)

    zext = jnp.concatenate([prev_ref[...], z], axis=0)
    z1 = pltpu.roll(zext, 1, 0)[CONV_HALO:]
    z2 = pltpu.roll(zext, 2, 0)[CONV_HALO:]
    y = cw_ref[0:1, :] * z2 + cw_ref[1:2, :] * z1 + cw_ref[2:3, :] * z
    prev_ref[...] = z[tm - CONV_HALO:, :]
    yb = _dot((cb * y).astype(BF), wcb_ref[...])

    ga = proj(2560, D_MODEL)
    gb = proj(3584, D_MODEL)
    o_ref[...] = jax.nn.sigmoid(ga) * ya + jax.nn.sigmoid(gb) * yb


def _mix_ab(h, w, b, lng, lnb, ws, bst, wga, cw, wcb, *, tm, seq):
    n, d = h.shape
    kern = functools.partial(_ab_kernel, tiles_per_seq=seq // tm)
    return pl.pallas_call(
        kern,
        grid=(n // tm,),
        in_specs=[pl.BlockSpec((tm, d), lambda i: (i, 0))] + [_resident(a.shape) for a in (w, b, lng, lnb, ws, bst, wga, cw, wcb)],
        out_specs=pl.BlockSpec((tm, d), lambda i: (i, 0)),
        out_shape=jax.ShapeDtypeStruct((n, d), F32),
        scratch_shapes=[pltpu.VMEM((CONV_HALO, CONV_WIDTH), F32)],
        compiler_params=_params(("arbitrary",)),
        name="mix_ab",
    )(h, w, b, lng, lnb, ws, bst, wga, cw, wcb)


def _mla_proj_kernel(h_ref, w_ref, b_ref, qg_ref, kvg_ref, wqa_ref, wqb_ref, wk_ref, wv_ref, pk_ref,
                     cq_ref, sq_ref, ck_ref, sk_ref, q_ref, k_ref, v_ref):
    hb = h_ref[...].astype(BF)
    z = _dot(hb, w_ref[...]) + b_ref[...]
    qn = _rms(z[:, 0:256], qg_ref[...]).astype(BF)
    kvn = _rms(z[:, 256:384], kvg_ref[...]).astype(BF)
    cq = jnp.concatenate([cq_ref[...]] * MLA_HEADS, axis=1)
    sq = jnp.concatenate([sq_ref[...]] * MLA_HEADS, axis=1)
    scale = (MLA_NOPE + MLA_ROPE) ** -0.5 * LOG2_E
    q = (_dot(qn, wqa_ref[...]) * cq + _dot(qn, wqb_ref[...]) * sq) * scale
    q_ref[...] = q.astype(BF)
    kpe = (z[:, 384:512] * ck_ref[...] + z[:, 512:640] * sk_ref[...]).astype(BF)
    k_ref[...] = (_dot(kvn, wk_ref[...]) + _dot(kpe, pk_ref[...])).astype(BF)
    v_ref[...] = (_dot(kvn, wv_ref[...]) + _denom_ones(v_ref.shape[1])).astype(BF)


def _mla_proj(h, w, b, qg, kvg, wqa, wqb, wk, wv, pk, cq, sq, ck, sk, *, tm, seq):
    n, d = h.shape
    tps = seq // tm
    tab = pl.BlockSpec((tm, LANE), lambda i: (i % tps, 0))
    wide = MLA_HEADS * LANE
    out = jax.ShapeDtypeStruct((n, wide), BF)
    return pl.pallas_call(
        _mla_proj_kernel,
        grid=(n // tm,),
        in_specs=[pl.BlockSpec((tm, d), lambda i: (i, 0))]
        + [_resident(a.shape) for a in (w, b, qg, kvg, wqa, wqb, wk, wv, pk)] + [tab] * 4,
        out_specs=[pl.BlockSpec((tm, wide), lambda i: (i, 0))] * 3,
        out_shape=[out, out, out],
        compiler_params=_params(("parallel",)),
        name="mla_proj",
    )(h, w, b, qg, kvg, wqa, wqb, wk, wv, pk, cq, sq, ck, sk)


def _online_softmax_step(s, v, carry):
    m, acc = carry
    m_new = jnp.maximum(m, jnp.max(s, -1, keepdims=True))
    p = jnp.exp2(s - m_new)
    acc = jnp.exp2(m - m_new) * acc + _dot(p.astype(BF), v)
    return m_new, acc


def _softmax_init(rows, width):
    return (jnp.full((rows, 1), NEG, F32), jnp.zeros((rows, width), F32))


def _normalize(acc):
    return acc * (1.0 / acc[:, DENOM_LANE:DENOM_LANE + 1])


def _denom_ones(width):
    lane = lax.broadcasted_iota(jnp.int32, (1, width), 1)
    return jnp.where(lane % LANE == DENOM_LANE, 1.0, 0.0)


def _flash_kernel(q_ref, k_ref, v_ref, o_ref, *, tq, tk, hp):
    qi = pl.program_id(2)
    q0 = qi * tq
    qs = [q_ref[0, :, h * LANE:(h + 1) * LANE] for h in range(hp)]

    def tile(j, carries, width, diagonal):
        k0 = pl.multiple_of(j * width, width)
        out = []
        for h in range(hp):
            s = _dot_t(qs[h], k_ref[0, pl.ds(k0, width), h * LANE:(h + 1) * LANE])
            if diagonal:
                r = lax.broadcasted_iota(jnp.int32, (tq, width), 0)
                c = lax.broadcasted_iota(jnp.int32, (tq, width), 1)
                s = jnp.where(c <= r, s, NEG)
            out.append(_online_softmax_step(s, v_ref[0, pl.ds(k0, width), h * LANE:(h + 1) * LANE], carries[h]))
        return tuple(out)

    n_wide = q0 // tk
    init = tuple(_softmax_init(tq, LANE) for _ in range(hp))
    carries = lax.fori_loop(0, n_wide, lambda j, c: tile(j, c, tk, False), init)
    carries = lax.fori_loop(n_wide * (tk // tq), qi, lambda j, c: tile(j, c, tq, False), carries)

    half = tq // 2
    carries = tile(2 * qi, carries, half, True)
    r = lax.broadcasted_iota(jnp.int32, (half, half), 0)
    c = lax.broadcasted_iota(jnp.int32, (half, half), 1)
    k1 = pl.multiple_of(q0 + half, half)
    out = []
    for h in range(hp):
        m, acc = carries[h]
        s = _dot_t(qs[h][half:], k_ref[0, pl.ds(k1, half), h * LANE:(h + 1) * LANE])
        m2, acc2 = _online_softmax_step(jnp.where(c <= r, s, NEG), v_ref[0, pl.ds(k1, half), h * LANE:(h + 1) * LANE],
                                        (m[half:], acc[half:]))
        out.append(jnp.concatenate([acc[:half], acc2], axis=0))
    o_ref[0] = jnp.concatenate([_normalize(acc) for acc in out], axis=1).astype(o_ref.dtype)


def _flash_causal(q, k, v, *, tq, tk, hp):
    bsz, seq, wide = q.shape
    heads = wide // LANE
    assert tk % tq == 0 and seq % tq == 0
    kern = functools.partial(_flash_kernel, tq=tq, tk=tk, hp=hp)
    return pl.pallas_call(
        kern,
        grid=(bsz, heads // hp, seq // tq),
        in_specs=[
            pl.BlockSpec((1, tq, hp * LANE), lambda b, h, i: (b, i, h)),
            pl.BlockSpec((1, seq, hp * LANE), lambda b, h, i: (b, 0, h)),
            pl.BlockSpec((1, seq, hp * LANE), lambda b, h, i: (b, 0, h)),
        ],
        out_specs=pl.BlockSpec((1, tq, hp * LANE), lambda b, h, i: (b, i, h)),
        out_shape=jax.ShapeDtypeStruct((bsz, seq, wide), BF),
        compiler_params=_params(("parallel", "parallel", "arbitrary")),
        name="mla_flash",
    )(q, k, v)


def _nsa_proj_kernel(h_ref, w_ref, b_ref, c_ref, s_ref, q_ref, kc_ref, vc_ref, ks_ref, vs_ref, kw_ref, vw_ref, g_ref,
                     *, tiles_per_seq):
    tm = h_ref.shape[0]
    hb = h_ref[...].astype(BF)
    pos = (pl.program_id(0) % tiles_per_seq) * tm + lax.broadcasted_iota(jnp.int32, (tm, LANE), 0)
    lane = lax.broadcasted_iota(jnp.int32, (tm, LANE), 1)
    tag = jnp.where(lane == NSA_DIM + lax.shift_right_logical(pos, 6), MASK_BIG, 0.0)
    tag2 = jnp.concatenate([tag] * NSA_GROUPS, axis=1)

    def proj(c0, width):
        return _dot(hb, w_ref[:, c0:c0 + width]) + b_ref[:, c0:c0 + width]

    c = c_ref[...]
    s = s_ref[...]
    c8 = jnp.concatenate([c] * NSA_HEADS, axis=1)
    s8 = jnp.concatenate([s] * NSA_HEADS, axis=1)
    c2 = jnp.concatenate([c] * NSA_GROUPS, axis=1)
    s2 = jnp.concatenate([s] * NSA_GROUPS, axis=1)
    q = (proj(0, 1024) * c8 + proj(1024, 1024) * s8) * (NSA_DIM ** -0.5 * LOG2_E)
    q_ref[...] = q.astype(BF)
    kc_ref[...] = proj(2048, 128)
    vc_ref[...] = proj(2176, 128)
    ks_ref[...] = (proj(2304, 256) * c2 + proj(2560, 256) * s2 + tag2).astype(BF)
    ones = _denom_ones(NSA_GROUPS * LANE)
    vs_ref[...] = (proj(2816, 256) + ones).astype(BF)
    kw_ref[...] = (proj(3072, 256) * c2 + proj(3328, 256) * s2).astype(BF)
    vw_ref[...] = (proj(3584, 256) + ones).astype(BF)
    g_ref[...] = jax.nn.sigmoid(proj(3840, 256))


def _nsa_proj(h, w, b, cn, sn, *, tm, seq):
    n, d = h.shape
    tps = seq // tm
    tab = pl.BlockSpec((tm, LANE), lambda i: (i % tps, 0))

    def out(width, dt):
        return pl.BlockSpec((tm, width), lambda i: (i, 0)), jax.ShapeDtypeStruct((n, width), dt)

    outs = [out(1024, BF), out(128, F32), out(128, F32), out(256, BF), out(256, BF), out(256, BF), out(256, BF), out(256, F32)]
    assert seq // SLC_BLOCK <= LANE - NSA_DIM
    return pl.pallas_call(
        functools.partial(_nsa_proj_kernel, tiles_per_seq=tps),
        grid=(n // tm,),
        in_specs=[pl.BlockSpec((tm, d), lambda i: (i, 0)), _resident(w.shape), _resident(b.shape), tab, tab],
        out_specs=[o[0] for o in outs],
        out_shape=[o[1] for o in outs],
        compiler_params=_params(("parallel",)),
        name="nsa_proj",
    )(h, w, b, cn, sn)


def _nsa_cmp_kernel(kc_ref, vc_ref, pe_ref, wk_ref, wkr_ref, wv_ref, c_ref, s_ref, kcmp_ref, vcmp_ref):
    n16 = kc_ref.shape[1]

    def halves(x_ref, pe_lo, pe_hi):
        a = x_ref[0]
        nxt = pltpu.roll(a, n16 - 1, 0)
        return (a + pe_lo).astype(BF), (nxt + pe_hi).astype(BF)

    klo, khi = halves(kc_ref, pe_ref[0:1, :], pe_ref[1:2, :])
    kc = _dot(klo, wk_ref[0]) + _dot(khi, wk_ref[1])
    kcr = _dot(klo, wkr_ref[0]) + _dot(khi, wkr_ref[1])
    kcmp_ref[0] = (kc * c_ref[...] + kcr * s_ref[...]).astype(BF)
    vlo, vhi = halves(vc_ref, pe_ref[2:3, :], pe_ref[3:4, :])
    vcmp_ref[0] = (_dot(vlo, wv_ref[0]) + _dot(vhi, wv_ref[1])).astype(BF)


def _nsa_compress(kc, vc, pe, wk, wkr, wv, cc, sc):
    bsz, n16, wide = kc.shape
    blk = pl.BlockSpec((1, n16, wide), lambda b: (b, 0, 0))
    oblk = pl.BlockSpec((1, n16, NSA_GROUPS * LANE), lambda b: (b, 0, 0))
    osh = jax.ShapeDtypeStruct((bsz, n16, NSA_GROUPS * LANE), BF)
    return pl.pallas_call(
        _nsa_cmp_kernel,
        grid=(bsz,),
        in_specs=[blk, blk] + [_resident(a.shape) for a in (pe, wk, wkr, wv, cc, sc)],
        out_specs=[oblk, oblk],
        out_shape=[osh, osh],
        compiler_params=_params(("parallel",)),
        name="nsa_compress",
    )(kc, vc, pe, wk, wkr, wv, cc, sc)


def _nsa_attn_kernel(q_ref, kcmp_ref, vcmp_ref, ks_ref, vs_ref, kw_ref, vw_ref, g_ref, ov_ref, o_ref, *, tk, top_k):
    qi = pl.program_id(1)
    T = q_ref.shape[1]
    R = NSA_HPG * T
    G = NSA_GROUPS
    q0 = qi * T
    qpos = lax.broadcasted_iota(jnp.int32, (T, 1), 0) + q0
    ncp = kcmp_ref.shape[1]
    nb = ov_ref.shape[0]
    ov_t = ov_ref[...]

    def add_per_query(x, b):
        w = x.shape[1]
        return (x.reshape(NSA_HPG, T, w) + b[None]).reshape(R, w)

    def lanes(g):
        return slice(g * LANE, (g + 1) * LANE)

    q4 = [jnp.concatenate([q_ref[0, :, (g * NSA_HPG + h) * LANE:(g * NSA_HPG + h + 1) * LANE]
                           for h in range(NSA_HPG)], axis=0) for g in range(G)]

    cmp_end = lax.broadcasted_iota(jnp.int32, (1, ncp), 1) * CMP_STRIDE + (CMP_BLOCK - 1)
    cbias = jnp.where(cmp_end <= qpos, 0.0, NEG)
    any_valid = jnp.where(qpos >= CMP_BLOCK - 1, 1.0, 0.0)
    jr = lax.broadcasted_iota(jnp.int32, (nb, 1), 0)
    jrf = jr.astype(F32)
    jq = lax.shift_right_logical(lax.broadcasted_iota(jnp.int32, (1, T), 1) + q0, 6)
    forced = (jr == 0) | (jr == jq) | (jr == jq - 1)
    eye_t = jnp.where(lax.broadcasted_iota(jnp.int32, (T, T), 0) == lax.broadcasted_iota(jnp.int32, (T, T), 1),
                      1.0, 0.0).astype(BF)
    o_cmp, q4s = [], []
    for g in range(G):
        sm = _dot_t(q4[g], kcmp_ref[0, :, lanes(g)]).reshape(NSA_HPG, T, ncp) + cbias[None]
        e = jnp.exp2(sm - jnp.max(sm, -1, keepdims=True))
        p = e * (any_valid[None] / jnp.sum(e, -1, keepdims=True))
        o_cmp.append(_dot(p.reshape(R, ncp).astype(BF), vcmp_ref[0, :, lanes(g)]))
        psum = p[0] + p[1] + p[2] + p[3]
        hi = psum.astype(BF)
        r1 = psum - hi.astype(F32)
        mid = r1.astype(BF)
        lo = (r1 - mid.astype(F32)).astype(BF)
        imp = _dot_t(ov_t, hi) + _dot_t(ov_t, mid) + _dot_t(ov_t, lo)
        imp = jnp.where(forced, 1e9, imp)
        imp = jnp.where(jr <= jq, imp, -1.0)
        work = imp
        sel = jnp.zeros_like(imp)
        for _ in range(top_k):
            mx = jnp.max(work, 0, keepdims=True)
            idx = jnp.min(jnp.where(work == mx, jrf, float(nb)), 0, keepdims=True)
            pick = jrf == idx
            sel = jnp.where(pick, 1.0, sel)
            work = jnp.where(pick, -2.0, work)
        unsel_t = jnp.where(imp >= 0.0, sel, 0.0) - 1.0
        pad_t = [jnp.zeros((NSA_DIM, T), F32), unsel_t]
        if LANE - NSA_DIM - nb:
            pad_t.append(jnp.zeros((LANE - NSA_DIM - nb, T), F32))
        unsel = _dot_t(eye_t, jnp.concatenate(pad_t, axis=0).astype(BF)).astype(BF)
        q4s.append(add_per_query(q4[g], unsel))

    def slc_tile(j, carries, diagonal):
        k0 = pl.multiple_of(j * tk, tk)
        out = []
        for g in range(G):
            sc = _dot_t(q4s[g], ks_ref[0, pl.ds(k0, tk), lanes(g)])
            if diagonal:
                sc = add_per_query(sc, jnp.where((lax.broadcasted_iota(jnp.int32, (1, tk), 1) + k0) <= qpos, 0.0, NEG))
            out.append(_online_softmax_step(sc, vs_ref[0, pl.ds(k0, tk), lanes(g)], carries[g]))
        return tuple(out)

    j_last = q0 // tk
    carries = lax.fori_loop(0, j_last, lambda j, c: slc_tile(j, c, False),
                            tuple(_softmax_init(R, LANE) for _ in range(G)))
    carries = slc_tile(j_last, carries, True)
    slc_acc = [acc for (_, acc) in carries]

    wk = WINDOW + T
    w0 = pl.multiple_of(jnp.maximum(q0 - WINDOW, 0), T)
    dist = qpos - (lax.broadcasted_iota(jnp.int32, (1, wk), 1) + w0)
    wbias = jnp.where((dist >= 0) & (dist < WINDOW), 0.0, NEG)
    o_win = []
    for g in range(G):
        sc = add_per_query(_dot_t(q4[g], kw_ref[0, pl.ds(w0, wk), lanes(g)]), wbias)
        e = jnp.exp2(sc - jnp.max(sc, -1, keepdims=True))
        o_win.append(_normalize(_dot(e.astype(BF), vw_ref[0, pl.ds(w0, wk), lanes(g)])))

    gw = NSA_HPG * LANE
    e_row = lax.broadcasted_iota(jnp.int32, (LANE, 3 * gw), 0)
    e_col = lax.broadcasted_iota(jnp.int32, (LANE, 3 * gw), 1)
    branch = jnp.where(e_col >= 2 * gw, 2, jnp.where(e_col >= gw, 1, 0))
    head = lax.shift_right_logical(e_col - branch * gw, 7)
    expand = jnp.where(e_row == 3 * head + branch, 1.0, 0.0).astype(BF)

    def heads_on_lanes(x):
        return jnp.concatenate([x[h * T:(h + 1) * T] for h in range(NSA_HPG)], axis=1)

    outs = []
    for g in range(G):
        gs = g_ref[0, :, lanes(g)]
        hi = gs.astype(BF)
        r1 = gs - hi.astype(F32)
        mid = r1.astype(BF)
        lo = (r1 - mid.astype(F32)).astype(BF)
        gx = _dot(hi, expand) + _dot(mid, expand) + _dot(lo, expand)
        outs.append(gx[:, 0:gw] * heads_on_lanes(o_cmp[g])
                    + gx[:, gw:2 * gw] * heads_on_lanes(_normalize(slc_acc[g]))
                    + gx[:, 2 * gw:3 * gw] * heads_on_lanes(o_win[g]))
    o_ref[0] = jnp.concatenate(outs, axis=1).astype(o_ref.dtype)


def _nsa_attention(q, kcmp, vcmp, ks, vs, kw, vw, gates, ov, *, tq, tk, top_k):
    bsz, seq, wide = q.shape
    n16 = kcmp.shape[1]
    gw = NSA_GROUPS * LANE
    assert tk % tq == 0 and seq % tk == 0 and seq >= WINDOW + tq
    kern = functools.partial(_nsa_attn_kernel, tk=tk, top_k=top_k)
    cblk = pl.BlockSpec((1, n16, gw), lambda b, i: (b, 0, 0))
    sblk = pl.BlockSpec((1, seq, gw), lambda b, i: (b, 0, 0))
    return pl.pallas_call(
        kern,
        grid=(bsz, seq // tq),
        in_specs=[pl.BlockSpec((1, tq, wide), lambda b, i: (b, i, 0)), cblk, cblk, sblk, sblk, sblk, sblk,
                  pl.BlockSpec((1, tq, gw), lambda b, i: (b, i, 0)), _resident(ov.shape)],
        out_specs=pl.BlockSpec((1, tq, wide), lambda b, i: (b, i, 0)),
        out_shape=jax.ShapeDtypeStruct((bsz, seq, wide), BF),
        compiler_params=_params(("parallel", "arbitrary")),
        name="nsa_attn",
    )(q, kcmp, vcmp, ks, vs, kw, vw, gates, ov)


def _merge_kernel(x_ref, ab_ref, oc_ref, od_ref, wg_ref, bg_ref, wc_ref, wd_ref, wo_ref, g_ref, b_ref, o_ref):
    x = x_ref[...]
    gates = _dot(x.astype(BF), wg_ref[...]) + bg_ref[...]
    yc = _dot(oc_ref[...], wc_ref[...])
    yd = _dot(od_ref[...], wd_ref[...])
    merged = ab_ref[...] + jax.nn.sigmoid(gates[:, :D_MODEL]) * yc + jax.nn.sigmoid(gates[:, D_MODEL:]) * yd
    mix = _dot(merged.astype(BF), wo_ref[...])
    o_ref[...] = _ln(ALPHA * x + mix, g_ref[...], b_ref[...])


def _merge_ln(x, ab, oc, od, wg, bg, wc, wd, wo, g, b, *, tm):
    n, d = x.shape
    row = pl.BlockSpec((tm, d), lambda i: (i, 0))
    return pl.pallas_call(
        _merge_kernel,
        grid=(n // tm,),
        in_specs=[row, row, row, row] + [_resident(a.shape) for a in (wg, bg, wc, wd, wo, g, b)],
        out_specs=row,
        out_shape=jax.ShapeDtypeStruct((n, d), F32),
        compiler_params=_params(("parallel",)),
        name="merge_ln",
    )(x, ab, oc, od, wg, bg, wc, wd, wo, g, b)


def _linear_kernel(x_ref, w_ref, o_ref):
    o_ref[...] = _dot(x_ref[...].astype(BF), w_ref[...]).astype(o_ref.dtype)


def _linear(x, w, *, tm, dtype):
    n, d = x.shape
    return pl.pallas_call(
        _linear_kernel,
        grid=(n // tm,),
        in_specs=[pl.BlockSpec((tm, d), lambda i: (i, 0)), _resident(w.shape)],
        out_specs=pl.BlockSpec((tm, w.shape[1]), lambda i: (i, 0)),
        out_shape=jax.ShapeDtypeStruct((n, w.shape[1]), dtype),
        compiler_params=_params(("parallel",)),
        name="mem_kv",
    )(x, w)


def _xattn_kernel(x_ref, k_ref, v_ref, wq_ref, wo_ref, g_ref, b_ref, o_ref):
    x = x_ref[...]
    q = _dot(x.astype(BF), wq_ref[...]).astype(BF)
    k = k_ref[0]
    v = v_ref[0]
    heads = []
    for h in range(XATTN_HEADS):
        sl = slice(h * XATTN_DIM, (h + 1) * XATTN_DIM)
        s = _dot_t(q[:, sl], k[:, sl]) * (XATTN_DIM ** -0.5)
        e = jnp.exp(s - jnp.max(s, -1, keepdims=True))
        p = e / jnp.sum(e, -1, keepdims=True)
        heads.append(_dot(p.astype(BF), v[:, sl]))
    o = jnp.concatenate(heads, axis=1).astype(BF)
    o_ref[...] = _ln(ALPHA * x + _dot(o, wo_ref[...]), g_ref[...], b_ref[...])


def _xattn_ln(x, kv, wq, wo, g, b, *, tm, seq):
    n, d = x.shape
    tps = seq // tm
    mlen = kv.shape[1]
    hd = XATTN_HEADS * XATTN_DIM
    return pl.pallas_call(
        _xattn_kernel,
        grid=(n // tm,),
        in_specs=[pl.BlockSpec((tm, d), lambda i: (i, 0)),
                  pl.BlockSpec((1, mlen, hd), lambda i: (i // tps, 0, 0)),
                  pl.BlockSpec((1, mlen, hd), lambda i: (i // tps, 0, 1))]
        + [_resident(a.shape) for a in (wq, wo, g, b)],
        out_specs=pl.BlockSpec((tm, d), lambda i: (i, 0)),
        out_shape=jax.ShapeDtypeStruct((n, d), F32),
        compiler_params=_params(("parallel",)),
        name="xattn_ln",
    )(x, kv, kv, wq, wo, g, b)


def _rope_tab(pos, dim):
    inv = ROPE_THETA ** (-(jnp.arange(0, dim, 2, dtype=F32) / dim))
    ang = pos[:, None] * inv[None, :]
    return jnp.cos(ang), jnp.sin(ang)


def _rot_cols(w, half):
    return jnp.concatenate([-w[..., half:2 * half], w[..., :half]], axis=-1)


def _pad_slots(w, n_slots, width):
    lead = w.shape[:-1]
    w = w.reshape(lead + (n_slots, width))
    w = jnp.pad(w, [(0, 0)] * len(lead) + [(0, 0), (0, LANE - width)])
    return w.reshape(lead + (n_slots * LANE,))


def _rot_slots(w, n_slots, width, half):
    lead = w.shape[:-1]
    w = w.reshape(lead + (n_slots, width))
    return _rot_cols(w, half).reshape(lead + (n_slots * width,))


def _pad_rows(w, n_slots, width):
    d = w.shape[-1]
    w = w.reshape(n_slots, width, d)
    w = jnp.pad(w, [(0, 0), (0, LANE - width), (0, 0)])
    return w.reshape(n_slots * LANE, d)


def _layer_params(l, p):
    w_in, b_in = p["w_in"][l], p["b_in"][l]

    def cols(o, wd):
        return w_in[:, o:o + wd], b_in[o:o + wd]

    out = {}
    out["w_ab"] = jnp.concatenate([w_in[:, 0:2560], w_in[:, _O_GA:_O_GC]], axis=1).astype(BF)
    out["b_ab"] = jnp.concatenate([b_in[0:2560], b_in[_O_GA:_O_GC]])[None, :]
    wkr, bkr = cols(_O_KROPE, MLA_ROPE)
    half = MLA_ROPE // 2
    padk = lambda a: jnp.pad(a, [(0, 0)] * (a.ndim - 1) + [(0, LANE - MLA_ROPE)])
    out["w_c"] = jnp.concatenate([w_in[:, _O_QLAT:_O_KROPE], padk(wkr), padk(_rot_cols(wkr, half))], axis=1).astype(BF)
    out["b_c"] = jnp.concatenate([b_in[_O_QLAT:_O_KROPE], padk(bkr), padk(_rot_cols(bkr, half))])[None, :]
    wuq = p["mla_wuq"][l].reshape(MLA_Q_RANK, MLA_HEADS, MLA_NOPE + MLA_ROPE)
    rope_rot = _rot_cols(wuq[..., MLA_NOPE:], half)
    wqa = jnp.pad(wuq, [(0, 0), (0, 0), (0, LANE - MLA_NOPE - MLA_ROPE)])
    wqb = jnp.pad(rope_rot, [(0, 0), (0, 0), (MLA_NOPE, LANE - MLA_NOPE - MLA_ROPE)])
    out["wqa"] = wqa.reshape(MLA_Q_RANK, MLA_HEADS * LANE).astype(BF)
    out["wqb"] = wqb.reshape(MLA_Q_RANK, MLA_HEADS * LANE).astype(BF)
    wukv = p["mla_wukv"][l].reshape(MLA_KV_RANK, MLA_HEADS, MLA_NOPE + MLA_V)
    out["wk_c"] = jnp.pad(wukv[..., :MLA_NOPE], [(0, 0), (0, 0), (0, LANE - MLA_NOPE)]).reshape(MLA_KV_RANK, -1).astype(BF)
    out["wv_c"] = jnp.pad(wukv[..., MLA_NOPE:], [(0, 0), (0, 0), (0, LANE - MLA_V)]).reshape(MLA_KV_RANK, -1).astype(BF)
    wq, bq = cols(_O_NQ, NSA_HEADS * NSA_DIM)
    hd = NSA_DIM // 2
    pieces_w, pieces_b = [], []

    def add(w, b, slots, roped):
        pieces_w.append(_pad_slots(w, slots, NSA_DIM))
        pieces_b.append(_pad_slots(b, slots, NSA_DIM))
        if roped:
            pieces_w.append(_pad_slots(_rot_slots(w, slots, NSA_DIM, hd), slots, NSA_DIM))
            pieces_b.append(_pad_slots(_rot_slots(b, slots, NSA_DIM, hd), slots, NSA_DIM))

    add(wq, bq, NSA_HEADS, True)
    wkc, bkc = cols(_O_NKC, 128)
    wvc, bvc = cols(_O_NVC, 128)
    pieces_w += [wkc, wvc]
    pieces_b += [bkc, bvc]
    add(*cols(_O_NKS, 128), NSA_GROUPS, True)
    add(*cols(_O_NVS, 128), NSA_GROUPS, False)
    add(*cols(_O_NKW, 128), NSA_GROUPS, True)
    add(*cols(_O_NVW, 128), NSA_GROUPS, False)
    wg, bg = cols(_O_NGATE, NSA_HEADS * 3)
    pieces_w.append(_pad_slots(wg, NSA_GROUPS, NSA_HPG * 3))
    pieces_b.append(_pad_slots(bg, NSA_GROUPS, NSA_HPG * 3))
    out["w_d"] = jnp.concatenate(pieces_w, axis=1).astype(BF)
    out["b_d"] = jnp.concatenate(pieces_b)[None, :]

    def cmp_weights(w):
        eye = jnp.eye(NSA_GROUPS, dtype=F32)
        wp = jnp.pad(w, [(0, 0), (0, 0), (0, LANE - NSA_DIM)])
        full = jnp.einsum("lde,gh->lgdhe", wp, eye).reshape(CMP_BLOCK, NSA_GROUPS * NSA_DIM, NSA_GROUPS * LANE)
        return full.reshape(2, CMP_STRIDE * NSA_GROUPS * NSA_DIM, NSA_GROUPS * LANE).astype(BF)

    wck = p["nsa_wcmp_k"][l]
    out["wcmp_k"] = cmp_weights(wck)
    out["wcmp_kr"] = cmp_weights(_rot_cols(wck, hd))
    out["wcmp_v"] = cmp_weights(p["nsa_wcmp_v"][l])

    def pe_rows(pe):
        t = jnp.broadcast_to(pe[:, None, :], (CMP_BLOCK, NSA_GROUPS, NSA_DIM))
        return t.reshape(2, CMP_STRIDE * NSA_GROUPS * NSA_DIM)

    out["pe"] = jnp.concatenate([pe_rows(p["nsa_pe_k"][l]), pe_rows(p["nsa_pe_v"][l])], axis=0)
    out["w_g"] = w_in[:, _O_GC:].astype(BF)
    out["b_g"] = b_in[_O_GC:][None, :]
    out["wout_c"] = _pad_rows(p["mla_wout"][l], MLA_HEADS, MLA_V).astype(BF)
    out["wout_d"] = _pad_rows(p["nsa_wout"][l], NSA_HEADS, NSA_DIM).astype(BF)
    return out


def _tables(seq):
    pos = jnp.arange(seq, dtype=F32)
    c16, s16 = _rope_tab(pos, MLA_ROPE)
    one = jnp.ones((seq, MLA_NOPE), F32)
    zero = jnp.zeros((seq, MLA_NOPE), F32)
    tail = LANE - MLA_NOPE - MLA_ROPE
    cq = jnp.concatenate([one, c16, c16, jnp.ones((seq, tail), F32)], axis=1)
    sq = jnp.concatenate([zero, s16, s16, jnp.zeros((seq, tail), F32)], axis=1)
    ck = jnp.pad(jnp.concatenate([c16, c16], axis=1), [(0, 0), (0, LANE - MLA_ROPE)])
    sk = jnp.pad(jnp.concatenate([s16, s16], axis=1), [(0, 0), (0, LANE - MLA_ROPE)])
    c32, s32 = _rope_tab(pos, NSA_DIM)
    cn = jnp.pad(jnp.concatenate([c32, c32], axis=1), [(0, 0), (0, LANE - NSA_DIM)])
    sn = jnp.pad(jnp.concatenate([s32, s32], axis=1), [(0, 0), (0, LANE - NSA_DIM)])
    n16 = seq // CMP_STRIDE
    cend = (jnp.arange(n16) * CMP_STRIDE + CMP_BLOCK - 1).astype(F32)
    cc32, cs32 = _rope_tab(cend, NSA_DIM)
    ccg = jnp.pad(jnp.concatenate([cc32, cc32], axis=1), [(0, 0), (0, LANE - NSA_DIM)])
    csg = jnp.pad(jnp.concatenate([cs32, cs32], axis=1), [(0, 0), (0, LANE - NSA_DIM)])
    cc = jnp.concatenate([ccg] * NSA_GROUPS, axis=1)
    cs = jnp.concatenate([csg] * NSA_GROUPS, axis=1)
    n_cmp = (seq - CMP_BLOCK) // CMP_STRIDE + 1
    n_slc = seq // SLC_BLOCK
    cstart = jnp.arange(n16) * CMP_STRIDE
    sstart = jnp.arange(n_slc) * SLC_BLOCK
    ovl = (jnp.minimum(cstart[None, :] + CMP_BLOCK, sstart[:, None] + SLC_BLOCK)
           - jnp.maximum(cstart[None, :], sstart[:, None]))
    ovl = jnp.clip(ovl, 0).astype(F32) / CMP_BLOCK
    ovl = jnp.where(jnp.arange(n16)[None, :] < n_cmp, ovl, 0.0).astype(BF)
    pk = jnp.zeros((LANE, MLA_HEADS, LANE), F32)
    pk = pk.at[jnp.arange(MLA_ROPE), :, MLA_NOPE + jnp.arange(MLA_ROPE)].set(1.0)
    pk = pk.reshape(LANE, MLA_HEADS * LANE).astype(BF)
    return dict(cq=cq, sq=sq, ck=ck, sk=sk, cn=cn, sn=sn, cc=cc, cs=cs, ovl=ovl, pk=pk)


def kernel(x, mem, ffn1_w1, ffn1_w3, ffn1_w2, ln1_g, ln1_b, w_in, b_in, gmlp_ln_g, gmlp_ln_b, gmlp_ws, gmlp_bs, gmlp_wout, conv_w, conv_wout, mla_qnorm_g, mla_kvnorm_g, mla_wuq, mla_wukv, mla_wout, nsa_pe_k, nsa_pe_v, nsa_wcmp_k, nsa_wcmp_v, nsa_wout, w_o, ln2_g, ln2_b, xattn_wq, xattn_wk, xattn_wv, xattn_wo, ln3_g, ln3_b, ffn2_w1, ffn2_w3, ffn2_w2, ln4_g, ln4_b):
    bsz, seq, d = x.shape
    mlen = mem.shape[1]
    n = bsz * seq
    assert d == D_MODEL and seq % 512 == 0 and seq >= WINDOW + 256
    p = dict(w_in=w_in, b_in=b_in, mla_wuq=mla_wuq, mla_wukv=mla_wukv, mla_wout=mla_wout,
             nsa_pe_k=nsa_pe_k, nsa_pe_v=nsa_pe_v, nsa_wcmp_k=nsa_wcmp_k, nsa_wcmp_v=nsa_wcmp_v, nsa_wout=nsa_wout)
    tb = _tables(seq)
    tm = 512
    tm_ffn = 1024
    tf = D_FF // 11
    n16 = seq // CMP_STRIDE
    top_k = min(SLC_TOPK, seq // SLC_BLOCK)
    row = lambda a: a[None, :]

    stack_rows = lambda w: _to_bf16(w.reshape(w.shape[0] * w.shape[1], w.shape[2]))
    ffn1 = [stack_rows(w) for w in (ffn1_w1, ffn1_w3, ffn1_w2)]
    ffn2 = [stack_rows(w) for w in (ffn2_w1, ffn2_w3, ffn2_w2)]
    h = x.reshape(n, d)
    mem2 = mem.reshape(bsz * mlen, d)
    for l in range(DEPTH):
        lp = _layer_params(l, p)
        h = _ffn_ln(h, *ffn1, row(ln1_g[l]), row(ln1_b[l]), layer=l, tm=tm_ffn, tf=tf)
        ab = _mix_ab(h, lp["w_ab"], lp["b_ab"], row(gmlp_ln_g[l]), row(gmlp_ln_b[l]), gmlp_ws[l], gmlp_bs[l].T,
                     gmlp_wout[l].astype(BF), conv_w[l], conv_wout[l].astype(BF), tm=tm, seq=seq)
        qc, kc_, vc_ = _mla_proj(h, lp["w_c"], lp["b_c"], row(mla_qnorm_g[l]), row(mla_kvnorm_g[l]),
                                 lp["wqa"], lp["wqb"], lp["wk_c"], lp["wv_c"], tb["pk"],
                                 tb["cq"], tb["sq"], tb["ck"], tb["sk"], tm=tm, seq=seq)
        wide = MLA_HEADS * LANE
        oc = _flash_causal(qc.reshape(bsz, seq, wide), kc_.reshape(bsz, seq, wide), vc_.reshape(bsz, seq, wide), tq=1024, tk=1024, hp=2)
        qn, nkc, nvc, nks, nvs, nkw, nvw, gates = _nsa_proj(h, lp["w_d"], lp["b_d"], tb["cn"], tb["sn"], tm=tm, seq=seq)
        kcmp, vcmp = _nsa_compress(nkc.reshape(bsz, n16, CMP_STRIDE * 128), nvc.reshape(bsz, n16, CMP_STRIDE * 128),
                                   lp["pe"], lp["wcmp_k"], lp["wcmp_kr"], lp["wcmp_v"], tb["cc"], tb["cs"])
        gw = NSA_GROUPS * LANE
        od = _nsa_attention(qn.reshape(bsz, seq, NSA_HEADS * LANE), kcmp, vcmp,
                            nks.reshape(bsz, seq, gw), nvs.reshape(bsz, seq, gw),
                            nkw.reshape(bsz, seq, gw), nvw.reshape(bsz, seq, gw),
                            gates.reshape(bsz, seq, gw), tb["ovl"], tq=512, tk=512, top_k=top_k)
        h = _merge_ln(h, ab, oc.reshape(n, wide), od.reshape(n, NSA_HEADS * LANE), lp["w_g"], lp["b_g"],
                      lp["wout_c"], lp["wout_d"], w_o[l].astype(BF), row(ln2_g[l]), row(ln2_b[l]), tm=tm)
        kv = _linear(mem2, jnp.concatenate([xattn_wk[l], xattn_wv[l]], axis=1).astype(BF), tm=min(256, bsz * mlen), dtype=BF)
        h = _xattn_ln(h, kv.reshape(bsz, mlen, 2 * XATTN_HEADS * XATTN_DIM), xattn_wq[l].astype(BF),
                      xattn_wo[l].astype(BF), row(ln3_g[l]), row(ln3_b[l]), tm=tm, seq=seq)
        h = _ffn_ln(h, *ffn2, row(ln4_g[l]), row(ln4_b[l]), layer=l, tm=tm_ffn, tf=tf)
    return h.reshape(bsz, seq, d)
```

```python
import functools

import jax
import jax.numpy as jnp
from jax import lax
from jax.experimental import pallas as pl
from jax.experimental.pallas import tpu as pltpu

BF = jnp.bfloat16
F32 = jnp.float32

D_MODEL = 1024
D_FF = 2816
LN_EPS = 1e-5
RMS_EPS = 1e-6
ROPE_THETA = 10000.0
DEPTH = 2
ALPHA = (2 * DEPTH) ** 0.25
NEG = -1e30
LOG2_E = 1.4426950408889634
DENOM_LANE = 64
MASK_BIG = 2.0 ** 100

GMLP_CHUNK = 128
GMLP_GROUPS = 4
GMLP_WIDTH = 512
CONV_WIDTH = 512
CONV_K = 3
MLA_HEADS = 8
MLA_Q_RANK = 256
MLA_KV_RANK = 128
MLA_NOPE = 64
MLA_ROPE = 32
MLA_V = 64
NSA_HEADS = 8
NSA_GROUPS = 2
NSA_HPG = 4
NSA_DIM = 64
CMP_BLOCK = 32
CMP_STRIDE = 16
SLC_BLOCK = 64
SLC_TOPK = 8
WINDOW = 512
XATTN_HEADS = 4
XATTN_DIM = 128

LANE = 128
CONV_HALO = 8
VMEM_LIMIT = 56 * 1024 * 1024

_O_U, _O_V, _O_CB, _O_CC, _O_CH = 0, 512, 1024, 1536, 2048
_O_QLAT, _O_KVLAT, _O_KROPE = 2560, 2816, 2944
_O_NQ, _O_NKC, _O_NVC, _O_NKS, _O_NVS, _O_NKW, _O_NVW, _O_NGATE = 2976, 3488, 3616, 3744, 3872, 4000, 4128, 4256
_O_GA, _O_GB, _O_GC, _O_GD = 4280, 5304, 6328, 7352


def _dot(a, b):
    return jnp.dot(a, b, preferred_element_type=F32)


def _dot_t(a, b):
    return lax.dot_general(a, b, (((1,), (1,)), ((), ())), preferred_element_type=F32)


def _ln(y, g, b):
    mu = jnp.mean(y, -1, keepdims=True)
    d = y - mu
    var = jnp.mean(d * d, -1, keepdims=True)
    return d * lax.rsqrt(var + LN_EPS) * g + b


def _rms(x, g):
    return x * lax.rsqrt(jnp.mean(x * x, -1, keepdims=True) + RMS_EPS) * g


def _resident(shape):
    n = len(shape)
    return pl.BlockSpec(shape, lambda *_: (0,) * n, pipeline_mode=pl.Buffered(1))


def _params(sem):
    return pltpu.CompilerParams(dimension_semantics=sem, vmem_limit_bytes=VMEM_LIMIT)


def _cast_kernel(x_ref, o_ref):
    o_ref[...] = x_ref[...].astype(o_ref.dtype)


def _to_bf16(w, *, rows=256):
    r, c = w.shape
    rows = min(rows, r)
    assert r % rows == 0
    return pl.pallas_call(
        _cast_kernel,
        grid=(r // rows,),
        in_specs=[pl.BlockSpec((rows, c), lambda i: (i, 0))],
        out_specs=pl.BlockSpec((rows, c), lambda i: (i, 0)),
        out_shape=jax.ShapeDtypeStruct((r, c), BF),
        compiler_params=_params(("parallel",)),
        name="to_bf16",
    )(w)


def _ffn_ln_kernel(x_ref, w1_ref, w3_ref, w2_ref, g_ref, b_ref, o_ref, *, tf):
    x = x_ref[...]
    xb = x.astype(BF)
    acc = None
    for c in range(w1_ref.shape[1] // tf):
        cols = slice(c * tf, (c + 1) * tf)
        h1 = _dot(xb, w1_ref[:, cols])
        h3 = _dot(xb, w3_ref[:, cols])
        hh = (h1 * jax.nn.sigmoid(h1)) * h3
        part = _dot(hh.astype(BF), w2_ref[cols, :])
        acc = part if acc is None else acc + part
    o_ref[...] = _ln(ALPHA * x + 0.5 * acc, g_ref[...], b_ref[...])


def _layer_block(shape, layer):
    return pl.BlockSpec(shape, lambda *_: (layer, 0), pipeline_mode=pl.Buffered(1))


def _ffn_ln(x, w1, w3, w2, g, b, *, layer, tm, tf):
    n, d = x.shape
    f = w1.shape[1]
    return pl.pallas_call(
        functools.partial(_ffn_ln_kernel, tf=tf),
        grid=(n // tm,),
        in_specs=[pl.BlockSpec((tm, d), lambda i: (i, 0)), _layer_block((d, f), layer), _layer_block((d, f), layer),
                  _layer_block((f, d), layer), _resident(g.shape), _resident(b.shape)],
        out_specs=pl.BlockSpec((tm, d), lambda i: (i, 0)),
        out_shape=jax.ShapeDtypeStruct((n, d), F32),
        compiler_params=_params(("parallel",)),
        name="ffn_ln",
    )(x, w1, w3, w2, g, b)


def _ab_kernel(h_ref, w_ref, b_ref, lng_ref, lnb_ref, ws_ref, bst_ref, wga_ref, cw_ref, wcb_ref,
               o_ref, prev_ref, *, tiles_per_seq):
    i = pl.program_id(0)
    tm = h_ref.shape[0]
    hb = h_ref[...].astype(BF)

    def proj(c0, width):
        return _dot(hb, w_ref[:, c0:c0 + width]) + b_ref[:, c0:c0 + width]

    u = proj(0, GMLP_WIDTH)
    v = _ln(proj(512, GMLP_WIDTH), lng_ref[...], lnb_ref[...]).astype(BF)
    row = lax.broadcasted_iota(jnp.int32, (GMLP_CHUNK, GMLP_CHUNK), 0)
    col = lax.broadcasted_iota(jnp.int32, (GMLP_CHUNK, GMLP_CHUNK), 1)
    gd = GMLP_WIDTH // GMLP_GROUPS
    wgs = [jnp.where(row >= col, ws_ref[g], 0.0).astype(BF) for g in range(GMLP_GROUPS)]
    chunks = []
    for c in range(tm // GMLP_CHUNK):
        r0 = c * GMLP_CHUNK
        chunks.append(jnp.concatenate(
            [_dot(wgs[g], v[r0:r0 + GMLP_CHUNK, g * gd:(g + 1) * gd]) + bst_ref[:, g:g + 1]
             for g in range(GMLP_GROUPS)], axis=1))
    s = jnp.concatenate(chunks, axis=0)
    ya = _dot((u * s).astype(BF), wga_ref[...])

    cb = proj(1024, CONV_WIDTH)
    z = proj(1536, CONV_WIDTH) * proj(2048, CONV_WIDTH)

    @pl.when(i % tiles_per_seq == 0)
    def _():
        prev_ref[...] = jnp.zeros_like(prev_ref)

    zext = jnp.concatenate([prev_ref[...], z], axis=0)
    z1 = pltpu.roll(zext, 1, 0)[CONV_HALO:]
    z2 = pltpu.roll(zext, 2, 0)[CONV_HALO:]
    y = cw_ref[0:1, :] * z2 + cw_ref[1:2, :] * z1 + cw_ref[2:3, :] * z
    prev_ref[...] = z[tm - CONV_HALO:, :]
    yb = _dot((cb * y).astype(BF), wcb_ref[...])

    ga = proj(2560, D_MODEL)
    gb = proj(3584, D_MODEL)
    o_ref[...] = jax.nn.sigmoid(ga) * ya + jax.nn.sigmoid(gb) * yb


def _mix_ab(h, w, b, lng, lnb, ws, bst, wga, cw, wcb, *, tm, seq):
    n, d = h.shape
    kern = functools.partial(_ab_kernel, tiles_per_seq=seq // tm)
    return pl.pallas_call(
        kern,
        grid=(n // tm,),
        in_specs=[pl.BlockSpec((tm, d), lambda i: (i, 0))] + [_resident(a.shape) for a in (w, b, lng, lnb, ws, bst, wga, cw, wcb)],
        out_specs=pl.BlockSpec((tm, d), lambda i: (i, 0)),
        out_shape=jax.ShapeDtypeStruct((n, d), F32),
        scratch_shapes=[pltpu.VMEM((CONV_HALO, CONV_WIDTH), F32)],
        compiler_params=_params(("arbitrary",)),
        name="mix_ab",
    )(h, w, b, lng, lnb, ws, bst, wga, cw, wcb)


def _mla_proj_kernel(h_ref, w_ref, b_ref, qg_ref, kvg_ref, wqa_ref, wqb_ref, wk_ref, wv_ref, pk_ref,
                     cq_ref, sq_ref, ck_ref, sk_ref, q_ref, k_ref, v_ref):
    hb = h_ref[...].astype(BF)
    z = _dot(hb, w_ref[...]) + b_ref[...]
    qn = _rms(z[:, 0:256], qg_ref[...]).astype(BF)
    kvn = _rms(z[:, 256:384], kvg_ref[...]).astype(BF)
    cq = jnp.concatenate([cq_ref[...]] * MLA_HEADS, axis=1)
    sq = jnp.concatenate([sq_ref[...]] * MLA_HEADS, axis=1)
    scale = (MLA_NOPE + MLA_ROPE) ** -0.5 * LOG2_E
    q = (_dot(qn, wqa_ref[...]) * cq + _dot(qn, wqb_ref[...]) * sq) * scale
    q_ref[...] = q.astype(BF)
    kpe = (z[:, 384:512] * ck_ref[...] + z[:, 512:640] * sk_ref[...]).astype(BF)
    k_ref[...] = (_dot(kvn, wk_ref[...]) + _dot(kpe, pk_ref[...])).astype(BF)
    v_ref[...] = (_dot(kvn, wv_ref[...]) + _denom_ones(v_ref.shape[1])).astype(BF)


def _mla_proj(h, w, b, qg, kvg, wqa, wqb, wk, wv, pk, cq, sq, ck, sk, *, tm, seq):
    n, d = h.shape
    tps = seq // tm
    tab = pl.BlockSpec((tm, LANE), lambda i: (i % tps, 0))
    wide = MLA_HEADS * LANE
    out = jax.ShapeDtypeStruct((n, wide), BF)
    return pl.pallas_call(
        _mla_proj_kernel,
        grid=(n // tm,),
        in_specs=[pl.BlockSpec((tm, d), lambda i: (i, 0))]
        + [_resident(a.shape) for a in (w, b, qg, kvg, wqa, wqb, wk, wv, pk)] + [tab] * 4,
        out_specs=[pl.BlockSpec((tm, wide), lambda i: (i, 0))] * 3,
        out_shape=[out, out, out],
        compiler_params=_params(("parallel",)),
        name="mla_proj",
    )(h, w, b, qg, kvg, wqa, wqb, wk, wv, pk, cq, sq, ck, sk)


def _online_softmax_step(s, v, carry):
    m, acc = carry
    m_new = jnp.maximum(m, jnp.max(s, -1, keepdims=True))
    p = jnp.exp2(s - m_new)
    acc = jnp.exp2(m - m_new) * acc + _dot(p.astype(BF), v)
    return m_new, acc


def _softmax_init(rows, width):
    return (jnp.full((rows, 1), NEG, F32), jnp.zeros((rows, width), F32))


def _normalize(acc):
    return acc * (1.0 / acc[:, DENOM_LANE:DENOM_LANE + 1])


def _denom_ones(width):
    lane = lax.broadcasted_iota(jnp.int32, (1, width), 1)
    return jnp.where(lane % LANE == DENOM_LANE, 1.0, 0.0)


def _flash_kernel(q_ref, k_ref, v_ref, o_ref, *, tq, tk, hp):
    qi = pl.program_id(2)
    q0 = qi * tq
    qs = [q_ref[0, :, h * LANE:(h + 1) * LANE] for h in range(hp)]

    def tile(j, carries, width, diagonal):
        k0 = pl.multiple_of(j * width, width)
        out = []
        for h in range(hp):
            s = _dot_t(qs[h], k_ref[0, pl.ds(k0, width), h * LANE:(h + 1) * LANE])
            if diagonal:
                r = lax.broadcasted_iota(jnp.int32, (tq, width), 0)
                c = lax.broadcasted_iota(jnp.int32, (tq, width), 1)
                s = jnp.where(c <= r, s, NEG)
            out.append(_online_softmax_step(s, v_ref[0, pl.ds(k0, width), h * LANE:(h + 1) * LANE], carries[h]))
        return tuple(out)

    init = tuple(_softmax_init(tq, LANE) for _ in range(hp))
    carries = lax.fori_loop(0, qi, lambda j, c: tile(j, c, tq, False), init)

    half = tq // 2
    carries = tile(2 * qi, carries, half, True)
    r = lax.broadcasted_iota(jnp.int32, (half, half), 0)
    c = lax.broadcasted_iota(jnp.int32, (half, half), 1)
    k1 = pl.multiple_of(q0 + half, half)
    out = []
    for h in range(hp):
        m, acc = carries[h]
        s = _dot_t(qs[h][half:], k_ref[0, pl.ds(k1, half), h * LANE:(h + 1) * LANE])
        m2, acc2 = _online_softmax_step(jnp.where(c <= r, s, NEG), v_ref[0, pl.ds(k1, half), h * LANE:(h + 1) * LANE],
                                        (m[half:], acc[half:]))
        out.append(jnp.concatenate([acc[:half], acc2], axis=0))
    o_ref[0] = jnp.concatenate([_normalize(acc) for acc in out], axis=1).astype(o_ref.dtype)


def _flash_causal(q, k, v, *, tq, tk, hp):
    bsz, seq, wide = q.shape
    heads = wide // LANE
    assert tk == tq and seq % tq == 0
    kern = functools.partial(_flash_kernel, tq=tq, tk=tk, hp=hp)
    return pl.pallas_call(
        kern,
        grid=(bsz, heads // hp, seq // tq),
        in_specs=[
            pl.BlockSpec((1, tq, hp * LANE), lambda b, h, i: (b, i, h)),
            pl.BlockSpec((1, seq, hp * LANE), lambda b, h, i: (b, 0, h)),
            pl.BlockSpec((1, seq, hp * LANE), lambda b, h, i: (b, 0, h)),
        ],
        out_specs=pl.BlockSpec((1, tq, hp * LANE), lambda b, h, i: (b, i, h)),
        out_shape=jax.ShapeDtypeStruct((bsz, seq, wide), BF),
        compiler_params=_params(("parallel", "parallel", "arbitrary")),
        name="mla_flash",
    )(q, k, v)


def _nsa_proj_kernel(h_ref, w_ref, b_ref, c_ref, s_ref, q_ref, kc_ref, vc_ref, ks_ref, vs_ref, kw_ref, vw_ref, g_ref,
                     *, tiles_per_seq):
    tm = h_ref.shape[0]
    hb = h_ref[...].astype(BF)
    pos = (pl.program_id(0) % tiles_per_seq) * tm + lax.broadcasted_iota(jnp.int32, (tm, LANE), 0)
    lane = lax.broadcasted_iota(jnp.int32, (tm, LANE), 1)
    tag = jnp.where(lane == NSA_DIM + lax.shift_right_logical(pos, 6), MASK_BIG, 0.0)
    tag2 = jnp.concatenate([tag] * NSA_GROUPS, axis=1)

    def proj(c0, width):
        return _dot(hb, w_ref[:, c0:c0 + width]) + b_ref[:, c0:c0 + width]

    c = c_ref[...]
    s = s_ref[...]
    c8 = jnp.concatenate([c] * NSA_HEADS, axis=1)
    s8 = jnp.concatenate([s] * NSA_HEADS, axis=1)
    c2 = jnp.concatenate([c] * NSA_GROUPS, axis=1)
    s2 = jnp.concatenate([s] * NSA_GROUPS, axis=1)
    q = (proj(0, 1024) * c8 + proj(1024, 1024) * s8) * (NSA_DIM ** -0.5 * LOG2_E)
    q_ref[...] = q.astype(BF)
    kc_ref[...] = proj(2048, 128)
    vc_ref[...] = proj(2176, 128)
    ks_ref[...] = (proj(2304, 256) * c2 + proj(2560, 256) * s2 + tag2).astype(BF)
    ones = _denom_ones(NSA_GROUPS * LANE)
    vs_ref[...] = (proj(2816, 256) + ones).astype(BF)
    kw_ref[...] = (proj(3072, 256) * c2 + proj(3328, 256) * s2).astype(BF)
    vw_ref[...] = (proj(3584, 256) + ones).astype(BF)
    g_ref[...] = jax.nn.sigmoid(proj(3840, 256))


def _nsa_proj(h, w, b, cn, sn, *, tm, seq):
    n, d = h.shape
    tps = seq // tm
    tab = pl.BlockSpec((tm, LANE), lambda i: (i % tps, 0))

    def out(width, dt):
        return pl.BlockSpec((tm, width), lambda i: (i, 0)), jax.ShapeDtypeStruct((n, width), dt)

    outs = [out(1024, BF), out(128, F32), out(128, F32), out(256, BF), out(256, BF), out(256, BF), out(256, BF), out(256, F32)]
    assert seq // SLC_BLOCK <= LANE - NSA_DIM
    return pl.pallas_call(
        functools.partial(_nsa_proj_kernel, tiles_per_seq=tps),
        grid=(n // tm,),
        in_specs=[pl.BlockSpec((tm, d), lambda i: (i, 0)), _resident(w.shape), _resident(b.shape), tab, tab],
        out_specs=[o[0] for o in outs],
        out_shape=[o[1] for o in outs],
        compiler_params=_params(("parallel",)),
        name="nsa_proj",
    )(h, w, b, cn, sn)


def _nsa_cmp_kernel(kc_ref, vc_ref, pe_ref, wk_ref, wkr_ref, wv_ref, c_ref, s_ref, kcmp_ref, vcmp_ref):
    n16 = kc_ref.shape[1]

    def halves(x_ref, pe_lo, pe_hi):
        a = x_ref[0]
        nxt = pltpu.roll(a, n16 - 1, 0)
        return (a + pe_lo).astype(BF), (nxt + pe_hi).astype(BF)

    klo, khi = halves(kc_ref, pe_ref[0:1, :], pe_ref[1:2, :])
    kc = _dot(klo, wk_ref[0]) + _dot(khi, wk_ref[1])
    kcr = _dot(klo, wkr_ref[0]) + _dot(khi, wkr_ref[1])
    kcmp_ref[0] = (kc * c_ref[...] + kcr * s_ref[...]).astype(BF)
    vlo, vhi = halves(vc_ref, pe_ref[2:3, :], pe_ref[3:4, :])
    vcmp_ref[0] = (_dot(vlo, wv_ref[0]) + _dot(vhi, wv_ref[1])).astype(BF)


def _nsa_compress(kc, vc, pe, wk, wkr, wv, cc, sc):
    bsz, n16, wide = kc.shape
    blk = pl.BlockSpec((1, n16, wide), lambda b: (b, 0, 0))
    oblk = pl.BlockSpec((1, n16, NSA_GROUPS * LANE), lambda b: (b, 0, 0))
    osh = jax.ShapeDtypeStruct((bsz, n16, NSA_GROUPS * LANE), BF)
    return pl.pallas_call(
        _nsa_cmp_kernel,
        grid=(bsz,),
        in_specs=[blk, blk] + [_resident(a.shape) for a in (pe, wk, wkr, wv, cc, sc)],
        out_specs=[oblk, oblk],
        out_shape=[osh, osh],
        compiler_params=_params(("parallel",)),
        name="nsa_compress",
    )(kc, vc, pe, wk, wkr, wv, cc, sc)


def _nsa_attn_kernel(q_ref, kcmp_ref, vcmp_ref, ks_ref, vs_ref, kw_ref, vw_ref, g_ref, ov_ref, o_ref, *, tk, top_k):
    qi = pl.program_id(1)
    T = q_ref.shape[1]
    R = NSA_HPG * T
    G = NSA_GROUPS
    q0 = qi * T
    qpos = lax.broadcasted_iota(jnp.int32, (T, 1), 0) + q0
    ncp = kcmp_ref.shape[1]
    nb = ov_ref.shape[0]
    ov_t = ov_ref[...]

    def add_per_query(x, b):
        w = x.shape[1]
        return (x.reshape(NSA_HPG, T, w) + b[None]).reshape(R, w)

    def lanes(g):
        return slice(g * LANE, (g + 1) * LANE)

    q4 = [jnp.concatenate([q_ref[0, :, (g * NSA_HPG + h) * LANE:(g * NSA_HPG + h + 1) * LANE]
                           for h in range(NSA_HPG)], axis=0) for g in range(G)]

    cmp_end = lax.broadcasted_iota(jnp.int32, (1, ncp), 1) * CMP_STRIDE + (CMP_BLOCK - 1)
    cbias = jnp.where(cmp_end <= qpos, 0.0, NEG)
    any_valid = jnp.where(qpos >= CMP_BLOCK - 1, 1.0, 0.0)
    jr = lax.broadcasted_iota(jnp.int32, (nb, 1), 0)
    jrf = jr.astype(F32)
    jq = lax.shift_right_logical(lax.broadcasted_iota(jnp.int32, (1, T), 1) + q0, 6)
    forced = (jr == 0) | (jr == jq) | (jr == jq - 1)
    eye_t = jnp.where(lax.broadcasted_iota(jnp.int32, (T, T), 0) == lax.broadcasted_iota(jnp.int32, (T, T), 1),
                      1.0, 0.0).astype(BF)
    o_cmp, q4s = [], []
    for g in range(G):
        sm = _dot_t(q4[g], kcmp_ref[0, :, lanes(g)]).reshape(NSA_HPG, T, ncp) + cbias[None]
        e = jnp.exp2(sm - jnp.max(sm, -1, keepdims=True))
        p = e * (any_valid[None] / jnp.sum(e, -1, keepdims=True))
        o_cmp.append(_dot(p.reshape(R, ncp).astype(BF), vcmp_ref[0, :, lanes(g)]))
        psum = p[0] + p[1] + p[2] + p[3]
        hi = psum.astype(BF)
        r1 = psum - hi.astype(F32)
        mid = r1.astype(BF)
        lo = (r1 - mid.astype(F32)).astype(BF)
        imp = _dot_t(ov_t, hi) + _dot_t(ov_t, mid) + _dot_t(ov_t, lo)
        imp = jnp.where(forced, 1e9, imp)
        imp = jnp.where(jr <= jq, imp, -1.0)
        work = imp
        sel = jnp.zeros_like(imp)
        for _ in range(top_k):
            mx = jnp.max(work, 0, keepdims=True)
            idx = jnp.min(jnp.where(work == mx, jrf, float(nb)), 0, keepdims=True)
            pick = jrf == idx
            sel = jnp.where(pick, 1.0, sel)
            work = jnp.where(pick, -2.0, work)
        unsel_t = jnp.where(imp >= 0.0, sel, 0.0) - 1.0
        pad_t = [jnp.zeros((NSA_DIM, T), F32), unsel_t]
        if LANE - NSA_DIM - nb:
            pad_t.append(jnp.zeros((LANE - NSA_DIM - nb, T), F32))
        unsel = _dot_t(eye_t, jnp.concatenate(pad_t, axis=0).astype(BF)).astype(BF)
        q4s.append(add_per_query(q4[g], unsel))

    def slc_tile(j, carries, width, diagonal):
        k0 = pl.multiple_of(j * width, width)
        out = []
        for g in range(G):
            sc = _dot_t(q4s[g], ks_ref[0, pl.ds(k0, width), lanes(g)])
            if diagonal:
                sc = add_per_query(sc, jnp.where((lax.broadcasted_iota(jnp.int32, (1, width), 1) + k0) <= qpos, 0.0, NEG))
            out.append(_online_softmax_step(sc, vs_ref[0, pl.ds(k0, width), lanes(g)], carries[g]))
        return tuple(out)

    carries = lax.fori_loop(0, q0 // tk, lambda j, c: slc_tile(j, c, tk, False),
                            tuple(_softmax_init(R, LANE) for _ in range(G)))
    carries = slc_tile(q0 // tk, carries, tk, True)
    slc_acc = [acc for (_, acc) in carries]

    wk = WINDOW + T
    w0 = pl.multiple_of(jnp.maximum(q0 - WINDOW, 0), T)
    dist = qpos - (lax.broadcasted_iota(jnp.int32, (1, wk), 1) + w0)
    wbias = jnp.where((dist >= 0) & (dist < WINDOW), 0.0, NEG)
    o_win = []
    for g in range(G):
        sc = add_per_query(_dot_t(q4[g], kw_ref[0, pl.ds(w0, wk), lanes(g)]), wbias)
        e = jnp.exp2(sc - jnp.max(sc, -1, keepdims=True))
        o_win.append(_normalize(_dot(e.astype(BF), vw_ref[0, pl.ds(w0, wk), lanes(g)])))

    gw = NSA_HPG * LANE
    e_row = lax.broadcasted_iota(jnp.int32, (LANE, 3 * gw), 0)
    e_col = lax.broadcasted_iota(jnp.int32, (LANE, 3 * gw), 1)
    branch = jnp.where(e_col >= 2 * gw, 2, jnp.where(e_col >= gw, 1, 0))
    head = lax.shift_right_logical(e_col - branch * gw, 7)
    expand = jnp.where(e_row == 3 * head + branch, 1.0, 0.0).astype(BF)

    def heads_on_lanes(x):
        return jnp.concatenate([x[h * T:(h + 1) * T] for h in range(NSA_HPG)], axis=1)

    outs = []
    for g in range(G):
        gs = g_ref[0, :, lanes(g)]
        hi = gs.astype(BF)
        lo = (gs - hi.astype(F32)).astype(BF)
        gx = _dot(hi, expand) + _dot(lo, expand)
        outs.append(gx[:, 0:gw] * heads_on_lanes(o_cmp[g])
                    + gx[:, gw:2 * gw] * heads_on_lanes(_normalize(slc_acc[g]))
                    + gx[:, 2 * gw:3 * gw] * heads_on_lanes(o_win[g]))
    o_ref[0] = jnp.concatenate(outs, axis=1).astype(o_ref.dtype)


def _nsa_attention(q, kcmp, vcmp, ks, vs, kw, vw, gates, ov, *, tq, tk, top_k):
    bsz, seq, wide = q.shape
    n16 = kcmp.shape[1]
    gw = NSA_GROUPS * LANE
    assert tk % tq == 0 and seq % tk == 0 and seq >= WINDOW + tq
    kern = functools.partial(_nsa_attn_kernel, tk=tk, top_k=top_k)
    cblk = pl.BlockSpec((1, n16, gw), lambda b, i: (b, 0, 0), pipeline_mode=pl.Buffered(1))
    sblk = pl.BlockSpec((1, seq, gw), lambda b, i: (b, 0, 0), pipeline_mode=pl.Buffered(1))
    return pl.pallas_call(
        kern,
        grid=(bsz, seq // tq),
        in_specs=[pl.BlockSpec((1, tq, wide), lambda b, i: (b, i, 0)), cblk, cblk, sblk, sblk, sblk, sblk,
                  pl.BlockSpec((1, tq, gw), lambda b, i: (b, i, 0)), _resident(ov.shape)],
        out_specs=pl.BlockSpec((1, tq, wide), lambda b, i: (b, i, 0)),
        out_shape=jax.ShapeDtypeStruct((bsz, seq, wide), BF),
        compiler_params=_params(("parallel", "arbitrary")),
        name="nsa_attn",
    )(q, kcmp, vcmp, ks, vs, kw, vw, gates, ov)


def _merge_kernel(x_ref, ab_ref, oc_ref, od_ref, wg_ref, bg_ref, wc_ref, wd_ref, wo_ref, g_ref, b_ref, o_ref):
    x = x_ref[...]
    gates = _dot(x.astype(BF), wg_ref[...]) + bg_ref[...]
    yc = _dot(oc_ref[...], wc_ref[...])
    yd = _dot(od_ref[...], wd_ref[...])
    merged = ab_ref[...] + jax.nn.sigmoid(gates[:, :D_MODEL]) * yc + jax.nn.sigmoid(gates[:, D_MODEL:]) * yd
    mix = _dot(merged.astype(BF), wo_ref[...])
    o_ref[...] = _ln(ALPHA * x + mix, g_ref[...], b_ref[...])


def _merge_ln(x, ab, oc, od, wg, bg, wc, wd, wo, g, b, *, tm):
    n, d = x.shape
    row = pl.BlockSpec((tm, d), lambda i: (i, 0))
    return pl.pallas_call(
        _merge_kernel,
        grid=(n // tm,),
        in_specs=[row, row, row, row] + [_resident(a.shape) for a in (wg, bg, wc, wd, wo, g, b)],
        out_specs=row,
        out_shape=jax.ShapeDtypeStruct((n, d), F32),
        compiler_params=_params(("parallel",)),
        name="merge_ln",
    )(x, ab, oc, od, wg, bg, wc, wd, wo, g, b)


def _linear_kernel(x_ref, w_ref, o_ref):
    o_ref[...] = _dot(x_ref[...].astype(BF), w_ref[...]).astype(o_ref.dtype)


def _linear(x, w, *, tm, dtype):
    n, d = x.shape
    return pl.pallas_call(
        _linear_kernel,
        grid=(n // tm,),
        in_specs=[pl.BlockSpec((tm, d), lambda i: (i, 0)), _resident(w.shape)],
        out_specs=pl.BlockSpec((tm, w.shape[1]), lambda i: (i, 0)),
        out_shape=jax.ShapeDtypeStruct((n, w.shape[1]), dtype),
        compiler_params=_params(("parallel",)),
        name="mem_kv",
    )(x, w)


def _xattn_kernel(x_ref, k_ref, v_ref, wq_ref, wo_ref, g_ref, b_ref, o_ref):
    x = x_ref[...]
    q = _dot(x.astype(BF), wq_ref[...]).astype(BF)
    k = k_ref[0]
    v = v_ref[0]
    heads = []
    for h in range(XATTN_HEADS):
        sl = slice(h * XATTN_DIM, (h + 1) * XATTN_DIM)
        s = _dot_t(q[:, sl], k[:, sl]) * (XATTN_DIM ** -0.5)
        e = jnp.exp(s - jnp.max(s, -1, keepdims=True))
        p = e / jnp.sum(e, -1, keepdims=True)
        heads.append(_dot(p.astype(BF), v[:, sl]))
    o = jnp.concatenate(heads, axis=1).astype(BF)
    o_ref[...] = _ln(ALPHA * x + _dot(o, wo_ref[...]), g_ref[...], b_ref[...])


def _xattn_ln(x, kv, wq, wo, g, b, *, tm, seq):
    n, d = x.shape
    tps = seq // tm
    mlen = kv.shape[1]
    hd = XATTN_HEADS * XATTN_DIM
    return pl.pallas_call(
        _xattn_kernel,
        grid=(n // tm,),
        in_specs=[pl.BlockSpec((tm, d), lambda i: (i, 0)),
                  pl.BlockSpec((1, mlen, hd), lambda i: (i // tps, 0, 0)),
                  pl.BlockSpec((1, mlen, hd), lambda i: (i // tps, 0, 1))]
        + [_resident(a.shape) for a in (wq, wo, g, b)],
        out_specs=pl.BlockSpec((tm, d), lambda i: (i, 0)),
        out_shape=jax.ShapeDtypeStruct((n, d), F32),
        compiler_params=_params(("parallel",)),
        name="xattn_ln",
    )(x, kv, kv, wq, wo, g, b)


def _rope_tab(pos, dim):
    inv = ROPE_THETA ** (-(jnp.arange(0, dim, 2, dtype=F32) / dim))
    ang = pos[:, None] * inv[None, :]
    return jnp.cos(ang), jnp.sin(ang)


def _rot_cols(w, half):
    return jnp.concatenate([-w[..., half:2 * half], w[..., :half]], axis=-1)


def _pad_slots(w, n_slots, width):
    lead = w.shape[:-1]
    w = w.reshape(lead + (n_slots, width))
    w = jnp.pad(w, [(0, 0)] * len(lead) + [(0, 0), (0, LANE - width)])
    return w.reshape(lead + (n_slots * LANE,))


def _rot_slots(w, n_slots, width, half):
    lead = w.shape[:-1]
    w = w.reshape(lead + (n_slots, width))
    return _rot_cols(w, half).reshape(lead + (n_slots * width,))


def _pad_rows(w, n_slots, width):
    d = w.shape[-1]
    w = w.reshape(n_slots, width, d)
    w = jnp.pad(w, [(0, 0), (0, LANE - width), (0, 0)])
    return w.reshape(n_slots * LANE, d)


def _layer_params(l, p):
    w_in, b_in = p["w_in"][l], p["b_in"][l]

    def cols(o, wd):
        return w_in[:, o:o + wd], b_in[o:o + wd]

    out = {}
    out["w_ab"] = jnp.concatenate([w_in[:, 0:2560], w_in[:, _O_GA:_O_GC]], axis=1).astype(BF)
    out["b_ab"] = jnp.concatenate([b_in[0:2560], b_in[_O_GA:_O_GC]])[None, :]
    wkr, bkr = cols(_O_KROPE, MLA_ROPE)
    half = MLA_ROPE // 2
    padk = lambda a: jnp.pad(a, [(0, 0)] * (a.ndim - 1) + [(0, LANE - MLA_ROPE)])
    out["w_c"] = jnp.concatenate([w_in[:, _O_QLAT:_O_KROPE], padk(wkr), padk(_rot_cols(wkr, half))], axis=1).astype(BF)
    out["b_c"] = jnp.concatenate([b_in[_O_QLAT:_O_KROPE], padk(bkr), padk(_rot_cols(bkr, half))])[None, :]
    wuq = p["mla_wuq"][l].reshape(MLA_Q_RANK, MLA_HEADS, MLA_NOPE + MLA_ROPE)
    rope_rot = _rot_cols(wuq[..., MLA_NOPE:], half)
    wqa = jnp.pad(wuq, [(0, 0), (0, 0), (0, LANE - MLA_NOPE - MLA_ROPE)])
    wqb = jnp.pad(rope_rot, [(0, 0), (0, 0), (MLA_NOPE, LANE - MLA_NOPE - MLA_ROPE)])
    out["wqa"] = wqa.reshape(MLA_Q_RANK, MLA_HEADS * LANE).astype(BF)
    out["wqb"] = wqb.reshape(MLA_Q_RANK, MLA_HEADS * LANE).astype(BF)
    wukv = p["mla_wukv"][l].reshape(MLA_KV_RANK, MLA_HEADS, MLA_NOPE + MLA_V)
    out["wk_c"] = jnp.pad(wukv[..., :MLA_NOPE], [(0, 0), (0, 0), (0, LANE - MLA_NOPE)]).reshape(MLA_KV_RANK, -1).astype(BF)
    out["wv_c"] = jnp.pad(wukv[..., MLA_NOPE:], [(0, 0), (0, 0), (0, LANE - MLA_V)]).reshape(MLA_KV_RANK, -1).astype(BF)
    wq, bq = cols(_O_NQ, NSA_HEADS * NSA_DIM)
    hd = NSA_DIM // 2
    pieces_w, pieces_b = [], []

    def add(w, b, slots, roped):
        pieces_w.append(_pad_slots(w, slots, NSA_DIM))
        pieces_b.append(_pad_slots(b, slots, NSA_DIM))
        if roped:
            pieces_w.append(_pad_slots(_rot_slots(w, slots, NSA_DIM, hd), slots, NSA_DIM))
            pieces_b.append(_pad_slots(_rot_slots(b, slots, NSA_DIM, hd), slots, NSA_DIM))

    add(wq, bq, NSA_HEADS, True)
    wkc, bkc = cols(_O_NKC, 128)
    wvc, bvc = cols(_O_NVC, 128)
    pieces_w += [wkc, wvc]
    pieces_b += [bkc, bvc]
    add(*cols(_O_NKS, 128), NSA_GROUPS, True)
    add(*cols(_O_NVS, 128), NSA_GROUPS, False)
    add(*cols(_O_NKW, 128), NSA_GROUPS, True)
    add(*cols(_O_NVW, 128), NSA_GROUPS, False)
    wg, bg = cols(_O_NGATE, NSA_HEADS * 3)
    pieces_w.append(_pad_slots(wg, NSA_GROUPS, NSA_HPG * 3))
    pieces_b.append(_pad_slots(bg, NSA_GROUPS, NSA_HPG * 3))
    out["w_d"] = jnp.concatenate(pieces_w, axis=1).astype(BF)
    out["b_d"] = jnp.concatenate(pieces_b)[None, :]

    def cmp_weights(w):
        eye = jnp.eye(NSA_GROUPS, dtype=F32)
        wp = jnp.pad(w, [(0, 0), (0, 0), (0, LANE - NSA_DIM)])
        full = jnp.einsum("lde,gh->lgdhe", wp, eye).reshape(CMP_BLOCK, NSA_GROUPS * NSA_DIM, NSA_GROUPS * LANE)
        return full.reshape(2, CMP_STRIDE * NSA_GROUPS * NSA_DIM, NSA_GROUPS * LANE).astype(BF)

    wck = p["nsa_wcmp_k"][l]
    out["wcmp_k"] = cmp_weights(wck)
    out["wcmp_kr"] = cmp_weights(_rot_cols(wck, hd))
    out["wcmp_v"] = cmp_weights(p["nsa_wcmp_v"][l])

    def pe_rows(pe):
        t = jnp.broadcast_to(pe[:, None, :], (CMP_BLOCK, NSA_GROUPS, NSA_DIM))
        return t.reshape(2, CMP_STRIDE * NSA_GROUPS * NSA_DIM)

    out["pe"] = jnp.concatenate([pe_rows(p["nsa_pe_k"][l]), pe_rows(p["nsa_pe_v"][l])], axis=0)
    out["w_g"] = w_in[:, _O_GC:].astype(BF)
    out["b_g"] = b_in[_O_GC:][None, :]
    out["wout_c"] = _pad_rows(p["mla_wout"][l], MLA_HEADS, MLA_V).astype(BF)
    out["wout_d"] = _pad_rows(p["nsa_wout"][l], NSA_HEADS, NSA_DIM).astype(BF)
    return out


def _tables(seq):
    pos = jnp.arange(seq, dtype=F32)
    c16, s16 = _rope_tab(pos, MLA_ROPE)
    one = jnp.ones((seq, MLA_NOPE), F32)
    zero = jnp.zeros((seq, MLA_NOPE), F32)
    tail = LANE - MLA_NOPE - MLA_ROPE
    cq = jnp.concatenate([one, c16, c16, jnp.ones((seq, tail), F32)], axis=1)
    sq = jnp.concatenate([zero, s16, s16, jnp.zeros((seq, tail), F32)], axis=1)
    ck = jnp.pad(jnp.concatenate([c16, c16], axis=1), [(0, 0), (0, LANE - MLA_ROPE)])
    sk = jnp.pad(jnp.concatenate([s16, s16], axis=1), [(0, 0), (0, LANE - MLA_ROPE)])
    c32, s32 = _rope_tab(pos, NSA_DIM)
    cn = jnp.pad(jnp.concatenate([c32, c32], axis=1), [(0, 0), (0, LANE - NSA_DIM)])
    sn = jnp.pad(jnp.concatenate([s32, s32], axis=1), [(0, 0), (0, LANE - NSA_DIM)])
    n16 = seq // CMP_STRIDE
    cend = (jnp.arange(n16) * CMP_STRIDE + CMP_BLOCK - 1).astype(F32)
    cc32, cs32 = _rope_tab(cend, NSA_DIM)
    ccg = jnp.pad(jnp.concatenate([cc32, cc32], axis=1), [(0, 0), (0, LANE - NSA_DIM)])
    csg = jnp.pad(jnp.concatenate([cs32, cs32], axis=1), [(0, 0), (0, LANE - NSA_DIM)])
    cc = jnp.concatenate([ccg] * NSA_GROUPS, axis=1)
    cs = jnp.concatenate([csg] * NSA_GROUPS, axis=1)
    n_cmp = (seq - CMP_BLOCK) // CMP_STRIDE + 1
    n_slc = seq // SLC_BLOCK
    cstart = jnp.arange(n16) * CMP_STRIDE
    sstart = jnp.arange(n_slc) * SLC_BLOCK
    ovl = (jnp.minimum(cstart[None, :] + CMP_BLOCK, sstart[:, None] + SLC_BLOCK)
           - jnp.maximum(cstart[None, :], sstart[:, None]))
    ovl = jnp.clip(ovl, 0).astype(F32) / CMP_BLOCK
    ovl = jnp.where(jnp.arange(n16)[None, :] < n_cmp, ovl, 0.0).astype(BF)
    pk = jnp.zeros((LANE, MLA_HEADS, LANE), F32)
    pk = pk.at[jnp.arange(MLA_ROPE), :, MLA_NOPE + jnp.arange(MLA_ROPE)].set(1.0)
    pk = pk.reshape(LANE, MLA_HEADS * LANE).astype(BF)
    return dict(cq=cq, sq=sq, ck=ck, sk=sk, cn=cn, sn=sn, cc=cc, cs=cs, ovl=ovl, pk=pk)


def kernel(x, mem, ffn1_w1, ffn1_w3, ffn1_w2, ln1_g, ln1_b, w_in, b_in, gmlp_ln_g, gmlp_ln_b, gmlp_ws, gmlp_bs, gmlp_wout, conv_w, conv_wout, mla_qnorm_g, mla_kvnorm_g, mla_wuq, mla_wukv, mla_wout, nsa_pe_k, nsa_pe_v, nsa_wcmp_k, nsa_wcmp_v, nsa_wout, w_o, ln2_g, ln2_b, xattn_wq, xattn_wk, xattn_wv, xattn_wo, ln3_g, ln3_b, ffn2_w1, ffn2_w3, ffn2_w2, ln4_g, ln4_b):
    bsz, seq, d = x.shape
    mlen = mem.shape[1]
    n = bsz * seq
    assert d == D_MODEL and seq % 512 == 0 and seq >= WINDOW + 256
    p = dict(w_in=w_in, b_in=b_in, mla_wuq=mla_wuq, mla_wukv=mla_wukv, mla_wout=mla_wout,
             nsa_pe_k=nsa_pe_k, nsa_pe_v=nsa_pe_v, nsa_wcmp_k=nsa_wcmp_k, nsa_wcmp_v=nsa_wcmp_v, nsa_wout=nsa_wout)
    tb = _tables(seq)
    tm = 512
    tm_ffn = 1024
    tf = D_FF // 11
    n16 = seq // CMP_STRIDE
    top_k = min(SLC_TOPK, seq // SLC_BLOCK)
    row = lambda a: a[None, :]

    stack_rows = lambda w: _to_bf16(w.reshape(w.shape[0] * w.shape[1], w.shape[2]))
    ffn1 = [stack_rows(w) for w in (ffn1_w1, ffn1_w3, ffn1_w2)]
    ffn2 = [stack_rows(w) for w in (ffn2_w1, ffn2_w3, ffn2_w2)]
    h = x.reshape(n, d)
    mem2 = mem.reshape(bsz * mlen, d)
    for l in range(DEPTH):
        lp = _layer_params(l, p)
        h = _ffn_ln(h, *ffn1, row(ln1_g[l]), row(ln1_b[l]), layer=l, tm=tm_ffn, tf=tf)
        ab = _mix_ab(h, lp["w_ab"], lp["b_ab"], row(gmlp_ln_g[l]), row(gmlp_ln_b[l]), gmlp_ws[l], gmlp_bs[l].T,
                     gmlp_wout[l].astype(BF), conv_w[l], conv_wout[l].astype(BF), tm=tm, seq=seq)
        qc, kc_, vc_ = _mla_proj(h, lp["w_c"], lp["b_c"], row(mla_qnorm_g[l]), row(mla_kvnorm_g[l]),
                                 lp["wqa"], lp["wqb"], lp["wk_c"], lp["wv_c"], tb["pk"],
                                 tb["cq"], tb["sq"], tb["ck"], tb["sk"], tm=tm, seq=seq)
        wide = MLA_HEADS * LANE
        oc = _flash_causal(qc.reshape(bsz, seq, wide), kc_.reshape(bsz, seq, wide), vc_.reshape(bsz, seq, wide), tq=1024, tk=1024, hp=2)
        qn, nkc, nvc, nks, nvs, nkw, nvw, gates = _nsa_proj(h, lp["w_d"], lp["b_d"], tb["cn"], tb["sn"], tm=tm, seq=seq)
        kcmp, vcmp = _nsa_compress(nkc.reshape(bsz, n16, CMP_STRIDE * 128), nvc.reshape(bsz, n16, CMP_STRIDE * 128),
                                   lp["pe"], lp["wcmp_k"], lp["wcmp_kr"], lp["wcmp_v"], tb["cc"], tb["cs"])
        gw = NSA_GROUPS * LANE
        od = _nsa_attention(qn.reshape(bsz, seq, NSA_HEADS * LANE), kcmp, vcmp,
                            nks.reshape(bsz, seq, gw), nvs.reshape(bsz, seq, gw),
                            nkw.reshape(bsz, seq, gw), nvw.reshape(bsz, seq, gw),
                            gates.reshape(bsz, seq, gw), tb["ovl"], tq=512, tk=512, top_k=top_k)
        h = _merge_ln(h, ab, oc.reshape(n, wide), od.reshape(n, NSA_HEADS * LANE), lp["w_g"], lp["b_g"],
                      lp["wout_c"], lp["wout_d"], w_o[l].astype(BF), row(ln2_g[l]), row(ln2_b[l]), tm=tm)
        kv = _linear(mem2, jnp.concatenate([xattn_wk[l], xattn_wv[l]], axis=1).astype(BF), tm=min(256, bsz * mlen), dtype=BF)
        h = _xattn_ln(h, kv.reshape(bsz, mlen, 2 * XATTN_HEADS * XATTN_DIM), xattn_wq[l].astype(BF),
                      xattn_wo[l].astype(BF), row(ln3_g[l]), row(ln3_b[l]), tm=tm, seq=seq)
        h = _ffn_ln(h, *ffn2, row(ln4_g[l]), row(ln4_b[l]), layer=l, tm=tm_ffn, tf=tf)
    return h.reshape(bsz, seq, d)
```

```python
import functools

import jax
import jax.numpy as jnp
from jax import lax
from jax.experimental import pallas as pl
from jax.experimental.pallas import tpu as pltpu

BF = jnp.bfloat16
F32 = jnp.float32

D_MODEL = 1024
D_FF = 2816
LN_EPS = 1e-5
RMS_EPS = 1e-6
ROPE_THETA = 10000.0
DEPTH = 2
ALPHA = (2 * DEPTH) ** 0.25
NEG = -1e30
LOG2_E = 1.4426950408889634
DENOM_LANE = 64
MASK_BIG = 2.0 ** 100

GMLP_CHUNK = 128
GMLP_GROUPS = 4
GMLP_WIDTH = 512
CONV_WIDTH = 512
CONV_K = 3
MLA_HEADS = 8
MLA_Q_RANK = 256
MLA_KV_RANK = 128
MLA_NOPE = 64
MLA_ROPE = 32
MLA_V = 64
NSA_HEADS = 8
NSA_GROUPS = 2
NSA_HPG = 4
NSA_DIM = 64
CMP_BLOCK = 32
CMP_STRIDE = 16
SLC_BLOCK = 64
SLC_SHIFT = SLC_BLOCK.bit_length() - 1
SLC_TOPK = 8
WINDOW = 512
XATTN_HEADS = 4
XATTN_DIM = 128

LANE = 128
CONV_HALO = 8
VMEM_LIMIT = 56 * 1024 * 1024

_O_U, _O_V, _O_CB, _O_CC, _O_CH = 0, 512, 1024, 1536, 2048
_O_QLAT, _O_KVLAT, _O_KROPE = 2560, 2816, 2944
_O_NQ, _O_NKC, _O_NVC, _O_NKS, _O_NVS, _O_NKW, _O_NVW, _O_NGATE = 2976, 3488, 3616, 3744, 3872, 4000, 4128, 4256
_O_GA, _O_GB, _O_GC, _O_GD = 4280, 5304, 6328, 7352


def _dot(a, b):
    return jnp.dot(a, b, preferred_element_type=F32)


def _dot_t(a, b):
    return lax.dot_general(a, b, (((1,), (1,)), ((), ())), preferred_element_type=F32)


def _ln(y, g, b):
    mu = jnp.mean(y, -1, keepdims=True)
    d = y - mu
    var = jnp.mean(d * d, -1, keepdims=True)
    return d * lax.rsqrt(var + LN_EPS) * g + b


def _rms(x, g):
    return x * lax.rsqrt(jnp.mean(x * x, -1, keepdims=True) + RMS_EPS) * g


def _resident(shape):
    n = len(shape)
    return pl.BlockSpec(shape, lambda *_: (0,) * n, pipeline_mode=pl.Buffered(1))


def _params(sem):
    return pltpu.CompilerParams(dimension_semantics=sem, vmem_limit_bytes=VMEM_LIMIT)


def _cast_kernel(x_ref, o_ref):
    o_ref[...] = x_ref[...].astype(o_ref.dtype)


def _to_bf16(w, *, rows=256):
    r, c = w.shape
    rows = min(rows, r)
    assert r % rows == 0
    return pl.pallas_call(
        _cast_kernel,
        grid=(r // rows,),
        in_specs=[pl.BlockSpec((rows, c), lambda i: (i, 0))],
        out_specs=pl.BlockSpec((rows, c), lambda i: (i, 0)),
        out_shape=jax.ShapeDtypeStruct((r, c), BF),
        compiler_params=_params(("parallel",)),
        name="to_bf16",
    )(w)


def _ffn_ln_kernel(x_ref, w1_ref, w3_ref, w2_ref, g_ref, b_ref, o_ref, *, tf):
    x = x_ref[...]
    xb = x.astype(BF)
    acc = None
    for c in range(w1_ref.shape[1] // tf):
        cols = slice(c * tf, (c + 1) * tf)
        h1 = _dot(xb, w1_ref[:, cols])
        h3 = _dot(xb, w3_ref[:, cols])
        hh = (h1 * jax.nn.sigmoid(h1)) * h3
        part = _dot(hh.astype(BF), w2_ref[cols, :])
        acc = part if acc is None else acc + part
    o_ref[...] = _ln(ALPHA * x + 0.5 * acc, g_ref[...], b_ref[...])


def _layer_block(shape, layer):
    return pl.BlockSpec(shape, lambda *_: (layer, 0), pipeline_mode=pl.Buffered(1))


def _ffn_ln(x, w1, w3, w2, g, b, *, layer, tm, tf):
    n, d = x.shape
    f = w1.shape[1]
    return pl.pallas_call(
        functools.partial(_ffn_ln_kernel, tf=tf),
        grid=(n // tm,),
        in_specs=[pl.BlockSpec((tm, d), lambda i: (i, 0)), _layer_block((d, f), layer), _layer_block((d, f), layer),
                  _layer_block((f, d), layer), _resident(g.shape), _resident(b.shape)],
        out_specs=pl.BlockSpec((tm, d), lambda i: (i, 0)),
        out_shape=jax.ShapeDtypeStruct((n, d), F32),
        compiler_params=_params(("parallel",)),
        name="ffn_ln",
    )(x, w1, w3, w2, g, b)


def _ab_kernel(h_ref, w_ref, b_ref, lng_ref, lnb_ref, ws_ref, bst_ref, wga_ref, cw_ref, wcb_ref,
               o_ref, prev_ref, *, tiles_per_seq):
    i = pl.program_id(0)
    tm = h_ref.shape[0]
    hb = h_ref[...].astype(BF)

    def proj(c0, width):
        return _dot(hb, w_ref[:, c0:c0 + width]) + b_ref[:, c0:c0 + width]

    u = proj(0, GMLP_WIDTH)
    v = _ln(proj(512, GMLP_WIDTH), lng_ref[...], lnb_ref[...]).astype(BF)
    row = lax.broadcasted_iota(jnp.int32, (GMLP_CHUNK, GMLP_CHUNK), 0)
    col = lax.broadcasted_iota(jnp.int32, (GMLP_CHUNK, GMLP_CHUNK), 1)
    gd = GMLP_WIDTH // GMLP_GROUPS
    wgs = [jnp.where(row >= col, ws_ref[g], 0.0).astype(BF) for g in range(GMLP_GROUPS)]
    chunks = []
    for c in range(tm // GMLP_CHUNK):
        r0 = c * GMLP_CHUNK
        chunks.append(jnp.concatenate(
            [_dot(wgs[g], v[r0:r0 + GMLP_CHUNK, g * gd:(g + 1) * gd]) + bst_ref[:, g:g + 1]
             for g in range(GMLP_GROUPS)], axis=1))
    s = jnp.concatenate(chunks, axis=0)
    ya = _dot((u * s).astype(BF), wga_ref[...])

    cb = proj(1024, CONV_WIDTH)
    z = proj(1536, CONV_WIDTH) * proj(2048, CONV_WIDTH)

    @pl.when(i % tiles_per_seq == 0)
    def _():
        prev_ref[...] = jnp.zeros_like(prev_ref)

    zext = jnp.concatenate([prev_ref[...], z], axis=0)
    z1 = pltpu.roll(zext, 1, 0)[CONV_HALO:]
    z2 = pltpu.roll(zext, 2, 0)[CONV_HALO:]
    y = cw_ref[0:1, :] * z2 + cw_ref[1:2, :] * z1 + cw_ref[2:3, :] * z
    prev_ref[...] = z[tm - CONV_HALO:, :]
    yb = _dot((cb * y).astype(BF), wcb_ref[...])

    ga = proj(2560, D_MODEL)
    gb = proj(3584, D_MODEL)
    o_ref[...] = jax.nn.sigmoid(ga) * ya + jax.nn.sigmoid(gb) * yb


def _mix_ab(h, w, b, lng, lnb, ws, bst, wga, cw, wcb, *, tm, seq):
    n, d = h.shape
    kern = functools.partial(_ab_kernel, tiles_per_seq=seq // tm)
    return pl.pallas_call(
        kern,
        grid=(n // tm,),
        in_specs=[pl.BlockSpec((tm, d), lambda i: (i, 0))] + [_resident(a.shape) for a in (w, b, lng, lnb, ws, bst, wga, cw, wcb)],
        out_specs=pl.BlockSpec((tm, d), lambda i: (i, 0)),
        out_shape=jax.ShapeDtypeStruct((n, d), F32),
        scratch_shapes=[pltpu.VMEM((CONV_HALO, CONV_WIDTH), F32)],
        compiler_params=_params(("arbitrary",)),
        name="mix_ab",
    )(h, w, b, lng, lnb, ws, bst, wga, cw, wcb)


def _mla_proj_kernel(h_ref, w_ref, b_ref, qg_ref, kvg_ref, wqa_ref, wqb_ref, wk_ref, wv_ref, pk_ref,
                     cq_ref, sq_ref, ck_ref, sk_ref, q_ref, k_ref, v_ref):
    hb = h_ref[...].astype(BF)
    z = _dot(hb, w_ref[...]) + b_ref[...]
    qn = _rms(z[:, 0:256], qg_ref[...]).astype(BF)
    kvn = _rms(z[:, 256:384], kvg_ref[...]).astype(BF)
    cq = jnp.concatenate([cq_ref[...]] * MLA_HEADS, axis=1)
    sq = jnp.concatenate([sq_ref[...]] * MLA_HEADS, axis=1)
    scale = (MLA_NOPE + MLA_ROPE) ** -0.5 * LOG2_E
    q = (_dot(qn, wqa_ref[...]) * cq + _dot(qn, wqb_ref[...]) * sq) * scale
    q_ref[...] = q.astype(BF)
    kpe = (z[:, 384:512] * ck_ref[...] + z[:, 512:640] * sk_ref[...]).astype(BF)
    k_ref[...] = (_dot(kvn, wk_ref[...]) + _dot(kpe, pk_ref[...])).astype(BF)
    v_ref[...] = (_dot(kvn, wv_ref[...]) + _denom_ones(v_ref.shape[1])).astype(BF)


def _mla_proj(h, w, b, qg, kvg, wqa, wqb, wk, wv, pk, cq, sq, ck, sk, *, tm, seq):
    n, d = h.shape
    tps = seq // tm
    tab = pl.BlockSpec((tm, LANE), lambda i: (i % tps, 0))
    wide = MLA_HEADS * LANE
    out = jax.ShapeDtypeStruct((n, wide), BF)
    return pl.pallas_call(
        _mla_proj_kernel,
        grid=(n // tm,),
        in_specs=[pl.BlockSpec((tm, d), lambda i: (i, 0))]
        + [_resident(a.shape) for a in (w, b, qg, kvg, wqa, wqb, wk, wv, pk)] + [tab] * 4,
        out_specs=[pl.BlockSpec((tm, wide), lambda i: (i, 0))] * 3,
        out_shape=[out, out, out],
        compiler_params=_params(("parallel",)),
        name="mla_proj",
    )(h, w, b, qg, kvg, wqa, wqb, wk, wv, pk, cq, sq, ck, sk)


def _online_softmax_step(s, v, carry):
    m, acc = carry
    m_new = jnp.maximum(m, jnp.max(s, -1, keepdims=True))
    p = jnp.exp2(s - m_new)
    acc = jnp.exp2(m - m_new) * acc + _dot(p.astype(BF), v)
    return m_new, acc


def _softmax_init(rows, width):
    return (jnp.full((rows, 1), NEG, F32), jnp.zeros((rows, width), F32))


def _normalize(acc):
    return acc * (1.0 / acc[:, DENOM_LANE:DENOM_LANE + 1])


def _denom_ones(width):
    lane = lax.broadcasted_iota(jnp.int32, (1, width), 1)
    return jnp.where(lane % LANE == DENOM_LANE, 1.0, 0.0)


def _flash_kernel(q_ref, k_ref, v_ref, o_ref, *, tq, hp):
    qi = pl.program_id(2)
    q0 = qi * tq
    qs = [q_ref[0, :, h * LANE:(h + 1) * LANE] for h in range(hp)]

    def tile(j, carries, width, diagonal):
        k0 = pl.multiple_of(j * width, width)
        out = []
        for h in range(hp):
            s = _dot_t(qs[h], k_ref[0, pl.ds(k0, width), h * LANE:(h + 1) * LANE])
            if diagonal:
                r = lax.broadcasted_iota(jnp.int32, (tq, width), 0)
                c = lax.broadcasted_iota(jnp.int32, (tq, width), 1)
                s = jnp.where(c <= r, s, NEG)
            out.append(_online_softmax_step(s, v_ref[0, pl.ds(k0, width), h * LANE:(h + 1) * LANE], carries[h]))
        return tuple(out)

    init = tuple(_softmax_init(tq, LANE) for _ in range(hp))
    carries = lax.fori_loop(0, qi, lambda j, c: tile(j, c, tq, False), init)

    half = tq // 2
    carries = tile(2 * qi, carries, half, True)
    r = lax.broadcasted_iota(jnp.int32, (half, half), 0)
    c = lax.broadcasted_iota(jnp.int32, (half, half), 1)
    k1 = pl.multiple_of(q0 + half, half)
    out = []
    for h in range(hp):
        m, acc = carries[h]
        s = _dot_t(qs[h][half:], k_ref[0, pl.ds(k1, half), h * LANE:(h + 1) * LANE])
        m2, acc2 = _online_softmax_step(jnp.where(c <= r, s, NEG), v_ref[0, pl.ds(k1, half), h * LANE:(h + 1) * LANE],
                                        (m[half:], acc[half:]))
        out.append(jnp.concatenate([acc[:half], acc2], axis=0))
    o_ref[0] = jnp.concatenate([_normalize(acc) for acc in out], axis=1).astype(o_ref.dtype)


def _flash_causal(q, k, v, *, tq, hp):
    bsz, seq, wide = q.shape
    heads = wide // LANE
    assert seq % tq == 0
    kern = functools.partial(_flash_kernel, tq=tq, hp=hp)
    return pl.pallas_call(
        kern,
        grid=(bsz, heads // hp, seq // tq),
        in_specs=[
            pl.BlockSpec((1, tq, hp * LANE), lambda b, h, i: (b, i, h)),
            pl.BlockSpec((1, seq, hp * LANE), lambda b, h, i: (b, 0, h)),
            pl.BlockSpec((1, seq, hp * LANE), lambda b, h, i: (b, 0, h)),
        ],
        out_specs=pl.BlockSpec((1, tq, hp * LANE), lambda b, h, i: (b, i, h)),
        out_shape=jax.ShapeDtypeStruct((bsz, seq, wide), BF),
        compiler_params=_params(("parallel", "parallel", "arbitrary")),
        name="mla_flash",
    )(q, k, v)


def _nsa_proj_kernel(h_ref, w_ref, b_ref, c_ref, s_ref, q_ref, kc_ref, vc_ref, ks_ref, vs_ref, kw_ref, vw_ref, g_ref,
                     *, tiles_per_seq):
    tm = h_ref.shape[0]
    hb = h_ref[...].astype(BF)
    pos = (pl.program_id(0) % tiles_per_seq) * tm + lax.broadcasted_iota(jnp.int32, (tm, LANE), 0)
    lane = lax.broadcasted_iota(jnp.int32, (tm, LANE), 1)
    tag = jnp.where(lane == NSA_DIM + lax.shift_right_logical(pos, SLC_SHIFT), MASK_BIG, 0.0)
    tag2 = jnp.concatenate([tag] * NSA_GROUPS, axis=1)

    def proj(c0, width):
        return _dot(hb, w_ref[:, c0:c0 + width]) + b_ref[:, c0:c0 + width]

    c = c_ref[...]
    s = s_ref[...]
    c8 = jnp.concatenate([c] * NSA_HEADS, axis=1)
    s8 = jnp.concatenate([s] * NSA_HEADS, axis=1)
    c2 = jnp.concatenate([c] * NSA_GROUPS, axis=1)
    s2 = jnp.concatenate([s] * NSA_GROUPS, axis=1)
    q = (proj(0, 1024) * c8 + proj(1024, 1024) * s8) * (NSA_DIM ** -0.5 * LOG2_E)
    q_ref[...] = q.astype(BF)
    kc_ref[...] = proj(2048, 128)
    vc_ref[...] = proj(2176, 128)
    ks_ref[...] = (proj(2304, 256) * c2 + proj(2560, 256) * s2 + tag2).astype(BF)
    ones = _denom_ones(NSA_GROUPS * LANE)
    vs_ref[...] = (proj(2816, 256) + ones).astype(BF)
    kw_ref[...] = (proj(3072, 256) * c2 + proj(3328, 256) * s2).astype(BF)
    vw_ref[...] = (proj(3584, 256) + ones).astype(BF)
    g_ref[...] = jax.nn.sigmoid(proj(3840, 256))


def _nsa_proj(h, w, b, cn, sn, *, tm, seq):
    n, d = h.shape
    tps = seq // tm
    tab = pl.BlockSpec((tm, LANE), lambda i: (i % tps, 0))

    def out(width, dt):
        return pl.BlockSpec((tm, width), lambda i: (i, 0)), jax.ShapeDtypeStruct((n, width), dt)

    outs = [out(1024, BF), out(128, F32), out(128, F32), out(256, BF), out(256, BF), out(256, BF), out(256, BF), out(256, F32)]
    assert seq // SLC_BLOCK <= LANE - NSA_DIM
    return pl.pallas_call(
        functools.partial(_nsa_proj_kernel, tiles_per_seq=tps),
        grid=(n // tm,),
        in_specs=[pl.BlockSpec((tm, d), lambda i: (i, 0)), _resident(w.shape), _resident(b.shape), tab, tab],
        out_specs=[o[0] for o in outs],
        out_shape=[o[1] for o in outs],
        compiler_params=_params(("parallel",)),
        name="nsa_proj",
    )(h, w, b, cn, sn)


def _nsa_cmp_kernel(kc_ref, vc_ref, pe_ref, wk_ref, wkr_ref, wv_ref, c_ref, s_ref, kcmp_ref, vcmp_ref):
    n16 = kcmp_ref.shape[1]
    gl = NSA_GROUPS * NSA_DIM
    kc = kcr = vc = None
    for l in range(CMP_STRIDE):
        rows = slice(l * gl, (l + 1) * gl)
        xk = kc_ref[0, pl.ds(l, n16, stride=CMP_STRIDE), :]
        xv = vc_ref[0, pl.ds(l, n16, stride=CMP_STRIDE), :]
        terms = [((xk + pe_ref[0:1, rows]).astype(BF), (xv + pe_ref[2:3, rows]).astype(BF), 0),
                 ((pltpu.roll(xk, n16 - 1, 0) + pe_ref[1:2, rows]).astype(BF),
                  (pltpu.roll(xv, n16 - 1, 0) + pe_ref[3:4, rows]).astype(BF), 1)]
        for ak, av, part in terms:
            pk, pkr, pv = _dot(ak, wk_ref[part, rows, :]), _dot(ak, wkr_ref[part, rows, :]), _dot(av, wv_ref[part, rows, :])
            kc, kcr, vc = (pk, pkr, pv) if kc is None else (kc + pk, kcr + pkr, vc + pv)
    kcmp_ref[0] = (kc * c_ref[...] + kcr * s_ref[...]).astype(BF)
    vcmp_ref[0] = vc.astype(BF)


def _nsa_compress(kc, vc, pe, wk, wkr, wv, cc, sc):
    bsz, seq, wide = kc.shape
    n16 = seq // CMP_STRIDE
    blk = pl.BlockSpec((1, seq, wide), lambda b: (b, 0, 0))
    oblk = pl.BlockSpec((1, n16, NSA_GROUPS * LANE), lambda b: (b, 0, 0))
    osh = jax.ShapeDtypeStruct((bsz, n16, NSA_GROUPS * LANE), BF)
    return pl.pallas_call(
        _nsa_cmp_kernel,
        grid=(bsz,),
        in_specs=[blk, blk] + [_resident(a.shape) for a in (pe, wk, wkr, wv, cc, sc)],
        out_specs=[oblk, oblk],
        out_shape=[osh, osh],
        compiler_params=_params(("parallel",)),
        name="nsa_compress",
    )(kc, vc, pe, wk, wkr, wv, cc, sc)


def _nsa_attn_kernel(q_ref, kcmp_ref, vcmp_ref, ks_ref, vs_ref, kw_ref, vw_ref, g_ref, ov_ref, o_ref, *, top_k):
    qi = pl.program_id(1)
    T = q_ref.shape[1]
    R = NSA_HPG * T
    G = NSA_GROUPS
    q0 = qi * T
    qpos = lax.broadcasted_iota(jnp.int32, (T, 1), 0) + q0
    ncp = kcmp_ref.shape[1]
    nb = ov_ref.shape[0]
    ov_t = ov_ref[...]

    def add_per_query(x, b):
        w = x.shape[1]
        return (x.reshape(NSA_HPG, T, w) + b[None]).reshape(R, w)

    def lanes(g):
        return slice(g * LANE, (g + 1) * LANE)

    q4 = [jnp.concatenate([q_ref[0, :, (g * NSA_HPG + h) * LANE:(g * NSA_HPG + h + 1) * LANE]
                           for h in range(NSA_HPG)], axis=0) for g in range(G)]

    cmp_end = lax.broadcasted_iota(jnp.int32, (1, ncp), 1) * CMP_STRIDE + (CMP_BLOCK - 1)
    cbias = jnp.where(cmp_end <= qpos, 0.0, NEG)
    any_valid = jnp.where(qpos >= CMP_BLOCK - 1, 1.0, 0.0)
    jr = lax.broadcasted_iota(jnp.int32, (nb, 1), 0)
    jrf = jr.astype(F32)
    jq = lax.shift_right_logical(lax.broadcasted_iota(jnp.int32, (1, T), 1) + q0, SLC_SHIFT)
    forced = (jr == 0) | (jr == jq) | (jr == jq - 1)
    eye_t = jnp.where(lax.broadcasted_iota(jnp.int32, (T, T), 0) == lax.broadcasted_iota(jnp.int32, (T, T), 1),
                      1.0, 0.0).astype(BF)
    o_cmp, q4s = [], []
    for g in range(G):
        sm = _dot_t(q4[g], kcmp_ref[0, :, lanes(g)]).reshape(NSA_HPG, T, ncp) + cbias[None]
        e = jnp.exp2(sm - jnp.max(sm, -1, keepdims=True))
        p = e * (any_valid[None] / jnp.sum(e, -1, keepdims=True))
        o_cmp.append(_dot(p.reshape(R, ncp).astype(BF), vcmp_ref[0, :, lanes(g)]))
        psum = p[0] + p[1] + p[2] + p[3]
        hi = psum.astype(BF)
        r1 = psum - hi.astype(F32)
        mid = r1.astype(BF)
        lo = (r1 - mid.astype(F32)).astype(BF)
        imp = _dot_t(ov_t, hi) + _dot_t(ov_t, mid) + _dot_t(ov_t, lo)
        imp = jnp.where(forced, 1e9, imp)
        imp = jnp.where(jr <= jq, imp, -1.0)
        work = imp
        sel = jnp.zeros_like(imp)
        for _ in range(top_k):
            mx = jnp.max(work, 0, keepdims=True)
            idx = jnp.min(jnp.where(work == mx, jrf, float(nb)), 0, keepdims=True)
            pick = jrf == idx
            sel = jnp.where(pick, 1.0, sel)
            work = jnp.where(pick, -2.0, work)
        unsel_t = jnp.where(imp >= 0.0, sel, 0.0) - 1.0
        pad_t = [jnp.zeros((NSA_DIM, T), F32), unsel_t]
        if LANE - NSA_DIM - nb:
            pad_t.append(jnp.zeros((LANE - NSA_DIM - nb, T), F32))
        unsel = _dot_t(eye_t, jnp.concatenate(pad_t, axis=0).astype(BF)).astype(BF)
        q4s.append(add_per_query(q4[g], unsel))

    def slc_tile(j, carries, diagonal):
        k0 = pl.multiple_of(j * T, T)
        out = []
        for g in range(G):
            sc = _dot_t(q4s[g], ks_ref[0, pl.ds(k0, T), lanes(g)])
            if diagonal:
                sc = add_per_query(sc, jnp.where((lax.broadcasted_iota(jnp.int32, (1, T), 1) + k0) <= qpos, 0.0, NEG))
            out.append(_online_softmax_step(sc, vs_ref[0, pl.ds(k0, T), lanes(g)], carries[g]))
        return tuple(out)

    carries = lax.fori_loop(0, qi, lambda j, c: slc_tile(j, c, False),
                            tuple(_softmax_init(R, LANE) for _ in range(G)))
    carries = slc_tile(qi, carries, True)
    slc_acc = [acc for (_, acc) in carries]

    wk = WINDOW + T
    w0 = pl.multiple_of(jnp.maximum(q0 - WINDOW, 0), T)
    dist = qpos - (lax.broadcasted_iota(jnp.int32, (1, wk), 1) + w0)
    wbias = jnp.where((dist >= 0) & (dist < WINDOW), 0.0, NEG)
    o_win = []
    for g in range(G):
        sc = add_per_query(_dot_t(q4[g], kw_ref[0, pl.ds(w0, wk), lanes(g)]), wbias)
        e = jnp.exp2(sc - jnp.max(sc, -1, keepdims=True))
        o_win.append(_normalize(_dot(e.astype(BF), vw_ref[0, pl.ds(w0, wk), lanes(g)])))

    gw = NSA_HPG * LANE
    e_row = lax.broadcasted_iota(jnp.int32, (LANE, 3 * gw), 0)
    e_col = lax.broadcasted_iota(jnp.int32, (LANE, 3 * gw), 1)
    branch = jnp.where(e_col >= 2 * gw, 2, jnp.where(e_col >= gw, 1, 0))
    head = lax.shift_right_logical(e_col - branch * gw, LANE.bit_length() - 1)
    expand = jnp.where(e_row == 3 * head + branch, 1.0, 0.0).astype(BF)

    def heads_on_lanes(x):
        return jnp.concatenate([x[h * T:(h + 1) * T] for h in range(NSA_HPG)], axis=1)

    outs = []
    for g in range(G):
        gs = g_ref[0, :, lanes(g)]
        hi = gs.astype(BF)
        lo = (gs - hi.astype(F32)).astype(BF)
        gx = _dot(hi, expand) + _dot(lo, expand)
        outs.append(gx[:, 0:gw] * heads_on_lanes(o_cmp[g])
                    + gx[:, gw:2 * gw] * heads_on_lanes(_normalize(slc_acc[g]))
                    + gx[:, 2 * gw:3 * gw] * heads_on_lanes(o_win[g]))
    o_ref[0] = jnp.concatenate(outs, axis=1).astype(o_ref.dtype)


def _nsa_attention(q, kcmp, vcmp, ks, vs, kw, vw, gates, ov, *, tq, top_k):
    bsz, seq, wide = q.shape
    n16 = kcmp.shape[1]
    gw = NSA_GROUPS * LANE
    assert seq % tq == 0 and seq >= WINDOW + tq
    kern = functools.partial(_nsa_attn_kernel, top_k=top_k)
    cblk = pl.BlockSpec((1, n16, gw), lambda b, i: (b, 0, 0))
    sblk = pl.BlockSpec((1, seq, gw), lambda b, i: (b, 0, 0))
    return pl.pallas_call(
        kern,
        grid=(bsz, seq // tq),
        in_specs=[pl.BlockSpec((1, tq, wide), lambda b, i: (b, i, 0)), cblk, cblk, sblk, sblk, sblk, sblk,
                  pl.BlockSpec((1, tq, gw), lambda b, i: (b, i, 0)), _resident(ov.shape)],
        out_specs=pl.BlockSpec((1, tq, wide), lambda b, i: (b, i, 0)),
        out_shape=jax.ShapeDtypeStruct((bsz, seq, wide), BF),
        compiler_params=_params(("parallel", "arbitrary")),
        name="nsa_attn",
    )(q, kcmp, vcmp, ks, vs, kw, vw, gates, ov)


def _merge_kernel(x_ref, ab_ref, oc_ref, od_ref, wg_ref, bg_ref, wc_ref, wd_ref, wo_ref, g_ref, b_ref, o_ref):
    x = x_ref[...]
    gates = _dot(x.astype(BF), wg_ref[...]) + bg_ref[...]
    yc = _dot(oc_ref[...], wc_ref[...])
    yd = _dot(od_ref[...], wd_ref[...])
    merged = ab_ref[...] + jax.nn.sigmoid(gates[:, :D_MODEL]) * yc + jax.nn.sigmoid(gates[:, D_MODEL:]) * yd
    mix = _dot(merged.astype(BF), wo_ref[...])
    o_ref[...] = _ln(ALPHA * x + mix, g_ref[...], b_ref[...])


def _merge_ln(x, ab, oc, od, wg, bg, wc, wd, wo, g, b, *, tm):
    n, d = x.shape
    row = pl.BlockSpec((tm, d), lambda i: (i, 0))
    return pl.pallas_call(
        _merge_kernel,
        grid=(n // tm,),
        in_specs=[row, row, row, row] + [_resident(a.shape) for a in (wg, bg, wc, wd, wo, g, b)],
        out_specs=row,
        out_shape=jax.ShapeDtypeStruct((n, d), F32),
        compiler_params=_params(("parallel",)),
        name="merge_ln",
    )(x, ab, oc, od, wg, bg, wc, wd, wo, g, b)


def _linear_kernel(x_ref, w_ref, o_ref):
    o_ref[...] = _dot(x_ref[...].astype(BF), w_ref[...]).astype(o_ref.dtype)


def _linear(x, w, *, tm, dtype):
    n, d = x.shape
    return pl.pallas_call(
        _linear_kernel,
        grid=(n // tm,),
        in_specs=[pl.BlockSpec((tm, d), lambda i: (i, 0)), _resident(w.shape)],
        out_specs=pl.BlockSpec((tm, w.shape[1]), lambda i: (i, 0)),
        out_shape=jax.ShapeDtypeStruct((n, w.shape[1]), dtype),
        compiler_params=_params(("parallel",)),
        name="mem_kv",
    )(x, w)


def _xattn_kernel(x_ref, k_ref, v_ref, wq_ref, wo_ref, g_ref, b_ref, o_ref):
    x = x_ref[...]
    q = _dot(x.astype(BF), wq_ref[...]).astype(BF)
    k = k_ref[0]
    v = v_ref[0]
    heads = []
    for h in range(XATTN_HEADS):
        sl = slice(h * XATTN_DIM, (h + 1) * XATTN_DIM)
        s = _dot_t(q[:, sl], k[:, sl]) * (XATTN_DIM ** -0.5)
        e = jnp.exp(s - jnp.max(s, -1, keepdims=True))
        p = e / jnp.sum(e, -1, keepdims=True)
        heads.append(_dot(p.astype(BF), v[:, sl]))
    o = jnp.concatenate(heads, axis=1).astype(BF)
    o_ref[...] = _ln(ALPHA * x + _dot(o, wo_ref[...]), g_ref[...], b_ref[...])


def _xattn_ln(x, kv, wq, wo, g, b, *, tm, seq):
    n, d = x.shape
    tps = seq // tm
    mlen = kv.shape[1]
    hd = XATTN_HEADS * XATTN_DIM
    return pl.pallas_call(
        _xattn_kernel,
        grid=(n // tm,),
        in_specs=[pl.BlockSpec((tm, d), lambda i: (i, 0)),
                  pl.BlockSpec((1, mlen, hd), lambda i: (i // tps, 0, 0)),
                  pl.BlockSpec((1, mlen, hd), lambda i: (i // tps, 0, 1))]
        + [_resident(a.shape) for a in (wq, wo, g, b)],
        out_specs=pl.BlockSpec((tm, d), lambda i: (i, 0)),
        out_shape=jax.ShapeDtypeStruct((n, d), F32),
        compiler_params=_params(("parallel",)),
        name="xattn_ln",
    )(x, kv, kv, wq, wo, g, b)


def _rope_tab(pos, dim):
    inv = ROPE_THETA ** (-(jnp.arange(0, dim, 2, dtype=F32) / dim))
    ang = pos[:, None] * inv[None, :]
    return jnp.cos(ang), jnp.sin(ang)


def _rot_cols(w, half):
    return jnp.concatenate([-w[..., half:2 * half], w[..., :half]], axis=-1)


def _pad_slots(w, n_slots, width):
    lead = w.shape[:-1]
    w = w.reshape(lead + (n_slots, width))
    w = jnp.pad(w, [(0, 0)] * len(lead) + [(0, 0), (0, LANE - width)])
    return w.reshape(lead + (n_slots * LANE,))


def _rot_slots(w, n_slots, width, half):
    lead = w.shape[:-1]
    w = w.reshape(lead + (n_slots, width))
    return _rot_cols(w, half).reshape(lead + (n_slots * width,))


def _pad_rows(w, n_slots, width):
    d = w.shape[-1]
    w = w.reshape(n_slots, width, d)
    w = jnp.pad(w, [(0, 0), (0, LANE - width), (0, 0)])
    return w.reshape(n_slots * LANE, d)


def _layer_params(l, p):
    w_in, b_in = p["w_in"][l], p["b_in"][l]

    def cols(o, wd):
        return w_in[:, o:o + wd], b_in[o:o + wd]

    out = {}
    out["w_ab"] = jnp.concatenate([w_in[:, 0:2560], w_in[:, _O_GA:_O_GC]], axis=1).astype(BF)
    out["b_ab"] = jnp.concatenate([b_in[0:2560], b_in[_O_GA:_O_GC]])[None, :]
    wkr, bkr = cols(_O_KROPE, MLA_ROPE)
    half = MLA_ROPE // 2
    padk = lambda a: jnp.pad(a, [(0, 0)] * (a.ndim - 1) + [(0, LANE - MLA_ROPE)])
    out["w_c"] = jnp.concatenate([w_in[:, _O_QLAT:_O_KROPE], padk(wkr), padk(_rot_cols(wkr, half))], axis=1).astype(BF)
    out["b_c"] = jnp.concatenate([b_in[_O_QLAT:_O_KROPE], padk(bkr), padk(_rot_cols(bkr, half))])[None, :]
    wuq = p["mla_wuq"][l].reshape(MLA_Q_RANK, MLA_HEADS, MLA_NOPE + MLA_ROPE)
    rope_rot = _rot_cols(wuq[..., MLA_NOPE:], half)
    wqa = jnp.pad(wuq, [(0, 0), (0, 0), (0, LANE - MLA_NOPE - MLA_ROPE)])
    wqb = jnp.pad(rope_rot, [(0, 0), (0, 0), (MLA_NOPE, LANE - MLA_NOPE - MLA_ROPE)])
    out["wqa"] = wqa.reshape(MLA_Q_RANK, MLA_HEADS * LANE).astype(BF)
    out["wqb"] = wqb.reshape(MLA_Q_RANK, MLA_HEADS * LANE).astype(BF)
    wukv = p["mla_wukv"][l].reshape(MLA_KV_RANK, MLA_HEADS, MLA_NOPE + MLA_V)
    out["wk_c"] = jnp.pad(wukv[..., :MLA_NOPE], [(0, 0), (0, 0), (0, LANE - MLA_NOPE)]).reshape(MLA_KV_RANK, -1).astype(BF)
    out["wv_c"] = jnp.pad(wukv[..., MLA_NOPE:], [(0, 0), (0, 0), (0, LANE - MLA_V)]).reshape(MLA_KV_RANK, -1).astype(BF)
    wq, bq = cols(_O_NQ, NSA_HEADS * NSA_DIM)
    hd = NSA_DIM // 2
    pieces_w, pieces_b = [], []

    def add(w, b, slots, roped):
        pieces_w.append(_pad_slots(w, slots, NSA_DIM))
        pieces_b.append(_pad_slots(b, slots, NSA_DIM))
        if roped:
            pieces_w.append(_pad_slots(_rot_slots(w, slots, NSA_DIM, hd), slots, NSA_DIM))
            pieces_b.append(_pad_slots(_rot_slots(b, slots, NSA_DIM, hd), slots, NSA_DIM))

    add(wq, bq, NSA_HEADS, True)
    wkc, bkc = cols(_O_NKC, 128)
    wvc, bvc = cols(_O_NVC, 128)
    pieces_w += [wkc, wvc]
    pieces_b += [bkc, bvc]
    add(*cols(_O_NKS, 128), NSA_GROUPS, True)
    add(*cols(_O_NVS, 128), NSA_GROUPS, False)
    add(*cols(_O_NKW, 128), NSA_GROUPS, True)
    add(*cols(_O_NVW, 128), NSA_GROUPS, False)
    wg, bg = cols(_O_NGATE, NSA_HEADS * 3)
    pieces_w.append(_pad_slots(wg, NSA_GROUPS, NSA_HPG * 3))
    pieces_b.append(_pad_slots(bg, NSA_GROUPS, NSA_HPG * 3))
    out["w_d"] = jnp.concatenate(pieces_w, axis=1).astype(BF)
    out["b_d"] = jnp.concatenate(pieces_b)[None, :]

    def cmp_weights(w):
        eye = jnp.eye(NSA_GROUPS, dtype=F32)
        wp = jnp.pad(w, [(0, 0), (0, 0), (0, LANE - NSA_DIM)])
        full = jnp.einsum("lde,gh->lgdhe", wp, eye).reshape(CMP_BLOCK, NSA_GROUPS * NSA_DIM, NSA_GROUPS * LANE)
        return full.reshape(2, CMP_STRIDE * NSA_GROUPS * NSA_DIM, NSA_GROUPS * LANE).astype(BF)

    wck = p["nsa_wcmp_k"][l]
    out["wcmp_k"] = cmp_weights(wck)
    out["wcmp_kr"] = cmp_weights(_rot_cols(wck, hd))
    out["wcmp_v"] = cmp_weights(p["nsa_wcmp_v"][l])

    def pe_rows(pe):
        t = jnp.broadcast_to(pe[:, None, :], (CMP_BLOCK, NSA_GROUPS, NSA_DIM))
        return t.reshape(2, CMP_STRIDE * NSA_GROUPS * NSA_DIM)

    out["pe"] = jnp.concatenate([pe_rows(p["nsa_pe_k"][l]), pe_rows(p["nsa_pe_v"][l])], axis=0)
    out["w_g"] = w_in[:, _O_GC:].astype(BF)
    out["b_g"] = b_in[_O_GC:][None, :]
    out["wout_c"] = _pad_rows(p["mla_wout"][l], MLA_HEADS, MLA_V).astype(BF)
    out["wout_d"] = _pad_rows(p["nsa_wout"][l], NSA_HEADS, NSA_DIM).astype(BF)
    return out


def _tables(seq):
    pos = jnp.arange(seq, dtype=F32)
    c16, s16 = _rope_tab(pos, MLA_ROPE)
    one = jnp.ones((seq, MLA_NOPE), F32)
    zero = jnp.zeros((seq, MLA_NOPE), F32)
    tail = LANE - MLA_NOPE - MLA_ROPE
    cq = jnp.concatenate([one, c16, c16, jnp.ones((seq, tail), F32)], axis=1)
    sq = jnp.concatenate([zero, s16, s16, jnp.zeros((seq, tail), F32)], axis=1)
    ck = jnp.pad(jnp.concatenate([c16, c16], axis=1), [(0, 0), (0, LANE - MLA_ROPE)])
    sk = jnp.pad(jnp.concatenate([s16, s16], axis=1), [(0, 0), (0, LANE - MLA_ROPE)])
    c32, s32 = _rope_tab(pos, NSA_DIM)
    cn = jnp.pad(jnp.concatenate([c32, c32], axis=1), [(0, 0), (0, LANE - NSA_DIM)])
    sn = jnp.pad(jnp.concatenate([s32, s32], axis=1), [(0, 0), (0, LANE - NSA_DIM)])
    n16 = seq // CMP_STRIDE
    cend = (jnp.arange(n16) * CMP_STRIDE + CMP_BLOCK - 1).astype(F32)
    cc32, cs32 = _rope_tab(cend, NSA_DIM)
    ccg = jnp.pad(jnp.concatenate([cc32, cc32], axis=1), [(0, 0), (0, LANE - NSA_DIM)])
    csg = jnp.pad(jnp.concatenate([cs32, cs32], axis=1), [(0, 0), (0, LANE - NSA_DIM)])
    cc = jnp.concatenate([ccg] * NSA_GROUPS, axis=1)
    cs = jnp.concatenate([csg] * NSA_GROUPS, axis=1)
    n_cmp = (seq - CMP_BLOCK) // CMP_STRIDE + 1
    n_slc = seq // SLC_BLOCK
    cstart = jnp.arange(n16) * CMP_STRIDE
    sstart = jnp.arange(n_slc) * SLC_BLOCK
    ovl = (jnp.minimum(cstart[None, :] + CMP_BLOCK, sstart[:, None] + SLC_BLOCK)
           - jnp.maximum(cstart[None, :], sstart[:, None]))
    ovl = jnp.clip(ovl, 0).astype(F32) / CMP_BLOCK
    ovl = jnp.where(jnp.arange(n16)[None, :] < n_cmp, ovl, 0.0).astype(BF)
    pk = jnp.zeros((LANE, MLA_HEADS, LANE), F32)
    pk = pk.at[jnp.arange(MLA_ROPE), :, MLA_NOPE + jnp.arange(MLA_ROPE)].set(1.0)
    pk = pk.reshape(LANE, MLA_HEADS * LANE).astype(BF)
    return dict(cq=cq, sq=sq, ck=ck, sk=sk, cn=cn, sn=sn, cc=cc, cs=cs, ovl=ovl, pk=pk)


def kernel(x, mem, ffn1_w1, ffn1_w3, ffn1_w2, ln1_g, ln1_b, w_in, b_in, gmlp_ln_g, gmlp_ln_b, gmlp_ws, gmlp_bs, gmlp_wout, conv_w, conv_wout, mla_qnorm_g, mla_kvnorm_g, mla_wuq, mla_wukv, mla_wout, nsa_pe_k, nsa_pe_v, nsa_wcmp_k, nsa_wcmp_v, nsa_wout, w_o, ln2_g, ln2_b, xattn_wq, xattn_wk, xattn_wv, xattn_wo, ln3_g, ln3_b, ffn2_w1, ffn2_w3, ffn2_w2, ln4_g, ln4_b):
    bsz, seq, d = x.shape
    mlen = mem.shape[1]
    n = bsz * seq
    assert d == D_MODEL and seq % 1024 == 0
    p = dict(w_in=w_in, b_in=b_in, mla_wuq=mla_wuq, mla_wukv=mla_wukv, mla_wout=mla_wout,
             nsa_pe_k=nsa_pe_k, nsa_pe_v=nsa_pe_v, nsa_wcmp_k=nsa_wcmp_k, nsa_wcmp_v=nsa_wcmp_v, nsa_wout=nsa_wout)
    tb = _tables(seq)
    tm = 512
    tm_ffn = 1024
    tf = D_FF // 11
    n16 = seq // CMP_STRIDE
    top_k = min(SLC_TOPK, seq // SLC_BLOCK)
    row = lambda a: a[None, :]

    stack_rows = lambda w: _to_bf16(w.reshape(w.shape[0] * w.shape[1], w.shape[2]))
    ffn1 = [stack_rows(w) for w in (ffn1_w1, ffn1_w3, ffn1_w2)]
    ffn2 = [stack_rows(w) for w in (ffn2_w1, ffn2_w3, ffn2_w2)]
    h = x.reshape(n, d)
    mem2 = mem.reshape(bsz * mlen, d)
    for l in range(DEPTH):
        lp = _layer_params(l, p)
        h = _ffn_ln(h, *ffn1, row(ln1_g[l]), row(ln1_b[l]), layer=l, tm=tm_ffn, tf=tf)
        ab = _mix_ab(h, lp["w_ab"], lp["b_ab"], row(gmlp_ln_g[l]), row(gmlp_ln_b[l]), gmlp_ws[l], gmlp_bs[l].T,
                     gmlp_wout[l].astype(BF), conv_w[l], conv_wout[l].astype(BF), tm=tm, seq=seq)
        qc, kc_, vc_ = _mla_proj(h, lp["w_c"], lp["b_c"], row(mla_qnorm_g[l]), row(mla_kvnorm_g[l]),
                                 lp["wqa"], lp["wqb"], lp["wk_c"], lp["wv_c"], tb["pk"],
                                 tb["cq"], tb["sq"], tb["ck"], tb["sk"], tm=tm, seq=seq)
        wide = MLA_HEADS * LANE
        oc = _flash_causal(qc.reshape(bsz, seq, wide), kc_.reshape(bsz, seq, wide), vc_.reshape(bsz, seq, wide), tq=1024, hp=2)
        qn, nkc, nvc, nks, nvs, nkw, nvw, gates = _nsa_proj(h, lp["w_d"], lp["b_d"], tb["cn"], tb["sn"], tm=tm, seq=seq)
        kcmp, vcmp = _nsa_compress(nkc.reshape(bsz, seq, NSA_GROUPS * NSA_DIM), nvc.reshape(bsz, seq, NSA_GROUPS * NSA_DIM),
                                   lp["pe"], lp["wcmp_k"], lp["wcmp_kr"], lp["wcmp_v"], tb["cc"], tb["cs"])
        gw = NSA_GROUPS * LANE
        od = _nsa_attention(qn.reshape(bsz, seq, NSA_HEADS * LANE), kcmp, vcmp,
                            nks.reshape(bsz, seq, gw), nvs.reshape(bsz, seq, gw),
                            nkw.reshape(bsz, seq, gw), nvw.reshape(bsz, seq, gw),
                            gates.reshape(bsz, seq, gw), tb["ovl"], tq=512, top_k=top_k)
        h = _merge_ln(h, ab, oc.reshape(n, wide), od.reshape(n, NSA_HEADS * LANE), lp["w_g"], lp["b_g"],
                      lp["wout_c"], lp["wout_d"], w_o[l].astype(BF), row(ln2_g[l]), row(ln2_b[l]), tm=tm)
        kv = _linear(mem2, jnp.concatenate([xattn_wk[l], xattn_wv[l]], axis=1).astype(BF), tm=min(256, bsz * mlen), dtype=BF)
        h = _xattn_ln(h, kv.reshape(bsz, mlen, 2 * XATTN_HEADS * XATTN_DIM), xattn_wq[l].astype(BF),
                      xattn_wo[l].astype(BF), row(ln3_g[l]), row(ln3_b[l]), tm=tm, seq=seq)
        h = _ffn_ln(h, *ffn2, row(ln4_g[l]), row(ln4_b[l]), layer=l, tm=tm_ffn, tf=tf)
    return h.reshape(bsz, seq, d)
```

```python
import functools

import jax
import jax.numpy as jnp
from jax import lax
from jax.experimental import pallas as pl
from jax.experimental.pallas import tpu as pltpu

BF = jnp.bfloat16
F32 = jnp.float32

D_MODEL = 1024
D_FF = 2816
LN_EPS = 1e-5
RMS_EPS = 1e-6
ROPE_THETA = 10000.0
DEPTH = 2
ALPHA = (2 * DEPTH) ** 0.25
NEG = -1e30
LOG2_E = 1.4426950408889634
DENOM_LANE = 64
MASK_BIG = 2.0 ** 100

GMLP_CHUNK = 128
GMLP_GROUPS = 4
GMLP_WIDTH = 512
CONV_WIDTH = 512
CONV_K = 3
MLA_HEADS = 8
MLA_Q_RANK = 256
MLA_KV_RANK = 128
MLA_NOPE = 64
MLA_ROPE = 32
MLA_V = 64
NSA_HEADS = 8
NSA_GROUPS = 2
NSA_HPG = 4
NSA_DIM = 64
CMP_BLOCK = 32
CMP_STRIDE = 16
SLC_BLOCK = 64
SLC_SHIFT = SLC_BLOCK.bit_length() - 1
SLC_TOPK = 8
WINDOW = 512
XATTN_HEADS = 4
XATTN_DIM = 128

LANE = 128
CONV_HALO = 8
VMEM_LIMIT = 56 * 1024 * 1024

_O_U, _O_V, _O_CB, _O_CC, _O_CH = 0, 512, 1024, 1536, 2048
_O_QLAT, _O_KVLAT, _O_KROPE = 2560, 2816, 2944
_O_NQ, _O_NKC, _O_NVC, _O_NKS, _O_NVS, _O_NKW, _O_NVW, _O_NGATE = 2976, 3488, 3616, 3744, 3872, 4000, 4128, 4256
_O_GA, _O_GB, _O_GC, _O_GD = 4280, 5304, 6328, 7352


def _dot(a, b):
    return jnp.dot(a, b, preferred_element_type=F32)


def _dot_t(a, b):
    return lax.dot_general(a, b, (((1,), (1,)), ((), ())), preferred_element_type=F32)


def _ln(y, g, b):
    mu = jnp.mean(y, -1, keepdims=True)
    d = y - mu
    var = jnp.mean(d * d, -1, keepdims=True)
    return d * lax.rsqrt(var + LN_EPS) * g + b


def _rms(x, g):
    return x * lax.rsqrt(jnp.mean(x * x, -1, keepdims=True) + RMS_EPS) * g


def _resident(shape):
    n = len(shape)
    return pl.BlockSpec(shape, lambda *_: (0,) * n, pipeline_mode=pl.Buffered(1))


def _layered(a, layer):
    n = a.ndim - 1
    return pl.BlockSpec((None,) + a.shape[1:], lambda *_: (layer,) + (0,) * n, pipeline_mode=pl.Buffered(1))


def _params(sem):
    return pltpu.CompilerParams(dimension_semantics=sem, vmem_limit_bytes=VMEM_LIMIT)


def _cast_kernel(x_ref, o_ref):
    o_ref[...] = x_ref[...].astype(o_ref.dtype)


def _to_bf16(w, *, rows=256):
    r, c = w.shape
    rows = min(rows, r)
    assert r % rows == 0
    return pl.pallas_call(
        _cast_kernel,
        grid=(r // rows,),
        in_specs=[pl.BlockSpec((rows, c), lambda i: (i, 0))],
        out_specs=pl.BlockSpec((rows, c), lambda i: (i, 0)),
        out_shape=jax.ShapeDtypeStruct((r, c), BF),
        compiler_params=_params(("parallel",)),
        name="to_bf16",
    )(w)


def _ffn_ln_kernel(x_ref, w1_ref, w3_ref, w2_ref, g_ref, b_ref, o_ref, *, tf):
    x = x_ref[...]
    xb = x.astype(BF)
    acc = None
    for c in range(w1_ref.shape[1] // tf):
        cols = slice(c * tf, (c + 1) * tf)
        h1 = _dot(xb, w1_ref[:, cols])
        h3 = _dot(xb, w3_ref[:, cols])
        hh = (h1 * jax.nn.sigmoid(h1)) * h3
        part = _dot(hh.astype(BF), w2_ref[cols, :])
        acc = part if acc is None else acc + part
    o_ref[...] = _ln(ALPHA * x + 0.5 * acc, g_ref[...], b_ref[...])


def _ffn_ln(x, w1, w3, w2, g, b, *, layer, tm, tf):
    n, d = x.shape
    return pl.pallas_call(
        functools.partial(_ffn_ln_kernel, tf=tf),
        grid=(n // tm,),
        in_specs=[pl.BlockSpec((tm, d), lambda i: (i, 0))] + [_layered(a, layer) for a in (w1, w3, w2, g, b)],
        out_specs=pl.BlockSpec((tm, d), lambda i: (i, 0)),
        out_shape=jax.ShapeDtypeStruct((n, d), F32),
        compiler_params=_params(("parallel",)),
        name="ffn_ln",
    )(x, w1, w3, w2, g, b)


def _ab_kernel(h_ref, w_ref, b_ref, lng_ref, lnb_ref, ws_ref, bst_ref, wga_ref, cw_ref, wcb_ref,
               o_ref, prev_ref, *, tiles_per_seq):
    i = pl.program_id(0)
    tm = h_ref.shape[0]
    hb = h_ref[...].astype(BF)

    def proj(c0, width):
        return _dot(hb, w_ref[:, c0:c0 + width]) + b_ref[:, c0:c0 + width]

    u = proj(0, GMLP_WIDTH)
    v = _ln(proj(512, GMLP_WIDTH), lng_ref[...], lnb_ref[...]).astype(BF)
    row = lax.broadcasted_iota(jnp.int32, (GMLP_CHUNK, GMLP_CHUNK), 0)
    col = lax.broadcasted_iota(jnp.int32, (GMLP_CHUNK, GMLP_CHUNK), 1)
    gd = GMLP_WIDTH // GMLP_GROUPS
    wgs = [jnp.where(row >= col, ws_ref[g], 0.0).astype(BF) for g in range(GMLP_GROUPS)]
    chunks = []
    for c in range(tm // GMLP_CHUNK):
        r0 = c * GMLP_CHUNK
        chunks.append(jnp.concatenate(
            [_dot(wgs[g], v[r0:r0 + GMLP_CHUNK, g * gd:(g + 1) * gd]) + bst_ref[:, g:g + 1]
             for g in range(GMLP_GROUPS)], axis=1))
    s = jnp.concatenate(chunks, axis=0)
    ya = _dot((u * s).astype(BF), wga_ref[...])

    cb = proj(1024, CONV_WIDTH)
    z = proj(1536, CONV_WIDTH) * proj(2048, CONV_WIDTH)

    @pl.when(i % tiles_per_seq == 0)
    def _():
        prev_ref[...] = jnp.zeros_like(prev_ref)

    zext = jnp.concatenate([prev_ref[...], z], axis=0)
    z1 = pltpu.roll(zext, 1, 0)[CONV_HALO:]
    z2 = pltpu.roll(zext, 2, 0)[CONV_HALO:]
    y = cw_ref[0:1, :] * z2 + cw_ref[1:2, :] * z1 + cw_ref[2:3, :] * z
    prev_ref[...] = z[tm - CONV_HALO:, :]
    yb = _dot((cb * y).astype(BF), wcb_ref[...])

    ga = proj(2560, D_MODEL)
    gb = proj(3584, D_MODEL)
    o_ref[...] = jax.nn.sigmoid(ga) * ya + jax.nn.sigmoid(gb) * yb


def _mix_ab(h, w, b, lng, lnb, ws, bst, wga, cw, wcb, *, layer, tm, seq):
    n, d = h.shape
    kern = functools.partial(_ab_kernel, tiles_per_seq=seq // tm)
    return pl.pallas_call(
        kern,
        grid=(n // tm,),
        in_specs=[pl.BlockSpec((tm, d), lambda i: (i, 0))]
        + [_layered(a, layer) for a in (w, b, lng, lnb, ws, bst, wga, cw, wcb)],
        out_specs=pl.BlockSpec((tm, d), lambda i: (i, 0)),
        out_shape=jax.ShapeDtypeStruct((n, d), F32),
        scratch_shapes=[pltpu.VMEM((CONV_HALO, CONV_WIDTH), F32)],
        compiler_params=_params(("arbitrary",)),
        name="mix_ab",
    )(h, w, b, lng, lnb, ws, bst, wga, cw, wcb)


def _mla_proj_kernel(h_ref, w_ref, b_ref, qg_ref, kvg_ref, wqa_ref, wqb_ref, wk_ref, wv_ref, pk_ref,
                     cq_ref, sq_ref, ck_ref, sk_ref, q_ref, k_ref, v_ref):
    hb = h_ref[...].astype(BF)
    z = _dot(hb, w_ref[...]) + b_ref[...]
    qn = _rms(z[:, 0:256], qg_ref[...]).astype(BF)
    kvn = _rms(z[:, 256:384], kvg_ref[...]).astype(BF)
    cq = jnp.concatenate([cq_ref[...]] * MLA_HEADS, axis=1)
    sq = jnp.concatenate([sq_ref[...]] * MLA_HEADS, axis=1)
    scale = (MLA_NOPE + MLA_ROPE) ** -0.5 * LOG2_E
    q = (_dot(qn, wqa_ref[...]) * cq + _dot(qn, wqb_ref[...]) * sq) * scale
    q_ref[...] = q.astype(BF)
    kpe = (z[:, 384:512] * ck_ref[...] + z[:, 512:640] * sk_ref[...]).astype(BF)
    k_ref[...] = (_dot(kvn, wk_ref[...]) + _dot(kpe, pk_ref[...])).astype(BF)
    v_ref[...] = (_dot(kvn, wv_ref[...]) + _denom_ones(v_ref.shape[1])).astype(BF)


def _mla_proj(h, w, b, qg, kvg, wqa, wqb, wk, wv, pk, cq, sq, ck, sk, *, layer, tm, seq):
    n, d = h.shape
    tps = seq // tm
    tab = pl.BlockSpec((tm, LANE), lambda i: (i % tps, 0))
    wide = MLA_HEADS * LANE
    out = jax.ShapeDtypeStruct((n, wide), BF)
    return pl.pallas_call(
        _mla_proj_kernel,
        grid=(n // tm,),
        in_specs=[pl.BlockSpec((tm, d), lambda i: (i, 0))]
        + [_layered(a, layer) for a in (w, b, qg, kvg, wqa, wqb, wk, wv)] + [_resident(pk.shape)] + [tab] * 4,
        out_specs=[pl.BlockSpec((tm, wide), lambda i: (i, 0))] * 3,
        out_shape=[out, out, out],
        compiler_params=_params(("parallel",)),
        name="mla_proj",
    )(h, w, b, qg, kvg, wqa, wqb, wk, wv, pk, cq, sq, ck, sk)


def _online_softmax_step(s, v, carry):
    m, acc = carry
    m_new = jnp.maximum(m, jnp.max(s, -1, keepdims=True))
    p = jnp.exp2(s - m_new)
    acc = jnp.exp2(m - m_new) * acc + _dot(p.astype(BF), v)
    return m_new, acc


def _softmax_init(rows, width):
    return (jnp.full((rows, 1), NEG, F32), jnp.zeros((rows, width), F32))


def _normalize(acc):
    return acc * (1.0 / acc[:, DENOM_LANE:DENOM_LANE + 1])


def _denom_ones(width):
    lane = lax.broadcasted_iota(jnp.int32, (1, width), 1)
    return jnp.where(lane % LANE == DENOM_LANE, 1.0, 0.0)


def _flash_kernel(q_ref, k_ref, v_ref, o_ref, *, tq, hp):
    qi = pl.program_id(2)
    q0 = qi * tq
    qs = [q_ref[0, :, h * LANE:(h + 1) * LANE] for h in range(hp)]

    def tile(j, carries, width, diagonal):
        k0 = pl.multiple_of(j * width, width)
        out = []
        for h in range(hp):
            s = _dot_t(qs[h], k_ref[0, pl.ds(k0, width), h * LANE:(h + 1) * LANE])
            if diagonal:
                r = lax.broadcasted_iota(jnp.int32, (tq, width), 0)
                c = lax.broadcasted_iota(jnp.int32, (tq, width), 1)
                s = jnp.where(c <= r, s, NEG)
            out.append(_online_softmax_step(s, v_ref[0, pl.ds(k0, width), h * LANE:(h + 1) * LANE], carries[h]))
        return tuple(out)

    init = tuple(_softmax_init(tq, LANE) for _ in range(hp))
    carries = lax.fori_loop(0, qi, lambda j, c: tile(j, c, tq, False), init)

    half = tq // 2
    carries = tile(2 * qi, carries, half, True)
    r = lax.broadcasted_iota(jnp.int32, (half, half), 0)
    c = lax.broadcasted_iota(jnp.int32, (half, half), 1)
    k1 = pl.multiple_of(q0 + half, half)
    out = []
    for h in range(hp):
        m, acc = carries[h]
        s = _dot_t(qs[h][half:], k_ref[0, pl.ds(k1, half), h * LANE:(h + 1) * LANE])
        m2, acc2 = _online_softmax_step(jnp.where(c <= r, s, NEG), v_ref[0, pl.ds(k1, half), h * LANE:(h + 1) * LANE],
                                        (m[half:], acc[half:]))
        out.append(jnp.concatenate([acc[:half], acc2], axis=0))
    o_ref[0] = jnp.concatenate([_normalize(acc) for acc in out], axis=1).astype(o_ref.dtype)


def _flash_causal(q, k, v, *, tq, hp):
    bsz, seq, wide = q.shape
    heads = wide // LANE
    assert seq % tq == 0
    kern = functools.partial(_flash_kernel, tq=tq, hp=hp)
    return pl.pallas_call(
        kern,
        grid=(bsz, heads // hp, seq // tq),
        in_specs=[
            pl.BlockSpec((1, tq, hp * LANE), lambda b, h, i: (b, i, h)),
            pl.BlockSpec((1, seq, hp * LANE), lambda b, h, i: (b, 0, h)),
            pl.BlockSpec((1, seq, hp * LANE), lambda b, h, i: (b, 0, h)),
        ],
        out_specs=pl.BlockSpec((1, tq, hp * LANE), lambda b, h, i: (b, i, h)),
        out_shape=jax.ShapeDtypeStruct((bsz, seq, wide), BF),
        compiler_params=_params(("parallel", "parallel", "arbitrary")),
        name="mla_flash",
    )(q, k, v)


def _nsa_proj_kernel(h_ref, w_ref, b_ref, c_ref, s_ref, q_ref, kc_ref, vc_ref, ks_ref, vs_ref, kw_ref, vw_ref, g_ref,
                     *, tiles_per_seq):
    tm = h_ref.shape[0]
    hb = h_ref[...].astype(BF)
    pos = (pl.program_id(0) % tiles_per_seq) * tm + lax.broadcasted_iota(jnp.int32, (tm, LANE), 0)
    lane = lax.broadcasted_iota(jnp.int32, (tm, LANE), 1)
    tag = jnp.where(lane == NSA_DIM + lax.shift_right_logical(pos, SLC_SHIFT), MASK_BIG, 0.0)
    tag2 = jnp.concatenate([tag] * NSA_GROUPS, axis=1)

    def proj(c0, width):
        return _dot(hb, w_ref[:, c0:c0 + width]) + b_ref[:, c0:c0 + width]

    c = c_ref[...]
    s = s_ref[...]
    c8 = jnp.concatenate([c] * NSA_HEADS, axis=1)
    s8 = jnp.concatenate([s] * NSA_HEADS, axis=1)
    c2 = jnp.concatenate([c] * NSA_GROUPS, axis=1)
    s2 = jnp.concatenate([s] * NSA_GROUPS, axis=1)
    q = (proj(0, 1024) * c8 + proj(1024, 1024) * s8) * (NSA_DIM ** -0.5 * LOG2_E)
    q_ref[...] = q.astype(BF)
    kc_ref[...] = proj(2048, 128)
    vc_ref[...] = proj(2176, 128)
    ks_ref[...] = (proj(2304, 256) * c2 + proj(2560, 256) * s2 + tag2).astype(BF)
    ones = _denom_ones(NSA_GROUPS * LANE)
    vs_ref[...] = (proj(2816, 256) + ones).astype(BF)
    kw_ref[...] = (proj(3072, 256) * c2 + proj(3328, 256) * s2).astype(BF)
    vw_ref[...] = (proj(3584, 256) + ones).astype(BF)
    g_ref[...] = jax.nn.sigmoid(proj(3840, 256))


def _nsa_proj(h, w, b, cn, sn, *, layer, tm, seq):
    n, d = h.shape
    tps = seq // tm
    tab = pl.BlockSpec((tm, LANE), lambda i: (i % tps, 0))

    def out(width, dt):
        return pl.BlockSpec((tm, width), lambda i: (i, 0)), jax.ShapeDtypeStruct((n, width), dt)

    outs = [out(1024, BF), out(128, F32), out(128, F32), out(256, BF), out(256, BF), out(256, BF), out(256, BF), out(256, F32)]
    assert seq // SLC_BLOCK <= LANE - NSA_DIM
    return pl.pallas_call(
        functools.partial(_nsa_proj_kernel, tiles_per_seq=tps),
        grid=(n // tm,),
        in_specs=[pl.BlockSpec((tm, d), lambda i: (i, 0)), _layered(w, layer), _layered(b, layer), tab, tab],
        out_specs=[o[0] for o in outs],
        out_shape=[o[1] for o in outs],
        compiler_params=_params(("parallel",)),
        name="nsa_proj",
    )(h, w, b, cn, sn)


def _nsa_cmp_kernel(kc_ref, vc_ref, pe_ref, wk_ref, wkr_ref, wv_ref, c_ref, s_ref, kcmp_ref, vcmp_ref):
    n16 = kcmp_ref.shape[1]
    gl = NSA_GROUPS * NSA_DIM
    kc = kcr = vc = None
    for l in range(CMP_STRIDE):
        rows = slice(l * gl, (l + 1) * gl)
        xk = kc_ref[0, pl.ds(l, n16, stride=CMP_STRIDE), :]
        xv = vc_ref[0, pl.ds(l, n16, stride=CMP_STRIDE), :]
        terms = [((xk + pe_ref[0:1, rows]).astype(BF), (xv + pe_ref[2:3, rows]).astype(BF), 0),
                 ((pltpu.roll(xk, n16 - 1, 0) + pe_ref[1:2, rows]).astype(BF),
                  (pltpu.roll(xv, n16 - 1, 0) + pe_ref[3:4, rows]).astype(BF), 1)]
        for ak, av, part in terms:
            pk, pkr, pv = _dot(ak, wk_ref[part, rows, :]), _dot(ak, wkr_ref[part, rows, :]), _dot(av, wv_ref[part, rows, :])
            kc, kcr, vc = (pk, pkr, pv) if kc is None else (kc + pk, kcr + pkr, vc + pv)
    kcmp_ref[0] = (kc * c_ref[...] + kcr * s_ref[...]).astype(BF)
    vcmp_ref[0] = vc.astype(BF)


def _nsa_compress(kc, vc, pe, wk, wkr, wv, cc, sc, *, layer):
    bsz, seq, wide = kc.shape
    n16 = seq // CMP_STRIDE
    blk = pl.BlockSpec((1, seq, wide), lambda b: (b, 0, 0))
    oblk = pl.BlockSpec((1, n16, NSA_GROUPS * LANE), lambda b: (b, 0, 0))
    osh = jax.ShapeDtypeStruct((bsz, n16, NSA_GROUPS * LANE), BF)
    return pl.pallas_call(
        _nsa_cmp_kernel,
        grid=(bsz,),
        in_specs=[blk, blk] + [_layered(a, layer) for a in (pe, wk, wkr, wv)] + [_resident(cc.shape), _resident(sc.shape)],
        out_specs=[oblk, oblk],
        out_shape=[osh, osh],
        compiler_params=_params(("parallel",)),
        name="nsa_compress",
    )(kc, vc, pe, wk, wkr, wv, cc, sc)


def _nsa_attn_kernel(q_ref, kcmp_ref, vcmp_ref, ks_ref, vs_ref, kw_ref, vw_ref, g_ref, ov_ref, o_ref, *, top_k):
    qi = pl.program_id(1)
    T = q_ref.shape[1]
    R = NSA_HPG * T
    G = NSA_GROUPS
    q0 = qi * T
    qpos = lax.broadcasted_iota(jnp.int32, (T, 1), 0) + q0
    ncp = kcmp_ref.shape[1]
    nb = ov_ref.shape[0]
    ov_t = ov_ref[...]

    def add_per_query(x, b):
        w = x.shape[1]
        return (x.reshape(NSA_HPG, T, w) + b[None]).reshape(R, w)

    def lanes(g):
        return slice(g * LANE, (g + 1) * LANE)

    q4 = [jnp.concatenate([q_ref[0, :, (g * NSA_HPG + h) * LANE:(g * NSA_HPG + h + 1) * LANE]
                           for h in range(NSA_HPG)], axis=0) for g in range(G)]

    cmp_end = lax.broadcasted_iota(jnp.int32, (1, ncp), 1) * CMP_STRIDE + (CMP_BLOCK - 1)
    cbias = jnp.where(cmp_end <= qpos, 0.0, NEG)
    any_valid = jnp.where(qpos >= CMP_BLOCK - 1, 1.0, 0.0)
    jr = lax.broadcasted_iota(jnp.int32, (nb, 1), 0)
    jrf = jr.astype(F32)
    jq = lax.shift_right_logical(lax.broadcasted_iota(jnp.int32, (1, T), 1) + q0, SLC_SHIFT)
    forced = (jr == 0) | (jr == jq) | (jr == jq - 1)
    eye_t = jnp.where(lax.broadcasted_iota(jnp.int32, (T, T), 0) == lax.broadcasted_iota(jnp.int32, (T, T), 1),
                      1.0, 0.0).astype(BF)
    o_cmp, q4s = [], []
    for g in range(G):
        sm = _dot_t(q4[g], kcmp_ref[0, :, lanes(g)]).reshape(NSA_HPG, T, ncp) + cbias[None]
        e = jnp.exp2(sm - jnp.max(sm, -1, keepdims=True))
        p = e * (any_valid[None] / jnp.sum(e, -1, keepdims=True))
        o_cmp.append(_dot(p.reshape(R, ncp).astype(BF), vcmp_ref[0, :, lanes(g)]))
        psum = p[0] + p[1] + p[2] + p[3]
        hi = psum.astype(BF)
        r1 = psum - hi.astype(F32)
        mid = r1.astype(BF)
        lo = (r1 - mid.astype(F32)).astype(BF)
        imp = _dot_t(ov_t, hi) + _dot_t(ov_t, mid) + _dot_t(ov_t, lo)
        imp = jnp.where(forced, 1e9, imp)
        imp = jnp.where(jr <= jq, imp, -1.0)
        work = imp
        sel = jnp.zeros_like(imp)
        for _ in range(top_k):
            mx = jnp.max(work, 0, keepdims=True)
            idx = jnp.min(jnp.where(work == mx, jrf, float(nb)), 0, keepdims=True)
            pick = jrf == idx
            sel = jnp.where(pick, 1.0, sel)
            work = jnp.where(pick, -2.0, work)
        unsel_t = jnp.where(imp >= 0.0, sel, 0.0) - 1.0
        pad_t = [jnp.zeros((NSA_DIM, T), F32), unsel_t]
        if LANE - NSA_DIM - nb:
            pad_t.append(jnp.zeros((LANE - NSA_DIM - nb, T), F32))
        unsel = _dot_t(eye_t, jnp.concatenate(pad_t, axis=0).astype(BF)).astype(BF)
        q4s.append(add_per_query(q4[g], unsel))

    def slc_tile(j, carries, diagonal):
        k0 = pl.multiple_of(j * T, T)
        out = []
        for g in range(G):
            sc = _dot_t(q4s[g], ks_ref[0, pl.ds(k0, T), lanes(g)])
            if diagonal:
                sc = add_per_query(sc, jnp.where((lax.broadcasted_iota(jnp.int32, (1, T), 1) + k0) <= qpos, 0.0, NEG))
            out.append(_online_softmax_step(sc, vs_ref[0, pl.ds(k0, T), lanes(g)], carries[g]))
        return tuple(out)

    carries = lax.fori_loop(0, qi, lambda j, c: slc_tile(j, c, False),
                            tuple(_softmax_init(R, LANE) for _ in range(G)))
    carries = slc_tile(qi, carries, True)
    slc_acc = [acc for (_, acc) in carries]

    wk = WINDOW + T
    w0 = pl.multiple_of(jnp.maximum(q0 - WINDOW, 0), T)
    dist = qpos - (lax.broadcasted_iota(jnp.int32, (1, wk), 1) + w0)
    wbias = jnp.where((dist >= 0) & (dist < WINDOW), 0.0, NEG)
    o_win = []
    for g in range(G):
        sc = add_per_query(_dot_t(q4[g], kw_ref[0, pl.ds(w0, wk), lanes(g)]), wbias)
        e = jnp.exp2(sc - jnp.max(sc, -1, keepdims=True))
        o_win.append(_normalize(_dot(e.astype(BF), vw_ref[0, pl.ds(w0, wk), lanes(g)])))

    gw = NSA_HPG * LANE
    e_row = lax.broadcasted_iota(jnp.int32, (LANE, 3 * gw), 0)
    e_col = lax.broadcasted_iota(jnp.int32, (LANE, 3 * gw), 1)
    branch = jnp.where(e_col >= 2 * gw, 2, jnp.where(e_col >= gw, 1, 0))
    head = lax.shift_right_logical(e_col - branch * gw, LANE.bit_length() - 1)
    expand = jnp.where(e_row == 3 * head + branch, 1.0, 0.0).astype(BF)

    def heads_on_lanes(x):
        return jnp.concatenate([x[h * T:(h + 1) * T] for h in range(NSA_HPG)], axis=1)

    outs = []
    for g in range(G):
        gs = g_ref[0, :, lanes(g)]
        hi = gs.astype(BF)
        lo = (gs - hi.astype(F32)).astype(BF)
        gx = _dot(hi, expand) + _dot(lo, expand)
        outs.append(gx[:, 0:gw] * heads_on_lanes(o_cmp[g])
                    + gx[:, gw:2 * gw] * heads_on_lanes(_normalize(slc_acc[g]))
                    + gx[:, 2 * gw:3 * gw] * heads_on_lanes(o_win[g]))
    o_ref[0] = jnp.concatenate(outs, axis=1).astype(o_ref.dtype)


def _nsa_attention(q, kcmp, vcmp, ks, vs, kw, vw, gates, ov, *, tq, top_k):
    bsz, seq, wide = q.shape
    n16 = kcmp.shape[1]
    gw = NSA_GROUPS * LANE
    assert seq % tq == 0 and seq >= WINDOW + tq
    kern = functools.partial(_nsa_attn_kernel, top_k=top_k)
    cblk = pl.BlockSpec((1, n16, gw), lambda b, i: (b, 0, 0))
    sblk = pl.BlockSpec((1, seq, gw), lambda b, i: (b, 0, 0))
    return pl.pallas_call(
        kern,
        grid=(bsz, seq // tq),
        in_specs=[pl.BlockSpec((1, tq, wide), lambda b, i: (b, i, 0)), cblk, cblk, sblk, sblk, sblk, sblk,
                  pl.BlockSpec((1, tq, gw), lambda b, i: (b, i, 0)), _resident(ov.shape)],
        out_specs=pl.BlockSpec((1, tq, wide), lambda b, i: (b, i, 0)),
        out_shape=jax.ShapeDtypeStruct((bsz, seq, wide), BF),
        compiler_params=_params(("parallel", "arbitrary")),
        name="nsa_attn",
    )(q, kcmp, vcmp, ks, vs, kw, vw, gates, ov)


def _merge_kernel(x_ref, ab_ref, oc_ref, od_ref, wg_ref, bg_ref, wc_ref, wd_ref, wo_ref, g_ref, b_ref, o_ref):
    x = x_ref[...]
    gates = _dot(x.astype(BF), wg_ref[...]) + bg_ref[...]
    yc = _dot(oc_ref[...], wc_ref[...])
    yd = _dot(od_ref[...], wd_ref[...])
    merged = ab_ref[...] + jax.nn.sigmoid(gates[:, :D_MODEL]) * yc + jax.nn.sigmoid(gates[:, D_MODEL:]) * yd
    mix = _dot(merged.astype(BF), wo_ref[...])
    o_ref[...] = _ln(ALPHA * x + mix, g_ref[...], b_ref[...])


def _merge_ln(x, ab, oc, od, wg, bg, wc, wd, wo, g, b, *, layer, tm):
    n, d = x.shape
    row = pl.BlockSpec((tm, d), lambda i: (i, 0))
    return pl.pallas_call(
        _merge_kernel,
        grid=(n // tm,),
        in_specs=[row, row, row, row] + [_layered(a, layer) for a in (wg, bg, wc, wd, wo, g, b)],
        out_specs=row,
        out_shape=jax.ShapeDtypeStruct((n, d), F32),
        compiler_params=_params(("parallel",)),
        name="merge_ln",
    )(x, ab, oc, od, wg, bg, wc, wd, wo, g, b)


def _linear_kernel(x_ref, w_ref, o_ref):
    o_ref[...] = _dot(x_ref[...].astype(BF), w_ref[...]).astype(o_ref.dtype)


def _linear(x, w, *, layer, tm, dtype):
    n, d = x.shape
    return pl.pallas_call(
        _linear_kernel,
        grid=(n // tm,),
        in_specs=[pl.BlockSpec((tm, d), lambda i: (i, 0)), _layered(w, layer)],
        out_specs=pl.BlockSpec((tm, w.shape[2]), lambda i: (i, 0)),
        out_shape=jax.ShapeDtypeStruct((n, w.shape[2]), dtype),
        compiler_params=_params(("parallel",)),
        name="mem_kv",
    )(x, w)


def _xattn_kernel(x_ref, k_ref, v_ref, wq_ref, wo_ref, g_ref, b_ref, o_ref):
    x = x_ref[...]
    q = _dot(x.astype(BF), wq_ref[...]).astype(BF)
    k = k_ref[0]
    v = v_ref[0]
    heads = []
    for h in range(XATTN_HEADS):
        sl = slice(h * XATTN_DIM, (h + 1) * XATTN_DIM)
        s = _dot_t(q[:, sl], k[:, sl]) * (XATTN_DIM ** -0.5)
        e = jnp.exp(s - jnp.max(s, -1, keepdims=True))
        p = e / jnp.sum(e, -1, keepdims=True)
        heads.append(_dot(p.astype(BF), v[:, sl]))
    o = jnp.concatenate(heads, axis=1).astype(BF)
    o_ref[...] = _ln(ALPHA * x + _dot(o, wo_ref[...]), g_ref[...], b_ref[...])


def _xattn_ln(x, kv, wq, wo, g, b, *, layer, tm, seq):
    n, d = x.shape
    tps = seq // tm
    mlen = kv.shape[1]
    hd = XATTN_HEADS * XATTN_DIM
    return pl.pallas_call(
        _xattn_kernel,
        grid=(n // tm,),
        in_specs=[pl.BlockSpec((tm, d), lambda i: (i, 0)),
                  pl.BlockSpec((1, mlen, hd), lambda i: (i // tps, 0, 0)),
                  pl.BlockSpec((1, mlen, hd), lambda i: (i // tps, 0, 1))]
        + [_layered(a, layer) for a in (wq, wo, g, b)],
        out_specs=pl.BlockSpec((tm, d), lambda i: (i, 0)),
        out_shape=jax.ShapeDtypeStruct((n, d), F32),
        compiler_params=_params(("parallel",)),
        name="xattn_ln",
    )(x, kv, kv, wq, wo, g, b)


def _rope_tab(pos, dim):
    inv = ROPE_THETA ** (-(jnp.arange(0, dim, 2, dtype=F32) / dim))
    ang = pos[:, None] * inv[None, :]
    return jnp.cos(ang), jnp.sin(ang)


def _rot_cols(w, half):
    return jnp.concatenate([-w[..., half:2 * half], w[..., :half]], axis=-1)


def _pad_slots(w, n_slots, width):
    lead = w.shape[:-1]
    w = w.reshape(lead + (n_slots, width))
    w = jnp.pad(w, [(0, 0)] * len(lead) + [(0, 0), (0, LANE - width)])
    return w.reshape(lead + (n_slots * LANE,))


def _rot_slots(w, n_slots, width, half):
    lead = w.shape[:-1]
    w = w.reshape(lead + (n_slots, width))
    return _rot_cols(w, half).reshape(lead + (n_slots * width,))


def _pad_rows(w, n_slots, width):
    d = w.shape[-1]
    w = w.reshape(n_slots, width, d)
    w = jnp.pad(w, [(0, 0), (0, LANE - width), (0, 0)])
    return w.reshape(n_slots * LANE, d)


def _layer_params(p):
    w_in, b_in = p["w_in"], p["b_in"]

    def cols(o, wd):
        return w_in[:, o:o + wd], b_in[o:o + wd]

    out = {}
    out["w_ab"] = jnp.concatenate([w_in[:, 0:2560], w_in[:, _O_GA:_O_GC]], axis=1).astype(BF)
    out["b_ab"] = jnp.concatenate([b_in[0:2560], b_in[_O_GA:_O_GC]])[None, :]
    wkr, bkr = cols(_O_KROPE, MLA_ROPE)
    half = MLA_ROPE // 2
    padk = lambda a: jnp.pad(a, [(0, 0)] * (a.ndim - 1) + [(0, LANE - MLA_ROPE)])
    out["w_c"] = jnp.concatenate([w_in[:, _O_QLAT:_O_KROPE], padk(wkr), padk(_rot_cols(wkr, half))], axis=1).astype(BF)
    out["b_c"] = jnp.concatenate([b_in[_O_QLAT:_O_KROPE], padk(bkr), padk(_rot_cols(bkr, half))])[None, :]
    wuq = p["mla_wuq"].reshape(MLA_Q_RANK, MLA_HEADS, MLA_NOPE + MLA_ROPE)
    rope_rot = _rot_cols(wuq[..., MLA_NOPE:], half)
    wqa = jnp.pad(wuq, [(0, 0), (0, 0), (0, LANE - MLA_NOPE - MLA_ROPE)])
    wqb = jnp.pad(rope_rot, [(0, 0), (0, 0), (MLA_NOPE, LANE - MLA_NOPE - MLA_ROPE)])
    out["wqa"] = wqa.reshape(MLA_Q_RANK, MLA_HEADS * LANE).astype(BF)
    out["wqb"] = wqb.reshape(MLA_Q_RANK, MLA_HEADS * LANE).astype(BF)
    wukv = p["mla_wukv"].reshape(MLA_KV_RANK, MLA_HEADS, MLA_NOPE + MLA_V)
    out["wk_c"] = jnp.pad(wukv[..., :MLA_NOPE], [(0, 0), (0, 0), (0, LANE - MLA_NOPE)]).reshape(MLA_KV_RANK, -1).astype(BF)
    out["wv_c"] = jnp.pad(wukv[..., MLA_NOPE:], [(0, 0), (0, 0), (0, LANE - MLA_V)]).reshape(MLA_KV_RANK, -1).astype(BF)
    wq, bq = cols(_O_NQ, NSA_HEADS * NSA_DIM)
    hd = NSA_DIM // 2
    pieces_w, pieces_b = [], []

    def add(w, b, slots, roped):
        pieces_w.append(_pad_slots(w, slots, NSA_DIM))
        pieces_b.append(_pad_slots(b, slots, NSA_DIM))
        if roped:
            pieces_w.append(_pad_slots(_rot_slots(w, slots, NSA_DIM, hd), slots, NSA_DIM))
            pieces_b.append(_pad_slots(_rot_slots(b, slots, NSA_DIM, hd), slots, NSA_DIM))

    add(wq, bq, NSA_HEADS, True)
    wkc, bkc = cols(_O_NKC, 128)
    wvc, bvc = cols(_O_NVC, 128)
    pieces_w += [wkc, wvc]
    pieces_b += [bkc, bvc]
    add(*cols(_O_NKS, 128), NSA_GROUPS, True)
    add(*cols(_O_NVS, 128), NSA_GROUPS, False)
    add(*cols(_O_NKW, 128), NSA_GROUPS, True)
    add(*cols(_O_NVW, 128), NSA_GROUPS, False)
    wg, bg = cols(_O_NGATE, NSA_HEADS * 3)
    pieces_w.append(_pad_slots(wg, NSA_GROUPS, NSA_HPG * 3))
    pieces_b.append(_pad_slots(bg, NSA_GROUPS, NSA_HPG * 3))
    out["w_d"] = jnp.concatenate(pieces_w, axis=1).astype(BF)
    out["b_d"] = jnp.concatenate(pieces_b)[None, :]

    def cmp_weights(w):
        eye = jnp.eye(NSA_GROUPS, dtype=F32)
        wp = jnp.pad(w, [(0, 0), (0, 0), (0, LANE - NSA_DIM)])
        full = jnp.einsum("lde,gh->lgdhe", wp, eye).reshape(CMP_BLOCK, NSA_GROUPS * NSA_DIM, NSA_GROUPS * LANE)
        return full.reshape(2, CMP_STRIDE * NSA_GROUPS * NSA_DIM, NSA_GROUPS * LANE).astype(BF)

    wck = p["nsa_wcmp_k"]
    out["wcmp_k"] = cmp_weights(wck)
    out["wcmp_kr"] = cmp_weights(_rot_cols(wck, hd))
    out["wcmp_v"] = cmp_weights(p["nsa_wcmp_v"])

    def pe_rows(pe):
        t = jnp.broadcast_to(pe[:, None, :], (CMP_BLOCK, NSA_GROUPS, NSA_DIM))
        return t.reshape(2, CMP_STRIDE * NSA_GROUPS * NSA_DIM)

    out["pe"] = jnp.concatenate([pe_rows(p["nsa_pe_k"]), pe_rows(p["nsa_pe_v"])], axis=0)
    out["w_g"] = w_in[:, _O_GC:].astype(BF)
    out["b_g"] = b_in[_O_GC:][None, :]
    out["wout_c"] = _pad_rows(p["mla_wout"], MLA_HEADS, MLA_V).astype(BF)
    out["wout_d"] = _pad_rows(p["nsa_wout"], NSA_HEADS, NSA_DIM).astype(BF)
    out["gmlp_bs_t"] = p["gmlp_bs"].T
    for name in ("gmlp_wout", "conv_wout", "w_o", "xattn_wq", "xattn_wo"):
        out[name] = p[name].astype(BF)
    out["xattn_wkv"] = jnp.concatenate([p["xattn_wk"], p["xattn_wv"]], axis=1).astype(BF)
    return out


def _tables(seq):
    pos = jnp.arange(seq, dtype=F32)
    c16, s16 = _rope_tab(pos, MLA_ROPE)
    one = jnp.ones((seq, MLA_NOPE), F32)
    zero = jnp.zeros((seq, MLA_NOPE), F32)
    tail = LANE - MLA_NOPE - MLA_ROPE
    cq = jnp.concatenate([one, c16, c16, jnp.ones((seq, tail), F32)], axis=1)
    sq = jnp.concatenate([zero, s16, s16, jnp.zeros((seq, tail), F32)], axis=1)
    ck = jnp.pad(jnp.concatenate([c16, c16], axis=1), [(0, 0), (0, LANE - MLA_ROPE)])
    sk = jnp.pad(jnp.concatenate([s16, s16], axis=1), [(0, 0), (0, LANE - MLA_ROPE)])
    c32, s32 = _rope_tab(pos, NSA_DIM)
    cn = jnp.pad(jnp.concatenate([c32, c32], axis=1), [(0, 0), (0, LANE - NSA_DIM)])
    sn = jnp.pad(jnp.concatenate([s32, s32], axis=1), [(0, 0), (0, LANE - NSA_DIM)])
    n16 = seq // CMP_STRIDE
    cend = (jnp.arange(n16) * CMP_STRIDE + CMP_BLOCK - 1).astype(F32)
    cc32, cs32 = _rope_tab(cend, NSA_DIM)
    ccg = jnp.pad(jnp.concatenate([cc32, cc32], axis=1), [(0, 0), (0, LANE - NSA_DIM)])
    csg = jnp.pad(jnp.concatenate([cs32, cs32], axis=1), [(0, 0), (0, LANE - NSA_DIM)])
    cc = jnp.concatenate([ccg] * NSA_GROUPS, axis=1)
    cs = jnp.concatenate([csg] * NSA_GROUPS, axis=1)
    n_cmp = (seq - CMP_BLOCK) // CMP_STRIDE + 1
    n_slc = seq // SLC_BLOCK
    cstart = jnp.arange(n16) * CMP_STRIDE
    sstart = jnp.arange(n_slc) * SLC_BLOCK
    ovl = (jnp.minimum(cstart[None, :] + CMP_BLOCK, sstart[:, None] + SLC_BLOCK)
           - jnp.maximum(cstart[None, :], sstart[:, None]))
    ovl = jnp.clip(ovl, 0).astype(F32) / CMP_BLOCK
    ovl = jnp.where(jnp.arange(n16)[None, :] < n_cmp, ovl, 0.0).astype(BF)
    pk = jnp.zeros((LANE, MLA_HEADS, LANE), F32)
    pk = pk.at[jnp.arange(MLA_ROPE), :, MLA_NOPE + jnp.arange(MLA_ROPE)].set(1.0)
    pk = pk.reshape(LANE, MLA_HEADS * LANE).astype(BF)
    return dict(cq=cq, sq=sq, ck=ck, sk=sk, cn=cn, sn=sn, cc=cc, cs=cs, ovl=ovl, pk=pk)


def kernel(x, mem, ffn1_w1, ffn1_w3, ffn1_w2, ln1_g, ln1_b, w_in, b_in, gmlp_ln_g, gmlp_ln_b, gmlp_ws, gmlp_bs, gmlp_wout, conv_w, conv_wout, mla_qnorm_g, mla_kvnorm_g, mla_wuq, mla_wukv, mla_wout, nsa_pe_k, nsa_pe_v, nsa_wcmp_k, nsa_wcmp_v, nsa_wout, w_o, ln2_g, ln2_b, xattn_wq, xattn_wk, xattn_wv, xattn_wo, ln3_g, ln3_b, ffn2_w1, ffn2_w3, ffn2_w2, ln4_g, ln4_b):
    bsz, seq, d = x.shape
    mlen = mem.shape[1]
    n = bsz * seq
    assert d == D_MODEL and seq % 1024 == 0
    lp = jax.vmap(_layer_params)(dict(
        w_in=w_in, b_in=b_in, mla_wuq=mla_wuq, mla_wukv=mla_wukv, mla_wout=mla_wout, nsa_pe_k=nsa_pe_k,
        nsa_pe_v=nsa_pe_v, nsa_wcmp_k=nsa_wcmp_k, nsa_wcmp_v=nsa_wcmp_v, nsa_wout=nsa_wout, gmlp_bs=gmlp_bs,
        gmlp_wout=gmlp_wout, conv_wout=conv_wout, w_o=w_o, xattn_wq=xattn_wq, xattn_wk=xattn_wk,
        xattn_wv=xattn_wv, xattn_wo=xattn_wo))
    tb = _tables(seq)
    tm = 512
    tm_ffn = 1024
    tf = D_FF // 11
    top_k = min(SLC_TOPK, seq // SLC_BLOCK)
    row = lambda a: a[:, None, :]

    def cast_stacked(w):
        return _to_bf16(w.reshape(w.shape[0] * w.shape[1], w.shape[2])).reshape(w.shape)

    ffn1 = [cast_stacked(w) for w in (ffn1_w1, ffn1_w3, ffn1_w2)]
    ffn2 = [cast_stacked(w) for w in (ffn2_w1, ffn2_w3, ffn2_w2)]
    h = x.reshape(n, d)
    mem2 = mem.reshape(bsz * mlen, d)
    wide = MLA_HEADS * LANE
    gw = NSA_GROUPS * LANE
    for l in range(DEPTH):
        h = _ffn_ln(h, *ffn1, row(ln1_g), row(ln1_b), layer=l, tm=tm_ffn, tf=tf)
        ab = _mix_ab(h, lp["w_ab"], lp["b_ab"], row(gmlp_ln_g), row(gmlp_ln_b), gmlp_ws, lp["gmlp_bs_t"],
                     lp["gmlp_wout"], conv_w, lp["conv_wout"], layer=l, tm=tm, seq=seq)
        qc, kc_, vc_ = _mla_proj(h, lp["w_c"], lp["b_c"], row(mla_qnorm_g), row(mla_kvnorm_g),
                                 lp["wqa"], lp["wqb"], lp["wk_c"], lp["wv_c"], tb["pk"],
                                 tb["cq"], tb["sq"], tb["ck"], tb["sk"], layer=l, tm=tm, seq=seq)
        oc = _flash_causal(qc.reshape(bsz, seq, wide), kc_.reshape(bsz, seq, wide), vc_.reshape(bsz, seq, wide), tq=1024, hp=2)
        qn, nkc, nvc, nks, nvs, nkw, nvw, gates = _nsa_proj(h, lp["w_d"], lp["b_d"], tb["cn"], tb["sn"],
                                                            layer=l, tm=tm, seq=seq)
        kcmp, vcmp = _nsa_compress(nkc.reshape(bsz, seq, NSA_GROUPS * NSA_DIM), nvc.reshape(bsz, seq, NSA_GROUPS * NSA_DIM),
                                   lp["pe"], lp["wcmp_k"], lp["wcmp_kr"], lp["wcmp_v"], tb["cc"], tb["cs"], layer=l)
        od = _nsa_attention(qn.reshape(bsz, seq, NSA_HEADS * LANE), kcmp, vcmp,
                            nks.reshape(bsz, seq, gw), nvs.reshape(bsz, seq, gw),
                            nkw.reshape(bsz, seq, gw), nvw.reshape(bsz, seq, gw),
                            gates.reshape(bsz, seq, gw), tb["ovl"], tq=512, top_k=top_k)
        h = _merge_ln(h, ab, oc.reshape(n, wide), od.reshape(n, NSA_HEADS * LANE), lp["w_g"], lp["b_g"],
                      lp["wout_c"], lp["wout_d"], lp["w_o"], row(ln2_g), row(ln2_b), layer=l, tm=tm)
        kv = _linear(mem2, lp["xattn_wkv"], layer=l, tm=min(256, bsz * mlen), dtype=BF)
        h = _xattn_ln(h, kv.reshape(bsz, mlen, 2 * XATTN_HEADS * XATTN_DIM), lp["xattn_wq"], lp["xattn_wo"],
                      row(ln3_g), row(ln3_b), layer=l, tm=tm, seq=seq)
        h = _ffn_ln(h, *ffn2, row(ln4_g), row(ln4_b), layer=l, tm=tm_ffn, tf=tf)
    return h.reshape(bsz, seq, d)
```

```python
import functools

import jax
import jax.numpy as jnp
from jax import lax
from jax.experimental import pallas as pl
from jax.experimental.pallas import tpu as pltpu

BF = jnp.bfloat16
F32 = jnp.float32

D_MODEL = 1024
D_FF = 2816
LN_EPS = 1e-5
RMS_EPS = 1e-6
ROPE_THETA = 10000.0
DEPTH = 2
ALPHA = (2 * DEPTH) ** 0.25
NEG = -1e30
LOG2_E = 1.4426950408889634
DENOM_LANE = 64
MASK_BIG = 2.0 ** 100

GMLP_CHUNK = 128
GMLP_GROUPS = 4
GMLP_WIDTH = 512
CONV_WIDTH = 512
CONV_K = 3
MLA_HEADS = 8
MLA_Q_RANK = 256
MLA_KV_RANK = 128
MLA_NOPE = 64
MLA_ROPE = 32
MLA_V = 64
NSA_HEADS = 8
NSA_GROUPS = 2
NSA_HPG = 4
NSA_DIM = 64
CMP_BLOCK = 32
CMP_STRIDE = 16
SLC_BLOCK = 64
SLC_SHIFT = SLC_BLOCK.bit_length() - 1
SLC_TOPK = 8
WINDOW = 512
XATTN_HEADS = 4
XATTN_DIM = 128

LANE = 128
CONV_HALO = 8
VMEM_LIMIT = 56 * 1024 * 1024

_O_U, _O_V, _O_CB, _O_CC, _O_CH = 0, 512, 1024, 1536, 2048
_O_QLAT, _O_KVLAT, _O_KROPE = 2560, 2816, 2944
_O_NQ, _O_NKC, _O_NVC, _O_NKS, _O_NVS, _O_NKW, _O_NVW, _O_NGATE = 2976, 3488, 3616, 3744, 3872, 4000, 4128, 4256
_O_GA, _O_GB, _O_GC, _O_GD = 4280, 5304, 6328, 7352


def _dot(a, b):
    return jnp.dot(a, b, preferred_element_type=F32)


def _dot_t(a, b):
    return lax.dot_general(a, b, (((1,), (1,)), ((), ())), preferred_element_type=F32)


def _ln(y, g, b):
    mu = jnp.mean(y, -1, keepdims=True)
    d = y - mu
    var = jnp.mean(d * d, -1, keepdims=True)
    return d * lax.rsqrt(var + LN_EPS) * g + b


def _rms(x, g):
    return x * lax.rsqrt(jnp.mean(x * x, -1, keepdims=True) + RMS_EPS) * g


def _resident(shape):
    n = len(shape)
    return pl.BlockSpec(shape, lambda *_: (0,) * n, pipeline_mode=pl.Buffered(1))


def _layered(a, layer):
    n = a.ndim - 1
    return pl.BlockSpec((None,) + a.shape[1:], lambda *_: (layer,) + (0,) * n, pipeline_mode=pl.Buffered(1))


def _params(sem):
    return pltpu.CompilerParams(dimension_semantics=sem, vmem_limit_bytes=VMEM_LIMIT)


def _cast_kernel(x_ref, o_ref):
    o_ref[...] = x_ref[...].astype(o_ref.dtype)


def _to_bf16(w, *, rows=256):
    r, c = w.shape
    rows = min(rows, r)
    assert r % rows == 0
    return pl.pallas_call(
        _cast_kernel,
        grid=(r // rows,),
        in_specs=[pl.BlockSpec((rows, c), lambda i: (i, 0))],
        out_specs=pl.BlockSpec((rows, c), lambda i: (i, 0)),
        out_shape=jax.ShapeDtypeStruct((r, c), BF),
        compiler_params=_params(("parallel",)),
        name="to_bf16",
    )(w)


def _ffn_ln_kernel(x_ref, w1_ref, w3_ref, w2_ref, g_ref, b_ref, o_ref, *, tf):
    x = x_ref[...]
    xb = x.astype(BF)
    acc = None
    for c in range(w1_ref.shape[1] // tf):
        cols = slice(c * tf, (c + 1) * tf)
        h1 = _dot(xb, w1_ref[:, cols])
        h3 = _dot(xb, w3_ref[:, cols])
        hh = (h1 * jax.nn.sigmoid(h1)) * h3
        part = _dot(hh.astype(BF), w2_ref[cols, :])
        acc = part if acc is None else acc + part
    o_ref[...] = _ln(ALPHA * x + 0.5 * acc, g_ref[...], b_ref[...])


def _ffn_ln(x, w1, w3, w2, g, b, *, layer, tm, tf):
    n, d = x.shape
    return pl.pallas_call(
        functools.partial(_ffn_ln_kernel, tf=tf),
        grid=(n // tm,),
        in_specs=[pl.BlockSpec((tm, d), lambda i: (i, 0))] + [_layered(a, layer) for a in (w1, w3, w2, g, b)],
        out_specs=pl.BlockSpec((tm, d), lambda i: (i, 0)),
        out_shape=jax.ShapeDtypeStruct((n, d), F32),
        compiler_params=_params(("parallel",)),
        name="ffn_ln",
    )(x, w1, w3, w2, g, b)


def _ab_kernel(h_ref, w_ref, b_ref, lng_ref, lnb_ref, ws_ref, bst_ref, wga_ref, cw_ref, wcb_ref,
               o_ref, prev_ref, *, tiles_per_seq):
    i = pl.program_id(0)
    tm = h_ref.shape[0]
    hb = h_ref[...].astype(BF)

    def proj(c0, width):
        return _dot(hb, w_ref[:, c0:c0 + width]) + b_ref[:, c0:c0 + width]

    u = proj(0, GMLP_WIDTH)
    v = _ln(proj(512, GMLP_WIDTH), lng_ref[...], lnb_ref[...]).astype(BF)
    row = lax.broadcasted_iota(jnp.int32, (GMLP_CHUNK, GMLP_CHUNK), 0)
    col = lax.broadcasted_iota(jnp.int32, (GMLP_CHUNK, GMLP_CHUNK), 1)
    gd = GMLP_WIDTH // GMLP_GROUPS
    wgs = [jnp.where(row >= col, ws_ref[g], 0.0).astype(BF) for g in range(GMLP_GROUPS)]
    chunks = []
    for c in range(tm // GMLP_CHUNK):
        r0 = c * GMLP_CHUNK
        chunks.append(jnp.concatenate(
            [_dot(wgs[g], v[r0:r0 + GMLP_CHUNK, g * gd:(g + 1) * gd]) + bst_ref[:, g:g + 1]
             for g in range(GMLP_GROUPS)], axis=1))
    s = jnp.concatenate(chunks, axis=0)
    ya = _dot((u * s).astype(BF), wga_ref[...])

    cb = proj(1024, CONV_WIDTH)
    z = proj(1536, CONV_WIDTH) * proj(2048, CONV_WIDTH)

    @pl.when(i % tiles_per_seq == 0)
    def _():
        prev_ref[...] = jnp.zeros_like(prev_ref)

    zext = jnp.concatenate([prev_ref[...], z], axis=0)
    z1 = pltpu.roll(zext, 1, 0)[CONV_HALO:]
    z2 = pltpu.roll(zext, 2, 0)[CONV_HALO:]
    y = cw_ref[0:1, :] * z2 + cw_ref[1:2, :] * z1 + cw_ref[2:3, :] * z
    prev_ref[...] = z[tm - CONV_HALO:, :]
    yb = _dot((cb * y).astype(BF), wcb_ref[...])

    ga = proj(2560, D_MODEL)
    gb = proj(3584, D_MODEL)
    o_ref[...] = jax.nn.sigmoid(ga) * ya + jax.nn.sigmoid(gb) * yb


def _mix_ab(h, w, b, lng, lnb, ws, bst, wga, cw, wcb, *, layer, tm, seq):
    n, d = h.shape
    kern = functools.partial(_ab_kernel, tiles_per_seq=seq // tm)
    return pl.pallas_call(
        kern,
        grid=(n // tm,),
        in_specs=[pl.BlockSpec((tm, d), lambda i: (i, 0))]
        + [_layered(a, layer) for a in (w, b, lng, lnb, ws, bst, wga, cw, wcb)],
        out_specs=pl.BlockSpec((tm, d), lambda i: (i, 0)),
        out_shape=jax.ShapeDtypeStruct((n, d), F32),
        scratch_shapes=[pltpu.VMEM((CONV_HALO, CONV_WIDTH), F32)],
        compiler_params=_params(("arbitrary",)),
        name="mix_ab",
    )(h, w, b, lng, lnb, ws, bst, wga, cw, wcb)


def _mla_proj_kernel(h_ref, w_ref, b_ref, qg_ref, kvg_ref, wq_ref, wk_ref, wv_ref,
                     cq_ref, sq_ref, ck_ref, sk_ref, q_ref, k_ref, v_ref):
    hb = h_ref[...].astype(BF)
    z = _dot(hb, w_ref[...]) + b_ref[...]
    qn = _rms(z[:, 0:256], qg_ref[...]).astype(BF)
    kvn = _rms(z[:, 256:384], kvg_ref[...]).astype(BF)
    half = MLA_ROPE // 2

    def rotate_half(x, start):
        w = x.shape[1]
        first = lax.broadcasted_iota(jnp.int32, (1, w), 1) % LANE < start + half
        return jnp.where(first, -pltpu.roll(x, w - half, 1), pltpu.roll(x, half, 1))

    cq = jnp.concatenate([cq_ref[...]] * MLA_HEADS, axis=1)
    sq = jnp.concatenate([sq_ref[...]] * MLA_HEADS, axis=1)
    scale = (MLA_NOPE + MLA_ROPE) ** -0.5 * LOG2_E
    q = _dot(qn, wq_ref[...])
    q_ref[...] = ((q * cq + rotate_half(q, MLA_NOPE) * sq) * scale).astype(BF)
    kr = z[:, 384:512]
    kpe = pltpu.roll(kr * ck_ref[...] + rotate_half(kr, 0) * sk_ref[...], MLA_NOPE, 1)
    k_ref[...] = (_dot(kvn, wk_ref[...]) + jnp.concatenate([kpe] * MLA_HEADS, axis=1)).astype(BF)
    v_ref[...] = (_dot(kvn, wv_ref[...]) + _denom_ones(v_ref.shape[1])).astype(BF)


def _mla_proj(h, w, b, qg, kvg, wq, wk, wv, cq, sq, ck, sk, *, layer, tm, seq):
    n, d = h.shape
    tps = seq // tm
    tab = pl.BlockSpec((tm, LANE), lambda i: (i % tps, 0))
    wide = MLA_HEADS * LANE
    out = jax.ShapeDtypeStruct((n, wide), BF)
    return pl.pallas_call(
        _mla_proj_kernel,
        grid=(n // tm,),
        in_specs=[pl.BlockSpec((tm, d), lambda i: (i, 0))]
        + [_layered(a, layer) for a in (w, b, qg, kvg, wq, wk, wv)] + [tab] * 4,
        out_specs=[pl.BlockSpec((tm, wide), lambda i: (i, 0))] * 3,
        out_shape=[out, out, out],
        compiler_params=_params(("parallel",)),
        name="mla_proj",
    )(h, w, b, qg, kvg, wq, wk, wv, cq, sq, ck, sk)


def _online_softmax_step(s, v, carry):
    m, acc = carry
    m_new = jnp.maximum(m, jnp.max(s, -1, keepdims=True))
    p = jnp.exp2(s - m_new)
    acc = jnp.exp2(m - m_new) * acc + _dot(p.astype(BF), v)
    return m_new, acc


def _softmax_init(rows, width):
    return (jnp.full((rows, 1), NEG, F32), jnp.zeros((rows, width), F32))


def _normalize(acc):
    return acc * (1.0 / acc[:, DENOM_LANE:DENOM_LANE + 1])


def _denom_ones(width):
    lane = lax.broadcasted_iota(jnp.int32, (1, width), 1)
    return jnp.where(lane % LANE == DENOM_LANE, 1.0, 0.0)


def _flash_kernel(q_ref, k_ref, v_ref, o_ref, *, tq, hp):
    qi = pl.program_id(2)
    q0 = qi * tq
    qs = [q_ref[0, :, h * LANE:(h + 1) * LANE] for h in range(hp)]

    def tile(j, carries, width, diagonal):
        k0 = pl.multiple_of(j * width, width)
        out = []
        for h in range(hp):
            s = _dot_t(qs[h], k_ref[0, pl.ds(k0, width), h * LANE:(h + 1) * LANE])
            if diagonal:
                r = lax.broadcasted_iota(jnp.int32, (tq, width), 0)
                c = lax.broadcasted_iota(jnp.int32, (tq, width), 1)
                s = jnp.where(c <= r, s, NEG)
            out.append(_online_softmax_step(s, v_ref[0, pl.ds(k0, width), h * LANE:(h + 1) * LANE], carries[h]))
        return tuple(out)

    init = tuple(_softmax_init(tq, LANE) for _ in range(hp))
    carries = lax.fori_loop(0, qi, lambda j, c: tile(j, c, tq, False), init)

    half = tq // 2
    carries = tile(2 * qi, carries, half, True)
    r = lax.broadcasted_iota(jnp.int32, (half, half), 0)
    c = lax.broadcasted_iota(jnp.int32, (half, half), 1)
    k1 = pl.multiple_of(q0 + half, half)
    out = []
    for h in range(hp):
        m, acc = carries[h]
        s = _dot_t(qs[h][half:], k_ref[0, pl.ds(k1, half), h * LANE:(h + 1) * LANE])
        m2, acc2 = _online_softmax_step(jnp.where(c <= r, s, NEG), v_ref[0, pl.ds(k1, half), h * LANE:(h + 1) * LANE],
                                        (m[half:], acc[half:]))
        out.append(jnp.concatenate([acc[:half], acc2], axis=0))
    o_ref[0] = jnp.concatenate([_normalize(acc) for acc in out], axis=1).astype(o_ref.dtype)


def _flash_causal(q, k, v, *, tq, hp):
    bsz, seq, wide = q.shape
    heads = wide // LANE
    assert seq % tq == 0
    kern = functools.partial(_flash_kernel, tq=tq, hp=hp)
    return pl.pallas_call(
        kern,
        grid=(bsz, heads // hp, seq // tq),
        in_specs=[
            pl.BlockSpec((1, tq, hp * LANE), lambda b, h, i: (b, i, h)),
            pl.BlockSpec((1, seq, hp * LANE), lambda b, h, i: (b, 0, h)),
            pl.BlockSpec((1, seq, hp * LANE), lambda b, h, i: (b, 0, h)),
        ],
        out_specs=pl.BlockSpec((1, tq, hp * LANE), lambda b, h, i: (b, i, h)),
        out_shape=jax.ShapeDtypeStruct((bsz, seq, wide), BF),
        compiler_params=_params(("parallel", "parallel", "arbitrary")),
        name="mla_flash",
    )(q, k, v)


def _nsa_proj_kernel(h_ref, w_ref, b_ref, c_ref, s_ref, q_ref, kc_ref, vc_ref, ks_ref, vs_ref, kw_ref, vw_ref, g_ref,
                     *, tiles_per_seq):
    tm = h_ref.shape[0]
    hb = h_ref[...].astype(BF)
    pos = (pl.program_id(0) % tiles_per_seq) * tm + lax.broadcasted_iota(jnp.int32, (tm, LANE), 0)
    lane = lax.broadcasted_iota(jnp.int32, (tm, LANE), 1)
    tag = jnp.where(lane == NSA_DIM + lax.shift_right_logical(pos, SLC_SHIFT), MASK_BIG, 0.0)
    tag2 = jnp.concatenate([tag] * NSA_GROUPS, axis=1)

    def proj(c0, width):
        return _dot(hb, w_ref[:, c0:c0 + width]) + b_ref[:, c0:c0 + width]

    c = c_ref[...]
    s = s_ref[...]
    half = NSA_DIM // 2

    def rope(x, slots):
        w = slots * LANE
        first = lax.broadcasted_iota(jnp.int32, (1, w), 1) % LANE < half
        rot = jnp.where(first, -pltpu.roll(x, w - half, 1), pltpu.roll(x, half, 1))
        return x * jnp.concatenate([c] * slots, axis=1) + rot * jnp.concatenate([s] * slots, axis=1)

    q_ref[...] = (rope(proj(0, 1024), NSA_HEADS) * (NSA_DIM ** -0.5 * LOG2_E)).astype(BF)
    kc_ref[...] = proj(1024, 128)
    vc_ref[...] = proj(1152, 128)
    ks_ref[...] = (rope(proj(1280, 256), NSA_GROUPS) + tag2).astype(BF)
    ones = _denom_ones(NSA_GROUPS * LANE)
    vs_ref[...] = (proj(1536, 256) + ones).astype(BF)
    kw_ref[...] = rope(proj(1792, 256), NSA_GROUPS).astype(BF)
    vw_ref[...] = (proj(2048, 256) + ones).astype(BF)
    g_ref[...] = jax.nn.sigmoid(proj(2304, 256))


def _nsa_proj(h, w, b, cn, sn, *, layer, tm, seq):
    n, d = h.shape
    tps = seq // tm
    tab = pl.BlockSpec((tm, LANE), lambda i: (i % tps, 0))

    def out(width, dt):
        return pl.BlockSpec((tm, width), lambda i: (i, 0)), jax.ShapeDtypeStruct((n, width), dt)

    outs = [out(1024, BF), out(128, F32), out(128, F32), out(256, BF), out(256, BF), out(256, BF), out(256, BF), out(256, F32)]
    assert seq // SLC_BLOCK <= LANE - NSA_DIM
    return pl.pallas_call(
        functools.partial(_nsa_proj_kernel, tiles_per_seq=tps),
        grid=(n // tm,),
        in_specs=[pl.BlockSpec((tm, d), lambda i: (i, 0)), _layered(w, layer), _layered(b, layer), tab, tab],
        out_specs=[o[0] for o in outs],
        out_shape=[o[1] for o in outs],
        compiler_params=_params(("parallel",)),
        name="nsa_proj",
    )(h, w, b, cn, sn)


def _nsa_cmp_kernel(kc_ref, vc_ref, pe_ref, wk_ref, wkr_ref, wv_ref, c_ref, s_ref, kcmp_ref, vcmp_ref):
    n16 = kcmp_ref.shape[1]
    gl = NSA_GROUPS * NSA_DIM
    kc = kcr = vc = None
    for l in range(CMP_STRIDE):
        rows = slice(l * gl, (l + 1) * gl)
        xk = kc_ref[0, pl.ds(l, n16, stride=CMP_STRIDE), :]
        xv = vc_ref[0, pl.ds(l, n16, stride=CMP_STRIDE), :]
        terms = [((xk + pe_ref[0:1, rows]).astype(BF), (xv + pe_ref[2:3, rows]).astype(BF), 0),
                 ((pltpu.roll(xk, n16 - 1, 0) + pe_ref[1:2, rows]).astype(BF),
                  (pltpu.roll(xv, n16 - 1, 0) + pe_ref[3:4, rows]).astype(BF), 1)]
        for ak, av, part in terms:
            pk, pkr, pv = _dot(ak, wk_ref[part, rows, :]), _dot(ak, wkr_ref[part, rows, :]), _dot(av, wv_ref[part, rows, :])
            kc, kcr, vc = (pk, pkr, pv) if kc is None else (kc + pk, kcr + pkr, vc + pv)
    kcmp_ref[0] = (kc * c_ref[...] + kcr * s_ref[...]).astype(BF)
    vcmp_ref[0] = vc.astype(BF)


def _nsa_compress(kc, vc, pe, wk, wkr, wv, cc, sc, *, layer):
    bsz, seq, wide = kc.shape
    n16 = seq // CMP_STRIDE
    blk = pl.BlockSpec((1, seq, wide), lambda b: (b, 0, 0))
    oblk = pl.BlockSpec((1, n16, NSA_GROUPS * LANE), lambda b: (b, 0, 0))
    osh = jax.ShapeDtypeStruct((bsz, n16, NSA_GROUPS * LANE), BF)
    return pl.pallas_call(
        _nsa_cmp_kernel,
        grid=(bsz,),
        in_specs=[blk, blk] + [_layered(a, layer) for a in (pe, wk, wkr, wv)] + [_resident(cc.shape), _resident(sc.shape)],
        out_specs=[oblk, oblk],
        out_shape=[osh, osh],
        compiler_params=_params(("parallel",)),
        name="nsa_compress",
    )(kc, vc, pe, wk, wkr, wv, cc, sc)


def _nsa_attn_kernel(q_ref, kcmp_ref, vcmp_ref, ks_ref, vs_ref, kw_ref, vw_ref, g_ref, ov_ref, o_ref, *, top_k):
    qi = pl.program_id(1)
    T = q_ref.shape[1]
    R = NSA_HPG * T
    G = NSA_GROUPS
    q0 = qi * T
    qpos = lax.broadcasted_iota(jnp.int32, (T, 1), 0) + q0
    ncp = kcmp_ref.shape[1]
    nb = ov_ref.shape[0]
    ov_t = ov_ref[...]

    def add_per_query(x, b):
        w = x.shape[1]
        return (x.reshape(NSA_HPG, T, w) + b[None]).reshape(R, w)

    def lanes(g):
        return slice(g * LANE, (g + 1) * LANE)

    q4 = [jnp.concatenate([q_ref[0, :, (g * NSA_HPG + h) * LANE:(g * NSA_HPG + h + 1) * LANE]
                           for h in range(NSA_HPG)], axis=0) for g in range(G)]

    cmp_end = lax.broadcasted_iota(jnp.int32, (1, ncp), 1) * CMP_STRIDE + (CMP_BLOCK - 1)
    cbias = jnp.where(cmp_end <= qpos, 0.0, NEG)
    any_valid = jnp.where(qpos >= CMP_BLOCK - 1, 1.0, 0.0)
    jr = lax.broadcasted_iota(jnp.int32, (nb, 1), 0)
    jrf = jr.astype(F32)
    jq = lax.shift_right_logical(lax.broadcasted_iota(jnp.int32, (1, T), 1) + q0, SLC_SHIFT)
    forced = (jr == 0) | (jr == jq) | (jr == jq - 1)
    eye_t = jnp.where(lax.broadcasted_iota(jnp.int32, (T, T), 0) == lax.broadcasted_iota(jnp.int32, (T, T), 1),
                      1.0, 0.0).astype(BF)
    o_cmp, q4s = [], []
    for g in range(G):
        sm = _dot_t(q4[g], kcmp_ref[0, :, lanes(g)]).reshape(NSA_HPG, T, ncp) + cbias[None]
        e = jnp.exp2(sm - jnp.max(sm, -1, keepdims=True))
        p = e * (any_valid[None] / jnp.sum(e, -1, keepdims=True))
        o_cmp.append(_dot(p.reshape(R, ncp).astype(BF), vcmp_ref[0, :, lanes(g)]))
        psum = p[0] + p[1] + p[2] + p[3]
        hi = psum.astype(BF)
        r1 = psum - hi.astype(F32)
        mid = r1.astype(BF)
        lo = (r1 - mid.astype(F32)).astype(BF)
        imp = _dot_t(ov_t, hi) + _dot_t(ov_t, mid) + _dot_t(ov_t, lo)
        imp = jnp.where(forced, 1e9, imp)
        imp = jnp.where(jr <= jq, imp, -1.0)
        work = imp
        sel = jnp.zeros_like(imp)
        for _ in range(top_k):
            mx = jnp.max(work, 0, keepdims=True)
            idx = jnp.min(jnp.where(work == mx, jrf, float(nb)), 0, keepdims=True)
            pick = jrf == idx
            sel = jnp.where(pick, 1.0, sel)
            work = jnp.where(pick, -2.0, work)
        unsel_t = jnp.where(imp >= 0.0, sel, 0.0) - 1.0
        pad_t = [jnp.zeros((NSA_DIM, T), F32), unsel_t]
        if LANE - NSA_DIM - nb:
            pad_t.append(jnp.zeros((LANE - NSA_DIM - nb, T), F32))
        unsel = _dot_t(eye_t, jnp.concatenate(pad_t, axis=0).astype(BF)).astype(BF)
        q4s.append(add_per_query(q4[g], unsel))

    def slc_tile(j, carries, diagonal):
        k0 = pl.multiple_of(j * T, T)
        out = []
        for g in range(G):
            sc = _dot_t(q4s[g], ks_ref[0, pl.ds(k0, T), lanes(g)])
            if diagonal:
                sc = add_per_query(sc, jnp.where((lax.broadcasted_iota(jnp.int32, (1, T), 1) + k0) <= qpos, 0.0, NEG))
            out.append(_online_softmax_step(sc, vs_ref[0, pl.ds(k0, T), lanes(g)], carries[g]))
        return tuple(out)

    carries = lax.fori_loop(0, qi, lambda j, c: slc_tile(j, c, False),
                            tuple(_softmax_init(R, LANE) for _ in range(G)))
    carries = slc_tile(qi, carries, True)
    slc_acc = [acc for (_, acc) in carries]

    wk = WINDOW + T
    w0 = pl.multiple_of(jnp.maximum(q0 - WINDOW, 0), T)
    dist = qpos - (lax.broadcasted_iota(jnp.int32, (1, wk), 1) + w0)
    wbias = jnp.where((dist >= 0) & (dist < WINDOW), 0.0, NEG)
    o_win = []
    for g in range(G):
        sc = add_per_query(_dot_t(q4[g], kw_ref[0, pl.ds(w0, wk), lanes(g)]), wbias)
        e = jnp.exp2(sc - jnp.max(sc, -1, keepdims=True))
        o_win.append(_normalize(_dot(e.astype(BF), vw_ref[0, pl.ds(w0, wk), lanes(g)])))

    gw = NSA_HPG * LANE
    e_row = lax.broadcasted_iota(jnp.int32, (LANE, 3 * gw), 0)
    e_col = lax.broadcasted_iota(jnp.int32, (LANE, 3 * gw), 1)
    branch = jnp.where(e_col >= 2 * gw, 2, jnp.where(e_col >= gw, 1, 0))
    head = lax.shift_right_logical(e_col - branch * gw, LANE.bit_length() - 1)
    expand = jnp.where(e_row == 3 * head + branch, 1.0, 0.0).astype(BF)

    def heads_on_lanes(x):
        return jnp.concatenate([x[h * T:(h + 1) * T] for h in range(NSA_HPG)], axis=1)

    outs = []
    for g in range(G):
        gs = g_ref[0, :, lanes(g)]
        hi = gs.astype(BF)
        lo = (gs - hi.astype(F32)).astype(BF)
        gx = _dot(hi, expand) + _dot(lo, expand)
        outs.append(gx[:, 0:gw] * heads_on_lanes(o_cmp[g])
                    + gx[:, gw:2 * gw] * heads_on_lanes(_normalize(slc_acc[g]))
                    + gx[:, 2 * gw:3 * gw] * heads_on_lanes(o_win[g]))
    o_ref[0] = jnp.concatenate(outs, axis=1).astype(o_ref.dtype)


def _nsa_attention(q, kcmp, vcmp, ks, vs, kw, vw, gates, ov, *, tq, top_k):
    bsz, seq, wide = q.shape
    n16 = kcmp.shape[1]
    gw = NSA_GROUPS * LANE
    assert seq % tq == 0 and seq >= WINDOW + tq
    kern = functools.partial(_nsa_attn_kernel, top_k=top_k)
    cblk = pl.BlockSpec((1, n16, gw), lambda b, i: (b, 0, 0))
    sblk = pl.BlockSpec((1, seq, gw), lambda b, i: (b, 0, 0))
    return pl.pallas_call(
        kern,
        grid=(bsz, seq // tq),
        in_specs=[pl.BlockSpec((1, tq, wide), lambda b, i: (b, i, 0)), cblk, cblk, sblk, sblk, sblk, sblk,
                  pl.BlockSpec((1, tq, gw), lambda b, i: (b, i, 0)), _resident(ov.shape)],
        out_specs=pl.BlockSpec((1, tq, wide), lambda b, i: (b, i, 0)),
        out_shape=jax.ShapeDtypeStruct((bsz, seq, wide), BF),
        compiler_params=_params(("parallel", "arbitrary")),
        name="nsa_attn",
    )(q, kcmp, vcmp, ks, vs, kw, vw, gates, ov)


def _merge_kernel(x_ref, ab_ref, oc_ref, od_ref, wg_ref, bg_ref, wc_ref, wd_ref, wo_ref, g_ref, b_ref, o_ref):
    x = x_ref[...]
    gates = _dot(x.astype(BF), wg_ref[...]) + bg_ref[...]
    yc = _dot(oc_ref[...], wc_ref[...])
    yd = _dot(od_ref[...], wd_ref[...])
    merged = ab_ref[...] + jax.nn.sigmoid(gates[:, :D_MODEL]) * yc + jax.nn.sigmoid(gates[:, D_MODEL:]) * yd
    mix = _dot(merged.astype(BF), wo_ref[...])
    o_ref[...] = _ln(ALPHA * x + mix, g_ref[...], b_ref[...])


def _merge_ln(x, ab, oc, od, wg, bg, wc, wd, wo, g, b, *, layer, tm):
    n, d = x.shape
    row = pl.BlockSpec((tm, d), lambda i: (i, 0))
    return pl.pallas_call(
        _merge_kernel,
        grid=(n // tm,),
        in_specs=[row, row, row, row] + [_layered(a, layer) for a in (wg, bg, wc, wd, wo, g, b)],
        out_specs=row,
        out_shape=jax.ShapeDtypeStruct((n, d), F32),
        compiler_params=_params(("parallel",)),
        name="merge_ln",
    )(x, ab, oc, od, wg, bg, wc, wd, wo, g, b)


def _linear_kernel(x_ref, w_ref, o_ref):
    o_ref[...] = _dot(x_ref[...].astype(BF), w_ref[...]).astype(o_ref.dtype)


def _linear(x, w, *, layer, tm, dtype):
    n, d = x.shape
    return pl.pallas_call(
        _linear_kernel,
        grid=(n // tm,),
        in_specs=[pl.BlockSpec((tm, d), lambda i: (i, 0)), _layered(w, layer)],
        out_specs=pl.BlockSpec((tm, w.shape[2]), lambda i: (i, 0)),
        out_shape=jax.ShapeDtypeStruct((n, w.shape[2]), dtype),
        compiler_params=_params(("parallel",)),
        name="mem_kv",
    )(x, w)


def _xattn_kernel(x_ref, k_ref, v_ref, wq_ref, wo_ref, g_ref, b_ref, o_ref):
    x = x_ref[...]
    q = _dot(x.astype(BF), wq_ref[...]).astype(BF)
    k = k_ref[0]
    v = v_ref[0]
    heads = []
    for h in range(XATTN_HEADS):
        sl = slice(h * XATTN_DIM, (h + 1) * XATTN_DIM)
        s = _dot_t(q[:, sl], k[:, sl]) * (XATTN_DIM ** -0.5)
        e = jnp.exp(s - jnp.max(s, -1, keepdims=True))
        p = e / jnp.sum(e, -1, keepdims=True)
        heads.append(_dot(p.astype(BF), v[:, sl]))
    o = jnp.concatenate(heads, axis=1).astype(BF)
    o_ref[...] = _ln(ALPHA * x + _dot(o, wo_ref[...]), g_ref[...], b_ref[...])


def _xattn_ln(x, kv, wq, wo, g, b, *, layer, tm, seq):
    n, d = x.shape
    tps = seq // tm
    mlen = kv.shape[1]
    hd = XATTN_HEADS * XATTN_DIM
    return pl.pallas_call(
        _xattn_kernel,
        grid=(n // tm,),
        in_specs=[pl.BlockSpec((tm, d), lambda i: (i, 0)),
                  pl.BlockSpec((1, mlen, hd), lambda i: (i // tps, 0, 0)),
                  pl.BlockSpec((1, mlen, hd), lambda i: (i // tps, 0, 1))]
        + [_layered(a, layer) for a in (wq, wo, g, b)],
        out_specs=pl.BlockSpec((tm, d), lambda i: (i, 0)),
        out_shape=jax.ShapeDtypeStruct((n, d), F32),
        compiler_params=_params(("parallel",)),
        name="xattn_ln",
    )(x, kv, kv, wq, wo, g, b)


def _rope_tab(pos, dim):
    inv = ROPE_THETA ** (-(jnp.arange(0, dim, 2, dtype=F32) / dim))
    ang = pos[:, None] * inv[None, :]
    return jnp.cos(ang), jnp.sin(ang)


def _rot_cols(w, half):
    return jnp.concatenate([-w[..., half:2 * half], w[..., :half]], axis=-1)


def _pad_slots(w, n_slots, width):
    lead = w.shape[:-1]
    w = w.reshape(lead + (n_slots, width))
    w = jnp.pad(w, [(0, 0)] * len(lead) + [(0, 0), (0, LANE - width)])
    return w.reshape(lead + (n_slots * LANE,))


def _pad_rows(w, n_slots, width):
    d = w.shape[-1]
    w = w.reshape(n_slots, width, d)
    w = jnp.pad(w, [(0, 0), (0, LANE - width), (0, 0)])
    return w.reshape(n_slots * LANE, d)


def _layer_params(p):
    w_in, b_in = p["w_in"], p["b_in"]

    def cols(o, wd):
        return w_in[:, o:o + wd], b_in[o:o + wd]

    out = {}
    out["w_ab"] = jnp.concatenate([w_in[:, 0:2560], w_in[:, _O_GA:_O_GC]], axis=1).astype(BF)
    out["b_ab"] = jnp.concatenate([b_in[0:2560], b_in[_O_GA:_O_GC]])[None, :]
    wkr, bkr = cols(_O_KROPE, MLA_ROPE)
    padk =lambda a: jnp.pad(a, [(0, 0)] * (a.ndim - 1) + [(0, LANE - MLA_ROPE)])
    out["w_c"] = jnp.concatenate([w_in[:, _O_QLAT:_O_KROPE], padk(wkr)], axis=1).astype(BF)
    out["b_c"] = jnp.concatenate([b_in[_O_QLAT:_O_KROPE], padk(bkr)])[None, :]
    wuq = p["mla_wuq"].reshape(MLA_Q_RANK, MLA_HEADS, MLA_NOPE + MLA_ROPE)
    wq_c = jnp.pad(wuq, [(0, 0), (0, 0), (0, LANE - MLA_NOPE - MLA_ROPE)])
    out["wq_c"] = wq_c.reshape(MLA_Q_RANK, MLA_HEADS * LANE).astype(BF)
    wukv = p["mla_wukv"].reshape(MLA_KV_RANK, MLA_HEADS, MLA_NOPE + MLA_V)
    out["wk_c"] = jnp.pad(wukv[..., :MLA_NOPE], [(0, 0), (0, 0), (0, LANE - MLA_NOPE)]).reshape(MLA_KV_RANK, -1).astype(BF)
    out["wv_c"] = jnp.pad(wukv[..., MLA_NOPE:], [(0, 0), (0, 0), (0, LANE - MLA_V)]).reshape(MLA_KV_RANK, -1).astype(BF)
    wq, bq = cols(_O_NQ, NSA_HEADS * NSA_DIM)
    hd = NSA_DIM // 2
    pieces_w, pieces_b = [], []

    def add(w, b, slots):
        pieces_w.append(_pad_slots(w, slots, NSA_DIM))
        pieces_b.append(_pad_slots(b, slots, NSA_DIM))

    add(wq, bq, NSA_HEADS)
    wkc, bkc = cols(_O_NKC, 128)
    wvc, bvc = cols(_O_NVC, 128)
    pieces_w += [wkc, wvc]
    pieces_b += [bkc, bvc]
    for o in (_O_NKS, _O_NVS, _O_NKW, _O_NVW):
        add(*cols(o, 128), NSA_GROUPS)
    wg, bg = cols(_O_NGATE, NSA_HEADS * 3)
    pieces_w.append(_pad_slots(wg, NSA_GROUPS, NSA_HPG * 3))
    pieces_b.append(_pad_slots(bg, NSA_GROUPS, NSA_HPG * 3))
    out["w_d"] = jnp.concatenate(pieces_w, axis=1).astype(BF)
    out["b_d"] = jnp.concatenate(pieces_b)[None, :]

    def cmp_weights(w):
        eye = jnp.eye(NSA_GROUPS, dtype=F32)
        wp = jnp.pad(w, [(0, 0), (0, 0), (0, LANE - NSA_DIM)])
        full = jnp.einsum("lde,gh->lgdhe", wp, eye).reshape(CMP_BLOCK, NSA_GROUPS * NSA_DIM, NSA_GROUPS * LANE)
        return full.reshape(2, CMP_STRIDE * NSA_GROUPS * NSA_DIM, NSA_GROUPS * LANE).astype(BF)

    wck = p["nsa_wcmp_k"]
    out["wcmp_k"] = cmp_weights(wck)
    out["wcmp_kr"] = cmp_weights(_rot_cols(wck, hd))
    out["wcmp_v"] = cmp_weights(p["nsa_wcmp_v"])

    def pe_rows(pe):
        t = jnp.broadcast_to(pe[:, None, :], (CMP_BLOCK, NSA_GROUPS, NSA_DIM))
        return t.reshape(2, CMP_STRIDE * NSA_GROUPS * NSA_DIM)

    out["pe"] = jnp.concatenate([pe_rows(p["nsa_pe_k"]), pe_rows(p["nsa_pe_v"])], axis=0)
    out["w_g"] = w_in[:, _O_GC:].astype(BF)
    out["b_g"] = b_in[_O_GC:][None, :]
    out["wout_c"] = _pad_rows(p["mla_wout"], MLA_HEADS, MLA_V).astype(BF)
    out["wout_d"] = _pad_rows(p["nsa_wout"], NSA_HEADS, NSA_DIM).astype(BF)
    out["gmlp_bs_t"] = p["gmlp_bs"].T
    for name in ("gmlp_wout", "conv_wout", "w_o", "xattn_wq", "xattn_wo"):
        out[name] = p[name].astype(BF)
    out["xattn_wkv"] = jnp.concatenate([p["xattn_wk"], p["xattn_wv"]], axis=1).astype(BF)
    return out


def _tables(seq):
    pos = jnp.arange(seq, dtype=F32)
    c16, s16 = _rope_tab(pos, MLA_ROPE)
    one = jnp.ones((seq, MLA_NOPE), F32)
    zero = jnp.zeros((seq, MLA_NOPE), F32)
    tail = LANE - MLA_NOPE - MLA_ROPE
    cq = jnp.concatenate([one, c16, c16, jnp.ones((seq, tail), F32)], axis=1)
    sq = jnp.concatenate([zero, s16, s16, jnp.zeros((seq, tail), F32)], axis=1)
    ck = jnp.pad(jnp.concatenate([c16, c16], axis=1), [(0, 0), (0, LANE - MLA_ROPE)])
    sk = jnp.pad(jnp.concatenate([s16, s16], axis=1), [(0, 0), (0, LANE - MLA_ROPE)])
    c32, s32 = _rope_tab(pos, NSA_DIM)
    cn = jnp.pad(jnp.concatenate([c32, c32], axis=1), [(0, 0), (0, LANE - NSA_DIM)])
    sn = jnp.pad(jnp.concatenate([s32, s32], axis=1), [(0, 0), (0, LANE - NSA_DIM)])
    n16 = seq // CMP_STRIDE
    cend = (jnp.arange(n16) * CMP_STRIDE + CMP_BLOCK - 1).astype(F32)
    cc32, cs32 = _rope_tab(cend, NSA_DIM)
    ccg = jnp.pad(jnp.concatenate([cc32, cc32], axis=1), [(0, 0), (0, LANE - NSA_DIM)])
    csg = jnp.pad(jnp.concatenate([cs32, cs32], axis=1), [(0, 0), (0, LANE - NSA_DIM)])
    cc = jnp.concatenate([ccg] * NSA_GROUPS, axis=1)
    cs = jnp.concatenate([csg] * NSA_GROUPS, axis=1)
    n_cmp = (seq - CMP_BLOCK) // CMP_STRIDE + 1
    n_slc = seq // SLC_BLOCK
    cstart = jnp.arange(n16) * CMP_STRIDE
    sstart = jnp.arange(n_slc) * SLC_BLOCK
    ovl = (jnp.minimum(cstart[None, :] + CMP_BLOCK, sstart[:, None] + SLC_BLOCK)
           - jnp.maximum(cstart[None, :], sstart[:, None]))
    ovl = jnp.clip(ovl, 0).astype(F32) / CMP_BLOCK
    ovl = jnp.where(jnp.arange(n16)[None, :] < n_cmp, ovl, 0.0).astype(BF)
    return dict(cq=cq, sq=sq, ck=ck, sk=sk, cn=cn, sn=sn, cc=cc, cs=cs, ovl=ovl)


def kernel(x, mem, ffn1_w1, ffn1_w3, ffn1_w2, ln1_g, ln1_b, w_in, b_in, gmlp_ln_g, gmlp_ln_b, gmlp_ws, gmlp_bs, gmlp_wout, conv_w, conv_wout, mla_qnorm_g, mla_kvnorm_g, mla_wuq, mla_wukv, mla_wout, nsa_pe_k, nsa_pe_v, nsa_wcmp_k, nsa_wcmp_v, nsa_wout, w_o, ln2_g, ln2_b, xattn_wq, xattn_wk, xattn_wv, xattn_wo, ln3_g, ln3_b, ffn2_w1, ffn2_w3, ffn2_w2, ln4_g, ln4_b):
    bsz, seq, d = x.shape
    mlen = mem.shape[1]
    n = bsz * seq
    assert d == D_MODEL and seq % 1024 == 0
    lp = jax.vmap(_layer_params)(dict(
        w_in=w_in, b_in=b_in, mla_wuq=mla_wuq, mla_wukv=mla_wukv, mla_wout=mla_wout, nsa_pe_k=nsa_pe_k,
        nsa_pe_v=nsa_pe_v, nsa_wcmp_k=nsa_wcmp_k, nsa_wcmp_v=nsa_wcmp_v, nsa_wout=nsa_wout, gmlp_bs=gmlp_bs,
        gmlp_wout=gmlp_wout, conv_wout=conv_wout, w_o=w_o, xattn_wq=xattn_wq, xattn_wk=xattn_wk,
        xattn_wv=xattn_wv, xattn_wo=xattn_wo))
    tb = _tables(seq)
    tm = 512
    tm_ffn = 1024
    tf = D_FF // 11
    top_k = min(SLC_TOPK, seq // SLC_BLOCK)
    row = lambda a: a[:, None, :]

    def cast_stacked(w):
        return _to_bf16(w.reshape(w.shape[0] * w.shape[1], w.shape[2])).reshape(w.shape)

    ffn1 = [cast_stacked(w) for w in (ffn1_w1, ffn1_w3, ffn1_w2)]
    ffn2 = [cast_stacked(w) for w in (ffn2_w1, ffn2_w3, ffn2_w2)]
    h = x.reshape(n, d)
    mem2 = mem.reshape(bsz * mlen, d)
    wide = MLA_HEADS * LANE
    gw = NSA_GROUPS * LANE
    for l in range(DEPTH):
        h = _ffn_ln(h, *ffn1, row(ln1_g), row(ln1_b), layer=l, tm=tm_ffn, tf=tf)
        ab = _mix_ab(h, lp["w_ab"], lp["b_ab"], row(gmlp_ln_g), row(gmlp_ln_b), gmlp_ws, lp["gmlp_bs_t"],
                     lp["gmlp_wout"], conv_w, lp["conv_wout"], layer=l, tm=tm, seq=seq)
        qc, kc_, vc_ = _mla_proj(h, lp["w_c"], lp["b_c"], row(mla_qnorm_g), row(mla_kvnorm_g),
                                 lp["wq_c"], lp["wk_c"], lp["wv_c"],
                                 tb["cq"], tb["sq"], tb["ck"], tb["sk"], layer=l, tm=tm, seq=seq)
        oc = _flash_causal(qc.reshape(bsz, seq, wide), kc_.reshape(bsz, seq, wide), vc_.reshape(bsz, seq, wide), tq=1024, hp=4)
        qn, nkc, nvc, nks, nvs, nkw, nvw, gates = _nsa_proj(h, lp["w_d"], lp["b_d"], tb["cn"], tb["sn"],
                                                            layer=l, tm=tm, seq=seq)
        kcmp, vcmp = _nsa_compress(nkc.reshape(bsz, seq, NSA_GROUPS * NSA_DIM), nvc.reshape(bsz, seq, NSA_GROUPS * NSA_DIM),
                                   lp["pe"], lp["wcmp_k"], lp["wcmp_kr"], lp["wcmp_v"], tb["cc"], tb["cs"], layer=l)
        od = _nsa_attention(qn.reshape(bsz, seq, NSA_HEADS * LANE), kcmp, vcmp,
                            nks.reshape(bsz, seq, gw), nvs.reshape(bsz, seq, gw),
                            nkw.reshape(bsz, seq, gw), nvw.reshape(bsz, seq, gw),
                            gates.reshape(bsz, seq, gw), tb["ovl"], tq=512, top_k=top_k)
        h = _merge_ln(h, ab, oc.reshape(n, wide), od.reshape(n, NSA_HEADS * LANE), lp["w_g"], lp["b_g"],
                      lp["wout_c"], lp["wout_d"], lp["w_o"], row(ln2_g), row(ln2_b), layer=l, tm=tm)
        kv = _linear(mem2, lp["xattn_wkv"], layer=l, tm=min(256, bsz * mlen), dtype=BF)
        h = _xattn_ln(h, kv.reshape(bsz, mlen, 2 * XATTN_HEADS * XATTN_DIM), lp["xattn_wq"], lp["xattn_wo"],
                      row(ln3_g), row(ln3_b), layer=l, tm=tm, seq=seq)
        h = _ffn_ln(h, *ffn2, row(ln4_g), row(ln4_b), layer=l, tm=tm_ffn, tf=tf)
    return h.reshape(bsz, seq, d)
```

```python
import functools

import jax
import jax.numpy as jnp
from jax import lax
from jax.experimental import pallas as pl
from jax.experimental.pallas import tpu as pltpu

BF = jnp.bfloat16
F32 = jnp.float32

D_MODEL = 1024
D_FF = 2816
LN_EPS = 1e-5
RMS_EPS = 1e-6
ROPE_THETA = 10000.0
DEPTH = 2
ALPHA = (2 * DEPTH) ** 0.25
NEG = -1e30
LOG2_E = 1.4426950408889634
DENOM_LANE = 64
MASK_BIG = 2.0 ** 100

GMLP_CHUNK = 128
GMLP_GROUPS = 4
GMLP_WIDTH = 512
CONV_WIDTH = 512
CONV_K = 3
MLA_HEADS = 8
MLA_Q_RANK = 256
MLA_KV_RANK = 128
MLA_NOPE = 64
MLA_ROPE = 32
MLA_V = 64
NSA_HEADS = 8
NSA_GROUPS = 2
NSA_HPG = 4
NSA_DIM = 64
CMP_BLOCK = 32
CMP_STRIDE = 16
SLC_BLOCK = 64
SLC_SHIFT = SLC_BLOCK.bit_length() - 1
SLC_TOPK = 8
WINDOW = 512
XATTN_HEADS = 4
XATTN_DIM = 128

LANE = 128
CONV_HALO = 8
VMEM_LIMIT = 56 * 1024 * 1024

_O_U, _O_V, _O_CB, _O_CC, _O_CH = 0, 512, 1024, 1536, 2048
_O_QLAT, _O_KVLAT, _O_KROPE = 2560, 2816, 2944
_O_NQ, _O_NKC, _O_NVC, _O_NKS, _O_NVS, _O_NKW, _O_NVW, _O_NGATE = 2976, 3488, 3616, 3744, 3872, 4000, 4128, 4256
_O_GA, _O_GB, _O_GC, _O_GD = 4280, 5304, 6328, 7352


def _dot(a, b):
    return jnp.dot(a, b, preferred_element_type=F32)


def _dot_t(a, b):
    return lax.dot_general(a, b, (((1,), (1,)), ((), ())), preferred_element_type=F32)


def _ln(y, g, b):
    mu = jnp.mean(y, -1, keepdims=True)
    d = y - mu
    var = jnp.mean(d * d, -1, keepdims=True)
    return d * lax.rsqrt(var + LN_EPS) * g + b


def _rms(x, g):
    return x * lax.rsqrt(jnp.mean(x * x, -1, keepdims=True) + RMS_EPS) * g


def _resident(shape):
    n = len(shape)
    return pl.BlockSpec(shape, lambda *_: (0,) * n, pipeline_mode=pl.Buffered(1))


def _layered(a, layer):
    n = a.ndim - 1
    return pl.BlockSpec((None,) + a.shape[1:], lambda *_: (layer,) + (0,) * n, pipeline_mode=pl.Buffered(1))


def _params(sem):
    return pltpu.CompilerParams(dimension_semantics=sem, vmem_limit_bytes=VMEM_LIMIT)


def _cast_kernel(x_ref, o_ref):
    o_ref[...] = x_ref[...].astype(o_ref.dtype)


def _to_bf16(w, *, rows=256):
    r, c = w.shape
    rows = min(rows, r)
    assert r % rows == 0
    return pl.pallas_call(
        _cast_kernel,
        grid=(r // rows,),
        in_specs=[pl.BlockSpec((rows, c), lambda i: (i, 0))],
        out_specs=pl.BlockSpec((rows, c), lambda i: (i, 0)),
        out_shape=jax.ShapeDtypeStruct((r, c), BF),
        compiler_params=_params(("parallel",)),
        name="to_bf16",
    )(w)


def _ffn_ln_kernel(x_ref, w1_ref, w3_ref, w2_ref, g_ref, b_ref, o_ref, *, tf):
    x = x_ref[...]
    xb = x.astype(BF)
    acc = None
    for c in range(w1_ref.shape[1] // tf):
        cols = slice(c * tf, (c + 1) * tf)
        h1 = _dot(xb, w1_ref[:, cols])
        h3 = _dot(xb, w3_ref[:, cols])
        hh = (h1 * jax.nn.sigmoid(h1)) * h3
        part = _dot(hh.astype(BF), w2_ref[cols, :])
        acc = part if acc is None else acc + part
    o_ref[...] = _ln(ALPHA * x + 0.5 * acc, g_ref[...], b_ref[...])


def _ffn_ln(x, w1, w3, w2, g, b, *, layer, tm, tf):
    n, d = x.shape
    return pl.pallas_call(
        functools.partial(_ffn_ln_kernel, tf=tf),
        grid=(n // tm,),
        in_specs=[pl.BlockSpec((tm, d), lambda i: (i, 0))] + [_layered(a, layer) for a in (w1, w3, w2, g, b)],
        out_specs=pl.BlockSpec((tm, d), lambda i: (i, 0)),
        out_shape=jax.ShapeDtypeStruct((n, d), F32),
        compiler_params=_params(("parallel",)),
        name="ffn_ln",
    )(x, w1, w3, w2, g, b)


def _ab_kernel(h_ref, w_ref, b_ref, lng_ref, lnb_ref, ws_ref, bst_ref, wga_ref, cw_ref, wcb_ref,
               o_ref, prev_ref, *, tiles_per_seq):
    i = pl.program_id(0)
    tm = h_ref.shape[0]
    hb = h_ref[...].astype(BF)

    def proj(c0, width):
        return _dot(hb, w_ref[:, c0:c0 + width]) + b_ref[:, c0:c0 + width]

    u = proj(0, GMLP_WIDTH)
    v = _ln(proj(512, GMLP_WIDTH), lng_ref[...], lnb_ref[...]).astype(BF)
    row = lax.broadcasted_iota(jnp.int32, (GMLP_CHUNK, GMLP_CHUNK), 0)
    col = lax.broadcasted_iota(jnp.int32, (GMLP_CHUNK, GMLP_CHUNK), 1)
    gd = GMLP_WIDTH // GMLP_GROUPS
    wgs = [jnp.where(row >= col, ws_ref[g], 0.0).astype(BF) for g in range(GMLP_GROUPS)]
    chunks = []
    for c in range(tm // GMLP_CHUNK):
        r0 = c * GMLP_CHUNK
        chunks.append(jnp.concatenate(
            [_dot(wgs[g], v[r0:r0 + GMLP_CHUNK, g * gd:(g + 1) * gd]) + bst_ref[:, g:g + 1]
             for g in range(GMLP_GROUPS)], axis=1))
    s = jnp.concatenate(chunks, axis=0)
    ya = _dot((u * s).astype(BF), wga_ref[...])

    cb = proj(1024, CONV_WIDTH)
    z = proj(1536, CONV_WIDTH) * proj(2048, CONV_WIDTH)

    @pl.when(i % tiles_per_seq == 0)
    def _():
        prev_ref[...] = jnp.zeros_like(prev_ref)

    zext = jnp.concatenate([prev_ref[...], z], axis=0)
    z1 = pltpu.roll(zext, 1, 0)[CONV_HALO:]
    z2 = pltpu.roll(zext, 2, 0)[CONV_HALO:]
    y = cw_ref[0:1, :] * z2 + cw_ref[1:2, :] * z1 + cw_ref[2:3, :] * z
    prev_ref[...] = z[tm - CONV_HALO:, :]
    yb = _dot((cb * y).astype(BF), wcb_ref[...])

    ga = proj(2560, D_MODEL)
    gb = proj(3584, D_MODEL)
    o_ref[...] = jax.nn.sigmoid(ga) * ya + jax.nn.sigmoid(gb) * yb


def _mix_ab(h, w, b, lng, lnb, ws, bst, wga, cw, wcb, *, layer, tm, seq):
    n, d = h.shape
    kern = functools.partial(_ab_kernel, tiles_per_seq=seq // tm)
    return pl.pallas_call(
        kern,
        grid=(n // tm,),
        in_specs=[pl.BlockSpec((tm, d), lambda i: (i, 0))]
        + [_layered(a, layer) for a in (w, b, lng, lnb, ws, bst, wga, cw, wcb)],
        out_specs=pl.BlockSpec((tm, d), lambda i: (i, 0)),
        out_shape=jax.ShapeDtypeStruct((n, d), F32),
        scratch_shapes=[pltpu.VMEM((CONV_HALO, CONV_WIDTH), F32)],
        compiler_params=_params(("arbitrary",)),
        name="mix_ab",
    )(h, w, b, lng, lnb, ws, bst, wga, cw, wcb)


def _mla_proj_kernel(h_ref, w_ref, b_ref, qg_ref, kvg_ref, wq_ref, wk_ref, wv_ref,
                     cq_ref, sq_ref, ck_ref, sk_ref, q_ref, k_ref, v_ref):
    hb = h_ref[...].astype(BF)
    z = _dot(hb, w_ref[...]) + b_ref[...]
    qn = _rms(z[:, 0:256], qg_ref[...]).astype(BF)
    kvn = _rms(z[:, 256:384], kvg_ref[...]).astype(BF)
    half = MLA_ROPE // 2

    def rotate_half(x, start):
        w = x.shape[1]
        first = lax.broadcasted_iota(jnp.int32, (1, w), 1) % LANE < start + half
        return jnp.where(first, -pltpu.roll(x, w - half, 1), pltpu.roll(x, half, 1))

    cq = jnp.concatenate([cq_ref[...]] * MLA_HEADS, axis=1)
    sq = jnp.concatenate([sq_ref[...]] * MLA_HEADS, axis=1)
    scale = (MLA_NOPE + MLA_ROPE) ** -0.5 * LOG2_E
    q = _dot(qn, wq_ref[...])
    q_ref[...] = ((q * cq + rotate_half(q, MLA_NOPE) * sq) * scale).astype(BF)
    kr = z[:, 384:512]
    kpe = pltpu.roll(kr * ck_ref[...] + rotate_half(kr, 0) * sk_ref[...], MLA_NOPE, 1)
    k_ref[...] = (_dot(kvn, wk_ref[...]) + jnp.concatenate([kpe] * MLA_HEADS, axis=1)).astype(BF)
    v_ref[...] = (_dot(kvn, wv_ref[...]) + _denom_ones(v_ref.shape[1])).astype(BF)


def _mla_proj(h, w, b, qg, kvg, wq, wk, wv, cq, sq, ck, sk, *, layer, tm, seq):
    n, d = h.shape
    tps = seq // tm
    tab = pl.BlockSpec((tm, LANE), lambda i: (i % tps, 0))
    wide = MLA_HEADS * LANE
    out = jax.ShapeDtypeStruct((n, wide), BF)
    return pl.pallas_call(
        _mla_proj_kernel,
        grid=(n // tm,),
        in_specs=[pl.BlockSpec((tm, d), lambda i: (i, 0))]
        + [_layered(a, layer) for a in (w, b, qg, kvg, wq, wk, wv)] + [tab] * 4,
        out_specs=[pl.BlockSpec((tm, wide), lambda i: (i, 0))] * 3,
        out_shape=[out, out, out],
        compiler_params=_params(("parallel",)),
        name="mla_proj",
    )(h, w, b, qg, kvg, wq, wk, wv, cq, sq, ck, sk)


def _online_softmax_step(s, v, carry):
    m, acc = carry
    m_new = jnp.maximum(m, jnp.max(s, -1, keepdims=True))
    p = jnp.exp2(s - m_new)
    acc = jnp.exp2(m - m_new) * acc + _dot(p.astype(BF), v)
    return m_new, acc


def _softmax_init(rows, width):
    return (jnp.full((rows, 1), NEG, F32), jnp.zeros((rows, width), F32))


def _normalize(acc):
    return acc * (1.0 / acc[:, DENOM_LANE:DENOM_LANE + 1])


def _pack_head_pairs(slots):
    low = lax.broadcasted_iota(jnp.int32, (1, LANE), 1) < DENOM_LANE
    return jnp.concatenate([jnp.where(low, a, pltpu.roll(b, DENOM_LANE, 1))
                            for a, b in zip(slots[0::2], slots[1::2])], axis=1)


def _denom_ones(width):
    lane = lax.broadcasted_iota(jnp.int32, (1, width), 1)
    return jnp.where(lane % LANE == DENOM_LANE, 1.0, 0.0)


def _flash_kernel(q_ref, k_ref, v_ref, o_ref, *, tq, hp):
    qi = pl.program_id(2)
    q0 = qi * tq
    qs = [q_ref[0, :, h * LANE:(h + 1) * LANE] for h in range(hp)]

    def tile(j, carries, width, diagonal):
        k0 = pl.multiple_of(j * width, width)
        out = []
        for h in range(hp):
            s = _dot_t(qs[h], k_ref[0, pl.ds(k0, width), h * LANE:(h + 1) * LANE])
            if diagonal:
                r = lax.broadcasted_iota(jnp.int32, (tq, width), 0)
                c = lax.broadcasted_iota(jnp.int32, (tq, width), 1)
                s = jnp.where(c <= r, s, NEG)
            out.append(_online_softmax_step(s, v_ref[0, pl.ds(k0, width), h * LANE:(h + 1) * LANE], carries[h]))
        return tuple(out)

    init = tuple(_softmax_init(tq, LANE) for _ in range(hp))
    carries = lax.fori_loop(0, qi, lambda j, c: tile(j, c, tq, False), init)

    half = tq // 2
    carries = tile(2 * qi, carries, half, True)
    r = lax.broadcasted_iota(jnp.int32, (half, half), 0)
    c = lax.broadcasted_iota(jnp.int32, (half, half), 1)
    k1 = pl.multiple_of(q0 + half, half)
    out = []
    for h in range(hp):
        m, acc = carries[h]
        s = _dot_t(qs[h][half:], k_ref[0, pl.ds(k1, half), h * LANE:(h + 1) * LANE])
        m2, acc2 = _online_softmax_step(jnp.where(c <= r, s, NEG), v_ref[0, pl.ds(k1, half), h * LANE:(h + 1) * LANE],
                                        (m[half:], acc[half:]))
        out.append(jnp.concatenate([acc[:half], acc2], axis=0))
    o_ref[0] = _pack_head_pairs([_normalize(acc) for acc in out]).astype(o_ref.dtype)


def _flash_causal(q, k, v, *, tq, hp):
    bsz, seq, wide = q.shape
    heads = wide // LANE
    assert seq % tq == 0 and hp % 2 == 0
    half_lane = LANE // 2
    kern = functools.partial(_flash_kernel, tq=tq, hp=hp)
    return pl.pallas_call(
        kern,
        grid=(bsz, heads // hp, seq // tq),
        in_specs=[
            pl.BlockSpec((1, tq, hp * LANE), lambda b, h, i: (b, i, h)),
            pl.BlockSpec((1, seq, hp * LANE), lambda b, h, i: (b, 0, h)),
            pl.BlockSpec((1, seq, hp * LANE), lambda b, h, i: (b, 0, h)),
        ],
        out_specs=pl.BlockSpec((1, tq, hp * half_lane), lambda b, h, i: (b, i, h)),
        out_shape=jax.ShapeDtypeStruct((bsz, seq, heads * half_lane), BF),
        compiler_params=_params(("parallel", "parallel", "arbitrary")),
        name="mla_flash",
    )(q, k, v)


def _nsa_proj_kernel(h_ref, w_ref, b_ref, c_ref, s_ref, q_ref, kc_ref, vc_ref, ks_ref, vs_ref, kw_ref, vw_ref, g_ref,
                     *, tiles_per_seq):
    tm = h_ref.shape[0]
    hb = h_ref[...].astype(BF)
    pos = (pl.program_id(0) % tiles_per_seq) * tm + lax.broadcasted_iota(jnp.int32, (tm, LANE), 0)
    lane = lax.broadcasted_iota(jnp.int32, (tm, LANE), 1)
    tag = jnp.where(lane == NSA_DIM + lax.shift_right_logical(pos, SLC_SHIFT), MASK_BIG, 0.0)
    tag2 = jnp.concatenate([tag] * NSA_GROUPS, axis=1)
    c = c_ref[...]
    s = s_ref[...]
    half = NSA_DIM // 2

    def rope(x):
        w = x.shape[1]
        first = lax.broadcasted_iota(jnp.int32, (1, w), 1) % NSA_DIM < half
        rot = jnp.where(first, -pltpu.roll(x, w - half, 1), pltpu.roll(x, half, 1))
        reps = w // LANE
        return x * jnp.concatenate([c] * reps, axis=1) + rot * jnp.concatenate([s] * reps, axis=1)

    low = lax.broadcasted_iota(jnp.int32, (1, LANE), 1) < NSA_DIM

    def spread(x):
        out = []
        for j in range(x.shape[1] // LANE):
            blk = x[:, j * LANE:(j + 1) * LANE]
            out += [jnp.where(low, blk, 0.0), jnp.where(low, pltpu.roll(blk, NSA_DIM, 1), 0.0)]
        return jnp.concatenate(out, axis=1)

    z = _dot(hb, w_ref[...]) + b_ref[...]
    q_ref[...] = (spread(rope(z[:, 0:512])) * (NSA_DIM ** -0.5 * LOG2_E)).astype(BF)
    kc_ref[...] = z[:, 512:640]
    vc_ref[...] = z[:, 640:768]
    ks_ref[...] = (spread(rope(z[:, 768:896])) + tag2).astype(BF)
    ones = _denom_ones(NSA_GROUPS * LANE)
    vs_ref[...] = (spread(z[:, 896:1024]) + ones).astype(BF)
    kw_ref[...] = spread(rope(z[:, 1024:1152])).astype(BF)
    vw_ref[...] = (spread(z[:, 1152:1280]) + ones).astype(BF)
    g_ref[...] = jax.nn.sigmoid(z[:, 1280:1408])


def _nsa_proj(h, w, b, cn, sn, *, layer, tm, seq):
    n, d = h.shape
    tps = seq // tm
    tab = pl.BlockSpec((tm, LANE), lambda i: (i % tps, 0))

    def out(width, dt):
        return pl.BlockSpec((tm, width), lambda i: (i, 0)), jax.ShapeDtypeStruct((n, width), dt)

    outs = [out(1024, BF), out(128, F32), out(128, F32), out(256, BF), out(256, BF), out(256, BF), out(256, BF), out(128, F32)]
    assert seq // SLC_BLOCK <= LANE - NSA_DIM
    return pl.pallas_call(
        functools.partial(_nsa_proj_kernel, tiles_per_seq=tps),
        grid=(n // tm,),
        in_specs=[pl.BlockSpec((tm, d), lambda i: (i, 0)), _layered(w, layer), _layered(b, layer), tab, tab],
        out_specs=[o[0] for o in outs],
        out_shape=[o[1] for o in outs],
        compiler_params=_params(("parallel",)),
        name="nsa_proj",
    )(h, w, b, cn, sn)


def _nsa_cmp_kernel(kc_ref, vc_ref, pe_ref, wk_ref, wkr_ref, wv_ref, c_ref, s_ref, kcmp_ref, vcmp_ref):
    n16 = kcmp_ref.shape[1]
    gl = NSA_GROUPS * NSA_DIM
    kc = kcr = vc = None
    for l in range(CMP_STRIDE):
        rows = slice(l * gl, (l + 1) * gl)
        xk = kc_ref[0, pl.ds(l, n16, stride=CMP_STRIDE), :]
        xv = vc_ref[0, pl.ds(l, n16, stride=CMP_STRIDE), :]
        terms = [((xk + pe_ref[0:1, rows]).astype(BF), (xv + pe_ref[2:3, rows]).astype(BF), 0),
                 ((pltpu.roll(xk, n16 - 1, 0) + pe_ref[1:2, rows]).astype(BF),
                  (pltpu.roll(xv, n16 - 1, 0) + pe_ref[3:4, rows]).astype(BF), 1)]
        for ak, av, part in terms:
            pk, pkr, pv = _dot(ak, wk_ref[part, rows, :]), _dot(ak, wkr_ref[part, rows, :]), _dot(av, wv_ref[part, rows, :])
            kc, kcr, vc = (pk, pkr, pv) if kc is None else (kc + pk, kcr + pkr, vc + pv)
    kcmp_ref[0] = (kc * c_ref[...] + kcr * s_ref[...]).astype(BF)
    vcmp_ref[0] = vc.astype(BF)


def _nsa_compress(kc, vc, pe, wk, wkr, wv, cc, sc, *, layer):
    bsz, seq, wide = kc.shape
    n16 = seq // CMP_STRIDE
    blk = pl.BlockSpec((1, seq, wide), lambda b: (b, 0, 0))
    oblk = pl.BlockSpec((1, n16, NSA_GROUPS * LANE), lambda b: (b, 0, 0))
    osh = jax.ShapeDtypeStruct((bsz, n16, NSA_GROUPS * LANE), BF)
    return pl.pallas_call(
        _nsa_cmp_kernel,
        grid=(bsz,),
        in_specs=[blk, blk] + [_layered(a, layer) for a in (pe, wk, wkr, wv)] + [_resident(cc.shape), _resident(sc.shape)],
        out_specs=[oblk, oblk],
        out_shape=[osh, osh],
        compiler_params=_params(("parallel",)),
        name="nsa_compress",
    )(kc, vc, pe, wk, wkr, wv, cc, sc)


def _nsa_attn_kernel(q_ref, kcmp_ref, vcmp_ref, ks_ref, vs_ref, kw_ref, vw_ref, g_ref, ov_ref, o_ref, *, top_k):
    qi = pl.program_id(1)
    T = q_ref.shape[1]
    R = NSA_HPG * T
    G = NSA_GROUPS
    q0 = qi * T
    qpos = lax.broadcasted_iota(jnp.int32, (T, 1), 0) + q0
    ncp = kcmp_ref.shape[1]
    nb = ov_ref.shape[0]
    ov_t = ov_ref[...]

    def add_per_query(x, b):
        w = x.shape[1]
        return (x.reshape(NSA_HPG, T, w) + b[None]).reshape(R, w)

    def lanes(g):
        return slice(g * LANE, (g + 1) * LANE)

    q4 = [jnp.concatenate([q_ref[0, :, (g * NSA_HPG + h) * LANE:(g * NSA_HPG + h + 1) * LANE]
                           for h in range(NSA_HPG)], axis=0) for g in range(G)]

    cmp_end = lax.broadcasted_iota(jnp.int32, (1, ncp), 1) * CMP_STRIDE + (CMP_BLOCK - 1)
    cbias = jnp.where(cmp_end <= qpos, 0.0, NEG)
    any_valid = jnp.where(qpos >= CMP_BLOCK - 1, 1.0, 0.0)
    jr = lax.broadcasted_iota(jnp.int32, (nb, 1), 0)
    jrf = jr.astype(F32)
    jq = lax.shift_right_logical(lax.broadcasted_iota(jnp.int32, (1, T), 1) + q0, SLC_SHIFT)
    forced = (jr == 0) | (jr == jq) | (jr == jq - 1)
    eye_t = jnp.where(lax.broadcasted_iota(jnp.int32, (T, T), 0) == lax.broadcasted_iota(jnp.int32, (T, T), 1),
                      1.0, 0.0).astype(BF)
    o_cmp, q4s = [], []
    for g in range(G):
        sm = _dot_t(q4[g], kcmp_ref[0, :, lanes(g)]).reshape(NSA_HPG, T, ncp) + cbias[None]
        e = jnp.exp2(sm - jnp.max(sm, -1, keepdims=True))
        p = e * (any_valid[None] / jnp.sum(e, -1, keepdims=True))
        o_cmp.append(_dot(p.reshape(R, ncp).astype(BF), vcmp_ref[0, :, lanes(g)]))
        psum = p[0] + p[1] + p[2] + p[3]
        hi = psum.astype(BF)
        r1 = psum - hi.astype(F32)
        mid = r1.astype(BF)
        lo = (r1 - mid.astype(F32)).astype(BF)
        imp = _dot_t(ov_t, hi) + _dot_t(ov_t, mid) + _dot_t(ov_t, lo)
        imp = jnp.where(forced, 1e9, imp)
        imp = jnp.where(jr <= jq, imp, -1.0)
        work = imp
        sel = jnp.zeros_like(imp)
        for _ in range(top_k):
            mx = jnp.max(work, 0, keepdims=True)
            idx = jnp.min(jnp.where(work == mx, jrf, float(nb)), 0, keepdims=True)
            pick = jrf == idx
            sel = jnp.where(pick, 1.0, sel)
            work = jnp.where(pick, -2.0, work)
        unsel_t = jnp.where(imp >= 0.0, sel, 0.0) - 1.0
        pad_t = [jnp.zeros((NSA_DIM, T), F32), unsel_t]
        if LANE - NSA_DIM - nb:
            pad_t.append(jnp.zeros((LANE - NSA_DIM - nb, T), F32))
        unsel = _dot_t(eye_t, jnp.concatenate(pad_t, axis=0).astype(BF)).astype(BF)
        q4s.append(add_per_query(q4[g], unsel))

    def slc_tile(j, carries, diagonal):
        k0 = pl.multiple_of(j * T, T)
        out = []
        for g in range(G):
            sc = _dot_t(q4s[g], ks_ref[0, pl.ds(k0, T), lanes(g)])
            if diagonal:
                sc = add_per_query(sc, jnp.where((lax.broadcasted_iota(jnp.int32, (1, T), 1) + k0) <= qpos, 0.0, NEG))
            out.append(_online_softmax_step(sc, vs_ref[0, pl.ds(k0, T), lanes(g)], carries[g]))
        return tuple(out)

    carries = lax.fori_loop(0, qi, lambda j, c: slc_tile(j, c, False),
                            tuple(_softmax_init(R, LANE) for _ in range(G)))
    carries = slc_tile(qi, carries, True)
    slc_acc = [acc for (_, acc) in carries]

    wk = WINDOW + T
    w0 = pl.multiple_of(jnp.maximum(q0 - WINDOW, 0), T)
    dist = qpos - (lax.broadcasted_iota(jnp.int32, (1, wk), 1) + w0)
    wbias = jnp.where((dist >= 0) & (dist < WINDOW), 0.0, NEG)
    o_win = []
    for g in range(G):
        sc = add_per_query(_dot_t(q4[g], kw_ref[0, pl.ds(w0, wk), lanes(g)]), wbias)
        e = jnp.exp2(sc - jnp.max(sc, -1, keepdims=True))
        o_win.append(_normalize(_dot(e.astype(BF), vw_ref[0, pl.ds(w0, wk), lanes(g)])))

    gw = NSA_HPG * LANE
    e_row = lax.broadcasted_iota(jnp.int32, (LANE, 3 * gw), 0)
    e_col = lax.broadcasted_iota(jnp.int32, (LANE, 3 * gw), 1)
    branch = jnp.where(e_col >= 2 * gw, 2, jnp.where(e_col >= gw, 1, 0))
    head = lax.shift_right_logical(e_col - branch * gw, LANE.bit_length() - 1)
    gate_col = 3 * head + branch

    def heads_on_lanes(x):
        return jnp.concatenate([x[h * T:(h + 1) * T] for h in range(NSA_HPG)], axis=1)

    gs = g_ref[0]
    hi = gs.astype(BF)
    lo = (gs - hi.astype(F32)).astype(BF)
    outs = []
    for g in range(G):
        expand = jnp.where(e_row == gate_col + g * (3 * NSA_HPG), 1.0, 0.0).astype(BF)
        gx = _dot(hi, expand) + _dot(lo, expand)
        mixed = (gx[:, 0:gw] * heads_on_lanes(o_cmp[g])
                 + gx[:, gw:2 * gw] * heads_on_lanes(_normalize(slc_acc[g]))
                 + gx[:, 2 * gw:3 * gw] * heads_on_lanes(o_win[g]))
        outs.append(_pack_head_pairs([mixed[:, h * LANE:(h + 1) * LANE] for h in range(NSA_HPG)]))
    o_ref[0] = jnp.concatenate(outs, axis=1).astype(o_ref.dtype)


def _nsa_attention(q, kcmp, vcmp, ks, vs, kw, vw, gates, ov, *, tq, top_k):
    bsz, seq, wide = q.shape
    n16 = kcmp.shape[1]
    gw = NSA_GROUPS * LANE
    assert seq % tq == 0 and seq >= WINDOW + tq
    kern = functools.partial(_nsa_attn_kernel, top_k=top_k)
    cblk = pl.BlockSpec((1, n16, gw), lambda b, i: (b, 0, 0))
    sblk = pl.BlockSpec((1, seq, gw), lambda b, i: (b, 0, 0))
    return pl.pallas_call(
        kern,
        grid=(bsz, seq // tq),
        in_specs=[pl.BlockSpec((1, tq, wide), lambda b, i: (b, i, 0)), cblk, cblk, sblk, sblk, sblk, sblk,
                  pl.BlockSpec((1, tq, LANE), lambda b, i: (b, i, 0)), _resident(ov.shape)],
        out_specs=pl.BlockSpec((1, tq, wide // 2), lambda b, i: (b, i, 0)),
        out_shape=jax.ShapeDtypeStruct((bsz, seq, wide // 2), BF),
        compiler_params=_params(("parallel", "arbitrary")),
        name="nsa_attn",
    )(q, kcmp, vcmp, ks, vs, kw, vw, gates, ov)


def _merge_kernel(x_ref, ab_ref, oc_ref, od_ref, wg_ref, bg_ref, wc_ref, wd_ref, wo_ref, g_ref, b_ref, o_ref):
    x = x_ref[...]
    gates = _dot(x.astype(BF), wg_ref[...]) + bg_ref[...]
    yc = _dot(oc_ref[...], wc_ref[...])
    yd = _dot(od_ref[...], wd_ref[...])
    merged = ab_ref[...] + jax.nn.sigmoid(gates[:, :D_MODEL]) * yc + jax.nn.sigmoid(gates[:, D_MODEL:]) * yd
    mix = _dot(merged.astype(BF), wo_ref[...])
    o_ref[...] = _ln(ALPHA * x + mix, g_ref[...], b_ref[...])


def _merge_ln(x, ab, oc, od, wg, bg, wc, wd, wo, g, b, *, layer, tm):
    n, d = x.shape
    row = pl.BlockSpec((tm, d), lambda i: (i, 0))
    rows = lambda a: pl.BlockSpec((tm, a.shape[1]), lambda i: (i, 0))
    return pl.pallas_call(
        _merge_kernel,
        grid=(n // tm,),
        in_specs=[row, row, rows(oc), rows(od)] + [_layered(a, layer) for a in (wg, bg, wc, wd, wo, g, b)],
        out_specs=row,
        out_shape=jax.ShapeDtypeStruct((n, d), F32),
        compiler_params=_params(("parallel",)),
        name="merge_ln",
    )(x, ab, oc, od, wg, bg, wc, wd, wo, g, b)


def _linear_kernel(x_ref, w_ref, o_ref):
    o_ref[...] = _dot(x_ref[...].astype(BF), w_ref[...]).astype(o_ref.dtype)


def _linear(x, w, *, layer, tm, dtype):
    n, d = x.shape
    return pl.pallas_call(
        _linear_kernel,
        grid=(n // tm,),
        in_specs=[pl.BlockSpec((tm, d), lambda i: (i, 0)), _layered(w, layer)],
        out_specs=pl.BlockSpec((tm, w.shape[2]), lambda i: (i, 0)),
        out_shape=jax.ShapeDtypeStruct((n, w.shape[2]), dtype),
        compiler_params=_params(("parallel",)),
        name="mem_kv",
    )(x, w)


def _xattn_kernel(x_ref, k_ref, v_ref, wq_ref, wo_ref, g_ref, b_ref, o_ref):
    x = x_ref[...]
    q = _dot(x.astype(BF), wq_ref[...]).astype(BF)
    k = k_ref[0]
    v = v_ref[0]
    heads = []
    for h in range(XATTN_HEADS):
        sl = slice(h * XATTN_DIM, (h + 1) * XATTN_DIM)
        s = _dot_t(q[:, sl], k[:, sl]) * (XATTN_DIM ** -0.5)
        e = jnp.exp(s - jnp.max(s, -1, keepdims=True))
        p = e / jnp.sum(e, -1, keepdims=True)
        heads.append(_dot(p.astype(BF), v[:, sl]))
    o = jnp.concatenate(heads, axis=1).astype(BF)
    o_ref[...] = _ln(ALPHA * x + _dot(o, wo_ref[...]), g_ref[...], b_ref[...])


def _xattn_ln(x, kv, wq, wo, g, b, *, layer, tm, seq):
    n, d = x.shape
    tps = seq // tm
    mlen = kv.shape[1]
    hd = XATTN_HEADS * XATTN_DIM
    return pl.pallas_call(
        _xattn_kernel,
        grid=(n // tm,),
        in_specs=[pl.BlockSpec((tm, d), lambda i: (i, 0)),
                  pl.BlockSpec((1, mlen, hd), lambda i: (i // tps, 0, 0)),
                  pl.BlockSpec((1, mlen, hd), lambda i: (i // tps, 0, 1))]
        + [_layered(a, layer) for a in (wq, wo, g, b)],
        out_specs=pl.BlockSpec((tm, d), lambda i: (i, 0)),
        out_shape=jax.ShapeDtypeStruct((n, d), F32),
        compiler_params=_params(("parallel",)),
        name="xattn_ln",
    )(x, kv, kv, wq, wo, g, b)


def _rope_tab(pos, dim):
    inv = ROPE_THETA ** (-(jnp.arange(0, dim, 2, dtype=F32) / dim))
    ang = pos[:, None] * inv[None, :]
    return jnp.cos(ang), jnp.sin(ang)


def _rot_cols(w, half):
    return jnp.concatenate([-w[..., half:2 * half], w[..., :half]], axis=-1)


def _pad_slots(w, n_slots, width):
    lead = w.shape[:-1]
    w = w.reshape(lead + (n_slots, width))
    w = jnp.pad(w, [(0, 0)] * len(lead) + [(0, 0), (0, LANE - width)])
    return w.reshape(lead + (n_slots * LANE,))


def _layer_params(p):
    w_in, b_in = p["w_in"], p["b_in"]

    def cols(o, wd):
        return w_in[:, o:o + wd], b_in[o:o + wd]

    out = {}
    out["w_ab"] = jnp.concatenate([w_in[:, 0:2560], w_in[:, _O_GA:_O_GC]], axis=1).astype(BF)
    out["b_ab"] = jnp.concatenate([b_in[0:2560], b_in[_O_GA:_O_GC]])[None, :]
    wkr, bkr = cols(_O_KROPE, MLA_ROPE)
    padk =lambda a: jnp.pad(a, [(0, 0)] * (a.ndim - 1) + [(0, LANE - MLA_ROPE)])
    out["w_c"] = jnp.concatenate([w_in[:, _O_QLAT:_O_KROPE], padk(wkr)], axis=1).astype(BF)
    out["b_c"] = jnp.concatenate([b_in[_O_QLAT:_O_KROPE], padk(bkr)])[None, :]
    wuq = p["mla_wuq"].reshape(MLA_Q_RANK, MLA_HEADS, MLA_NOPE + MLA_ROPE)
    wq_c = jnp.pad(wuq, [(0, 0), (0, 0), (0, LANE - MLA_NOPE - MLA_ROPE)])
    out["wq_c"] = wq_c.reshape(MLA_Q_RANK, MLA_HEADS * LANE).astype(BF)
    wukv = p["mla_wukv"].reshape(MLA_KV_RANK, MLA_HEADS, MLA_NOPE + MLA_V)
    out["wk_c"] = jnp.pad(wukv[..., :MLA_NOPE], [(0, 0), (0, 0), (0, LANE - MLA_NOPE)]).reshape(MLA_KV_RANK, -1).astype(BF)
    out["wv_c"] = jnp.pad(wukv[..., MLA_NOPE:], [(0, 0), (0, 0), (0, LANE - MLA_V)]).reshape(MLA_KV_RANK, -1).astype(BF)
    hd = NSA_DIM // 2
    n_gate = NSA_HEADS * 3
    out["w_d"] = jnp.pad(w_in[:, _O_NQ:_O_NGATE + n_gate], [(0, 0), (0, LANE - n_gate)]).astype(BF)
    out["b_d"] = jnp.pad(b_in[_O_NQ:_O_NGATE + n_gate], [(0, LANE - n_gate)])[None, :]

    def cmp_weights(w):
        eye = jnp.eye(NSA_GROUPS, dtype=F32)
        wp = jnp.pad(w, [(0, 0), (0, 0), (0, LANE - NSA_DIM)])
        full = jnp.einsum("lde,gh->lgdhe", wp, eye).reshape(CMP_BLOCK, NSA_GROUPS * NSA_DIM, NSA_GROUPS * LANE)
        return full.reshape(2, CMP_STRIDE * NSA_GROUPS * NSA_DIM, NSA_GROUPS * LANE).astype(BF)

    wck = p["nsa_wcmp_k"]
    out["wcmp_k"] = cmp_weights(wck)
    out["wcmp_kr"] = cmp_weights(_rot_cols(wck, hd))
    out["wcmp_v"] = cmp_weights(p["nsa_wcmp_v"])

    def pe_rows(pe):
        t = jnp.broadcast_to(pe[:, None, :], (CMP_BLOCK, NSA_GROUPS, NSA_DIM))
        return t.reshape(2, CMP_STRIDE * NSA_GROUPS * NSA_DIM)

    out["pe"] = jnp.concatenate([pe_rows(p["nsa_pe_k"]), pe_rows(p["nsa_pe_v"])], axis=0)
    out["w_g"] = w_in[:, _O_GC:].astype(BF)
    out["b_g"] = b_in[_O_GC:][None, :]
    out["wout_c"] = p["mla_wout"].astype(BF)
    out["wout_d"] = p["nsa_wout"].astype(BF)
    out["gmlp_bs_t"] = p["gmlp_bs"].T
    for name in ("gmlp_wout", "conv_wout", "w_o", "xattn_wq", "xattn_wo"):
        out[name] = p[name].astype(BF)
    out["xattn_wkv"] = jnp.concatenate([p["xattn_wk"], p["xattn_wv"]], axis=1).astype(BF)
    return out


def _tables(seq):
    pos = jnp.arange(seq, dtype=F32)
    c16, s16 = _rope_tab(pos, MLA_ROPE)
    one = jnp.ones((seq, MLA_NOPE), F32)
    zero = jnp.zeros((seq, MLA_NOPE), F32)
    tail = LANE - MLA_NOPE - MLA_ROPE
    cq = jnp.concatenate([one, c16, c16, jnp.ones((seq, tail), F32)], axis=1)
    sq = jnp.concatenate([zero, s16, s16, jnp.zeros((seq, tail), F32)], axis=1)
    ck = jnp.pad(jnp.concatenate([c16, c16], axis=1), [(0, 0), (0, LANE - MLA_ROPE)])
    sk = jnp.pad(jnp.concatenate([s16, s16], axis=1), [(0, 0), (0, LANE - MLA_ROPE)])
    c32, s32 = _rope_tab(pos, NSA_DIM)
    cn = jnp.concatenate([c32, c32] * (LANE // NSA_DIM), axis=1)
    sn = jnp.concatenate([s32, s32] * (LANE // NSA_DIM), axis=1)
    n16 = seq // CMP_STRIDE
    cend = (jnp.arange(n16) * CMP_STRIDE + CMP_BLOCK - 1).astype(F32)
    cc32, cs32 = _rope_tab(cend, NSA_DIM)
    ccg = jnp.pad(jnp.concatenate([cc32, cc32], axis=1), [(0, 0), (0, LANE - NSA_DIM)])
    csg = jnp.pad(jnp.concatenate([cs32, cs32], axis=1), [(0, 0), (0, LANE - NSA_DIM)])
    cc = jnp.concatenate([ccg] * NSA_GROUPS, axis=1)
    cs = jnp.concatenate([csg] * NSA_GROUPS, axis=1)
    n_cmp = (seq - CMP_BLOCK) // CMP_STRIDE + 1
    n_slc = seq // SLC_BLOCK
    cstart = jnp.arange(n16) * CMP_STRIDE
    sstart = jnp.arange(n_slc) * SLC_BLOCK
    ovl = (jnp.minimum(cstart[None, :] + CMP_BLOCK, sstart[:, None] + SLC_BLOCK)
           - jnp.maximum(cstart[None, :], sstart[:, None]))
    ovl = jnp.clip(ovl, 0).astype(F32) / CMP_BLOCK
    ovl = jnp.where(jnp.arange(n16)[None, :] < n_cmp, ovl, 0.0).astype(BF)
    return dict(cq=cq, sq=sq, ck=ck, sk=sk, cn=cn, sn=sn, cc=cc, cs=cs, ovl=ovl)


def kernel(x, mem, ffn1_w1, ffn1_w3, ffn1_w2, ln1_g, ln1_b, w_in, b_in, gmlp_ln_g, gmlp_ln_b, gmlp_ws, gmlp_bs, gmlp_wout, conv_w, conv_wout, mla_qnorm_g, mla_kvnorm_g, mla_wuq, mla_wukv, mla_wout, nsa_pe_k, nsa_pe_v, nsa_wcmp_k, nsa_wcmp_v, nsa_wout, w_o, ln2_g, ln2_b, xattn_wq, xattn_wk, xattn_wv, xattn_wo, ln3_g, ln3_b, ffn2_w1, ffn2_w3, ffn2_w2, ln4_g, ln4_b):
    bsz, seq, d = x.shape
    mlen = mem.shape[1]
    n = bsz * seq
    assert d == D_MODEL and seq % 1024 == 0
    lp = jax.vmap(_layer_params)(dict(
        w_in=w_in, b_in=b_in, mla_wuq=mla_wuq, mla_wukv=mla_wukv, mla_wout=mla_wout, nsa_pe_k=nsa_pe_k,
        nsa_pe_v=nsa_pe_v, nsa_wcmp_k=nsa_wcmp_k, nsa_wcmp_v=nsa_wcmp_v, nsa_wout=nsa_wout, gmlp_bs=gmlp_bs,
        gmlp_wout=gmlp_wout, conv_wout=conv_wout, w_o=w_o, xattn_wq=xattn_wq, xattn_wk=xattn_wk,
        xattn_wv=xattn_wv, xattn_wo=xattn_wo))
    tb = _tables(seq)
    tm = 512
    tm_ffn = 1024
    tf = D_FF // 11
    top_k = min(SLC_TOPK, seq // SLC_BLOCK)
    row = lambda a: a[:, None, :]

    def cast_stacked(w):
        return _to_bf16(w.reshape(w.shape[0] * w.shape[1], w.shape[2])).reshape(w.shape)

    ffn1 = [cast_stacked(w) for w in (ffn1_w1, ffn1_w3, ffn1_w2)]
    ffn2 = [cast_stacked(w) for w in (ffn2_w1, ffn2_w3, ffn2_w2)]
    h = x.reshape(n, d)
    mem2 = mem.reshape(bsz * mlen, d)
    wide = MLA_HEADS * LANE
    gw = NSA_GROUPS * LANE
    for l in range(DEPTH):
        h = _ffn_ln(h, *ffn1, row(ln1_g), row(ln1_b), layer=l, tm=tm_ffn, tf=tf)
        ab = _mix_ab(h, lp["w_ab"], lp["b_ab"], row(gmlp_ln_g), row(gmlp_ln_b), gmlp_ws, lp["gmlp_bs_t"],
                     lp["gmlp_wout"], conv_w, lp["conv_wout"], layer=l, tm=tm, seq=seq)
        qc, kc_, vc_ = _mla_proj(h, lp["w_c"], lp["b_c"], row(mla_qnorm_g), row(mla_kvnorm_g),
                                 lp["wq_c"], lp["wk_c"], lp["wv_c"],
                                 tb["cq"], tb["sq"], tb["ck"], tb["sk"], layer=l, tm=tm, seq=seq)
        oc = _flash_causal(qc.reshape(bsz, seq, wide), kc_.reshape(bsz, seq, wide), vc_.reshape(bsz, seq, wide), tq=1024, hp=4)
        qn, nkc, nvc, nks, nvs, nkw, nvw, gates = _nsa_proj(h, lp["w_d"], lp["b_d"], tb["cn"], tb["sn"],
                                                            layer=l, tm=tm, seq=seq)
        kcmp, vcmp = _nsa_compress(nkc.reshape(bsz, seq, NSA_GROUPS * NSA_DIM), nvc.reshape(bsz, seq, NSA_GROUPS * NSA_DIM),
                                   lp["pe"], lp["wcmp_k"], lp["wcmp_kr"], lp["wcmp_v"], tb["cc"], tb["cs"], layer=l)
        od = _nsa_attention(qn.reshape(bsz, seq, NSA_HEADS * LANE), kcmp, vcmp,
                            nks.reshape(bsz, seq, gw), nvs.reshape(bsz, seq, gw),
                            nkw.reshape(bsz, seq, gw), nvw.reshape(bsz, seq, gw),
                            gates.reshape(bsz, seq, LANE), tb["ovl"], tq=512, top_k=top_k)
        h = _merge_ln(h, ab, oc.reshape(n, MLA_HEADS * MLA_V), od.reshape(n, NSA_HEADS * NSA_DIM), lp["w_g"], lp["b_g"],
                      lp["wout_c"], lp["wout_d"], lp["w_o"], row(ln2_g), row(ln2_b), layer=l, tm=tm)
        kv = _linear(mem2, lp["xattn_wkv"], layer=l, tm=min(256, bsz * mlen), dtype=BF)
        h = _xattn_ln(h, kv.reshape(bsz, mlen, 2 * XATTN_HEADS * XATTN_DIM), lp["xattn_wq"], lp["xattn_wo"],
                      row(ln3_g), row(ln3_b), layer=l, tm=tm, seq=seq)
        h = _ffn_ln(h, *ffn2, row(ln4_g), row(ln4_b), layer=l, tm=tm_ffn, tf=tf)
    return h.reshape(bsz, seq, d)
```

```python
import functools

import jax
import jax.numpy as jnp
from jax import lax
from jax.experimental import pallas as pl
from jax.experimental.pallas import tpu as pltpu

BF = jnp.bfloat16
F32 = jnp.float32

D_MODEL = 1024
D_FF = 2816
LN_EPS = 1e-5
RMS_EPS = 1e-6
ROPE_THETA = 10000.0
DEPTH = 2
ALPHA = (2 * DEPTH) ** 0.25
NEG = -1e30
LOG2_E = 1.4426950408889634
DENOM_LANE = 64
MASK_BIG = 2.0 ** 100

GMLP_CHUNK = 128
GMLP_GROUPS = 4
GMLP_WIDTH = 512
CONV_WIDTH = 512
CONV_K = 3
MLA_HEADS = 8
MLA_Q_RANK = 256
MLA_KV_RANK = 128
MLA_NOPE = 64
MLA_ROPE = 32
MLA_V = 64
NSA_HEADS = 8
NSA_GROUPS = 2
NSA_HPG = 4
NSA_DIM = 64
CMP_BLOCK = 32
CMP_STRIDE = 16
SLC_BLOCK = 64
SLC_SHIFT = SLC_BLOCK.bit_length() - 1
SLC_TOPK = 8
WINDOW = 512
XATTN_HEADS = 4
XATTN_DIM = 128

LANE = 128
CONV_HALO = 8
VMEM_LIMIT = 56 * 1024 * 1024

_O_U, _O_V, _O_CB, _O_CC, _O_CH = 0, 512, 1024, 1536, 2048
_O_QLAT, _O_KVLAT, _O_KROPE = 2560, 2816, 2944
_O_NQ, _O_NKC, _O_NVC, _O_NKS, _O_NVS, _O_NKW, _O_NVW, _O_NGATE = 2976, 3488, 3616, 3744, 3872, 4000, 4128, 4256
_O_GA, _O_GB, _O_GC, _O_GD = 4280, 5304, 6328, 7352


def _dot(a, b):
    return jnp.dot(a, b, preferred_element_type=F32)


def _dot_t(a, b):
    return lax.dot_general(a, b, (((1,), (1,)), ((), ())), preferred_element_type=F32)


def _ln(y, g, b):
    mu = jnp.mean(y, -1, keepdims=True)
    d = y - mu
    var = jnp.mean(d * d, -1, keepdims=True)
    return d * lax.rsqrt(var + LN_EPS) * g + b


def _rms(x, g):
    return x * lax.rsqrt(jnp.mean(x * x, -1, keepdims=True) + RMS_EPS) * g


def _resident(shape):
    n = len(shape)
    return pl.BlockSpec(shape, lambda *_: (0,) * n, pipeline_mode=pl.Buffered(1))


def _layered(a, layer):
    n = a.ndim - 1
    return pl.BlockSpec((None,) + a.shape[1:], lambda *_: (layer,) + (0,) * n, pipeline_mode=pl.Buffered(1))


def _params(sem):
    return pltpu.CompilerParams(dimension_semantics=sem, vmem_limit_bytes=VMEM_LIMIT)


def _cast_kernel(x_ref, o_ref):
    o_ref[...] = x_ref[...].astype(o_ref.dtype)


def _to_bf16(w, *, rows=512):
    r, c = w.shape
    rows = min(rows, r)
    assert r % rows == 0
    return pl.pallas_call(
        _cast_kernel,
        grid=(r // rows,),
        in_specs=[pl.BlockSpec((rows, c), lambda i: (i, 0))],
        out_specs=pl.BlockSpec((rows, c), lambda i: (i, 0)),
        out_shape=jax.ShapeDtypeStruct((r, c), BF),
        compiler_params=_params(("parallel",)),
        name="to_bf16",
    )(w)


def _ffn_ln_kernel(x_ref, w1_ref, w3_ref, w2_ref, g_ref, b_ref, o_ref, *, tf):
    x = x_ref[...]
    xb = x.astype(BF)
    acc = None
    for c in range(w1_ref.shape[1] // tf):
        cols = slice(c * tf, (c + 1) * tf)
        h1 = _dot(xb, w1_ref[:, cols])
        h3 = _dot(xb, w3_ref[:, cols])
        hh = (h1 * jax.nn.sigmoid(h1)) * h3
        part = _dot(hh.astype(BF), w2_ref[cols, :])
        acc = part if acc is None else acc + part
    o_ref[...] = _ln(ALPHA * x + 0.5 * acc, g_ref[...], b_ref[...])


def _ffn_ln(x, w1, w3, w2, g, b, *, layer, tm, tf):
    n, d = x.shape
    return pl.pallas_call(
        functools.partial(_ffn_ln_kernel, tf=tf),
        grid=(n // tm,),
        in_specs=[pl.BlockSpec((tm, d), lambda i: (i, 0))] + [_layered(a, layer) for a in (w1, w3, w2, g, b)],
        out_specs=pl.BlockSpec((tm, d), lambda i: (i, 0)),
        out_shape=jax.ShapeDtypeStruct((n, d), F32),
        compiler_params=_params(("parallel",)),
        name="ffn_ln",
    )(x, w1, w3, w2, g, b)


def _ab_kernel(h_ref, w_ref, b_ref, lng_ref, lnb_ref, ws_ref, bst_ref, wga_ref, cw_ref, wcb_ref,
               o_ref, prev_ref, *, tiles_per_seq):
    i = pl.program_id(0)
    tm = h_ref.shape[0]
    hb = h_ref[...].astype(BF)

    def proj(c0, width):
        return _dot(hb, w_ref[:, c0:c0 + width]) + b_ref[:, c0:c0 + width]

    u = proj(0, GMLP_WIDTH)
    v = _ln(proj(512, GMLP_WIDTH), lng_ref[...], lnb_ref[...]).astype(BF)
    row = lax.broadcasted_iota(jnp.int32, (GMLP_CHUNK, GMLP_CHUNK), 0)
    col = lax.broadcasted_iota(jnp.int32, (GMLP_CHUNK, GMLP_CHUNK), 1)
    gd = GMLP_WIDTH // GMLP_GROUPS
    wgs = [jnp.where(row >= col, ws_ref[g], 0.0).astype(BF) for g in range(GMLP_GROUPS)]
    chunks = []
    for c in range(tm // GMLP_CHUNK):
        r0 = c * GMLP_CHUNK
        chunks.append(jnp.concatenate(
            [_dot(wgs[g], v[r0:r0 + GMLP_CHUNK, g * gd:(g + 1) * gd]) + bst_ref[:, g:g + 1]
             for g in range(GMLP_GROUPS)], axis=1))
    s = jnp.concatenate(chunks, axis=0)
    ya = _dot((u * s).astype(BF), wga_ref[...])

    cb = proj(1024, CONV_WIDTH)
    z = proj(1536, CONV_WIDTH) * proj(2048, CONV_WIDTH)

    @pl.when(i % tiles_per_seq == 0)
    def _():
        prev_ref[...] = jnp.zeros_like(prev_ref)

    zext = jnp.concatenate([prev_ref[...], z], axis=0)
    z1 = pltpu.roll(zext, 1, 0)[CONV_HALO:]
    z2 = pltpu.roll(zext, 2, 0)[CONV_HALO:]
    y = cw_ref[0:1, :] * z2 + cw_ref[1:2, :] * z1 + cw_ref[2:3, :] * z
    prev_ref[...] = z[tm - CONV_HALO:, :]
    yb = _dot((cb * y).astype(BF), wcb_ref[...])

    ga = proj(2560, D_MODEL)
    gb = proj(3584, D_MODEL)
    o_ref[...] = jax.nn.sigmoid(ga) * ya + jax.nn.sigmoid(gb) * yb


def _mix_ab(h, w, b, lng, lnb, ws, bst, wga, cw, wcb, *, layer, tm, seq):
    n, d = h.shape
    kern = functools.partial(_ab_kernel, tiles_per_seq=seq // tm)
    return pl.pallas_call(
        kern,
        grid=(n // tm,),
        in_specs=[pl.BlockSpec((tm, d), lambda i: (i, 0))]
        + [_layered(a, layer) for a in (w, b, lng, lnb, ws, bst, wga, cw, wcb)],
        out_specs=pl.BlockSpec((tm, d), lambda i: (i, 0)),
        out_shape=jax.ShapeDtypeStruct((n, d), F32),
        scratch_shapes=[pltpu.VMEM((CONV_HALO, CONV_WIDTH), F32)],
        compiler_params=_params(("arbitrary",)),
        name="mix_ab",
    )(h, w, b, lng, lnb, ws, bst, wga, cw, wcb)


def _mla_proj_kernel(h_ref, w_ref, b_ref, qg_ref, kvg_ref, wq_ref, wk_ref, wv_ref,
                     cq_ref, sq_ref, ck_ref, sk_ref, q_ref, k_ref, v_ref):
    hb = h_ref[...].astype(BF)
    z = _dot(hb, w_ref[...]) + b_ref[...]
    qn = _rms(z[:, 0:256], qg_ref[...]).astype(BF)
    kvn = _rms(z[:, 256:384], kvg_ref[...]).astype(BF)
    half = MLA_ROPE // 2

    def rotate_half(x, start):
        w = x.shape[1]
        first = lax.broadcasted_iota(jnp.int32, (1, w), 1) % LANE < start + half
        return jnp.where(first, -pltpu.roll(x, w - half, 1), pltpu.roll(x, half, 1))

    cq = jnp.concatenate([cq_ref[...]] * MLA_HEADS, axis=1)
    sq = jnp.concatenate([sq_ref[...]] * MLA_HEADS, axis=1)
    scale = (MLA_NOPE + MLA_ROPE) ** -0.5 * LOG2_E
    q = _dot(qn, wq_ref[...])
    q_ref[...] = ((q * cq + rotate_half(q, MLA_NOPE) * sq) * scale).astype(BF)
    kr = z[:, 384:512]
    kpe = pltpu.roll(kr * ck_ref[...] + rotate_half(kr, 0) * sk_ref[...], MLA_NOPE, 1)
    k_ref[...] = (_dot(kvn, wk_ref[...]) + jnp.concatenate([kpe] * MLA_HEADS, axis=1)).astype(BF)
    v_ref[...] = (_dot(kvn, wv_ref[...]) + _denom_ones(v_ref.shape[1])).astype(BF)


def _mla_proj(h, w, b, qg, kvg, wq, wk, wv, cq, sq, ck, sk, *, layer, tm, seq):
    n, d = h.shape
    tps = seq // tm
    tab = pl.BlockSpec((tm, LANE), lambda i: (i % tps, 0))
    wide = MLA_HEADS * LANE
    out = jax.ShapeDtypeStruct((n, wide), BF)
    return pl.pallas_call(
        _mla_proj_kernel,
        grid=(n // tm,),
        in_specs=[pl.BlockSpec((tm, d), lambda i: (i, 0))]
        + [_layered(a, layer) for a in (w, b, qg, kvg, wq, wk, wv)] + [tab] * 4,
        out_specs=[pl.BlockSpec((tm, wide), lambda i: (i, 0))] * 3,
        out_shape=[out, out, out],
        compiler_params=_params(("parallel",)),
        name="mla_proj",
    )(h, w, b, qg, kvg, wq, wk, wv, cq, sq, ck, sk)


def _online_softmax_step(s, v, carry):
    m, acc = carry
    m_new = jnp.maximum(m, jnp.max(s, -1, keepdims=True))
    p = jnp.exp2(s - m_new)
    acc = jnp.exp2(m - m_new) * acc + _dot(p.astype(BF), v)
    return m_new, acc


def _softmax_init(rows, width):
    return (jnp.full((rows, 1), NEG, F32), jnp.zeros((rows, width), F32))


def _normalize(acc):
    return acc * (1.0 / acc[:, DENOM_LANE:DENOM_LANE + 1])


def _pack_head_pairs(slots):
    low = lax.broadcasted_iota(jnp.int32, (1, LANE), 1) < DENOM_LANE
    return jnp.concatenate([jnp.where(low, a, pltpu.roll(b, DENOM_LANE, 1))
                            for a, b in zip(slots[0::2], slots[1::2])], axis=1)


def _denom_ones(width):
    lane = lax.broadcasted_iota(jnp.int32, (1, width), 1)
    return jnp.where(lane % LANE == DENOM_LANE, 1.0, 0.0)


def _flash_kernel(q_ref, k_ref, v_ref, o_ref, *, tq, hp):
    qi = pl.program_id(2)
    q0 = qi * tq
    qs = [q_ref[0, :, h * LANE:(h + 1) * LANE] for h in range(hp)]

    def tile(j, carries, width, diagonal):
        k0 = pl.multiple_of(j * width, width)
        out = []
        for h in range(hp):
            s = _dot_t(qs[h], k_ref[0, pl.ds(k0, width), h * LANE:(h + 1) * LANE])
            if diagonal:
                r = lax.broadcasted_iota(jnp.int32, (tq, width), 0)
                c = lax.broadcasted_iota(jnp.int32, (tq, width), 1)
                s = jnp.where(c <= r, s, NEG)
            out.append(_online_softmax_step(s, v_ref[0, pl.ds(k0, width), h * LANE:(h + 1) * LANE], carries[h]))
        return tuple(out)

    init = tuple(_softmax_init(tq, LANE) for _ in range(hp))
    carries = lax.fori_loop(0, qi, lambda j, c: tile(j, c, tq, False), init)

    half = tq // 2
    carries = tile(2 * qi, carries, half, True)
    r = lax.broadcasted_iota(jnp.int32, (half, half), 0)
    c = lax.broadcasted_iota(jnp.int32, (half, half), 1)
    k1 = pl.multiple_of(q0 + half, half)
    out = []
    for h in range(hp):
        m, acc = carries[h]
        s = _dot_t(qs[h][half:], k_ref[0, pl.ds(k1, half), h * LANE:(h + 1) * LANE])
        m2, acc2 = _online_softmax_step(jnp.where(c <= r, s, NEG), v_ref[0, pl.ds(k1, half), h * LANE:(h + 1) * LANE],
                                        (m[half:], acc[half:]))
        out.append(jnp.concatenate([acc[:half], acc2], axis=0))
    o_ref[0] = _pack_head_pairs([_normalize(acc) for acc in out]).astype(o_ref.dtype)


def _flash_causal(q, k, v, *, tq, hp):
    bsz, seq, wide = q.shape
    heads = wide // LANE
    assert seq % tq == 0 and hp % 2 == 0
    half_lane = LANE // 2
    kern = functools.partial(_flash_kernel, tq=tq, hp=hp)
    return pl.pallas_call(
        kern,
        grid=(bsz, heads // hp, seq // tq),
        in_specs=[
            pl.BlockSpec((1, tq, hp * LANE), lambda b, h, i: (b, i, h)),
            pl.BlockSpec((1, seq, hp * LANE), lambda b, h, i: (b, 0, h)),
            pl.BlockSpec((1, seq, hp * LANE), lambda b, h, i: (b, 0, h)),
        ],
        out_specs=pl.BlockSpec((1, tq, hp * half_lane), lambda b, h, i: (b, i, h)),
        out_shape=jax.ShapeDtypeStruct((bsz, seq, heads * half_lane), BF),
        compiler_params=_params(("parallel", "parallel", "arbitrary")),
        name="mla_flash",
    )(q, k, v)


def _nsa_proj_kernel(h_ref, w_ref, b_ref, c_ref, s_ref, q_ref, kc_ref, vc_ref, ks_ref, vs_ref, kw_ref, vw_ref, g_ref,
                     *, tiles_per_seq):
    tm = h_ref.shape[0]
    hb = h_ref[...].astype(BF)
    pos = (pl.program_id(0) % tiles_per_seq) * tm + lax.broadcasted_iota(jnp.int32, (tm, LANE), 0)
    lane = lax.broadcasted_iota(jnp.int32, (tm, LANE), 1)
    tag = jnp.where(lane == NSA_DIM + lax.shift_right_logical(pos, SLC_SHIFT), MASK_BIG, 0.0)
    tag2 = jnp.concatenate([tag] * NSA_GROUPS, axis=1)
    c = c_ref[...]
    s = s_ref[...]
    half = NSA_DIM // 2

    def rope(x):
        w = x.shape[1]
        first = lax.broadcasted_iota(jnp.int32, (1, w), 1) % NSA_DIM < half
        rot = jnp.where(first, -pltpu.roll(x, w - half, 1), pltpu.roll(x, half, 1))
        reps = w // LANE
        return x * jnp.concatenate([c] * reps, axis=1) + rot * jnp.concatenate([s] * reps, axis=1)

    low = lax.broadcasted_iota(jnp.int32, (1, LANE), 1) < NSA_DIM

    def spread(x):
        out = []
        for j in range(x.shape[1] // LANE):
            blk = x[:, j * LANE:(j + 1) * LANE]
            out += [jnp.where(low, blk, 0.0), jnp.where(low, pltpu.roll(blk, NSA_DIM, 1), 0.0)]
        return jnp.concatenate(out, axis=1)

    z = _dot(hb, w_ref[...]) + b_ref[...]
    q_ref[...] = (spread(rope(z[:, 0:512])) * (NSA_DIM ** -0.5 * LOG2_E)).astype(BF)
    kc_ref[...] = z[:, 512:640]
    vc_ref[...] = z[:, 640:768]
    ks_ref[...] = (spread(rope(z[:, 768:896])) + tag2).astype(BF)
    ones = _denom_ones(NSA_GROUPS * LANE)
    vs_ref[...] = (spread(z[:, 896:1024]) + ones).astype(BF)
    kw_ref[...] = spread(rope(z[:, 1024:1152])).astype(BF)
    vw_ref[...] = (spread(z[:, 1152:1280]) + ones).astype(BF)
    g_ref[...] = jax.nn.sigmoid(z[:, 1280:1408])


def _nsa_proj(h, w, b, cn, sn, *, layer, tm, seq):
    n, d = h.shape
    tps = seq // tm
    tab = pl.BlockSpec((tm, LANE), lambda i: (i % tps, 0))

    def out(width, dt):
        return pl.BlockSpec((tm, width), lambda i: (i, 0)), jax.ShapeDtypeStruct((n, width), dt)

    outs = [out(1024, BF), out(128, F32), out(128, F32), out(256, BF), out(256, BF), out(256, BF), out(256, BF), out(128, F32)]
    assert seq // SLC_BLOCK <= LANE - NSA_DIM
    return pl.pallas_call(
        functools.partial(_nsa_proj_kernel, tiles_per_seq=tps),
        grid=(n // tm,),
        in_specs=[pl.BlockSpec((tm, d), lambda i: (i, 0)), _layered(w, layer), _layered(b, layer), tab, tab],
        out_specs=[o[0] for o in outs],
        out_shape=[o[1] for o in outs],
        compiler_params=_params(("parallel",)),
        name="nsa_proj",
    )(h, w, b, cn, sn)


def _nsa_cmp_kernel(kc_ref, vc_ref, pe_ref, wk_ref, wkr_ref, wv_ref, c_ref, s_ref, kcmp_ref, vcmp_ref):
    n16 = kcmp_ref.shape[1]
    gl = NSA_GROUPS * NSA_DIM
    kc = kcr = vc = None
    for l in range(CMP_STRIDE):
        rows = slice(l * gl, (l + 1) * gl)
        xk = kc_ref[0, pl.ds(l, n16, stride=CMP_STRIDE), :]
        xv = vc_ref[0, pl.ds(l, n16, stride=CMP_STRIDE), :]
        terms = [((xk + pe_ref[0:1, rows]).astype(BF), (xv + pe_ref[2:3, rows]).astype(BF), 0),
                 ((pltpu.roll(xk, n16 - 1, 0) + pe_ref[1:2, rows]).astype(BF),
                  (pltpu.roll(xv, n16 - 1, 0) + pe_ref[3:4, rows]).astype(BF), 1)]
        for ak, av, part in terms:
            pk, pkr, pv = _dot(ak, wk_ref[part, rows, :]), _dot(ak, wkr_ref[part, rows, :]), _dot(av, wv_ref[part, rows, :])
            kc, kcr, vc = (pk, pkr, pv) if kc is None else (kc + pk, kcr + pkr, vc + pv)
    kcmp_ref[0] = (kc * c_ref[...] + kcr * s_ref[...]).astype(BF)
    vcmp_ref[0] = vc.astype(BF)


def _nsa_compress(kc, vc, pe, wk, wkr, wv, cc, sc, *, layer):
    bsz, seq, wide = kc.shape
    n16 = seq // CMP_STRIDE
    blk = pl.BlockSpec((1, seq, wide), lambda b: (b, 0, 0))
    oblk = pl.BlockSpec((1, n16, NSA_GROUPS * LANE), lambda b: (b, 0, 0))
    osh = jax.ShapeDtypeStruct((bsz, n16, NSA_GROUPS * LANE), BF)
    return pl.pallas_call(
        _nsa_cmp_kernel,
        grid=(bsz,),
        in_specs=[blk, blk] + [_layered(a, layer) for a in (pe, wk, wkr, wv)] + [_resident(cc.shape), _resident(sc.shape)],
        out_specs=[oblk, oblk],
        out_shape=[osh, osh],
        compiler_params=_params(("parallel",)),
        name="nsa_compress",
    )(kc, vc, pe, wk, wkr, wv, cc, sc)


def _nsa_attn_kernel(q_ref, kcmp_ref, vcmp_ref, ks_ref, vs_ref, kw_ref, vw_ref, g_ref, ov_ref, o_ref, *, top_k):
    qi = pl.program_id(1)
    T = q_ref.shape[1]
    R = NSA_HPG * T
    G = NSA_GROUPS
    q0 = qi * T
    qpos = lax.broadcasted_iota(jnp.int32, (T, 1), 0) + q0
    ncp = kcmp_ref.shape[1]
    nb = ov_ref.shape[0]
    ov_t = ov_ref[...]

    def add_per_query(x, b):
        w = x.shape[1]
        return (x.reshape(NSA_HPG, T, w) + b[None]).reshape(R, w)

    def lanes(g):
        return slice(g * LANE, (g + 1) * LANE)

    q4 = [jnp.concatenate([q_ref[0, :, (g * NSA_HPG + h) * LANE:(g * NSA_HPG + h + 1) * LANE]
                           for h in range(NSA_HPG)], axis=0) for g in range(G)]

    cmp_end = lax.broadcasted_iota(jnp.int32, (1, ncp), 1) * CMP_STRIDE + (CMP_BLOCK - 1)
    cbias = jnp.where(cmp_end <= qpos, 0.0, NEG)
    any_valid = jnp.where(qpos >= CMP_BLOCK - 1, 1.0, 0.0)
    jr = lax.broadcasted_iota(jnp.int32, (nb, 1), 0)
    jrf = jr.astype(F32)
    jq = lax.shift_right_logical(lax.broadcasted_iota(jnp.int32, (1, T), 1) + q0, SLC_SHIFT)
    forced = (jr == 0) | (jr == jq) | (jr == jq - 1)
    eye_t = jnp.where(lax.broadcasted_iota(jnp.int32, (T, T), 0) == lax.broadcasted_iota(jnp.int32, (T, T), 1),
                      1.0, 0.0).astype(BF)
    o_cmp, q4s = [], []
    for g in range(G):
        sm = _dot_t(q4[g], kcmp_ref[0, :, lanes(g)]).reshape(NSA_HPG, T, ncp) + cbias[None]
        e = jnp.exp2(sm - jnp.max(sm, -1, keepdims=True))
        p = e * (any_valid[None] / jnp.sum(e, -1, keepdims=True))
        o_cmp.append(_dot(p.reshape(R, ncp).astype(BF), vcmp_ref[0, :, lanes(g)]))
        psum = p[0] + p[1] + p[2] + p[3]
        hi = psum.astype(BF)
        r1 = psum - hi.astype(F32)
        mid = r1.astype(BF)
        lo = (r1 - mid.astype(F32)).astype(BF)
        imp = _dot_t(ov_t, hi) + _dot_t(ov_t, mid) + _dot_t(ov_t, lo)
        imp = jnp.where(forced, 1e9, imp)
        imp = jnp.where(jr <= jq, imp, -1.0)
        work = imp
        sel = jnp.zeros_like(imp)
        for _ in range(top_k):
            mx = jnp.max(work, 0, keepdims=True)
            idx = jnp.min(jnp.where(work == mx, jrf, float(nb)), 0, keepdims=True)
            pick = jrf == idx
            sel = jnp.where(pick, 1.0, sel)
            work = jnp.where(pick, -2.0, work)
        unsel_t = jnp.where(imp >= 0.0, sel, 0.0) - 1.0
        pad_t = [jnp.zeros((NSA_DIM, T), F32), unsel_t]
        if LANE - NSA_DIM - nb:
            pad_t.append(jnp.zeros((LANE - NSA_DIM - nb, T), F32))
        unsel = _dot_t(eye_t, jnp.concatenate(pad_t, axis=0).astype(BF)).astype(BF)
        q4s.append(add_per_query(q4[g], unsel))

    def slc_tile(j, carries, diagonal):
        k0 = pl.multiple_of(j * T, T)
        out = []
        for g in range(G):
            sc = _dot_t(q4s[g], ks_ref[0, pl.ds(k0, T), lanes(g)])
            if diagonal:
                sc = add_per_query(sc, jnp.where((lax.broadcasted_iota(jnp.int32, (1, T), 1) + k0) <= qpos, 0.0, NEG))
            out.append(_online_softmax_step(sc, vs_ref[0, pl.ds(k0, T), lanes(g)], carries[g]))
        return tuple(out)

    carries = lax.fori_loop(0, qi, lambda j, c: slc_tile(j, c, False),
                            tuple(_softmax_init(R, LANE) for _ in range(G)))
    carries = slc_tile(qi, carries, True)
    slc_acc = [acc for (_, acc) in carries]

    wk = WINDOW + T
    w0 = pl.multiple_of(jnp.maximum(q0 - WINDOW, 0), T)
    dist = qpos - (lax.broadcasted_iota(jnp.int32, (1, wk), 1) + w0)
    wbias = jnp.where((dist >= 0) & (dist < WINDOW), 0.0, NEG)
    o_win = []
    for g in range(G):
        sc = add_per_query(_dot_t(q4[g], kw_ref[0, pl.ds(w0, wk), lanes(g)]), wbias)
        e = jnp.exp2(sc - jnp.max(sc, -1, keepdims=True))
        o_win.append(_normalize(_dot(e.astype(BF), vw_ref[0, pl.ds(w0, wk), lanes(g)])))

    gw = NSA_HPG * LANE
    e_row = lax.broadcasted_iota(jnp.int32, (LANE, 3 * gw), 0)
    e_col = lax.broadcasted_iota(jnp.int32, (LANE, 3 * gw), 1)
    branch = jnp.where(e_col >= 2 * gw, 2, jnp.where(e_col >= gw, 1, 0))
    head = lax.shift_right_logical(e_col - branch * gw, LANE.bit_length() - 1)
    gate_col = 3 * head + branch

    def heads_on_lanes(x):
        return jnp.concatenate([x[h * T:(h + 1) * T] for h in range(NSA_HPG)], axis=1)

    gs = g_ref[0]
    hi = gs.astype(BF)
    lo = (gs - hi.astype(F32)).astype(BF)
    outs = []
    for g in range(G):
        expand = jnp.where(e_row == gate_col + g * (3 * NSA_HPG), 1.0, 0.0).astype(BF)
        gx = _dot(hi, expand) + _dot(lo, expand)
        mixed = (gx[:, 0:gw] * heads_on_lanes(o_cmp[g])
                 + gx[:, gw:2 * gw] * heads_on_lanes(_normalize(slc_acc[g]))
                 + gx[:, 2 * gw:3 * gw] * heads_on_lanes(o_win[g]))
        outs.append(_pack_head_pairs([mixed[:, h * LANE:(h + 1) * LANE] for h in range(NSA_HPG)]))
    o_ref[0] = jnp.concatenate(outs, axis=1).astype(o_ref.dtype)


def _nsa_attention(q, kcmp, vcmp, ks, vs, kw, vw, gates, ov, *, tq, top_k):
    bsz, seq, wide = q.shape
    n16 = kcmp.shape[1]
    gw = NSA_GROUPS * LANE
    assert seq % tq == 0 and seq >= WINDOW + tq
    kern = functools.partial(_nsa_attn_kernel, top_k=top_k)
    cblk = pl.BlockSpec((1, n16, gw), lambda b, i: (b, 0, 0))
    sblk = pl.BlockSpec((1, seq, gw), lambda b, i: (b, 0, 0))
    return pl.pallas_call(
        kern,
        grid=(bsz, seq // tq),
        in_specs=[pl.BlockSpec((1, tq, wide), lambda b, i: (b, i, 0)), cblk, cblk, sblk, sblk, sblk, sblk,
                  pl.BlockSpec((1, tq, LANE), lambda b, i: (b, i, 0)), _resident(ov.shape)],
        out_specs=pl.BlockSpec((1, tq, wide // 2), lambda b, i: (b, i, 0)),
        out_shape=jax.ShapeDtypeStruct((bsz, seq, wide // 2), BF),
        compiler_params=_params(("parallel", "arbitrary")),
        name="nsa_attn",
    )(q, kcmp, vcmp, ks, vs, kw, vw, gates, ov)


def _merge_kernel(x_ref, ab_ref, oc_ref, od_ref, wg_ref, bg_ref, wc_ref, wd_ref, wo_ref, g_ref, b_ref, o_ref):
    x = x_ref[...]
    gates = _dot(x.astype(BF), wg_ref[...]) + bg_ref[...]
    yc = _dot(oc_ref[...], wc_ref[...])
    yd = _dot(od_ref[...], wd_ref[...])
    merged = ab_ref[...] + jax.nn.sigmoid(gates[:, :D_MODEL]) * yc + jax.nn.sigmoid(gates[:, D_MODEL:]) * yd
    mix = _dot(merged.astype(BF), wo_ref[...])
    o_ref[...] = _ln(ALPHA * x + mix, g_ref[...], b_ref[...])


def _merge_ln(x, ab, oc, od, wg, bg, wc, wd, wo, g, b, *, layer, tm):
    n, d = x.shape
    row = pl.BlockSpec((tm, d), lambda i: (i, 0))
    rows = lambda a: pl.BlockSpec((tm, a.shape[1]), lambda i: (i, 0))
    return pl.pallas_call(
        _merge_kernel,
        grid=(n // tm,),
        in_specs=[row, row, rows(oc), rows(od)] + [_layered(a, layer) for a in (wg, bg, wc, wd, wo, g, b)],
        out_specs=row,
        out_shape=jax.ShapeDtypeStruct((n, d), F32),
        compiler_params=_params(("parallel",)),
        name="merge_ln",
    )(x, ab, oc, od, wg, bg, wc, wd, wo, g, b)


def _linear_kernel(x_ref, w_ref, o_ref):
    o_ref[...] = _dot(x_ref[...].astype(BF), w_ref[...]).astype(o_ref.dtype)


def _linear(x, w, *, layer, tm, dtype):
    n, d = x.shape
    return pl.pallas_call(
        _linear_kernel,
        grid=(n // tm,),
        in_specs=[pl.BlockSpec((tm, d), lambda i: (i, 0)), _layered(w, layer)],
        out_specs=pl.BlockSpec((tm, w.shape[2]), lambda i: (i, 0)),
        out_shape=jax.ShapeDtypeStruct((n, w.shape[2]), dtype),
        compiler_params=_params(("parallel",)),
        name="mem_kv",
    )(x, w)


def _xattn_kernel(x_ref, k_ref, v_ref, wq_ref, wo_ref, g_ref, b_ref, o_ref):
    x = x_ref[...]
    q = _dot(x.astype(BF), wq_ref[...]).astype(BF)
    k = k_ref[0]
    v = v_ref[0]
    heads = []
    for h in range(XATTN_HEADS):
        sl = slice(h * XATTN_DIM, (h + 1) * XATTN_DIM)
        s = _dot_t(q[:, sl], k[:, sl]) * (XATTN_DIM ** -0.5 * LOG2_E)
        e = jnp.exp2(s - jnp.max(s, -1, keepdims=True))
        heads.append(_dot(e.astype(BF), v[:, sl]) * (1.0 / jnp.sum(e, -1, keepdims=True)))
    o = jnp.concatenate(heads, axis=1).astype(BF)
    o_ref[...] = _ln(ALPHA * x + _dot(o, wo_ref[...]), g_ref[...], b_ref[...])


def _xattn_ln(x, kv, wq, wo, g, b, *, layer, tm, seq):
    n, d = x.shape
    tps = seq // tm
    mlen = kv.shape[1]
    hd = XATTN_HEADS * XATTN_DIM
    return pl.pallas_call(
        _xattn_kernel,
        grid=(n // tm,),
        in_specs=[pl.BlockSpec((tm, d), lambda i: (i, 0)),
                  pl.BlockSpec((1, mlen, hd), lambda i: (i // tps, 0, 0)),
                  pl.BlockSpec((1, mlen, hd), lambda i: (i // tps, 0, 1))]
        + [_layered(a, layer) for a in (wq, wo, g, b)],
        out_specs=pl.BlockSpec((tm, d), lambda i: (i, 0)),
        out_shape=jax.ShapeDtypeStruct((n, d), F32),
        compiler_params=_params(("parallel",)),
        name="xattn_ln",
    )(x, kv, kv, wq, wo, g, b)


def _rope_tab(pos, dim):
    inv = ROPE_THETA ** (-(jnp.arange(0, dim, 2, dtype=F32) / dim))
    ang = pos[:, None] * inv[None, :]
    return jnp.cos(ang), jnp.sin(ang)


def _rot_cols(w, half):
    return jnp.concatenate([-w[..., half:2 * half], w[..., :half]], axis=-1)


def _pad_slots(w, n_slots, width):
    lead = w.shape[:-1]
    w = w.reshape(lead + (n_slots, width))
    w = jnp.pad(w, [(0, 0)] * len(lead) + [(0, 0), (0, LANE - width)])
    return w.reshape(lead + (n_slots * LANE,))


def _layer_params(p):
    w_in, b_in = p["w_in"], p["b_in"]

    def cols(o, wd):
        return w_in[:, o:o + wd], b_in[o:o + wd]

    out = {}
    out["w_ab"] = jnp.concatenate([w_in[:, 0:2560], w_in[:, _O_GA:_O_GC]], axis=1).astype(BF)
    out["b_ab"] = jnp.concatenate([b_in[0:2560], b_in[_O_GA:_O_GC]])[None, :]
    wkr, bkr = cols(_O_KROPE, MLA_ROPE)
    padk =lambda a: jnp.pad(a, [(0, 0)] * (a.ndim - 1) + [(0, LANE - MLA_ROPE)])
    out["w_c"] = jnp.concatenate([w_in[:, _O_QLAT:_O_KROPE], padk(wkr)], axis=1).astype(BF)
    out["b_c"] = jnp.concatenate([b_in[_O_QLAT:_O_KROPE], padk(bkr)])[None, :]
    wuq = p["mla_wuq"].reshape(MLA_Q_RANK, MLA_HEADS, MLA_NOPE + MLA_ROPE)
    wq_c = jnp.pad(wuq, [(0, 0), (0, 0), (0, LANE - MLA_NOPE - MLA_ROPE)])
    out["wq_c"] = wq_c.reshape(MLA_Q_RANK, MLA_HEADS * LANE).astype(BF)
    wukv = p["mla_wukv"].reshape(MLA_KV_RANK, MLA_HEADS, MLA_NOPE + MLA_V)
    out["wk_c"] = jnp.pad(wukv[..., :MLA_NOPE], [(0, 0), (0, 0), (0, LANE - MLA_NOPE)]).reshape(MLA_KV_RANK, -1).astype(BF)
    out["wv_c"] = jnp.pad(wukv[..., MLA_NOPE:], [(0, 0), (0, 0), (0, LANE - MLA_V)]).reshape(MLA_KV_RANK, -1).astype(BF)
    hd = NSA_DIM // 2
    n_gate = NSA_HEADS * 3
    out["w_d"] = jnp.pad(w_in[:, _O_NQ:_O_NGATE + n_gate], [(0, 0), (0, LANE - n_gate)]).astype(BF)
    out["b_d"] = jnp.pad(b_in[_O_NQ:_O_NGATE + n_gate], [(0, LANE - n_gate)])[None, :]

    def cmp_weights(w):
        eye = jnp.eye(NSA_GROUPS, dtype=F32)
        wp = jnp.pad(w, [(0, 0), (0, 0), (0, LANE - NSA_DIM)])
        full = jnp.einsum("lde,gh->lgdhe", wp, eye).reshape(CMP_BLOCK, NSA_GROUPS * NSA_DIM, NSA_GROUPS * LANE)
        return full.reshape(2, CMP_STRIDE * NSA_GROUPS * NSA_DIM, NSA_GROUPS * LANE).astype(BF)

    wck = p["nsa_wcmp_k"]
    out["wcmp_k"] = cmp_weights(wck)
    out["wcmp_kr"] = cmp_weights(_rot_cols(wck, hd))
    out["wcmp_v"] = cmp_weights(p["nsa_wcmp_v"])

    def pe_rows(pe):
        t = jnp.broadcast_to(pe[:, None, :], (CMP_BLOCK, NSA_GROUPS, NSA_DIM))
        return t.reshape(2, CMP_STRIDE * NSA_GROUPS * NSA_DIM)

    out["pe"] = jnp.concatenate([pe_rows(p["nsa_pe_k"]), pe_rows(p["nsa_pe_v"])], axis=0)
    out["w_g"] = w_in[:, _O_GC:].astype(BF)
    out["b_g"] = b_in[_O_GC:][None, :]
    out["wout_c"] = p["mla_wout"].astype(BF)
    out["wout_d"] = p["nsa_wout"].astype(BF)
    out["gmlp_bs_t"] = p["gmlp_bs"].T
    for name in ("gmlp_wout", "conv_wout", "w_o", "xattn_wq", "xattn_wo"):
        out[name] = p[name].astype(BF)
    out["xattn_wkv"] = jnp.concatenate([p["xattn_wk"], p["xattn_wv"]], axis=1).astype(BF)
    return out


def _tables(seq):
    pos = jnp.arange(seq, dtype=F32)
    c16, s16 = _rope_tab(pos, MLA_ROPE)
    one = jnp.ones((seq, MLA_NOPE), F32)
    zero = jnp.zeros((seq, MLA_NOPE), F32)
    tail = LANE - MLA_NOPE - MLA_ROPE
    cq = jnp.concatenate([one, c16, c16, jnp.ones((seq, tail), F32)], axis=1)
    sq = jnp.concatenate([zero, s16, s16, jnp.zeros((seq, tail), F32)], axis=1)
    ck = jnp.pad(jnp.concatenate([c16, c16], axis=1), [(0, 0), (0, LANE - MLA_ROPE)])
    sk = jnp.pad(jnp.concatenate([s16, s16], axis=1), [(0, 0), (0, LANE - MLA_ROPE)])
    c32, s32 = _rope_tab(pos, NSA_DIM)
    cn = jnp.concatenate([c32, c32] * (LANE // NSA_DIM), axis=1)
    sn = jnp.concatenate([s32, s32] * (LANE // NSA_DIM), axis=1)
    n16 = seq // CMP_STRIDE
    cend = (jnp.arange(n16) * CMP_STRIDE + CMP_BLOCK - 1).astype(F32)
    cc32, cs32 = _rope_tab(cend, NSA_DIM)
    ccg = jnp.pad(jnp.concatenate([cc32, cc32], axis=1), [(0, 0), (0, LANE - NSA_DIM)])
    csg = jnp.pad(jnp.concatenate([cs32, cs32], axis=1), [(0, 0), (0, LANE - NSA_DIM)])
    cc = jnp.concatenate([ccg] * NSA_GROUPS, axis=1)
    cs = jnp.concatenate([csg] * NSA_GROUPS, axis=1)
    n_cmp = (seq - CMP_BLOCK) // CMP_STRIDE + 1
    n_slc = seq // SLC_BLOCK
    cstart = jnp.arange(n16) * CMP_STRIDE
    sstart = jnp.arange(n_slc) * SLC_BLOCK
    ovl = (jnp.minimum(cstart[None, :] + CMP_BLOCK, sstart[:, None] + SLC_BLOCK)
           - jnp.maximum(cstart[None, :], sstart[:, None]))
    ovl = jnp.clip(ovl, 0).astype(F32) / CMP_BLOCK
    ovl = jnp.where(jnp.arange(n16)[None, :] < n_cmp, ovl, 0.0).astype(BF)
    return dict(cq=cq, sq=sq, ck=ck, sk=sk, cn=cn, sn=sn, cc=cc, cs=cs, ovl=ovl)


def kernel(x, mem, ffn1_w1, ffn1_w3, ffn1_w2, ln1_g, ln1_b, w_in, b_in, gmlp_ln_g, gmlp_ln_b, gmlp_ws, gmlp_bs, gmlp_wout, conv_w, conv_wout, mla_qnorm_g, mla_kvnorm_g, mla_wuq, mla_wukv, mla_wout, nsa_pe_k, nsa_pe_v, nsa_wcmp_k, nsa_wcmp_v, nsa_wout, w_o, ln2_g, ln2_b, xattn_wq, xattn_wk, xattn_wv, xattn_wo, ln3_g, ln3_b, ffn2_w1, ffn2_w3, ffn2_w2, ln4_g, ln4_b):
    bsz, seq, d = x.shape
    mlen = mem.shape[1]
    n = bsz * seq
    assert d == D_MODEL and seq % 1024 == 0
    lp = jax.vmap(_layer_params)(dict(
        w_in=w_in, b_in=b_in, mla_wuq=mla_wuq, mla_wukv=mla_wukv, mla_wout=mla_wout, nsa_pe_k=nsa_pe_k,
        nsa_pe_v=nsa_pe_v, nsa_wcmp_k=nsa_wcmp_k, nsa_wcmp_v=nsa_wcmp_v, nsa_wout=nsa_wout, gmlp_bs=gmlp_bs,
        gmlp_wout=gmlp_wout, conv_wout=conv_wout, w_o=w_o, xattn_wq=xattn_wq, xattn_wk=xattn_wk,
        xattn_wv=xattn_wv, xattn_wo=xattn_wo))
    tb = _tables(seq)
    tm = 1024
    tm_merge = 512
    tf = D_FF // 11
    top_k = min(SLC_TOPK, seq // SLC_BLOCK)
    row = lambda a: a[:, None, :]

    def cast_stacked(w):
        return _to_bf16(w.reshape(w.shape[0] * w.shape[1], w.shape[2])).reshape(w.shape)

    ffn1 = [cast_stacked(w) for w in (ffn1_w1, ffn1_w3, ffn1_w2)]
    ffn2 = [cast_stacked(w) for w in (ffn2_w1, ffn2_w3, ffn2_w2)]
    h = x.reshape(n, d)
    mem2 = mem.reshape(bsz * mlen, d)
    wide = MLA_HEADS * LANE
    gw = NSA_GROUPS * LANE
    for l in range(DEPTH):
        h = _ffn_ln(h, *ffn1, row(ln1_g), row(ln1_b), layer=l, tm=tm, tf=tf)
        ab = _mix_ab(h, lp["w_ab"], lp["b_ab"], row(gmlp_ln_g), row(gmlp_ln_b), gmlp_ws, lp["gmlp_bs_t"],
                     lp["gmlp_wout"], conv_w, lp["conv_wout"], layer=l, tm=tm, seq=seq)
        qc, kc_, vc_ = _mla_proj(h, lp["w_c"], lp["b_c"], row(mla_qnorm_g), row(mla_kvnorm_g),
                                 lp["wq_c"], lp["wk_c"], lp["wv_c"],
                                 tb["cq"], tb["sq"], tb["ck"], tb["sk"], layer=l, tm=tm, seq=seq)
        oc = _flash_causal(qc.reshape(bsz, seq, wide), kc_.reshape(bsz, seq, wide), vc_.reshape(bsz, seq, wide), tq=1024, hp=4)
        qn, nkc, nvc, nks, nvs, nkw, nvw, gates = _nsa_proj(h, lp["w_d"], lp["b_d"], tb["cn"], tb["sn"],
                                                            layer=l, tm=tm, seq=seq)
        kcmp, vcmp = _nsa_compress(nkc.reshape(bsz, seq, NSA_GROUPS * NSA_DIM), nvc.reshape(bsz, seq, NSA_GROUPS * NSA_DIM),
                                   lp["pe"], lp["wcmp_k"], lp["wcmp_kr"], lp["wcmp_v"], tb["cc"], tb["cs"], layer=l)
        od = _nsa_attention(qn.reshape(bsz, seq, NSA_HEADS * LANE), kcmp, vcmp,
                            nks.reshape(bsz, seq, gw), nvs.reshape(bsz, seq, gw),
                            nkw.reshape(bsz, seq, gw), nvw.reshape(bsz, seq, gw),
                            gates.reshape(bsz, seq, LANE), tb["ovl"], tq=512, top_k=top_k)
        h = _merge_ln(h, ab, oc.reshape(n, MLA_HEADS * MLA_V), od.reshape(n, NSA_HEADS * NSA_DIM), lp["w_g"], lp["b_g"],
                      lp["wout_c"], lp["wout_d"], lp["w_o"], row(ln2_g), row(ln2_b), layer=l, tm=tm_merge)
        kv = _linear(mem2, lp["xattn_wkv"], layer=l, tm=min(256, bsz * mlen), dtype=BF)
        h = _xattn_ln(h, kv.reshape(bsz, mlen, 2 * XATTN_HEADS * XATTN_DIM), lp["xattn_wq"], lp["xattn_wo"],
                      row(ln3_g), row(ln3_b), layer=l, tm=tm, seq=seq)
        h = _ffn_ln(h, *ffn2, row(ln4_g), row(ln4_b), layer=l, tm=tm, tf=tf)
    return h.reshape(bsz, seq, d)
```

```python
import functools

import jax
import jax.numpy as jnp
from jax import lax
from jax.experimental import pallas as pl
from jax.experimental.pallas import tpu as pltpu

BF = jnp.bfloat16
F32 = jnp.float32

D_MODEL = 1024
D_FF = 2816
LN_EPS = 1e-5
RMS_EPS = 1e-6
ROPE_THETA = 10000.0
DEPTH = 2
ALPHA = (2 * DEPTH) ** 0.25
NEG = -1e30
LOG2_E = 1.4426950408889634
DENOM_LANE = 64
MASK_BIG = 2.0 ** 100

GMLP_CHUNK = 128
GMLP_GROUPS = 4
GMLP_WIDTH = 512
CONV_WIDTH = 512
CONV_K = 3
MLA_HEADS = 8
MLA_Q_RANK = 256
MLA_KV_RANK = 128
MLA_NOPE = 64
MLA_ROPE = 32
MLA_V = 64
NSA_HEADS = 8
NSA_GROUPS = 2
NSA_HPG = 4
NSA_DIM = 64
CMP_BLOCK = 32
CMP_STRIDE = 16
SLC_BLOCK = 64
SLC_SHIFT = SLC_BLOCK.bit_length() - 1
SLC_TOPK = 8
WINDOW = 512
XATTN_HEADS = 4
XATTN_DIM = 128

LANE = 128
CONV_HALO = 8
VMEM_LIMIT = 56 * 1024 * 1024

_O_U, _O_V, _O_CB, _O_CC, _O_CH = 0, 512, 1024, 1536, 2048
_O_QLAT, _O_KVLAT, _O_KROPE = 2560, 2816, 2944
_O_NQ, _O_NKC, _O_NVC, _O_NKS, _O_NVS, _O_NKW, _O_NVW, _O_NGATE = 2976, 3488, 3616, 3744, 3872, 4000, 4128, 4256
_O_GA, _O_GB, _O_GC, _O_GD = 4280, 5304, 6328, 7352


def _dot(a, b):
    return jnp.dot(a, b, preferred_element_type=F32)


def _dot_t(a, b):
    return lax.dot_general(a, b, (((1,), (1,)), ((), ())), preferred_element_type=F32)


def _ln(y, g, b):
    mu = jnp.mean(y, -1, keepdims=True)
    d = y - mu
    var = jnp.mean(d * d, -1, keepdims=True)
    return d * lax.rsqrt(var + LN_EPS) * g + b


def _rms(x, g):
    return x * lax.rsqrt(jnp.mean(x * x, -1, keepdims=True) + RMS_EPS) * g


def _resident(shape):
    n = len(shape)
    return pl.BlockSpec(shape, lambda *_: (0,) * n, pipeline_mode=pl.Buffered(1))


def _layered(a, layer):
    n = a.ndim - 1
    return pl.BlockSpec((None,) + a.shape[1:], lambda *_: (layer,) + (0,) * n, pipeline_mode=pl.Buffered(1))


def _params(sem):
    return pltpu.CompilerParams(dimension_semantics=sem, vmem_limit_bytes=VMEM_LIMIT)


def _cast_kernel(x_ref, o_ref):
    o_ref[...] = x_ref[...].astype(o_ref.dtype)


def _to_bf16(w, *, rows=512):
    r, c = w.shape
    rows = min(rows, r)
    assert r % rows == 0
    return pl.pallas_call(
        _cast_kernel,
        grid=(r // rows,),
        in_specs=[pl.BlockSpec((rows, c), lambda i: (i, 0))],
        out_specs=pl.BlockSpec((rows, c), lambda i: (i, 0)),
        out_shape=jax.ShapeDtypeStruct((r, c), BF),
        compiler_params=_params(("parallel",)),
        name="to_bf16",
    )(w)


def _ffn_ln_kernel(x_ref, w1_ref, w3_ref, w2_ref, g_ref, b_ref, o_ref, *, tf):
    x = x_ref[...]
    xb = x.astype(BF)
    acc = None
    for c in range(w1_ref.shape[1] // tf):
        cols = slice(c * tf, (c + 1) * tf)
        h1 = _dot(xb, w1_ref[:, cols])
        h3 = _dot(xb, w3_ref[:, cols])
        hh = (h1 * jax.nn.sigmoid(h1)) * h3
        part = _dot(hh.astype(BF), w2_ref[cols, :])
        acc = part if acc is None else acc + part
    o_ref[...] = _ln(ALPHA * x + 0.5 * acc, g_ref[...], b_ref[...])


def _ffn_ln(x, w1, w3, w2, g, b, *, layer, tm, tf):
    n, d = x.shape
    return pl.pallas_call(
        functools.partial(_ffn_ln_kernel, tf=tf),
        grid=(n // tm,),
        in_specs=[pl.BlockSpec((tm, d), lambda i: (i, 0))] + [_layered(a, layer) for a in (w1, w3, w2, g, b)],
        out_specs=pl.BlockSpec((tm, d), lambda i: (i, 0)),
        out_shape=jax.ShapeDtypeStruct((n, d), F32),
        compiler_params=_params(("parallel",)),
        name="ffn_ln",
    )(x, w1, w3, w2, g, b)


def _ab_kernel(h_ref, w_ref, b_ref, lng_ref, lnb_ref, ws_ref, bst_ref, wga_ref, cw_ref, wcb_ref,
               o_ref, prev_ref, *, tiles_per_seq):
    i = pl.program_id(0)
    tm = h_ref.shape[0]
    hb = h_ref[...].astype(BF)

    def proj(c0, width):
        return _dot(hb, w_ref[:, c0:c0 + width]) + b_ref[:, c0:c0 + width]

    u = proj(0, GMLP_WIDTH)
    v = _ln(proj(512, GMLP_WIDTH), lng_ref[...], lnb_ref[...]).astype(BF)
    row = lax.broadcasted_iota(jnp.int32, (GMLP_CHUNK, GMLP_CHUNK), 0)
    col = lax.broadcasted_iota(jnp.int32, (GMLP_CHUNK, GMLP_CHUNK), 1)
    gd = GMLP_WIDTH // GMLP_GROUPS
    wgs = [jnp.where(row >= col, ws_ref[g], 0.0).astype(BF) for g in range(GMLP_GROUPS)]
    chunks = []
    for c in range(tm // GMLP_CHUNK):
        r0 = c * GMLP_CHUNK
        chunks.append(jnp.concatenate(
            [_dot(wgs[g], v[r0:r0 + GMLP_CHUNK, g * gd:(g + 1) * gd]) + bst_ref[:, g:g + 1]
             for g in range(GMLP_GROUPS)], axis=1))
    s = jnp.concatenate(chunks, axis=0)
    ya = _dot((u * s).astype(BF), wga_ref[...])

    cb = proj(1024, CONV_WIDTH)
    z = proj(1536, CONV_WIDTH) * proj(2048, CONV_WIDTH)

    @pl.when(i % tiles_per_seq == 0)
    def _():
        prev_ref[...] = jnp.zeros_like(prev_ref)

    zext = jnp.concatenate([prev_ref[...], z], axis=0)
    z1 = pltpu.roll(zext, 1, 0)[CONV_HALO:]
    z2 = pltpu.roll(zext, 2, 0)[CONV_HALO:]
    y = cw_ref[0:1, :] * z2 + cw_ref[1:2, :] * z1 + cw_ref[2:3, :] * z
    prev_ref[...] = z[tm - CONV_HALO:, :]
    yb = _dot((cb * y).astype(BF), wcb_ref[...])

    ga = proj(2560, D_MODEL)
    gb = proj(3584, D_MODEL)
    o_ref[...] = jax.nn.sigmoid(ga) * ya + jax.nn.sigmoid(gb) * yb


def _mix_ab(h, w, b, lng, lnb, ws, bst, wga, cw, wcb, *, layer, tm, seq):
    n, d = h.shape
    kern = functools.partial(_ab_kernel, tiles_per_seq=seq // tm)
    return pl.pallas_call(
        kern,
        grid=(n // tm,),
        in_specs=[pl.BlockSpec((tm, d), lambda i: (i, 0))]
        + [_layered(a, layer) for a in (w, b, lng, lnb, ws, bst, wga, cw, wcb)],
        out_specs=pl.BlockSpec((tm, d), lambda i: (i, 0)),
        out_shape=jax.ShapeDtypeStruct((n, d), F32),
        scratch_shapes=[pltpu.VMEM((CONV_HALO, CONV_WIDTH), F32)],
        compiler_params=_params(("arbitrary",)),
        name="mix_ab",
    )(h, w, b, lng, lnb, ws, bst, wga, cw, wcb)


def _mla_proj_kernel(h_ref, w_ref, b_ref, qg_ref, kvg_ref, wq_ref, wk_ref, wv_ref,
                     cq_ref, sq_ref, ck_ref, sk_ref, q_ref, k_ref, v_ref):
    hb = h_ref[...].astype(BF)
    z = _dot(hb, w_ref[...]) + b_ref[...]
    qn = _rms(z[:, 0:256], qg_ref[...]).astype(BF)
    kvn = _rms(z[:, 256:384], kvg_ref[...]).astype(BF)
    half = MLA_ROPE // 2

    def rotate_half(x, start):
        w = x.shape[1]
        first = lax.broadcasted_iota(jnp.int32, (1, w), 1) % LANE < start + half
        return jnp.where(first, -pltpu.roll(x, w - half, 1), pltpu.roll(x, half, 1))

    cq = jnp.concatenate([cq_ref[...]] * MLA_HEADS, axis=1)
    sq = jnp.concatenate([sq_ref[...]] * MLA_HEADS, axis=1)
    scale = (MLA_NOPE + MLA_ROPE) ** -0.5 * LOG2_E
    q = _dot(qn, wq_ref[...])
    q_ref[...] = ((q * cq + rotate_half(q, MLA_NOPE) * sq) * scale).astype(BF)
    kr = z[:, 384:512]
    kpe = pltpu.roll(kr * ck_ref[...] + rotate_half(kr, 0) * sk_ref[...], MLA_NOPE, 1)
    k_ref[...] = (_dot(kvn, wk_ref[...]) + jnp.concatenate([kpe] * MLA_HEADS, axis=1)).astype(BF)
    v_ref[...] = (_dot(kvn, wv_ref[...]) + _denom_ones(v_ref.shape[1])).astype(BF)


def _mla_proj(h, w, b, qg, kvg, wq, wk, wv, cq, sq, ck, sk, *, layer, tm, seq):
    n, d = h.shape
    tps = seq // tm
    tab = pl.BlockSpec((tm, LANE), lambda i: (i % tps, 0))
    wide = MLA_HEADS * LANE
    out = jax.ShapeDtypeStruct((n, wide), BF)
    return pl.pallas_call(
        _mla_proj_kernel,
        grid=(n // tm,),
        in_specs=[pl.BlockSpec((tm, d), lambda i: (i, 0))]
        + [_layered(a, layer) for a in (w, b, qg, kvg, wq, wk, wv)] + [tab] * 4,
        out_specs=[pl.BlockSpec((tm, wide), lambda i: (i, 0))] * 3,
        out_shape=[out, out, out],
        compiler_params=_params(("parallel",)),
        name="mla_proj",
    )(h, w, b, qg, kvg, wq, wk, wv, cq, sq, ck, sk)


def _online_softmax_step(s, v, carry):
    m, acc = carry
    m_new = jnp.maximum(m, jnp.max(s, -1, keepdims=True))
    p = jnp.exp2(s - m_new)
    acc = jnp.exp2(m - m_new) * acc + _dot(p.astype(BF), v)
    return m_new, acc


def _softmax_init(rows, width):
    return (jnp.full((rows, 1), NEG, F32), jnp.zeros((rows, width), F32))


def _normalize(acc):
    return acc * (1.0 / acc[:, DENOM_LANE:DENOM_LANE + 1])


def _pack_head_pairs(slots):
    low = lax.broadcasted_iota(jnp.int32, (1, LANE), 1) < DENOM_LANE
    return jnp.concatenate([jnp.where(low, a, pltpu.roll(b, DENOM_LANE, 1))
                            for a, b in zip(slots[0::2], slots[1::2])], axis=1)


def _denom_ones(width):
    lane = lax.broadcasted_iota(jnp.int32, (1, width), 1)
    return jnp.where(lane % LANE == DENOM_LANE, 1.0, 0.0)


def _flash_kernel(q_ref, k_ref, v_ref, o_ref, *, tq, hp):
    qi = pl.program_id(2)
    q0 = qi * tq
    qs = [q_ref[0, :, h * LANE:(h + 1) * LANE] for h in range(hp)]

    def tile(j, carries, width, diagonal):
        k0 = pl.multiple_of(j * width, width)
        out = []
        for h in range(hp):
            s = _dot_t(qs[h], k_ref[0, pl.ds(k0, width), h * LANE:(h + 1) * LANE])
            if diagonal:
                r = lax.broadcasted_iota(jnp.int32, (tq, width), 0)
                c = lax.broadcasted_iota(jnp.int32, (tq, width), 1)
                s = jnp.where(c <= r, s, NEG)
            out.append(_online_softmax_step(s, v_ref[0, pl.ds(k0, width), h * LANE:(h + 1) * LANE], carries[h]))
        return tuple(out)

    init = tuple(_softmax_init(tq, LANE) for _ in range(hp))
    carries = lax.fori_loop(0, qi, lambda j, c: tile(j, c, tq, False), init)

    half = tq // 2
    carries = tile(2 * qi, carries, half, True)
    r = lax.broadcasted_iota(jnp.int32, (half, half), 0)
    c = lax.broadcasted_iota(jnp.int32, (half, half), 1)
    k1 = pl.multiple_of(q0 + half, half)
    out = []
    for h in range(hp):
        m, acc = carries[h]
        s = _dot_t(qs[h][half:], k_ref[0, pl.ds(k1, half), h * LANE:(h + 1) * LANE])
        m2, acc2 = _online_softmax_step(jnp.where(c <= r, s, NEG), v_ref[0, pl.ds(k1, half), h * LANE:(h + 1) * LANE],
                                        (m[half:], acc[half:]))
        out.append(jnp.concatenate([acc[:half], acc2], axis=0))
    o_ref[0] = _pack_head_pairs([_normalize(acc) for acc in out]).astype(o_ref.dtype)


def _flash_causal(q, k, v, *, tq, hp):
    bsz, seq, wide = q.shape
    heads = wide // LANE
    assert seq % tq == 0 and hp % 2 == 0
    half_lane = LANE // 2
    kern = functools.partial(_flash_kernel, tq=tq, hp=hp)
    return pl.pallas_call(
        kern,
        grid=(bsz, heads // hp, seq // tq),
        in_specs=[
            pl.BlockSpec((1, tq, hp * LANE), lambda b, h, i: (b, i, h)),
            pl.BlockSpec((1, seq, hp * LANE), lambda b, h, i: (b, 0, h)),
            pl.BlockSpec((1, seq, hp * LANE), lambda b, h, i: (b, 0, h)),
        ],
        out_specs=pl.BlockSpec((1, tq, hp * half_lane), lambda b, h, i: (b, i, h)),
        out_shape=jax.ShapeDtypeStruct((bsz, seq, heads * half_lane), BF),
        compiler_params=_params(("parallel", "parallel", "arbitrary")),
        name="mla_flash",
    )(q, k, v)


def _nsa_proj_kernel(h_ref, w_ref, b_ref, c_ref, s_ref, q_ref, kc_ref, vc_ref, ks_ref, vs_ref, kw_ref, vw_ref, g_ref,
                     *, tiles_per_seq):
    tm = h_ref.shape[0]
    hb = h_ref[...].astype(BF)
    pos = (pl.program_id(0) % tiles_per_seq) * tm + lax.broadcasted_iota(jnp.int32, (tm, LANE), 0)
    lane = lax.broadcasted_iota(jnp.int32, (tm, LANE), 1)
    tag = jnp.where(lane == NSA_DIM + lax.shift_right_logical(pos, SLC_SHIFT), MASK_BIG, 0.0)
    tag2 = jnp.concatenate([tag] * NSA_GROUPS, axis=1)
    c = c_ref[...]
    s = s_ref[...]
    half = NSA_DIM // 2

    def rope(x):
        w = x.shape[1]
        first = lax.broadcasted_iota(jnp.int32, (1, w), 1) % NSA_DIM < half
        rot = jnp.where(first, -pltpu.roll(x, w - half, 1), pltpu.roll(x, half, 1))
        reps = w // LANE
        return x * jnp.concatenate([c] * reps, axis=1) + rot * jnp.concatenate([s] * reps, axis=1)

    low = lax.broadcasted_iota(jnp.int32, (1, LANE), 1) < NSA_DIM

    def spread(x):
        out = []
        for j in range(x.shape[1] // LANE):
            blk = x[:, j * LANE:(j + 1) * LANE]
            out += [jnp.where(low, blk, 0.0), jnp.where(low, pltpu.roll(blk, NSA_DIM, 1), 0.0)]
        return jnp.concatenate(out, axis=1)

    z = _dot(hb, w_ref[...]) + b_ref[...]
    q_ref[...] = (spread(rope(z[:, 0:512])) * (NSA_DIM ** -0.5 * LOG2_E)).astype(BF)
    kc_ref[...] = z[:, 512:640]
    vc_ref[...] = z[:, 640:768]
    ks_ref[...] = (spread(rope(z[:, 768:896])) + tag2).astype(BF)
    ones = _denom_ones(NSA_GROUPS * LANE)
    vs_ref[...] = (spread(z[:, 896:1024]) + ones).astype(BF)
    kw_ref[...] = spread(rope(z[:, 1024:1152])).astype(BF)
    vw_ref[...] = (spread(z[:, 1152:1280]) + ones).astype(BF)
    g_ref[...] = jax.nn.sigmoid(z[:, 1280:1408])


def _nsa_proj(h, w, b, cn, sn, *, layer, tm, seq):
    n, d = h.shape
    tps = seq // tm
    tab = pl.BlockSpec((tm, LANE), lambda i: (i % tps, 0))

    def out(width, dt):
        return pl.BlockSpec((tm, width), lambda i: (i, 0)), jax.ShapeDtypeStruct((n, width), dt)

    outs = [out(1024, BF), out(128, F32), out(128, F32), out(256, BF), out(256, BF), out(256, BF), out(256, BF), out(128, F32)]
    assert seq // SLC_BLOCK <= LANE - NSA_DIM
    return pl.pallas_call(
        functools.partial(_nsa_proj_kernel, tiles_per_seq=tps),
        grid=(n // tm,),
        in_specs=[pl.BlockSpec((tm, d), lambda i: (i, 0)), _layered(w, layer), _layered(b, layer), tab, tab],
        out_specs=[o[0] for o in outs],
        out_shape=[o[1] for o in outs],
        compiler_params=_params(("parallel",)),
        name="nsa_proj",
    )(h, w, b, cn, sn)


def _nsa_cmp_kernel(kc_ref, vc_ref, pe_ref, wk_ref, wkr_ref, wv_ref, c_ref, s_ref, kcmp_ref, vcmp_ref):
    n16 = kcmp_ref.shape[1]
    gl = NSA_GROUPS * NSA_DIM
    kc = kcr = vc = None
    for l in range(CMP_STRIDE):
        rows = slice(l * gl, (l + 1) * gl)
        xk = kc_ref[0, pl.ds(l, n16, stride=CMP_STRIDE), :]
        xv = vc_ref[0, pl.ds(l, n16, stride=CMP_STRIDE), :]
        terms = [((xk + pe_ref[0:1, rows]).astype(BF), (xv + pe_ref[2:3, rows]).astype(BF), 0),
                 ((pltpu.roll(xk, n16 - 1, 0) + pe_ref[1:2, rows]).astype(BF),
                  (pltpu.roll(xv, n16 - 1, 0) + pe_ref[3:4, rows]).astype(BF), 1)]
        for ak, av, part in terms:
            pk, pkr, pv = _dot(ak, wk_ref[part, rows, :]), _dot(ak, wkr_ref[part, rows, :]), _dot(av, wv_ref[part, rows, :])
            kc, kcr, vc = (pk, pkr, pv) if kc is None else (kc + pk, kcr + pkr, vc + pv)
    kcmp_ref[0] = (kc * c_ref[...] + kcr * s_ref[...]).astype(BF)
    vcmp_ref[0] = vc.astype(BF)


def _nsa_compress(kc, vc, pe, wk, wkr, wv, cc, sc, *, layer):
    bsz, seq, wide = kc.shape
    n16 = seq // CMP_STRIDE
    blk = pl.BlockSpec((1, seq, wide), lambda b: (b, 0, 0))
    oblk = pl.BlockSpec((1, n16, NSA_GROUPS * LANE), lambda b: (b, 0, 0))
    osh = jax.ShapeDtypeStruct((bsz, n16, NSA_GROUPS * LANE), BF)
    return pl.pallas_call(
        _nsa_cmp_kernel,
        grid=(bsz,),
        in_specs=[blk, blk] + [_layered(a, layer) for a in (pe, wk, wkr, wv)] + [_resident(cc.shape), _resident(sc.shape)],
        out_specs=[oblk, oblk],
        out_shape=[osh, osh],
        compiler_params=_params(("parallel",)),
        name="nsa_compress",
    )(kc, vc, pe, wk, wkr, wv, cc, sc)


def _nsa_attn_kernel(q_ref, kcmp_ref, vcmp_ref, ks_ref, vs_ref, kw_ref, vw_ref, g_ref, ov_ref, o_ref, *, top_k):
    qi = pl.program_id(1)
    T = q_ref.shape[1]
    R = NSA_HPG * T
    G = NSA_GROUPS
    q0 = qi * T
    qpos = lax.broadcasted_iota(jnp.int32, (T, 1), 0) + q0
    ncp = kcmp_ref.shape[1]
    nb = ov_ref.shape[0]
    ov_t = ov_ref[...]

    def add_per_query(x, b):
        w = x.shape[1]
        return (x.reshape(NSA_HPG, T, w) + b[None]).reshape(R, w)

    def lanes(g):
        return slice(g * LANE, (g + 1) * LANE)

    q4 = [jnp.concatenate([q_ref[0, :, (g * NSA_HPG + h) * LANE:(g * NSA_HPG + h + 1) * LANE]
                           for h in range(NSA_HPG)], axis=0) for g in range(G)]

    cmp_end = lax.broadcasted_iota(jnp.int32, (1, ncp), 1) * CMP_STRIDE + (CMP_BLOCK - 1)
    cbias = jnp.where(cmp_end <= qpos, 0.0, NEG)
    any_valid = jnp.where(qpos >= CMP_BLOCK - 1, 1.0, 0.0)
    jr = lax.broadcasted_iota(jnp.int32, (nb, 1), 0)
    jrf = jr.astype(F32)
    jq = lax.shift_right_logical(lax.broadcasted_iota(jnp.int32, (1, T), 1) + q0, SLC_SHIFT)
    forced = (jr == 0) | (jr == jq) | (jr == jq - 1)
    eye_t = jnp.where(lax.broadcasted_iota(jnp.int32, (T, T), 0) == lax.broadcasted_iota(jnp.int32, (T, T), 1),
                      1.0, 0.0).astype(BF)
    o_cmp, q4s = [], []
    for g in range(G):
        sm = _dot_t(q4[g], kcmp_ref[0, :, lanes(g)]).reshape(NSA_HPG, T, ncp) + cbias[None]
        e = jnp.exp2(sm - jnp.max(sm, -1, keepdims=True))
        p = e * (any_valid[None] / jnp.sum(e, -1, keepdims=True))
        o_cmp.append(_dot(p.reshape(R, ncp).astype(BF), vcmp_ref[0, :, lanes(g)]))
        psum = p[0] + p[1] + p[2] + p[3]
        hi = psum.astype(BF)
        r1 = psum - hi.astype(F32)
        mid = r1.astype(BF)
        lo = (r1 - mid.astype(F32)).astype(BF)
        imp = _dot_t(ov_t, hi) + _dot_t(ov_t, mid) + _dot_t(ov_t, lo)
        imp = jnp.where(forced, 1e9, imp)
        imp = jnp.where(jr <= jq, imp, -1.0)
        work = imp
        sel = jnp.zeros_like(imp)
        for _ in range(top_k):
            mx = jnp.max(work, 0, keepdims=True)
            idx = jnp.min(jnp.where(work == mx, jrf, float(nb)), 0, keepdims=True)
            pick = jrf == idx
            sel = jnp.where(pick, 1.0, sel)
            work = jnp.where(pick, -2.0, work)
        unsel_t = jnp.where(imp >= 0.0, sel, 0.0) - 1.0
        pad_t = [jnp.zeros((NSA_DIM, T), F32), unsel_t]
        if LANE - NSA_DIM - nb:
            pad_t.append(jnp.zeros((LANE - NSA_DIM - nb, T), F32))
        unsel = _dot_t(eye_t, jnp.concatenate(pad_t, axis=0).astype(BF)).astype(BF)
        q4s.append(add_per_query(q4[g], unsel))

    def slc_tile(j, carries, diagonal):
        k0 = pl.multiple_of(j * T, T)
        out = []
        for g in range(G):
            sc = _dot_t(q4s[g], ks_ref[0, pl.ds(k0, T), lanes(g)])
            if diagonal:
                sc = add_per_query(sc, jnp.where((lax.broadcasted_iota(jnp.int32, (1, T), 1) + k0) <= qpos, 0.0, NEG))
            out.append(_online_softmax_step(sc, vs_ref[0, pl.ds(k0, T), lanes(g)], carries[g]))
        return tuple(out)

    carries = lax.fori_loop(0, qi, lambda j, c: slc_tile(j, c, False),
                            tuple(_softmax_init(R, LANE) for _ in range(G)))
    carries = slc_tile(qi, carries, True)
    slc_acc = [acc for (_, acc) in carries]

    wk = WINDOW + T
    w0 = pl.multiple_of(jnp.maximum(q0 - WINDOW, 0), T)
    dist = qpos - (lax.broadcasted_iota(jnp.int32, (1, wk), 1) + w0)
    wbias = jnp.where((dist >= 0) & (dist < WINDOW), 0.0, NEG)
    o_win = []
    for g in range(G):
        sc = add_per_query(_dot_t(q4[g], kw_ref[0, pl.ds(w0, wk), lanes(g)]), wbias)
        e = jnp.exp2(sc - jnp.max(sc, -1, keepdims=True))
        o_win.append(_normalize(_dot(e.astype(BF), vw_ref[0, pl.ds(w0, wk), lanes(g)])))

    gw = NSA_HPG * LANE
    e_row = lax.broadcasted_iota(jnp.int32, (LANE, 3 * gw), 0)
    e_col = lax.broadcasted_iota(jnp.int32, (LANE, 3 * gw), 1)
    branch = jnp.where(e_col >= 2 * gw, 2, jnp.where(e_col >= gw, 1, 0))
    head = lax.shift_right_logical(e_col - branch * gw, LANE.bit_length() - 1)
    gate_col = 3 * head + branch

    def heads_on_lanes(x):
        return jnp.concatenate([x[h * T:(h + 1) * T] for h in range(NSA_HPG)], axis=1)

    gs = g_ref[0]
    hi = gs.astype(BF)
    lo = (gs - hi.astype(F32)).astype(BF)
    hi_lo = jnp.concatenate([hi, lo], axis=1)
    outs = []
    for g in range(G):
        expand = jnp.where(e_row == gate_col + g * (3 * NSA_HPG), 1.0, 0.0).astype(BF)
        gx = _dot(hi_lo, jnp.concatenate([expand, expand], axis=0))
        mixed = (gx[:, 0:gw] * heads_on_lanes(o_cmp[g])
                 + gx[:, gw:2 * gw] * heads_on_lanes(_normalize(slc_acc[g]))
                 + gx[:, 2 * gw:3 * gw] * heads_on_lanes(o_win[g]))
        outs.append(_pack_head_pairs([mixed[:, h * LANE:(h + 1) * LANE] for h in range(NSA_HPG)]))
    o_ref[0] = jnp.concatenate(outs, axis=1).astype(o_ref.dtype)


def _nsa_attention(q, kcmp, vcmp, ks, vs, kw, vw, gates, ov, *, tq, top_k):
    bsz, seq, wide = q.shape
    n16 = kcmp.shape[1]
    gw = NSA_GROUPS * LANE
    assert seq % tq == 0 and seq >= WINDOW + tq
    kern = functools.partial(_nsa_attn_kernel, top_k=top_k)
    cblk = pl.BlockSpec((1, n16, gw), lambda b, i: (b, 0, 0))
    sblk = pl.BlockSpec((1, seq, gw), lambda b, i: (b, 0, 0))
    return pl.pallas_call(
        kern,
        grid=(bsz, seq // tq),
        in_specs=[pl.BlockSpec((1, tq, wide), lambda b, i: (b, i, 0)), cblk, cblk, sblk, sblk, sblk, sblk,
                  pl.BlockSpec((1, tq, LANE), lambda b, i: (b, i, 0)), _resident(ov.shape)],
        out_specs=pl.BlockSpec((1, tq, wide // 2), lambda b, i: (b, i, 0)),
        out_shape=jax.ShapeDtypeStruct((bsz, seq, wide // 2), BF),
        compiler_params=_params(("parallel", "arbitrary")),
        name="nsa_attn",
    )(q, kcmp, vcmp, ks, vs, kw, vw, gates, ov)


def _merge_kernel(x_ref, ab_ref, oc_ref, od_ref, wg_ref, bg_ref, wc_ref, wd_ref, wo_ref, g_ref, b_ref, o_ref):
    x = x_ref[...]
    gates = _dot(x.astype(BF), wg_ref[...]) + bg_ref[...]
    yc = _dot(oc_ref[...], wc_ref[...])
    yd = _dot(od_ref[...], wd_ref[...])
    merged = ab_ref[...] + jax.nn.sigmoid(gates[:, :D_MODEL]) * yc + jax.nn.sigmoid(gates[:, D_MODEL:]) * yd
    mix = _dot(merged.astype(BF), wo_ref[...])
    o_ref[...] = _ln(ALPHA * x + mix, g_ref[...], b_ref[...])


def _merge_ln(x, ab, oc, od, wg, bg, wc, wd, wo, g, b, *, layer, tm):
    n, d = x.shape
    row = pl.BlockSpec((tm, d), lambda i: (i, 0))
    rows = lambda a: pl.BlockSpec((tm, a.shape[1]), lambda i: (i, 0))
    return pl.pallas_call(
        _merge_kernel,
        grid=(n // tm,),
        in_specs=[row, row, rows(oc), rows(od)] + [_layered(a, layer) for a in (wg, bg, wc, wd, wo, g, b)],
        out_specs=row,
        out_shape=jax.ShapeDtypeStruct((n, d), F32),
        compiler_params=_params(("parallel",)),
        name="merge_ln",
    )(x, ab, oc, od, wg, bg, wc, wd, wo, g, b)


def _linear_kernel(x_ref, w_ref, o_ref):
    o_ref[...] = _dot(x_ref[...].astype(BF), w_ref[...]).astype(o_ref.dtype)


def _linear(x, w, *, layer, tm, dtype):
    n, d = x.shape
    return pl.pallas_call(
        _linear_kernel,
        grid=(n // tm,),
        in_specs=[pl.BlockSpec((tm, d), lambda i: (i, 0)), _layered(w, layer)],
        out_specs=pl.BlockSpec((tm, w.shape[2]), lambda i: (i, 0)),
        out_shape=jax.ShapeDtypeStruct((n, w.shape[2]), dtype),
        compiler_params=_params(("parallel",)),
        name="mem_kv",
    )(x, w)


def _xattn_kernel(x_ref, k_ref, v_ref, wq_ref, wo_ref, g_ref, b_ref, o_ref):
    x = x_ref[...]
    q = _dot(x.astype(BF), wq_ref[...]).astype(BF)
    k = k_ref[0]
    v = v_ref[0]
    heads = []
    for h in range(XATTN_HEADS):
        sl = slice(h * XATTN_DIM, (h + 1) * XATTN_DIM)
        s = _dot_t(q[:, sl], k[:, sl]) * (XATTN_DIM ** -0.5 * LOG2_E)
        e = jnp.exp2(s - jnp.max(s, -1, keepdims=True))
        heads.append(_dot(e.astype(BF), v[:, sl]) * (1.0 / jnp.sum(e, -1, keepdims=True)))
    o = jnp.concatenate(heads, axis=1).astype(BF)
    o_ref[...] = _ln(ALPHA * x + _dot(o, wo_ref[...]), g_ref[...], b_ref[...])


def _xattn_ln(x, kv, wq, wo, g, b, *, layer, tm, seq):
    n, d = x.shape
    tps = seq // tm
    mlen = kv.shape[1]
    hd = XATTN_HEADS * XATTN_DIM
    return pl.pallas_call(
        _xattn_kernel,
        grid=(n // tm,),
        in_specs=[pl.BlockSpec((tm, d), lambda i: (i, 0)),
                  pl.BlockSpec((1, mlen, hd), lambda i: (i // tps, 0, 0)),
                  pl.BlockSpec((1, mlen, hd), lambda i: (i // tps, 0, 1))]
        + [_layered(a, layer) for a in (wq, wo, g, b)],
        out_specs=pl.BlockSpec((tm, d), lambda i: (i, 0)),
        out_shape=jax.ShapeDtypeStruct((n, d), F32),
        compiler_params=_params(("parallel",)),
        name="xattn_ln",
    )(x, kv, kv, wq, wo, g, b)


def _rope_tab(pos, dim):
    inv = ROPE_THETA ** (-(jnp.arange(0, dim, 2, dtype=F32) / dim))
    ang = pos[:, None] * inv[None, :]
    return jnp.cos(ang), jnp.sin(ang)


def _rot_cols(w, half):
    return jnp.concatenate([-w[..., half:2 * half], w[..., :half]], axis=-1)


def _pad_slots(w, n_slots, width):
    lead = w.shape[:-1]
    w = w.reshape(lead + (n_slots, width))
    w = jnp.pad(w, [(0, 0)] * len(lead) + [(0, 0), (0, LANE - width)])
    return w.reshape(lead + (n_slots * LANE,))


def _layer_params(p):
    w_in, b_in = p["w_in"], p["b_in"]

    def cols(o, wd):
        return w_in[:, o:o + wd], b_in[o:o + wd]

    out = {}
    out["w_ab"] = jnp.concatenate([w_in[:, 0:2560], w_in[:, _O_GA:_O_GC]], axis=1).astype(BF)
    out["b_ab"] = jnp.concatenate([b_in[0:2560], b_in[_O_GA:_O_GC]])[None, :]
    wkr, bkr = cols(_O_KROPE, MLA_ROPE)
    padk =lambda a: jnp.pad(a, [(0, 0)] * (a.ndim - 1) + [(0, LANE - MLA_ROPE)])
    out["w_c"] = jnp.concatenate([w_in[:, _O_QLAT:_O_KROPE], padk(wkr)], axis=1).astype(BF)
    out["b_c"] = jnp.concatenate([b_in[_O_QLAT:_O_KROPE], padk(bkr)])[None, :]
    wuq = p["mla_wuq"].reshape(MLA_Q_RANK, MLA_HEADS, MLA_NOPE + MLA_ROPE)
    wq_c = jnp.pad(wuq, [(0, 0), (0, 0), (0, LANE - MLA_NOPE - MLA_ROPE)])
    out["wq_c"] = wq_c.reshape(MLA_Q_RANK, MLA_HEADS * LANE).astype(BF)
    wukv = p["mla_wukv"].reshape(MLA_KV_RANK, MLA_HEADS, MLA_NOPE + MLA_V)
    out["wk_c"] = jnp.pad(wukv[..., :MLA_NOPE], [(0, 0), (0, 0), (0, LANE - MLA_NOPE)]).reshape(MLA_KV_RANK, -1).astype(BF)
    out["wv_c"] = jnp.pad(wukv[..., MLA_NOPE:], [(0, 0), (0, 0), (0, LANE - MLA_V)]).reshape(MLA_KV_RANK, -1).astype(BF)
    hd = NSA_DIM // 2
    n_gate = NSA_HEADS * 3
    out["w_d"] = jnp.pad(w_in[:, _O_NQ:_O_NGATE + n_gate], [(0, 0), (0, LANE - n_gate)]).astype(BF)
    out["b_d"] = jnp.pad(b_in[_O_NQ:_O_NGATE + n_gate], [(0, LANE - n_gate)])[None, :]

    def cmp_weights(w):
        eye = jnp.eye(NSA_GROUPS, dtype=F32)
        wp = jnp.pad(w, [(0, 0), (0, 0), (0, LANE - NSA_DIM)])
        full = jnp.einsum("lde,gh->lgdhe", wp, eye).reshape(CMP_BLOCK, NSA_GROUPS * NSA_DIM, NSA_GROUPS * LANE)
        return full.reshape(2, CMP_STRIDE * NSA_GROUPS * NSA_DIM, NSA_GROUPS * LANE).astype(BF)

    wck = p["nsa_wcmp_k"]
    out["wcmp_k"] = cmp_weights(wck)
    out["wcmp_kr"] = cmp_weights(_rot_cols(wck, hd))
    out["wcmp_v"] = cmp_weights(p["nsa_wcmp_v"])

    def pe_rows(pe):
        t = jnp.broadcast_to(pe[:, None, :], (CMP_BLOCK, NSA_GROUPS, NSA_DIM))
        return t.reshape(2, CMP_STRIDE * NSA_GROUPS * NSA_DIM)

    out["pe"] = jnp.concatenate([pe_rows(p["nsa_pe_k"]), pe_rows(p["nsa_pe_v"])], axis=0)
    out["w_g"] = w_in[:, _O_GC:].astype(BF)
    out["b_g"] = b_in[_O_GC:][None, :]
    out["wout_c"] = p["mla_wout"].astype(BF)
    out["wout_d"] = p["nsa_wout"].astype(BF)
    out["gmlp_bs_t"] = p["gmlp_bs"].T
    for name in ("gmlp_wout", "conv_wout", "w_o", "xattn_wq", "xattn_wo"):
        out[name] = p[name].astype(BF)
    out["xattn_wkv"] = jnp.concatenate([p["xattn_wk"], p["xattn_wv"]], axis=1).astype(BF)
    return out


def _tables(seq):
    pos = jnp.arange(seq, dtype=F32)
    c16, s16 = _rope_tab(pos, MLA_ROPE)
    one = jnp.ones((seq, MLA_NOPE), F32)
    zero = jnp.zeros((seq, MLA_NOPE), F32)
    tail = LANE - MLA_NOPE - MLA_ROPE
    cq = jnp.concatenate([one, c16, c16, jnp.ones((seq, tail), F32)], axis=1)
    sq = jnp.concatenate([zero, s16, s16, jnp.zeros((seq, tail), F32)], axis=1)
    ck = jnp.pad(jnp.concatenate([c16, c16], axis=1), [(0, 0), (0, LANE - MLA_ROPE)])
    sk = jnp.pad(jnp.concatenate([s16, s16], axis=1), [(0, 0), (0, LANE - MLA_ROPE)])
    c32, s32 = _rope_tab(pos, NSA_DIM)
    cn = jnp.concatenate([c32, c32] * (LANE // NSA_DIM), axis=1)
    sn = jnp.concatenate([s32, s32] * (LANE // NSA_DIM), axis=1)
    n16 = seq // CMP_STRIDE
    cend = (jnp.arange(n16) * CMP_STRIDE + CMP_BLOCK - 1).astype(F32)
    cc32, cs32 = _rope_tab(cend, NSA_DIM)
    ccg = jnp.pad(jnp.concatenate([cc32, cc32], axis=1), [(0, 0), (0, LANE - NSA_DIM)])
    csg = jnp.pad(jnp.concatenate([cs32, cs32], axis=1), [(0, 0), (0, LANE - NSA_DIM)])
    cc = jnp.concatenate([ccg] * NSA_GROUPS, axis=1)
    cs = jnp.concatenate([csg] * NSA_GROUPS, axis=1)
    n_cmp = (seq - CMP_BLOCK) // CMP_STRIDE + 1
    n_slc = seq // SLC_BLOCK
    cstart = jnp.arange(n16) * CMP_STRIDE
    sstart = jnp.arange(n_slc) * SLC_BLOCK
    ovl = (jnp.minimum(cstart[None, :] + CMP_BLOCK, sstart[:, None] + SLC_BLOCK)
           - jnp.maximum(cstart[None, :], sstart[:, None]))
    ovl = jnp.clip(ovl, 0).astype(F32) / CMP_BLOCK
    ovl = jnp.where(jnp.arange(n16)[None, :] < n_cmp, ovl, 0.0).astype(BF)
    return dict(cq=cq, sq=sq, ck=ck, sk=sk, cn=cn, sn=sn, cc=cc, cs=cs, ovl=ovl)


def kernel(x, mem, ffn1_w1, ffn1_w3, ffn1_w2, ln1_g, ln1_b, w_in, b_in, gmlp_ln_g, gmlp_ln_b, gmlp_ws, gmlp_bs, gmlp_wout, conv_w, conv_wout, mla_qnorm_g, mla_kvnorm_g, mla_wuq, mla_wukv, mla_wout, nsa_pe_k, nsa_pe_v, nsa_wcmp_k, nsa_wcmp_v, nsa_wout, w_o, ln2_g, ln2_b, xattn_wq, xattn_wk, xattn_wv, xattn_wo, ln3_g, ln3_b, ffn2_w1, ffn2_w3, ffn2_w2, ln4_g, ln4_b):
    bsz, seq, d = x.shape
    mlen = mem.shape[1]
    n = bsz * seq
    assert d == D_MODEL and seq % 1024 == 0
    lp = jax.vmap(_layer_params)(dict(
        w_in=w_in, b_in=b_in, mla_wuq=mla_wuq, mla_wukv=mla_wukv, mla_wout=mla_wout, nsa_pe_k=nsa_pe_k,
        nsa_pe_v=nsa_pe_v, nsa_wcmp_k=nsa_wcmp_k, nsa_wcmp_v=nsa_wcmp_v, nsa_wout=nsa_wout, gmlp_bs=gmlp_bs,
        gmlp_wout=gmlp_wout, conv_wout=conv_wout, w_o=w_o, xattn_wq=xattn_wq, xattn_wk=xattn_wk,
        xattn_wv=xattn_wv, xattn_wo=xattn_wo))
    tb = _tables(seq)
    tm = 1024
    tm_merge = 512
    tf = D_FF // 11
    top_k = min(SLC_TOPK, seq // SLC_BLOCK)
    row = lambda a: a[:, None, :]

    def cast_stacked(w):
        return _to_bf16(w.reshape(w.shape[0] * w.shape[1], w.shape[2])).reshape(w.shape)

    ffn1 = [cast_stacked(w) for w in (ffn1_w1, ffn1_w3, ffn1_w2)]
    ffn2 = [cast_stacked(w) for w in (ffn2_w1, ffn2_w3, ffn2_w2)]
    h = x.reshape(n, d)
    mem2 = mem.reshape(bsz * mlen, d)
    wide = MLA_HEADS * LANE
    gw = NSA_GROUPS * LANE
    for l in range(DEPTH):
        h = _ffn_ln(h, *ffn1, row(ln1_g), row(ln1_b), layer=l, tm=tm, tf=tf)
        ab = _mix_ab(h, lp["w_ab"], lp["b_ab"], row(gmlp_ln_g), row(gmlp_ln_b), gmlp_ws, lp["gmlp_bs_t"],
                     lp["gmlp_wout"], conv_w, lp["conv_wout"], layer=l, tm=tm, seq=seq)
        qc, kc_, vc_ = _mla_proj(h, lp["w_c"], lp["b_c"], row(mla_qnorm_g), row(mla_kvnorm_g),
                                 lp["wq_c"], lp["wk_c"], lp["wv_c"],
                                 tb["cq"], tb["sq"], tb["ck"], tb["sk"], layer=l, tm=tm, seq=seq)
        oc = _flash_causal(qc.reshape(bsz, seq, wide), kc_.reshape(bsz, seq, wide), vc_.reshape(bsz, seq, wide), tq=1024, hp=4)
        qn, nkc, nvc, nks, nvs, nkw, nvw, gates = _nsa_proj(h, lp["w_d"], lp["b_d"], tb["cn"], tb["sn"],
                                                            layer=l, tm=tm, seq=seq)
        kcmp, vcmp = _nsa_compress(nkc.reshape(bsz, seq, NSA_GROUPS * NSA_DIM), nvc.reshape(bsz, seq, NSA_GROUPS * NSA_DIM),
                                   lp["pe"], lp["wcmp_k"], lp["wcmp_kr"], lp["wcmp_v"], tb["cc"], tb["cs"], layer=l)
        od = _nsa_attention(qn.reshape(bsz, seq, NSA_HEADS * LANE), kcmp, vcmp,
                            nks.reshape(bsz, seq, gw), nvs.reshape(bsz, seq, gw),
                            nkw.reshape(bsz, seq, gw), nvw.reshape(bsz, seq, gw),
                            gates.reshape(bsz, seq, LANE), tb["ovl"], tq=512, top_k=top_k)
        h = _merge_ln(h, ab, oc.reshape(n, MLA_HEADS * MLA_V), od.reshape(n, NSA_HEADS * NSA_DIM), lp["w_g"], lp["b_g"],
                      lp["wout_c"], lp["wout_d"], lp["w_o"], row(ln2_g), row(ln2_b), layer=l, tm=tm_merge)
        kv = _linear(mem2, lp["xattn_wkv"], layer=l, tm=min(256, bsz * mlen), dtype=BF)
        h = _xattn_ln(h, kv.reshape(bsz, mlen, 2 * XATTN_HEADS * XATTN_DIM), lp["xattn_wq"], lp["xattn_wo"],
                      row(ln3_g), row(ln3_b), layer=l, tm=tm, seq=seq)
        h = _ffn_ln(h, *ffn2, row(ln4_g), row(ln4_b), layer=l, tm=tm, tf=tf)
    return h.reshape(bsz, seq, d)
```

```python
import functools

import jax
import jax.numpy as jnp
from jax import lax
from jax.experimental import pallas as pl
from jax.experimental.pallas import tpu as pltpu

BF = jnp.bfloat16
F32 = jnp.float32

D_MODEL = 1024
D_FF = 2816
LN_EPS = 1e-5
RMS_EPS = 1e-6
ROPE_THETA = 10000.0
DEPTH = 2
ALPHA = (2 * DEPTH) ** 0.25
NEG = -1e30
LOG2_E = 1.4426950408889634
DENOM_LANE = 64
MASK_BIG = 2.0 ** 100

GMLP_CHUNK = 128
GMLP_GROUPS = 4
GMLP_WIDTH = 512
CONV_WIDTH = 512
CONV_K = 3
MLA_HEADS = 8
MLA_Q_RANK = 256
MLA_KV_RANK = 128
MLA_NOPE = 64
MLA_ROPE = 32
MLA_V = 64
NSA_HEADS = 8
NSA_GROUPS = 2
NSA_HPG = 4
NSA_DIM = 64
CMP_BLOCK = 32
CMP_STRIDE = 16
SLC_BLOCK = 64
SLC_SHIFT = SLC_BLOCK.bit_length() - 1
SLC_TOPK = 8
WINDOW = 512
XATTN_HEADS = 4
XATTN_DIM = 128

LANE = 128
CONV_HALO = 8
VMEM_LIMIT = 56 * 1024 * 1024

_O_U, _O_V, _O_CB, _O_CC, _O_CH = 0, 512, 1024, 1536, 2048
_O_QLAT, _O_KVLAT, _O_KROPE = 2560, 2816, 2944
_O_NQ, _O_NKC, _O_NVC, _O_NKS, _O_NVS, _O_NKW, _O_NVW, _O_NGATE = 2976, 3488, 3616, 3744, 3872, 4000, 4128, 4256
_O_GA, _O_GB, _O_GC, _O_GD = 4280, 5304, 6328, 7352


def _dot(a, b):
    return jnp.dot(a, b, preferred_element_type=F32)


def _dot_t(a, b):
    return lax.dot_general(a, b, (((1,), (1,)), ((), ())), preferred_element_type=F32)


def _ln(y, g, b):
    mu = jnp.mean(y, -1, keepdims=True)
    d = y - mu
    var = jnp.mean(d * d, -1, keepdims=True)
    return d * lax.rsqrt(var + LN_EPS) * g + b


def _rms(x, g):
    return x * lax.rsqrt(jnp.mean(x * x, -1, keepdims=True) + RMS_EPS) * g


def _resident(shape):
    n = len(shape)
    return pl.BlockSpec(shape, lambda *_: (0,) * n, pipeline_mode=pl.Buffered(1))


def _layered(a, layer):
    n = a.ndim - 1
    return pl.BlockSpec((None,) + a.shape[1:], lambda *_: (layer,) + (0,) * n, pipeline_mode=pl.Buffered(1))


def _params(sem):
    return pltpu.CompilerParams(dimension_semantics=sem, vmem_limit_bytes=VMEM_LIMIT)


def _cast_kernel(x_ref, o_ref):
    o_ref[...] = x_ref[...].astype(o_ref.dtype)


def _to_bf16(w, *, rows=512):
    r, c = w.shape
    rows = min(rows, r)
    assert r % rows == 0
    return pl.pallas_call(
        _cast_kernel,
        grid=(r // rows,),
        in_specs=[pl.BlockSpec((rows, c), lambda i: (i, 0))],
        out_specs=pl.BlockSpec((rows, c), lambda i: (i, 0)),
        out_shape=jax.ShapeDtypeStruct((r, c), BF),
        compiler_params=_params(("parallel",)),
        name="to_bf16",
    )(w)


def _ffn_ln_kernel(x_ref, w1_ref, w3_ref, w2_ref, g_ref, b_ref, o_ref, *, tf):
    x = x_ref[...]
    xb = x.astype(BF)
    acc = None
    for c in range(w1_ref.shape[1] // tf):
        cols = slice(c * tf, (c + 1) * tf)
        h1 = _dot(xb, w1_ref[:, cols])
        h3 = _dot(xb, w3_ref[:, cols])
        hh = (h1 * jax.nn.sigmoid(h1)) * h3
        part = _dot(hh.astype(BF), w2_ref[cols, :])
        acc = part if acc is None else acc + part
    o_ref[...] = _ln(ALPHA * x + 0.5 * acc, g_ref[...], b_ref[...])


def _ffn_ln(x, w1, w3, w2, g, b, *, layer, tm, tf):
    n, d = x.shape
    return pl.pallas_call(
        functools.partial(_ffn_ln_kernel, tf=tf),
        grid=(n // tm,),
        in_specs=[pl.BlockSpec((tm, d), lambda i: (i, 0))] + [_layered(a, layer) for a in (w1, w3, w2, g, b)],
        out_specs=pl.BlockSpec((tm, d), lambda i: (i, 0)),
        out_shape=jax.ShapeDtypeStruct((n, d), F32),
        compiler_params=_params(("parallel",)),
        name="ffn_ln",
    )(x, w1, w3, w2, g, b)


def _ab_kernel(h_ref, w_ref, b_ref, lng_ref, lnb_ref, ws_ref, bst_ref, wga_ref, cw_ref, wcb_ref,
               o_ref, prev_ref, *, tiles_per_seq):
    i = pl.program_id(0)
    tm = h_ref.shape[0]
    hb = h_ref[...].astype(BF)

    def proj(c0, width):
        return _dot(hb, w_ref[:, c0:c0 + width]) + b_ref[:, c0:c0 + width]

    u = proj(0, GMLP_WIDTH)
    v = _ln(proj(512, GMLP_WIDTH), lng_ref[...], lnb_ref[...]).astype(BF)
    row = lax.broadcasted_iota(jnp.int32, (GMLP_CHUNK, GMLP_CHUNK), 0)
    col = lax.broadcasted_iota(jnp.int32, (GMLP_CHUNK, GMLP_CHUNK), 1)
    gd = GMLP_WIDTH // GMLP_GROUPS
    wgs = [jnp.where(row >= col, ws_ref[g], 0.0).astype(BF) for g in range(GMLP_GROUPS)]
    chunks = []
    for c in range(tm // GMLP_CHUNK):
        r0 = c * GMLP_CHUNK
        chunks.append(jnp.concatenate(
            [_dot(wgs[g], v[r0:r0 + GMLP_CHUNK, g * gd:(g + 1) * gd]) + bst_ref[:, g:g + 1]
             for g in range(GMLP_GROUPS)], axis=1))
    s = jnp.concatenate(chunks, axis=0)
    ya = _dot((u * s).astype(BF), wga_ref[...])

    cb = proj(1024, CONV_WIDTH)
    z = proj(1536, CONV_WIDTH) * proj(2048, CONV_WIDTH)

    @pl.when(i % tiles_per_seq == 0)
    def _():
        prev_ref[...] = jnp.zeros_like(prev_ref)

    zext = jnp.concatenate([prev_ref[...], z], axis=0)
    z1 = pltpu.roll(zext, 1, 0)[CONV_HALO:]
    z2 = pltpu.roll(zext, 2, 0)[CONV_HALO:]
    y = cw_ref[0:1, :] * z2 + cw_ref[1:2, :] * z1 + cw_ref[2:3, :] * z
    prev_ref[...] = z[tm - CONV_HALO:, :]
    yb = _dot((cb * y).astype(BF), wcb_ref[...])

    ga = proj(2560, D_MODEL)
    gb = proj(3584, D_MODEL)
    o_ref[...] = jax.nn.sigmoid(ga) * ya + jax.nn.sigmoid(gb) * yb


def _mix_ab(h, w, b, lng, lnb, ws, bst, wga, cw, wcb, *, layer, tm, seq):
    n, d = h.shape
    kern = functools.partial(_ab_kernel, tiles_per_seq=seq // tm)
    return pl.pallas_call(
        kern,
        grid=(n // tm,),
        in_specs=[pl.BlockSpec((tm, d), lambda i: (i, 0))]
        + [_layered(a, layer) for a in (w, b, lng, lnb, ws, bst, wga, cw, wcb)],
        out_specs=pl.BlockSpec((tm, d), lambda i: (i, 0)),
        out_shape=jax.ShapeDtypeStruct((n, d), F32),
        scratch_shapes=[pltpu.VMEM((CONV_HALO, CONV_WIDTH), F32)],
        compiler_params=_params(("arbitrary",)),
        name="mix_ab",
    )(h, w, b, lng, lnb, ws, bst, wga, cw, wcb)


def _mla_proj_kernel(h_ref, w_ref, b_ref, qg_ref, kvg_ref, wq_ref, wk_ref, wv_ref,
                     cq_ref, sq_ref, ck_ref, sk_ref, q_ref, k_ref, v_ref):
    hb = h_ref[...].astype(BF)
    z = _dot(hb, w_ref[...]) + b_ref[...]
    qn = _rms(z[:, 0:256], qg_ref[...]).astype(BF)
    kvn = _rms(z[:, 256:384], kvg_ref[...]).astype(BF)
    half = MLA_ROPE // 2

    def rotate_half(x, start):
        w = x.shape[1]
        first = lax.broadcasted_iota(jnp.int32, (1, w), 1) % LANE < start + half
        return jnp.where(first, -pltpu.roll(x, w - half, 1), pltpu.roll(x, half, 1))

    cq = jnp.concatenate([cq_ref[...]] * MLA_HEADS, axis=1)
    sq = jnp.concatenate([sq_ref[...]] * MLA_HEADS, axis=1)
    scale = (MLA_NOPE + MLA_ROPE) ** -0.5 * LOG2_E
    q = _dot(qn, wq_ref[...])
    q_ref[...] = ((q * cq + rotate_half(q, MLA_NOPE) * sq) * scale).astype(BF)
    kr = z[:, 384:512]
    kpe = pltpu.roll(kr * ck_ref[...] + rotate_half(kr, 0) * sk_ref[...], MLA_NOPE, 1)
    k_ref[...] = (_dot(kvn, wk_ref[...]) + jnp.concatenate([kpe] * MLA_HEADS, axis=1)).astype(BF)
    v_ref[...] = (_dot(kvn, wv_ref[...]) + _denom_ones(v_ref.shape[1])).astype(BF)


def _mla_proj(h, w, b, qg, kvg, wq, wk, wv, cq, sq, ck, sk, *, layer, tm, seq):
    n, d = h.shape
    tps = seq // tm
    tab = pl.BlockSpec((tm, LANE), lambda i: (i % tps, 0))
    wide = MLA_HEADS * LANE
    out = jax.ShapeDtypeStruct((n, wide), BF)
    return pl.pallas_call(
        _mla_proj_kernel,
        grid=(n // tm,),
        in_specs=[pl.BlockSpec((tm, d), lambda i: (i, 0))]
        + [_layered(a, layer) for a in (w, b, qg, kvg, wq, wk, wv)] + [tab] * 4,
        out_specs=[pl.BlockSpec((tm, wide), lambda i: (i, 0))] * 3,
        out_shape=[out, out, out],
        compiler_params=_params(("parallel",)),
        name="mla_proj",
    )(h, w, b, qg, kvg, wq, wk, wv, cq, sq, ck, sk)


def _online_softmax_step(s, v, carry):
    m, acc = carry
    m_new = jnp.maximum(m, jnp.max(s, -1, keepdims=True))
    p = jnp.exp2(s - m_new)
    acc = jnp.exp2(m - m_new) * acc + _dot(p.astype(BF), v)
    return m_new, acc


def _softmax_init(rows, width):
    return (jnp.full((rows, 1), NEG, F32), jnp.zeros((rows, width), F32))


def _normalize(acc):
    return acc * (1.0 / acc[:, DENOM_LANE:DENOM_LANE + 1])


def _pack_head_pairs(slots):
    low = lax.broadcasted_iota(jnp.int32, (1, LANE), 1) < DENOM_LANE
    return jnp.concatenate([jnp.where(low, a, pltpu.roll(b, DENOM_LANE, 1))
                            for a, b in zip(slots[0::2], slots[1::2])], axis=1)


def _denom_ones(width):
    lane = lax.broadcasted_iota(jnp.int32, (1, width), 1)
    return jnp.where(lane % LANE == DENOM_LANE, 1.0, 0.0)


def _flash_kernel(q_ref, k_ref, v_ref, o_ref, *, tq, hp):
    qi = pl.program_id(2)
    q0 = qi * tq
    qs = [q_ref[0, :, h * LANE:(h + 1) * LANE] for h in range(hp)]

    def tile(j, carries, width, diagonal):
        k0 = pl.multiple_of(j * width, width)
        out = []
        for h in range(hp):
            s = _dot_t(qs[h], k_ref[0, pl.ds(k0, width), h * LANE:(h + 1) * LANE])
            if diagonal:
                r = lax.broadcasted_iota(jnp.int32, (tq, width), 0)
                c = lax.broadcasted_iota(jnp.int32, (tq, width), 1)
                s = jnp.where(c <= r, s, NEG)
            out.append(_online_softmax_step(s, v_ref[0, pl.ds(k0, width), h * LANE:(h + 1) * LANE], carries[h]))
        return tuple(out)

    init = tuple(_softmax_init(tq, LANE) for _ in range(hp))
    carries = lax.fori_loop(0, qi, lambda j, c: tile(j, c, tq, False), init)

    half = tq // 2
    carries = tile(2 * qi, carries, half, True)
    r = lax.broadcasted_iota(jnp.int32, (half, half), 0)
    c = lax.broadcasted_iota(jnp.int32, (half, half), 1)
    k1 = pl.multiple_of(q0 + half, half)
    out = []
    for h in range(hp):
        m, acc = carries[h]
        s = _dot_t(qs[h][half:], k_ref[0, pl.ds(k1, half), h * LANE:(h + 1) * LANE])
        m2, acc2 = _online_softmax_step(jnp.where(c <= r, s, NEG), v_ref[0, pl.ds(k1, half), h * LANE:(h + 1) * LANE],
                                        (m[half:], acc[half:]))
        out.append(jnp.concatenate([acc[:half], acc2], axis=0))
    o_ref[0] = _pack_head_pairs([_normalize(acc) for acc in out]).astype(o_ref.dtype)


def _flash_causal(q, k, v, *, tq, hp):
    bsz, seq, wide = q.shape
    heads = wide // LANE
    assert seq % tq == 0 and hp % 2 == 0
    half_lane = LANE // 2
    kern = functools.partial(_flash_kernel, tq=tq, hp=hp)
    return pl.pallas_call(
        kern,
        grid=(bsz, heads // hp, seq // tq),
        in_specs=[
            pl.BlockSpec((1, tq, hp * LANE), lambda b, h, i: (b, i, h)),
            pl.BlockSpec((1, seq, hp * LANE), lambda b, h, i: (b, 0, h)),
            pl.BlockSpec((1, seq, hp * LANE), lambda b, h, i: (b, 0, h)),
        ],
        out_specs=pl.BlockSpec((1, tq, hp * half_lane), lambda b, h, i: (b, i, h)),
        out_shape=jax.ShapeDtypeStruct((bsz, seq, heads * half_lane), BF),
        compiler_params=_params(("parallel", "parallel", "arbitrary")),
        name="mla_flash",
    )(q, k, v)


def _nsa_proj_kernel(h_ref, w_ref, b_ref, c_ref, s_ref, q_ref, kc_ref, vc_ref, ks_ref, vs_ref, kw_ref, vw_ref, g_ref,
                     *, tiles_per_seq):
    tm = h_ref.shape[0]
    hb = h_ref[...].astype(BF)
    pos = (pl.program_id(0) % tiles_per_seq) * tm + lax.broadcasted_iota(jnp.int32, (tm, LANE), 0)
    lane = lax.broadcasted_iota(jnp.int32, (tm, LANE), 1)
    tag = jnp.where(lane == NSA_DIM + lax.shift_right_logical(pos, SLC_SHIFT), MASK_BIG, 0.0)
    tag2 = jnp.concatenate([tag] * NSA_GROUPS, axis=1)
    c = c_ref[...]
    s = s_ref[...]
    half = NSA_DIM // 2

    def rope(x):
        w = x.shape[1]
        first = lax.broadcasted_iota(jnp.int32, (1, w), 1) % NSA_DIM < half
        rot = jnp.where(first, -pltpu.roll(x, w - half, 1), pltpu.roll(x, half, 1))
        reps = w // LANE
        return x * jnp.concatenate([c] * reps, axis=1) + rot * jnp.concatenate([s] * reps, axis=1)

    low = lax.broadcasted_iota(jnp.int32, (1, LANE), 1) < NSA_DIM

    def spread(x):
        out = []
        for j in range(x.shape[1] // LANE):
            blk = x[:, j * LANE:(j + 1) * LANE]
            out += [jnp.where(low, blk, 0.0), jnp.where(low, pltpu.roll(blk, NSA_DIM, 1), 0.0)]
        return jnp.concatenate(out, axis=1)

    z = _dot(hb, w_ref[...]) + b_ref[...]
    q_ref[...] = (spread(rope(z[:, 0:512])) * (NSA_DIM ** -0.5 * LOG2_E)).astype(BF)
    kc_ref[...] = z[:, 512:640]
    vc_ref[...] = z[:, 640:768]
    ks_ref[...] = (spread(rope(z[:, 768:896])) + tag2).astype(BF)
    ones = _denom_ones(NSA_GROUPS * LANE)
    vs_ref[...] = (spread(z[:, 896:1024]) + ones).astype(BF)
    kw_ref[...] = spread(rope(z[:, 1024:1152])).astype(BF)
    vw_ref[...] = (spread(z[:, 1152:1280]) + ones).astype(BF)
    g_ref[...] = jax.nn.sigmoid(z[:, 1280:1408])


def _nsa_proj(h, w, b, cn, sn, *, layer, tm, seq):
    n, d = h.shape
    tps = seq // tm
    tab = pl.BlockSpec((tm, LANE), lambda i: (i % tps, 0))

    def out(width, dt):
        return pl.BlockSpec((tm, width), lambda i: (i, 0)), jax.ShapeDtypeStruct((n, width), dt)

    outs = [out(1024, BF), out(128, F32), out(128, F32), out(256, BF), out(256, BF), out(256, BF), out(256, BF), out(128, F32)]
    assert seq // SLC_BLOCK <= LANE - NSA_DIM
    return pl.pallas_call(
        functools.partial(_nsa_proj_kernel, tiles_per_seq=tps),
        grid=(n // tm,),
        in_specs=[pl.BlockSpec((tm, d), lambda i: (i, 0)), _layered(w, layer), _layered(b, layer), tab, tab],
        out_specs=[o[0] for o in outs],
        out_shape=[o[1] for o in outs],
        compiler_params=_params(("parallel",)),
        name="nsa_proj",
    )(h, w, b, cn, sn)


def _nsa_cmp_kernel(kc_ref, vc_ref, pe_ref, wk_ref, wkr_ref, wv_ref, c_ref, s_ref, kcmp_ref, vcmp_ref):
    n16 = kcmp_ref.shape[1]
    gl = NSA_GROUPS * NSA_DIM
    kc = kcr = vc = None
    for l in range(CMP_STRIDE):
        rows = slice(l * gl, (l + 1) * gl)
        xk = kc_ref[0, pl.ds(l, n16, stride=CMP_STRIDE), :]
        xv = vc_ref[0, pl.ds(l, n16, stride=CMP_STRIDE), :]
        terms = [((xk + pe_ref[0:1, rows]).astype(BF), (xv + pe_ref[2:3, rows]).astype(BF), 0),
                 ((pltpu.roll(xk, n16 - 1, 0) + pe_ref[1:2, rows]).astype(BF),
                  (pltpu.roll(xv, n16 - 1, 0) + pe_ref[3:4, rows]).astype(BF), 1)]
        for ak, av, part in terms:
            pk, pkr, pv = _dot(ak, wk_ref[part, rows, :]), _dot(ak, wkr_ref[part, rows, :]), _dot(av, wv_ref[part, rows, :])
            kc, kcr, vc = (pk, pkr, pv) if kc is None else (kc + pk, kcr + pkr, vc + pv)
    kcmp_ref[0] = (kc * c_ref[...] + kcr * s_ref[...]).astype(BF)
    vcmp_ref[0] = vc.astype(BF)


def _nsa_compress(kc, vc, pe, wk, wkr, wv, cc, sc, *, layer):
    bsz, seq, wide = kc.shape
    n16 = seq // CMP_STRIDE
    blk = pl.BlockSpec((1, seq, wide), lambda b: (b, 0, 0))
    oblk = pl.BlockSpec((1, n16, NSA_GROUPS * LANE), lambda b: (b, 0, 0))
    osh = jax.ShapeDtypeStruct((bsz, n16, NSA_GROUPS * LANE), BF)
    return pl.pallas_call(
        _nsa_cmp_kernel,
        grid=(bsz,),
        in_specs=[blk, blk] + [_layered(a, layer) for a in (pe, wk, wkr, wv)] + [_resident(cc.shape), _resident(sc.shape)],
        out_specs=[oblk, oblk],
        out_shape=[osh, osh],
        compiler_params=_params(("parallel",)),
        name="nsa_compress",
    )(kc, vc, pe, wk, wkr, wv, cc, sc)


def _nsa_attn_kernel(q_ref, kcmp_ref, vcmp_ref, ks_ref, vs_ref, kw_ref, vw_ref, g_ref, ov_ref, o_ref, *, top_k):
    qi = pl.program_id(1)
    T = q_ref.shape[1]
    Th = T // 2
    R = NSA_HPG * T
    Rh = R // 2
    G = NSA_GROUPS
    q0 = qi * T
    qpos = lax.broadcasted_iota(jnp.int32, (T, 1), 0) + q0
    ncp = kcmp_ref.shape[1]
    nb = ov_ref.shape[0]
    ov_t = ov_ref[...]

    def add_per_query(x, b):
        w = x.shape[1]
        return (x.reshape(2, NSA_HPG, Th, w) + b.reshape(2, 1, Th, w)).reshape(R, w)

    def add_per_query_half(x, b):
        w = x.shape[1]
        return (x.reshape(NSA_HPG, Th, w) + b[None]).reshape(Rh, w)

    def lanes(g):
        return slice(g * LANE, (g + 1) * LANE)

    q4 = [jnp.concatenate([q_ref[0, half * Th:(half + 1) * Th, (g * NSA_HPG + h) * LANE:(g * NSA_HPG + h + 1) * LANE]
                           for half in range(2) for h in range(NSA_HPG)], axis=0) for g in range(G)]

    cmp_end = lax.broadcasted_iota(jnp.int32, (1, ncp), 1) * CMP_STRIDE + (CMP_BLOCK - 1)
    cbias = jnp.where(cmp_end <= qpos, 0.0, NEG)
    any_valid = jnp.where(qpos >= CMP_BLOCK - 1, 1.0, 0.0)
    jr = lax.broadcasted_iota(jnp.int32, (nb, 1), 0)
    jrf = jr.astype(F32)
    jq = lax.shift_right_logical(lax.broadcasted_iota(jnp.int32, (1, T), 1) + q0, SLC_SHIFT)
    forced = (jr == 0) | (jr == jq) | (jr == jq - 1)
    eye_t = jnp.where(lax.broadcasted_iota(jnp.int32, (T, T), 0) == lax.broadcasted_iota(jnp.int32, (T, T), 1),
                      1.0, 0.0).astype(BF)
    o_cmp, q4s = [], []
    for g in range(G):
        sm = _dot_t(q4[g], kcmp_ref[0, :, lanes(g)]).reshape(2, NSA_HPG, Th, ncp) + cbias.reshape(2, 1, Th, ncp)
        e = jnp.exp2(sm - jnp.max(sm, -1, keepdims=True))
        p = e * (any_valid.reshape(2, 1, Th, 1) / jnp.sum(e, -1, keepdims=True))
        o_cmp.append(_dot(p.reshape(R, ncp).astype(BF), vcmp_ref[0, :, lanes(g)]))
        psum = (p[:, 0] + p[:, 1] + p[:, 2] + p[:, 3]).reshape(T, ncp)
        hi = psum.astype(BF)
        r1 = psum - hi.astype(F32)
        mid = r1.astype(BF)
        lo = (r1 - mid.astype(F32)).astype(BF)
        imp = _dot_t(ov_t, hi) + _dot_t(ov_t, mid) + _dot_t(ov_t, lo)
        imp = jnp.where(forced, 1e9, imp)
        imp = jnp.where(jr <= jq, imp, -1.0)
        work = imp
        sel = jnp.zeros_like(imp)
        for _ in range(top_k):
            mx = jnp.max(work, 0, keepdims=True)
            idx = jnp.min(jnp.where(work == mx, jrf, float(nb)), 0, keepdims=True)
            pick = jrf == idx
            sel = jnp.where(pick, 1.0, sel)
            work = jnp.where(pick, -2.0, work)
        unsel_t = jnp.where(imp >= 0.0, sel, 0.0) - 1.0
        pad_t = [jnp.zeros((NSA_DIM, T), F32), unsel_t]
        if LANE - NSA_DIM - nb:
            pad_t.append(jnp.zeros((LANE - NSA_DIM - nb, T), F32))
        unsel = _dot_t(eye_t, jnp.concatenate(pad_t, axis=0).astype(BF)).astype(BF)
        q4s.append(add_per_query(q4[g], unsel))

    def slc_tile(j, carries):
        k0 = pl.multiple_of(j * T, T)
        return tuple(_online_softmax_step(_dot_t(q4s[g], ks_ref[0, pl.ds(k0, T), lanes(g)]),
                                          vs_ref[0, pl.ds(k0, T), lanes(g)], carries[g]) for g in range(G))

    carries = lax.fori_loop(0, qi, slc_tile, tuple(_softmax_init(R, LANE) for _ in range(G)))

    def causal_bias(k0, qp, nk):
        return jnp.where((lax.broadcasted_iota(jnp.int32, (1, nk), 1) + k0) <= qp, 0.0, NEG)

    k_a = pl.multiple_of(q0, Th)
    k_b = pl.multiple_of(q0 + Th, Th)
    bias_a = causal_bias(k_a, qpos, Th)
    bias_b = causal_bias(k_b, qpos[Th:], Th)
    slc_acc = []
    for g in range(G):
        sc = add_per_query(_dot_t(q4s[g], ks_ref[0, pl.ds(k_a, Th), lanes(g)]), bias_a)
        m, acc = _online_softmax_step(sc, vs_ref[0, pl.ds(k_a, Th), lanes(g)], carries[g])
        sc = add_per_query_half(_dot_t(q4s[g][Rh:], ks_ref[0, pl.ds(k_b, Th), lanes(g)]), bias_b)
        _, acc_b = _online_softmax_step(sc, vs_ref[0, pl.ds(k_b, Th), lanes(g)], (m[Rh:], acc[Rh:]))
        slc_acc.append((acc[:Rh], acc_b))

    wk = WINDOW + Th
    o_win = []
    for g in range(G):
        parts = []
        for half in range(2):
            w0 = pl.multiple_of(jnp.maximum(q0 + half * Th - WINDOW, 0), Th)
            dist = qpos[half * Th:(half + 1) * Th] - (lax.broadcasted_iota(jnp.int32, (1, wk), 1) + w0)
            wbias = jnp.where((dist >= 0) & (dist < WINDOW), 0.0, NEG)
            sc = add_per_query_half(_dot_t(q4[g][half * Rh:(half + 1) * Rh], kw_ref[0, pl.ds(w0, wk), lanes(g)]), wbias)
            e = jnp.exp2(sc - jnp.max(sc, -1, keepdims=True))
            parts.append(_normalize(_dot(e.astype(BF), vw_ref[0, pl.ds(w0, wk), lanes(g)])))
        o_win.append(parts)

    gw = NSA_HPG * LANE
    e_row = lax.broadcasted_iota(jnp.int32, (LANE, 3 * gw), 0)
    e_col = lax.broadcasted_iota(jnp.int32, (LANE, 3 * gw), 1)
    branch = jnp.where(e_col >= 2 * gw, 2, jnp.where(e_col >= gw, 1, 0))
    head = lax.shift_right_logical(e_col - branch * gw, LANE.bit_length() - 1)
    gate_col = 3 * head + branch

    def heads_on_lanes(x):
        return jnp.concatenate([x[h * Th:(h + 1) * Th] for h in range(NSA_HPG)], axis=1)

    gs = g_ref[0]
    hi = gs.astype(BF)
    lo = (gs - hi.astype(F32)).astype(BF)
    hi_lo = jnp.concatenate([hi, lo], axis=1)
    packed = NSA_HPG * NSA_DIM
    for g in range(G):
        expand = jnp.where(e_row == gate_col + g * (3 * NSA_HPG), 1.0, 0.0).astype(BF)
        gx = _dot(hi_lo, jnp.concatenate([expand, expand], axis=0))
        for half in range(2):
            rows = slice(half * Th, (half + 1) * Th)
            mixed = (gx[rows, 0:gw] * heads_on_lanes(o_cmp[g][half * Rh:(half + 1) * Rh])
                     + gx[rows, gw:2 * gw] * heads_on_lanes(_normalize(slc_acc[g][half]))
                     + gx[rows, 2 * gw:3 * gw] * heads_on_lanes(o_win[g][half]))
            o_ref[0, rows, g * packed:(g + 1) * packed] = _pack_head_pairs(
                [mixed[:, h * LANE:(h + 1) * LANE] for h in range(NSA_HPG)]).astype(o_ref.dtype)


def _nsa_attention(q, kcmp, vcmp, ks, vs, kw, vw, gates, ov, *, tq, top_k):
    bsz, seq, wide = q.shape
    n16 = kcmp.shape[1]
    gw = NSA_GROUPS * LANE
    assert seq % tq == 0 and seq >= WINDOW + tq
    kern = functools.partial(_nsa_attn_kernel, top_k=top_k)
    cblk = pl.BlockSpec((1, n16, gw), lambda b, i: (b, 0, 0))
    sblk = pl.BlockSpec((1, seq, gw), lambda b, i: (b, 0, 0))
    return pl.pallas_call(
        kern,
        grid=(bsz, seq // tq),
        in_specs=[pl.BlockSpec((1, tq, wide), lambda b, i: (b, i, 0)), cblk, cblk, sblk, sblk, sblk, sblk,
                  pl.BlockSpec((1, tq, LANE), lambda b, i: (b, i, 0)), _resident(ov.shape)],
        out_specs=pl.BlockSpec((1, tq, wide // 2), lambda b, i: (b, i, 0)),
        out_shape=jax.ShapeDtypeStruct((bsz, seq, wide // 2), BF),
        compiler_params=_params(("parallel", "arbitrary")),
        name="nsa_attn",
    )(q, kcmp, vcmp, ks, vs, kw, vw, gates, ov)


def _merge_kernel(x_ref, ab_ref, oc_ref, od_ref, wg_ref, bg_ref, wc_ref, wd_ref, wo_ref, g_ref, b_ref, o_ref):
    x = x_ref[...]
    gates = _dot(x.astype(BF), wg_ref[...]) + bg_ref[...]
    yc = _dot(oc_ref[...], wc_ref[...])
    yd = _dot(od_ref[...], wd_ref[...])
    merged = ab_ref[...] + jax.nn.sigmoid(gates[:, :D_MODEL]) * yc + jax.nn.sigmoid(gates[:, D_MODEL:]) * yd
    mix = _dot(merged.astype(BF), wo_ref[...])
    o_ref[...] = _ln(ALPHA * x + mix, g_ref[...], b_ref[...])


def _merge_ln(x, ab, oc, od, wg, bg, wc, wd, wo, g, b, *, layer, tm):
    n, d = x.shape
    row = pl.BlockSpec((tm, d), lambda i: (i, 0))
    rows = lambda a: pl.BlockSpec((tm, a.shape[1]), lambda i: (i, 0))
    return pl.pallas_call(
        _merge_kernel,
        grid=(n // tm,),
        in_specs=[row, row, rows(oc), rows(od)] + [_layered(a, layer) for a in (wg, bg, wc, wd, wo, g, b)],
        out_specs=row,
        out_shape=jax.ShapeDtypeStruct((n, d), F32),
        compiler_params=_params(("parallel",)),
        name="merge_ln",
    )(x, ab, oc, od, wg, bg, wc, wd, wo, g, b)


def _linear_kernel(x_ref, w_ref, o_ref):
    o_ref[...] = _dot(x_ref[...].astype(BF), w_ref[...]).astype(o_ref.dtype)


def _linear(x, w, *, layer, tm, dtype):
    n, d = x.shape
    return pl.pallas_call(
        _linear_kernel,
        grid=(n // tm,),
        in_specs=[pl.BlockSpec((tm, d), lambda i: (i, 0)), _layered(w, layer)],
        out_specs=pl.BlockSpec((tm, w.shape[2]), lambda i: (i, 0)),
        out_shape=jax.ShapeDtypeStruct((n, w.shape[2]), dtype),
        compiler_params=_params(("parallel",)),
        name="mem_kv",
    )(x, w)


def _xattn_kernel(x_ref, k_ref, v_ref, wq_ref, wo_ref, g_ref, b_ref, o_ref):
    x = x_ref[...]
    q = _dot(x.astype(BF), wq_ref[...]).astype(BF)
    k = k_ref[0]
    v = v_ref[0]
    heads = []
    for h in range(XATTN_HEADS):
        sl = slice(h * XATTN_DIM, (h + 1) * XATTN_DIM)
        s = _dot_t(q[:, sl], k[:, sl]) * (XATTN_DIM ** -0.5 * LOG2_E)
        e = jnp.exp2(s - jnp.max(s, -1, keepdims=True))
        heads.append(_dot(e.astype(BF), v[:, sl]) * (1.0 / jnp.sum(e, -1, keepdims=True)))
    o = jnp.concatenate(heads, axis=1).astype(BF)
    o_ref[...] = _ln(ALPHA * x + _dot(o, wo_ref[...]), g_ref[...], b_ref[...])


def _xattn_ln(x, kv, wq, wo, g, b, *, layer, tm, seq):
    n, d = x.shape
    tps = seq // tm
    mlen = kv.shape[1]
    hd = XATTN_HEADS * XATTN_DIM
    return pl.pallas_call(
        _xattn_kernel,
        grid=(n // tm,),
        in_specs=[pl.BlockSpec((tm, d), lambda i: (i, 0)),
                  pl.BlockSpec((1, mlen, hd), lambda i: (i // tps, 0, 0)),
                  pl.BlockSpec((1, mlen, hd), lambda i: (i // tps, 0, 1))]
        + [_layered(a, layer) for a in (wq, wo, g, b)],
        out_specs=pl.BlockSpec((tm, d), lambda i: (i, 0)),
        out_shape=jax.ShapeDtypeStruct((n, d), F32),
        compiler_params=_params(("parallel",)),
        name="xattn_ln",
    )(x, kv, kv, wq, wo, g, b)


def _rope_tab(pos, dim):
    inv = ROPE_THETA ** (-(jnp.arange(0, dim, 2, dtype=F32) / dim))
    ang = pos[:, None] * inv[None, :]
    return jnp.cos(ang), jnp.sin(ang)


def _rot_cols(w, half):
    return jnp.concatenate([-w[..., half:2 * half], w[..., :half]], axis=-1)


def _pad_slots(w, n_slots, width):
    lead = w.shape[:-1]
    w = w.reshape(lead + (n_slots, width))
    w = jnp.pad(w, [(0, 0)] * len(lead) + [(0, 0), (0, LANE - width)])
    return w.reshape(lead + (n_slots * LANE,))


def _layer_params(p):
    w_in, b_in = p["w_in"], p["b_in"]

    def cols(o, wd):
        return w_in[:, o:o + wd], b_in[o:o + wd]

    out = {}
    out["w_ab"] = jnp.concatenate([w_in[:, 0:2560], w_in[:, _O_GA:_O_GC]], axis=1).astype(BF)
    out["b_ab"] = jnp.concatenate([b_in[0:2560], b_in[_O_GA:_O_GC]])[None, :]
    wkr, bkr = cols(_O_KROPE, MLA_ROPE)
    padk =lambda a: jnp.pad(a, [(0, 0)] * (a.ndim - 1) + [(0, LANE - MLA_ROPE)])
    out["w_c"] = jnp.concatenate([w_in[:, _O_QLAT:_O_KROPE], padk(wkr)], axis=1).astype(BF)
    out["b_c"] = jnp.concatenate([b_in[_O_QLAT:_O_KROPE], padk(bkr)])[None, :]
    wuq = p["mla_wuq"].reshape(MLA_Q_RANK, MLA_HEADS, MLA_NOPE + MLA_ROPE)
    wq_c = jnp.pad(wuq, [(0, 0), (0, 0), (0, LANE - MLA_NOPE - MLA_ROPE)])
    out["wq_c"] = wq_c.reshape(MLA_Q_RANK, MLA_HEADS * LANE).astype(BF)
    wukv = p["mla_wukv"].reshape(MLA_KV_RANK, MLA_HEADS, MLA_NOPE + MLA_V)
    out["wk_c"] = jnp.pad(wukv[..., :MLA_NOPE], [(0, 0), (0, 0), (0, LANE - MLA_NOPE)]).reshape(MLA_KV_RANK, -1).astype(BF)
    out["wv_c"] = jnp.pad(wukv[..., MLA_NOPE:], [(0, 0), (0, 0), (0, LANE - MLA_V)]).reshape(MLA_KV_RANK, -1).astype(BF)
    hd = NSA_DIM // 2
    n_gate = NSA_HEADS * 3
    out["w_d"] = jnp.pad(w_in[:, _O_NQ:_O_NGATE + n_gate], [(0, 0), (0, LANE - n_gate)]).astype(BF)
    out["b_d"] = jnp.pad(b_in[_O_NQ:_O_NGATE + n_gate], [(0, LANE - n_gate)])[None, :]

    def cmp_weights(w):
        eye = jnp.eye(NSA_GROUPS, dtype=F32)
        wp = jnp.pad(w, [(0, 0), (0, 0), (0, LANE - NSA_DIM)])
        full = jnp.einsum("lde,gh->lgdhe", wp, eye).reshape(CMP_BLOCK, NSA_GROUPS * NSA_DIM, NSA_GROUPS * LANE)
        return full.reshape(2, CMP_STRIDE * NSA_GROUPS * NSA_DIM, NSA_GROUPS * LANE).astype(BF)

    wck = p["nsa_wcmp_k"]
    out["wcmp_k"] = cmp_weights(wck)
    out["wcmp_kr"] = cmp_weights(_rot_cols(wck, hd))
    out["wcmp_v"] = cmp_weights(p["nsa_wcmp_v"])

    def pe_rows(pe):
        t = jnp.broadcast_to(pe[:, None, :], (CMP_BLOCK, NSA_GROUPS, NSA_DIM))
        return t.reshape(2, CMP_STRIDE * NSA_GROUPS * NSA_DIM)

    out["pe"] = jnp.concatenate([pe_rows(p["nsa_pe_k"]), pe_rows(p["nsa_pe_v"])], axis=0)
    out["w_g"] = w_in[:, _O_GC:].astype(BF)
    out["b_g"] = b_in[_O_GC:][None, :]
    out["wout_c"] = p["mla_wout"].astype(BF)
    out["wout_d"] = p["nsa_wout"].astype(BF)
    out["gmlp_bs_t"] = p["gmlp_bs"].T
    for name in ("gmlp_wout", "conv_wout", "w_o", "xattn_wq", "xattn_wo"):
        out[name] = p[name].astype(BF)
    out["xattn_wkv"] = jnp.concatenate([p["xattn_wk"], p["xattn_wv"]], axis=1).astype(BF)
    return out


def _tables(seq):
    pos = jnp.arange(seq, dtype=F32)
    c16, s16 = _rope_tab(pos, MLA_ROPE)
    one = jnp.ones((seq, MLA_NOPE), F32)
    zero = jnp.zeros((seq, MLA_NOPE), F32)
    tail = LANE - MLA_NOPE - MLA_ROPE
    cq = jnp.concatenate([one, c16, c16, jnp.ones((seq, tail), F32)], axis=1)
    sq = jnp.concatenate([zero, s16, s16, jnp.zeros((seq, tail), F32)], axis=1)
    ck = jnp.pad(jnp.concatenate([c16, c16], axis=1), [(0, 0), (0, LANE - MLA_ROPE)])
    sk = jnp.pad(jnp.concatenate([s16, s16], axis=1), [(0, 0), (0, LANE - MLA_ROPE)])
    c32, s32 = _rope_tab(pos, NSA_DIM)
    cn = jnp.concatenate([c32, c32] * (LANE // NSA_DIM), axis=1)
    sn = jnp.concatenate([s32, s32] * (LANE // NSA_DIM), axis=1)
    n16 = seq // CMP_STRIDE
    cend = (jnp.arange(n16) * CMP_STRIDE + CMP_BLOCK - 1).astype(F32)
    cc32, cs32 = _rope_tab(cend, NSA_DIM)
    ccg = jnp.pad(jnp.concatenate([cc32, cc32], axis=1), [(0, 0), (0, LANE - NSA_DIM)])
    csg = jnp.pad(jnp.concatenate([cs32, cs32], axis=1), [(0, 0), (0, LANE - NSA_DIM)])
    cc = jnp.concatenate([ccg] * NSA_GROUPS, axis=1)
    cs = jnp.concatenate([csg] * NSA_GROUPS, axis=1)
    n_cmp = (seq - CMP_BLOCK) // CMP_STRIDE + 1
    n_slc = seq // SLC_BLOCK
    cstart = jnp.arange(n16) * CMP_STRIDE
    sstart = jnp.arange(n_slc) * SLC_BLOCK
    ovl = (jnp.minimum(cstart[None, :] + CMP_BLOCK, sstart[:, None] + SLC_BLOCK)
           - jnp.maximum(cstart[None, :], sstart[:, None]))
    ovl = jnp.clip(ovl, 0).astype(F32) / CMP_BLOCK
    ovl = jnp.where(jnp.arange(n16)[None, :] < n_cmp, ovl, 0.0).astype(BF)
    return dict(cq=cq, sq=sq, ck=ck, sk=sk, cn=cn, sn=sn, cc=cc, cs=cs, ovl=ovl)


def kernel(x, mem, ffn1_w1, ffn1_w3, ffn1_w2, ln1_g, ln1_b, w_in, b_in, gmlp_ln_g, gmlp_ln_b, gmlp_ws, gmlp_bs, gmlp_wout, conv_w, conv_wout, mla_qnorm_g, mla_kvnorm_g, mla_wuq, mla_wukv, mla_wout, nsa_pe_k, nsa_pe_v, nsa_wcmp_k, nsa_wcmp_v, nsa_wout, w_o, ln2_g, ln2_b, xattn_wq, xattn_wk, xattn_wv, xattn_wo, ln3_g, ln3_b, ffn2_w1, ffn2_w3, ffn2_w2, ln4_g, ln4_b):
    bsz, seq, d = x.shape
    mlen = mem.shape[1]
    n = bsz * seq
    assert d == D_MODEL and seq % 1024 == 0
    lp = jax.vmap(_layer_params)(dict(
        w_in=w_in, b_in=b_in, mla_wuq=mla_wuq, mla_wukv=mla_wukv, mla_wout=mla_wout, nsa_pe_k=nsa_pe_k,
        nsa_pe_v=nsa_pe_v, nsa_wcmp_k=nsa_wcmp_k, nsa_wcmp_v=nsa_wcmp_v, nsa_wout=nsa_wout, gmlp_bs=gmlp_bs,
        gmlp_wout=gmlp_wout, conv_wout=conv_wout, w_o=w_o, xattn_wq=xattn_wq, xattn_wk=xattn_wk,
        xattn_wv=xattn_wv, xattn_wo=xattn_wo))
    tb = _tables(seq)
    tm = 1024
    tm_merge = 512
    tf = D_FF // 11
    top_k = min(SLC_TOPK, seq // SLC_BLOCK)
    row = lambda a: a[:, None, :]

    def cast_stacked(w):
        return _to_bf16(w.reshape(w.shape[0] * w.shape[1], w.shape[2])).reshape(w.shape)

    ffn1 = [cast_stacked(w) for w in (ffn1_w1, ffn1_w3, ffn1_w2)]
    ffn2 = [cast_stacked(w) for w in (ffn2_w1, ffn2_w3, ffn2_w2)]
    h = x.reshape(n, d)
    mem2 = mem.reshape(bsz * mlen, d)
    wide = MLA_HEADS * LANE
    gw = NSA_GROUPS * LANE
    for l in range(DEPTH):
        h = _ffn_ln(h, *ffn1, row(ln1_g), row(ln1_b), layer=l, tm=tm, tf=tf)
        ab = _mix_ab(h, lp["w_ab"], lp["b_ab"], row(gmlp_ln_g), row(gmlp_ln_b), gmlp_ws, lp["gmlp_bs_t"],
                     lp["gmlp_wout"], conv_w, lp["conv_wout"], layer=l, tm=tm, seq=seq)
        qc, kc_, vc_ = _mla_proj(h, lp["w_c"], lp["b_c"], row(mla_qnorm_g), row(mla_kvnorm_g),
                                 lp["wq_c"], lp["wk_c"], lp["wv_c"],
                                 tb["cq"], tb["sq"], tb["ck"], tb["sk"], layer=l, tm=tm, seq=seq)
        oc = _flash_causal(qc.reshape(bsz, seq, wide), kc_.reshape(bsz, seq, wide), vc_.reshape(bsz, seq, wide), tq=1024, hp=4)
        qn, nkc, nvc, nks, nvs, nkw, nvw, gates = _nsa_proj(h, lp["w_d"], lp["b_d"], tb["cn"], tb["sn"],
                                                            layer=l, tm=tm, seq=seq)
        kcmp, vcmp = _nsa_compress(nkc.reshape(bsz, seq, NSA_GROUPS * NSA_DIM), nvc.reshape(bsz, seq, NSA_GROUPS * NSA_DIM),
                                   lp["pe"], lp["wcmp_k"], lp["wcmp_kr"], lp["wcmp_v"], tb["cc"], tb["cs"], layer=l)
        od = _nsa_attention(qn.reshape(bsz, seq, NSA_HEADS * LANE), kcmp, vcmp,
                            nks.reshape(bsz, seq, gw), nvs.reshape(bsz, seq, gw),
                            nkw.reshape(bsz, seq, gw), nvw.reshape(bsz, seq, gw),
                            gates.reshape(bsz, seq, LANE), tb["ovl"], tq=512, top_k=top_k)
        h = _merge_ln(h, ab, oc.reshape(n, MLA_HEADS * MLA_V), od.reshape(n, NSA_HEADS * NSA_DIM), lp["w_g"], lp["b_g"],
                      lp["wout_c"], lp["wout_d"], lp["w_o"], row(ln2_g), row(ln2_b), layer=l, tm=tm_merge)
        kv = _linear(mem2, lp["xattn_wkv"], layer=l, tm=min(256, bsz * mlen), dtype=BF)
        h = _xattn_ln(h, kv.reshape(bsz, mlen, 2 * XATTN_HEADS * XATTN_DIM), lp["xattn_wq"], lp["xattn_wo"],
                      row(ln3_g), row(ln3_b), layer=l, tm=tm, seq=seq)
        h = _ffn_ln(h, *ffn2, row(ln4_g), row(ln4_b), layer=l, tm=tm, tf=tf)
    return h.reshape(bsz, seq, d)
```

```python
import functools

import jax
import jax.numpy as jnp
from jax import lax
from jax.experimental import pallas as pl
from jax.experimental.pallas import tpu as pltpu

BF = jnp.bfloat16
F32 = jnp.float32

D_MODEL = 1024
D_FF = 2816
LN_EPS = 1e-5
RMS_EPS = 1e-6
ROPE_THETA = 10000.0
DEPTH = 2
ALPHA = (2 * DEPTH) ** 0.25
NEG = -1e30
LOG2_E = 1.4426950408889634
DENOM_LANE = 64
MASK_BIG = 2.0 ** 100

GMLP_CHUNK = 128
GMLP_GROUPS = 4
GMLP_WIDTH = 512
CONV_WIDTH = 512
CONV_K = 3
MLA_HEADS = 8
MLA_Q_RANK = 256
MLA_KV_RANK = 128
MLA_NOPE = 64
MLA_ROPE = 32
MLA_V = 64
NSA_HEADS = 8
NSA_GROUPS = 2
NSA_HPG = 4
NSA_DIM = 64
CMP_BLOCK = 32
CMP_STRIDE = 16
SLC_BLOCK = 64
SLC_SHIFT = SLC_BLOCK.bit_length() - 1
SLC_TOPK = 8
WINDOW = 512
XATTN_HEADS = 4
XATTN_DIM = 128

LANE = 128
CONV_HALO = 8
VMEM_LIMIT = 56 * 1024 * 1024

_O_U, _O_V, _O_CB, _O_CC, _O_CH = 0, 512, 1024, 1536, 2048
_O_QLAT, _O_KVLAT, _O_KROPE = 2560, 2816, 2944
_O_NQ, _O_NKC, _O_NVC, _O_NKS, _O_NVS, _O_NKW, _O_NVW, _O_NGATE = 2976, 3488, 3616, 3744, 3872, 4000, 4128, 4256
_O_GA, _O_GB, _O_GC, _O_GD = 4280, 5304, 6328, 7352


def _dot(a, b):
    return jnp.dot(a, b, preferred_element_type=F32)


def _dot_t(a, b):
    return lax.dot_general(a, b, (((1,), (1,)), ((), ())), preferred_element_type=F32)


def _ln(y, g, b):
    mu = jnp.mean(y, -1, keepdims=True)
    d = y - mu
    var = jnp.mean(d * d, -1, keepdims=True)
    return d * lax.rsqrt(var + LN_EPS) * g + b


def _rms(x, g):
    return x * lax.rsqrt(jnp.mean(x * x, -1, keepdims=True) + RMS_EPS) * g


def _resident(shape):
    n = len(shape)
    return pl.BlockSpec(shape, lambda *_: (0,) * n, pipeline_mode=pl.Buffered(1))


def _layered(a, layer):
    n = a.ndim - 1
    return pl.BlockSpec((None,) + a.shape[1:], lambda *_: (layer,) + (0,) * n, pipeline_mode=pl.Buffered(1))


def _params(sem):
    return pltpu.CompilerParams(dimension_semantics=sem, vmem_limit_bytes=VMEM_LIMIT)


def _cast_kernel(x_ref, o_ref):
    o_ref[...] = x_ref[...].astype(o_ref.dtype)


def _to_bf16(w, *, rows=512):
    r, c = w.shape
    rows = min(rows, r)
    assert r % rows == 0
    return pl.pallas_call(
        _cast_kernel,
        grid=(r // rows,),
        in_specs=[pl.BlockSpec((rows, c), lambda i: (i, 0))],
        out_specs=pl.BlockSpec((rows, c), lambda i: (i, 0)),
        out_shape=jax.ShapeDtypeStruct((r, c), BF),
        compiler_params=_params(("parallel",)),
        name="to_bf16",
    )(w)


def _ffn_ln_kernel(x_ref, w1_ref, w3_ref, w2_ref, g_ref, b_ref, o_ref, *, tf):
    x = x_ref[...]
    xb = x.astype(BF)
    acc = None
    for c in range(w1_ref.shape[1] // tf):
        cols = slice(c * tf, (c + 1) * tf)
        h1 = _dot(xb, w1_ref[:, cols])
        h3 = _dot(xb, w3_ref[:, cols])
        hh = (h1 * jax.nn.sigmoid(h1)) * h3
        part = _dot(hh.astype(BF), w2_ref[cols, :])
        acc = part if acc is None else acc + part
    o_ref[...] = _ln(ALPHA * x + 0.5 * acc, g_ref[...], b_ref[...])


def _ffn_ln(x, w1, w3, w2, g, b, *, layer, tm, tf):
    n, d = x.shape
    return pl.pallas_call(
        functools.partial(_ffn_ln_kernel, tf=tf),
        grid=(n // tm,),
        in_specs=[pl.BlockSpec((tm, d), lambda i: (i, 0))] + [_layered(a, layer) for a in (w1, w3, w2, g, b)],
        out_specs=pl.BlockSpec((tm, d), lambda i: (i, 0)),
        out_shape=jax.ShapeDtypeStruct((n, d), F32),
        compiler_params=_params(("parallel",)),
        name="ffn_ln",
    )(x, w1, w3, w2, g, b)


def _ab_kernel(h_ref, w_ref, b_ref, lng_ref, lnb_ref, ws_ref, bst_ref, wga_ref, cw_ref, wcb_ref,
               o_ref, prev_ref, *, tiles_per_seq):
    i = pl.program_id(0)
    tm = h_ref.shape[0]
    hb = h_ref[...].astype(BF)

    def proj(c0, width):
        return _dot(hb, w_ref[:, c0:c0 + width]) + b_ref[:, c0:c0 + width]

    u = proj(0, GMLP_WIDTH)
    v = _ln(proj(512, GMLP_WIDTH), lng_ref[...], lnb_ref[...]).astype(BF)
    row = lax.broadcasted_iota(jnp.int32, (GMLP_CHUNK, GMLP_CHUNK), 0)
    col = lax.broadcasted_iota(jnp.int32, (GMLP_CHUNK, GMLP_CHUNK), 1)
    gd = GMLP_WIDTH // GMLP_GROUPS
    wgs = [jnp.where(row >= col, ws_ref[g], 0.0).astype(BF) for g in range(GMLP_GROUPS)]
    chunks = []
    for c in range(tm // GMLP_CHUNK):
        r0 = c * GMLP_CHUNK
        chunks.append(jnp.concatenate(
            [_dot(wgs[g], v[r0:r0 + GMLP_CHUNK, g * gd:(g + 1) * gd]) + bst_ref[:, g:g + 1]
             for g in range(GMLP_GROUPS)], axis=1))
    s = jnp.concatenate(chunks, axis=0)
    ya = _dot((u * s).astype(BF), wga_ref[...])

    cb = proj(1024, CONV_WIDTH)
    z = proj(1536, CONV_WIDTH) * proj(2048, CONV_WIDTH)

    @pl.when(i % tiles_per_seq == 0)
    def _():
        prev_ref[...] = jnp.zeros_like(prev_ref)

    zext = jnp.concatenate([prev_ref[...], z], axis=0)
    z1 = pltpu.roll(zext, 1, 0)[CONV_HALO:]
    z2 = pltpu.roll(zext, 2, 0)[CONV_HALO:]
    y = cw_ref[0:1, :] * z2 + cw_ref[1:2, :] * z1 + cw_ref[2:3, :] * z
    prev_ref[...] = z[tm - CONV_HALO:, :]
    yb = _dot((cb * y).astype(BF), wcb_ref[...])

    ga = proj(2560, D_MODEL)
    gb = proj(3584, D_MODEL)
    o_ref[...] = jax.nn.sigmoid(ga) * ya + jax.nn.sigmoid(gb) * yb


def _mix_ab(h, w, b, lng, lnb, ws, bst, wga, cw, wcb, *, layer, tm, seq):
    n, d = h.shape
    kern = functools.partial(_ab_kernel, tiles_per_seq=seq // tm)
    return pl.pallas_call(
        kern,
        grid=(n // tm,),
        in_specs=[pl.BlockSpec((tm, d), lambda i: (i, 0))]
        + [_layered(a, layer) for a in (w, b, lng, lnb, ws, bst, wga, cw, wcb)],
        out_specs=pl.BlockSpec((tm, d), lambda i: (i, 0)),
        out_shape=jax.ShapeDtypeStruct((n, d), F32),
        scratch_shapes=[pltpu.VMEM((CONV_HALO, CONV_WIDTH), F32)],
        compiler_params=_params(("arbitrary",)),
        name="mix_ab",
    )(h, w, b, lng, lnb, ws, bst, wga, cw, wcb)


def _mla_proj_kernel(h_ref, w_ref, b_ref, qg_ref, kvg_ref, wq_ref, wqr_ref, wk_ref, wv_ref,
                     cq_ref, sq_ref, ck_ref, sk_ref, q_ref, k_ref, v_ref):
    hb = h_ref[...].astype(BF)
    z = _dot(hb, w_ref[...]) + b_ref[...]
    qn = _rms(z[:, 0:256], qg_ref[...]).astype(BF)
    kvn = _rms(z[:, 256:384], kvg_ref[...]).astype(BF)
    half = MLA_ROPE // 2

    def rotate_half(x, start):
        w = x.shape[1]
        first = lax.broadcasted_iota(jnp.int32, (1, w), 1) % LANE < start + half
        return jnp.where(first, -pltpu.roll(x, w - half, 1), pltpu.roll(x, half, 1))

    cq = jnp.concatenate([cq_ref[...]] * MLA_HEADS, axis=1)
    sq = jnp.concatenate([sq_ref[...]] * MLA_HEADS, axis=1)
    scale = (MLA_NOPE + MLA_ROPE) ** -0.5 * LOG2_E
    q_ref[...] = ((_dot(qn, wq_ref[...]) * cq + _dot(qn, wqr_ref[...]) * sq) * scale).astype(BF)
    kr = z[:, 384:512]
    kpe = pltpu.roll(kr * ck_ref[...] + rotate_half(kr, 0) * sk_ref[...], MLA_NOPE, 1)
    k_ref[...] = (_dot(kvn, wk_ref[...]) + jnp.concatenate([kpe] * MLA_HEADS, axis=1)).astype(BF)
    v_ref[...] = (_dot(kvn, wv_ref[...]) + _denom_ones(v_ref.shape[1])).astype(BF)


def _mla_proj(h, w, b, qg, kvg, wq, wqr, wk, wv, cq, sq, ck, sk, *, layer, tm, seq):
    n, d = h.shape
    tps = seq // tm
    tab = pl.BlockSpec((tm, LANE), lambda i: (i % tps, 0))
    wide = MLA_HEADS * LANE
    out = jax.ShapeDtypeStruct((n, wide), BF)
    return pl.pallas_call(
        _mla_proj_kernel,
        grid=(n // tm,),
        in_specs=[pl.BlockSpec((tm, d), lambda i: (i, 0))]
        + [_layered(a, layer) for a in (w, b, qg, kvg, wq, wqr, wk, wv)] + [tab] * 4,
        out_specs=[pl.BlockSpec((tm, wide), lambda i: (i, 0))] * 3,
        out_shape=[out, out, out],
        compiler_params=_params(("parallel",)),
        name="mla_proj",
    )(h, w, b, qg, kvg, wq, wqr, wk, wv, cq, sq, ck, sk)


def _online_softmax_step(s, v, carry):
    m, acc = carry
    m_new = jnp.maximum(m, jnp.max(s, -1, keepdims=True))
    p = jnp.exp2(s - m_new)
    acc = jnp.exp2(m - m_new) * acc + _dot(p.astype(BF), v)
    return m_new, acc


def _softmax_init(rows, width):
    return (jnp.full((rows, 1), NEG, F32), jnp.zeros((rows, width), F32))


def _normalize(acc):
    return acc * (1.0 / acc[:, DENOM_LANE:DENOM_LANE + 1])


def _pack_head_pairs(slots):
    low = lax.broadcasted_iota(jnp.int32, (1, LANE), 1) < DENOM_LANE
    return jnp.concatenate([jnp.where(low, a, pltpu.roll(b, DENOM_LANE, 1))
                            for a, b in zip(slots[0::2], slots[1::2])], axis=1)


def _denom_ones(width):
    lane = lax.broadcasted_iota(jnp.int32, (1, width), 1)
    return jnp.where(lane % LANE == DENOM_LANE, 1.0, 0.0)


def _flash_kernel(q_ref, k_ref, v_ref, o_ref, *, tq, hp):
    qi = pl.program_id(2)
    q0 = qi * tq
    qs = [q_ref[0, :, h * LANE:(h + 1) * LANE] for h in range(hp)]

    def tile(j, carries, width, diagonal):
        k0 = pl.multiple_of(j * width, width)
        out = []
        for h in range(hp):
            s = _dot_t(qs[h], k_ref[0, pl.ds(k0, width), h * LANE:(h + 1) * LANE])
            if diagonal:
                r = lax.broadcasted_iota(jnp.int32, (tq, width), 0)
                c = lax.broadcasted_iota(jnp.int32, (tq, width), 1)
                s = jnp.where(c <= r, s, NEG)
            out.append(_online_softmax_step(s, v_ref[0, pl.ds(k0, width), h * LANE:(h + 1) * LANE], carries[h]))
        return tuple(out)

    init = tuple(_softmax_init(tq, LANE) for _ in range(hp))
    carries = lax.fori_loop(0, qi, lambda j, c: tile(j, c, tq, False), init)

    half = tq // 2
    carries = tile(2 * qi, carries, half, True)
    r = lax.broadcasted_iota(jnp.int32, (half, half), 0)
    c = lax.broadcasted_iota(jnp.int32, (half, half), 1)
    k1 = pl.multiple_of(q0 + half, half)
    out = []
    for h in range(hp):
        m, acc = carries[h]
        s = _dot_t(qs[h][half:], k_ref[0, pl.ds(k1, half), h * LANE:(h + 1) * LANE])
        m2, acc2 = _online_softmax_step(jnp.where(c <= r, s, NEG), v_ref[0, pl.ds(k1, half), h * LANE:(h + 1) * LANE],
                                        (m[half:], acc[half:]))
        out.append(jnp.concatenate([acc[:half], acc2], axis=0))
    o_ref[0] = _pack_head_pairs([_normalize(acc) for acc in out]).astype(o_ref.dtype)


def _flash_causal(q, k, v, *, tq, hp):
    bsz, seq, wide = q.shape
    heads = wide // LANE
    assert seq % tq == 0 and hp % 2 == 0
    half_lane = LANE // 2
    kern = functools.partial(_flash_kernel, tq=tq, hp=hp)
    return pl.pallas_call(
        kern,
        grid=(bsz, heads // hp, seq // tq),
        in_specs=[
            pl.BlockSpec((1, tq, hp * LANE), lambda b, h, i: (b, i, h)),
            pl.BlockSpec((1, seq, hp * LANE), lambda b, h, i: (b, 0, h)),
            pl.BlockSpec((1, seq, hp * LANE), lambda b, h, i: (b, 0, h)),
        ],
        out_specs=pl.BlockSpec((1, tq, hp * half_lane), lambda b, h, i: (b, i, h)),
        out_shape=jax.ShapeDtypeStruct((bsz, seq, heads * half_lane), BF),
        compiler_params=_params(("parallel", "parallel", "arbitrary")),
        name="mla_flash",
    )(q, k, v)


def _nsa_proj_kernel(h_ref, w_ref, b_ref, c_ref, s_ref, q_ref, kc_ref, vc_ref, ks_ref, vs_ref, kw_ref, vw_ref, g_ref,
                     *, tiles_per_seq):
    tm = h_ref.shape[0]
    hb = h_ref[...].astype(BF)
    pos = (pl.program_id(0) % tiles_per_seq) * tm + lax.broadcasted_iota(jnp.int32, (tm, LANE), 0)
    lane = lax.broadcasted_iota(jnp.int32, (tm, LANE), 1)
    tag = jnp.where(lane == NSA_DIM + lax.shift_right_logical(pos, SLC_SHIFT), MASK_BIG, 0.0)
    tag2 = jnp.concatenate([tag] * NSA_GROUPS, axis=1)
    c = c_ref[...]
    s = s_ref[...]
    half = NSA_DIM // 2

    def rope(x):
        w = x.shape[1]
        first = lax.broadcasted_iota(jnp.int32, (1, w), 1) % NSA_DIM < half
        rot = jnp.where(first, -pltpu.roll(x, w - half, 1), pltpu.roll(x, half, 1))
        reps = w // LANE
        return x * jnp.concatenate([c] * reps, axis=1) + rot * jnp.concatenate([s] * reps, axis=1)

    low = lax.broadcasted_iota(jnp.int32, (1, LANE), 1) < NSA_DIM

    def spread(x):
        out = []
        for j in range(x.shape[1] // LANE):
            blk = x[:, j * LANE:(j + 1) * LANE]
            out += [jnp.where(low, blk, 0.0), jnp.where(low, pltpu.roll(blk, NSA_DIM, 1), 0.0)]
        return jnp.concatenate(out, axis=1)

    z = _dot(hb, w_ref[...]) + b_ref[...]
    q_ref[...] = (spread(rope(z[:, 0:512])) * (NSA_DIM ** -0.5 * LOG2_E)).astype(BF)
    kc_ref[...] = z[:, 512:640]
    vc_ref[...] = z[:, 640:768]
    ks_ref[...] = (spread(rope(z[:, 768:896])) + tag2).astype(BF)
    ones = _denom_ones(NSA_GROUPS * LANE)
    vs_ref[...] = (spread(z[:, 896:1024]) + ones).astype(BF)
    kw_ref[...] = spread(rope(z[:, 1024:1152])).astype(BF)
    vw_ref[...] = (spread(z[:, 1152:1280]) + ones).astype(BF)
    g_ref[...] = jax.nn.sigmoid(z[:, 1280:1408])


def _nsa_proj(h, w, b, cn, sn, *, layer, tm, seq):
    n, d = h.shape
    tps = seq // tm
    tab = pl.BlockSpec((tm, LANE), lambda i: (i % tps, 0))

    def out(width, dt):
        return pl.BlockSpec((tm, width), lambda i: (i, 0)), jax.ShapeDtypeStruct((n, width), dt)

    outs = [out(1024, BF), out(128, F32), out(128, F32), out(256, BF), out(256, BF), out(256, BF), out(256, BF), out(128, F32)]
    assert seq // SLC_BLOCK <= LANE - NSA_DIM
    return pl.pallas_call(
        functools.partial(_nsa_proj_kernel, tiles_per_seq=tps),
        grid=(n // tm,),
        in_specs=[pl.BlockSpec((tm, d), lambda i: (i, 0)), _layered(w, layer), _layered(b, layer), tab, tab],
        out_specs=[o[0] for o in outs],
        out_shape=[o[1] for o in outs],
        compiler_params=_params(("parallel",)),
        name="nsa_proj",
    )(h, w, b, cn, sn)


def _nsa_cmp_kernel(kc_ref, vc_ref, pe_ref, wk_ref, wkr_ref, wv_ref, c_ref, s_ref, kcmp_ref, vcmp_ref):
    n16 = kcmp_ref.shape[1]
    gl = NSA_GROUPS * NSA_DIM
    kc = kcr = vc = None
    for l in range(CMP_STRIDE):
        rows = slice(l * gl, (l + 1) * gl)
        xk = kc_ref[0, pl.ds(l, n16, stride=CMP_STRIDE), :]
        xv = vc_ref[0, pl.ds(l, n16, stride=CMP_STRIDE), :]
        terms = [((xk + pe_ref[0:1, rows]).astype(BF), (xv + pe_ref[2:3, rows]).astype(BF), 0),
                 ((pltpu.roll(xk, n16 - 1, 0) + pe_ref[1:2, rows]).astype(BF),
                  (pltpu.roll(xv, n16 - 1, 0) + pe_ref[3:4, rows]).astype(BF), 1)]
        for ak, av, part in terms:
            pk, pkr, pv = _dot(ak, wk_ref[part, rows, :]), _dot(ak, wkr_ref[part, rows, :]), _dot(av, wv_ref[part, rows, :])
            kc, kcr, vc = (pk, pkr, pv) if kc is None else (kc + pk, kcr + pkr, vc + pv)
    kcmp_ref[0] = (kc * c_ref[...] + kcr * s_ref[...]).astype(BF)
    vcmp_ref[0] = vc.astype(BF)


def _nsa_compress(kc, vc, pe, wk, wkr, wv, cc, sc, *, layer):
    bsz, seq, wide = kc.shape
    n16 = seq // CMP_STRIDE
    blk = pl.BlockSpec((1, seq, wide), lambda b: (b, 0, 0))
    oblk = pl.BlockSpec((1, n16, NSA_GROUPS * LANE), lambda b: (b, 0, 0))
    osh = jax.ShapeDtypeStruct((bsz, n16, NSA_GROUPS * LANE), BF)
    return pl.pallas_call(
        _nsa_cmp_kernel,
        grid=(bsz,),
        in_specs=[blk, blk] + [_layered(a, layer) for a in (pe, wk, wkr, wv)] + [_resident(cc.shape), _resident(sc.shape)],
        out_specs=[oblk, oblk],
        out_shape=[osh, osh],
        compiler_params=_params(("parallel",)),
        name="nsa_compress",
    )(kc, vc, pe, wk, wkr, wv, cc, sc)


def _nsa_attn_kernel(q_ref, kcmp_ref, vcmp_ref, ks_ref, vs_ref, kw_ref, vw_ref, g_ref, ov_ref, o_ref, *, top_k):
    qi = pl.program_id(1)
    T = q_ref.shape[1]
    Th = T // 2
    R = NSA_HPG * T
    Rh = R // 2
    G = NSA_GROUPS
    q0 = qi * T
    qpos = lax.broadcasted_iota(jnp.int32, (T, 1), 0) + q0
    ncp = kcmp_ref.shape[1]
    nb = ov_ref.shape[0]
    ov_t = ov_ref[...]

    def add_per_query(x, b):
        w = x.shape[1]
        return (x.reshape(2, NSA_HPG, Th, w) + b.reshape(2, 1, Th, w)).reshape(R, w)

    def add_per_query_half(x, b):
        w = x.shape[1]
        return (x.reshape(NSA_HPG, Th, w) + b[None]).reshape(Rh, w)

    def lanes(g):
        return slice(g * LANE, (g + 1) * LANE)

    q4 = [jnp.concatenate([q_ref[0, half * Th:(half + 1) * Th, (g * NSA_HPG + h) * LANE:(g * NSA_HPG + h + 1) * LANE]
                           for half in range(2) for h in range(NSA_HPG)], axis=0) for g in range(G)]

    cmp_end = lax.broadcasted_iota(jnp.int32, (1, ncp), 1) * CMP_STRIDE + (CMP_BLOCK - 1)
    cbias = jnp.where(cmp_end <= qpos, 0.0, NEG)
    any_valid = jnp.where(qpos >= CMP_BLOCK - 1, 1.0, 0.0)
    jr = lax.broadcasted_iota(jnp.int32, (nb, 1), 0)
    jrf = jr.astype(F32)
    jq = lax.shift_right_logical(lax.broadcasted_iota(jnp.int32, (1, T), 1) + q0, SLC_SHIFT)
    forced = (jr == 0) | (jr == jq) | (jr == jq - 1)
    eye_t = jnp.where(lax.broadcasted_iota(jnp.int32, (T, T), 0) == lax.broadcasted_iota(jnp.int32, (T, T), 1),
                      1.0, 0.0).astype(BF)
    o_cmp, q4s = [], []
    for g in range(G):
        sm = _dot_t(q4[g], kcmp_ref[0, :, lanes(g)]).reshape(2, NSA_HPG, Th, ncp) + cbias.reshape(2, 1, Th, ncp)
        e = jnp.exp2(sm - jnp.max(sm, -1, keepdims=True))
        p = e * (any_valid.reshape(2, 1, Th, 1) / jnp.sum(e, -1, keepdims=True))
        o_cmp.append(_dot(p.reshape(R, ncp).astype(BF), vcmp_ref[0, :, lanes(g)]))
        psum = (p[:, 0] + p[:, 1] + p[:, 2] + p[:, 3]).reshape(T, ncp)
        hi = psum.astype(BF)
        r1 = psum - hi.astype(F32)
        mid = r1.astype(BF)
        lo = (r1 - mid.astype(F32)).astype(BF)
        imp = _dot_t(ov_t, hi) + _dot_t(ov_t, mid) + _dot_t(ov_t, lo)
        imp = jnp.where(forced, 1e9, imp)
        imp = jnp.where(jr <= jq, imp, -1.0)
        work = imp
        sel = jnp.zeros_like(imp)
        for _ in range(top_k):
            mx = jnp.max(work, 0, keepdims=True)
            idx = jnp.min(jnp.where(work == mx, jrf, float(nb)), 0, keepdims=True)
            pick = jrf == idx
            sel = jnp.where(pick, 1.0, sel)
            work = jnp.where(pick, -2.0, work)
        unsel_t = jnp.where(imp >= 0.0, sel, 0.0) - 1.0
        pad_t = [jnp.zeros((NSA_DIM, T), F32), unsel_t]
        if LANE - NSA_DIM - nb:
            pad_t.append(jnp.zeros((LANE - NSA_DIM - nb, T), F32))
        unsel = _dot_t(eye_t, jnp.concatenate(pad_t, axis=0).astype(BF)).astype(BF)
        q4s.append(add_per_query(q4[g], unsel))

    def slc_tile(j, carries):
        k0 = pl.multiple_of(j * T, T)
        return tuple(_online_softmax_step(_dot_t(q4s[g], ks_ref[0, pl.ds(k0, T), lanes(g)]),
                                          vs_ref[0, pl.ds(k0, T), lanes(g)], carries[g]) for g in range(G))

    carries = lax.fori_loop(0, qi, slc_tile, tuple(_softmax_init(R, LANE) for _ in range(G)))

    def causal_bias(k0, qp, nk):
        return jnp.where((lax.broadcasted_iota(jnp.int32, (1, nk), 1) + k0) <= qp, 0.0, NEG)

    k_a = pl.multiple_of(q0, Th)
    k_b = pl.multiple_of(q0 + Th, Th)
    bias_a = causal_bias(k_a, qpos, Th)
    bias_b = causal_bias(k_b, qpos[Th:], Th)
    slc_acc = []
    for g in range(G):
        sc = add_per_query(_dot_t(q4s[g], ks_ref[0, pl.ds(k_a, Th), lanes(g)]), bias_a)
        m, acc = _online_softmax_step(sc, vs_ref[0, pl.ds(k_a, Th), lanes(g)], carries[g])
        sc = add_per_query_half(_dot_t(q4s[g][Rh:], ks_ref[0, pl.ds(k_b, Th), lanes(g)]), bias_b)
        _, acc_b = _online_softmax_step(sc, vs_ref[0, pl.ds(k_b, Th), lanes(g)], (m[Rh:], acc[Rh:]))
        slc_acc.append((acc[:Rh], acc_b))

    wk = WINDOW + Th
    o_win = []
    for g in range(G):
        parts = []
        for half in range(2):
            w0 = pl.multiple_of(jnp.maximum(q0 + half * Th - WINDOW, 0), Th)
            dist = qpos[half * Th:(half + 1) * Th] - (lax.broadcasted_iota(jnp.int32, (1, wk), 1) + w0)
            wbias = jnp.where((dist >= 0) & (dist < WINDOW), 0.0, NEG)
            sc = add_per_query_half(_dot_t(q4[g][half * Rh:(half + 1) * Rh], kw_ref[0, pl.ds(w0, wk), lanes(g)]), wbias)
            e = jnp.exp2(sc - jnp.max(sc, -1, keepdims=True))
            parts.append(_normalize(_dot(e.astype(BF), vw_ref[0, pl.ds(w0, wk), lanes(g)])))
        o_win.append(parts)

    gw = NSA_HPG * LANE
    e_row = lax.broadcasted_iota(jnp.int32, (LANE, 3 * gw), 0)
    e_col = lax.broadcasted_iota(jnp.int32, (LANE, 3 * gw), 1)
    branch = jnp.where(e_col >= 2 * gw, 2, jnp.where(e_col >= gw, 1, 0))
    head = lax.shift_right_logical(e_col - branch * gw, LANE.bit_length() - 1)
    gate_col = 3 * head + branch

    def heads_on_lanes(x):
        return jnp.concatenate([x[h * Th:(h + 1) * Th] for h in range(NSA_HPG)], axis=1)

    gs = g_ref[0]
    hi = gs.astype(BF)
    lo = (gs - hi.astype(F32)).astype(BF)
    hi_lo = jnp.concatenate([hi, lo], axis=1)
    packed = NSA_HPG * NSA_DIM
    for g in range(G):
        expand = jnp.where(e_row == gate_col + g * (3 * NSA_HPG), 1.0, 0.0).astype(BF)
        gx = _dot(hi_lo, jnp.concatenate([expand, expand], axis=0))
        for half in range(2):
            rows = slice(half * Th, (half + 1) * Th)
            mixed = (gx[rows, 0:gw] * heads_on_lanes(o_cmp[g][half * Rh:(half + 1) * Rh])
                     + gx[rows, gw:2 * gw] * heads_on_lanes(_normalize(slc_acc[g][half]))
                     + gx[rows, 2 * gw:3 * gw] * heads_on_lanes(o_win[g][half]))
            o_ref[0, rows, g * packed:(g + 1) * packed] = _pack_head_pairs(
                [mixed[:, h * LANE:(h + 1) * LANE] for h in range(NSA_HPG)]).astype(o_ref.dtype)


def _nsa_attention(q, kcmp, vcmp, ks, vs, kw, vw, gates, ov, *, tq, top_k):
    bsz, seq, wide = q.shape
    n16 = kcmp.shape[1]
    gw = NSA_GROUPS * LANE
    assert seq % tq == 0 and seq >= WINDOW + tq
    kern = functools.partial(_nsa_attn_kernel, top_k=top_k)
    cblk = pl.BlockSpec((1, n16, gw), lambda b, i: (b, 0, 0))
    sblk = pl.BlockSpec((1, seq, gw), lambda b, i: (b, 0, 0))
    return pl.pallas_call(
        kern,
        grid=(bsz, seq // tq),
        in_specs=[pl.BlockSpec((1, tq, wide), lambda b, i: (b, i, 0)), cblk, cblk, sblk, sblk, sblk, sblk,
                  pl.BlockSpec((1, tq, LANE), lambda b, i: (b, i, 0)), _resident(ov.shape)],
        out_specs=pl.BlockSpec((1, tq, wide // 2), lambda b, i: (b, i, 0)),
        out_shape=jax.ShapeDtypeStruct((bsz, seq, wide // 2), BF),
        compiler_params=_params(("parallel", "arbitrary")),
        name="nsa_attn",
    )(q, kcmp, vcmp, ks, vs, kw, vw, gates, ov)


def _merge_kernel(x_ref, ab_ref, oc_ref, od_ref, wg_ref, bg_ref, wc_ref, wd_ref, wo_ref, g_ref, b_ref, o_ref):
    x = x_ref[...]
    gates = _dot(x.astype(BF), wg_ref[...]) + bg_ref[...]
    yc = _dot(oc_ref[...], wc_ref[...])
    yd = _dot(od_ref[...], wd_ref[...])
    merged = ab_ref[...] + jax.nn.sigmoid(gates[:, :D_MODEL]) * yc + jax.nn.sigmoid(gates[:, D_MODEL:]) * yd
    mix = _dot(merged.astype(BF), wo_ref[...])
    o_ref[...] = _ln(ALPHA * x + mix, g_ref[...], b_ref[...])


def _merge_ln(x, ab, oc, od, wg, bg, wc, wd, wo, g, b, *, layer, tm):
    n, d = x.shape
    row = pl.BlockSpec((tm, d), lambda i: (i, 0))
    rows = lambda a: pl.BlockSpec((tm, a.shape[1]), lambda i: (i, 0))
    return pl.pallas_call(
        _merge_kernel,
        grid=(n // tm,),
        in_specs=[row, row, rows(oc), rows(od)] + [_layered(a, layer) for a in (wg, bg, wc, wd, wo, g, b)],
        out_specs=row,
        out_shape=jax.ShapeDtypeStruct((n, d), F32),
        compiler_params=_params(("parallel",)),
        name="merge_ln",
    )(x, ab, oc, od, wg, bg, wc, wd, wo, g, b)


def _linear_kernel(x_ref, w_ref, o_ref):
    o_ref[...] = _dot(x_ref[...].astype(BF), w_ref[...]).astype(o_ref.dtype)


def _linear(x, w, *, layer, tm, dtype):
    n, d = x.shape
    return pl.pallas_call(
        _linear_kernel,
        grid=(n // tm,),
        in_specs=[pl.BlockSpec((tm, d), lambda i: (i, 0)), _layered(w, layer)],
        out_specs=pl.BlockSpec((tm, w.shape[2]), lambda i: (i, 0)),
        out_shape=jax.ShapeDtypeStruct((n, w.shape[2]), dtype),
        compiler_params=_params(("parallel",)),
        name="mem_kv",
    )(x, w)


def _xattn_kernel(x_ref, k_ref, v_ref, wq_ref, wo_ref, g_ref, b_ref, o_ref):
    x = x_ref[...]
    q = _dot(x.astype(BF), wq_ref[...]).astype(BF)
    k = k_ref[0]
    v = v_ref[0]
    heads = []
    for h in range(XATTN_HEADS):
        sl = slice(h * XATTN_DIM, (h + 1) * XATTN_DIM)
        s = _dot_t(q[:, sl], k[:, sl]) * (XATTN_DIM ** -0.5 * LOG2_E)
        e = jnp.exp2(s - jnp.max(s, -1, keepdims=True))
        heads.append(_dot(e.astype(BF), v[:, sl]) * (1.0 / jnp.sum(e, -1, keepdims=True)))
    o = jnp.concatenate(heads, axis=1).astype(BF)
    o_ref[...] = _ln(ALPHA * x + _dot(o, wo_ref[...]), g_ref[...], b_ref[...])


def _xattn_ln(x, kv, wq, wo, g, b, *, layer, tm, seq):
    n, d = x.shape
    tps = seq // tm
    mlen = kv.shape[1]
    hd = XATTN_HEADS * XATTN_DIM
    return pl.pallas_call(
        _xattn_kernel,
        grid=(n // tm,),
        in_specs=[pl.BlockSpec((tm, d), lambda i: (i, 0)),
                  pl.BlockSpec((1, mlen, hd), lambda i: (i // tps, 0, 0)),
                  pl.BlockSpec((1, mlen, hd), lambda i: (i // tps, 0, 1))]
        + [_layered(a, layer) for a in (wq, wo, g, b)],
        out_specs=pl.BlockSpec((tm, d), lambda i: (i, 0)),
        out_shape=jax.ShapeDtypeStruct((n, d), F32),
        compiler_params=_params(("parallel",)),
        name="xattn_ln",
    )(x, kv, kv, wq, wo, g, b)


def _rope_tab(pos, dim):
    inv = ROPE_THETA ** (-(jnp.arange(0, dim, 2, dtype=F32) / dim))
    ang = pos[:, None] * inv[None, :]
    return jnp.cos(ang), jnp.sin(ang)


def _rot_cols(w, half):
    return jnp.concatenate([-w[..., half:2 * half], w[..., :half]], axis=-1)


def _pad_slots(w, n_slots, width):
    lead = w.shape[:-1]
    w = w.reshape(lead + (n_slots, width))
    w = jnp.pad(w, [(0, 0)] * len(lead) + [(0, 0), (0, LANE - width)])
    return w.reshape(lead + (n_slots * LANE,))


def _layer_params(p):
    w_in, b_in = p["w_in"], p["b_in"]

    def cols(o, wd):
        return w_in[:, o:o + wd], b_in[o:o + wd]

    out = {}
    out["w_ab"] = jnp.concatenate([w_in[:, 0:2560], w_in[:, _O_GA:_O_GC]], axis=1).astype(BF)
    out["b_ab"] = jnp.concatenate([b_in[0:2560], b_in[_O_GA:_O_GC]])[None, :]
    wkr, bkr = cols(_O_KROPE, MLA_ROPE)
    padk =lambda a: jnp.pad(a, [(0, 0)] * (a.ndim - 1) + [(0, LANE - MLA_ROPE)])
    out["w_c"] = jnp.concatenate([w_in[:, _O_QLAT:_O_KROPE], padk(wkr)], axis=1).astype(BF)
    out["b_c"] = jnp.concatenate([b_in[_O_QLAT:_O_KROPE], padk(bkr)])[None, :]
    wuq = p["mla_wuq"].reshape(MLA_Q_RANK, MLA_HEADS, MLA_NOPE + MLA_ROPE)
    wq_c = jnp.pad(wuq, [(0, 0), (0, 0), (0, LANE - MLA_NOPE - MLA_ROPE)])
    out["wq_c"] = wq_c.reshape(MLA_Q_RANK, MLA_HEADS * LANE).astype(BF)
    wqr_c = jnp.pad(_rot_cols(wuq[..., MLA_NOPE:], MLA_ROPE // 2),
                    [(0, 0), (0, 0), (MLA_NOPE, LANE - MLA_NOPE - MLA_ROPE)])
    out["wqr_c"] = wqr_c.reshape(MLA_Q_RANK, MLA_HEADS * LANE).astype(BF)
    wukv = p["mla_wukv"].reshape(MLA_KV_RANK, MLA_HEADS, MLA_NOPE + MLA_V)
    out["wk_c"] = jnp.pad(wukv[..., :MLA_NOPE], [(0, 0), (0, 0), (0, LANE - MLA_NOPE)]).reshape(MLA_KV_RANK, -1).astype(BF)
    out["wv_c"] = jnp.pad(wukv[..., MLA_NOPE:], [(0, 0), (0, 0), (0, LANE - MLA_V)]).reshape(MLA_KV_RANK, -1).astype(BF)
    hd = NSA_DIM // 2
    n_gate = NSA_HEADS * 3
    out["w_d"] = jnp.pad(w_in[:, _O_NQ:_O_NGATE + n_gate], [(0, 0), (0, LANE - n_gate)]).astype(BF)
    out["b_d"] = jnp.pad(b_in[_O_NQ:_O_NGATE + n_gate], [(0, LANE - n_gate)])[None, :]

    def cmp_weights(w):
        eye = jnp.eye(NSA_GROUPS, dtype=F32)
        wp = jnp.pad(w, [(0, 0), (0, 0), (0, LANE - NSA_DIM)])
        full = jnp.einsum("lde,gh->lgdhe", wp, eye).reshape(CMP_BLOCK, NSA_GROUPS * NSA_DIM, NSA_GROUPS * LANE)
        return full.reshape(2, CMP_STRIDE * NSA_GROUPS * NSA_DIM, NSA_GROUPS * LANE).astype(BF)

    wck = p["nsa_wcmp_k"]
    out["wcmp_k"] = cmp_weights(wck)
    out["wcmp_kr"] = cmp_weights(_rot_cols(wck, hd))
    out["wcmp_v"] = cmp_weights(p["nsa_wcmp_v"])

    def pe_rows(pe):
        t = jnp.broadcast_to(pe[:, None, :], (CMP_BLOCK, NSA_GROUPS, NSA_DIM))
        return t.reshape(2, CMP_STRIDE * NSA_GROUPS * NSA_DIM)

    out["pe"] = jnp.concatenate([pe_rows(p["nsa_pe_k"]), pe_rows(p["nsa_pe_v"])], axis=0)
    out["w_g"] = w_in[:, _O_GC:].astype(BF)
    out["b_g"] = b_in[_O_GC:][None, :]
    out["wout_c"] = p["mla_wout"].astype(BF)
    out["wout_d"] = p["nsa_wout"].astype(BF)
    out["gmlp_bs_t"] = p["gmlp_bs"].T
    for name in ("gmlp_wout", "conv_wout", "w_o", "xattn_wq", "xattn_wo"):
        out[name] = p[name].astype(BF)
    out["xattn_wkv"] = jnp.concatenate([p["xattn_wk"], p["xattn_wv"]], axis=1).astype(BF)
    return out


def _tables(seq):
    pos = jnp.arange(seq, dtype=F32)
    c16, s16 = _rope_tab(pos, MLA_ROPE)
    one = jnp.ones((seq, MLA_NOPE), F32)
    zero = jnp.zeros((seq, MLA_NOPE), F32)
    tail = LANE - MLA_NOPE - MLA_ROPE
    cq = jnp.concatenate([one, c16, c16, jnp.ones((seq, tail), F32)], axis=1)
    sq = jnp.concatenate([zero, s16, s16, jnp.zeros((seq, tail), F32)], axis=1)
    ck = jnp.pad(jnp.concatenate([c16, c16], axis=1), [(0, 0), (0, LANE - MLA_ROPE)])
    sk = jnp.pad(jnp.concatenate([s16, s16], axis=1), [(0, 0), (0, LANE - MLA_ROPE)])
    c32, s32 = _rope_tab(pos, NSA_DIM)
    cn = jnp.concatenate([c32, c32] * (LANE // NSA_DIM), axis=1)
    sn = jnp.concatenate([s32, s32] * (LANE // NSA_DIM), axis=1)
    n16 = seq // CMP_STRIDE
    cend = (jnp.arange(n16) * CMP_STRIDE + CMP_BLOCK - 1).astype(F32)
    cc32, cs32 = _rope_tab(cend, NSA_DIM)
    ccg = jnp.pad(jnp.concatenate([cc32, cc32], axis=1), [(0, 0), (0, LANE - NSA_DIM)])
    csg = jnp.pad(jnp.concatenate([cs32, cs32], axis=1), [(0, 0), (0, LANE - NSA_DIM)])
    cc = jnp.concatenate([ccg] * NSA_GROUPS, axis=1)
    cs = jnp.concatenate([csg] * NSA_GROUPS, axis=1)
    n_cmp = (seq - CMP_BLOCK) // CMP_STRIDE + 1
    n_slc = seq // SLC_BLOCK
    cstart = jnp.arange(n16) * CMP_STRIDE
    sstart = jnp.arange(n_slc) * SLC_BLOCK
    ovl = (jnp.minimum(cstart[None, :] + CMP_BLOCK, sstart[:, None] + SLC_BLOCK)
           - jnp.maximum(cstart[None, :], sstart[:, None]))
    ovl = jnp.clip(ovl, 0).astype(F32) / CMP_BLOCK
    ovl = jnp.where(jnp.arange(n16)[None, :] < n_cmp, ovl, 0.0).astype(BF)
    return dict(cq=cq, sq=sq, ck=ck, sk=sk, cn=cn, sn=sn, cc=cc, cs=cs, ovl=ovl)


def kernel(x, mem, ffn1_w1, ffn1_w3, ffn1_w2, ln1_g, ln1_b, w_in, b_in, gmlp_ln_g, gmlp_ln_b, gmlp_ws, gmlp_bs, gmlp_wout, conv_w, conv_wout, mla_qnorm_g, mla_kvnorm_g, mla_wuq, mla_wukv, mla_wout, nsa_pe_k, nsa_pe_v, nsa_wcmp_k, nsa_wcmp_v, nsa_wout, w_o, ln2_g, ln2_b, xattn_wq, xattn_wk, xattn_wv, xattn_wo, ln3_g, ln3_b, ffn2_w1, ffn2_w3, ffn2_w2, ln4_g, ln4_b):
    bsz, seq, d = x.shape
    mlen = mem.shape[1]
    n = bsz * seq
    assert d == D_MODEL and seq % 1024 == 0
    lp = jax.vmap(_layer_params)(dict(
        w_in=w_in, b_in=b_in, mla_wuq=mla_wuq, mla_wukv=mla_wukv, mla_wout=mla_wout, nsa_pe_k=nsa_pe_k,
        nsa_pe_v=nsa_pe_v, nsa_wcmp_k=nsa_wcmp_k, nsa_wcmp_v=nsa_wcmp_v, nsa_wout=nsa_wout, gmlp_bs=gmlp_bs,
        gmlp_wout=gmlp_wout, conv_wout=conv_wout, w_o=w_o, xattn_wq=xattn_wq, xattn_wk=xattn_wk,
        xattn_wv=xattn_wv, xattn_wo=xattn_wo))
    tb = _tables(seq)
    tm = 1024
    tm_merge = 512
    tf = D_FF // 11
    top_k = min(SLC_TOPK, seq // SLC_BLOCK)
    row = lambda a: a[:, None, :]

    def cast_stacked(w):
        return _to_bf16(w.reshape(w.shape[0] * w.shape[1], w.shape[2])).reshape(w.shape)

    ffn1 = [cast_stacked(w) for w in (ffn1_w1, ffn1_w3, ffn1_w2)]
    ffn2 = [cast_stacked(w) for w in (ffn2_w1, ffn2_w3, ffn2_w2)]
    h = x.reshape(n, d)
    mem2 = mem.reshape(bsz * mlen, d)
    wide = MLA_HEADS * LANE
    gw = NSA_GROUPS * LANE
    for l in range(DEPTH):
        h = _ffn_ln(h, *ffn1, row(ln1_g), row(ln1_b), layer=l, tm=tm, tf=tf)
        ab = _mix_ab(h, lp["w_ab"], lp["b_ab"], row(gmlp_ln_g), row(gmlp_ln_b), gmlp_ws, lp["gmlp_bs_t"],
                     lp["gmlp_wout"], conv_w, lp["conv_wout"], layer=l, tm=tm, seq=seq)
        qc, kc_, vc_ = _mla_proj(h, lp["w_c"], lp["b_c"], row(mla_qnorm_g), row(mla_kvnorm_g),
                                 lp["wq_c"], lp["wqr_c"], lp["wk_c"], lp["wv_c"],
                                 tb["cq"], tb["sq"], tb["ck"], tb["sk"], layer=l, tm=tm, seq=seq)
        oc = _flash_causal(qc.reshape(bsz, seq, wide), kc_.reshape(bsz, seq, wide), vc_.reshape(bsz, seq, wide), tq=1024, hp=4)
        qn, nkc, nvc, nks, nvs, nkw, nvw, gates = _nsa_proj(h, lp["w_d"], lp["b_d"], tb["cn"], tb["sn"],
                                                            layer=l, tm=tm, seq=seq)
        kcmp, vcmp = _nsa_compress(nkc.reshape(bsz, seq, NSA_GROUPS * NSA_DIM), nvc.reshape(bsz, seq, NSA_GROUPS * NSA_DIM),
                                   lp["pe"], lp["wcmp_k"], lp["wcmp_kr"], lp["wcmp_v"], tb["cc"], tb["cs"], layer=l)
        od = _nsa_attention(qn.reshape(bsz, seq, NSA_HEADS * LANE), kcmp, vcmp,
                            nks.reshape(bsz, seq, gw), nvs.reshape(bsz, seq, gw),
                            nkw.reshape(bsz, seq, gw), nvw.reshape(bsz, seq, gw),
                            gates.reshape(bsz, seq, LANE), tb["ovl"], tq=512, top_k=top_k)
        h = _merge_ln(h, ab, oc.reshape(n, MLA_HEADS * MLA_V), od.reshape(n, NSA_HEADS * NSA_DIM), lp["w_g"], lp["b_g"],
                      lp["wout_c"], lp["wout_d"], lp["w_o"], row(ln2_g), row(ln2_b), layer=l, tm=tm_merge)
        kv = _linear(mem2, lp["xattn_wkv"], layer=l, tm=min(256, bsz * mlen), dtype=BF)
        h = _xattn_ln(h, kv.reshape(bsz, mlen, 2 * XATTN_HEADS * XATTN_DIM), lp["xattn_wq"], lp["xattn_wo"],
                      row(ln3_g), row(ln3_b), layer=l, tm=tm, seq=seq)
        h = _ffn_ln(h, *ffn2, row(ln4_g), row(ln4_b), layer=l, tm=tm, tf=tf)
    return h.reshape(bsz, seq, d)
```

```python
import functools

import jax
import jax.numpy as jnp
from jax import lax
from jax.experimental import pallas as pl
from jax.experimental.pallas import tpu as pltpu

BF = jnp.bfloat16
F32 = jnp.float32

D_MODEL = 1024
D_FF = 2816
LN_EPS = 1e-5
RMS_EPS = 1e-6
ROPE_THETA = 10000.0
DEPTH = 2
ALPHA = (2 * DEPTH) ** 0.25
NEG = -1e30
LOG2_E = 1.4426950408889634
DENOM_LANE = 64
MASK_BIG = 2.0 ** 100

GMLP_CHUNK = 128
GMLP_GROUPS = 4
GMLP_WIDTH = 512
CONV_WIDTH = 512
CONV_K = 3
MLA_HEADS = 8
MLA_Q_RANK = 256
MLA_KV_RANK = 128
MLA_NOPE = 64
MLA_ROPE = 32
MLA_V = 64
NSA_HEADS = 8
NSA_GROUPS = 2
NSA_HPG = 4
NSA_DIM = 64
CMP_BLOCK = 32
CMP_STRIDE = 16
SLC_BLOCK = 64
SLC_SHIFT = SLC_BLOCK.bit_length() - 1
SLC_TOPK = 8
WINDOW = 512
XATTN_HEADS = 4
XATTN_DIM = 128

LANE = 128
CONV_HALO = 8
VMEM_LIMIT = 56 * 1024 * 1024

_O_U, _O_V, _O_CB, _O_CC, _O_CH = 0, 512, 1024, 1536, 2048
_O_QLAT, _O_KVLAT, _O_KROPE = 2560, 2816, 2944
_O_NQ, _O_NKC, _O_NVC, _O_NKS, _O_NVS, _O_NKW, _O_NVW, _O_NGATE = 2976, 3488, 3616, 3744, 3872, 4000, 4128, 4256
_O_GA, _O_GB, _O_GC, _O_GD = 4280, 5304, 6328, 7352


def _dot(a, b):
    return jnp.dot(a, b, preferred_element_type=F32)


def _dot_t(a, b):
    return lax.dot_general(a, b, (((1,), (1,)), ((), ())), preferred_element_type=F32)


def _ln(y, g, b):
    mu = jnp.mean(y, -1, keepdims=True)
    d = y - mu
    var = jnp.mean(d * d, -1, keepdims=True)
    return d * lax.rsqrt(var + LN_EPS) * g + b


def _rms(x, g):
    return x * lax.rsqrt(jnp.mean(x * x, -1, keepdims=True) + RMS_EPS) * g


def _resident(shape):
    n = len(shape)
    return pl.BlockSpec(shape, lambda *_: (0,) * n, pipeline_mode=pl.Buffered(1))


def _layered(a, layer):
    n = a.ndim - 1
    return pl.BlockSpec((None,) + a.shape[1:], lambda *_: (layer,) + (0,) * n, pipeline_mode=pl.Buffered(1))


def _params(sem):
    return pltpu.CompilerParams(dimension_semantics=sem, vmem_limit_bytes=VMEM_LIMIT)


def _cast_kernel(x_ref, o_ref):
    o_ref[...] = x_ref[...].astype(o_ref.dtype)


def _to_bf16(w, *, rows=512):
    r, c = w.shape
    rows = min(rows, r)
    assert r % rows == 0
    return pl.pallas_call(
        _cast_kernel,
        grid=(r // rows,),
        in_specs=[pl.BlockSpec((rows, c), lambda i: (i, 0))],
        out_specs=pl.BlockSpec((rows, c), lambda i: (i, 0)),
        out_shape=jax.ShapeDtypeStruct((r, c), BF),
        compiler_params=_params(("parallel",)),
        name="to_bf16",
    )(w)


def _ffn_ln_kernel(x_ref, w1_ref, w3_ref, w2_ref, g_ref, b_ref, o_ref, *, tf):
    x = x_ref[...]
    xb = x.astype(BF)
    acc = None
    for c in range(w1_ref.shape[1] // tf):
        cols = slice(c * tf, (c + 1) * tf)
        h1 = _dot(xb, w1_ref[:, cols])
        h3 = _dot(xb, w3_ref[:, cols])
        hh = (h1 * jax.nn.sigmoid(h1)) * h3
        part = _dot(hh.astype(BF), w2_ref[cols, :])
        acc = part if acc is None else acc + part
    o_ref[...] = _ln(ALPHA * x + 0.5 * acc, g_ref[...], b_ref[...])


def _ffn_ln(x, w1, w3, w2, g, b, *, layer, tm, tf):
    n, d = x.shape
    return pl.pallas_call(
        functools.partial(_ffn_ln_kernel, tf=tf),
        grid=(n // tm,),
        in_specs=[pl.BlockSpec((tm, d), lambda i: (i, 0))] + [_layered(a, layer) for a in (w1, w3, w2, g, b)],
        out_specs=pl.BlockSpec((tm, d), lambda i: (i, 0)),
        out_shape=jax.ShapeDtypeStruct((n, d), F32),
        compiler_params=_params(("parallel",)),
        name="ffn_ln",
    )(x, w1, w3, w2, g, b)


def _ab_kernel(h_ref, w_ref, b_ref, lng_ref, lnb_ref, ws_ref, bst_ref, wga_ref, cw_ref, wcb_ref,
               o_ref, prev_ref, *, tiles_per_seq):
    i = pl.program_id(0)
    tm = h_ref.shape[0]
    hb = h_ref[...].astype(BF)

    def proj(c0, width):
        return _dot(hb, w_ref[:, c0:c0 + width]) + b_ref[:, c0:c0 + width]

    u = proj(0, GMLP_WIDTH)
    v = _ln(proj(512, GMLP_WIDTH), lng_ref[...], lnb_ref[...]).astype(BF)
    row = lax.broadcasted_iota(jnp.int32, (GMLP_CHUNK, GMLP_CHUNK), 0)
    col = lax.broadcasted_iota(jnp.int32, (GMLP_CHUNK, GMLP_CHUNK), 1)
    gd = GMLP_WIDTH // GMLP_GROUPS
    wgs = [jnp.where(row >= col, ws_ref[g], 0.0).astype(BF) for g in range(GMLP_GROUPS)]
    chunks = []
    for c in range(tm // GMLP_CHUNK):
        r0 = c * GMLP_CHUNK
        chunks.append(jnp.concatenate(
            [_dot(wgs[g], v[r0:r0 + GMLP_CHUNK, g * gd:(g + 1) * gd]) + bst_ref[:, g:g + 1]
             for g in range(GMLP_GROUPS)], axis=1))
    s = jnp.concatenate(chunks, axis=0)
    ya = _dot((u * s).astype(BF), wga_ref[...])

    cb = proj(1024, CONV_WIDTH)
    z = proj(1536, CONV_WIDTH) * proj(2048, CONV_WIDTH)

    @pl.when(i % tiles_per_seq == 0)
    def _():
        prev_ref[...] = jnp.zeros_like(prev_ref)

    zext = jnp.concatenate([prev_ref[...], z], axis=0)
    z1 = pltpu.roll(zext, 1, 0)[CONV_HALO:]
    z2 = pltpu.roll(zext, 2, 0)[CONV_HALO:]
    y = cw_ref[0:1, :] * z2 + cw_ref[1:2, :] * z1 + cw_ref[2:3, :] * z
    prev_ref[...] = z[tm - CONV_HALO:, :]
    yb = _dot((cb * y).astype(BF), wcb_ref[...])

    ga = proj(2560, D_MODEL)
    gb = proj(3584, D_MODEL)
    o_ref[...] = jax.nn.sigmoid(ga) * ya + jax.nn.sigmoid(gb) * yb


def _mix_ab(h, w, b, lng, lnb, ws, bst, wga, cw, wcb, *, layer, tm, seq):
    n, d = h.shape
    kern = functools.partial(_ab_kernel, tiles_per_seq=seq // tm)
    return pl.pallas_call(
        kern,
        grid=(n // tm,),
        in_specs=[pl.BlockSpec((tm, d), lambda i: (i, 0))]
        + [_layered(a, layer) for a in (w, b, lng, lnb, ws, bst, wga, cw, wcb)],
        out_specs=pl.BlockSpec((tm, d), lambda i: (i, 0)),
        out_shape=jax.ShapeDtypeStruct((n, d), F32),
        scratch_shapes=[pltpu.VMEM((CONV_HALO, CONV_WIDTH), F32)],
        compiler_params=_params(("arbitrary",)),
        name="mix_ab",
    )(h, w, b, lng, lnb, ws, bst, wga, cw, wcb)


def _mla_proj_kernel(h_ref, w_ref, b_ref, qg_ref, kvg_ref, wq_ref, wqr_ref, wk_ref, wv_ref,
                     cq_ref, sq_ref, ck_ref, sk_ref, q_ref, k_ref, v_ref):
    hb = h_ref[...].astype(BF)
    z = _dot(hb, w_ref[...]) + b_ref[...]
    qn = _rms(z[:, 0:256], qg_ref[...]).astype(BF)
    kvn = _rms(z[:, 256:384], kvg_ref[...]).astype(BF)
    half = MLA_ROPE // 2

    def rotate_half(x, start):
        w = x.shape[1]
        first = lax.broadcasted_iota(jnp.int32, (1, w), 1) % LANE < start + half
        return jnp.where(first, -pltpu.roll(x, w - half, 1), pltpu.roll(x, half, 1))

    cq = jnp.concatenate([cq_ref[...]] * MLA_HEADS, axis=1)
    sq = jnp.concatenate([sq_ref[...]] * MLA_HEADS, axis=1)
    scale = (MLA_NOPE + MLA_ROPE) ** -0.5 * LOG2_E
    q_ref[...] = ((_dot(qn, wq_ref[...]) * cq + _dot(qn, wqr_ref[...]) * sq) * scale).astype(BF)
    kr = z[:, 384:512]
    kpe = pltpu.roll(kr * ck_ref[...] + rotate_half(kr, 0) * sk_ref[...], MLA_NOPE, 1)
    k_ref[...] = (_dot(kvn, wk_ref[...]) + jnp.concatenate([kpe] * MLA_HEADS, axis=1)).astype(BF)
    v_ref[...] = (_dot(kvn, wv_ref[...]) + _denom_ones(v_ref.shape[1])).astype(BF)


def _mla_proj(h, w, b, qg, kvg, wq, wqr, wk, wv, cq, sq, ck, sk, *, layer, tm, seq):
    n, d = h.shape
    tps = seq // tm
    tab = pl.BlockSpec((tm, LANE), lambda i: (i % tps, 0))
    wide = MLA_HEADS * LANE
    out = jax.ShapeDtypeStruct((n, wide), BF)
    return pl.pallas_call(
        _mla_proj_kernel,
        grid=(n // tm,),
        in_specs=[pl.BlockSpec((tm, d), lambda i: (i, 0))]
        + [_layered(a, layer) for a in (w, b, qg, kvg, wq, wqr, wk, wv)] + [tab] * 4,
        out_specs=[pl.BlockSpec((tm, wide), lambda i: (i, 0))] * 3,
        out_shape=[out, out, out],
        compiler_params=_params(("parallel",)),
        name="mla_proj",
    )(h, w, b, qg, kvg, wq, wqr, wk, wv, cq, sq, ck, sk)


def _online_softmax_step(s, v, carry):
    m, acc = carry
    m_new = jnp.maximum(m, jnp.max(s, -1, keepdims=True))
    p = jnp.exp2(s - m_new)
    acc = jnp.exp2(m - m_new) * acc + _dot(p.astype(BF), v)
    return m_new, acc


def _softmax_init(rows, width):
    return (jnp.full((rows, 1), NEG, F32), jnp.zeros((rows, width), F32))


def _normalize(acc):
    return acc * (1.0 / acc[:, DENOM_LANE:DENOM_LANE + 1])


def _pack_head_pairs(slots):
    low = lax.broadcasted_iota(jnp.int32, (1, LANE), 1) < DENOM_LANE
    return jnp.concatenate([jnp.where(low, a, pltpu.roll(b, DENOM_LANE, 1))
                            for a, b in zip(slots[0::2], slots[1::2])], axis=1)


def _denom_ones(width):
    lane = lax.broadcasted_iota(jnp.int32, (1, width), 1)
    return jnp.where(lane % LANE == DENOM_LANE, 1.0, 0.0)


def _flash_kernel(q_ref, k_ref, v_ref, o_ref, *, tq, hp):
    qi = pl.program_id(2)
    q0 = qi * tq
    qs = [q_ref[0, :, h * LANE:(h + 1) * LANE] for h in range(hp)]

    def tile(j, carries, width, diagonal):
        k0 = pl.multiple_of(j * width, width)
        out = []
        for h in range(hp):
            s = _dot_t(qs[h], k_ref[0, pl.ds(k0, width), h * LANE:(h + 1) * LANE])
            if diagonal:
                r = lax.broadcasted_iota(jnp.int32, (tq, width), 0)
                c = lax.broadcasted_iota(jnp.int32, (tq, width), 1)
                s = jnp.where(c <= r, s, NEG)
            out.append(_online_softmax_step(s, v_ref[0, pl.ds(k0, width), h * LANE:(h + 1) * LANE], carries[h]))
        return tuple(out)

    init = tuple(_softmax_init(tq, LANE) for _ in range(hp))
    carries = lax.fori_loop(0, qi, lambda j, c: tile(j, c, tq, False), init)

    half = tq // 2
    carries = tile(2 * qi, carries, half, True)
    r = lax.broadcasted_iota(jnp.int32, (half, half), 0)
    c = lax.broadcasted_iota(jnp.int32, (half, half), 1)
    k1 = pl.multiple_of(q0 + half, half)
    out = []
    for h in range(hp):
        m, acc = carries[h]
        s = _dot_t(qs[h][half:], k_ref[0, pl.ds(k1, half), h * LANE:(h + 1) * LANE])
        m2, acc2 = _online_softmax_step(jnp.where(c <= r, s, NEG), v_ref[0, pl.ds(k1, half), h * LANE:(h + 1) * LANE],
                                        (m[half:], acc[half:]))
        out.append(jnp.concatenate([acc[:half], acc2], axis=0))
    o_ref[0] = _pack_head_pairs([_normalize(acc) for acc in out]).astype(o_ref.dtype)


def _flash_causal(q, k, v, *, tq, hp):
    bsz, seq, wide = q.shape
    heads = wide // LANE
    assert seq % tq == 0 and hp % 2 == 0
    half_lane = LANE // 2
    kern = functools.partial(_flash_kernel, tq=tq, hp=hp)
    return pl.pallas_call(
        kern,
        grid=(bsz, heads // hp, seq // tq),
        in_specs=[
            pl.BlockSpec((1, tq, hp * LANE), lambda b, h, i: (b, i, h)),
            pl.BlockSpec((1, seq, hp * LANE), lambda b, h, i: (b, 0, h)),
            pl.BlockSpec((1, seq, hp * LANE), lambda b, h, i: (b, 0, h)),
        ],
        out_specs=pl.BlockSpec((1, tq, hp * half_lane), lambda b, h, i: (b, i, h)),
        out_shape=jax.ShapeDtypeStruct((bsz, seq, heads * half_lane), BF),
        compiler_params=_params(("parallel", "parallel", "arbitrary")),
        name="mla_flash",
    )(q, k, v)


def _nsa_proj_kernel(h_ref, w_ref, b_ref, c_ref, s_ref, q_ref, kc_ref, vc_ref, ks_ref, vs_ref, kw_ref, vw_ref, g_ref,
                     *, tiles_per_seq):
    tm = h_ref.shape[0]
    hb = h_ref[...].astype(BF)
    pos = (pl.program_id(0) % tiles_per_seq) * tm + lax.broadcasted_iota(jnp.int32, (tm, LANE), 0)
    lane = lax.broadcasted_iota(jnp.int32, (tm, LANE), 1)
    tag = jnp.where(lane == NSA_DIM + lax.shift_right_logical(pos, SLC_SHIFT), MASK_BIG, 0.0)
    tag2 = jnp.concatenate([tag] * NSA_GROUPS, axis=1)
    c = c_ref[...]
    s = s_ref[...]
    half = NSA_DIM // 2

    def rope(x):
        w = x.shape[1]
        first = lax.broadcasted_iota(jnp.int32, (1, w), 1) % NSA_DIM < half
        rot = jnp.where(first, -pltpu.roll(x, w - half, 1), pltpu.roll(x, half, 1))
        reps = w // LANE
        return x * jnp.concatenate([c] * reps, axis=1) + rot * jnp.concatenate([s] * reps, axis=1)

    low = lax.broadcasted_iota(jnp.int32, (1, LANE), 1) < NSA_DIM

    def spread(x):
        out = []
        for j in range(x.shape[1] // LANE):
            blk = x[:, j * LANE:(j + 1) * LANE]
            out += [jnp.where(low, blk, 0.0), jnp.where(low, pltpu.roll(blk, NSA_DIM, 1), 0.0)]
        return jnp.concatenate(out, axis=1)

    z = _dot(hb, w_ref[...]) + b_ref[...]
    q_ref[...] = (spread(rope(z[:, 0:512])) * (NSA_DIM ** -0.5 * LOG2_E)).astype(BF)
    kc_ref[...] = z[:, 512:640]
    vc_ref[...] = z[:, 640:768]
    ks_ref[...] = (spread(rope(z[:, 768:896])) + tag2).astype(BF)
    ones = _denom_ones(NSA_GROUPS * LANE)
    vs_ref[...] = (spread(z[:, 896:1024]) + ones).astype(BF)
    kw_ref[...] = spread(rope(z[:, 1024:1152])).astype(BF)
    vw_ref[...] = (spread(z[:, 1152:1280]) + ones).astype(BF)
    g_ref[...] = jax.nn.sigmoid(z[:, 1280:1408])


def _nsa_proj(h, w, b, cn, sn, *, layer, tm, seq):
    n, d = h.shape
    tps = seq // tm
    tab = pl.BlockSpec((tm, LANE), lambda i: (i % tps, 0))

    def out(width, dt):
        return pl.BlockSpec((tm, width), lambda i: (i, 0)), jax.ShapeDtypeStruct((n, width), dt)

    outs = [out(1024, BF), out(128, F32), out(128, F32), out(256, BF), out(256, BF), out(256, BF), out(256, BF), out(128, F32)]
    assert seq // SLC_BLOCK <= LANE - NSA_DIM
    return pl.pallas_call(
        functools.partial(_nsa_proj_kernel, tiles_per_seq=tps),
        grid=(n // tm,),
        in_specs=[pl.BlockSpec((tm, d), lambda i: (i, 0)), _layered(w, layer), _layered(b, layer), tab, tab],
        out_specs=[o[0] for o in outs],
        out_shape=[o[1] for o in outs],
        compiler_params=_params(("parallel",)),
        name="nsa_proj",
    )(h, w, b, cn, sn)


def _nsa_cmp_kernel(kc_ref, vc_ref, pe_ref, wk_ref, wv_ref, c_ref, s_ref, kcmp_ref, vcmp_ref):
    n16 = kcmp_ref.shape[1]
    gl = NSA_GROUPS * NSA_DIM
    kc = vc = None
    for l in range(CMP_STRIDE):
        rows = slice(l * gl, (l + 1) * gl)

        def both_halves(x_ref, pe_lo, pe_hi):
            x = x_ref[0, pl.ds(l, n16, stride=CMP_STRIDE), :]
            return jnp.concatenate([x + pe_lo, pltpu.roll(x, n16 - 1, 0) + pe_hi], axis=1).astype(BF)

        pk = _dot(both_halves(kc_ref, pe_ref[0:1, rows], pe_ref[1:2, rows]),
                  jnp.concatenate([wk_ref[0, rows, :], wk_ref[1, rows, :]], axis=0))
        pv = _dot(both_halves(vc_ref, pe_ref[2:3, rows], pe_ref[3:4, rows]),
                  jnp.concatenate([wv_ref[0, rows, :], wv_ref[1, rows, :]], axis=0))
        kc, vc = (pk, pv) if kc is None else (kc + pk, vc + pv)
    w = kc.shape[1]
    half = NSA_DIM // 2
    first = lax.broadcasted_iota(jnp.int32, (1, w), 1) % LANE < half
    rot = jnp.where(first, -pltpu.roll(kc, w - half, 1), pltpu.roll(kc, half, 1))
    kcmp_ref[0] = (kc * c_ref[...] + rot * s_ref[...]).astype(BF)
    vcmp_ref[0] = vc.astype(BF)


def _nsa_compress(kc, vc, pe, wk, wv, cc, sc, *, layer):
    bsz, seq, wide = kc.shape
    n16 = seq // CMP_STRIDE
    blk = pl.BlockSpec((1, seq, wide), lambda b: (b, 0, 0))
    oblk = pl.BlockSpec((1, n16, NSA_GROUPS * LANE), lambda b: (b, 0, 0))
    osh = jax.ShapeDtypeStruct((bsz, n16, NSA_GROUPS * LANE), BF)
    return pl.pallas_call(
        _nsa_cmp_kernel,
        grid=(bsz,),
        in_specs=[blk, blk] + [_layered(a, layer) for a in (pe, wk, wv)] + [_resident(cc.shape), _resident(sc.shape)],
        out_specs=[oblk, oblk],
        out_shape=[osh, osh],
        compiler_params=_params(("parallel",)),
        name="nsa_compress",
    )(kc, vc, pe, wk, wv, cc, sc)


def _nsa_attn_kernel(q_ref, kcmp_ref, vcmp_ref, ks_ref, vs_ref, kw_ref, vw_ref, g_ref, ov_ref, o_ref, *, top_k):
    qi = pl.program_id(1)
    T = q_ref.shape[1]
    Th = T // 2
    R = NSA_HPG * T
    Rh = R // 2
    G = NSA_GROUPS
    q0 = qi * T
    qpos = lax.broadcasted_iota(jnp.int32, (T, 1), 0) + q0
    ncp = kcmp_ref.shape[1]
    nb = ov_ref.shape[0]
    ov_t = ov_ref[...]

    def add_per_query(x, b):
        w = x.shape[1]
        return (x.reshape(2, NSA_HPG, Th, w) + b.reshape(2, 1, Th, w)).reshape(R, w)

    def add_per_query_half(x, b):
        w = x.shape[1]
        return (x.reshape(NSA_HPG, Th, w) + b[None]).reshape(Rh, w)

    def lanes(g):
        return slice(g * LANE, (g + 1) * LANE)

    q4 = [jnp.concatenate([q_ref[0, half * Th:(half + 1) * Th, (g * NSA_HPG + h) * LANE:(g * NSA_HPG + h + 1) * LANE]
                           for half in range(2) for h in range(NSA_HPG)], axis=0) for g in range(G)]

    cmp_end = lax.broadcasted_iota(jnp.int32, (1, ncp), 1) * CMP_STRIDE + (CMP_BLOCK - 1)
    cbias = jnp.where(cmp_end <= qpos, 0.0, NEG)
    any_valid = jnp.where(qpos >= CMP_BLOCK - 1, 1.0, 0.0)
    jr = lax.broadcasted_iota(jnp.int32, (nb, 1), 0)
    jrf = jr.astype(F32)
    jq = lax.shift_right_logical(lax.broadcasted_iota(jnp.int32, (1, T), 1) + q0, SLC_SHIFT)
    forced = (jr == 0) | (jr == jq) | (jr == jq - 1)
    eye_t = jnp.where(lax.broadcasted_iota(jnp.int32, (T, T), 0) == lax.broadcasted_iota(jnp.int32, (T, T), 1),
                      1.0, 0.0).astype(BF)
    o_cmp, q4s = [], []
    for g in range(G):
        sm = _dot_t(q4[g], kcmp_ref[0, :, lanes(g)]).reshape(2, NSA_HPG, Th, ncp) + cbias.reshape(2, 1, Th, ncp)
        e = jnp.exp2(sm - jnp.max(sm, -1, keepdims=True))
        p = e * (any_valid.reshape(2, 1, Th, 1) / jnp.sum(e, -1, keepdims=True))
        o_cmp.append(_dot(p.reshape(R, ncp).astype(BF), vcmp_ref[0, :, lanes(g)]))
        psum = (p[:, 0] + p[:, 1] + p[:, 2] + p[:, 3]).reshape(T, ncp)
        hi = psum.astype(BF)
        r1 = psum - hi.astype(F32)
        mid = r1.astype(BF)
        lo = (r1 - mid.astype(F32)).astype(BF)
        imp = _dot_t(ov_t, hi) + _dot_t(ov_t, mid) + _dot_t(ov_t, lo)
        imp = jnp.where(forced, 1e9, imp)
        imp = jnp.where(jr <= jq, imp, -1.0)
        work = imp
        sel = jnp.zeros_like(imp)
        for _ in range(top_k):
            mx = jnp.max(work, 0, keepdims=True)
            idx = jnp.min(jnp.where(work == mx, jrf, float(nb)), 0, keepdims=True)
            pick = jrf == idx
            sel = jnp.where(pick, 1.0, sel)
            work = jnp.where(pick, -2.0, work)
        unsel_t = jnp.where(imp >= 0.0, sel, 0.0) - 1.0
        pad_t = [jnp.zeros((NSA_DIM, T), F32), unsel_t]
        if LANE - NSA_DIM - nb:
            pad_t.append(jnp.zeros((LANE - NSA_DIM - nb, T), F32))
        unsel = _dot_t(eye_t, jnp.concatenate(pad_t, axis=0).astype(BF)).astype(BF)
        q4s.append(add_per_query(q4[g], unsel))

    def slc_tile(j, carries):
        k0 = pl.multiple_of(j * T, T)
        return tuple(_online_softmax_step(_dot_t(q4s[g], ks_ref[0, pl.ds(k0, T), lanes(g)]),
                                          vs_ref[0, pl.ds(k0, T), lanes(g)], carries[g]) for g in range(G))

    carries = lax.fori_loop(0, qi, slc_tile, tuple(_softmax_init(R, LANE) for _ in range(G)))

    def causal_bias(k0, qp, nk):
        return jnp.where((lax.broadcasted_iota(jnp.int32, (1, nk), 1) + k0) <= qp, 0.0, NEG)

    k_a = pl.multiple_of(q0, Th)
    k_b = pl.multiple_of(q0 + Th, Th)
    bias_a = causal_bias(k_a, qpos, Th)
    bias_b = causal_bias(k_b, qpos[Th:], Th)
    slc_acc = []
    for g in range(G):
        sc = add_per_query(_dot_t(q4s[g], ks_ref[0, pl.ds(k_a, Th), lanes(g)]), bias_a)
        m, acc = _online_softmax_step(sc, vs_ref[0, pl.ds(k_a, Th), lanes(g)], carries[g])
        sc = add_per_query_half(_dot_t(q4s[g][Rh:], ks_ref[0, pl.ds(k_b, Th), lanes(g)]), bias_b)
        _, acc_b = _online_softmax_step(sc, vs_ref[0, pl.ds(k_b, Th), lanes(g)], (m[Rh:], acc[Rh:]))
        slc_acc.append((acc[:Rh], acc_b))

    wk = WINDOW + Th
    o_win = []
    for g in range(G):
        parts = []
        for half in range(2):
            w0 = pl.multiple_of(jnp.maximum(q0 + half * Th - WINDOW, 0), Th)
            dist = qpos[half * Th:(half + 1) * Th] - (lax.broadcasted_iota(jnp.int32, (1, wk), 1) + w0)
            wbias = jnp.where((dist >= 0) & (dist < WINDOW), 0.0, NEG)
            sc = add_per_query_half(_dot_t(q4[g][half * Rh:(half + 1) * Rh], kw_ref[0, pl.ds(w0, wk), lanes(g)]), wbias)
            e = jnp.exp2(sc - jnp.max(sc, -1, keepdims=True))
            parts.append(_normalize(_dot(e.astype(BF), vw_ref[0, pl.ds(w0, wk), lanes(g)])))
        o_win.append(parts)

    gw = NSA_HPG * LANE
    e_row = lax.broadcasted_iota(jnp.int32, (LANE, 3 * gw), 0)
    e_col = lax.broadcasted_iota(jnp.int32, (LANE, 3 * gw), 1)
    branch = jnp.where(e_col >= 2 * gw, 2, jnp.where(e_col >= gw, 1, 0))
    head = lax.shift_right_logical(e_col - branch * gw, LANE.bit_length() - 1)
    gate_col = 3 * head + branch

    def heads_on_lanes(x):
        return jnp.concatenate([x[h * Th:(h + 1) * Th] for h in range(NSA_HPG)], axis=1)

    gs = g_ref[0]
    hi = gs.astype(BF)
    lo = (gs - hi.astype(F32)).astype(BF)
    hi_lo = jnp.concatenate([hi, lo], axis=1)
    packed = NSA_HPG * NSA_DIM
    for g in range(G):
        expand = jnp.where(e_row == gate_col + g * (3 * NSA_HPG), 1.0, 0.0).astype(BF)
        gx = _dot(hi_lo, jnp.concatenate([expand, expand], axis=0))
        for half in range(2):
            rows = slice(half * Th, (half + 1) * Th)
            mixed = (gx[rows, 0:gw] * heads_on_lanes(o_cmp[g][half * Rh:(half + 1) * Rh])
                     + gx[rows, gw:2 * gw] * heads_on_lanes(_normalize(slc_acc[g][half]))
                     + gx[rows, 2 * gw:3 * gw] * heads_on_lanes(o_win[g][half]))
            o_ref[0, rows, g * packed:(g + 1) * packed] = _pack_head_pairs(
                [mixed[:, h * LANE:(h + 1) * LANE] for h in range(NSA_HPG)]).astype(o_ref.dtype)


def _nsa_attention(q, kcmp, vcmp, ks, vs, kw, vw, gates, ov, *, tq, top_k):
    bsz, seq, wide = q.shape
    n16 = kcmp.shape[1]
    gw = NSA_GROUPS * LANE
    assert seq % tq == 0 and seq >= WINDOW + tq
    kern = functools.partial(_nsa_attn_kernel, top_k=top_k)
    cblk = pl.BlockSpec((1, n16, gw), lambda b, i: (b, 0, 0))
    sblk = pl.BlockSpec((1, seq, gw), lambda b, i: (b, 0, 0))
    return pl.pallas_call(
        kern,
        grid=(bsz, seq // tq),
        in_specs=[pl.BlockSpec((1, tq, wide), lambda b, i: (b, i, 0)), cblk, cblk, sblk, sblk, sblk, sblk,
                  pl.BlockSpec((1, tq, LANE), lambda b, i: (b, i, 0)), _resident(ov.shape)],
        out_specs=pl.BlockSpec((1, tq, wide // 2), lambda b, i: (b, i, 0)),
        out_shape=jax.ShapeDtypeStruct((bsz, seq, wide // 2), BF),
        compiler_params=_params(("parallel", "arbitrary")),
        name="nsa_attn",
    )(q, kcmp, vcmp, ks, vs, kw, vw, gates, ov)


def _merge_kernel(x_ref, ab_ref, oc_ref, od_ref, wg_ref, bg_ref, wc_ref, wd_ref, wo_ref, g_ref, b_ref, o_ref):
    x = x_ref[...]
    gates = _dot(x.astype(BF), wg_ref[...]) + bg_ref[...]
    yc = _dot(oc_ref[...], wc_ref[...])
    yd = _dot(od_ref[...], wd_ref[...])
    merged = ab_ref[...] + jax.nn.sigmoid(gates[:, :D_MODEL]) * yc + jax.nn.sigmoid(gates[:, D_MODEL:]) * yd
    mix = _dot(merged.astype(BF), wo_ref[...])
    o_ref[...] = _ln(ALPHA * x + mix, g_ref[...], b_ref[...])


def _merge_ln(x, ab, oc, od, wg, bg, wc, wd, wo, g, b, *, layer, tm):
    n, d = x.shape
    row = pl.BlockSpec((tm, d), lambda i: (i, 0))
    rows = lambda a: pl.BlockSpec((tm, a.shape[1]), lambda i: (i, 0))
    return pl.pallas_call(
        _merge_kernel,
        grid=(n // tm,),
        in_specs=[row, row, rows(oc), rows(od)] + [_layered(a, layer) for a in (wg, bg, wc, wd, wo, g, b)],
        out_specs=row,
        out_shape=jax.ShapeDtypeStruct((n, d), F32),
        compiler_params=_params(("parallel",)),
        name="merge_ln",
    )(x, ab, oc, od, wg, bg, wc, wd, wo, g, b)


def _linear_kernel(x_ref, w_ref, o_ref):
    o_ref[...] = _dot(x_ref[...].astype(BF), w_ref[...]).astype(o_ref.dtype)


def _linear(x, w, *, layer, tm, dtype):
    n, d = x.shape
    return pl.pallas_call(
        _linear_kernel,
        grid=(n // tm,),
        in_specs=[pl.BlockSpec((tm, d), lambda i: (i, 0)), _layered(w, layer)],
        out_specs=pl.BlockSpec((tm, w.shape[2]), lambda i: (i, 0)),
        out_shape=jax.ShapeDtypeStruct((n, w.shape[2]), dtype),
        compiler_params=_params(("parallel",)),
        name="mem_kv",
    )(x, w)


def _xattn_kernel(x_ref, k_ref, v_ref, wq_ref, wo_ref, g_ref, b_ref, o_ref):
    x = x_ref[...]
    q = _dot(x.astype(BF), wq_ref[...]).astype(BF)
    k = k_ref[0]
    v = v_ref[0]
    heads = []
    for h in range(XATTN_HEADS):
        sl = slice(h * XATTN_DIM, (h + 1) * XATTN_DIM)
        s = _dot_t(q[:, sl], k[:, sl]) * (XATTN_DIM ** -0.5 * LOG2_E)
        e = jnp.exp2(s - jnp.max(s, -1, keepdims=True))
        heads.append(_dot(e.astype(BF), v[:, sl]) * (1.0 / jnp.sum(e, -1, keepdims=True)))
    o = jnp.concatenate(heads, axis=1).astype(BF)
    o_ref[...] = _ln(ALPHA * x + _dot(o, wo_ref[...]), g_ref[...], b_ref[...])


def _xattn_ln(x, kv, wq, wo, g, b, *, layer, tm, seq):
    n, d = x.shape
    tps = seq // tm
    mlen = kv.shape[1]
    hd = XATTN_HEADS * XATTN_DIM
    return pl.pallas_call(
        _xattn_kernel,
        grid=(n // tm,),
        in_specs=[pl.BlockSpec((tm, d), lambda i: (i, 0)),
                  pl.BlockSpec((1, mlen, hd), lambda i: (i // tps, 0, 0)),
                  pl.BlockSpec((1, mlen, hd), lambda i: (i // tps, 0, 1))]
        + [_layered(a, layer) for a in (wq, wo, g, b)],
        out_specs=pl.BlockSpec((tm, d), lambda i: (i, 0)),
        out_shape=jax.ShapeDtypeStruct((n, d), F32),
        compiler_params=_params(("parallel",)),
        name="xattn_ln",
    )(x, kv, kv, wq, wo, g, b)


def _rope_tab(pos, dim):
    inv = ROPE_THETA ** (-(jnp.arange(0, dim, 2, dtype=F32) / dim))
    ang = pos[:, None] * inv[None, :]
    return jnp.cos(ang), jnp.sin(ang)


def _rot_cols(w, half):
    return jnp.concatenate([-w[..., half:2 * half], w[..., :half]], axis=-1)


def _layer_params(p):
    w_in, b_in = p["w_in"], p["b_in"]

    def cols(o, wd):
        return w_in[:, o:o + wd], b_in[o:o + wd]

    out = {}
    out["w_ab"] = jnp.concatenate([w_in[:, 0:2560], w_in[:, _O_GA:_O_GC]], axis=1).astype(BF)
    out["b_ab"] = jnp.concatenate([b_in[0:2560], b_in[_O_GA:_O_GC]])[None, :]
    wkr, bkr = cols(_O_KROPE, MLA_ROPE)
    padk =lambda a: jnp.pad(a, [(0, 0)] * (a.ndim - 1) + [(0, LANE - MLA_ROPE)])
    out["w_c"] = jnp.concatenate([w_in[:, _O_QLAT:_O_KROPE], padk(wkr)], axis=1).astype(BF)
    out["b_c"] = jnp.concatenate([b_in[_O_QLAT:_O_KROPE], padk(bkr)])[None, :]
    wuq = p["mla_wuq"].reshape(MLA_Q_RANK, MLA_HEADS, MLA_NOPE + MLA_ROPE)
    wq_c = jnp.pad(wuq, [(0, 0), (0, 0), (0, LANE - MLA_NOPE - MLA_ROPE)])
    out["wq_c"] = wq_c.reshape(MLA_Q_RANK, MLA_HEADS * LANE).astype(BF)
    wqr_c = jnp.pad(_rot_cols(wuq[..., MLA_NOPE:], MLA_ROPE // 2),
                    [(0, 0), (0, 0), (MLA_NOPE, LANE - MLA_NOPE - MLA_ROPE)])
    out["wqr_c"] = wqr_c.reshape(MLA_Q_RANK, MLA_HEADS * LANE).astype(BF)
    wukv = p["mla_wukv"].reshape(MLA_KV_RANK, MLA_HEADS, MLA_NOPE + MLA_V)
    out["wk_c"] = jnp.pad(wukv[..., :MLA_NOPE], [(0, 0), (0, 0), (0, LANE - MLA_NOPE)]).reshape(MLA_KV_RANK, -1).astype(BF)
    out["wv_c"] = jnp.pad(wukv[..., MLA_NOPE:], [(0, 0), (0, 0), (0, LANE - MLA_V)]).reshape(MLA_KV_RANK, -1).astype(BF)
    n_gate = NSA_HEADS * 3
    out["w_d"] = jnp.pad(w_in[:, _O_NQ:_O_NGATE + n_gate], [(0, 0), (0, LANE - n_gate)]).astype(BF)
    out["b_d"] = jnp.pad(b_in[_O_NQ:_O_NGATE + n_gate], [(0, LANE - n_gate)])[None, :]

    def cmp_weights(w):
        eye = jnp.eye(NSA_GROUPS, dtype=F32)
        wp = jnp.pad(w, [(0, 0), (0, 0), (0, LANE - NSA_DIM)])
        full = jnp.einsum("lde,gh->lgdhe", wp, eye).reshape(CMP_BLOCK, NSA_GROUPS * NSA_DIM, NSA_GROUPS * LANE)
        return full.reshape(2, CMP_STRIDE * NSA_GROUPS * NSA_DIM, NSA_GROUPS * LANE).astype(BF)

    out["wcmp_k"] = cmp_weights(p["nsa_wcmp_k"])
    out["wcmp_v"] = cmp_weights(p["nsa_wcmp_v"])

    def pe_rows(pe):
        t = jnp.broadcast_to(pe[:, None, :], (CMP_BLOCK, NSA_GROUPS, NSA_DIM))
        return t.reshape(2, CMP_STRIDE * NSA_GROUPS * NSA_DIM)

    out["pe"] = jnp.concatenate([pe_rows(p["nsa_pe_k"]), pe_rows(p["nsa_pe_v"])], axis=0)
    out["w_g"] = w_in[:, _O_GC:].astype(BF)
    out["b_g"] = b_in[_O_GC:][None, :]
    out["wout_c"] = p["mla_wout"].astype(BF)
    out["wout_d"] = p["nsa_wout"].astype(BF)
    out["gmlp_bs_t"] = p["gmlp_bs"].T
    for name in ("gmlp_wout", "conv_wout", "w_o", "xattn_wq", "xattn_wo"):
        out[name] = p[name].astype(BF)
    out["xattn_wkv"] = jnp.concatenate([p["xattn_wk"], p["xattn_wv"]], axis=1).astype(BF)
    return out


def _tables(seq):
    pos = jnp.arange(seq, dtype=F32)
    c16, s16 = _rope_tab(pos, MLA_ROPE)
    one = jnp.ones((seq, MLA_NOPE), F32)
    zero = jnp.zeros((seq, MLA_NOPE), F32)
    tail = LANE - MLA_NOPE - MLA_ROPE
    cq = jnp.concatenate([one, c16, c16, jnp.ones((seq, tail), F32)], axis=1)
    sq = jnp.concatenate([zero, s16, s16, jnp.zeros((seq, tail), F32)], axis=1)
    ck = jnp.pad(jnp.concatenate([c16, c16], axis=1), [(0, 0), (0, LANE - MLA_ROPE)])
    sk = jnp.pad(jnp.concatenate([s16, s16], axis=1), [(0, 0), (0, LANE - MLA_ROPE)])
    c32, s32 = _rope_tab(pos, NSA_DIM)
    cn = jnp.concatenate([c32, c32] * (LANE // NSA_DIM), axis=1)
    sn = jnp.concatenate([s32, s32] * (LANE // NSA_DIM), axis=1)
    n16 = seq // CMP_STRIDE
    cend = (jnp.arange(n16) * CMP_STRIDE + CMP_BLOCK - 1).astype(F32)
    cc32, cs32 = _rope_tab(cend, NSA_DIM)
    ccg = jnp.pad(jnp.concatenate([cc32, cc32], axis=1), [(0, 0), (0, LANE - NSA_DIM)])
    csg = jnp.pad(jnp.concatenate([cs32, cs32], axis=1), [(0, 0), (0, LANE - NSA_DIM)])
    cc = jnp.concatenate([ccg] * NSA_GROUPS, axis=1)
    cs = jnp.concatenate([csg] * NSA_GROUPS, axis=1)
    n_cmp = (seq - CMP_BLOCK) // CMP_STRIDE + 1
    n_slc = seq // SLC_BLOCK
    cstart = jnp.arange(n16) * CMP_STRIDE
    sstart = jnp.arange(n_slc) * SLC_BLOCK
    ovl = (jnp.minimum(cstart[None, :] + CMP_BLOCK, sstart[:, None] + SLC_BLOCK)
           - jnp.maximum(cstart[None, :], sstart[:, None]))
    ovl = jnp.clip(ovl, 0).astype(F32) / CMP_BLOCK
    ovl = jnp.where(jnp.arange(n16)[None, :] < n_cmp, ovl, 0.0).astype(BF)
    return dict(cq=cq, sq=sq, ck=ck, sk=sk, cn=cn, sn=sn, cc=cc, cs=cs, ovl=ovl)


def kernel(x, mem, ffn1_w1, ffn1_w3, ffn1_w2, ln1_g, ln1_b, w_in, b_in, gmlp_ln_g, gmlp_ln_b, gmlp_ws, gmlp_bs, gmlp_wout, conv_w, conv_wout, mla_qnorm_g, mla_kvnorm_g, mla_wuq, mla_wukv, mla_wout, nsa_pe_k, nsa_pe_v, nsa_wcmp_k, nsa_wcmp_v, nsa_wout, w_o, ln2_g, ln2_b, xattn_wq, xattn_wk, xattn_wv, xattn_wo, ln3_g, ln3_b, ffn2_w1, ffn2_w3, ffn2_w2, ln4_g, ln4_b):
    bsz, seq, d = x.shape
    mlen = mem.shape[1]
    n = bsz * seq
    assert d == D_MODEL and seq % 1024 == 0
    lp = jax.vmap(_layer_params)(dict(
        w_in=w_in, b_in=b_in, mla_wuq=mla_wuq, mla_wukv=mla_wukv, mla_wout=mla_wout, nsa_pe_k=nsa_pe_k,
        nsa_pe_v=nsa_pe_v, nsa_wcmp_k=nsa_wcmp_k, nsa_wcmp_v=nsa_wcmp_v, nsa_wout=nsa_wout, gmlp_bs=gmlp_bs,
        gmlp_wout=gmlp_wout, conv_wout=conv_wout, w_o=w_o, xattn_wq=xattn_wq, xattn_wk=xattn_wk,
        xattn_wv=xattn_wv, xattn_wo=xattn_wo))
    tb = _tables(seq)
    tm = 1024
    tm_merge = 512
    tf = D_FF // 11
    top_k = min(SLC_TOPK, seq // SLC_BLOCK)
    row = lambda a: a[:, None, :]

    def cast_stacked(w):
        return _to_bf16(w.reshape(w.shape[0] * w.shape[1], w.shape[2])).reshape(w.shape)

    ffn1 = [cast_stacked(w) for w in (ffn1_w1, ffn1_w3, ffn1_w2)]
    ffn2 = [cast_stacked(w) for w in (ffn2_w1, ffn2_w3, ffn2_w2)]
    h = x.reshape(n, d)
    mem2 = mem.reshape(bsz * mlen, d)
    wide = MLA_HEADS * LANE
    gw = NSA_GROUPS * LANE
    for l in range(DEPTH):
        h = _ffn_ln(h, *ffn1, row(ln1_g), row(ln1_b), layer=l, tm=tm, tf=tf)
        ab = _mix_ab(h, lp["w_ab"], lp["b_ab"], row(gmlp_ln_g), row(gmlp_ln_b), gmlp_ws, lp["gmlp_bs_t"],
                     lp["gmlp_wout"], conv_w, lp["conv_wout"], layer=l, tm=tm, seq=seq)
        qc, kc_, vc_ = _mla_proj(h, lp["w_c"], lp["b_c"], row(mla_qnorm_g), row(mla_kvnorm_g),
                                 lp["wq_c"], lp["wqr_c"], lp["wk_c"], lp["wv_c"],
                                 tb["cq"], tb["sq"], tb["ck"], tb["sk"], layer=l, tm=tm, seq=seq)
        oc = _flash_causal(qc.reshape(bsz, seq, wide), kc_.reshape(bsz, seq, wide), vc_.reshape(bsz, seq, wide), tq=1024, hp=4)
        qn, nkc, nvc, nks, nvs, nkw, nvw, gates = _nsa_proj(h, lp["w_d"], lp["b_d"], tb["cn"], tb["sn"],
                                                            layer=l, tm=tm, seq=seq)
        kcmp, vcmp = _nsa_compress(nkc.reshape(bsz, seq, NSA_GROUPS * NSA_DIM), nvc.reshape(bsz, seq, NSA_GROUPS * NSA_DIM),
                                   lp["pe"], lp["wcmp_k"], lp["wcmp_v"], tb["cc"], tb["cs"], layer=l)
        od = _nsa_attention(qn.reshape(bsz, seq, NSA_HEADS * LANE), kcmp, vcmp,
                            nks.reshape(bsz, seq, gw), nvs.reshape(bsz, seq, gw),
                            nkw.reshape(bsz, seq, gw), nvw.reshape(bsz, seq, gw),
                            gates.reshape(bsz, seq, LANE), tb["ovl"], tq=512, top_k=top_k)
        h = _merge_ln(h, ab, oc.reshape(n, MLA_HEADS * MLA_V), od.reshape(n, NSA_HEADS * NSA_DIM), lp["w_g"], lp["b_g"],
                      lp["wout_c"], lp["wout_d"], lp["w_o"], row(ln2_g), row(ln2_b), layer=l, tm=tm_merge)
        kv = _linear(mem2, lp["xattn_wkv"], layer=l, tm=min(256, bsz * mlen), dtype=BF)
        h = _xattn_ln(h, kv.reshape(bsz, mlen, 2 * XATTN_HEADS * XATTN_DIM), lp["xattn_wq"], lp["xattn_wo"],
                      row(ln3_g), row(ln3_b), layer=l, tm=tm, seq=seq)
        h = _ffn_ln(h, *ffn2, row(ln4_g), row(ln4_b), layer=l, tm=tm, tf=tf)
    return h.reshape(bsz, seq, d)
```

```python
import functools

import jax
import jax.numpy as jnp
from jax import lax
from jax.experimental import pallas as pl
from jax.experimental.pallas import tpu as pltpu

BF = jnp.bfloat16
F32 = jnp.float32

D_MODEL = 1024
D_FF = 2816
LN_EPS = 1e-5
RMS_EPS = 1e-6
ROPE_THETA = 10000.0
DEPTH = 2
ALPHA = (2 * DEPTH) ** 0.25
NEG = -1e30
LOG2_E = 1.4426950408889634
DENOM_LANE = 64
MASK_BIG = 2.0 ** 100

GMLP_CHUNK = 128
GMLP_GROUPS = 4
GMLP_WIDTH = 512
CONV_WIDTH = 512
CONV_K = 3
MLA_HEADS = 8
MLA_Q_RANK = 256
MLA_KV_RANK = 128
MLA_NOPE = 64
MLA_ROPE = 32
MLA_V = 64
NSA_HEADS = 8
NSA_GROUPS = 2
NSA_HPG = 4
NSA_DIM = 64
CMP_BLOCK = 32
CMP_STRIDE = 16
SLC_BLOCK = 64
SLC_SHIFT = SLC_BLOCK.bit_length() - 1
SLC_TOPK = 8
WINDOW = 512
XATTN_HEADS = 4
XATTN_DIM = 128

LANE = 128
CONV_HALO = 8
VMEM_LIMIT = 56 * 1024 * 1024

_O_U, _O_V, _O_CB, _O_CC, _O_CH = 0, 512, 1024, 1536, 2048
_O_QLAT, _O_KVLAT, _O_KROPE = 2560, 2816, 2944
_O_NQ, _O_NKC, _O_NVC, _O_NKS, _O_NVS, _O_NKW, _O_NVW, _O_NGATE = 2976, 3488, 3616, 3744, 3872, 4000, 4128, 4256
_O_GA, _O_GB, _O_GC, _O_GD = 4280, 5304, 6328, 7352


def _dot(a, b):
    return jnp.dot(a, b, preferred_element_type=F32)


def _dot_t(a, b):
    return lax.dot_general(a, b, (((1,), (1,)), ((), ())), preferred_element_type=F32)


def _ln(y, g, b):
    mu = jnp.mean(y, -1, keepdims=True)
    d = y - mu
    var = jnp.mean(d * d, -1, keepdims=True)
    return d * lax.rsqrt(var + LN_EPS) * g + b


def _rms(x, g):
    return x * lax.rsqrt(jnp.mean(x * x, -1, keepdims=True) + RMS_EPS) * g


def _resident(shape):
    n = len(shape)
    return pl.BlockSpec(shape, lambda *_: (0,) * n, pipeline_mode=pl.Buffered(1))


def _layered(a, layer):
    n = a.ndim - 1
    return pl.BlockSpec((None,) + a.shape[1:], lambda *_: (layer,) + (0,) * n, pipeline_mode=pl.Buffered(1))


def _params(sem):
    return pltpu.CompilerParams(dimension_semantics=sem, vmem_limit_bytes=VMEM_LIMIT)


def _cast_kernel(x_ref, o_ref):
    o_ref[...] = x_ref[...].astype(o_ref.dtype)


def _to_bf16(w, *, rows=512):
    r, c = w.shape
    rows = min(rows, r)
    assert r % rows == 0
    return pl.pallas_call(
        _cast_kernel,
        grid=(r // rows,),
        in_specs=[pl.BlockSpec((rows, c), lambda i: (i, 0))],
        out_specs=pl.BlockSpec((rows, c), lambda i: (i, 0)),
        out_shape=jax.ShapeDtypeStruct((r, c), BF),
        compiler_params=_params(("parallel",)),
        name="to_bf16",
    )(w)


def _ffn_ln_kernel(x_ref, w1_ref, w3_ref, w2_ref, g_ref, b_ref, o_ref, hh_ref, *, tf):
    x = x_ref[...]
    xb = x.astype(BF)
    for c in range(w1_ref.shape[1] // tf):
        cols = slice(c * tf, (c + 1) * tf)
        h1 = _dot(xb, w1_ref[:, cols])
        h3 = _dot(xb, w3_ref[:, cols])
        hh_ref[:, cols] = ((h1 * jax.nn.sigmoid(h1)) * h3).astype(BF)
    o_ref[...] = _ln(ALPHA * x + 0.5 * _dot(hh_ref[...], w2_ref[...]), g_ref[...], b_ref[...])


def _ffn_ln(x, w1, w3, w2, g, b, *, layer, tm, tf):
    n, d = x.shape
    return pl.pallas_call(
        functools.partial(_ffn_ln_kernel, tf=tf),
        grid=(n // tm,),
        in_specs=[pl.BlockSpec((tm, d), lambda i: (i, 0))] + [_layered(a, layer) for a in (w1, w3, w2, g, b)],
        out_specs=pl.BlockSpec((tm, d), lambda i: (i, 0)),
        out_shape=jax.ShapeDtypeStruct((n, d), F32),
        scratch_shapes=[pltpu.VMEM((tm, w1.shape[2]), BF)],
        compiler_params=_params(("parallel",)),
        name="ffn_ln",
    )(x, w1, w3, w2, g, b)


def _ab_kernel(h_ref, w_ref, b_ref, lng_ref, lnb_ref, ws_ref, bst_ref, wga_ref, cw_ref, wcb_ref,
               o_ref, prev_ref, *, tiles_per_seq):
    i = pl.program_id(0)
    tm = h_ref.shape[0]
    hb = h_ref[...].astype(BF)

    def proj(c0, width):
        return _dot(hb, w_ref[:, c0:c0 + width]) + b_ref[:, c0:c0 + width]

    u = proj(0, GMLP_WIDTH)
    v = _ln(proj(512, GMLP_WIDTH), lng_ref[...], lnb_ref[...]).astype(BF)
    row = lax.broadcasted_iota(jnp.int32, (GMLP_CHUNK, GMLP_CHUNK), 0)
    col = lax.broadcasted_iota(jnp.int32, (GMLP_CHUNK, GMLP_CHUNK), 1)
    gd = GMLP_WIDTH // GMLP_GROUPS
    wgs = [jnp.where(row >= col, ws_ref[g], 0.0).astype(BF) for g in range(GMLP_GROUPS)]
    chunks = []
    for c in range(tm // GMLP_CHUNK):
        r0 = c * GMLP_CHUNK
        chunks.append(jnp.concatenate(
            [_dot(wgs[g], v[r0:r0 + GMLP_CHUNK, g * gd:(g + 1) * gd]) + bst_ref[:, g:g + 1]
             for g in range(GMLP_GROUPS)], axis=1))
    s = jnp.concatenate(chunks, axis=0)
    ya = _dot((u * s).astype(BF), wga_ref[...])

    cb = proj(1024, CONV_WIDTH)
    z = proj(1536, CONV_WIDTH) * proj(2048, CONV_WIDTH)

    @pl.when(i % tiles_per_seq == 0)
    def _():
        prev_ref[...] = jnp.zeros_like(prev_ref)

    zext = jnp.concatenate([prev_ref[...], z], axis=0)
    z1 = pltpu.roll(zext, 1, 0)[CONV_HALO:]
    z2 = pltpu.roll(zext, 2, 0)[CONV_HALO:]
    y = cw_ref[0:1, :] * z2 + cw_ref[1:2, :] * z1 + cw_ref[2:3, :] * z
    prev_ref[...] = z[tm - CONV_HALO:, :]
    yb = _dot((cb * y).astype(BF), wcb_ref[...])

    ga = proj(2560, D_MODEL)
    gb = proj(3584, D_MODEL)
    o_ref[...] = jax.nn.sigmoid(ga) * ya + jax.nn.sigmoid(gb) * yb


def _mix_ab(h, w, b, lng, lnb, ws, bst, wga, cw, wcb, *, layer, tm, seq):
    n, d = h.shape
    kern = functools.partial(_ab_kernel, tiles_per_seq=seq // tm)
    return pl.pallas_call(
        kern,
        grid=(n // tm,),
        in_specs=[pl.BlockSpec((tm, d), lambda i: (i, 0))]
        + [_layered(a, layer) for a in (w, b, lng, lnb, ws, bst, wga, cw, wcb)],
        out_specs=pl.BlockSpec((tm, d), lambda i: (i, 0)),
        out_shape=jax.ShapeDtypeStruct((n, d), F32),
        scratch_shapes=[pltpu.VMEM((CONV_HALO, CONV_WIDTH), F32)],
        compiler_params=_params(("arbitrary",)),
        name="mix_ab",
    )(h, w, b, lng, lnb, ws, bst, wga, cw, wcb)


def _mla_proj_kernel(h_ref, w_ref, b_ref, qg_ref, kvg_ref, wq_ref, wqr_ref, wk_ref, wv_ref,
                     cq_ref, sq_ref, ck_ref, sk_ref, q_ref, k_ref, v_ref):
    hb = h_ref[...].astype(BF)
    z = _dot(hb, w_ref[...]) + b_ref[...]
    qn = _rms(z[:, 0:256], qg_ref[...]).astype(BF)
    kvn = _rms(z[:, 256:384], kvg_ref[...]).astype(BF)
    half = MLA_ROPE // 2

    def rotate_half(x, start):
        w = x.shape[1]
        first = lax.broadcasted_iota(jnp.int32, (1, w), 1) % LANE < start + half
        return jnp.where(first, -pltpu.roll(x, w - half, 1), pltpu.roll(x, half, 1))

    cq = jnp.concatenate([cq_ref[...]] * MLA_HEADS, axis=1)
    sq = jnp.concatenate([sq_ref[...]] * MLA_HEADS, axis=1)
    scale = (MLA_NOPE + MLA_ROPE) ** -0.5 * LOG2_E
    q_ref[...] = ((_dot(qn, wq_ref[...]) * cq + _dot(qn, wqr_ref[...]) * sq) * scale).astype(BF)
    kr = z[:, 384:512]
    kpe = pltpu.roll(kr * ck_ref[...] + rotate_half(kr, 0) * sk_ref[...], MLA_NOPE, 1)
    k_ref[...] = (_dot(kvn, wk_ref[...]) + jnp.concatenate([kpe] * MLA_HEADS, axis=1)).astype(BF)
    v_ref[...] = (_dot(kvn, wv_ref[...]) + _denom_ones(v_ref.shape[1])).astype(BF)


def _mla_proj(h, w, b, qg, kvg, wq, wqr, wk, wv, cq, sq, ck, sk, *, layer, tm, seq):
    n, d = h.shape
    tps = seq // tm
    tab = pl.BlockSpec((tm, LANE), lambda i: (i % tps, 0))
    wide = MLA_HEADS * LANE
    out = jax.ShapeDtypeStruct((n, wide), BF)
    return pl.pallas_call(
        _mla_proj_kernel,
        grid=(n // tm,),
        in_specs=[pl.BlockSpec((tm, d), lambda i: (i, 0))]
        + [_layered(a, layer) for a in (w, b, qg, kvg, wq, wqr, wk, wv)] + [tab] * 4,
        out_specs=[pl.BlockSpec((tm, wide), lambda i: (i, 0))] * 3,
        out_shape=[out, out, out],
        compiler_params=_params(("parallel",)),
        name="mla_proj",
    )(h, w, b, qg, kvg, wq, wqr, wk, wv, cq, sq, ck, sk)


def _online_softmax_step(s, v, carry):
    m, acc = carry
    m_new = jnp.maximum(m, jnp.max(s, -1, keepdims=True))
    p = jnp.exp2(s - m_new)
    acc = jnp.exp2(m - m_new) * acc + _dot(p.astype(BF), v)
    return m_new, acc


def _softmax_init(rows, width):
    return (jnp.full((rows, 1), NEG, F32), jnp.zeros((rows, width), F32))


def _normalize(acc):
    return acc * (1.0 / acc[:, DENOM_LANE:DENOM_LANE + 1])


def _pack_head_pairs(slots):
    low = lax.broadcasted_iota(jnp.int32, (1, LANE), 1) < DENOM_LANE
    return jnp.concatenate([jnp.where(low, a, pltpu.roll(b, DENOM_LANE, 1))
                            for a, b in zip(slots[0::2], slots[1::2])], axis=1)


def _denom_ones(width):
    lane = lax.broadcasted_iota(jnp.int32, (1, width), 1)
    return jnp.where(lane % LANE == DENOM_LANE, 1.0, 0.0)


def _flash_kernel(q_ref, k_ref, v_ref, o_ref, *, tq, hp):
    qi = pl.program_id(2)
    q0 = qi * tq
    qs = [q_ref[0, :, h * LANE:(h + 1) * LANE] for h in range(hp)]

    def tile(j, carries, width, diagonal):
        k0 = pl.multiple_of(j * width, width)
        out = []
        for h in range(hp):
            s = _dot_t(qs[h], k_ref[0, pl.ds(k0, width), h * LANE:(h + 1) * LANE])
            if diagonal:
                r = lax.broadcasted_iota(jnp.int32, (tq, width), 0)
                c = lax.broadcasted_iota(jnp.int32, (tq, width), 1)
                s = jnp.where(c <= r, s, NEG)
            out.append(_online_softmax_step(s, v_ref[0, pl.ds(k0, width), h * LANE:(h + 1) * LANE], carries[h]))
        return tuple(out)

    init = tuple(_softmax_init(tq, LANE) for _ in range(hp))
    carries = lax.fori_loop(0, qi, lambda j, c: tile(j, c, tq, False), init)

    half = tq // 2
    carries = tile(2 * qi, carries, half, True)
    r = lax.broadcasted_iota(jnp.int32, (half, half), 0)
    c = lax.broadcasted_iota(jnp.int32, (half, half), 1)
    k1 = pl.multiple_of(q0 + half, half)
    out = []
    for h in range(hp):
        m, acc = carries[h]
        s = _dot_t(qs[h][half:], k_ref[0, pl.ds(k1, half), h * LANE:(h + 1) * LANE])
        m2, acc2 = _online_softmax_step(jnp.where(c <= r, s, NEG), v_ref[0, pl.ds(k1, half), h * LANE:(h + 1) * LANE],
                                        (m[half:], acc[half:]))
        out.append(jnp.concatenate([acc[:half], acc2], axis=0))
    o_ref[0] = _pack_head_pairs([_normalize(acc) for acc in out]).astype(o_ref.dtype)


def _flash_causal(q, k, v, *, tq, hp):
    bsz, seq, wide = q.shape
    heads = wide // LANE
    assert seq % tq == 0 and hp % 2 == 0
    half_lane = LANE // 2
    kern = functools.partial(_flash_kernel, tq=tq, hp=hp)
    return pl.pallas_call(
        kern,
        grid=(bsz, heads // hp, seq // tq),
        in_specs=[
            pl.BlockSpec((1, tq, hp * LANE), lambda b, h, i: (b, i, h)),
            pl.BlockSpec((1, seq, hp * LANE), lambda b, h, i: (b, 0, h)),
            pl.BlockSpec((1, seq, hp * LANE), lambda b, h, i: (b, 0, h)),
        ],
        out_specs=pl.BlockSpec((1, tq, hp * half_lane), lambda b, h, i: (b, i, h)),
        out_shape=jax.ShapeDtypeStruct((bsz, seq, heads * half_lane), BF),
        compiler_params=_params(("parallel", "parallel", "arbitrary")),
        name="mla_flash",
    )(q, k, v)


def _nsa_proj_kernel(h_ref, w_ref, b_ref, c_ref, s_ref, q_ref, kc_ref, vc_ref, ks_ref, vs_ref, kw_ref, vw_ref, g_ref,
                     *, tiles_per_seq):
    tm = h_ref.shape[0]
    hb = h_ref[...].astype(BF)
    pos = (pl.program_id(0) % tiles_per_seq) * tm + lax.broadcasted_iota(jnp.int32, (tm, LANE), 0)
    lane = lax.broadcasted_iota(jnp.int32, (tm, LANE), 1)
    tag = jnp.where(lane == NSA_DIM + lax.shift_right_logical(pos, SLC_SHIFT), MASK_BIG, 0.0)
    tag2 = jnp.concatenate([tag] * NSA_GROUPS, axis=1)
    c = c_ref[...]
    s = s_ref[...]
    half = NSA_DIM // 2

    def rope(x):
        w = x.shape[1]
        first = lax.broadcasted_iota(jnp.int32, (1, w), 1) % NSA_DIM < half
        rot = jnp.where(first, -pltpu.roll(x, w - half, 1), pltpu.roll(x, half, 1))
        reps = w // LANE
        return x * jnp.concatenate([c] * reps, axis=1) + rot * jnp.concatenate([s] * reps, axis=1)

    low = lax.broadcasted_iota(jnp.int32, (1, LANE), 1) < NSA_DIM

    def spread(x):
        out = []
        for j in range(x.shape[1] // LANE):
            blk = x[:, j * LANE:(j + 1) * LANE]
            out += [jnp.where(low, blk, 0.0), jnp.where(low, pltpu.roll(blk, NSA_DIM, 1), 0.0)]
        return jnp.concatenate(out, axis=1)

    z = _dot(hb, w_ref[...]) + b_ref[...]
    q_ref[...] = (spread(rope(z[:, 0:512])) * (NSA_DIM ** -0.5 * LOG2_E)).astype(BF)
    kc_ref[...] = z[:, 512:640]
    vc_ref[...] = z[:, 640:768]
    ks_ref[...] = (spread(rope(z[:, 768:896])) + tag2).astype(BF)
    ones = _denom_ones(NSA_GROUPS * LANE)
    vs_ref[...] = (spread(z[:, 896:1024]) + ones).astype(BF)
    kw_ref[...] = spread(rope(z[:, 1024:1152])).astype(BF)
    vw_ref[...] = (spread(z[:, 1152:1280]) + ones).astype(BF)
    g_ref[...] = jax.nn.sigmoid(z[:, 1280:1408])


def _nsa_proj(h, w, b, cn, sn, *, layer, tm, seq):
    n, d = h.shape
    tps = seq // tm
    tab = pl.BlockSpec((tm, LANE), lambda i: (i % tps, 0))

    def out(width, dt):
        return pl.BlockSpec((tm, width), lambda i: (i, 0)), jax.ShapeDtypeStruct((n, width), dt)

    outs = [out(1024, BF), out(128, F32), out(128, F32), out(256, BF), out(256, BF), out(256, BF), out(256, BF), out(128, F32)]
    assert seq // SLC_BLOCK <= LANE - NSA_DIM
    return pl.pallas_call(
        functools.partial(_nsa_proj_kernel, tiles_per_seq=tps),
        grid=(n // tm,),
        in_specs=[pl.BlockSpec((tm, d), lambda i: (i, 0)), _layered(w, layer), _layered(b, layer), tab, tab],
        out_specs=[o[0] for o in outs],
        out_shape=[o[1] for o in outs],
        compiler_params=_params(("parallel",)),
        name="nsa_proj",
    )(h, w, b, cn, sn)


def _nsa_cmp_kernel(kc_ref, vc_ref, pe_ref, wk_ref, wv_ref, c_ref, s_ref, kcmp_ref, vcmp_ref):
    n16 = kcmp_ref.shape[1]
    gl = NSA_GROUPS * NSA_DIM
    kc = vc = None
    for l in range(CMP_STRIDE):
        rows = slice(l * gl, (l + 1) * gl)

        def both_halves(x_ref, pe_lo, pe_hi):
            x = x_ref[0, pl.ds(l, n16, stride=CMP_STRIDE), :]
            return jnp.concatenate([x + pe_lo, pltpu.roll(x, n16 - 1, 0) + pe_hi], axis=1).astype(BF)

        pk = _dot(both_halves(kc_ref, pe_ref[0:1, rows], pe_ref[1:2, rows]),
                  jnp.concatenate([wk_ref[0, rows, :], wk_ref[1, rows, :]], axis=0))
        pv = _dot(both_halves(vc_ref, pe_ref[2:3, rows], pe_ref[3:4, rows]),
                  jnp.concatenate([wv_ref[0, rows, :], wv_ref[1, rows, :]], axis=0))
        kc, vc = (pk, pv) if kc is None else (kc + pk, vc + pv)
    w = kc.shape[1]
    half = NSA_DIM // 2
    first = lax.broadcasted_iota(jnp.int32, (1, w), 1) % LANE < half
    rot = jnp.where(first, -pltpu.roll(kc, w - half, 1), pltpu.roll(kc, half, 1))
    kcmp_ref[0] = (kc * c_ref[...] + rot * s_ref[...]).astype(BF)
    vcmp_ref[0] = vc.astype(BF)


def _nsa_compress(kc, vc, pe, wk, wv, cc, sc, *, layer):
    bsz, seq, wide = kc.shape
    n16 = seq // CMP_STRIDE
    blk = pl.BlockSpec((1, seq, wide), lambda b: (b, 0, 0))
    oblk = pl.BlockSpec((1, n16, NSA_GROUPS * LANE), lambda b: (b, 0, 0))
    osh = jax.ShapeDtypeStruct((bsz, n16, NSA_GROUPS * LANE), BF)
    return pl.pallas_call(
        _nsa_cmp_kernel,
        grid=(bsz,),
        in_specs=[blk, blk] + [_layered(a, layer) for a in (pe, wk, wv)] + [_resident(cc.shape), _resident(sc.shape)],
        out_specs=[oblk, oblk],
        out_shape=[osh, osh],
        compiler_params=_params(("parallel",)),
        name="nsa_compress",
    )(kc, vc, pe, wk, wv, cc, sc)


def _nsa_attn_kernel(q_ref, kcmp_ref, vcmp_ref, ks_ref, vs_ref, kw_ref, vw_ref, g_ref, ov_ref, o_ref, *, top_k):
    qi = pl.program_id(1)
    T = q_ref.shape[1]
    Th = T // 2
    R = NSA_HPG * T
    Rh = R // 2
    G = NSA_GROUPS
    q0 = qi * T
    qpos = lax.broadcasted_iota(jnp.int32, (T, 1), 0) + q0
    ncp = kcmp_ref.shape[1]
    nb = ov_ref.shape[0]
    ov_t = ov_ref[...]

    def add_per_query(x, b):
        w = x.shape[1]
        return (x.reshape(2, NSA_HPG, Th, w) + b.reshape(2, 1, Th, w)).reshape(R, w)

    def add_per_query_half(x, b):
        w = x.shape[1]
        return (x.reshape(NSA_HPG, Th, w) + b[None]).reshape(Rh, w)

    def lanes(g):
        return slice(g * LANE, (g + 1) * LANE)

    q4 = [jnp.concatenate([q_ref[0, half * Th:(half + 1) * Th, (g * NSA_HPG + h) * LANE:(g * NSA_HPG + h + 1) * LANE]
                           for half in range(2) for h in range(NSA_HPG)], axis=0) for g in range(G)]

    cmp_end = lax.broadcasted_iota(jnp.int32, (1, ncp), 1) * CMP_STRIDE + (CMP_BLOCK - 1)
    cbias = jnp.where(cmp_end <= qpos, 0.0, NEG)
    any_valid = jnp.where(qpos >= CMP_BLOCK - 1, 1.0, 0.0)
    jr = lax.broadcasted_iota(jnp.int32, (nb, 1), 0)
    jrf = jr.astype(F32)
    jq = lax.shift_right_logical(lax.broadcasted_iota(jnp.int32, (1, T), 1) + q0, SLC_SHIFT)
    forced = (jr == 0) | (jr == jq) | (jr == jq - 1)
    eye_t = jnp.where(lax.broadcasted_iota(jnp.int32, (T, T), 0) == lax.broadcasted_iota(jnp.int32, (T, T), 1),
                      1.0, 0.0).astype(BF)
    o_cmp, q4s = [], []
    for g in range(G):
        sm = _dot_t(q4[g], kcmp_ref[0, :, lanes(g)]).reshape(2, NSA_HPG, Th, ncp) + cbias.reshape(2, 1, Th, ncp)
        e = jnp.exp2(sm - jnp.max(sm, -1, keepdims=True))
        p = e * (any_valid.reshape(2, 1, Th, 1) / jnp.sum(e, -1, keepdims=True))
        o_cmp.append(_dot(p.reshape(R, ncp).astype(BF), vcmp_ref[0, :, lanes(g)]))
        psum = (p[:, 0] + p[:, 1] + p[:, 2] + p[:, 3]).reshape(T, ncp)
        hi = psum.astype(BF)
        r1 = psum - hi.astype(F32)
        mid = r1.astype(BF)
        lo = (r1 - mid.astype(F32)).astype(BF)
        imp = _dot_t(ov_t, hi) + _dot_t(ov_t, mid) + _dot_t(ov_t, lo)
        imp = jnp.where(forced, 1e9, imp)
        imp = jnp.where(jr <= jq, imp, -1.0)
        work = imp
        sel = jnp.zeros_like(imp)
        for _ in range(top_k):
            mx = jnp.max(work, 0, keepdims=True)
            idx = jnp.min(jnp.where(work == mx, jrf, float(nb)), 0, keepdims=True)
            pick = jrf == idx
            sel = jnp.where(pick, 1.0, sel)
            work = jnp.where(pick, -2.0, work)
        unsel_t = jnp.where(imp >= 0.0, sel, 0.0) - 1.0
        pad_t = [jnp.zeros((NSA_DIM, T), F32), unsel_t]
        if LANE - NSA_DIM - nb:
            pad_t.append(jnp.zeros((LANE - NSA_DIM - nb, T), F32))
        unsel = _dot_t(eye_t, jnp.concatenate(pad_t, axis=0).astype(BF)).astype(BF)
        q4s.append(add_per_query(q4[g], unsel))

    def slc_tile(j, carries):
        k0 = pl.multiple_of(j * T, T)
        return tuple(_online_softmax_step(_dot_t(q4s[g], ks_ref[0, pl.ds(k0, T), lanes(g)]),
                                          vs_ref[0, pl.ds(k0, T), lanes(g)], carries[g]) for g in range(G))

    carries = lax.fori_loop(0, qi, slc_tile, tuple(_softmax_init(R, LANE) for _ in range(G)))

    def causal_bias(k0, qp, nk):
        return jnp.where((lax.broadcasted_iota(jnp.int32, (1, nk), 1) + k0) <= qp, 0.0, NEG)

    k_a = pl.multiple_of(q0, Th)
    k_b = pl.multiple_of(q0 + Th, Th)
    bias_a = causal_bias(k_a, qpos, Th)
    bias_b = causal_bias(k_b, qpos[Th:], Th)
    slc_acc = []
    for g in range(G):
        sc = add_per_query(_dot_t(q4s[g], ks_ref[0, pl.ds(k_a, Th), lanes(g)]), bias_a)
        m, acc = _online_softmax_step(sc, vs_ref[0, pl.ds(k_a, Th), lanes(g)], carries[g])
        sc = add_per_query_half(_dot_t(q4s[g][Rh:], ks_ref[0, pl.ds(k_b, Th), lanes(g)]), bias_b)
        _, acc_b = _online_softmax_step(sc, vs_ref[0, pl.ds(k_b, Th), lanes(g)], (m[Rh:], acc[Rh:]))
        slc_acc.append((acc[:Rh], acc_b))

    wk = WINDOW + Th
    o_win = []
    for g in range(G):
        parts = []
        for half in range(2):
            w0 = pl.multiple_of(jnp.maximum(q0 + half * Th - WINDOW, 0), Th)
            dist = qpos[half * Th:(half + 1) * Th] - (lax.broadcasted_iota(jnp.int32, (1, wk), 1) + w0)
            wbias = jnp.where((dist >= 0) & (dist < WINDOW), 0.0, NEG)
            sc = add_per_query_half(_dot_t(q4[g][half * Rh:(half + 1) * Rh], kw_ref[0, pl.ds(w0, wk), lanes(g)]), wbias)
            e = jnp.exp2(sc - jnp.max(sc, -1, keepdims=True))
            parts.append(_normalize(_dot(e.astype(BF), vw_ref[0, pl.ds(w0, wk), lanes(g)])))
        o_win.append(parts)

    gw = NSA_HPG * LANE
    e_row = lax.broadcasted_iota(jnp.int32, (LANE, 3 * gw), 0)
    e_col = lax.broadcasted_iota(jnp.int32, (LANE, 3 * gw), 1)
    branch = jnp.where(e_col >= 2 * gw, 2, jnp.where(e_col >= gw, 1, 0))
    head = lax.shift_right_logical(e_col - branch * gw, LANE.bit_length() - 1)
    gate_col = 3 * head + branch

    def heads_on_lanes(x):
        return jnp.concatenate([x[h * Th:(h + 1) * Th] for h in range(NSA_HPG)], axis=1)

    gs = g_ref[0]
    hi = gs.astype(BF)
    lo = (gs - hi.astype(F32)).astype(BF)
    hi_lo = jnp.concatenate([hi, lo], axis=1)
    packed = NSA_HPG * NSA_DIM
    for g in range(G):
        expand = jnp.where(e_row == gate_col + g * (3 * NSA_HPG), 1.0, 0.0).astype(BF)
        gx = _dot(hi_lo, jnp.concatenate([expand, expand], axis=0))
        for half in range(2):
            rows = slice(half * Th, (half + 1) * Th)
            mixed = (gx[rows, 0:gw] * heads_on_lanes(o_cmp[g][half * Rh:(half + 1) * Rh])
                     + gx[rows, gw:2 * gw] * heads_on_lanes(_normalize(slc_acc[g][half]))
                     + gx[rows, 2 * gw:3 * gw] * heads_on_lanes(o_win[g][half]))
            o_ref[0, rows, g * packed:(g + 1) * packed] = _pack_head_pairs(
                [mixed[:, h * LANE:(h + 1) * LANE] for h in range(NSA_HPG)]).astype(o_ref.dtype)


def _nsa_attention(q, kcmp, vcmp, ks, vs, kw, vw, gates, ov, *, tq, top_k):
    bsz, seq, wide = q.shape
    n16 = kcmp.shape[1]
    gw = NSA_GROUPS * LANE
    assert seq % tq == 0 and seq >= WINDOW + tq
    kern = functools.partial(_nsa_attn_kernel, top_k=top_k)
    cblk = pl.BlockSpec((1, n16, gw), lambda b, i: (b, 0, 0))
    sblk = pl.BlockSpec((1, seq, gw), lambda b, i: (b, 0, 0))
    return pl.pallas_call(
        kern,
        grid=(bsz, seq // tq),
        in_specs=[pl.BlockSpec((1, tq, wide), lambda b, i: (b, i, 0)), cblk, cblk, sblk, sblk, sblk, sblk,
                  pl.BlockSpec((1, tq, LANE), lambda b, i: (b, i, 0)), _resident(ov.shape)],
        out_specs=pl.BlockSpec((1, tq, wide // 2), lambda b, i: (b, i, 0)),
        out_shape=jax.ShapeDtypeStruct((bsz, seq, wide // 2), BF),
        compiler_params=_params(("parallel", "arbitrary")),
        name="nsa_attn",
    )(q, kcmp, vcmp, ks, vs, kw, vw, gates, ov)


def _merge_kernel(x_ref, ab_ref, oc_ref, od_ref, wg_ref, bg_ref, wc_ref, wd_ref, wo_ref, g_ref, b_ref, o_ref):
    x = x_ref[...]
    gates = _dot(x.astype(BF), wg_ref[...]) + bg_ref[...]
    yc = _dot(oc_ref[...], wc_ref[...])
    yd = _dot(od_ref[...], wd_ref[...])
    merged = ab_ref[...] + jax.nn.sigmoid(gates[:, :D_MODEL]) * yc + jax.nn.sigmoid(gates[:, D_MODEL:]) * yd
    mix = _dot(merged.astype(BF), wo_ref[...])
    o_ref[...] = _ln(ALPHA * x + mix, g_ref[...], b_ref[...])


def _merge_ln(x, ab, oc, od, wg, bg, wc, wd, wo, g, b, *, layer, tm):
    n, d = x.shape
    row = pl.BlockSpec((tm, d), lambda i: (i, 0))
    rows = lambda a: pl.BlockSpec((tm, a.shape[1]), lambda i: (i, 0))
    return pl.pallas_call(
        _merge_kernel,
        grid=(n // tm,),
        in_specs=[row, row, rows(oc), rows(od)] + [_layered(a, layer) for a in (wg, bg, wc, wd, wo, g, b)],
        out_specs=row,
        out_shape=jax.ShapeDtypeStruct((n, d), F32),
        compiler_params=_params(("parallel",)),
        name="merge_ln",
    )(x, ab, oc, od, wg, bg, wc, wd, wo, g, b)


def _linear_kernel(x_ref, w_ref, o_ref):
    o_ref[...] = _dot(x_ref[...].astype(BF), w_ref[...]).astype(o_ref.dtype)


def _linear(x, w, *, layer, tm, dtype):
    n, d = x.shape
    return pl.pallas_call(
        _linear_kernel,
        grid=(n // tm,),
        in_specs=[pl.BlockSpec((tm, d), lambda i: (i, 0)), _layered(w, layer)],
        out_specs=pl.BlockSpec((tm, w.shape[2]), lambda i: (i, 0)),
        out_shape=jax.ShapeDtypeStruct((n, w.shape[2]), dtype),
        compiler_params=_params(("parallel",)),
        name="mem_kv",
    )(x, w)


def _xattn_kernel(x_ref, k_ref, v_ref, wq_ref, wo_ref, g_ref, b_ref, o_ref):
    x = x_ref[...]
    q = _dot(x.astype(BF), wq_ref[...]).astype(BF)
    k = k_ref[0]
    v = v_ref[0]
    heads = []
    for h in range(XATTN_HEADS):
        sl = slice(h * XATTN_DIM, (h + 1) * XATTN_DIM)
        s = _dot_t(q[:, sl], k[:, sl]) * (XATTN_DIM ** -0.5 * LOG2_E)
        e = jnp.exp2(s - jnp.max(s, -1, keepdims=True))
        heads.append(_dot(e.astype(BF), v[:, sl]) * (1.0 / jnp.sum(e, -1, keepdims=True)))
    o = jnp.concatenate(heads, axis=1).astype(BF)
    o_ref[...] = _ln(ALPHA * x + _dot(o, wo_ref[...]), g_ref[...], b_ref[...])


def _xattn_ln(x, kv, wq, wo, g, b, *, layer, tm, seq):
    n, d = x.shape
    tps = seq // tm
    mlen = kv.shape[1]
    hd = XATTN_HEADS * XATTN_DIM
    return pl.pallas_call(
        _xattn_kernel,
        grid=(n // tm,),
        in_specs=[pl.BlockSpec((tm, d), lambda i: (i, 0)),
                  pl.BlockSpec((1, mlen, hd), lambda i: (i // tps, 0, 0)),
                  pl.BlockSpec((1, mlen, hd), lambda i: (i // tps, 0, 1))]
        + [_layered(a, layer) for a in (wq, wo, g, b)],
        out_specs=pl.BlockSpec((tm, d), lambda i: (i, 0)),
        out_shape=jax.ShapeDtypeStruct((n, d), F32),
        compiler_params=_params(("parallel",)),
        name="xattn_ln",
    )(x, kv, kv, wq, wo, g, b)


def _rope_tab(pos, dim):
    inv = ROPE_THETA ** (-(jnp.arange(0, dim, 2, dtype=F32) / dim))
    ang = pos[:, None] * inv[None, :]
    return jnp.cos(ang), jnp.sin(ang)


def _rot_cols(w, half):
    return jnp.concatenate([-w[..., half:2 * half], w[..., :half]], axis=-1)


def _layer_params(p):
    w_in, b_in = p["w_in"], p["b_in"]

    def cols(o, wd):
        return w_in[:, o:o + wd], b_in[o:o + wd]

    out = {}
    out["w_ab"] = jnp.concatenate([w_in[:, 0:2560], w_in[:, _O_GA:_O_GC]], axis=1).astype(BF)
    out["b_ab"] = jnp.concatenate([b_in[0:2560], b_in[_O_GA:_O_GC]])[None, :]
    wkr, bkr = cols(_O_KROPE, MLA_ROPE)
    padk =lambda a: jnp.pad(a, [(0, 0)] * (a.ndim - 1) + [(0, LANE - MLA_ROPE)])
    out["w_c"] = jnp.concatenate([w_in[:, _O_QLAT:_O_KROPE], padk(wkr)], axis=1).astype(BF)
    out["b_c"] = jnp.concatenate([b_in[_O_QLAT:_O_KROPE], padk(bkr)])[None, :]
    wuq = p["mla_wuq"].reshape(MLA_Q_RANK, MLA_HEADS, MLA_NOPE + MLA_ROPE)
    wq_c = jnp.pad(wuq, [(0, 0), (0, 0), (0, LANE - MLA_NOPE - MLA_ROPE)])
    out["wq_c"] = wq_c.reshape(MLA_Q_RANK, MLA_HEADS * LANE).astype(BF)
    wqr_c = jnp.pad(_rot_cols(wuq[..., MLA_NOPE:], MLA_ROPE // 2),
                    [(0, 0), (0, 0), (MLA_NOPE, LANE - MLA_NOPE - MLA_ROPE)])
    out["wqr_c"] = wqr_c.reshape(MLA_Q_RANK, MLA_HEADS * LANE).astype(BF)
    wukv = p["mla_wukv"].reshape(MLA_KV_RANK, MLA_HEADS, MLA_NOPE + MLA_V)
    out["wk_c"] = jnp.pad(wukv[..., :MLA_NOPE], [(0, 0), (0, 0), (0, LANE - MLA_NOPE)]).reshape(MLA_KV_RANK, -1).astype(BF)
    out["wv_c"] = jnp.pad(wukv[..., MLA_NOPE:], [(0, 0), (0, 0), (0, LANE - MLA_V)]).reshape(MLA_KV_RANK, -1).astype(BF)
    n_gate = NSA_HEADS * 3
    out["w_d"] = jnp.pad(w_in[:, _O_NQ:_O_NGATE + n_gate], [(0, 0), (0, LANE - n_gate)]).astype(BF)
    out["b_d"] = jnp.pad(b_in[_O_NQ:_O_NGATE + n_gate], [(0, LANE - n_gate)])[None, :]

    def cmp_weights(w):
        eye = jnp.eye(NSA_GROUPS, dtype=F32)
        wp = jnp.pad(w, [(0, 0), (0, 0), (0, LANE - NSA_DIM)])
        full = jnp.einsum("lde,gh->lgdhe", wp, eye).reshape(CMP_BLOCK, NSA_GROUPS * NSA_DIM, NSA_GROUPS * LANE)
        return full.reshape(2, CMP_STRIDE * NSA_GROUPS * NSA_DIM, NSA_GROUPS * LANE).astype(BF)

    out["wcmp_k"] = cmp_weights(p["nsa_wcmp_k"])
    out["wcmp_v"] = cmp_weights(p["nsa_wcmp_v"])

    def pe_rows(pe):
        t = jnp.broadcast_to(pe[:, None, :], (CMP_BLOCK, NSA_GROUPS, NSA_DIM))
        return t.reshape(2, CMP_STRIDE * NSA_GROUPS * NSA_DIM)

    out["pe"] = jnp.concatenate([pe_rows(p["nsa_pe_k"]), pe_rows(p["nsa_pe_v"])], axis=0)
    out["w_g"] = w_in[:, _O_GC:].astype(BF)
    out["b_g"] = b_in[_O_GC:][None, :]
    out["wout_c"] = p["mla_wout"].astype(BF)
    out["wout_d"] = p["nsa_wout"].astype(BF)
    out["gmlp_bs_t"] = p["gmlp_bs"].T
    for name in ("gmlp_wout", "conv_wout", "w_o", "xattn_wq", "xattn_wo"):
        out[name] = p[name].astype(BF)
    out["xattn_wkv"] = jnp.concatenate([p["xattn_wk"], p["xattn_wv"]], axis=1).astype(BF)
    return out


def _tables(seq):
    pos = jnp.arange(seq, dtype=F32)
    c16, s16 = _rope_tab(pos, MLA_ROPE)
    one = jnp.ones((seq, MLA_NOPE), F32)
    zero = jnp.zeros((seq, MLA_NOPE), F32)
    tail = LANE - MLA_NOPE - MLA_ROPE
    cq = jnp.concatenate([one, c16, c16, jnp.ones((seq, tail), F32)], axis=1)
    sq = jnp.concatenate([zero, s16, s16, jnp.zeros((seq, tail), F32)], axis=1)
    ck = jnp.pad(jnp.concatenate([c16, c16], axis=1), [(0, 0), (0, LANE - MLA_ROPE)])
    sk = jnp.pad(jnp.concatenate([s16, s16], axis=1), [(0, 0), (0, LANE - MLA_ROPE)])
    c32, s32 = _rope_tab(pos, NSA_DIM)
    cn = jnp.concatenate([c32, c32] * (LANE // NSA_DIM), axis=1)
    sn = jnp.concatenate([s32, s32] * (LANE // NSA_DIM), axis=1)
    n16 = seq // CMP_STRIDE
    cend = (jnp.arange(n16) * CMP_STRIDE + CMP_BLOCK - 1).astype(F32)
    cc32, cs32 = _rope_tab(cend, NSA_DIM)
    ccg = jnp.pad(jnp.concatenate([cc32, cc32], axis=1), [(0, 0), (0, LANE - NSA_DIM)])
    csg = jnp.pad(jnp.concatenate([cs32, cs32], axis=1), [(0, 0), (0, LANE - NSA_DIM)])
    cc = jnp.concatenate([ccg] * NSA_GROUPS, axis=1)
    cs = jnp.concatenate([csg] * NSA_GROUPS, axis=1)
    n_cmp = (seq - CMP_BLOCK) // CMP_STRIDE + 1
    n_slc = seq // SLC_BLOCK
    cstart = jnp.arange(n16) * CMP_STRIDE
    sstart = jnp.arange(n_slc) * SLC_BLOCK
    ovl = (jnp.minimum(cstart[None, :] + CMP_BLOCK, sstart[:, None] + SLC_BLOCK)
           - jnp.maximum(cstart[None, :], sstart[:, None]))
    ovl = jnp.clip(ovl, 0).astype(F32) / CMP_BLOCK
    ovl = jnp.where(jnp.arange(n16)[None, :] < n_cmp, ovl, 0.0).astype(BF)
    return dict(cq=cq, sq=sq, ck=ck, sk=sk, cn=cn, sn=sn, cc=cc, cs=cs, ovl=ovl)


def kernel(x, mem, ffn1_w1, ffn1_w3, ffn1_w2, ln1_g, ln1_b, w_in, b_in, gmlp_ln_g, gmlp_ln_b, gmlp_ws, gmlp_bs, gmlp_wout, conv_w, conv_wout, mla_qnorm_g, mla_kvnorm_g, mla_wuq, mla_wukv, mla_wout, nsa_pe_k, nsa_pe_v, nsa_wcmp_k, nsa_wcmp_v, nsa_wout, w_o, ln2_g, ln2_b, xattn_wq, xattn_wk, xattn_wv, xattn_wo, ln3_g, ln3_b, ffn2_w1, ffn2_w3, ffn2_w2, ln4_g, ln4_b):
    bsz, seq, d = x.shape
    mlen = mem.shape[1]
    n = bsz * seq
    assert d == D_MODEL and seq % 1024 == 0
    lp = jax.vmap(_layer_params)(dict(
        w_in=w_in, b_in=b_in, mla_wuq=mla_wuq, mla_wukv=mla_wukv, mla_wout=mla_wout, nsa_pe_k=nsa_pe_k,
        nsa_pe_v=nsa_pe_v, nsa_wcmp_k=nsa_wcmp_k, nsa_wcmp_v=nsa_wcmp_v, nsa_wout=nsa_wout, gmlp_bs=gmlp_bs,
        gmlp_wout=gmlp_wout, conv_wout=conv_wout, w_o=w_o, xattn_wq=xattn_wq, xattn_wk=xattn_wk,
        xattn_wv=xattn_wv, xattn_wo=xattn_wo))
    tb = _tables(seq)
    tm = 1024
    tm_merge = 512
    tf = D_FF // 11
    top_k = min(SLC_TOPK, seq // SLC_BLOCK)
    row = lambda a: a[:, None, :]

    def cast_stacked(w):
        return _to_bf16(w.reshape(w.shape[0] * w.shape[1], w.shape[2])).reshape(w.shape)

    ffn1 = [cast_stacked(w) for w in (ffn1_w1, ffn1_w3, ffn1_w2)]
    ffn2 = [cast_stacked(w) for w in (ffn2_w1, ffn2_w3, ffn2_w2)]
    h = x.reshape(n, d)
    mem2 = mem.reshape(bsz * mlen, d)
    wide = MLA_HEADS * LANE
    gw = NSA_GROUPS * LANE
    for l in range(DEPTH):
        h = _ffn_ln(h, *ffn1, row(ln1_g), row(ln1_b), layer=l, tm=tm, tf=tf)
        ab = _mix_ab(h, lp["w_ab"], lp["b_ab"], row(gmlp_ln_g), row(gmlp_ln_b), gmlp_ws, lp["gmlp_bs_t"],
                     lp["gmlp_wout"], conv_w, lp["conv_wout"], layer=l, tm=tm, seq=seq)
        qc, kc_, vc_ = _mla_proj(h, lp["w_c"], lp["b_c"], row(mla_qnorm_g), row(mla_kvnorm_g),
                                 lp["wq_c"], lp["wqr_c"], lp["wk_c"], lp["wv_c"],
                                 tb["cq"], tb["sq"], tb["ck"], tb["sk"], layer=l, tm=tm, seq=seq)
        oc = _flash_causal(qc.reshape(bsz, seq, wide), kc_.reshape(bsz, seq, wide), vc_.reshape(bsz, seq, wide), tq=1024, hp=4)
        qn, nkc, nvc, nks, nvs, nkw, nvw, gates = _nsa_proj(h, lp["w_d"], lp["b_d"], tb["cn"], tb["sn"],
                                                            layer=l, tm=tm, seq=seq)
        kcmp, vcmp = _nsa_compress(nkc.reshape(bsz, seq, NSA_GROUPS * NSA_DIM), nvc.reshape(bsz, seq, NSA_GROUPS * NSA_DIM),
                                   lp["pe"], lp["wcmp_k"], lp["wcmp_v"], tb["cc"], tb["cs"], layer=l)
        od = _nsa_attention(qn.reshape(bsz, seq, NSA_HEADS * LANE), kcmp, vcmp,
                            nks.reshape(bsz, seq, gw), nvs.reshape(bsz, seq, gw),
                            nkw.reshape(bsz, seq, gw), nvw.reshape(bsz, seq, gw),
                            gates.reshape(bsz, seq, LANE), tb["ovl"], tq=512, top_k=top_k)
        h = _merge_ln(h, ab, oc.reshape(n, MLA_HEADS * MLA_V), od.reshape(n, NSA_HEADS * NSA_DIM), lp["w_g"], lp["b_g"],
                      lp["wout_c"], lp["wout_d"], lp["w_o"], row(ln2_g), row(ln2_b), layer=l, tm=tm_merge)
        kv = _linear(mem2, lp["xattn_wkv"], layer=l, tm=min(256, bsz * mlen), dtype=BF)
        h = _xattn_ln(h, kv.reshape(bsz, mlen, 2 * XATTN_HEADS * XATTN_DIM), lp["xattn_wq"], lp["xattn_wo"],
                      row(ln3_g), row(ln3_b), layer=l, tm=tm, seq=seq)
        h = _ffn_ln(h, *ffn2, row(ln4_g), row(ln4_b), layer=l, tm=tm, tf=tf)
    return h.reshape(bsz, seq, d)
```

```python
import functools

import jax
import jax.numpy as jnp
from jax import lax
from jax.experimental import pallas as pl
from jax.experimental.pallas import tpu as pltpu

BF = jnp.bfloat16
F32 = jnp.float32

D_MODEL = 1024
D_FF = 2816
LN_EPS = 1e-5
RMS_EPS = 1e-6
ROPE_THETA = 10000.0
DEPTH = 2
ALPHA = (2 * DEPTH) ** 0.25
NEG = -1e30
LOG2_E = 1.4426950408889634
DENOM_LANE = 64
MASK_BIG = 2.0 ** 100

GMLP_CHUNK = 128
GMLP_GROUPS = 4
GMLP_WIDTH = 512
CONV_WIDTH = 512
CONV_K = 3
MLA_HEADS = 8
MLA_Q_RANK = 256
MLA_KV_RANK = 128
MLA_NOPE = 64
MLA_ROPE = 32
MLA_V = 64
NSA_HEADS = 8
NSA_GROUPS = 2
NSA_HPG = 4
NSA_DIM = 64
CMP_BLOCK = 32
CMP_STRIDE = 16
SLC_BLOCK = 64
SLC_SHIFT = SLC_BLOCK.bit_length() - 1
SLC_TOPK = 8
WINDOW = 512
XATTN_HEADS = 4
XATTN_DIM = 128

LANE = 128
CONV_HALO = 8
VMEM_LIMIT = 56 * 1024 * 1024

_O_U, _O_V, _O_CB, _O_CC, _O_CH = 0, 512, 1024, 1536, 2048
_O_QLAT, _O_KVLAT, _O_KROPE = 2560, 2816, 2944
_O_NQ, _O_NKC, _O_NVC, _O_NKS, _O_NVS, _O_NKW, _O_NVW, _O_NGATE = 2976, 3488, 3616, 3744, 3872, 4000, 4128, 4256
_O_GA, _O_GB, _O_GC, _O_GD = 4280, 5304, 6328, 7352


def _dot(a, b):
    return jnp.dot(a, b, preferred_element_type=F32)


def _dot_t(a, b):
    return lax.dot_general(a, b, (((1,), (1,)), ((), ())), preferred_element_type=F32)


def _ln(y, g, b):
    mu = jnp.mean(y, -1, keepdims=True)
    d = y - mu
    var = jnp.mean(d * d, -1, keepdims=True)
    return d * lax.rsqrt(var + LN_EPS) * g + b


def _rms(x, g):
    return x * lax.rsqrt(jnp.mean(x * x, -1, keepdims=True) + RMS_EPS) * g


def _resident(shape):
    n = len(shape)
    return pl.BlockSpec(shape, lambda *_: (0,) * n, pipeline_mode=pl.Buffered(1))


def _layered(a, layer):
    n = a.ndim - 1
    return pl.BlockSpec((None,) + a.shape[1:], lambda *_: (layer,) + (0,) * n, pipeline_mode=pl.Buffered(1))


def _params(sem):
    return pltpu.CompilerParams(dimension_semantics=sem, vmem_limit_bytes=VMEM_LIMIT)


def _cast_kernel(x_ref, o_ref):
    o_ref[...] = x_ref[...].astype(o_ref.dtype)


def _to_bf16(w, *, rows=512):
    r, c = w.shape
    rows = min(rows, r)
    assert r % rows == 0
    return pl.pallas_call(
        _cast_kernel,
        grid=(r // rows,),
        in_specs=[pl.BlockSpec((rows, c), lambda i: (i, 0))],
        out_specs=pl.BlockSpec((rows, c), lambda i: (i, 0)),
        out_shape=jax.ShapeDtypeStruct((r, c), BF),
        compiler_params=_params(("parallel",)),
        name="to_bf16",
    )(w)


def _ffn_ln_kernel(x_ref, w1_ref, w3_ref, w2_ref, g_ref, b_ref, o_ref, hh_ref, *, tf):
    x = x_ref[...]
    xb = x.astype(BF)
    for c in range(w1_ref.shape[1] // tf):
        cols = slice(c * tf, (c + 1) * tf)
        h1 = _dot(xb, w1_ref[:, cols])
        h3 = _dot(xb, w3_ref[:, cols])
        hh_ref[:, cols] = ((h1 * jax.nn.sigmoid(h1)) * h3).astype(BF)
    o_ref[...] = _ln(ALPHA * x + 0.5 * _dot(hh_ref[...], w2_ref[...]), g_ref[...], b_ref[...])


def _ffn_ln(x, w1, w3, w2, g, b, *, layer, tm, tf):
    n, d = x.shape
    return pl.pallas_call(
        functools.partial(_ffn_ln_kernel, tf=tf),
        grid=(n // tm,),
        in_specs=[pl.BlockSpec((tm, d), lambda i: (i, 0))] + [_layered(a, layer) for a in (w1, w3, w2, g, b)],
        out_specs=pl.BlockSpec((tm, d), lambda i: (i, 0)),
        out_shape=jax.ShapeDtypeStruct((n, d), F32),
        scratch_shapes=[pltpu.VMEM((tm, w1.shape[2]), BF)],
        compiler_params=_params(("parallel",)),
        name="ffn_ln",
    )(x, w1, w3, w2, g, b)


def _ab_kernel(h_ref, w_ref, b_ref, lng_ref, lnb_ref, ws_ref, bst_ref, wga_ref, cw_ref, wcb_ref,
               o_ref, prev_ref, *, tiles_per_seq):
    i = pl.program_id(0)
    tm = h_ref.shape[0]
    hb = h_ref[...].astype(BF)

    def proj(c0, width):
        return _dot(hb, w_ref[:, c0:c0 + width]) + b_ref[:, c0:c0 + width]

    u = proj(0, GMLP_WIDTH)
    v = _ln(proj(512, GMLP_WIDTH), lng_ref[...], lnb_ref[...]).astype(BF)
    row = lax.broadcasted_iota(jnp.int32, (GMLP_CHUNK, GMLP_CHUNK), 0)
    col = lax.broadcasted_iota(jnp.int32, (GMLP_CHUNK, GMLP_CHUNK), 1)
    gd = GMLP_WIDTH // GMLP_GROUPS
    wgs = [jnp.where(row >= col, ws_ref[g], 0.0).astype(BF) for g in range(GMLP_GROUPS)]
    chunks = []
    for c in range(tm // GMLP_CHUNK):
        r0 = c * GMLP_CHUNK
        chunks.append(jnp.concatenate(
            [_dot(wgs[g], v[r0:r0 + GMLP_CHUNK, g * gd:(g + 1) * gd]) + bst_ref[:, g:g + 1]
             for g in range(GMLP_GROUPS)], axis=1))
    s = jnp.concatenate(chunks, axis=0)
    ya = _dot((u * s).astype(BF), wga_ref[...])

    cb = proj(1024, CONV_WIDTH)
    z = proj(1536, CONV_WIDTH) * proj(2048, CONV_WIDTH)

    @pl.when(i % tiles_per_seq == 0)
    def _():
        prev_ref[...] = jnp.zeros_like(prev_ref)

    zext = jnp.concatenate([prev_ref[...], z], axis=0)
    z1 = pltpu.roll(zext, 1, 0)[CONV_HALO:]
    z2 = pltpu.roll(zext, 2, 0)[CONV_HALO:]
    y = cw_ref[0:1, :] * z2 + cw_ref[1:2, :] * z1 + cw_ref[2:3, :] * z
    prev_ref[...] = z[tm - CONV_HALO:, :]
    yb = _dot((cb * y).astype(BF), wcb_ref[...])

    ga = proj(2560, D_MODEL)
    gb = proj(3584, D_MODEL)
    o_ref[...] = jax.nn.sigmoid(ga) * ya + jax.nn.sigmoid(gb) * yb


def _mix_ab(h, w, b, lng, lnb, ws, bst, wga, cw, wcb, *, layer, tm, seq):
    n, d = h.shape
    kern = functools.partial(_ab_kernel, tiles_per_seq=seq // tm)
    return pl.pallas_call(
        kern,
        grid=(n // tm,),
        in_specs=[pl.BlockSpec((tm, d), lambda i: (i, 0))]
        + [_layered(a, layer) for a in (w, b, lng, lnb, ws, bst, wga, cw, wcb)],
        out_specs=pl.BlockSpec((tm, d), lambda i: (i, 0)),
        out_shape=jax.ShapeDtypeStruct((n, d), F32),
        scratch_shapes=[pltpu.VMEM((CONV_HALO, CONV_WIDTH), F32)],
        compiler_params=_params(("arbitrary",)),
        name="mix_ab",
    )(h, w, b, lng, lnb, ws, bst, wga, cw, wcb)


def _mla_proj_kernel(h_ref, w_ref, b_ref, qg_ref, kvg_ref, wq_ref, wqr_ref, wk_ref, wv_ref,
                     cq_ref, sq_ref, ck_ref, sk_ref, q_ref, k_ref, v_ref):
    hb = h_ref[...].astype(BF)
    z = _dot(hb, w_ref[...]) + b_ref[...]
    qn = _rms(z[:, 0:256], qg_ref[...]).astype(BF)
    kvn = _rms(z[:, 256:384], kvg_ref[...]).astype(BF)
    half = MLA_ROPE // 2

    def rotate_half(x, start):
        w = x.shape[1]
        first = lax.broadcasted_iota(jnp.int32, (1, w), 1) % LANE < start + half
        return jnp.where(first, -pltpu.roll(x, w - half, 1), pltpu.roll(x, half, 1))

    cq = jnp.concatenate([cq_ref[...]] * MLA_HEADS, axis=1)
    sq = jnp.concatenate([sq_ref[...]] * MLA_HEADS, axis=1)
    scale = (MLA_NOPE + MLA_ROPE) ** -0.5 * LOG2_E
    q_ref[...] = ((_dot(qn, wq_ref[...]) * cq + _dot(qn, wqr_ref[...]) * sq) * scale).astype(BF)
    kr = z[:, 384:512]
    kpe = pltpu.roll(kr * ck_ref[...] + rotate_half(kr, 0) * sk_ref[...], MLA_NOPE, 1)
    k_ref[...] = (_dot(kvn, wk_ref[...]) + jnp.concatenate([kpe] * MLA_HEADS, axis=1)).astype(BF)
    v_ref[...] = (_dot(kvn, wv_ref[...]) + _denom_ones(v_ref.shape[1])).astype(BF)


def _mla_proj(h, w, b, qg, kvg, wq, wqr, wk, wv, cq, sq, ck, sk, *, layer, tm, seq):
    n, d = h.shape
    tps = seq // tm
    tab = pl.BlockSpec((tm, LANE), lambda i: (i % tps, 0))
    wide = MLA_HEADS * LANE
    out = jax.ShapeDtypeStruct((n, wide), BF)
    return pl.pallas_call(
        _mla_proj_kernel,
        grid=(n // tm,),
        in_specs=[pl.BlockSpec((tm, d), lambda i: (i, 0))]
        + [_layered(a, layer) for a in (w, b, qg, kvg, wq, wqr, wk, wv)] + [tab] * 4,
        out_specs=[pl.BlockSpec((tm, wide), lambda i: (i, 0))] * 3,
        out_shape=[out, out, out],
        compiler_params=_params(("parallel",)),
        name="mla_proj",
    )(h, w, b, qg, kvg, wq, wqr, wk, wv, cq, sq, ck, sk)


def _online_softmax_step(s, v, carry):
    m, acc = carry
    m_new = jnp.maximum(m, jnp.max(s, -1, keepdims=True))
    p = jnp.exp2(s - m_new)
    acc = jnp.exp2(m - m_new) * acc + _dot(p.astype(BF), v)
    return m_new, acc


def _softmax_init(rows, width):
    return (jnp.full((rows, 1), NEG, F32), jnp.zeros((rows, width), F32))


def _normalize(acc):
    return acc * (1.0 / acc[:, DENOM_LANE:DENOM_LANE + 1])


def _pack_head_pairs(slots):
    low = lax.broadcasted_iota(jnp.int32, (1, LANE), 1) < DENOM_LANE
    return jnp.concatenate([jnp.where(low, a, pltpu.roll(b, DENOM_LANE, 1))
                            for a, b in zip(slots[0::2], slots[1::2])], axis=1)


def _denom_ones(width):
    lane = lax.broadcasted_iota(jnp.int32, (1, width), 1)
    return jnp.where(lane % LANE == DENOM_LANE, 1.0, 0.0)


def _flash_kernel(q_ref, k_ref, v_ref, o_ref, *, tq, hp):
    qi = pl.program_id(2)
    q0 = qi * tq
    qs = [q_ref[0, :, h * LANE:(h + 1) * LANE] for h in range(hp)]

    def tile(j, carries, width, diagonal):
        k0 = pl.multiple_of(j * width, width)
        out = []
        for h in range(hp):
            s = _dot_t(qs[h], k_ref[0, pl.ds(k0, width), h * LANE:(h + 1) * LANE])
            if diagonal:
                r = lax.broadcasted_iota(jnp.int32, (tq, width), 0)
                c = lax.broadcasted_iota(jnp.int32, (tq, width), 1)
                s = jnp.where(c <= r, s, NEG)
            out.append(_online_softmax_step(s, v_ref[0, pl.ds(k0, width), h * LANE:(h + 1) * LANE], carries[h]))
        return tuple(out)

    init = tuple(_softmax_init(tq, LANE) for _ in range(hp))
    carries = lax.fori_loop(0, qi, lambda j, c: tile(j, c, tq, False), init)

    half = tq // 2
    carries = tile(2 * qi, carries, half, True)
    r = lax.broadcasted_iota(jnp.int32, (half, half), 0)
    c = lax.broadcasted_iota(jnp.int32, (half, half), 1)
    k1 = pl.multiple_of(q0 + half, half)
    out = []
    for h in range(hp):
        m, acc = carries[h]
        s = _dot_t(qs[h][half:], k_ref[0, pl.ds(k1, half), h * LANE:(h + 1) * LANE])
        m2, acc2 = _online_softmax_step(jnp.where(c <= r, s, NEG), v_ref[0, pl.ds(k1, half), h * LANE:(h + 1) * LANE],
                                        (m[half:], acc[half:]))
        out.append(jnp.concatenate([acc[:half], acc2], axis=0))
    o_ref[0] = _pack_head_pairs([_normalize(acc) for acc in out]).astype(o_ref.dtype)


def _flash_causal(q, k, v, *, tq, hp):
    bsz, seq, wide = q.shape
    heads = wide // LANE
    assert seq % tq == 0 and hp % 2 == 0
    half_lane = LANE // 2
    kern = functools.partial(_flash_kernel, tq=tq, hp=hp)
    return pl.pallas_call(
        kern,
        grid=(bsz, heads // hp, seq // tq),
        in_specs=[
            pl.BlockSpec((1, tq, hp * LANE), lambda b, h, i: (b, i, h)),
            pl.BlockSpec((1, seq, hp * LANE), lambda b, h, i: (b, 0, h)),
            pl.BlockSpec((1, seq, hp * LANE), lambda b, h, i: (b, 0, h)),
        ],
        out_specs=pl.BlockSpec((1, tq, hp * half_lane), lambda b, h, i: (b, i, h)),
        out_shape=jax.ShapeDtypeStruct((bsz, seq, heads * half_lane), BF),
        compiler_params=_params(("parallel", "parallel", "arbitrary")),
        name="mla_flash",
    )(q, k, v)


def _nsa_proj_kernel(h_ref, w_ref, b_ref, c_ref, s_ref, q_ref, kc_ref, vc_ref, ks_ref, vs_ref, kw_ref, vw_ref, g_ref,
                     *, tiles_per_seq):
    tm = h_ref.shape[0]
    hb = h_ref[...].astype(BF)
    pos = (pl.program_id(0) % tiles_per_seq) * tm + lax.broadcasted_iota(jnp.int32, (tm, LANE), 0)
    lane = lax.broadcasted_iota(jnp.int32, (tm, LANE), 1)
    tag = jnp.where(lane == NSA_DIM + lax.shift_right_logical(pos, SLC_SHIFT), MASK_BIG, 0.0)
    tag2 = jnp.concatenate([tag] * NSA_GROUPS, axis=1)
    c = c_ref[...]
    s = s_ref[...]
    half = NSA_DIM // 2

    def rope(x):
        w = x.shape[1]
        first = lax.broadcasted_iota(jnp.int32, (1, w), 1) % NSA_DIM < half
        rot = jnp.where(first, -pltpu.roll(x, w - half, 1), pltpu.roll(x, half, 1))
        reps = w // LANE
        return x * jnp.concatenate([c] * reps, axis=1) + rot * jnp.concatenate([s] * reps, axis=1)

    low = lax.broadcasted_iota(jnp.int32, (1, LANE), 1) < NSA_DIM

    def spread(x):
        out = []
        for j in range(x.shape[1] // LANE):
            blk = x[:, j * LANE:(j + 1) * LANE]
            out += [jnp.where(low, blk, 0.0), jnp.where(low, pltpu.roll(blk, NSA_DIM, 1), 0.0)]
        return jnp.concatenate(out, axis=1)

    z = _dot(hb, w_ref[...]) + b_ref[...]
    q_ref[...] = (spread(rope(z[:, 0:512])) * (NSA_DIM ** -0.5 * LOG2_E)).astype(BF)
    kc_ref[...] = z[:, 512:640]
    vc_ref[...] = z[:, 640:768]
    ks_ref[...] = (spread(rope(z[:, 768:896])) + tag2).astype(BF)
    ones = _denom_ones(NSA_GROUPS * LANE)
    vs_ref[...] = (spread(z[:, 896:1024]) + ones).astype(BF)
    kw_ref[...] = spread(rope(z[:, 1024:1152])).astype(BF)
    vw_ref[...] = (spread(z[:, 1152:1280]) + ones).astype(BF)
    g_ref[...] = jax.nn.sigmoid(z[:, 1280:1408])


def _nsa_proj(h, w, b, cn, sn, *, layer, tm, seq):
    n, d = h.shape
    tps = seq // tm
    tab = pl.BlockSpec((tm, LANE), lambda i: (i % tps, 0))

    def out(width, dt):
        return pl.BlockSpec((tm, width), lambda i: (i, 0)), jax.ShapeDtypeStruct((n, width), dt)

    outs = [out(1024, BF), out(128, F32), out(128, F32), out(256, BF), out(256, BF), out(256, BF), out(256, BF), out(128, F32)]
    assert seq // SLC_BLOCK <= LANE - NSA_DIM
    return pl.pallas_call(
        functools.partial(_nsa_proj_kernel, tiles_per_seq=tps),
        grid=(n // tm,),
        in_specs=[pl.BlockSpec((tm, d), lambda i: (i, 0)), _layered(w, layer), _layered(b, layer), tab, tab],
        out_specs=[o[0] for o in outs],
        out_shape=[o[1] for o in outs],
        compiler_params=_params(("parallel",)),
        name="nsa_proj",
    )(h, w, b, cn, sn)


def _nsa_cmp_kernel(kc_ref, vc_ref, pe_ref, wk_ref, wv_ref, c_ref, s_ref, kcmp_ref, vcmp_ref):
    n16 = kcmp_ref.shape[1]
    gl = NSA_GROUPS * NSA_DIM
    kc = vc = None
    for l in range(CMP_STRIDE):
        rows = slice(l * gl, (l + 1) * gl)

        def both_halves(x_ref, pe_lo, pe_hi):
            x = x_ref[0, pl.ds(l, n16, stride=CMP_STRIDE), :]
            return jnp.concatenate([x + pe_lo, pltpu.roll(x, n16 - 1, 0) + pe_hi], axis=1).astype(BF)

        pk = _dot(both_halves(kc_ref, pe_ref[0:1, rows], pe_ref[1:2, rows]),
                  jnp.concatenate([wk_ref[0, rows, :], wk_ref[1, rows, :]], axis=0))
        pv = _dot(both_halves(vc_ref, pe_ref[2:3, rows], pe_ref[3:4, rows]),
                  jnp.concatenate([wv_ref[0, rows, :], wv_ref[1, rows, :]], axis=0))
        kc, vc = (pk, pv) if kc is None else (kc + pk, vc + pv)
    w = kc.shape[1]
    half = NSA_DIM // 2
    first = lax.broadcasted_iota(jnp.int32, (1, w), 1) % LANE < half
    rot = jnp.where(first, -pltpu.roll(kc, w - half, 1), pltpu.roll(kc, half, 1))
    kcmp_ref[0] = (kc * c_ref[...] + rot * s_ref[...]).astype(BF)
    vcmp_ref[0] = vc.astype(BF)


def _nsa_compress(kc, vc, pe, wk, wv, cc, sc, *, layer):
    bsz, seq, wide = kc.shape
    n16 = seq // CMP_STRIDE
    blk = pl.BlockSpec((1, seq, wide), lambda b: (b, 0, 0))
    oblk = pl.BlockSpec((1, n16, NSA_GROUPS * LANE), lambda b: (b, 0, 0))
    osh = jax.ShapeDtypeStruct((bsz, n16, NSA_GROUPS * LANE), BF)
    return pl.pallas_call(
        _nsa_cmp_kernel,
        grid=(bsz,),
        in_specs=[blk, blk] + [_layered(a, layer) for a in (pe, wk, wv)] + [_resident(cc.shape), _resident(sc.shape)],
        out_specs=[oblk, oblk],
        out_shape=[osh, osh],
        compiler_params=_params(("parallel",)),
        name="nsa_compress",
    )(kc, vc, pe, wk, wv, cc, sc)


def _nsa_attn_kernel(q_ref, kcmp_ref, vcmp_ref, ks_ref, vs_ref, kw_ref, vw_ref, g_ref, ov_ref, o_ref, *, top_k):
    qi = pl.program_id(1)
    T = q_ref.shape[1]
    Th = T // 2
    R = NSA_HPG * T
    Rh = R // 2
    G = NSA_GROUPS
    q0 = qi * T
    qpos = lax.broadcasted_iota(jnp.int32, (T, 1), 0) + q0
    ncp = kcmp_ref.shape[1]
    nb = ov_ref.shape[0]
    ov_t = ov_ref[...]

    def add_per_query(x, b):
        w = x.shape[1]
        return (x.reshape(2, NSA_HPG, Th, w) + b.reshape(2, 1, Th, w)).reshape(R, w)

    def add_per_query_half(x, b):
        w = x.shape[1]
        return (x.reshape(NSA_HPG, Th, w) + b[None]).reshape(Rh, w)

    def lanes(g):
        return slice(g * LANE, (g + 1) * LANE)

    q4 = [jnp.concatenate([q_ref[0, half * Th:(half + 1) * Th, (g * NSA_HPG + h) * LANE:(g * NSA_HPG + h + 1) * LANE]
                           for half in range(2) for h in range(NSA_HPG)], axis=0) for g in range(G)]

    cmp_end = lax.broadcasted_iota(jnp.int32, (1, ncp), 1) * CMP_STRIDE + (CMP_BLOCK - 1)
    cbias = jnp.where(cmp_end <= qpos, 0.0, NEG)
    any_valid = jnp.where(qpos >= CMP_BLOCK - 1, 1.0, 0.0)
    jr = lax.broadcasted_iota(jnp.int32, (nb, 1), 0)
    jrf = jr.astype(F32)
    jq = lax.shift_right_logical(lax.broadcasted_iota(jnp.int32, (1, T), 1) + q0, SLC_SHIFT)
    forced = (jr == 0) | (jr == jq) | (jr == jq - 1)
    eye_t = jnp.where(lax.broadcasted_iota(jnp.int32, (T, T), 0) == lax.broadcasted_iota(jnp.int32, (T, T), 1),
                      1.0, 0.0).astype(BF)
    o_cmp, q4s = [], []
    for g in range(G):
        sm = _dot_t(q4[g], kcmp_ref[0, :, lanes(g)]).reshape(2, NSA_HPG, Th, ncp) + cbias.reshape(2, 1, Th, ncp)
        e = jnp.exp2(sm - jnp.max(sm, -1, keepdims=True))
        p = e * (any_valid.reshape(2, 1, Th, 1) / jnp.sum(e, -1, keepdims=True))
        o_cmp.append(_dot(p.reshape(R, ncp).astype(BF), vcmp_ref[0, :, lanes(g)]))
        psum = (p[:, 0] + p[:, 1] + p[:, 2] + p[:, 3]).reshape(T, ncp)
        hi = psum.astype(BF)
        r1 = psum - hi.astype(F32)
        mid = r1.astype(BF)
        lo = (r1 - mid.astype(F32)).astype(BF)
        imp = _dot_t(ov_t, hi) + _dot_t(ov_t, mid) + _dot_t(ov_t, lo)
        imp = jnp.where(forced, 1e9, imp)
        imp = jnp.where(jr <= jq, imp, -1.0)
        work = imp
        sel = jnp.zeros_like(imp)
        for _ in range(top_k):
            mx = jnp.max(work, 0, keepdims=True)
            idx = jnp.min(jnp.where(work == mx, jrf, float(nb)), 0, keepdims=True)
            pick = jrf == idx
            sel = jnp.where(pick, 1.0, sel)
            work = jnp.where(pick, -2.0, work)
        unsel_t = jnp.where(imp >= 0.0, sel, 0.0) - 1.0
        pad_t = [jnp.zeros((NSA_DIM, T), F32), unsel_t]
        if LANE - NSA_DIM - nb:
            pad_t.append(jnp.zeros((LANE - NSA_DIM - nb, T), F32))
        unsel = _dot_t(eye_t, jnp.concatenate(pad_t, axis=0).astype(BF)).astype(BF)
        q4s.append(add_per_query(q4[g], unsel))

    def slc_tile(j, carries):
        k0 = pl.multiple_of(j * T, T)
        return tuple(_online_softmax_step(_dot_t(q4s[g], ks_ref[0, pl.ds(k0, T), lanes(g)]),
                                          vs_ref[0, pl.ds(k0, T), lanes(g)], carries[g]) for g in range(G))

    carries = lax.fori_loop(0, qi, slc_tile, tuple(_softmax_init(R, LANE) for _ in range(G)))

    def causal_bias(k0, qp, nk):
        return jnp.where((lax.broadcasted_iota(jnp.int32, (1, nk), 1) + k0) <= qp, 0.0, NEG)

    k_a = pl.multiple_of(q0, Th)
    k_b = pl.multiple_of(q0 + Th, Th)
    bias_a = causal_bias(k_a, qpos, Th)
    bias_b = causal_bias(k_b, qpos[Th:], Th)
    slc_acc = []
    for g in range(G):
        sc = add_per_query(_dot_t(q4s[g], ks_ref[0, pl.ds(k_a, Th), lanes(g)]), bias_a)
        m, acc = _online_softmax_step(sc, vs_ref[0, pl.ds(k_a, Th), lanes(g)], carries[g])
        sc = add_per_query_half(_dot_t(q4s[g][Rh:], ks_ref[0, pl.ds(k_b, Th), lanes(g)]), bias_b)
        _, acc_b = _online_softmax_step(sc, vs_ref[0, pl.ds(k_b, Th), lanes(g)], (m[Rh:], acc[Rh:]))
        slc_acc.append((acc[:Rh], acc_b))

    wk = WINDOW + Th
    o_win = []
    for g in range(G):
        parts = []
        for half in range(2):
            w0 = pl.multiple_of(jnp.maximum(q0 + half * Th - WINDOW, 0), Th)
            dist = qpos[half * Th:(half + 1) * Th] - (lax.broadcasted_iota(jnp.int32, (1, wk), 1) + w0)
            wbias = jnp.where((dist >= 0) & (dist < WINDOW), 0.0, NEG)
            sc = add_per_query_half(_dot_t(q4[g][half * Rh:(half + 1) * Rh], kw_ref[0, pl.ds(w0, wk), lanes(g)]), wbias)
            e = jnp.exp2(sc - jnp.max(sc, -1, keepdims=True))
            parts.append(_normalize(_dot(e.astype(BF), vw_ref[0, pl.ds(w0, wk), lanes(g)])))
        o_win.append(parts)

    gw = NSA_HPG * LANE
    e_row = lax.broadcasted_iota(jnp.int32, (LANE, 3 * gw), 0)
    e_col = lax.broadcasted_iota(jnp.int32, (LANE, 3 * gw), 1)
    branch = jnp.where(e_col >= 2 * gw, 2, jnp.where(e_col >= gw, 1, 0))
    head = lax.shift_right_logical(e_col - branch * gw, LANE.bit_length() - 1)
    gate_col = 3 * head + branch

    def heads_on_lanes(x):
        return jnp.concatenate([x[h * Th:(h + 1) * Th] for h in range(NSA_HPG)], axis=1)

    gs = g_ref[0]
    hi = gs.astype(BF)
    lo = (gs - hi.astype(F32)).astype(BF)
    hi_lo = jnp.concatenate([hi, lo], axis=1)
    packed = NSA_HPG * NSA_DIM
    for g in range(G):
        expand = jnp.where(e_row == gate_col + g * (3 * NSA_HPG), 1.0, 0.0).astype(BF)
        gx = _dot(hi_lo, jnp.concatenate([expand, expand], axis=0))
        for half in range(2):
            rows = slice(half * Th, (half + 1) * Th)
            mixed = (gx[rows, 0:gw] * heads_on_lanes(o_cmp[g][half * Rh:(half + 1) * Rh])
                     + gx[rows, gw:2 * gw] * heads_on_lanes(_normalize(slc_acc[g][half]))
                     + gx[rows, 2 * gw:3 * gw] * heads_on_lanes(o_win[g][half]))
            o_ref[0, rows, g * packed:(g + 1) * packed] = _pack_head_pairs(
                [mixed[:, h * LANE:(h + 1) * LANE] for h in range(NSA_HPG)]).astype(o_ref.dtype)


def _nsa_attention(q, kcmp, vcmp, ks, vs, kw, vw, gates, ov, *, tq, top_k):
    bsz, seq, wide = q.shape
    n16 = kcmp.shape[1]
    gw = NSA_GROUPS * LANE
    assert seq % tq == 0 and seq >= WINDOW + tq
    kern = functools.partial(_nsa_attn_kernel, top_k=top_k)
    cblk = pl.BlockSpec((1, n16, gw), lambda b, i: (b, 0, 0))
    sblk = pl.BlockSpec((1, seq, gw), lambda b, i: (b, 0, 0))
    return pl.pallas_call(
        kern,
        grid=(bsz, seq // tq),
        in_specs=[pl.BlockSpec((1, tq, wide), lambda b, i: (b, i, 0)), cblk, cblk, sblk, sblk, sblk, sblk,
                  pl.BlockSpec((1, tq, LANE), lambda b, i: (b, i, 0)), _resident(ov.shape)],
        out_specs=pl.BlockSpec((1, tq, wide // 2), lambda b, i: (b, i, 0)),
        out_shape=jax.ShapeDtypeStruct((bsz, seq, wide // 2), BF),
        compiler_params=_params(("parallel", "arbitrary")),
        name="nsa_attn",
    )(q, kcmp, vcmp, ks, vs, kw, vw, gates, ov)


def _merge_kernel(x_ref, ab_ref, oc_ref, od_ref, wg_ref, bg_ref, wc_ref, wd_ref, wo_ref, g_ref, b_ref, o_ref):
    x = x_ref[...]
    xb = x.astype(BF)
    merged = ab_ref[...]
    for j, (o_in, w_out) in enumerate(((oc_ref, wc_ref), (od_ref, wd_ref))):
        cols = slice(j * D_MODEL, (j + 1) * D_MODEL)
        gate = jax.nn.sigmoid(_dot(xb, wg_ref[:, cols]) + bg_ref[:, cols])
        merged = merged + gate * _dot(o_in[...], w_out[...])
    mix = _dot(merged.astype(BF), wo_ref[...])
    o_ref[...] = _ln(ALPHA * x + mix, g_ref[...], b_ref[...])


def _merge_ln(x, ab, oc, od, wg, bg, wc, wd, wo, g, b, *, layer, tm):
    n, d = x.shape
    row = pl.BlockSpec((tm, d), lambda i: (i, 0))
    rows = lambda a: pl.BlockSpec((tm, a.shape[1]), lambda i: (i, 0))
    return pl.pallas_call(
        _merge_kernel,
        grid=(n // tm,),
        in_specs=[row, row, rows(oc), rows(od)] + [_layered(a, layer) for a in (wg, bg, wc, wd, wo, g, b)],
        out_specs=row,
        out_shape=jax.ShapeDtypeStruct((n, d), F32),
        compiler_params=_params(("parallel",)),
        name="merge_ln",
    )(x, ab, oc, od, wg, bg, wc, wd, wo, g, b)


def _linear_kernel(x_ref, w_ref, o_ref):
    o_ref[...] = _dot(x_ref[...].astype(BF), w_ref[...]).astype(o_ref.dtype)


def _linear(x, w, *, layer, tm, dtype):
    n, d = x.shape
    return pl.pallas_call(
        _linear_kernel,
        grid=(n // tm,),
        in_specs=[pl.BlockSpec((tm, d), lambda i: (i, 0)), _layered(w, layer)],
        out_specs=pl.BlockSpec((tm, w.shape[2]), lambda i: (i, 0)),
        out_shape=jax.ShapeDtypeStruct((n, w.shape[2]), dtype),
        compiler_params=_params(("parallel",)),
        name="mem_kv",
    )(x, w)


def _xattn_kernel(x_ref, k_ref, v_ref, wq_ref, wo_ref, g_ref, b_ref, o_ref):
    x = x_ref[...]
    q = _dot(x.astype(BF), wq_ref[...]).astype(BF)
    k = k_ref[0]
    v = v_ref[0]
    heads = []
    for h in range(XATTN_HEADS):
        sl = slice(h * XATTN_DIM, (h + 1) * XATTN_DIM)
        s = _dot_t(q[:, sl], k[:, sl]) * (XATTN_DIM ** -0.5 * LOG2_E)
        e = jnp.exp2(s - jnp.max(s, -1, keepdims=True))
        heads.append(_dot(e.astype(BF), v[:, sl]) * (1.0 / jnp.sum(e, -1, keepdims=True)))
    o = jnp.concatenate(heads, axis=1).astype(BF)
    o_ref[...] = _ln(ALPHA * x + _dot(o, wo_ref[...]), g_ref[...], b_ref[...])


def _xattn_ln(x, kv, wq, wo, g, b, *, layer, tm, seq):
    n, d = x.shape
    tps = seq // tm
    mlen = kv.shape[1]
    hd = XATTN_HEADS * XATTN_DIM
    return pl.pallas_call(
        _xattn_kernel,
        grid=(n // tm,),
        in_specs=[pl.BlockSpec((tm, d), lambda i: (i, 0)),
                  pl.BlockSpec((1, mlen, hd), lambda i: (i // tps, 0, 0)),
                  pl.BlockSpec((1, mlen, hd), lambda i: (i // tps, 0, 1))]
        + [_layered(a, layer) for a in (wq, wo, g, b)],
        out_specs=pl.BlockSpec((tm, d), lambda i: (i, 0)),
        out_shape=jax.ShapeDtypeStruct((n, d), F32),
        compiler_params=_params(("parallel",)),
        name="xattn_ln",
    )(x, kv, kv, wq, wo, g, b)


def _rope_tab(pos, dim):
    inv = ROPE_THETA ** (-(jnp.arange(0, dim, 2, dtype=F32) / dim))
    ang = pos[:, None] * inv[None, :]
    return jnp.cos(ang), jnp.sin(ang)


def _rot_cols(w, half):
    return jnp.concatenate([-w[..., half:2 * half], w[..., :half]], axis=-1)


def _layer_params(p):
    w_in, b_in = p["w_in"], p["b_in"]

    def cols(o, wd):
        return w_in[:, o:o + wd], b_in[o:o + wd]

    out = {}
    out["w_ab"] = jnp.concatenate([w_in[:, 0:2560], w_in[:, _O_GA:_O_GC]], axis=1).astype(BF)
    out["b_ab"] = jnp.concatenate([b_in[0:2560], b_in[_O_GA:_O_GC]])[None, :]
    wkr, bkr = cols(_O_KROPE, MLA_ROPE)
    padk =lambda a: jnp.pad(a, [(0, 0)] * (a.ndim - 1) + [(0, LANE - MLA_ROPE)])
    out["w_c"] = jnp.concatenate([w_in[:, _O_QLAT:_O_KROPE], padk(wkr)], axis=1).astype(BF)
    out["b_c"] = jnp.concatenate([b_in[_O_QLAT:_O_KROPE], padk(bkr)])[None, :]
    wuq = p["mla_wuq"].reshape(MLA_Q_RANK, MLA_HEADS, MLA_NOPE + MLA_ROPE)
    wq_c = jnp.pad(wuq, [(0, 0), (0, 0), (0, LANE - MLA_NOPE - MLA_ROPE)])
    out["wq_c"] = wq_c.reshape(MLA_Q_RANK, MLA_HEADS * LANE).astype(BF)
    wqr_c = jnp.pad(_rot_cols(wuq[..., MLA_NOPE:], MLA_ROPE // 2),
                    [(0, 0), (0, 0), (MLA_NOPE, LANE - MLA_NOPE - MLA_ROPE)])
    out["wqr_c"] = wqr_c.reshape(MLA_Q_RANK, MLA_HEADS * LANE).astype(BF)
    wukv = p["mla_wukv"].reshape(MLA_KV_RANK, MLA_HEADS, MLA_NOPE + MLA_V)
    out["wk_c"] = jnp.pad(wukv[..., :MLA_NOPE], [(0, 0), (0, 0), (0, LANE - MLA_NOPE)]).reshape(MLA_KV_RANK, -1).astype(BF)
    out["wv_c"] = jnp.pad(wukv[..., MLA_NOPE:], [(0, 0), (0, 0), (0, LANE - MLA_V)]).reshape(MLA_KV_RANK, -1).astype(BF)
    n_gate = NSA_HEADS * 3
    out["w_d"] = jnp.pad(w_in[:, _O_NQ:_O_NGATE + n_gate], [(0, 0), (0, LANE - n_gate)]).astype(BF)
    out["b_d"] = jnp.pad(b_in[_O_NQ:_O_NGATE + n_gate], [(0, LANE - n_gate)])[None, :]

    def cmp_weights(w):
        eye = jnp.eye(NSA_GROUPS, dtype=F32)
        wp = jnp.pad(w, [(0, 0), (0, 0), (0, LANE - NSA_DIM)])
        full = jnp.einsum("lde,gh->lgdhe", wp, eye).reshape(CMP_BLOCK, NSA_GROUPS * NSA_DIM, NSA_GROUPS * LANE)
        return full.reshape(2, CMP_STRIDE * NSA_GROUPS * NSA_DIM, NSA_GROUPS * LANE).astype(BF)

    out["wcmp_k"] = cmp_weights(p["nsa_wcmp_k"])
    out["wcmp_v"] = cmp_weights(p["nsa_wcmp_v"])

    def pe_rows(pe):
        t = jnp.broadcast_to(pe[:, None, :], (CMP_BLOCK, NSA_GROUPS, NSA_DIM))
        return t.reshape(2, CMP_STRIDE * NSA_GROUPS * NSA_DIM)

    out["pe"] = jnp.concatenate([pe_rows(p["nsa_pe_k"]), pe_rows(p["nsa_pe_v"])], axis=0)
    out["w_g"] = w_in[:, _O_GC:].astype(BF)
    out["b_g"] = b_in[_O_GC:][None, :]
    out["wout_c"] = p["mla_wout"].astype(BF)
    out["wout_d"] = p["nsa_wout"].astype(BF)
    out["gmlp_bs_t"] = p["gmlp_bs"].T
    for name in ("gmlp_wout", "conv_wout", "w_o", "xattn_wq", "xattn_wo"):
        out[name] = p[name].astype(BF)
    out["xattn_wkv"] = jnp.concatenate([p["xattn_wk"], p["xattn_wv"]], axis=1).astype(BF)
    return out


def _tables(seq):
    pos = jnp.arange(seq, dtype=F32)
    c16, s16 = _rope_tab(pos, MLA_ROPE)
    one = jnp.ones((seq, MLA_NOPE), F32)
    zero = jnp.zeros((seq, MLA_NOPE), F32)
    tail = LANE - MLA_NOPE - MLA_ROPE
    cq = jnp.concatenate([one, c16, c16, jnp.ones((seq, tail), F32)], axis=1)
    sq = jnp.concatenate([zero, s16, s16, jnp.zeros((seq, tail), F32)], axis=1)
    ck = jnp.pad(jnp.concatenate([c16, c16], axis=1), [(0, 0), (0, LANE - MLA_ROPE)])
    sk = jnp.pad(jnp.concatenate([s16, s16], axis=1), [(0, 0), (0, LANE - MLA_ROPE)])
    c32, s32 = _rope_tab(pos, NSA_DIM)
    cn = jnp.concatenate([c32, c32] * (LANE // NSA_DIM), axis=1)
    sn = jnp.concatenate([s32, s32] * (LANE // NSA_DIM), axis=1)
    n16 = seq // CMP_STRIDE
    cend = (jnp.arange(n16) * CMP_STRIDE + CMP_BLOCK - 1).astype(F32)
    cc32, cs32 = _rope_tab(cend, NSA_DIM)
    ccg = jnp.pad(jnp.concatenate([cc32, cc32], axis=1), [(0, 0), (0, LANE - NSA_DIM)])
    csg = jnp.pad(jnp.concatenate([cs32, cs32], axis=1), [(0, 0), (0, LANE - NSA_DIM)])
    cc = jnp.concatenate([ccg] * NSA_GROUPS, axis=1)
    cs = jnp.concatenate([csg] * NSA_GROUPS, axis=1)
    n_cmp = (seq - CMP_BLOCK) // CMP_STRIDE + 1
    n_slc = seq // SLC_BLOCK
    cstart = jnp.arange(n16) * CMP_STRIDE
    sstart = jnp.arange(n_slc) * SLC_BLOCK
    ovl = (jnp.minimum(cstart[None, :] + CMP_BLOCK, sstart[:, None] + SLC_BLOCK)
           - jnp.maximum(cstart[None, :], sstart[:, None]))
    ovl = jnp.clip(ovl, 0).astype(F32) / CMP_BLOCK
    ovl = jnp.where(jnp.arange(n16)[None, :] < n_cmp, ovl, 0.0).astype(BF)
    return dict(cq=cq, sq=sq, ck=ck, sk=sk, cn=cn, sn=sn, cc=cc, cs=cs, ovl=ovl)


def kernel(x, mem, ffn1_w1, ffn1_w3, ffn1_w2, ln1_g, ln1_b, w_in, b_in, gmlp_ln_g, gmlp_ln_b, gmlp_ws, gmlp_bs, gmlp_wout, conv_w, conv_wout, mla_qnorm_g, mla_kvnorm_g, mla_wuq, mla_wukv, mla_wout, nsa_pe_k, nsa_pe_v, nsa_wcmp_k, nsa_wcmp_v, nsa_wout, w_o, ln2_g, ln2_b, xattn_wq, xattn_wk, xattn_wv, xattn_wo, ln3_g, ln3_b, ffn2_w1, ffn2_w3, ffn2_w2, ln4_g, ln4_b):
    bsz, seq, d = x.shape
    mlen = mem.shape[1]
    n = bsz * seq
    assert d == D_MODEL and seq % 1024 == 0
    lp = jax.vmap(_layer_params)(dict(
        w_in=w_in, b_in=b_in, mla_wuq=mla_wuq, mla_wukv=mla_wukv, mla_wout=mla_wout, nsa_pe_k=nsa_pe_k,
        nsa_pe_v=nsa_pe_v, nsa_wcmp_k=nsa_wcmp_k, nsa_wcmp_v=nsa_wcmp_v, nsa_wout=nsa_wout, gmlp_bs=gmlp_bs,
        gmlp_wout=gmlp_wout, conv_wout=conv_wout, w_o=w_o, xattn_wq=xattn_wq, xattn_wk=xattn_wk,
        xattn_wv=xattn_wv, xattn_wo=xattn_wo))
    tb = _tables(seq)
    tm = 1024
    tm_merge = 1024
    tf = D_FF // 11
    top_k = min(SLC_TOPK, seq // SLC_BLOCK)
    row = lambda a: a[:, None, :]

    def cast_stacked(w):
        return _to_bf16(w.reshape(w.shape[0] * w.shape[1], w.shape[2])).reshape(w.shape)

    ffn1 = [cast_stacked(w) for w in (ffn1_w1, ffn1_w3, ffn1_w2)]
    ffn2 = [cast_stacked(w) for w in (ffn2_w1, ffn2_w3, ffn2_w2)]
    h = x.reshape(n, d)
    mem2 = mem.reshape(bsz * mlen, d)
    wide = MLA_HEADS * LANE
    gw = NSA_GROUPS * LANE
    for l in range(DEPTH):
        h = _ffn_ln(h, *ffn1, row(ln1_g), row(ln1_b), layer=l, tm=tm, tf=tf)
        ab = _mix_ab(h, lp["w_ab"], lp["b_ab"], row(gmlp_ln_g), row(gmlp_ln_b), gmlp_ws, lp["gmlp_bs_t"],
                     lp["gmlp_wout"], conv_w, lp["conv_wout"], layer=l, tm=tm, seq=seq)
        qc, kc_, vc_ = _mla_proj(h, lp["w_c"], lp["b_c"], row(mla_qnorm_g), row(mla_kvnorm_g),
                                 lp["wq_c"], lp["wqr_c"], lp["wk_c"], lp["wv_c"],
                                 tb["cq"], tb["sq"], tb["ck"], tb["sk"], layer=l, tm=tm, seq=seq)
        oc = _flash_causal(qc.reshape(bsz, seq, wide), kc_.reshape(bsz, seq, wide), vc_.reshape(bsz, seq, wide), tq=1024, hp=4)
        qn, nkc, nvc, nks, nvs, nkw, nvw, gates = _nsa_proj(h, lp["w_d"], lp["b_d"], tb["cn"], tb["sn"],
                                                            layer=l, tm=tm, seq=seq)
        kcmp, vcmp = _nsa_compress(nkc.reshape(bsz, seq, NSA_GROUPS * NSA_DIM), nvc.reshape(bsz, seq, NSA_GROUPS * NSA_DIM),
                                   lp["pe"], lp["wcmp_k"], lp["wcmp_v"], tb["cc"], tb["cs"], layer=l)
        od = _nsa_attention(qn.reshape(bsz, seq, NSA_HEADS * LANE), kcmp, vcmp,
                            nks.reshape(bsz, seq, gw), nvs.reshape(bsz, seq, gw),
                            nkw.reshape(bsz, seq, gw), nvw.reshape(bsz, seq, gw),
                            gates.reshape(bsz, seq, LANE), tb["ovl"], tq=512, top_k=top_k)
        h = _merge_ln(h, ab, oc.reshape(n, MLA_HEADS * MLA_V), od.reshape(n, NSA_HEADS * NSA_DIM), lp["w_g"], lp["b_g"],
                      lp["wout_c"], lp["wout_d"], lp["w_o"], row(ln2_g), row(ln2_b), layer=l, tm=tm_merge)
        kv = _linear(mem2, lp["xattn_wkv"], layer=l, tm=min(256, bsz * mlen), dtype=BF)
        h = _xattn_ln(h, kv.reshape(bsz, mlen, 2 * XATTN_HEADS * XATTN_DIM), lp["xattn_wq"], lp["xattn_wo"],
                      row(ln3_g), row(ln3_b), layer=l, tm=tm, seq=seq)
        h = _ffn_ln(h, *ffn2, row(ln4_g), row(ln4_b), layer=l, tm=tm, tf=tf)
    return h.reshape(bsz, seq, d)
```
